```python
import jax, jax.numpy as jnp
from jax import lax
import numpy as np

D_MODEL = 1024
BATCH = 8
SEQ = 8192
DEPTH = 2

MIX_WIDTH = D_MODEL
ATTN_WIDTH = MIX_WIDTH // 2
POOL_WIDTH = MIX_WIDTH - ATTN_WIDTH
HEAD_DIM = 64
N_HEADS = ATTN_WIDTH // HEAD_DIM
N_KV_HEADS = 2
GROUP = N_HEADS // N_KV_HEADS
KV_WIDTH = N_KV_HEADS * HEAD_DIM
WINDOW = 128
BLOCK = 128
ROT_DIM = HEAD_DIM // 4
ROPE_THETA = 500000.0
POOL_WINDOWS = (2, 4, 8, 16)
N_POOL_GROUPS = len(POOL_WINDOWS)
POOL_GROUP_WIDTH = POOL_WIDTH // N_POOL_GROUPS
IN_WIDTH = ATTN_WIDTH + 2 * KV_WIDTH + POOL_WIDTH
D_FF = ((int(np.ceil(8 * D_MODEL / 3)) + 255) // 256) * 256
N_MOD = 6
EPS = 1e-6
NEG_INF = -1e30

kernel_name = "hybrid_swa_sink_pool_swiglu_block"


def rms_norm(x, g):
    xf = x.astype(jnp.float32)
    y = xf * lax.rsqrt(jnp.mean(xf * xf, axis=-1, keepdims=True) + EPS)
    return (y * g.astype(jnp.float32)).astype(x.dtype)


def partial_rotary(t, positions):
    inv_freq = ROPE_THETA ** (-jnp.arange(0, ROT_DIM, 2, dtype=jnp.float32) / ROT_DIM)
    ang = positions.astype(jnp.float32)[:, :, None] * inv_freq
    cos = jnp.cos(ang)[:, :, None, :]
    sin = jnp.sin(ang)[:, :, None, :]
    tf = t.astype(jnp.float32)
    half = ROT_DIM // 2
    t1, t2, rest = tf[..., :half], tf[..., half:ROT_DIM], tf[..., ROT_DIM:]
    rot = jnp.concatenate([t1 * cos - t2 * sin, t2 * cos + t1 * sin, rest], axis=-1)
    return rot.astype(t.dtype)


def sliding_window_attention_with_sinks(q, k, v, sinks):
    B, S = q.shape[0], q.shape[1]
    nb = S // BLOCK
    qb = q.reshape(B, nb, BLOCK, N_KV_HEADS, GROUP, HEAD_DIM)
    kb = k.reshape(B, nb, BLOCK, N_KV_HEADS, HEAD_DIM)
    vb = v.reshape(B, nb, BLOCK, N_KV_HEADS, HEAD_DIM)
    pad = ((0, 0), (1, 0), (0, 0), (0, 0), (0, 0))
    k_cat = jnp.concatenate([jnp.pad(kb, pad)[:, :-1], kb], axis=2)
    v_cat = jnp.concatenate([jnp.pad(vb, pad)[:, :-1], vb], axis=2)
    scores = jnp.einsum("bnqkgd,bnskd->bnkgqs", qb, k_cat).astype(jnp.float32)
    scores = scores * (HEAD_DIM ** -0.5)
    qi = jnp.arange(BLOCK)[:, None]
    kj = jnp.arange(2 * BLOCK)[None, :]
    diff = qi + BLOCK - kj
    blk = jnp.arange(nb)[:, None, None]
    key_abs = blk * BLOCK + kj[None] - BLOCK
    valid = (diff[None] >= 0) & (diff[None] < WINDOW) & (key_abs >= 0)
    scores = jnp.where(valid[None, :, None, None], scores, NEG_INF)
    sink = jnp.broadcast_to(
        sinks.astype(jnp.float32).reshape(1, 1, N_KV_HEADS, GROUP, 1, 1),
        scores.shape[:-1] + (1,))
    probs = jax.nn.softmax(jnp.concatenate([scores, sink], axis=-1), axis=-1)[..., :-1]
    out = jnp.einsum("bnkgqs,bnskd->bnqkgd", probs.astype(v.dtype), v_cat)
    return out.reshape(B, S, N_HEADS * HEAD_DIM)


def causal_pool_mixer(u, pool_w, pool_scale):
    S = u.shape[1]
    t = jnp.arange(S, dtype=jnp.float32)[None, :, None]
    outs = []
    for gi, w in enumerate(POOL_WINDOWS):
        ug = u[..., gi * POOL_GROUP_WIDTH:(gi + 1) * POOL_GROUP_WIDTH].astype(jnp.float32)
        cs = jnp.pad(jnp.cumsum(ug, axis=1), ((0, 0), (1, 0), (0, 0)))
        upper = cs[:, 1:]
        lower = jnp.pad(cs, ((0, 0), (w - 1, 0), (0, 0)))[:, :S]
        count = jnp.minimum(t + 1.0, float(w))
        pooled = (upper - lower) / count - ug
        outs.append(jnp.einsum("bsc,cd->bsd", pooled.astype(u.dtype), pool_w[gi]))
    return jnp.concatenate(outs, axis=-1) * pool_scale


def _fwd_setup_inputs(seed: int = 0) -> dict:
    key = jax.random.key(seed)
    ks = jax.random.split(key, 20)
    f32 = jnp.float32
    def nrm(k, shape, scale):
        return jax.random.normal(k, shape, f32) * scale
    x = jax.random.normal(ks[0], (BATCH, SEQ, D_MODEL), f32)
    c = jax.random.normal(ks[1], (BATCH, D_MODEL), f32)
    offsets = jax.random.randint(ks[2], (BATCH, 1), 0, 4096, dtype=jnp.int32)
    positions = (offsets + jnp.arange(SEQ, dtype=jnp.int32)[None, :]).astype(jnp.int32)
    return {
        "x": x,
        "c": c,
        "positions": positions,
        "ada_w": nrm(ks[3], (DEPTH, D_MODEL, N_MOD * D_MODEL), D_MODEL ** -0.5),
        "ada_b": nrm(ks[4], (DEPTH, N_MOD * D_MODEL), 0.02),
        "w_in": nrm(ks[5], (DEPTH, D_MODEL, IN_WIDTH), D_MODEL ** -0.5),
        "b_in": nrm(ks[6], (DEPTH, IN_WIDTH), 0.02),
        "sinks": nrm(ks[7], (DEPTH, N_HEADS), 1.0),
        "pool_w": nrm(ks[8], (DEPTH, N_POOL_GROUPS, POOL_GROUP_WIDTH, POOL_GROUP_WIDTH), POOL_GROUP_WIDTH ** -0.5),
        "pool_scale": 1.0 + nrm(ks[9], (DEPTH, POOL_WIDTH), 0.1),
        "w_out": nrm(ks[10], (DEPTH, MIX_WIDTH, D_MODEL), MIX_WIDTH ** -0.5),
        "w_gate": nrm(ks[11], (DEPTH, D_MODEL, D_FF), D_MODEL ** -0.5),
        "w_up": nrm(ks[12], (DEPTH, D_MODEL, D_FF), D_MODEL ** -0.5),
        "w_down": nrm(ks[13], (DEPTH, D_FF, D_MODEL), D_FF ** -0.5),
        "g_pre_mix": 1.0 + nrm(ks[14], (DEPTH, D_MODEL), 0.02),
        "g_post_mix": 1.0 + nrm(ks[15], (DEPTH, D_MODEL), 0.02),
        "g_pre_ffn": 1.0 + nrm(ks[16], (DEPTH, D_MODEL), 0.02),
        "g_post_ffn": 1.0 + nrm(ks[17], (DEPTH, D_MODEL), 0.02),
    }


def _fwd_reference(x, c, positions, ada_w, ada_b, w_in, b_in, sinks, pool_w, pool_scale,
              w_out, w_gate, w_up, w_down, g_pre_mix, g_post_mix, g_pre_ffn, g_post_ffn):
    B, S = x.shape[0], x.shape[1]
    c_act = jax.nn.silu(c)
    for l in range(DEPTH):
        mod = c_act @ ada_w[l] + ada_b[l]
        shift_m, scale_m, gate_m, shift_f, scale_f, gate_f = [
            m[:, None, :] for m in jnp.split(mod, N_MOD, axis=-1)]

        h = rms_norm(x, g_pre_mix[l]) * (1.0 + scale_m) + shift_m
        proj = h @ w_in[l] + b_in[l]
        q, k, v, u = jnp.split(
            proj, [ATTN_WIDTH, ATTN_WIDTH + KV_WIDTH, ATTN_WIDTH + 2 * KV_WIDTH], axis=-1)
        q = partial_rotary(q.reshape(B, S, N_HEADS, HEAD_DIM), positions)
        k = partial_rotary(k.reshape(B, S, N_KV_HEADS, HEAD_DIM), positions)
        v = v.reshape(B, S, N_KV_HEADS, HEAD_DIM)
        attn_out = sliding_window_attention_with_sinks(q, k, v, sinks[l])
        pool_out = causal_pool_mixer(u, pool_w[l], pool_scale[l])
        mix = jnp.concatenate([attn_out, pool_out], axis=-1) @ w_out[l]
        x = x + gate_m * rms_norm(mix, g_post_mix[l])

        h = rms_norm(x, g_pre_ffn[l]) * (1.0 + scale_f) + shift_f
        f = (jax.nn.silu(h @ w_gate[l]) * (h @ w_up[l])) @ w_down[l]
        x = x + gate_f * rms_norm(f, g_post_ffn[l])
    return x


import jax as _jax
import jax.numpy as _jnp

TWIN_FORMAT = 'train_step'
FWD_PARAMS = ['x', 'c', 'positions', 'ada_w', 'ada_b', 'w_in', 'b_in', 'sinks', 'pool_w', 'pool_scale', 'w_out', 'w_gate', 'w_up', 'w_down', 'g_pre_mix', 'g_post_mix', 'g_pre_ffn', 'g_post_ffn']
TWIN_WEIGHTS = ['ada_w', 'ada_b', 'w_in', 'b_in', 'sinks', 'pool_w', 'pool_scale', 'w_out', 'w_gate', 'w_up', 'w_down', 'g_pre_mix', 'g_post_mix', 'g_pre_ffn', 'g_post_ffn']
TWIN_DIFF_INPUT = 'x'
TWIN_INPUTS = ['x', 'c', 'positions', 'ada_w', 'ada_b', 'w_in', 'b_in', 'sinks', 'pool_w', 'pool_scale', 'w_out', 'w_gate', 'w_up', 'w_down', 'g_pre_mix', 'g_post_mix', 'g_pre_ffn', 'g_post_ffn', 'loss_target', 'm_ada_w', 'm_ada_b', 'm_w_in', 'm_b_in', 'm_sinks', 'm_pool_w', 'm_pool_scale', 'm_w_out', 'm_w_gate', 'm_w_up', 'm_w_down', 'm_g_pre_mix', 'm_g_post_mix', 'm_g_pre_ffn', 'm_g_post_ffn', 'v_ada_w', 'v_ada_b', 'v_w_in', 'v_b_in', 'v_sinks', 'v_pool_w', 'v_pool_scale', 'v_w_out', 'v_w_gate', 'v_w_up', 'v_w_down', 'v_g_pre_mix', 'v_g_post_mix', 'v_g_pre_ffn', 'v_g_post_ffn']
TWIN_OUTPUTS = ['loss', 'grad_x', 'grad_ada_w', 'grad_ada_b', 'grad_w_in', 'grad_b_in', 'grad_sinks', 'grad_pool_w', 'grad_pool_scale', 'grad_w_out', 'grad_w_gate', 'grad_w_up', 'grad_w_down', 'grad_g_pre_mix', 'grad_g_post_mix', 'grad_g_pre_ffn', 'grad_g_post_ffn', 'delta_ada_w', 'delta_ada_b', 'delta_w_in', 'delta_b_in', 'delta_sinks', 'delta_pool_w', 'delta_pool_scale', 'delta_w_out', 'delta_w_gate', 'delta_w_up', 'delta_w_down', 'delta_g_pre_mix', 'delta_g_post_mix', 'delta_g_pre_ffn', 'delta_g_post_ffn', 'new_m_ada_w', 'new_m_ada_b', 'new_m_w_in', 'new_m_b_in', 'new_m_sinks', 'new_m_pool_w', 'new_m_pool_scale', 'new_m_w_out', 'new_m_w_gate', 'new_m_w_up', 'new_m_w_down', 'new_m_g_pre_mix', 'new_m_g_post_mix', 'new_m_g_pre_ffn', 'new_m_g_post_ffn', 'new_v_ada_w', 'new_v_ada_b', 'new_v_w_in', 'new_v_b_in', 'new_v_sinks', 'new_v_pool_w', 'new_v_pool_scale', 'new_v_w_out', 'new_v_w_gate', 'new_v_w_up', 'new_v_w_down', 'new_v_g_pre_mix', 'new_v_g_post_mix', 'new_v_g_pre_ffn', 'new_v_g_post_ffn']
TWIN_LEAF_KINDS = {'loss': 'loss', 'grad_x': 'grad_x', 'grad_ada_w': 'grad_w', 'grad_ada_b': 'grad_w', 'grad_w_in': 'grad_w', 'grad_b_in': 'grad_w', 'grad_sinks': 'grad_w', 'grad_pool_w': 'grad_w', 'grad_pool_scale': 'grad_w', 'grad_w_out': 'grad_w', 'grad_w_gate': 'grad_w', 'grad_w_up': 'grad_w', 'grad_w_down': 'grad_w', 'grad_g_pre_mix': 'grad_w', 'grad_g_post_mix': 'grad_w', 'grad_g_pre_ffn': 'grad_w', 'grad_g_post_ffn': 'grad_w', 'delta_ada_w': 'delta_w', 'delta_ada_b': 'delta_w', 'delta_w_in': 'delta_w', 'delta_b_in': 'delta_w', 'delta_sinks': 'delta_w', 'delta_pool_w': 'delta_w', 'delta_pool_scale': 'delta_w', 'delta_w_out': 'delta_w', 'delta_w_gate': 'delta_w', 'delta_w_up': 'delta_w', 'delta_w_down': 'delta_w', 'delta_g_pre_mix': 'delta_w', 'delta_g_post_mix': 'delta_w', 'delta_g_pre_ffn': 'delta_w', 'delta_g_post_ffn': 'delta_w', 'new_m_ada_w': 'new_m', 'new_m_ada_b': 'new_m', 'new_m_w_in': 'new_m', 'new_m_b_in': 'new_m', 'new_m_sinks': 'new_m', 'new_m_pool_w': 'new_m', 'new_m_pool_scale': 'new_m', 'new_m_w_out': 'new_m', 'new_m_w_gate': 'new_m', 'new_m_w_up': 'new_m', 'new_m_w_down': 'new_m', 'new_m_g_pre_mix': 'new_m', 'new_m_g_post_mix': 'new_m', 'new_m_g_pre_ffn': 'new_m', 'new_m_g_post_ffn': 'new_m', 'new_v_ada_w': 'new_v', 'new_v_ada_b': 'new_v', 'new_v_w_in': 'new_v', 'new_v_b_in': 'new_v', 'new_v_sinks': 'new_v', 'new_v_pool_w': 'new_v', 'new_v_pool_scale': 'new_v', 'new_v_w_out': 'new_v', 'new_v_w_gate': 'new_v', 'new_v_w_up': 'new_v', 'new_v_w_down': 'new_v', 'new_v_g_pre_mix': 'new_v', 'new_v_g_post_mix': 'new_v', 'new_v_g_pre_ffn': 'new_v', 'new_v_g_post_ffn': 'new_v'}


def _forward(args):
    return _fwd_reference(*[args[k] for k in FWD_PARAMS])


def _output_shape():
    out = _jax.eval_shape(lambda: _forward(_fwd_setup_inputs(0)))
    return out.shape, out.dtype

N_MICROBATCH = 1
ADAM_LR = 0.001
ADAM_B1 = 0.9
ADAM_B2 = 0.999
ADAM_EPS = 1e-08
ADAM_WD = 0.01
ADAM_STEP = 10
PER_EXAMPLE_BATCH_AXIS = {'x': 0, 'c': 0, 'positions': 0, 'loss_target': 0}
SHARED_INPUTS = []
_WEIGHT_DTYPES = {'ada_w': _jnp.float32, 'ada_b': _jnp.float32, 'w_in': _jnp.float32, 'b_in': _jnp.float32, 'sinks': _jnp.float32, 'pool_w': _jnp.float32, 'pool_scale': _jnp.float32, 'w_out': _jnp.float32, 'w_gate': _jnp.float32, 'w_up': _jnp.float32, 'w_down': _jnp.float32, 'g_pre_mix': _jnp.float32, 'g_post_mix': _jnp.float32, 'g_pre_ffn': _jnp.float32, 'g_post_ffn': _jnp.float32}
MOMENT_SCALE = {'ada_w': 8.364660e+00, 'ada_b': 1.486669e+01, 'w_in': 5.553934e+00, 'b_in': 9.081950e+00, 'sinks': 2.161226e-01, 'pool_w': 9.627262e-01, 'pool_scale': 1.359620e+00, 'w_out': 6.643575e+00, 'w_gate': 1.384818e+00, 'w_up': 1.975086e+00, 'w_down': 3.263971e+00, 'g_pre_mix': 1.795214e+00, 'g_post_mix': 3.166700e+01, 'g_pre_ffn': 1.738780e+00, 'g_post_ffn': 3.010001e+01}


def _to_microbatches(a, axis):
    t = _jnp.moveaxis(a, axis, 0)
    t = t.reshape((N_MICROBATCH, t.shape[0] // N_MICROBATCH) + t.shape[1:])
    return _jnp.moveaxis(t, 1, axis + 1)


def setup_inputs(seed: int = 0) -> dict:
    inp = _fwd_setup_inputs(seed)
    key = _jax.random.fold_in(_jax.random.key(seed), 7919)
    shape, _ = _output_shape()
    out = dict(inp)
    out["loss_target"] = _jax.random.normal(_jax.random.fold_in(key, 0), shape, _jnp.float32)
    for i, name in enumerate(TWIN_WEIGHTS):
        w = inp[name].astype(_jnp.float32)
        if MOMENT_SCALE is None:
            s = _jnp.sqrt(_jnp.mean(_jnp.square(w)) + 1e-30)
        else:
            s = MOMENT_SCALE[name]
        km, kv = _jax.random.split(_jax.random.fold_in(key, i + 1))
        out[name] = w
        out["m_" + name] = s * _jax.random.normal(km, w.shape, _jnp.float32)
        out["v_" + name] = (s * s) * _jax.random.uniform(kv, w.shape, _jnp.float32, 0.5, 1.5)
    if N_MICROBATCH > 1:
        for name, axis in PER_EXAMPLE_BATCH_AXIS.items():
            out[name] = _to_microbatches(out[name], axis)
    return {'x': out['x'], 'c': out['c'], 'positions': out['positions'], 'ada_w': out['ada_w'], 'ada_b': out['ada_b'], 'w_in': out['w_in'], 'b_in': out['b_in'], 'sinks': out['sinks'], 'pool_w': out['pool_w'], 'pool_scale': out['pool_scale'], 'w_out': out['w_out'], 'w_gate': out['w_gate'], 'w_up': out['w_up'], 'w_down': out['w_down'], 'g_pre_mix': out['g_pre_mix'], 'g_post_mix': out['g_post_mix'], 'g_pre_ffn': out['g_pre_ffn'], 'g_post_ffn': out['g_post_ffn'], 'loss_target': out['loss_target'], 'm_ada_w': out['m_ada_w'], 'm_ada_b': out['m_ada_b'], 'm_w_in': out['m_w_in'], 'm_b_in': out['m_b_in'], 'm_sinks': out['m_sinks'], 'm_pool_w': out['m_pool_w'], 'm_pool_scale': out['m_pool_scale'], 'm_w_out': out['m_w_out'], 'm_w_gate': out['m_w_gate'], 'm_w_up': out['m_w_up'], 'm_w_down': out['m_w_down'], 'm_g_pre_mix': out['m_g_pre_mix'], 'm_g_post_mix': out['m_g_post_mix'], 'm_g_pre_ffn': out['m_g_pre_ffn'], 'm_g_post_ffn': out['m_g_post_ffn'], 'v_ada_w': out['v_ada_w'], 'v_ada_b': out['v_ada_b'], 'v_w_in': out['v_w_in'], 'v_b_in': out['v_b_in'], 'v_sinks': out['v_sinks'], 'v_pool_w': out['v_pool_w'], 'v_pool_scale': out['v_pool_scale'], 'v_w_out': out['v_w_out'], 'v_w_gate': out['v_w_gate'], 'v_w_up': out['v_w_up'], 'v_w_down': out['v_w_down'], 'v_g_pre_mix': out['v_g_pre_mix'], 'v_g_post_mix': out['v_g_post_mix'], 'v_g_pre_ffn': out['v_g_pre_ffn'], 'v_g_post_ffn': out['v_g_post_ffn']}


def _loss(weights, diff, rest, loss_target):
    with _jax.named_scope("forward"):
        args = {**rest, TWIN_DIFF_INPUT: diff, **{k: w.astype(_WEIGHT_DTYPES[k]) for k, w in weights.items()}}
        y = _forward(args)
    with _jax.named_scope("loss_head"):
        err = _jnp.square(y.astype(_jnp.float32) - loss_target)
        return 0.5 * _jnp.sum(_jnp.mean(err, axis=-1)) if err.ndim else 0.5 * err


def _adamw(w, g, m, v):
    m = ADAM_B1 * m + (1.0 - ADAM_B1) * g
    v = ADAM_B2 * v + (1.0 - ADAM_B2) * _jnp.square(g)
    m_hat = m / (1.0 - ADAM_B1 ** ADAM_STEP)
    v_hat = v / (1.0 - ADAM_B2 ** ADAM_STEP)
    delta = -ADAM_LR * (m_hat / (_jnp.sqrt(v_hat) + ADAM_EPS) + ADAM_WD * w)
    return delta, m, v


def reference(x, c, positions, ada_w, ada_b, w_in, b_in, sinks, pool_w, pool_scale, w_out, w_gate, w_up, w_down, g_pre_mix, g_post_mix, g_pre_ffn, g_post_ffn, loss_target, m_ada_w, m_ada_b, m_w_in, m_b_in, m_sinks, m_pool_w, m_pool_scale, m_w_out, m_w_gate, m_w_up, m_w_down, m_g_pre_mix, m_g_post_mix, m_g_pre_ffn, m_g_post_ffn, v_ada_w, v_ada_b, v_w_in, v_b_in, v_sinks, v_pool_w, v_pool_scale, v_w_out, v_w_gate, v_w_up, v_w_down, v_g_pre_mix, v_g_post_mix, v_g_pre_ffn, v_g_post_ffn):
    given = dict(x=x, c=c, positions=positions, ada_w=ada_w, ada_b=ada_b, w_in=w_in, b_in=b_in, sinks=sinks, pool_w=pool_w, pool_scale=pool_scale, w_out=w_out, w_gate=w_gate, w_up=w_up, w_down=w_down, g_pre_mix=g_pre_mix, g_post_mix=g_post_mix, g_pre_ffn=g_pre_ffn, g_post_ffn=g_post_ffn, loss_target=loss_target, m_ada_w=m_ada_w, m_ada_b=m_ada_b, m_w_in=m_w_in, m_b_in=m_b_in, m_sinks=m_sinks, m_pool_w=m_pool_w, m_pool_scale=m_pool_scale, m_w_out=m_w_out, m_w_gate=m_w_gate, m_w_up=m_w_up, m_w_down=m_w_down, m_g_pre_mix=m_g_pre_mix, m_g_post_mix=m_g_post_mix, m_g_pre_ffn=m_g_pre_ffn, m_g_post_ffn=m_g_post_ffn, v_ada_w=v_ada_w, v_ada_b=v_ada_b, v_w_in=v_w_in, v_b_in=v_b_in, v_sinks=v_sinks, v_pool_w=v_pool_w, v_pool_scale=v_pool_scale, v_w_out=v_w_out, v_w_gate=v_w_gate, v_w_up=v_w_up, v_w_down=v_w_down, v_g_pre_mix=v_g_pre_mix, v_g_post_mix=v_g_post_mix, v_g_pre_ffn=v_g_pre_ffn, v_g_post_ffn=v_g_post_ffn)
    weights = {n: given[n] for n in TWIN_WEIGHTS}
    shared = {n: given[n] for n in SHARED_INPUTS}
    per_example = {n: given[n] for n in ['x', 'c', 'positions']}
    grad_fn = _jax.value_and_grad(_loss, argnums=(0, 1))

    def one_microbatch(ex, loss_target):
        ex = dict(ex)
        diff = ex.pop(TWIN_DIFF_INPUT)
        return grad_fn(weights, diff, {**shared, **ex}, loss_target)

    if N_MICROBATCH == 1:
        loss, (grad_w, grad_x) = one_microbatch(per_example, given["loss_target"])
    else:
        def body(carry, xs):
            loss_sum, grad_sum = carry
            l_k, (gw_k, gx_k) = one_microbatch(xs[0], xs[1])
            with _jax.named_scope("update"):
                return (loss_sum + l_k, _jax.tree.map(_jnp.add, grad_sum, gw_k)), gx_k

        init = (_jnp.zeros((), _jnp.float32), _jax.tree.map(_jnp.zeros_like, weights))
        (loss, grad_w), grad_x = _jax.lax.scan(body, init, (per_example, given["loss_target"]))
    with _jax.named_scope("update"):
        delta_w, new_m, new_v = {}, {}, {}
        for n in TWIN_WEIGHTS:
            delta_w[n], new_m[n], new_v[n] = _adamw(weights[n], grad_w[n], given["m_" + n], given["v_" + n])
    return (loss, grad_x, *[grad_w[n] for n in TWIN_WEIGHTS], *[delta_w[n] for n in TWIN_WEIGHTS],
            *[new_m[n] for n in TWIN_WEIGHTS], *[new_v[n] for n in TWIN_WEIGHTS])
```

```python
import functools

import jax
import jax.numpy as jnp
from jax import lax
from jax.experimental import pallas as pl
from jax.experimental.pallas import tpu as pltpu

F32 = jnp.float32
BF16 = jnp.bfloat16
MESH = pl.DeviceIdType.MESH

D_MODEL = 1024
ATTN_W = 512
KV_W = 128
POOL_W = 512
IN_W = 1280
D_FF = 2816
N_SHARD = 4
FF_SH = D_FF // N_SHARD
IN_SH = IN_W // N_SHARD
OUT_SH = D_MODEL // N_SHARD
ADA_SH = 6 * D_MODEL // N_SHARD
HEAD = 64
N_HEADS = 8
GROUP = 4
BLK = 128
POOL_WINDOWS = (2, 4, 8, 16)
HALO = 16
ROT = 16
ROPE_THETA = 500000.0
EPS = 1e-6
NEG_INF = -1e30
N_DEV = 8

ADAM_LR = 0.001
ADAM_B1 = 0.9
ADAM_B2 = 0.999
ADAM_EPS = 1e-08
ADAM_WD = 0.01
ADAM_STEP = 10

VMEM_LIMIT = 48 * 1024 * 1024


def _cp(*sem):
    return pltpu.CompilerParams(dimension_semantics=sem, vmem_limit_bytes=VMEM_LIMIT)


def _full(shape):
    nd = len(shape)
    return pl.BlockSpec(shape, lambda *_: (0,) * nd)


def _resident(shape):
    nd = len(shape)
    return pl.BlockSpec(shape, lambda *_: (0,) * nd, pipeline_mode=pl.Buffered(1))


def _rows(tm, ncol):
    return pl.BlockSpec((tm, ncol), lambda i: (i, 0))


def _sds(shape, dtype):
    return jax.ShapeDtypeStruct(shape, dtype)


def _nt(a, b):
    return lax.dot_general(a, b, (((1,), (1,)), ((), ())), preferred_element_type=F32)


def _tn(a, b):
    return lax.dot_general(a, b, (((0,), (0,)), ((), ())), preferred_element_type=F32)


def _mm(a, b):
    return jnp.dot(a, b, preferred_element_type=F32)


def _rstd(x):
    return lax.rsqrt(jnp.mean(x * x, axis=-1, keepdims=True) + EPS)


def _colsum(x):
    return jnp.sum(x, axis=0, keepdims=True)


def _norm_bwd(dhat, xhat, rstd):
    return rstd * (dhat - xhat * jnp.mean(dhat * xhat, axis=-1, keepdims=True))


def _rope_tables(pos_b, lane_tab):
    T = pos_b.shape[0]
    tm = min(T, 1024)

    def body(pos_ref, tab_ref, c_ref, s1_ref, s2_ref):
        ang = pos_ref[...].astype(F32) * tab_ref[0:1, :]
        cs = jnp.cos(ang)
        sn = jnp.sin(ang)
        m_rot = tab_ref[1:2, :]
        c_ref[...] = cs * m_rot + (1.0 - m_rot)
        s1_ref[...] = -sn * tab_ref[2:3, :]
        s2_ref[...] = sn * tab_ref[3:4, :]

    out = _sds((T, 128), F32)
    return pl.pallas_call(
        body, name="rope_tables", grid=(T // tm,),
        in_specs=[_rows(tm, 128), _full((8, 128))],
        out_specs=[_rows(tm, 128)] * 3, out_shape=[out] * 3,
        compiler_params=_cp("parallel"),
    )(pos_b, lane_tab)


def _rot_fwd(t, c, s1, s2):
    w = t.shape[-1]
    return t * c + pltpu.roll(t, w - 8, 1) * s1 + pltpu.roll(t, 8, 1) * s2


def _rot_bwd(d, c, s1, s2):
    w = d.shape[-1]
    return d * c + pltpu.roll(d * s1, 8, 1) + pltpu.roll(d * s2, w - 8, 1)


def _fwd_in(x, mod8, g8, w_in, b_in, rc, rs1, rs2):
    T = x.shape[0]
    tm = min(T, 512)

    def body(x_ref, mod_ref, g_ref, w_ref, b_ref, c_ref, s1_ref, s2_ref,
             h_ref, q_ref, k_ref, v_ref, u_ref):
        xf = x_ref[...]
        h = (xf * _rstd(xf) * g_ref[0:1, :]) * (1.0 + mod_ref[1:2, :]) + mod_ref[0:1, :]
        hb = h.astype(BF16)
        h_ref[...] = hb
        c = c_ref[...]
        s1 = s1_ref[...]
        s2 = s2_ref[...]
        q = _mm(hb, w_ref[:, 0:ATTN_W]) + b_ref[:, 0:ATTN_W]
        q = _rot_fwd(q, jnp.tile(c, (1, 4)), jnp.tile(s1, (1, 4)), jnp.tile(s2, (1, 4)))
        q_ref[...] = (q * (HEAD ** -0.5)).astype(BF16)
        k = _mm(hb, w_ref[:, ATTN_W:ATTN_W + KV_W]) + b_ref[:, ATTN_W:ATTN_W + KV_W]
        k_ref[...] = _rot_fwd(k, c, s1, s2).astype(BF16)
        v = _mm(hb, w_ref[:, ATTN_W + KV_W:ATTN_W + 2 * KV_W]) + b_ref[:, ATTN_W + KV_W:ATTN_W + 2 * KV_W]
        v_ref[...] = v.astype(BF16)
        u_ref[...] = _mm(hb, w_ref[:, ATTN_W + 2 * KV_W:IN_W]) + b_ref[:, ATTN_W + 2 * KV_W:IN_W]

    return pl.pallas_call(
        body, name="fwd_in", grid=(T // tm,),
        in_specs=[_rows(tm, D_MODEL), _full((8, D_MODEL)), _full((8, D_MODEL)),
                  _resident((D_MODEL, IN_W)), _full((1, IN_W)),
                  _rows(tm, 128), _rows(tm, 128), _rows(tm, 128)],
        out_specs=[_rows(tm, D_MODEL), _rows(tm, ATTN_W), _rows(tm, KV_W), _rows(tm, KV_W), _rows(tm, POOL_W)],
        out_shape=[_sds((T, D_MODEL), BF16), _sds((T, ATTN_W), BF16), _sds((T, KV_W), BF16),
                   _sds((T, KV_W), BF16), _sds((T, POOL_W), F32)],
        compiler_params=_cp("parallel"),
    )(x, mod8, g8, w_in, b_in, rc, rs1, rs2)


def _band_mask(n):
    row = lax.broadcasted_iota(jnp.int32, (BLK, 2 * BLK), 0)
    col = lax.broadcasted_iota(jnp.int32, (BLK, 2 * BLK), 1)
    first = jnp.where(n > 0, 0, 2 * BLK)
    in_prev = jnp.logical_and(col < BLK, col > row + first)
    in_cur = jnp.logical_and(col >= BLK, (col - BLK) <= row)
    return jnp.logical_or(in_prev, in_cur)


def _attn_fwd(q, k, v, sink_b):
    T = q.shape[0]
    nb = T // BLK

    def body(q_ref, kp_ref, kc_ref, vp_ref, vc_ref, sk_ref, o_ref, lse_ref):
        n = pl.program_id(0)
        kcat = jnp.concatenate([kp_ref[...], kc_ref[...]], axis=0)
        vcat = jnp.concatenate([vp_ref[...], vc_ref[...]], axis=0)
        valid = _band_mask(n)
        lane = lax.broadcasted_iota(jnp.int32, (BLK, 128), 1)
        lse_all = jnp.zeros((BLK, 128), F32)
        outs = []
        for h in range(N_HEADS):
            j = h // GROUP
            qh = q_ref[:, h * HEAD:(h + 1) * HEAD]
            s = _nt(qh, kcat[:, j * HEAD:(j + 1) * HEAD])
            s = jnp.where(valid, s, NEG_INF)
            sk = sk_ref[h:h + 1, 0:1]
            m = jnp.maximum(jnp.max(s, axis=-1, keepdims=True), sk)
            p = jnp.exp(s - m)
            den = jnp.sum(p, axis=-1, keepdims=True) + jnp.exp(sk - m)
            p = p / den
            outs.append(_mm(p.astype(BF16), vcat[:, j * HEAD:(j + 1) * HEAD]))
            lse_all = jnp.where(lane == h, m + jnp.log(den), lse_all)
        o_ref[...] = jnp.concatenate(outs, axis=1).astype(BF16)
        lse_ref[...] = lse_all

    prev = lambda n: (jnp.maximum(n - 1, 0), 0)
    cur = lambda n: (n, 0)
    return pl.pallas_call(
        body, name="attn_fwd", grid=(nb,),
        in_specs=[pl.BlockSpec((BLK, ATTN_W), cur),
                  pl.BlockSpec((BLK, KV_W), prev), pl.BlockSpec((BLK, KV_W), cur),
                  pl.BlockSpec((BLK, KV_W), prev), pl.BlockSpec((BLK, KV_W), cur),
                  _full((8, 128))],
        out_specs=[pl.BlockSpec((BLK, ATTN_W), cur), pl.BlockSpec((BLK, 128), cur)],
        out_shape=[_sds((T, ATTN_W), BF16), _sds((T, 128), F32)],
        compiler_params=_cp("parallel"),
    )(q, k, k, v, v, sink_b)


def _pool_fwd(u, pool_w, pool_scale):
    T = u.shape[0]
    tm = min(T, 512)

    def body(u_ref, w_ref, sc_ref, out_ref, pooled_ref, halo):
        i = pl.program_id(0)

        @pl.when(i == 0)
        def _():
            halo[...] = jnp.zeros_like(halo)

        ub = u_ref[...]
        ext = jnp.concatenate([halo[...], ub], axis=0)
        halo[...] = ub[tm - HALO:, :]
        tpos = (i * tm + lax.broadcasted_iota(jnp.int32, (tm, 1), 0)).astype(F32)
        for g, w in enumerate(POOL_WINDOWS):
            lanes = slice(g * 128, (g + 1) * 128)
            s = ext[:, lanes]
            sh = 1
            while sh < w:
                s = s + pltpu.roll(s, sh, 0)
                sh *= 2
            cnt = jnp.minimum(tpos + 1.0, float(w))
            pb = (s[HALO:, :] / cnt - ub[:, lanes]).astype(BF16)
            z = _mm(pb, w_ref[g].astype(BF16))
            out_ref[:, lanes] = (z * sc_ref[:, lanes]).astype(BF16)
            pooled_ref[:, lanes] = pb

    return pl.pallas_call(
        body, name="pool_fwd", grid=(T // tm,),
        in_specs=[_rows(tm, POOL_W), _full((4, 128, 128)), _full((1, POOL_W))],
        out_specs=[_rows(tm, POOL_W), _rows(tm, POOL_W)],
        out_shape=[_sds((T, POOL_W), BF16), _sds((T, POOL_W), BF16)],
        scratch_shapes=[pltpu.VMEM((HALO, POOL_W), F32)],
        compiler_params=_cp("arbitrary"),
    )(u, pool_w, pool_scale)


def _fwd_out(attn, pool, x, w_out, g8, mod8):
    T = x.shape[0]
    tm = min(T, 512)

    def body(a_ref, p_ref, x_ref, w_ref, g_ref, mod_ref, mix_ref, x1_ref):
        mix = _mm(a_ref[...], w_ref[0:ATTN_W, :]) + _mm(p_ref[...], w_ref[ATTN_W:, :])
        mix_ref[...] = mix
        x1_ref[...] = x_ref[...] + mod_ref[2:3, :] * (mix * _rstd(mix) * g_ref[1:2, :])

    return pl.pallas_call(
        body, name="fwd_out", grid=(T // tm,),
        in_specs=[_rows(tm, ATTN_W), _rows(tm, POOL_W), _rows(tm, D_MODEL),
                  _resident((D_MODEL, D_MODEL)), _full((8, D_MODEL)), _full((8, D_MODEL))],
        out_specs=[_rows(tm, D_MODEL), _rows(tm, D_MODEL)],
        out_shape=[_sds((T, D_MODEL), F32), _sds((T, D_MODEL), F32)],
        compiler_params=_cp("parallel"),
    )(attn, pool, x, w_out, g8, mod8)


def _sh_rows(tm):
    return pl.BlockSpec((N_SHARD, tm, FF_SH), lambda i: (0, i, 0))


def _ffn_fwd(x1, mod8, g8, wg, wu, wd):
    T = x1.shape[0]
    tm = min(T, 256)

    def body(x_ref, mod_ref, g_ref, wg_ref, wu_ref, wd_ref, a_ref, b_ref, f_ref, x2_ref):
        xf = x_ref[...]
        h = (xf * _rstd(xf) * g_ref[2:3, :]) * (1.0 + mod_ref[4:5, :]) + mod_ref[3:4, :]
        hb = h.astype(BF16)
        f = jnp.zeros((tm, D_MODEL), F32)
        for s in range(N_SHARD):
            a = _mm(hb, wg_ref[s])
            b = _mm(hb, wu_ref[s])
            a_ref[s] = a.astype(BF16)
            b_ref[s] = b.astype(BF16)
            act = (a * jax.nn.sigmoid(a)) * b
            f = f + _mm(act.astype(BF16), wd_ref[s])
        f_ref[...] = f
        x2_ref[...] = xf + mod_ref[5:6, :] * (f * _rstd(f) * g_ref[3:4, :])

    act_shape = _sds((N_SHARD, T, FF_SH), BF16)
    return pl.pallas_call(
        body, name="ffn_fwd", grid=(T // tm,),
        in_specs=[_rows(tm, D_MODEL), _full((8, D_MODEL)), _full((8, D_MODEL)),
                  _resident((N_SHARD, D_MODEL, FF_SH)), _resident((N_SHARD, D_MODEL, FF_SH)),
                  _resident((N_SHARD, FF_SH, D_MODEL))],
        out_specs=[_sh_rows(tm), _sh_rows(tm), _rows(tm, D_MODEL), _rows(tm, D_MODEL)],
        out_shape=[act_shape, act_shape, _sds((T, D_MODEL), F32), _sds((T, D_MODEL), F32)],
        compiler_params=_cp("parallel"),
    )(x1, mod8, g8, wg, wu, wd)


def _loss_grad(y, target):
    T = y.shape[0]
    tm = min(T, 1024)

    def body(y_ref, t_ref, dy_ref, loss_ref):
        @pl.when(pl.program_id(0) == 0)
        def _():
            loss_ref[...] = jnp.zeros_like(loss_ref)

        e = y_ref[...] - t_ref[...]
        dy_ref[...] = e * (1.0 / D_MODEL)
        part = 0.5 * jnp.sum(jnp.mean(e * e, axis=-1, keepdims=True), axis=0, keepdims=True)
        loss_ref[...] += part

    return pl.pallas_call(
        body, name="loss_grad", grid=(T // tm,),
        in_specs=[_rows(tm, D_MODEL), _rows(tm, D_MODEL)],
        out_specs=[_rows(tm, D_MODEL), _full((8, 128))],
        out_shape=[_sds((T, D_MODEL), F32), _sds((8, 128), F32)],
        compiler_params=_cp("arbitrary"),
    )(y, target)


def _ffn_bwd_act(dx2, f, a, b, mod8, g8, wd):
    T = dx2.shape[0]
    tm = min(T, 256)

    def body(dx_ref, f_ref, a_ref, b_ref, mod_ref, g_ref, wd_ref,
             df_ref, da_ref, db_ref, act_ref, red_ref):
        @pl.when(pl.program_id(0) == 0)
        def _():
            red_ref[...] = jnp.zeros_like(red_ref)

        dx = dx_ref[...]
        fv = f_ref[...]
        rstd = _rstd(fv)
        fhat = fv * rstd
        gpost = g_ref[3:4, :]
        red_ref[0:1, :] += _colsum(dx * (fhat * gpost))
        dn = dx * mod_ref[5:6, :]
        red_ref[1:2, :] += _colsum(dn * fhat)
        dfb = _norm_bwd(dn * gpost, fhat, rstd).astype(BF16)
        df_ref[...] = dfb
        for s in range(N_SHARD):
            dact = _nt(dfb, wd_ref[s])
            av = a_ref[s].astype(F32)
            bv = b_ref[s].astype(F32)
            sig = jax.nn.sigmoid(av)
            sl = av * sig
            act_ref[s] = (sl * bv).astype(BF16)
            da_ref[s] = (dact * bv * (sig * (1.0 + av * (1.0 - sig)))).astype(BF16)
            db_ref[s] = (dact * sl).astype(BF16)

    act_shape = _sds((N_SHARD, T, FF_SH), BF16)
    return pl.pallas_call(
        body, name="ffn_bwd_act", grid=(T // tm,),
        in_specs=[_rows(tm, D_MODEL), _rows(tm, D_MODEL), _sh_rows(tm), _sh_rows(tm),
                  _full((8, D_MODEL)), _full((8, D_MODEL)), _resident((N_SHARD, FF_SH, D_MODEL))],
        out_specs=[_rows(tm, D_MODEL), _sh_rows(tm), _sh_rows(tm), _sh_rows(tm), _full((8, D_MODEL))],
        out_shape=[_sds((T, D_MODEL), BF16), act_shape, act_shape, act_shape, _sds((8, D_MODEL), F32)],
        compiler_params=_cp("arbitrary"),
    )(dx2, f, a, b, mod8, g8, wd)


def _ffn_bwd_in(da, db, x1, dx2, mod8, g8, wg, wu):
    T = x1.shape[0]
    tm = min(T, 256)

    def body(da_ref, db_ref, x_ref, dx_ref, mod_ref, g_ref, wg_ref, wu_ref, dx1_ref, h2_ref, red_ref):
        @pl.when(pl.program_id(0) == 0)
        def _():
            red_ref[...] = jnp.zeros_like(red_ref)

        dh = jnp.zeros((tm, D_MODEL), F32)
        for s in range(N_SHARD):
            dh = dh + _nt(da_ref[s], wg_ref[s]) + _nt(db_ref[s], wu_ref[s])
        xf = x_ref[...]
        rstd = _rstd(xf)
        xhat = xf * rstd
        gpre = g_ref[2:3, :]
        scale1 = 1.0 + mod_ref[4:5, :]
        h2_ref[...] = ((xhat * gpre) * scale1 + mod_ref[3:4, :]).astype(BF16)
        red_ref[0:1, :] += _colsum(dh)
        red_ref[1:2, :] += _colsum(dh * (xhat * gpre))
        red_ref[2:3, :] += _colsum(dh * scale1 * xhat)
        dx1_ref[...] = dx_ref[...] + _norm_bwd(dh * scale1 * gpre, xhat, rstd)

    return pl.pallas_call(
        body, name="ffn_bwd_in", grid=(T // tm,),
        in_specs=[_sh_rows(tm), _sh_rows(tm), _rows(tm, D_MODEL), _rows(tm, D_MODEL),
                  _full((8, D_MODEL)), _full((8, D_MODEL)),
                  _resident((N_SHARD, D_MODEL, FF_SH)), _resident((N_SHARD, D_MODEL, FF_SH))],
        out_specs=[_rows(tm, D_MODEL), _rows(tm, D_MODEL), _full((8, D_MODEL))],
        out_shape=[_sds((T, D_MODEL), F32), _sds((T, D_MODEL), BF16), _sds((8, D_MODEL), F32)],
        compiler_params=_cp("arbitrary"),
    )(da, db, x1, dx2, mod8, g8, wg, wu)


def _wgrad(a, b, name):
    T, K = a.shape
    N = b.shape[1]
    tt = min(T, 512)
    tk = min(K, 512)

    def body(a_ref, b_ref, o_ref):
        @pl.when(pl.program_id(1) == 0)
        def _():
            o_ref[...] = jnp.zeros_like(o_ref)

        o_ref[...] += _tn(a_ref[...], b_ref[...])

    return pl.pallas_call(
        body, name=name, grid=(K // tk, T // tt),
        in_specs=[pl.BlockSpec((tt, tk), lambda i, t: (t, i)), pl.BlockSpec((tt, N), lambda i, t: (t, 0))],
        out_specs=pl.BlockSpec((tk, N), lambda i, t: (i, 0)),
        out_shape=_sds((K, N), F32),
        compiler_params=_cp("parallel", "arbitrary"),
    )(a, b)


def _wgrad_cols(a, b, name):
    T, K = a.shape
    n = b.shape[2]
    tt = min(T, 512)

    def body(a_ref, b_ref, o_ref):
        @pl.when(pl.program_id(1) == 0)
        def _():
            o_ref[...] = jnp.zeros_like(o_ref)

        o_ref[...] += _tn(a_ref[...], b_ref[...])

    return pl.pallas_call(
        body, name=name, grid=(N_SHARD, T // tt),
        in_specs=[pl.BlockSpec((tt, K), lambda s, t: (t, 0)), pl.BlockSpec((None, tt, n), lambda s, t: (s, t, 0))],
        out_specs=pl.BlockSpec((None, K, n), lambda s, t: (s, 0, 0)),
        out_shape=_sds((N_SHARD, K, n), F32),
        compiler_params=_cp("parallel", "arbitrary"),
    )(a, b)


def _wgrad_rows(a, b, name):
    T, N = b.shape
    k = a.shape[2]
    tt = min(T, 512)

    def body(a_ref, b_ref, o_ref):
        @pl.when(pl.program_id(1) == 0)
        def _():
            o_ref[...] = jnp.zeros_like(o_ref)

        o_ref[...] += _tn(a_ref[...], b_ref[...])

    return pl.pallas_call(
        body, name=name, grid=(N_SHARD, T // tt),
        in_specs=[pl.BlockSpec((None, tt, k), lambda s, t: (s, t, 0)), pl.BlockSpec((tt, N), lambda s, t: (t, 0))],
        out_specs=pl.BlockSpec((None, k, N), lambda s, t: (s, 0, 0)),
        out_shape=_sds((N_SHARD, k, N), F32),
        compiler_params=_cp("parallel", "arbitrary"),
    )(a, b)


def _mix_bwd(dx1, mix, mod8, g8, w_out):
    T = dx1.shape[0]
    tm = min(T, 512)

    def body(dx_ref, mix_ref, mod_ref, g_ref, w_ref, dmix_ref, da_ref, dp_ref, red_ref):
        @pl.when(pl.program_id(0) == 0)
        def _():
            red_ref[...] = jnp.zeros_like(red_ref)

        dx = dx_ref[...]
        mv = mix_ref[...]
        rstd = _rstd(mv)
        mhat = mv * rstd
        gpost = g_ref[1:2, :]
        red_ref[0:1, :] += _colsum(dx * (mhat * gpost))
        dn = dx * mod_ref[2:3, :]
        red_ref[1:2, :] += _colsum(dn * mhat)
        dmb = _norm_bwd(dn * gpost, mhat, rstd).astype(BF16)
        dmix_ref[...] = dmb
        da_ref[...] = _nt(dmb, w_ref[0:ATTN_W, :]).astype(BF16)
        dp_ref[...] = _nt(dmb, w_ref[ATTN_W:, :]).astype(BF16)

    return pl.pallas_call(
        body, name="mix_bwd", grid=(T // tm,),
        in_specs=[_rows(tm, D_MODEL), _rows(tm, D_MODEL), _full((8, D_MODEL)), _full((8, D_MODEL)),
                  _resident((D_MODEL, D_MODEL))],
        out_specs=[_rows(tm, D_MODEL), _rows(tm, ATTN_W), _rows(tm, POOL_W), _full((8, D_MODEL))],
        out_shape=[_sds((T, D_MODEL), BF16), _sds((T, ATTN_W), BF16), _sds((T, POOL_W), BF16),
                   _sds((8, D_MODEL), F32)],
        compiler_params=_cp("arbitrary"),
    )(dx1, mix, mod8, g8, w_out)


def _attn_bwd(q, k, v, lse, dattn, sink_b):
    T = q.shape[0]
    nb = T // BLK

    def body(q_ref, do_ref, lse_ref, kp_ref, kc_ref, vp_ref, vc_ref, sk_ref,
             dq_ref, dk_ref, dv_ref, dsk_ref, carry_k, carry_v):
        n = pl.program_id(0)

        @pl.when(n == 0)
        def _():
            carry_k[...] = jnp.zeros_like(carry_k)
            carry_v[...] = jnp.zeros_like(carry_v)
            dsk_ref[...] = jnp.zeros_like(dsk_ref)

        @pl.when(n < nb)
        def _():
            kcat = jnp.concatenate([kp_ref[...], kc_ref[...]], axis=0)
            vcat = jnp.concatenate([vp_ref[...], vc_ref[...]], axis=0)
            valid = _band_mask(n)
            lane = lax.broadcasted_iota(jnp.int32, (BLK, 128), 1)
            lse_all = lse_ref[...]
            dqs = []
            dks = []
            dvs = []
            for j in range(N_HEADS // GROUP):
                kj = kcat[:, j * HEAD:(j + 1) * HEAD]
                vj = vcat[:, j * HEAD:(j + 1) * HEAD]
                ds_g = []
                p_g = []
                q_g = []
                do_g = []
                for g in range(GROUP):
                    h = j * GROUP + g
                    qh = q_ref[:, h * HEAD:(h + 1) * HEAD]
                    doh = do_ref[:, h * HEAD:(h + 1) * HEAD]
                    lse_h = jnp.sum(jnp.where(lane == h, lse_all, 0.0), axis=-1, keepdims=True)
                    s = jnp.where(valid, _nt(qh, kj), NEG_INF)
                    p = jnp.exp(s - lse_h)
                    dp = _nt(doh, vj)
                    delta = jnp.sum(p * dp, axis=-1, keepdims=True)
                    ds = (p * (dp - delta)).astype(BF16)
                    psink = jnp.exp(sk_ref[h:h + 1, 0:1] - lse_h)
                    dsk_ref[h:h + 1, :] += -jnp.sum(psink * delta, axis=0, keepdims=True)
                    dqs.append(_mm(ds, kj))
                    ds_g.append(ds)
                    p_g.append(p.astype(BF16))
                    q_g.append(qh)
                    do_g.append(doh)
                dks.append(_tn(jnp.concatenate(ds_g, axis=0), jnp.concatenate(q_g, axis=0)))
                dvs.append(_tn(jnp.concatenate(p_g, axis=0), jnp.concatenate(do_g, axis=0)))
            dq_ref[...] = jnp.concatenate(dqs, axis=1)
            dkcat = jnp.concatenate(dks, axis=1)
            dvcat = jnp.concatenate(dvs, axis=1)
            dk_ref[...] = carry_k[...] + dkcat[0:BLK, :]
            dv_ref[...] = carry_v[...] + dvcat[0:BLK, :]
            carry_k[...] = dkcat[BLK:, :]
            carry_v[...] = dvcat[BLK:, :]

        @pl.when(n == nb)
        def _():
            dk_ref[...] = carry_k[...]
            dv_ref[...] = carry_v[...]

    cur = lambda n: (jnp.minimum(n, nb - 1), 0)
    prev = lambda n: (jnp.maximum(n - 1, 0), 0)
    return pl.pallas_call(
        body, name="attn_bwd", grid=(nb + 1,),
        in_specs=[pl.BlockSpec((BLK, ATTN_W), cur), pl.BlockSpec((BLK, ATTN_W), cur), pl.BlockSpec((BLK, 128), cur),
                  pl.BlockSpec((BLK, KV_W), prev), pl.BlockSpec((BLK, KV_W), cur),
                  pl.BlockSpec((BLK, KV_W), prev), pl.BlockSpec((BLK, KV_W), cur),
                  _full((8, 128))],
        out_specs=[pl.BlockSpec((BLK, ATTN_W), cur), pl.BlockSpec((BLK, KV_W), prev),
                   pl.BlockSpec((BLK, KV_W), prev), _full((8, 128))],
        out_shape=[_sds((T, ATTN_W), F32), _sds((T, KV_W), F32), _sds((T, KV_W), F32), _sds((8, 128), F32)],
        scratch_shapes=[pltpu.VMEM((BLK, KV_W), F32), pltpu.VMEM((BLK, KV_W), F32)],
        compiler_params=_cp("arbitrary"),
    )(q, dattn, lse, k, k, v, v, sink_b)


def _pool_bwd(dpool, pooled, pool_w, pool_scale):
    T = dpool.shape[0]
    tm = min(T, 512)
    nbk = T // tm
    ext_rows = tm + HALO

    def body(dp_ref, pl_ref, w_ref, sc_ref, du_ref, dw_ref, dsc_ref, halo):
        i = pl.program_id(0)

        @pl.when(i == 0)
        def _():
            halo[...] = jnp.zeros_like(halo)
            dw_ref[...] = jnp.zeros_like(dw_ref)
            dsc_ref[...] = jnp.zeros_like(dsc_ref)

        blk = nbk - 1 - i
        tpos = (blk * tm + lax.broadcasted_iota(jnp.int32, (tm, 1), 0)).astype(F32)
        for g, w in enumerate(POOL_WINDOWS):
            lanes = slice(g * 128, (g + 1) * 128)
            dp = dp_ref[:, lanes].astype(F32)
            pb = pl_ref[:, lanes]
            wg = w_ref[g].astype(BF16)
            z = _mm(pb, wg)
            dsc_ref[0:1, lanes] += _colsum(dp * z)
            dz = (dp * sc_ref[:, lanes]).astype(BF16)
            dw_ref[g] += _tn(pb, dz)
            dpl = _nt(dz, wg)
            e = dpl / jnp.minimum(tpos + 1.0, float(w))
            s = jnp.concatenate([e, halo[:, lanes]], axis=0)
            halo[:, lanes] = e[0:HALO, :]
            sh = 1
            while sh < w:
                s = s + pltpu.roll(s, ext_rows - sh, 0)
                sh *= 2
            du_ref[:, lanes] = s[0:tm, :] - dpl

    rev = lambda i: (nbk - 1 - i, 0)
    return pl.pallas_call(
        body, name="pool_bwd", grid=(nbk,),
        in_specs=[pl.BlockSpec((tm, POOL_W), rev), pl.BlockSpec((tm, POOL_W), rev),
                  _full((4, 128, 128)), _full((1, POOL_W))],
        out_specs=[pl.BlockSpec((tm, POOL_W), rev), _full((4, 128, 128)), _full((8, POOL_W))],
        out_shape=[_sds((T, POOL_W), F32), _sds((4, 128, 128), F32), _sds((8, POOL_W), F32)],
        scratch_shapes=[pltpu.VMEM((HALO, POOL_W), F32)],
        compiler_params=_cp("arbitrary"),
    )(dpool, pooled, pool_w, pool_scale)


def _in_bwd(dq, dk, dv, du, rc, rs1, rs2, x, dx1, mod8, g8, w_in):
    T = x.shape[0]
    tm = min(T, 512)

    def body(dq_ref, dk_ref, dv_ref, du_ref, c_ref, s1_ref, s2_ref, x_ref, dx1_ref, mod_ref, g_ref, w_ref,
             dx_ref, dproj_ref, red_ref, dbin_ref):
        @pl.when(pl.program_id(0) == 0)
        def _():
            red_ref[...] = jnp.zeros_like(red_ref)
            dbin_ref[...] = jnp.zeros_like(dbin_ref)

        c = c_ref[...]
        s1 = s1_ref[...]
        s2 = s2_ref[...]
        dqp = _rot_bwd(dq_ref[...] * (HEAD ** -0.5), jnp.tile(c, (1, 4)), jnp.tile(s1, (1, 4)), jnp.tile(s2, (1, 4)))
        dkp = _rot_bwd(dk_ref[...], c, s1, s2)
        pieces = ((0, ATTN_W, dqp), (ATTN_W, ATTN_W + KV_W, dkp),
                  (ATTN_W + KV_W, ATTN_W + 2 * KV_W, dv_ref[...]), (ATTN_W + 2 * KV_W, IN_W, du_ref[...]))
        dh = jnp.zeros((tm, D_MODEL), F32)
        for lo, hi, val in pieces:
            dbin_ref[0:1, lo:hi] += _colsum(val)
            vb = val.astype(BF16)
            dproj_ref[:, lo:hi] = vb
            dh = dh + _nt(vb, w_ref[:, lo:hi])
        xf = x_ref[...]
        rstd = _rstd(xf)
        xhat = xf * rstd
        gpre = g_ref[0:1, :]
        scale1 = 1.0 + mod_ref[1:2, :]
        red_ref[0:1, :] += _colsum(dh)
        red_ref[1:2, :] += _colsum(dh * (xhat * gpre))
        red_ref[2:3, :] += _colsum(dh * scale1 * xhat)
        dx_ref[...] = dx1_ref[...] + _norm_bwd(dh * scale1 * gpre, xhat, rstd)

    return pl.pallas_call(
        body, name="in_bwd", grid=(T // tm,),
        in_specs=[_rows(tm, ATTN_W), _rows(tm, KV_W), _rows(tm, KV_W), _rows(tm, POOL_W),
                  _rows(tm, 128), _rows(tm, 128), _rows(tm, 128), _rows(tm, D_MODEL), _rows(tm, D_MODEL),
                  _full((8, D_MODEL)), _full((8, D_MODEL)), _resident((D_MODEL, IN_W))],
        out_specs=[_rows(tm, D_MODEL), _rows(tm, IN_W), _full((8, D_MODEL)), _full((8, IN_W))],
        out_shape=[_sds((T, D_MODEL), F32), _sds((T, IN_W), BF16), _sds((8, D_MODEL), F32), _sds((8, IN_W), F32)],
        compiler_params=_cp("arbitrary"),
    )(dq, dk, dv, du, rc, rs1, rs2, x, dx1, mod8, g8, w_in)


def _mod_fwd(c_all, ada_w, ada_b_sh):
    tn = 512

    def body(c_ref, w_ref, b_ref, o_ref):
        cv = c_ref[...]
        ca = (cv * jax.nn.sigmoid(cv)).astype(BF16)
        o_ref[...] = _mm(ca, w_ref[...].astype(BF16)) + b_ref[...]

    return pl.pallas_call(
        body, name="mod_fwd", grid=(2, ADA_SH // tn),
        in_specs=[_full((8, D_MODEL)), pl.BlockSpec((None, D_MODEL, tn), lambda l, j: (l, 0, j)),
                  pl.BlockSpec((None, 1, tn), lambda l, j: (l, 0, j))],
        out_specs=pl.BlockSpec((None, 8, tn), lambda l, j: (l, 0, j)),
        out_shape=_sds((2, 8, ADA_SH), F32),
        compiler_params=_cp("parallel", "parallel"),
    )(c_all, ada_w, ada_b_sh)


def _ada_wgrad(c_all_t, dmod_sh):
    tn = 512

    def body(c_ref, d_ref, o_ref):
        cv = c_ref[...]
        ca = cv * jax.nn.sigmoid(cv)
        o_ref[...] = jnp.dot(ca, d_ref[...], preferred_element_type=F32, precision=lax.Precision.HIGHEST)

    return pl.pallas_call(
        body, name="ada_wgrad", grid=(2, ADA_SH // tn),
        in_specs=[_full((D_MODEL, 8)), pl.BlockSpec((None, 8, tn), lambda l, j: (l, 0, j))],
        out_specs=pl.BlockSpec((None, D_MODEL, tn), lambda l, j: (l, 0, j)),
        out_shape=_sds((2, D_MODEL, ADA_SH), F32),
        compiler_params=_cp("parallel", "parallel"),
    )(c_all_t, dmod_sh)


def _sum_devices(g):
    R = g.shape[1]

    def body(g_ref, o_ref):
        acc = g_ref[0]
        for d in range(1, N_DEV):
            acc = acc + g_ref[d]
        o_ref[...] = acc

    return pl.pallas_call(
        body, name="sum_devices", grid=(1,),
        in_specs=[_full((N_DEV, R, 128))], out_specs=_full((R, 128)), out_shape=_sds((R, 128), F32),
        compiler_params=_cp("arbitrary"),
    )(g)


def _adamw(w, g, m, v, name):
    R, C = w.shape
    tr = R
    for cand in (256, 128, 64, 32, 16, 8):
        if R % cand == 0 and cand * C * 4 <= 2 * 1024 * 1024:
            tr = cand
            break

    def body(w_ref, g_ref, m_ref, v_ref, d_ref, nm_ref, nv_ref):
        gv = g_ref[...]
        mn = ADAM_B1 * m_ref[...] + (1.0 - ADAM_B1) * gv
        vn = ADAM_B2 * v_ref[...] + (1.0 - ADAM_B2) * (gv * gv)
        m_hat = mn / (1.0 - ADAM_B1 ** ADAM_STEP)
        v_hat = vn / (1.0 - ADAM_B2 ** ADAM_STEP)
        d_ref[...] = -ADAM_LR * (m_hat / (jnp.sqrt(v_hat) + ADAM_EPS) + ADAM_WD * w_ref[...])
        nm_ref[...] = mn
        nv_ref[...] = vn

    spec = pl.BlockSpec((tr, C), lambda i: (i, 0))
    out = _sds((R, C), F32)
    return pl.pallas_call(
        body, name=name, grid=(R // tr,),
        in_specs=[spec] * 4, out_specs=[spec] * 3, out_shape=[out] * 3,
        compiler_params=_cp("parallel"),
    )(w, g, m, v)


def _adamw_nd(w, g, m, v, name):
    shape = w.shape
    if w.ndim == 2 and shape[1] < 128:
        view = (1, shape[0] * shape[1])
    else:
        view = (-1, shape[-1])
    outs = _adamw(*[t.reshape(view) for t in (w, g, m, v)], name=name)
    return [o.reshape(shape) for o in outs]


def _coords():
    return lax.axis_index("x"), lax.axis_index("y"), lax.axis_index("c")


def _other_chips(x, y):
    return [(1 - x, y), (x, 1 - y), (1 - x, 1 - y)]


def _allgather8(blk, name):
    m_per, n = blk.shape

    def body(x_ref, out_ref, send_sems, recv_sems, local_sem):
        x, y, c = _coords()
        me, sibling = (x, y, c), (x, y, 1 - c)
        chips = _other_chips(x, y)

        def rows(px, py, pc):
            return out_ref.at[pl.ds((4 * px + 2 * py + pc) * m_per, m_per), :]

        def copy(k, block, to, src=None):
            return pltpu.make_async_remote_copy(
                src_ref=rows(*block) if src is None else src, dst_ref=rows(*block),
                send_sem=send_sems.at[k], recv_sem=recv_sems.at[k], device_id=to, device_id_type=MESH)

        mine = pltpu.make_async_copy(x_ref, rows(*me), local_sem)
        mine.start()
        first = [copy(0, me, sibling, src=x_ref)]
        first += [copy(1 + j, me, (*chip, c), src=x_ref) for j, chip in enumerate(chips)]
        for cp in first:
            cp.start()
        passed = [copy(4 + j, (*chip, c), sibling) for j, chip in enumerate(chips)]
        for j, chip in enumerate(chips):
            copy(1 + j, (*chip, c), me).wait_recv()
            passed[j].start()
        copy(0, sibling, me).wait_recv()
        for j, chip in enumerate(chips):
            copy(4 + j, (*chip, 1 - c), me).wait_recv()
        for cp in first + passed:
            cp.wait_send()
        mine.wait()

    return pl.pallas_call(
        body, name=name,
        out_shape=_sds((N_DEV * m_per, n), blk.dtype),
        in_specs=[pl.BlockSpec(memory_space=pltpu.VMEM)],
        out_specs=pl.BlockSpec(memory_space=pltpu.VMEM),
        scratch_shapes=[pltpu.SemaphoreType.DMA((7,)), pltpu.SemaphoreType.DMA((7,)), pltpu.SemaphoreType.DMA],
        compiler_params=pltpu.CompilerParams(vmem_limit_bytes=VMEM_LIMIT),
    )(blk)


def _allgather_weights(shards, name):
    nt = len(shards)
    hom = [pl.BlockSpec(memory_space=pl.ANY)] * nt

    def body(*refs):
        ins = refs[:nt]
        outs = refs[nt:2 * nt]
        send_sems, recv_sems, local_sems = refs[2 * nt:]
        x, y, c = _coords()
        p = 2 * x + y
        sibling = (x, y, 1 - c)
        chips = _other_chips(x, y)

        def half(t, chip, hc):
            r = ins[t].shape[0] // 2
            return outs[t].at[2 * chip[0] + chip[1], pl.ds(hc * r, r)]

        def copy(t, k, block_chip, hc, to, src=None):
            dst = half(t, block_chip, hc)
            return pltpu.make_async_remote_copy(
                src_ref=dst if src is None else src, dst_ref=dst,
                send_sem=send_sems.at[t, k], recv_sem=recv_sems.at[t, k], device_id=to, device_id_type=MESH)

        started = []
        locals_ = []
        for t in range(nt):
            r = ins[t].shape[0] // 2
            lc = pltpu.make_async_copy(ins[t], outs[t].at[p], local_sems.at[t])
            lc.start()
            locals_.append(lc)
            for j, chip in enumerate(chips):
                cp = copy(t, j, (x, y), c, (*chip, c), src=ins[t].at[pl.ds(c * r, r)])
                cp.start()
                started.append(cp)
        for t in range(nt):
            for j, chip in enumerate(chips):
                copy(t, j, chip, c, sibling).wait_recv()
                fw = copy(t, 3 + j, chip, c, sibling)
                fw.start()
                started.append(fw)
        for t in range(nt):
            for j, chip in enumerate(chips):
                copy(t, 3 + j, chip, 1 - c, sibling).wait_recv()
        for cp in started:
            cp.wait_send()
        for lc in locals_:
            lc.wait()

    return pl.pallas_call(
        body, name=name,
        out_shape=[_sds((N_SHARD,) + s.shape, s.dtype) for s in shards],
        in_specs=hom, out_specs=hom,
        scratch_shapes=[pltpu.SemaphoreType.DMA((nt, 6)), pltpu.SemaphoreType.DMA((nt, 6)),
                        pltpu.SemaphoreType.DMA((nt,))],
    )(*shards)


def _swap_halves(grads, name):
    nt = len(grads)
    hom = [pl.BlockSpec(memory_space=pl.ANY)] * nt

    def body(*refs):
        ins = refs[:nt]
        outs = refs[nt:2 * nt]
        send_sems, recv_sems = refs[2 * nt:]
        x, y, c = _coords()
        sibling = (x, y, 1 - c)
        cps = []
        for t in range(nt):
            r = ins[t].shape[1] // 2
            cp = pltpu.make_async_remote_copy(
                src_ref=ins[t].at[:, pl.ds((1 - c) * r, r)], dst_ref=outs[t],
                send_sem=send_sems.at[t], recv_sem=recv_sems.at[t], device_id=sibling, device_id_type=MESH)
            cp.start()
            cps.append(cp)
        for cp in cps:
            cp.wait()

    return pl.pallas_call(
        body, name=name,
        out_shape=[_sds((N_SHARD, g.shape[1] // 2, g.shape[2]), g.dtype) for g in grads],
        in_specs=hom, out_specs=hom,
        scratch_shapes=[pltpu.SemaphoreType.DMA((nt,)), pltpu.SemaphoreType.DMA((nt,))],
    )(*grads)


def _scatter_chips(sums, name):
    nt = len(sums)
    hom = [pl.BlockSpec(memory_space=pl.ANY)] * nt

    def body(*refs):
        ins = refs[:nt]
        outs = refs[nt:2 * nt]
        send_sems, recv_sems = refs[2 * nt:]
        x, y, c = _coords()
        chips = _other_chips(x, y)
        cps = []
        for t in range(nt):
            for j, chip in enumerate(chips):
                cp = pltpu.make_async_remote_copy(
                    src_ref=ins[t].at[2 * chip[0] + chip[1]], dst_ref=outs[t].at[j],
                    send_sem=send_sems.at[t, j], recv_sem=recv_sems.at[t, j],
                    device_id=(*chip, c), device_id_type=MESH)
                cp.start()
                cps.append(cp)
        for cp in cps:
            cp.wait()

    return pl.pallas_call(
        body, name=name,
        out_shape=[_sds((3,) + s.shape[1:], s.dtype) for s in sums],
        in_specs=hom, out_specs=hom,
        scratch_shapes=[pltpu.SemaphoreType.DMA((nt, 3)), pltpu.SemaphoreType.DMA((nt, 3))],
    )(*sums)


def _join_halves(tots, name):
    nt = len(tots)
    hom = [pl.BlockSpec(memory_space=pl.ANY)] * nt

    def body(*refs):
        ins = refs[:nt]
        outs = refs[nt:2 * nt]
        send_sems, recv_sems, local_sems = refs[2 * nt:]
        x, y, c = _coords()
        sibling = (x, y, 1 - c)
        cps = []
        lcs = []
        for t in range(nt):
            lc = pltpu.make_async_copy(ins[t], outs[t].at[c], local_sems.at[t])
            lc.start()
            lcs.append(lc)
            cp = pltpu.make_async_remote_copy(
                src_ref=ins[t], dst_ref=outs[t].at[c],
                send_sem=send_sems.at[t], recv_sem=recv_sems.at[t], device_id=sibling, device_id_type=MESH)
            cp.start()
            cps.append(cp)
        for cp in cps:
            cp.wait()
        for lc in lcs:
            lc.wait()

    return pl.pallas_call(
        body, name=name,
        out_shape=[_sds((2,) + t.shape, t.dtype) for t in tots],
        in_specs=hom, out_specs=hom,
        scratch_shapes=[pltpu.SemaphoreType.DMA((nt,)), pltpu.SemaphoreType.DMA((nt,)),
                        pltpu.SemaphoreType.DMA((nt,))],
    )(*tots)


def _pair_sum(g, recv, core, chip, name):
    _, _, r, n = g.shape
    tr = r
    for cand in (512, 256, 128, 64, 32, 16):
        if r % cand == 0 and cand * n * 4 <= 2 * 1024 * 1024:
            tr = cand
            break

    def body(core_ref, chip_ref, g_ref, r_ref, sb_ref, own_ref):
        tot = g_ref[...] + r_ref[...]
        sb_ref[...] = tot.astype(BF16)

        @pl.when(pl.program_id(1) == chip_ref[0])
        def _():
            own_ref[...] = tot

    grid_spec = pltpu.PrefetchScalarGridSpec(
        num_scalar_prefetch=2, grid=(r // tr, N_SHARD),
        in_specs=[pl.BlockSpec((None, None, tr, n), lambda i, s, co, ch: (s, co[0], i, 0)),
                  pl.BlockSpec((None, tr, n), lambda i, s, co, ch: (s, i, 0))],
        out_specs=[pl.BlockSpec((None, tr, n), lambda i, s, co, ch: (s, i, 0)),
                   pl.BlockSpec((tr, n), lambda i, s, co, ch: (i, 0))])
    return pl.pallas_call(
        body, name=name, grid_spec=grid_spec,
        out_shape=[_sds((N_SHARD, r, n), BF16), _sds((r, n), F32)],
        compiler_params=_cp("arbitrary", "arbitrary"),
    )(core, chip, g, recv)


def _chip_sum(own, recv, name):
    r, n = own.shape
    tr = r
    for cand in (512, 256, 128, 64, 32, 16):
        if r % cand == 0 and cand * n * 4 <= 2 * 1024 * 1024:
            tr = cand
            break

    def body(o_ref, r_ref, t_ref):
        acc = o_ref[...]
        for j in range(3):
            acc = acc + r_ref[j].astype(F32)
        t_ref[...] = acc

    return pl.pallas_call(
        body, name=name, grid=(r // tr,),
        in_specs=[pl.BlockSpec((tr, n), lambda i: (i, 0)), pl.BlockSpec((3, tr, n), lambda i: (0, i, 0))],
        out_specs=pl.BlockSpec((tr, n), lambda i: (i, 0)), out_shape=_sds((r, n), F32),
        compiler_params=_cp("parallel"),
    )(own, recv)


def _reduce_scatter(grads, tag):
    x, y, c = _coords()
    core = jnp.reshape(c, (1,)).astype(jnp.int32)
    chip = jnp.reshape(2 * x + y, (1,)).astype(jnp.int32)
    recv = _swap_halves(grads, name="rs_swap_" + tag)
    sums, owns = [], []
    for t, (g, rv) in enumerate(zip(grads, recv)):
        r = g.shape[1] // 2
        sb, own = _pair_sum(g.reshape(N_SHARD, 2, r, g.shape[2]), rv, core, chip, name=f"rs_pair_{tag}_{t}")
        sums.append(sb)
        owns.append(own)
    got = _scatter_chips(sums, name="rs_scatter_" + tag)
    tots = [_chip_sum(o, gt, name=f"rs_chip_{tag}_{t}") for t, (o, gt) in enumerate(zip(owns, got))]
    full = _join_halves(tots, name="rs_join_" + tag)
    return [f.reshape(2 * f.shape[1], f.shape[2]) for f in full]


def _rope_lane_table():
    d = jnp.arange(128) % HEAD
    inv_freq = ROPE_THETA ** (-jnp.arange(0, ROT, 2, dtype=F32) / ROT)
    rot = d < ROT
    rows = [jnp.where(rot, inv_freq[d % (ROT // 2)], 0.0), rot.astype(F32),
            (d < ROT // 2).astype(F32), jnp.logical_and(d >= ROT // 2, rot).astype(F32)]
    return jnp.concatenate([jnp.stack(rows), jnp.zeros((4, 128), F32)], axis=0)


def _pad8(rows):
    return jnp.concatenate([rows, jnp.zeros((8 - rows.shape[0], rows.shape[1]), F32)], axis=0)


def kernel(x, c, positions, ada_w, ada_b, w_in, b_in, sinks, pool_w, pool_scale, w_out, w_gate, w_up, w_down, g_pre_mix, g_post_mix, g_pre_ffn, g_post_ffn, loss_target, m_ada_w, m_ada_b, m_w_in, m_b_in, m_sinks, m_pool_w, m_pool_scale, m_w_out, m_w_gate, m_w_up, m_w_down, m_g_pre_mix, m_g_post_mix, m_g_pre_ffn, m_g_post_ffn, v_ada_w, v_ada_b, v_w_in, v_b_in, v_sinks, v_pool_w, v_pool_scale, v_w_out, v_w_gate, v_w_up, v_w_down, v_g_pre_mix, v_g_post_mix, v_g_pre_ffn, v_g_post_ffn):
    T = x.shape[1]
    n_layers = ada_w.shape[0]
    ax, ay, ac = _coords()
    my_dev = 4 * ax + 2 * ay + ac
    my_chip = 2 * ax + ay
    x0 = x.reshape(T, D_MODEL)
    target = loss_target.reshape(T, D_MODEL)

    c_all = _allgather8(c.reshape(8, 128), name="ag_c").reshape(N_DEV, D_MODEL)
    ada_b_sh = lax.dynamic_slice_in_dim(ada_b, my_chip * ADA_SH, ADA_SH, axis=1).reshape(n_layers, 1, ADA_SH)
    mod_part = _mod_fwd(c_all, ada_w, ada_b_sh)
    mod_all = _allgather8(mod_part.reshape(n_layers * 8, ADA_SH), name="ag_mod")
    mod_all = mod_all.reshape(N_DEV, n_layers, 8, ADA_SH)[0::2]
    mod_mine = lax.dynamic_index_in_dim(mod_all, my_dev, axis=2, keepdims=False)
    mod = jnp.transpose(mod_mine, (1, 0, 2)).reshape(n_layers, 6, D_MODEL)

    pos_b = jnp.broadcast_to(positions.reshape(T, 1), (T, 128))
    rc, rs1, rs2 = _rope_tables(pos_b, _rope_lane_table())

    def gather_layer(l):
        shards = [w_in[l].astype(BF16), w_out[l].astype(BF16), w_gate[l].astype(BF16),
                  w_up[l].astype(BF16), w_down[l].astype(BF16)]
        gin, gout, gg, gu, gd = _allgather_weights(shards, name=f"ag_w{l}")
        win_full = jnp.transpose(gin, (1, 0, 2)).reshape(D_MODEL, IN_W)
        return win_full, gout.reshape(D_MODEL, D_MODEL), gg, gu, gd

    weights = [gather_layer(l) for l in range(n_layers)]

    saved = []
    xl = x0
    for l in range(n_layers):
        win, wout, wg, wu, wd = weights[l]
        mod8 = _pad8(mod[l])
        g8 = _pad8(jnp.stack([g_pre_mix[l], g_post_mix[l], g_pre_ffn[l], g_post_ffn[l]]))
        sink_b = jnp.broadcast_to(sinks[l][:, None], (N_HEADS, 128))
        psc = pool_scale[l].reshape(1, POOL_W)
        h, q, k, v, u = _fwd_in(xl, mod8, g8, win, b_in[l].reshape(1, IN_W), rc, rs1, rs2)
        attn, lse = _attn_fwd(q, k, v, sink_b)
        pool, pooled = _pool_fwd(u, pool_w[l], psc)
        mix, x1 = _fwd_out(attn, pool, xl, wout, g8, mod8)
        a, b, f, x2 = _ffn_fwd(x1, mod8, g8, wg, wu, wd)
        saved.append(dict(x=xl, h=h, q=q, k=k, v=v, lse=lse, attn=attn, pool=pool, pooled=pooled, mix=mix,
                          x1=x1, a=a, b=b, f=f, mod8=mod8, g8=g8, sink_b=sink_b, psc=psc))
        xl = x2

    dy, loss_tile = _loss_grad(xl, target)
    loss = lax.psum(loss_tile[0, 0], ("x", "y", "c"))

    big_grads = [None] * n_layers
    small = [None] * n_layers
    dmod_rows = [None] * n_layers
    dx = dy
    for l in reversed(range(n_layers)):
        s = saved[l]
        win, wout, wg, wu, wd = weights[l]
        df, da, db, act, red_a = _ffn_bwd_act(dx, s["f"], s["a"], s["b"], s["mod8"], s["g8"], wd)
        dx1, h2, red_b = _ffn_bwd_in(da, db, s["x1"], dx, s["mod8"], s["g8"], wg, wu)
        g_wd = _wgrad_rows(act, df, name="wgrad_down")
        g_wg = _wgrad_cols(h2, da, name="wgrad_gate")
        g_wu = _wgrad_cols(h2, db, name="wgrad_up")
        dmix, dattn, dpool, red_c = _mix_bwd(dx1, s["mix"], s["mod8"], s["g8"], wout)
        ap = jnp.concatenate([s["attn"], s["pool"]], axis=1)
        g_wout = _wgrad(ap, dmix, name="wgrad_out")
        dq, dk, dv, dsink = _attn_bwd(s["q"], s["k"], s["v"], s["lse"], dattn, s["sink_b"])
        du, g_poolw, dpsc = _pool_bwd(dpool, s["pooled"], pool_w[l], s["psc"])
        dx, dproj, red_d, dbin = _in_bwd(dq, dk, dv, du, rc, rs1, rs2, s["x"], dx1, s["mod8"], s["g8"], win)
        g_win = _wgrad(s["h"], dproj, name="wgrad_in")
        g_win_sh = jnp.transpose(g_win.reshape(D_MODEL, N_SHARD, IN_SH), (1, 0, 2))
        big_grads[l] = [g_win_sh, g_wout.reshape(N_SHARD, OUT_SH, D_MODEL), g_wg, g_wu, g_wd]
        dmod_rows[l] = jnp.concatenate([red_d[0], red_d[1], red_c[0], red_b[0], red_b[1], red_a[0]])
        small[l] = jnp.concatenate([red_d[2], red_c[1], red_b[2], red_a[1], dbin[0], dpsc[0], dsink[:, 0],
                                    jnp.zeros((120,), F32), g_poolw.reshape(-1)])
    grad_x = dx.reshape(1, T, D_MODEL)

    per_layer = small[0].shape[0]
    rows_small = n_layers * per_layer // 128
    rows_mod = n_layers * 6 * D_MODEL // 128
    rows_pad = -(rows_small + rows_mod) % 8
    pack = jnp.concatenate(small + dmod_rows + [jnp.zeros((rows_pad * 128,), F32)]).reshape(-1, 128)
    gathered = _allgather8(pack, name="ag_small").reshape(N_DEV, pack.shape[0], 128)
    summed = _sum_devices(gathered)
    small_sum = summed[:rows_small].reshape(n_layers, per_layer)
    o = 0
    small_g = {}
    for nm, width in (("g_pre_mix", D_MODEL), ("g_post_mix", D_MODEL), ("g_pre_ffn", D_MODEL),
                      ("g_post_ffn", D_MODEL), ("b_in", IN_W), ("pool_scale", POOL_W), ("sinks", 128),
                      ("pool_w", 4 * 128 * 128)):
        small_g[nm] = small_sum[:, o:o + width]
        o += width
    small_g["sinks"] = small_g["sinks"][:, :N_HEADS]
    small_g["pool_w"] = small_g["pool_w"].reshape(n_layers, 4, 128, 128)
    small_g["ada_b"] = summed[rows_small:rows_small + rows_mod].reshape(n_layers, 6 * D_MODEL)
    dmod_all = gathered[:, rows_small:rows_small + rows_mod].reshape(N_DEV, n_layers, N_SHARD, ADA_SH)
    dmod_sh = lax.dynamic_index_in_dim(dmod_all, my_chip, axis=2, keepdims=False)
    g_ada_w = _ada_wgrad(jnp.transpose(c_all), jnp.transpose(dmod_sh, (1, 0, 2)))

    red = [_reduce_scatter(big_grads[l], tag=str(l)) for l in range(n_layers)]
    g_w_in = jnp.stack([red[l][0] for l in range(n_layers)])
    g_w_out = jnp.stack([red[l][1] for l in range(n_layers)])
    g_w_gate = jnp.stack([red[l][2] for l in range(n_layers)])
    g_w_up = jnp.stack([red[l][3] for l in range(n_layers)])
    g_w_down = jnp.stack([red[l][4] for l in range(n_layers)])

    grads = dict(ada_w=g_ada_w, ada_b=small_g["ada_b"], w_in=g_w_in, b_in=small_g["b_in"], sinks=small_g["sinks"],
                 pool_w=small_g["pool_w"], pool_scale=small_g["pool_scale"], w_out=g_w_out, w_gate=g_w_gate,
                 w_up=g_w_up, w_down=g_w_down, g_pre_mix=small_g["g_pre_mix"], g_post_mix=small_g["g_post_mix"],
                 g_pre_ffn=small_g["g_pre_ffn"], g_post_ffn=small_g["g_post_ffn"])
    params = dict(ada_w=(ada_w, m_ada_w, v_ada_w), ada_b=(ada_b, m_ada_b, v_ada_b), w_in=(w_in, m_w_in, v_w_in),
                  b_in=(b_in, m_b_in, v_b_in), sinks=(sinks, m_sinks, v_sinks), pool_w=(pool_w, m_pool_w, v_pool_w),
                  pool_scale=(pool_scale, m_pool_scale, v_pool_scale), w_out=(w_out, m_w_out, v_w_out),
                  w_gate=(w_gate, m_w_gate, v_w_gate), w_up=(w_up, m_w_up, v_w_up),
                  w_down=(w_down, m_w_down, v_w_down), g_pre_mix=(g_pre_mix, m_g_pre_mix, v_g_pre_mix),
                  g_post_mix=(g_post_mix, m_g_post_mix, v_g_post_mix), g_pre_ffn=(g_pre_ffn, m_g_pre_ffn, v_g_pre_ffn),
                  g_post_ffn=(g_post_ffn, m_g_post_ffn, v_g_post_ffn))
    names = list(params)
    deltas, new_m, new_v = [], [], []
    for nm in names:
        w, m, v = params[nm]
        d, mn, vn = _adamw_nd(w, grads[nm], m, v, name="adamw_" + nm)
        deltas.append(d)
        new_m.append(mn)
        new_v.append(vn)
    return (loss, grad_x, *[grads[nm] for nm in names], *deltas, *new_m, *new_v)
```

```python
import functools

import jax
import jax.numpy as jnp
from jax import lax
from jax.experimental import pallas as pl
from jax.experimental.pallas import tpu as pltpu

F32 = jnp.float32
BF16 = jnp.bfloat16
MESH = pl.DeviceIdType.MESH

D_MODEL = 1024
ATTN_W = 512
KV_W = 128
KVD_W = 256
POOL_W = 512
IN_W = 1280
D_FF = 2816
N_SHARD = 4
FF_SH = D_FF // N_SHARD
IN_SH = IN_W // N_SHARD
OUT_SH = D_MODEL // N_SHARD
ADA_SH = 6 * D_MODEL // N_SHARD
HEAD = 64
N_HEADS = 8
GROUP = 4
BLK = 128
POOL_WINDOWS = (2, 4, 8, 16)
HALO = 16
ROT = 16
ROPE_THETA = 500000.0
EPS = 1e-6
NEG_INF = -1e30
N_DEV = 8

ADAM_LR = 0.001
ADAM_B1 = 0.9
ADAM_B2 = 0.999
ADAM_EPS = 1e-08
ADAM_WD = 0.01
ADAM_STEP = 10

VMEM_LIMIT = 48 * 1024 * 1024


def _cp(*sem):
    return pltpu.CompilerParams(dimension_semantics=sem, vmem_limit_bytes=VMEM_LIMIT)


def _full(shape):
    nd = len(shape)
    return pl.BlockSpec(shape, lambda *_: (0,) * nd)


def _resident(shape):
    nd = len(shape)
    return pl.BlockSpec(shape, lambda *_: (0,) * nd, pipeline_mode=pl.Buffered(1))


def _rows(tm, ncol):
    return pl.BlockSpec((tm, ncol), lambda i: (i, 0))


def _sds(shape, dtype):
    return jax.ShapeDtypeStruct(shape, dtype)


def _nt(a, b):
    return lax.dot_general(a, b, (((1,), (1,)), ((), ())), preferred_element_type=F32)


def _tn(a, b):
    return lax.dot_general(a, b, (((0,), (0,)), ((), ())), preferred_element_type=F32)


def _mm(a, b):
    return jnp.dot(a, b, preferred_element_type=F32)


def _rstd(x):
    return lax.rsqrt(jnp.mean(x * x, axis=-1, keepdims=True) + EPS)


def _colsum(x):
    return jnp.sum(x, axis=0, keepdims=True)


def _norm_bwd(dhat, xhat, rstd):
    return rstd * (dhat - xhat * jnp.mean(dhat * xhat, axis=-1, keepdims=True))


def _rope_tables(pos_b, lane_tab):
    T = pos_b.shape[0]
    tm = min(T, 1024)

    def body(pos_ref, tab_ref, c_ref, s1_ref, s2_ref):
        ang = pos_ref[...].astype(F32) * tab_ref[0:1, :]
        cs = jnp.cos(ang)
        sn = jnp.sin(ang)
        m_rot = tab_ref[1:2, :]
        c_ref[...] = cs * m_rot + (1.0 - m_rot)
        s1_ref[...] = -sn * tab_ref[2:3, :]
        s2_ref[...] = sn * tab_ref[3:4, :]

    out = _sds((T, 128), F32)
    return pl.pallas_call(
        body, name="rope_tables", grid=(T // tm,),
        in_specs=[_rows(tm, 128), _full((8, 128))],
        out_specs=[_rows(tm, 128)] * 3, out_shape=[out] * 3,
        compiler_params=_cp("parallel"),
    )(pos_b, lane_tab)


def _rot_fwd(t, c, s1, s2):
    w = t.shape[-1]
    return t * c + pltpu.roll(t, w - 8, 1) * s1 + pltpu.roll(t, 8, 1) * s2


def _rot_bwd(d, c, s1, s2):
    w = d.shape[-1]
    return d * c + pltpu.roll(d * s1, 8, 1) + pltpu.roll(d * s2, w - 8, 1)


def _store_dup(ref, t):
    low = lax.broadcasted_iota(jnp.int32, t.shape, 1) < HEAD
    sw = pltpu.roll(t, HEAD, 1)
    ref[:, 0:128] = jnp.where(low, t, sw).astype(BF16)
    ref[:, 128:256] = jnp.where(low, sw, t).astype(BF16)


def _fold_dup(d):
    low = lax.broadcasted_iota(jnp.int32, (d.shape[0], 128), 1) < HEAD
    d0 = d[:, 0:128]
    d1 = d[:, 128:256]
    return jnp.where(low, d0 + pltpu.roll(d0, HEAD, 1), d1 + pltpu.roll(d1, HEAD, 1))


def _fwd_in(x, mod8, g8, w_in, b_in, rc, rs1, rs2):
    T = x.shape[0]
    tm = min(T, 512)

    def body(x_ref, mod_ref, g_ref, w_ref, b_ref, c_ref, s1_ref, s2_ref,
             h_ref, q_ref, k_ref, v_ref, u_ref):
        xf = x_ref[...]
        h = (xf * _rstd(xf) * g_ref[0:1, :]) * (1.0 + mod_ref[1:2, :]) + mod_ref[0:1, :]
        hb = h.astype(BF16)
        h_ref[...] = hb
        c = c_ref[...]
        s1 = s1_ref[...]
        s2 = s2_ref[...]
        q = _mm(hb, w_ref[:, 0:ATTN_W]) + b_ref[:, 0:ATTN_W]
        q = _rot_fwd(q, jnp.tile(c, (1, 4)), jnp.tile(s1, (1, 4)), jnp.tile(s2, (1, 4)))
        q_ref[...] = (q * (HEAD ** -0.5)).astype(BF16)
        k = _mm(hb, w_ref[:, ATTN_W:ATTN_W + KV_W]) + b_ref[:, ATTN_W:ATTN_W + KV_W]
        _store_dup(k_ref, _rot_fwd(k, c, s1, s2))
        v = _mm(hb, w_ref[:, ATTN_W + KV_W:ATTN_W + 2 * KV_W]) + b_ref[:, ATTN_W + KV_W:ATTN_W + 2 * KV_W]
        _store_dup(v_ref, v)
        u_ref[...] = _mm(hb, w_ref[:, ATTN_W + 2 * KV_W:IN_W]) + b_ref[:, ATTN_W + 2 * KV_W:IN_W]

    return pl.pallas_call(
        body, name="fwd_in", grid=(T // tm,),
        in_specs=[_rows(tm, D_MODEL), _full((8, D_MODEL)), _full((8, D_MODEL)),
                  _resident((D_MODEL, IN_W)), _full((1, IN_W)),
                  _rows(tm, 128), _rows(tm, 128), _rows(tm, 128)],
        out_specs=[_rows(tm, D_MODEL), _rows(tm, ATTN_W), _rows(tm, KVD_W), _rows(tm, KVD_W), _rows(tm, POOL_W)],
        out_shape=[_sds((T, D_MODEL), BF16), _sds((T, ATTN_W), BF16), _sds((T, KVD_W), BF16),
                   _sds((T, KVD_W), BF16), _sds((T, POOL_W), F32)],
        compiler_params=_cp("parallel"),
    )(x, mod8, g8, w_in, b_in, rc, rs1, rs2)


def _band_mask(n):
    row = lax.broadcasted_iota(jnp.int32, (BLK, 2 * BLK), 0)
    col = lax.broadcasted_iota(jnp.int32, (BLK, 2 * BLK), 1)
    first = jnp.where(n > 0, 0, 2 * BLK)
    in_prev = jnp.logical_and(col < BLK, col > row + first)
    in_cur = jnp.logical_and(col >= BLK, (col - BLK) <= row)
    return jnp.logical_or(in_prev, in_cur)


def _stack_heads(x_ref, j):
    low = lax.broadcasted_iota(jnp.int32, (BLK, 128), 1) < HEAD
    parts = []
    for gp in (2 * j, 2 * j + 1):
        x2 = x_ref[:, gp * 128:(gp + 1) * 128]
        parts.append(jnp.where(low, x2, jnp.zeros_like(x2)))
        parts.append(jnp.where(low, jnp.zeros_like(x2), x2))
    return jnp.concatenate(parts, axis=0)


def _unstack_heads(o):
    low = lax.broadcasted_iota(jnp.int32, (BLK, 128), 1) < HEAD
    return [jnp.where(low, o[0:BLK], o[BLK:2 * BLK]), jnp.where(low, o[2 * BLK:3 * BLK], o[3 * BLK:4 * BLK])]


def _sink_rows(sk_ref, j):
    return jnp.concatenate([jnp.broadcast_to(sk_ref[GROUP * j + r:GROUP * j + r + 1, 0:1], (BLK, 1))
                            for r in range(GROUP)], axis=0)


def _attn_fwd(q, kd, vd, sink_b):
    T = q.shape[0]
    nb = T // BLK

    def body(q_ref, kp_ref, kc_ref, vp_ref, vc_ref, sk_ref, o_ref, lse_ref):
        n = pl.program_id(0)
        valid = jnp.concatenate([_band_mask(n)] * GROUP, axis=0)
        lane = lax.broadcasted_iota(jnp.int32, (BLK, 128), 1)
        lse_all = jnp.zeros((BLK, 128), F32)
        for j in range(N_HEADS // GROUP):
            lanes = slice(j * 128, (j + 1) * 128)
            kcat = jnp.concatenate([kp_ref[:, lanes], kc_ref[:, lanes]], axis=0)
            vcat = jnp.concatenate([vp_ref[:, lanes], vc_ref[:, lanes]], axis=0)
            s = jnp.where(valid, _nt(_stack_heads(q_ref, j), kcat), NEG_INF)
            sk = _sink_rows(sk_ref, j)
            m = jnp.maximum(jnp.max(s, axis=-1, keepdims=True), sk)
            p = jnp.exp(s - m)
            den = jnp.sum(p, axis=-1, keepdims=True) + jnp.exp(sk - m)
            p = p * (1.0 / den)
            o = _mm(p.astype(BF16), vcat)
            o_ref[:, 2 * j * 128:(2 * j + 2) * 128] = jnp.concatenate(_unstack_heads(o), axis=1).astype(BF16)
            lse = m + jnp.log(den)
            for r in range(GROUP):
                lse_all = jnp.where(lane == GROUP * j + r, lse[r * BLK:(r + 1) * BLK], lse_all)
        lse_ref[...] = lse_all

    prev = lambda n: (jnp.maximum(n - 1, 0), 0)
    cur = lambda n: (n, 0)
    return pl.pallas_call(
        body, name="attn_fwd", grid=(nb,),
        in_specs=[pl.BlockSpec((BLK, ATTN_W), cur),
                  pl.BlockSpec((BLK, KVD_W), prev), pl.BlockSpec((BLK, KVD_W), cur),
                  pl.BlockSpec((BLK, KVD_W), prev), pl.BlockSpec((BLK, KVD_W), cur),
                  _full((8, 128))],
        out_specs=[pl.BlockSpec((BLK, ATTN_W), cur), pl.BlockSpec((BLK, 128), cur)],
        out_shape=[_sds((T, ATTN_W), BF16), _sds((T, 128), F32)],
        compiler_params=_cp("parallel"),
    )(q, kd, kd, vd, vd, sink_b)


def _pool_fwd(u, pool_w, pool_scale):
    T = u.shape[0]
    tm = min(T, 512)

    def body(u_ref, w_ref, sc_ref, out_ref, pooled_ref, halo):
        i = pl.program_id(0)

        @pl.when(i == 0)
        def _():
            halo[...] = jnp.zeros_like(halo)

        ub = u_ref[...]
        ext = jnp.concatenate([halo[...], ub], axis=0)
        halo[...] = ub[tm - HALO:, :]
        tpos = (i * tm + lax.broadcasted_iota(jnp.int32, (tm, 1), 0)).astype(F32)
        for g, w in enumerate(POOL_WINDOWS):
            lanes = slice(g * 128, (g + 1) * 128)
            s = ext[:, lanes]
            sh = 1
            while sh < w:
                s = s + pltpu.roll(s, sh, 0)
                sh *= 2
            cnt = jnp.minimum(tpos + 1.0, float(w))
            pb = (s[HALO:, :] / cnt - ub[:, lanes]).astype(BF16)
            z = _mm(pb, w_ref[g].astype(BF16))
            out_ref[:, lanes] = (z * sc_ref[:, lanes]).astype(BF16)
            pooled_ref[:, lanes] = pb

    return pl.pallas_call(
        body, name="pool_fwd", grid=(T // tm,),
        in_specs=[_rows(tm, POOL_W), _full((4, 128, 128)), _full((1, POOL_W))],
        out_specs=[_rows(tm, POOL_W), _rows(tm, POOL_W)],
        out_shape=[_sds((T, POOL_W), BF16), _sds((T, POOL_W), BF16)],
        scratch_shapes=[pltpu.VMEM((HALO, POOL_W), F32)],
        compiler_params=_cp("arbitrary"),
    )(u, pool_w, pool_scale)


def _fwd_out(attn, pool, x, w_out, g8, mod8):
    T = x.shape[0]
    tm = min(T, 512)

    def body(a_ref, p_ref, x_ref, w_ref, g_ref, mod_ref, mix_ref, x1_ref):
        mix = _mm(a_ref[...], w_ref[0:ATTN_W, :]) + _mm(p_ref[...], w_ref[ATTN_W:, :])
        mix_ref[...] = mix
        x1_ref[...] = x_ref[...] + mod_ref[2:3, :] * (mix * _rstd(mix) * g_ref[1:2, :])

    return pl.pallas_call(
        body, name="fwd_out", grid=(T // tm,),
        in_specs=[_rows(tm, ATTN_W), _rows(tm, POOL_W), _rows(tm, D_MODEL),
                  _resident((D_MODEL, D_MODEL)), _full((8, D_MODEL)), _full((8, D_MODEL))],
        out_specs=[_rows(tm, D_MODEL), _rows(tm, D_MODEL)],
        out_shape=[_sds((T, D_MODEL), F32), _sds((T, D_MODEL), F32)],
        compiler_params=_cp("parallel"),
    )(attn, pool, x, w_out, g8, mod8)


def _sh_rows(tm):
    return pl.BlockSpec((N_SHARD, tm, FF_SH), lambda i: (0, i, 0))


def _ffn_fwd(x1, mod8, g8, wg, wu, wd):
    T = x1.shape[0]
    tm = min(T, 256)

    def body(x_ref, mod_ref, g_ref, wg_ref, wu_ref, wd_ref, a_ref, b_ref, f_ref, x2_ref):
        xf = x_ref[...]
        h = (xf * _rstd(xf) * g_ref[2:3, :]) * (1.0 + mod_ref[4:5, :]) + mod_ref[3:4, :]
        hb = h.astype(BF16)
        f = jnp.zeros((tm, D_MODEL), F32)
        for s in range(N_SHARD):
            a = _mm(hb, wg_ref[s])
            b = _mm(hb, wu_ref[s])
            a_ref[s] = a.astype(BF16)
            b_ref[s] = b.astype(BF16)
            act = (a * jax.nn.sigmoid(a)) * b
            f = f + _mm(act.astype(BF16), wd_ref[s])
        f_ref[...] = f
        x2_ref[...] = xf + mod_ref[5:6, :] * (f * _rstd(f) * g_ref[3:4, :])

    act_shape = _sds((N_SHARD, T, FF_SH), BF16)
    return pl.pallas_call(
        body, name="ffn_fwd", grid=(T // tm,),
        in_specs=[_rows(tm, D_MODEL), _full((8, D_MODEL)), _full((8, D_MODEL)),
                  _resident((N_SHARD, D_MODEL, FF_SH)), _resident((N_SHARD, D_MODEL, FF_SH)),
                  _resident((N_SHARD, FF_SH, D_MODEL))],
        out_specs=[_sh_rows(tm), _sh_rows(tm), _rows(tm, D_MODEL), _rows(tm, D_MODEL)],
        out_shape=[act_shape, act_shape, _sds((T, D_MODEL), F32), _sds((T, D_MODEL), F32)],
        compiler_params=_cp("parallel"),
    )(x1, mod8, g8, wg, wu, wd)


def _loss_grad(y, target):
    T = y.shape[0]
    tm = min(T, 1024)

    def body(y_ref, t_ref, dy_ref, loss_ref):
        @pl.when(pl.program_id(0) == 0)
        def _():
            loss_ref[...] = jnp.zeros_like(loss_ref)

        e = y_ref[...] - t_ref[...]
        dy_ref[...] = e * (1.0 / D_MODEL)
        part = 0.5 * jnp.sum(jnp.mean(e * e, axis=-1, keepdims=True), axis=0, keepdims=True)
        loss_ref[...] += part

    return pl.pallas_call(
        body, name="loss_grad", grid=(T // tm,),
        in_specs=[_rows(tm, D_MODEL), _rows(tm, D_MODEL)],
        out_specs=[_rows(tm, D_MODEL), _full((8, 128))],
        out_shape=[_sds((T, D_MODEL), F32), _sds((8, 128), F32)],
        compiler_params=_cp("arbitrary"),
    )(y, target)


def _ffn_bwd_act(dx2, f, a, b, mod8, g8, wd):
    T = dx2.shape[0]
    tm = min(T, 256)

    def body(dx_ref, f_ref, a_ref, b_ref, mod_ref, g_ref, wd_ref,
             df_ref, da_ref, db_ref, act_ref, red_ref):
        @pl.when(pl.program_id(0) == 0)
        def _():
            red_ref[...] = jnp.zeros_like(red_ref)

        dx = dx_ref[...]
        fv = f_ref[...]
        rstd = _rstd(fv)
        fhat = fv * rstd
        gpost = g_ref[3:4, :]
        red_ref[0:1, :] += _colsum(dx * (fhat * gpost))
        dn = dx * mod_ref[5:6, :]
        red_ref[1:2, :] += _colsum(dn * fhat)
        dfb = _norm_bwd(dn * gpost, fhat, rstd).astype(BF16)
        df_ref[...] = dfb
        for s in range(N_SHARD):
            dact = _nt(dfb, wd_ref[s])
            av = a_ref[s].astype(F32)
            bv = b_ref[s].astype(F32)
            sig = jax.nn.sigmoid(av)
            sl = av * sig
            act_ref[s] = (sl * bv).astype(BF16)
            da_ref[s] = (dact * bv * (sig * (1.0 + av * (1.0 - sig)))).astype(BF16)
            db_ref[s] = (dact * sl).astype(BF16)

    act_shape = _sds((N_SHARD, T, FF_SH), BF16)
    return pl.pallas_call(
        body, name="ffn_bwd_act", grid=(T // tm,),
        in_specs=[_rows(tm, D_MODEL), _rows(tm, D_MODEL), _sh_rows(tm), _sh_rows(tm),
                  _full((8, D_MODEL)), _full((8, D_MODEL)), _resident((N_SHARD, FF_SH, D_MODEL))],
        out_specs=[_rows(tm, D_MODEL), _sh_rows(tm), _sh_rows(tm), _sh_rows(tm), _full((8, D_MODEL))],
        out_shape=[_sds((T, D_MODEL), BF16), act_shape, act_shape, act_shape, _sds((8, D_MODEL), F32)],
        compiler_params=_cp("arbitrary"),
    )(dx2, f, a, b, mod8, g8, wd)


def _ffn_bwd_in(da, db, x1, dx2, mod8, g8, wg, wu):
    T = x1.shape[0]
    tm = min(T, 256)

    def body(da_ref, db_ref, x_ref, dx_ref, mod_ref, g_ref, wg_ref, wu_ref, dx1_ref, h2_ref, red_ref):
        @pl.when(pl.program_id(0) == 0)
        def _():
            red_ref[...] = jnp.zeros_like(red_ref)

        dh = jnp.zeros((tm, D_MODEL), F32)
        for s in range(N_SHARD):
            dh = dh + _nt(da_ref[s], wg_ref[s]) + _nt(db_ref[s], wu_ref[s])
        xf = x_ref[...]
        rstd = _rstd(xf)
        xhat = xf * rstd
        gpre = g_ref[2:3, :]
        scale1 = 1.0 + mod_ref[4:5, :]
        h2_ref[...] = ((xhat * gpre) * scale1 + mod_ref[3:4, :]).astype(BF16)
        red_ref[0:1, :] += _colsum(dh)
        red_ref[1:2, :] += _colsum(dh * (xhat * gpre))
        red_ref[2:3, :] += _colsum(dh * scale1 * xhat)
        dx1_ref[...] = dx_ref[...] + _norm_bwd(dh * scale1 * gpre, xhat, rstd)

    return pl.pallas_call(
        body, name="ffn_bwd_in", grid=(T // tm,),
        in_specs=[_sh_rows(tm), _sh_rows(tm), _rows(tm, D_MODEL), _rows(tm, D_MODEL),
                  _full((8, D_MODEL)), _full((8, D_MODEL)),
                  _resident((N_SHARD, D_MODEL, FF_SH)), _resident((N_SHARD, D_MODEL, FF_SH))],
        out_specs=[_rows(tm, D_MODEL), _rows(tm, D_MODEL), _full((8, D_MODEL))],
        out_shape=[_sds((T, D_MODEL), F32), _sds((T, D_MODEL), BF16), _sds((8, D_MODEL), F32)],
        compiler_params=_cp("arbitrary"),
    )(da, db, x1, dx2, mod8, g8, wg, wu)


def _wgrad(a, b, name):
    T, K = a.shape
    N = b.shape[1]
    tt = min(T, 512)
    tk = min(K, 512)

    def body(a_ref, b_ref, o_ref):
        @pl.when(pl.program_id(1) == 0)
        def _():
            o_ref[...] = jnp.zeros_like(o_ref)

        o_ref[...] += _tn(a_ref[...], b_ref[...])

    return pl.pallas_call(
        body, name=name, grid=(K // tk, T // tt),
        in_specs=[pl.BlockSpec((tt, tk), lambda i, t: (t, i)), pl.BlockSpec((tt, N), lambda i, t: (t, 0))],
        out_specs=pl.BlockSpec((tk, N), lambda i, t: (i, 0)),
        out_shape=_sds((K, N), F32),
        compiler_params=_cp("parallel", "arbitrary"),
    )(a, b)


def _wgrad_cols(a, b, name):
    T, K = a.shape
    n = b.shape[2]
    tt = min(T, 512)

    def body(a_ref, b_ref, o_ref):
        @pl.when(pl.program_id(1) == 0)
        def _():
            o_ref[...] = jnp.zeros_like(o_ref)

        o_ref[...] += _tn(a_ref[...], b_ref[...])

    return pl.pallas_call(
        body, name=name, grid=(N_SHARD, T // tt),
        in_specs=[pl.BlockSpec((tt, K), lambda s, t: (t, 0)), pl.BlockSpec((None, tt, n), lambda s, t: (s, t, 0))],
        out_specs=pl.BlockSpec((None, K, n), lambda s, t: (s, 0, 0)),
        out_shape=_sds((N_SHARD, K, n), F32),
        compiler_params=_cp("parallel", "arbitrary"),
    )(a, b)


def _wgrad_rows(a, b, name):
    T, N = b.shape
    k = a.shape[2]
    tt = min(T, 512)

    def body(a_ref, b_ref, o_ref):
        @pl.when(pl.program_id(1) == 0)
        def _():
            o_ref[...] = jnp.zeros_like(o_ref)

        o_ref[...] += _tn(a_ref[...], b_ref[...])

    return pl.pallas_call(
        body, name=name, grid=(N_SHARD, T // tt),
        in_specs=[pl.BlockSpec((None, tt, k), lambda s, t: (s, t, 0)), pl.BlockSpec((tt, N), lambda s, t: (t, 0))],
        out_specs=pl.BlockSpec((None, k, N), lambda s, t: (s, 0, 0)),
        out_shape=_sds((N_SHARD, k, N), F32),
        compiler_params=_cp("parallel", "arbitrary"),
    )(a, b)


def _mix_bwd(dx1, mix, mod8, g8, w_out):
    T = dx1.shape[0]
    tm = min(T, 512)

    def body(dx_ref, mix_ref, mod_ref, g_ref, w_ref, dmix_ref, da_ref, dp_ref, red_ref):
        @pl.when(pl.program_id(0) == 0)
        def _():
            red_ref[...] = jnp.zeros_like(red_ref)

        dx = dx_ref[...]
        mv = mix_ref[...]
        rstd = _rstd(mv)
        mhat = mv * rstd
        gpost = g_ref[1:2, :]
        red_ref[0:1, :] += _colsum(dx * (mhat * gpost))
        dn = dx * mod_ref[2:3, :]
        red_ref[1:2, :] += _colsum(dn * mhat)
        dmb = _norm_bwd(dn * gpost, mhat, rstd).astype(BF16)
        dmix_ref[...] = dmb
        da_ref[...] = _nt(dmb, w_ref[0:ATTN_W, :]).astype(BF16)
        dp_ref[...] = _nt(dmb, w_ref[ATTN_W:, :]).astype(BF16)

    return pl.pallas_call(
        body, name="mix_bwd", grid=(T // tm,),
        in_specs=[_rows(tm, D_MODEL), _rows(tm, D_MODEL), _full((8, D_MODEL)), _full((8, D_MODEL)),
                  _resident((D_MODEL, D_MODEL))],
        out_specs=[_rows(tm, D_MODEL), _rows(tm, ATTN_W), _rows(tm, POOL_W), _full((8, D_MODEL))],
        out_shape=[_sds((T, D_MODEL), BF16), _sds((T, ATTN_W), BF16), _sds((T, POOL_W), BF16),
                   _sds((8, D_MODEL), F32)],
        compiler_params=_cp("arbitrary"),
    )(dx1, mix, mod8, g8, w_out)


def _attn_bwd(q, kd, vd, lse, dattn, sink_b):
    T = q.shape[0]
    nb = T // BLK

    def body(q_ref, do_ref, lse_ref, kp_ref, kc_ref, vp_ref, vc_ref, sk_ref,
             dq_ref, dk_ref, dv_ref, dsk_ref, carry_k, carry_v):
        n = pl.program_id(0)

        @pl.when(n == 0)
        def _():
            carry_k[...] = jnp.zeros_like(carry_k)
            carry_v[...] = jnp.zeros_like(carry_v)
            dsk_ref[...] = jnp.zeros_like(dsk_ref)

        @pl.when(n < nb)
        def _():
            valid = jnp.concatenate([_band_mask(n)] * GROUP, axis=0)
            lane = lax.broadcasted_iota(jnp.int32, (BLK, 128), 1)
            lse_all = lse_ref[...]
            for j in range(N_HEADS // GROUP):
                lanes = slice(j * 128, (j + 1) * 128)
                kcat = jnp.concatenate([kp_ref[:, lanes], kc_ref[:, lanes]], axis=0)
                vcat = jnp.concatenate([vp_ref[:, lanes], vc_ref[:, lanes]], axis=0)
                qs = _stack_heads(q_ref, j)
                dos = _stack_heads(do_ref, j)
                lse = jnp.concatenate(
                    [jnp.sum(jnp.where(lane == GROUP * j + r, lse_all, 0.0), axis=-1, keepdims=True)
                     for r in range(GROUP)], axis=0)
                p = jnp.exp(jnp.where(valid, _nt(qs, kcat), NEG_INF) - lse)
                dp = _nt(dos, vcat)
                delta = jnp.sum(p * dp, axis=-1, keepdims=True)
                ds = (p * (dp - delta)).astype(BF16)
                sink_term = jnp.exp(_sink_rows(sk_ref, j) - lse) * delta
                for r in range(GROUP):
                    h = GROUP * j + r
                    dsk_ref[h:h + 1, :] += -jnp.sum(sink_term[r * BLK:(r + 1) * BLK], axis=0, keepdims=True)
                dq_ref[:, 2 * j * 128:(2 * j + 2) * 128] = jnp.concatenate(_unstack_heads(_mm(ds, kcat)), axis=1)
                dk = _tn(ds, qs)
                dv = _tn(p.astype(BF16), dos)
                dk_ref[:, lanes] = carry_k[:, lanes] + dk[0:BLK]
                dv_ref[:, lanes] = carry_v[:, lanes] + dv[0:BLK]
                carry_k[:, lanes] = dk[BLK:]
                carry_v[:, lanes] = dv[BLK:]

        @pl.when(n == nb)
        def _():
            dk_ref[...] = carry_k[...]
            dv_ref[...] = carry_v[...]

    cur = lambda n: (jnp.minimum(n, nb - 1), 0)
    prev = lambda n: (jnp.maximum(n - 1, 0), 0)
    return pl.pallas_call(
        body, name="attn_bwd", grid=(nb + 1,),
        in_specs=[pl.BlockSpec((BLK, ATTN_W), cur), pl.BlockSpec((BLK, ATTN_W), cur), pl.BlockSpec((BLK, 128), cur),
                  pl.BlockSpec((BLK, KVD_W), prev), pl.BlockSpec((BLK, KVD_W), cur),
                  pl.BlockSpec((BLK, KVD_W), prev), pl.BlockSpec((BLK, KVD_W), cur),
                  _full((8, 128))],
        out_specs=[pl.BlockSpec((BLK, ATTN_W), cur), pl.BlockSpec((BLK, KVD_W), prev),
                   pl.BlockSpec((BLK, KVD_W), prev), _full((8, 128))],
        out_shape=[_sds((T, ATTN_W), F32), _sds((T, KVD_W), F32), _sds((T, KVD_W), F32), _sds((8, 128), F32)],
        scratch_shapes=[pltpu.VMEM((BLK, KVD_W), F32), pltpu.VMEM((BLK, KVD_W), F32)],
        compiler_params=_cp("arbitrary"),
    )(q, dattn, lse, kd, kd, vd, vd, sink_b)


def _pool_bwd(dpool, pooled, pool_w, pool_scale):
    T = dpool.shape[0]
    tm = min(T, 512)
    nbk = T // tm
    ext_rows = tm + HALO

    def body(dp_ref, pl_ref, w_ref, sc_ref, du_ref, dw_ref, dsc_ref, halo):
        i = pl.program_id(0)

        @pl.when(i == 0)
        def _():
            halo[...] = jnp.zeros_like(halo)
            dw_ref[...] = jnp.zeros_like(dw_ref)
            dsc_ref[...] = jnp.zeros_like(dsc_ref)

        blk = nbk - 1 - i
        tpos = (blk * tm + lax.broadcasted_iota(jnp.int32, (tm, 1), 0)).astype(F32)
        for g, w in enumerate(POOL_WINDOWS):
            lanes = slice(g * 128, (g + 1) * 128)
            dp = dp_ref[:, lanes].astype(F32)
            pb = pl_ref[:, lanes]
            wg = w_ref[g].astype(BF16)
            z = _mm(pb, wg)
            dsc_ref[0:1, lanes] += _colsum(dp * z)
            dz = (dp * sc_ref[:, lanes]).astype(BF16)
            dw_ref[g] += _tn(pb, dz)
            dpl = _nt(dz, wg)
            e = dpl / jnp.minimum(tpos + 1.0, float(w))
            s = jnp.concatenate([e, halo[:, lanes]], axis=0)
            halo[:, lanes] = e[0:HALO, :]
            sh = 1
            while sh < w:
                s = s + pltpu.roll(s, ext_rows - sh, 0)
                sh *= 2
            du_ref[:, lanes] = s[0:tm, :] - dpl

    rev = lambda i: (nbk - 1 - i, 0)
    return pl.pallas_call(
        body, name="pool_bwd", grid=(nbk,),
        in_specs=[pl.BlockSpec((tm, POOL_W), rev), pl.BlockSpec((tm, POOL_W), rev),
                  _full((4, 128, 128)), _full((1, POOL_W))],
        out_specs=[pl.BlockSpec((tm, POOL_W), rev), _full((4, 128, 128)), _full((8, POOL_W))],
        out_shape=[_sds((T, POOL_W), F32), _sds((4, 128, 128), F32), _sds((8, POOL_W), F32)],
        scratch_shapes=[pltpu.VMEM((HALO, POOL_W), F32)],
        compiler_params=_cp("arbitrary"),
    )(dpool, pooled, pool_w, pool_scale)


def _in_bwd(dq, dk, dv, du, rc, rs1, rs2, x, dx1, mod8, g8, w_in):
    T = x.shape[0]
    tm = min(T, 512)

    def body(dq_ref, dk_ref, dv_ref, du_ref, c_ref, s1_ref, s2_ref, x_ref, dx1_ref, mod_ref, g_ref, w_ref,
             dx_ref, dproj_ref, red_ref, dbin_ref):
        @pl.when(pl.program_id(0) == 0)
        def _():
            red_ref[...] = jnp.zeros_like(red_ref)
            dbin_ref[...] = jnp.zeros_like(dbin_ref)

        c = c_ref[...]
        s1 = s1_ref[...]
        s2 = s2_ref[...]
        dqp = _rot_bwd(dq_ref[...] * (HEAD ** -0.5), jnp.tile(c, (1, 4)), jnp.tile(s1, (1, 4)), jnp.tile(s2, (1, 4)))
        dkp = _rot_bwd(_fold_dup(dk_ref[...]), c, s1, s2)
        pieces = ((0, ATTN_W, dqp), (ATTN_W, ATTN_W + KV_W, dkp),
                  (ATTN_W + KV_W, ATTN_W + 2 * KV_W, _fold_dup(dv_ref[...])), (ATTN_W + 2 * KV_W, IN_W, du_ref[...]))
        dh = jnp.zeros((tm, D_MODEL), F32)
        for lo, hi, val in pieces:
            dbin_ref[0:1, lo:hi] += _colsum(val)
            vb = val.astype(BF16)
            dproj_ref[:, lo:hi] = vb
            dh = dh + _nt(vb, w_ref[:, lo:hi])
        xf = x_ref[...]
        rstd = _rstd(xf)
        xhat = xf * rstd
        gpre = g_ref[0:1, :]
        scale1 = 1.0 + mod_ref[1:2, :]
        red_ref[0:1, :] += _colsum(dh)
        red_ref[1:2, :] += _colsum(dh * (xhat * gpre))
        red_ref[2:3, :] += _colsum(dh * scale1 * xhat)
        dx_ref[...] = dx1_ref[...] + _norm_bwd(dh * scale1 * gpre, xhat, rstd)

    return pl.pallas_call(
        body, name="in_bwd", grid=(T // tm,),
        in_specs=[_rows(tm, ATTN_W), _rows(tm, KVD_W), _rows(tm, KVD_W), _rows(tm, POOL_W),
                  _rows(tm, 128), _rows(tm, 128), _rows(tm, 128), _rows(tm, D_MODEL), _rows(tm, D_MODEL),
                  _full((8, D_MODEL)), _full((8, D_MODEL)), _resident((D_MODEL, IN_W))],
        out_specs=[_rows(tm, D_MODEL), _rows(tm, IN_W), _full((8, D_MODEL)), _full((8, IN_W))],
        out_shape=[_sds((T, D_MODEL), F32), _sds((T, IN_W), BF16), _sds((8, D_MODEL), F32), _sds((8, IN_W), F32)],
        compiler_params=_cp("arbitrary"),
    )(dq, dk, dv, du, rc, rs1, rs2, x, dx1, mod8, g8, w_in)


def _mod_fwd(c_all, ada_w, ada_b_sh):
    tn = 512

    def body(c_ref, w_ref, b_ref, o_ref):
        cv = c_ref[...]
        ca = (cv * jax.nn.sigmoid(cv)).astype(BF16)
        o_ref[...] = _mm(ca, w_ref[...].astype(BF16)) + b_ref[...]

    return pl.pallas_call(
        body, name="mod_fwd", grid=(2, ADA_SH // tn),
        in_specs=[_full((8, D_MODEL)), pl.BlockSpec((None, D_MODEL, tn), lambda l, j: (l, 0, j)),
                  pl.BlockSpec((None, 1, tn), lambda l, j: (l, 0, j))],
        out_specs=pl.BlockSpec((None, 8, tn), lambda l, j: (l, 0, j)),
        out_shape=_sds((2, 8, ADA_SH), F32),
        compiler_params=_cp("parallel", "parallel"),
    )(c_all, ada_w, ada_b_sh)


def _ada_wgrad(c_all_t, dmod_sh):
    tn = 512

    def body(c_ref, d_ref, o_ref):
        cv = c_ref[...]
        ca = cv * jax.nn.sigmoid(cv)
        o_ref[...] = jnp.dot(ca, d_ref[...], preferred_element_type=F32, precision=lax.Precision.HIGHEST)

    return pl.pallas_call(
        body, name="ada_wgrad", grid=(2, ADA_SH // tn),
        in_specs=[_full((D_MODEL, 8)), pl.BlockSpec((None, 8, tn), lambda l, j: (l, 0, j))],
        out_specs=pl.BlockSpec((None, D_MODEL, tn), lambda l, j: (l, 0, j)),
        out_shape=_sds((2, D_MODEL, ADA_SH), F32),
        compiler_params=_cp("parallel", "parallel"),
    )(c_all_t, dmod_sh)


def _sum_devices(g):
    R = g.shape[1]

    def body(g_ref, o_ref):
        acc = g_ref[0]
        for d in range(1, N_DEV):
            acc = acc + g_ref[d]
        o_ref[...] = acc

    return pl.pallas_call(
        body, name="sum_devices", grid=(1,),
        in_specs=[_full((N_DEV, R, 128))], out_specs=_full((R, 128)), out_shape=_sds((R, 128), F32),
        compiler_params=_cp("arbitrary"),
    )(g)


def _adamw(w, g, m, v, name):
    R, C = w.shape
    tr = R
    for cand in (256, 128, 64, 32, 16, 8):
        if R % cand == 0 and cand * C * 4 <= 2 * 1024 * 1024:
            tr = cand
            break

    def body(w_ref, g_ref, m_ref, v_ref, d_ref, nm_ref, nv_ref):
        gv = g_ref[...]
        mn = ADAM_B1 * m_ref[...] + (1.0 - ADAM_B1) * gv
        vn = ADAM_B2 * v_ref[...] + (1.0 - ADAM_B2) * (gv * gv)
        m_hat = mn / (1.0 - ADAM_B1 ** ADAM_STEP)
        v_hat = vn / (1.0 - ADAM_B2 ** ADAM_STEP)
        d_ref[...] = -ADAM_LR * (m_hat / (jnp.sqrt(v_hat) + ADAM_EPS) + ADAM_WD * w_ref[...])
        nm_ref[...] = mn
        nv_ref[...] = vn

    spec = pl.BlockSpec((tr, C), lambda i: (i, 0))
    out = _sds((R, C), F32)
    return pl.pallas_call(
        body, name=name, grid=(R // tr,),
        in_specs=[spec] * 4, out_specs=[spec] * 3, out_shape=[out] * 3,
        compiler_params=_cp("parallel"),
    )(w, g, m, v)


def _adamw_nd(w, g, m, v, name):
    shape = w.shape
    if w.ndim == 2 and shape[1] < 128:
        view = (1, shape[0] * shape[1])
    else:
        view = (-1, shape[-1])
    outs = _adamw(*[t.reshape(view) for t in (w, g, m, v)], name=name)
    return [o.reshape(shape) for o in outs]


def _coords():
    return lax.axis_index("x"), lax.axis_index("y"), lax.axis_index("c")


def _other_chips(x, y):
    return [(1 - x, y), (x, 1 - y), (1 - x, 1 - y)]


def _allgather8(blk, name):
    m_per, n = blk.shape

    def body(x_ref, out_ref, send_sems, recv_sems, local_sem):
        x, y, c = _coords()
        me, sibling = (x, y, c), (x, y, 1 - c)
        chips = _other_chips(x, y)

        def rows(px, py, pc):
            return out_ref.at[pl.ds((4 * px + 2 * py + pc) * m_per, m_per), :]

        def copy(k, block, to, src=None):
            return pltpu.make_async_remote_copy(
                src_ref=rows(*block) if src is None else src, dst_ref=rows(*block),
                send_sem=send_sems.at[k], recv_sem=recv_sems.at[k], device_id=to, device_id_type=MESH)

        mine = pltpu.make_async_copy(x_ref, rows(*me), local_sem)
        mine.start()
        first = [copy(0, me, sibling, src=x_ref)]
        first += [copy(1 + j, me, (*chip, c), src=x_ref) for j, chip in enumerate(chips)]
        for cp in first:
            cp.start()
        passed = [copy(4 + j, (*chip, c), sibling) for j, chip in enumerate(chips)]
        for j, chip in enumerate(chips):
            copy(1 + j, (*chip, c), me).wait_recv()
            passed[j].start()
        copy(0, sibling, me).wait_recv()
        for j, chip in enumerate(chips):
            copy(4 + j, (*chip, 1 - c), me).wait_recv()
        for cp in first + passed:
            cp.wait_send()
        mine.wait()

    return pl.pallas_call(
        body, name=name,
        out_shape=_sds((N_DEV * m_per, n), blk.dtype),
        in_specs=[pl.BlockSpec(memory_space=pltpu.VMEM)],
        out_specs=pl.BlockSpec(memory_space=pltpu.VMEM),
        scratch_shapes=[pltpu.SemaphoreType.DMA((7,)), pltpu.SemaphoreType.DMA((7,)), pltpu.SemaphoreType.DMA],
        compiler_params=pltpu.CompilerParams(vmem_limit_bytes=VMEM_LIMIT),
    )(blk)


def _row_tile(r, n):
    for cand in (512, 256, 128, 64, 32, 16):
        if r % cand == 0 and cand * n * 4 <= 2 * 1024 * 1024:
            return cand
    return r


def _cast_slot(w, chip, name):
    r, n = w.shape
    tr = _row_tile(r, n)

    def body(chip_ref, w_ref, o_ref):
        o_ref[...] = w_ref[...].astype(BF16)

    grid_spec = pltpu.PrefetchScalarGridSpec(
        num_scalar_prefetch=1, grid=(r // tr,),
        in_specs=[pl.BlockSpec((tr, n), lambda i, ch: (i, 0))],
        out_specs=pl.BlockSpec((None, tr, n), lambda i, ch: (ch[0], i, 0)))
    return pl.pallas_call(
        body, name=name, grid_spec=grid_spec, out_shape=_sds((N_SHARD, r, n), BF16),
        compiler_params=_cp("arbitrary"),
    )(chip, w)


def _allgather_weights(bufs, name):
    nt = len(bufs)
    hom = [pl.BlockSpec(memory_space=pl.ANY)] * nt

    def body(*refs):
        outs = refs[nt:2 * nt]
        send_sems, recv_sems = refs[2 * nt:]
        x, y, c = _coords()
        sibling = (x, y, 1 - c)
        chips = _other_chips(x, y)

        def copy(t, k, block_chip, hc, to):
            r = outs[t].shape[1] // 2
            blk = outs[t].at[2 * block_chip[0] + block_chip[1], pl.ds(hc * r, r)]
            return pltpu.make_async_remote_copy(
                src_ref=blk, dst_ref=blk,
                send_sem=send_sems.at[t, k], recv_sem=recv_sems.at[t, k], device_id=to, device_id_type=MESH)

        started = []
        for t in range(nt):
            for j, chip in enumerate(chips):
                cp = copy(t, j, (x, y), c, (*chip, c))
                cp.start()
                started.append(cp)
        for t in range(nt):
            for j, chip in enumerate(chips):
                copy(t, j, chip, c, sibling).wait_recv()
                fw = copy(t, 3 + j, chip, c, sibling)
                fw.start()
                started.append(fw)
        for t in range(nt):
            for j, chip in enumerate(chips):
                copy(t, 3 + j, chip, 1 - c, sibling).wait_recv()
        for cp in started:
            cp.wait_send()

    return pl.pallas_call(
        body, name=name,
        out_shape=[_sds(b.shape, b.dtype) for b in bufs],
        in_specs=hom, out_specs=hom,
        input_output_aliases={t: t for t in range(nt)},
        scratch_shapes=[pltpu.SemaphoreType.DMA((nt, 6)), pltpu.SemaphoreType.DMA((nt, 6))],
    )(*bufs)


def _swap_halves(grads, name):
    nt = len(grads)
    hom = [pl.BlockSpec(memory_space=pl.ANY)] * nt

    def body(*refs):
        ins = refs[:nt]
        outs = refs[nt:2 * nt]
        send_sems, recv_sems = refs[2 * nt:]
        x, y, c = _coords()
        sibling = (x, y, 1 - c)
        cps = []
        for t in range(nt):
            r = ins[t].shape[1] // 2
            cp = pltpu.make_async_remote_copy(
                src_ref=ins[t].at[:, pl.ds((1 - c) * r, r)], dst_ref=outs[t],
                send_sem=send_sems.at[t], recv_sem=recv_sems.at[t], device_id=sibling, device_id_type=MESH)
            cp.start()
            cps.append(cp)
        for cp in cps:
            cp.wait()

    return pl.pallas_call(
        body, name=name,
        out_shape=[_sds((N_SHARD, g.shape[1] // 2, g.shape[2]), g.dtype) for g in grads],
        in_specs=hom, out_specs=hom,
        scratch_shapes=[pltpu.SemaphoreType.DMA((nt,)), pltpu.SemaphoreType.DMA((nt,))],
    )(*grads)


def _scatter_chips(sums, name):
    nt = len(sums)
    hom = [pl.BlockSpec(memory_space=pl.ANY)] * nt

    def body(*refs):
        ins = refs[:nt]
        outs = refs[nt:2 * nt]
        send_sems, recv_sems = refs[2 * nt:]
        x, y, c = _coords()
        chips = _other_chips(x, y)
        cps = []
        for t in range(nt):
            for j, chip in enumerate(chips):
                cp = pltpu.make_async_remote_copy(
                    src_ref=ins[t].at[2 * chip[0] + chip[1]], dst_ref=outs[t].at[j],
                    send_sem=send_sems.at[t, j], recv_sem=recv_sems.at[t, j],
                    device_id=(*chip, c), device_id_type=MESH)
                cp.start()
                cps.append(cp)
        for cp in cps:
            cp.wait()

    return pl.pallas_call(
        body, name=name,
        out_shape=[_sds((3,) + s.shape[1:], s.dtype) for s in sums],
        in_specs=hom, out_specs=hom,
        scratch_shapes=[pltpu.SemaphoreType.DMA((nt, 3)), pltpu.SemaphoreType.DMA((nt, 3))],
    )(*sums)


def _join_halves(tots, name):
    nt = len(tots)
    hom = [pl.BlockSpec(memory_space=pl.ANY)] * nt

    def body(*refs):
        outs = refs[nt:2 * nt]
        send_sems, recv_sems = refs[2 * nt:]
        x, y, c = _coords()
        sibling = (x, y, 1 - c)
        cps = []
        for t in range(nt):
            cp = pltpu.make_async_remote_copy(
                src_ref=outs[t].at[c], dst_ref=outs[t].at[c],
                send_sem=send_sems.at[t], recv_sem=recv_sems.at[t], device_id=sibling, device_id_type=MESH)
            cp.start()
            cps.append(cp)
        for t in range(nt):
            pltpu.make_async_remote_copy(
                src_ref=outs[t].at[c], dst_ref=outs[t].at[1 - c],
                send_sem=send_sems.at[t], recv_sem=recv_sems.at[t], device_id=sibling, device_id_type=MESH).wait_recv()
        for cp in cps:
            cp.wait_send()

    return pl.pallas_call(
        body, name=name,
        out_shape=[_sds(t.shape, t.dtype) for t in tots],
        in_specs=hom, out_specs=hom,
        input_output_aliases={t: t for t in range(nt)},
        scratch_shapes=[pltpu.SemaphoreType.DMA((nt,)), pltpu.SemaphoreType.DMA((nt,))],
    )(*tots)


def _pair_sum(g, recv, core, chip, name):
    _, _, r, n = g.shape
    tr = _row_tile(r, n)

    def body(core_ref, chip_ref, g_ref, r_ref, sb_ref, own_ref):
        tot = g_ref[...] + r_ref[...]
        sb_ref[...] = tot.astype(BF16)

        @pl.when(pl.program_id(1) == chip_ref[0])
        def _():
            own_ref[...] = tot

    grid_spec = pltpu.PrefetchScalarGridSpec(
        num_scalar_prefetch=2, grid=(r // tr, N_SHARD),
        in_specs=[pl.BlockSpec((None, None, tr, n), lambda i, s, co, ch: (s, co[0], i, 0)),
                  pl.BlockSpec((None, tr, n), lambda i, s, co, ch: (s, i, 0))],
        out_specs=[pl.BlockSpec((None, tr, n), lambda i, s, co, ch: (s, i, 0)),
                   pl.BlockSpec((tr, n), lambda i, s, co, ch: (i, 0))])
    return pl.pallas_call(
        body, name=name, grid_spec=grid_spec,
        out_shape=[_sds((N_SHARD, r, n), BF16), _sds((r, n), F32)],
        compiler_params=_cp("arbitrary", "arbitrary"),
    )(core, chip, g, recv)


def _chip_sum(own, recv, core, name):
    r, n = own.shape
    tr = _row_tile(r, n)

    def body(core_ref, o_ref, r_ref, t_ref):
        acc = o_ref[...]
        for j in range(3):
            acc = acc + r_ref[j].astype(F32)
        t_ref[...] = acc

    grid_spec = pltpu.PrefetchScalarGridSpec(
        num_scalar_prefetch=1, grid=(r // tr,),
        in_specs=[pl.BlockSpec((tr, n), lambda i, co: (i, 0)), pl.BlockSpec((3, tr, n), lambda i, co: (0, i, 0))],
        out_specs=pl.BlockSpec((None, tr, n), lambda i, co: (co[0], i, 0)))
    return pl.pallas_call(
        body, name=name, grid_spec=grid_spec, out_shape=_sds((2, r, n), F32),
        compiler_params=_cp("arbitrary"),
    )(core, own, recv)


def _reduce_scatter(grads, tag):
    x, y, c = _coords()
    core = jnp.reshape(c, (1,)).astype(jnp.int32)
    chip = jnp.reshape(2 * x + y, (1,)).astype(jnp.int32)
    recv = _swap_halves(grads, name="rs_swap_" + tag)
    sums, owns = [], []
    for t, (g, rv) in enumerate(zip(grads, recv)):
        r = g.shape[1] // 2
        sb, own = _pair_sum(g.reshape(N_SHARD, 2, r, g.shape[2]), rv, core, chip, name=f"rs_pair_{tag}_{t}")
        sums.append(sb)
        owns.append(own)
    got = _scatter_chips(sums, name="rs_scatter_" + tag)
    tots = [_chip_sum(o, gt, core, name=f"rs_chip_{tag}_{t}") for t, (o, gt) in enumerate(zip(owns, got))]
    full = _join_halves(tots, name="rs_join_" + tag)
    return [f.reshape(2 * f.shape[1], f.shape[2]) for f in full]


def _rope_lane_table():
    d = jnp.arange(128) % HEAD
    inv_freq = ROPE_THETA ** (-jnp.arange(0, ROT, 2, dtype=F32) / ROT)
    rot = d < ROT
    rows = [jnp.where(rot, inv_freq[d % (ROT // 2)], 0.0), rot.astype(F32),
            (d < ROT // 2).astype(F32), jnp.logical_and(d >= ROT // 2, rot).astype(F32)]
    return jnp.concatenate([jnp.stack(rows), jnp.zeros((4, 128), F32)], axis=0)


def _pad8(rows):
    return jnp.concatenate([rows, jnp.zeros((8 - rows.shape[0], rows.shape[1]), F32)], axis=0)


def kernel(x, c, positions, ada_w, ada_b, w_in, b_in, sinks, pool_w, pool_scale, w_out, w_gate, w_up, w_down, g_pre_mix, g_post_mix, g_pre_ffn, g_post_ffn, loss_target, m_ada_w, m_ada_b, m_w_in, m_b_in, m_sinks, m_pool_w, m_pool_scale, m_w_out, m_w_gate, m_w_up, m_w_down, m_g_pre_mix, m_g_post_mix, m_g_pre_ffn, m_g_post_ffn, v_ada_w, v_ada_b, v_w_in, v_b_in, v_sinks, v_pool_w, v_pool_scale, v_w_out, v_w_gate, v_w_up, v_w_down, v_g_pre_mix, v_g_post_mix, v_g_pre_ffn, v_g_post_ffn):
    T = x.shape[1]
    n_layers = ada_w.shape[0]
    ax, ay, ac = _coords()
    my_dev = 4 * ax + 2 * ay + ac
    my_chip = 2 * ax + ay
    x0 = x.reshape(T, D_MODEL)
    target = loss_target.reshape(T, D_MODEL)

    c_all = _allgather8(c.reshape(8, 128), name="ag_c").reshape(N_DEV, D_MODEL)
    ada_b_sh = lax.dynamic_slice_in_dim(ada_b, my_chip * ADA_SH, ADA_SH, axis=1).reshape(n_layers, 1, ADA_SH)
    mod_part = _mod_fwd(c_all, ada_w, ada_b_sh)
    mod_all = _allgather8(mod_part.reshape(n_layers * 8, ADA_SH), name="ag_mod")
    mod_all = mod_all.reshape(N_DEV, n_layers, 8, ADA_SH)[0::2]
    mod_mine = lax.dynamic_index_in_dim(mod_all, my_dev, axis=2, keepdims=False)
    mod = jnp.transpose(mod_mine, (1, 0, 2)).reshape(n_layers, 6, D_MODEL)

    pos_b = jnp.broadcast_to(positions.reshape(T, 1), (T, 128))
    rc, rs1, rs2 = _rope_tables(pos_b, _rope_lane_table())

    chip1 = jnp.reshape(my_chip, (1,)).astype(jnp.int32)

    def gather_layer(l):
        bufs = [_cast_slot(w[l], chip1, name=f"cast_{nm}{l}")
                for nm, w in (("w_in", w_in), ("w_out", w_out), ("w_gate", w_gate), ("w_up", w_up), ("w_down", w_down))]
        gin, gout, gg, gu, gd = _allgather_weights(bufs, name=f"ag_w{l}")
        win_full = jnp.transpose(gin, (1, 0, 2)).reshape(D_MODEL, IN_W)
        return win_full, gout.reshape(D_MODEL, D_MODEL), gg, gu, gd

    weights = [gather_layer(l) for l in range(n_layers)]

    saved = []
    xl = x0
    for l in range(n_layers):
        win, wout, wg, wu, wd = weights[l]
        mod8 = _pad8(mod[l])
        g8 = _pad8(jnp.stack([g_pre_mix[l], g_post_mix[l], g_pre_ffn[l], g_post_ffn[l]]))
        sink_b = jnp.broadcast_to(sinks[l][:, None], (N_HEADS, 128))
        psc = pool_scale[l].reshape(1, POOL_W)
        h, q, k, v, u = _fwd_in(xl, mod8, g8, win, b_in[l].reshape(1, IN_W), rc, rs1, rs2)
        attn, lse = _attn_fwd(q, k, v, sink_b)
        pool, pooled = _pool_fwd(u, pool_w[l], psc)
        mix, x1 = _fwd_out(attn, pool, xl, wout, g8, mod8)
        a, b, f, x2 = _ffn_fwd(x1, mod8, g8, wg, wu, wd)
        saved.append(dict(x=xl, h=h, q=q, k=k, v=v, lse=lse, attn=attn, pool=pool, pooled=pooled, mix=mix,
                          x1=x1, a=a, b=b, f=f, mod8=mod8, g8=g8, sink_b=sink_b, psc=psc))
        xl = x2

    dy, loss_tile = _loss_grad(xl, target)
    loss = lax.psum(loss_tile[0, 0], ("x", "y", "c"))

    big_grads = [None] * n_layers
    small = [None] * n_layers
    dmod_rows = [None] * n_layers
    dx = dy
    for l in reversed(range(n_layers)):
        s = saved[l]
        win, wout, wg, wu, wd = weights[l]
        df, da, db, act, red_a = _ffn_bwd_act(dx, s["f"], s["a"], s["b"], s["mod8"], s["g8"], wd)
        dx1, h2, red_b = _ffn_bwd_in(da, db, s["x1"], dx, s["mod8"], s["g8"], wg, wu)
        g_wd = _wgrad_rows(act, df, name="wgrad_down")
        g_wg = _wgrad_cols(h2, da, name="wgrad_gate")
        g_wu = _wgrad_cols(h2, db, name="wgrad_up")
        dmix, dattn, dpool, red_c = _mix_bwd(dx1, s["mix"], s["mod8"], s["g8"], wout)
        g_wout = jnp.concatenate([_wgrad(s["attn"], dmix, name="wgrad_out_a"),
                                  _wgrad(s["pool"], dmix, name="wgrad_out_p")], axis=0)
        dq, dk, dv, dsink = _attn_bwd(s["q"], s["k"], s["v"], s["lse"], dattn, s["sink_b"])
        du, g_poolw, dpsc = _pool_bwd(dpool, s["pooled"], pool_w[l], s["psc"])
        dx, dproj, red_d, dbin = _in_bwd(dq, dk, dv, du, rc, rs1, rs2, s["x"], dx1, s["mod8"], s["g8"], win)
        g_win = _wgrad(s["h"], dproj, name="wgrad_in")
        g_win_sh = jnp.transpose(g_win.reshape(D_MODEL, N_SHARD, IN_SH), (1, 0, 2))
        big_grads[l] = [g_win_sh, g_wout.reshape(N_SHARD, OUT_SH, D_MODEL), g_wg, g_wu, g_wd]
        dmod_rows[l] = jnp.concatenate([red_d[0], red_d[1], red_c[0], red_b[0], red_b[1], red_a[0]])
        small[l] = jnp.concatenate([red_d[2], red_c[1], red_b[2], red_a[1], dbin[0], dpsc[0], dsink[:, 0],
                                    jnp.zeros((120,), F32), g_poolw.reshape(-1)])
    grad_x = dx.reshape(1, T, D_MODEL)

    per_layer = small[0].shape[0]
    rows_small = n_layers * per_layer // 128
    rows_mod = n_layers * 6 * D_MODEL // 128
    rows_pad = -(rows_small + rows_mod) % 8
    pack = jnp.concatenate(small + dmod_rows + [jnp.zeros((rows_pad * 128,), F32)]).reshape(-1, 128)
    gathered = _allgather8(pack, name="ag_small").reshape(N_DEV, pack.shape[0], 128)
    summed = _sum_devices(gathered)
    small_sum = summed[:rows_small].reshape(n_layers, per_layer)
    o = 0
    small_g = {}
    for nm, width in (("g_pre_mix", D_MODEL), ("g_post_mix", D_MODEL), ("g_pre_ffn", D_MODEL),
                      ("g_post_ffn", D_MODEL), ("b_in", IN_W), ("pool_scale", POOL_W), ("sinks", 128),
                      ("pool_w", 4 * 128 * 128)):
        small_g[nm] = small_sum[:, o:o + width]
        o += width
    small_g["sinks"] = small_g["sinks"][:, :N_HEADS]
    small_g["pool_w"] = small_g["pool_w"].reshape(n_layers, 4, 128, 128)
    small_g["ada_b"] = summed[rows_small:rows_small + rows_mod].reshape(n_layers, 6 * D_MODEL)
    dmod_all = gathered[:, rows_small:rows_small + rows_mod].reshape(N_DEV, n_layers, N_SHARD, ADA_SH)
    dmod_sh = lax.dynamic_index_in_dim(dmod_all, my_chip, axis=2, keepdims=False)
    g_ada_w = _ada_wgrad(jnp.transpose(c_all), jnp.transpose(dmod_sh, (1, 0, 2)))

    red = [_reduce_scatter(big_grads[l], tag=str(l)) for l in range(n_layers)]
    g_w_in = jnp.stack([red[l][0] for l in range(n_layers)])
    g_w_out = jnp.stack([red[l][1] for l in range(n_layers)])
    g_w_gate = jnp.stack([red[l][2] for l in range(n_layers)])
    g_w_up = jnp.stack([red[l][3] for l in range(n_layers)])
    g_w_down = jnp.stack([red[l][4] for l in range(n_layers)])

    grads = dict(ada_w=g_ada_w, ada_b=small_g["ada_b"], w_in=g_w_in, b_in=small_g["b_in"], sinks=small_g["sinks"],
                 pool_w=small_g["pool_w"], pool_scale=small_g["pool_scale"], w_out=g_w_out, w_gate=g_w_gate,
                 w_up=g_w_up, w_down=g_w_down, g_pre_mix=small_g["g_pre_mix"], g_post_mix=small_g["g_post_mix"],
                 g_pre_ffn=small_g["g_pre_ffn"], g_post_ffn=small_g["g_post_ffn"])
    params = dict(ada_w=(ada_w, m_ada_w, v_ada_w), ada_b=(ada_b, m_ada_b, v_ada_b), w_in=(w_in, m_w_in, v_w_in),
                  b_in=(b_in, m_b_in, v_b_in), sinks=(sinks, m_sinks, v_sinks), pool_w=(pool_w, m_pool_w, v_pool_w),
                  pool_scale=(pool_scale, m_pool_scale, v_pool_scale), w_out=(w_out, m_w_out, v_w_out),
                  w_gate=(w_gate, m_w_gate, v_w_gate), w_up=(w_up, m_w_up, v_w_up),
                  w_down=(w_down, m_w_down, v_w_down), g_pre_mix=(g_pre_mix, m_g_pre_mix, v_g_pre_mix),
                  g_post_mix=(g_post_mix, m_g_post_mix, v_g_post_mix), g_pre_ffn=(g_pre_ffn, m_g_pre_ffn, v_g_pre_ffn),
                  g_post_ffn=(g_post_ffn, m_g_post_ffn, v_g_post_ffn))
    names = list(params)
    deltas, new_m, new_v = [], [], []
    for nm in names:
        w, m, v = params[nm]
        d, mn, vn = _adamw_nd(w, grads[nm], m, v, name="adamw_" + nm)
        deltas.append(d)
        new_m.append(mn)
        new_v.append(vn)
    return (loss, grad_x, *[grads[nm] for nm in names], *deltas, *new_m, *new_v)
```

```python
import functools

import jax
import jax.numpy as jnp
from jax import lax
from jax.experimental import pallas as pl
from jax.experimental.pallas import tpu as pltpu

F32 = jnp.float32
BF16 = jnp.bfloat16
MESH = pl.DeviceIdType.MESH

D_MODEL = 1024
ATTN_W = 512
KV_W = 128
KVD_W = 256
POOL_W = 512
IN_W = 1280
D_FF = 2816
N_SHARD = 4
FF_SH = D_FF // N_SHARD
IN_SH = IN_W // N_SHARD
OUT_SH = D_MODEL // N_SHARD
ADA_SH = 6 * D_MODEL // N_SHARD
HEAD = 64
N_HEADS = 8
GROUP = 4
BLK = 128
POOL_WINDOWS = (2, 4, 8, 16)
HALO = 16
ROT = 16
ROPE_THETA = 500000.0
EPS = 1e-6
NEG_INF = -1e30
N_DEV = 8

ADAM_LR = 0.001
ADAM_B1 = 0.9
ADAM_B2 = 0.999
ADAM_EPS = 1e-08
ADAM_WD = 0.01
ADAM_STEP = 10

VMEM_LIMIT = 48 * 1024 * 1024


def _cp(*sem):
    return pltpu.CompilerParams(dimension_semantics=sem, vmem_limit_bytes=VMEM_LIMIT)


def _full(shape):
    nd = len(shape)
    return pl.BlockSpec(shape, lambda *_: (0,) * nd)


def _resident(shape):
    nd = len(shape)
    return pl.BlockSpec(shape, lambda *_: (0,) * nd, pipeline_mode=pl.Buffered(1))


def _rows(tm, ncol):
    return pl.BlockSpec((tm, ncol), lambda i: (i, 0))


def _sds(shape, dtype):
    return jax.ShapeDtypeStruct(shape, dtype)


def _nt(a, b):
    return lax.dot_general(a, b, (((1,), (1,)), ((), ())), preferred_element_type=F32)


def _tn(a, b):
    return lax.dot_general(a, b, (((0,), (0,)), ((), ())), preferred_element_type=F32)


def _mm(a, b):
    return jnp.dot(a, b, preferred_element_type=F32)


def _rstd(x):
    return lax.rsqrt(jnp.mean(x * x, axis=-1, keepdims=True) + EPS)


def _colsum(x):
    return jnp.sum(x, axis=0, keepdims=True)


def _norm_bwd(dhat, xhat, rstd):
    return rstd * (dhat - xhat * jnp.mean(dhat * xhat, axis=-1, keepdims=True))


def _rope_tables(pos_b, lane_tab):
    T = pos_b.shape[0]
    tm = min(T, 1024)

    def body(pos_ref, tab_ref, c_ref, s1_ref, s2_ref):
        ang = pos_ref[...].astype(F32) * tab_ref[0:1, :]
        cs = jnp.cos(ang)
        sn = jnp.sin(ang)
        m_rot = tab_ref[1:2, :]
        c_ref[...] = cs * m_rot + (1.0 - m_rot)
        s1_ref[...] = -sn * tab_ref[2:3, :]
        s2_ref[...] = sn * tab_ref[3:4, :]

    out = _sds((T, 128), F32)
    return pl.pallas_call(
        body, name="rope_tables", grid=(T // tm,),
        in_specs=[_rows(tm, 128), _full((8, 128))],
        out_specs=[_rows(tm, 128)] * 3, out_shape=[out] * 3,
        compiler_params=_cp("parallel"),
    )(pos_b, lane_tab)


def _rot_fwd(t, c, s1, s2):
    w = t.shape[-1]
    return t * c + pltpu.roll(t, w - 8, 1) * s1 + pltpu.roll(t, 8, 1) * s2


def _rot_bwd(d, c, s1, s2):
    w = d.shape[-1]
    return d * c + pltpu.roll(d * s1, 8, 1) + pltpu.roll(d * s2, w - 8, 1)


def _store_dup(ref, t):
    low = lax.broadcasted_iota(jnp.int32, t.shape, 1) < HEAD
    sw = pltpu.roll(t, HEAD, 1)
    ref[:, 0:128] = jnp.where(low, t, sw).astype(BF16)
    ref[:, 128:256] = jnp.where(low, sw, t).astype(BF16)


def _fold_dup(d):
    low = lax.broadcasted_iota(jnp.int32, (d.shape[0], 128), 1) < HEAD
    d0 = d[:, 0:128]
    d1 = d[:, 128:256]
    return jnp.where(low, d0 + pltpu.roll(d0, HEAD, 1), d1 + pltpu.roll(d1, HEAD, 1))


def _fwd_in(x, mod8, g8, w_in, b_in, rc, rs1, rs2):
    T = x.shape[0]
    tm = min(T, 512)

    def body(x_ref, mod_ref, g_ref, w_ref, b_ref, c_ref, s1_ref, s2_ref,
             h_ref, q_ref, k_ref, v_ref, u_ref):
        xf = x_ref[...]
        h = (xf * _rstd(xf) * g_ref[0:1, :]) * (1.0 + mod_ref[1:2, :]) + mod_ref[0:1, :]
        hb = h.astype(BF16)
        h_ref[...] = hb
        c = c_ref[...]
        s1 = s1_ref[...]
        s2 = s2_ref[...]
        q = _mm(hb, w_ref[:, 0:ATTN_W]) + b_ref[:, 0:ATTN_W]
        q = _rot_fwd(q, jnp.tile(c, (1, 4)), jnp.tile(s1, (1, 4)), jnp.tile(s2, (1, 4)))
        q_ref[...] = (q * (HEAD ** -0.5)).astype(BF16)
        k = _mm(hb, w_ref[:, ATTN_W:ATTN_W + KV_W]) + b_ref[:, ATTN_W:ATTN_W + KV_W]
        _store_dup(k_ref, _rot_fwd(k, c, s1, s2))
        v = _mm(hb, w_ref[:, ATTN_W + KV_W:ATTN_W + 2 * KV_W]) + b_ref[:, ATTN_W + KV_W:ATTN_W + 2 * KV_W]
        _store_dup(v_ref, v)
        u_ref[...] = _mm(hb, w_ref[:, ATTN_W + 2 * KV_W:IN_W]) + b_ref[:, ATTN_W + 2 * KV_W:IN_W]

    return pl.pallas_call(
        body, name="fwd_in", grid=(T // tm,),
        in_specs=[_rows(tm, D_MODEL), _full((8, D_MODEL)), _full((8, D_MODEL)),
                  _resident((D_MODEL, IN_W)), _full((1, IN_W)),
                  _rows(tm, 128), _rows(tm, 128), _rows(tm, 128)],
        out_specs=[_rows(tm, D_MODEL), _rows(tm, ATTN_W), _rows(tm, KVD_W), _rows(tm, KVD_W), _rows(tm, POOL_W)],
        out_shape=[_sds((T, D_MODEL), BF16), _sds((T, ATTN_W), BF16), _sds((T, KVD_W), BF16),
                   _sds((T, KVD_W), BF16), _sds((T, POOL_W), F32)],
        compiler_params=_cp("parallel"),
    )(x, mod8, g8, w_in, b_in, rc, rs1, rs2)


def _band_mask(n):
    row = lax.broadcasted_iota(jnp.int32, (BLK, 2 * BLK), 0)
    col = lax.broadcasted_iota(jnp.int32, (BLK, 2 * BLK), 1)
    first = jnp.where(n > 0, 0, 2 * BLK)
    in_prev = jnp.logical_and(col < BLK, col > row + first)
    in_cur = jnp.logical_and(col >= BLK, (col - BLK) <= row)
    return jnp.logical_or(in_prev, in_cur)


def _stack_heads(x_ref, j):
    low = lax.broadcasted_iota(jnp.int32, (BLK, 128), 1) < HEAD
    parts = []
    for gp in (2 * j, 2 * j + 1):
        x2 = x_ref[:, gp * 128:(gp + 1) * 128]
        parts.append(jnp.where(low, x2, jnp.zeros_like(x2)))
        parts.append(jnp.where(low, jnp.zeros_like(x2), x2))
    return jnp.concatenate(parts, axis=0)


def _unstack_heads(o):
    low = lax.broadcasted_iota(jnp.int32, (BLK, 128), 1) < HEAD
    return [jnp.where(low, o[0:BLK], o[BLK:2 * BLK]), jnp.where(low, o[2 * BLK:3 * BLK], o[3 * BLK:4 * BLK])]


def _sink_rows(sk_ref, j):
    return jnp.concatenate([jnp.broadcast_to(sk_ref[GROUP * j + r:GROUP * j + r + 1, 0:1], (BLK, 1))
                            for r in range(GROUP)], axis=0)


def _attn_fwd(q, kd, vd, sink_b):
    T = q.shape[0]
    nb = T // BLK

    def body(q_ref, kp_ref, kc_ref, vp_ref, vc_ref, sk_ref, o_ref, lse_ref):
        n = pl.program_id(0)
        valid = jnp.concatenate([_band_mask(n)] * GROUP, axis=0)
        lane = lax.broadcasted_iota(jnp.int32, (BLK, 128), 1)
        lse_all = jnp.zeros((BLK, 128), F32)
        for j in range(N_HEADS // GROUP):
            lanes = slice(j * 128, (j + 1) * 128)
            kcat = jnp.concatenate([kp_ref[:, lanes], kc_ref[:, lanes]], axis=0)
            vcat = jnp.concatenate([vp_ref[:, lanes], vc_ref[:, lanes]], axis=0)
            s = jnp.where(valid, _nt(_stack_heads(q_ref, j), kcat), NEG_INF)
            sk = _sink_rows(sk_ref, j)
            m = jnp.maximum(jnp.max(s, axis=-1, keepdims=True), sk)
            p = jnp.exp(s - m)
            den = jnp.sum(p, axis=-1, keepdims=True) + jnp.exp(sk - m)
            p = p * (1.0 / den)
            o = _mm(p.astype(BF16), vcat)
            o_ref[:, 2 * j * 128:(2 * j + 2) * 128] = jnp.concatenate(_unstack_heads(o), axis=1).astype(BF16)
            lse = m + jnp.log(den)
            for r in range(GROUP):
                lse_all = jnp.where(lane == GROUP * j + r, lse[r * BLK:(r + 1) * BLK], lse_all)
        lse_ref[...] = lse_all

    prev = lambda n: (jnp.maximum(n - 1, 0), 0)
    cur = lambda n: (n, 0)
    return pl.pallas_call(
        body, name="attn_fwd", grid=(nb,),
        in_specs=[pl.BlockSpec((BLK, ATTN_W), cur),
                  pl.BlockSpec((BLK, KVD_W), prev), pl.BlockSpec((BLK, KVD_W), cur),
                  pl.BlockSpec((BLK, KVD_W), prev), pl.BlockSpec((BLK, KVD_W), cur),
                  _full((8, 128))],
        out_specs=[pl.BlockSpec((BLK, ATTN_W), cur), pl.BlockSpec((BLK, 128), cur)],
        out_shape=[_sds((T, ATTN_W), BF16), _sds((T, 128), F32)],
        compiler_params=_cp("parallel"),
    )(q, kd, kd, vd, vd, sink_b)


def _pool_fwd(u, pool_w, pool_scale):
    T = u.shape[0]
    tm = min(T, 512)

    def body(u_ref, w_ref, sc_ref, out_ref, pooled_ref, halo):
        i = pl.program_id(0)

        @pl.when(i == 0)
        def _():
            halo[...] = jnp.zeros_like(halo)

        ub = u_ref[...]
        ext = jnp.concatenate([halo[...], ub], axis=0)
        halo[...] = ub[tm - HALO:, :]
        tpos = (i * tm + lax.broadcasted_iota(jnp.int32, (tm, 1), 0)).astype(F32)
        for g, w in enumerate(POOL_WINDOWS):
            lanes = slice(g * 128, (g + 1) * 128)
            s = ext[:, lanes]
            sh = 1
            while sh < w:
                s = s + pltpu.roll(s, sh, 0)
                sh *= 2
            cnt = jnp.minimum(tpos + 1.0, float(w))
            pb = (s[HALO:, :] / cnt - ub[:, lanes]).astype(BF16)
            z = _mm(pb, w_ref[g].astype(BF16))
            out_ref[:, lanes] = (z * sc_ref[:, lanes]).astype(BF16)
            pooled_ref[:, lanes] = pb

    return pl.pallas_call(
        body, name="pool_fwd", grid=(T // tm,),
        in_specs=[_rows(tm, POOL_W), _full((4, 128, 128)), _full((1, POOL_W))],
        out_specs=[_rows(tm, POOL_W), _rows(tm, POOL_W)],
        out_shape=[_sds((T, POOL_W), BF16), _sds((T, POOL_W), BF16)],
        scratch_shapes=[pltpu.VMEM((HALO, POOL_W), F32)],
        compiler_params=_cp("arbitrary"),
    )(u, pool_w, pool_scale)


def _fwd_out(attn, pool, x, w_out, g8, mod8):
    T = x.shape[0]
    tm = min(T, 512)

    def body(a_ref, p_ref, x_ref, w_ref, g_ref, mod_ref, mix_ref, x1_ref):
        mix = _mm(a_ref[...], w_ref[0:ATTN_W, :]) + _mm(p_ref[...], w_ref[ATTN_W:, :])
        mix_ref[...] = mix
        x1_ref[...] = x_ref[...] + mod_ref[2:3, :] * (mix * _rstd(mix) * g_ref[1:2, :])

    return pl.pallas_call(
        body, name="fwd_out", grid=(T // tm,),
        in_specs=[_rows(tm, ATTN_W), _rows(tm, POOL_W), _rows(tm, D_MODEL),
                  _resident((D_MODEL, D_MODEL)), _full((8, D_MODEL)), _full((8, D_MODEL))],
        out_specs=[_rows(tm, D_MODEL), _rows(tm, D_MODEL)],
        out_shape=[_sds((T, D_MODEL), F32), _sds((T, D_MODEL), F32)],
        compiler_params=_cp("parallel"),
    )(attn, pool, x, w_out, g8, mod8)


def _sh_rows(tm):
    return pl.BlockSpec((N_SHARD, tm, FF_SH), lambda i: (0, i, 0))


def _ffn_fwd(x1, mod8, g8, wg, wu, wd):
    T = x1.shape[0]
    tm = min(T, 256)

    def body(x_ref, mod_ref, g_ref, wg_ref, wu_ref, wd_ref, a_ref, b_ref, f_ref, x2_ref):
        xf = x_ref[...]
        h = (xf * _rstd(xf) * g_ref[2:3, :]) * (1.0 + mod_ref[4:5, :]) + mod_ref[3:4, :]
        hb = h.astype(BF16)
        f = jnp.zeros((tm, D_MODEL), F32)
        for s in range(N_SHARD):
            a = _mm(hb, wg_ref[s])
            b = _mm(hb, wu_ref[s])
            a_ref[s] = a.astype(BF16)
            b_ref[s] = b.astype(BF16)
            act = (a * jax.nn.sigmoid(a)) * b
            f = f + _mm(act.astype(BF16), wd_ref[s])
        f_ref[...] = f
        x2_ref[...] = xf + mod_ref[5:6, :] * (f * _rstd(f) * g_ref[3:4, :])

    act_shape = _sds((N_SHARD, T, FF_SH), BF16)
    return pl.pallas_call(
        body, name="ffn_fwd", grid=(T // tm,),
        in_specs=[_rows(tm, D_MODEL), _full((8, D_MODEL)), _full((8, D_MODEL)),
                  _resident((N_SHARD, D_MODEL, FF_SH)), _resident((N_SHARD, D_MODEL, FF_SH)),
                  _resident((N_SHARD, FF_SH, D_MODEL))],
        out_specs=[_sh_rows(tm), _sh_rows(tm), _rows(tm, D_MODEL), _rows(tm, D_MODEL)],
        out_shape=[act_shape, act_shape, _sds((T, D_MODEL), F32), _sds((T, D_MODEL), F32)],
        compiler_params=_cp("parallel"),
    )(x1, mod8, g8, wg, wu, wd)


def _loss_grad(y, target):
    T = y.shape[0]
    tm = min(T, 1024)

    def body(y_ref, t_ref, dy_ref, loss_ref):
        @pl.when(pl.program_id(0) == 0)
        def _():
            loss_ref[...] = jnp.zeros_like(loss_ref)

        e = y_ref[...] - t_ref[...]
        dy_ref[...] = e * (1.0 / D_MODEL)
        part = 0.5 * jnp.sum(jnp.mean(e * e, axis=-1, keepdims=True), axis=0, keepdims=True)
        loss_ref[...] += part

    return pl.pallas_call(
        body, name="loss_grad", grid=(T // tm,),
        in_specs=[_rows(tm, D_MODEL), _rows(tm, D_MODEL)],
        out_specs=[_rows(tm, D_MODEL), _full((8, 128))],
        out_shape=[_sds((T, D_MODEL), F32), _sds((8, 128), F32)],
        compiler_params=_cp("arbitrary"),
    )(y, target)


def _ffn_bwd_act(dx2, f, a, b, mod8, g8, wd):
    T = dx2.shape[0]
    tm = min(T, 256)

    def body(dx_ref, f_ref, a_ref, b_ref, mod_ref, g_ref, wd_ref,
             df_ref, da_ref, db_ref, act_ref, red_ref):
        @pl.when(pl.program_id(0) == 0)
        def _():
            red_ref[...] = jnp.zeros_like(red_ref)

        dx = dx_ref[...]
        fv = f_ref[...]
        rstd = _rstd(fv)
        fhat = fv * rstd
        gpost = g_ref[3:4, :]
        red_ref[0:1, :] += _colsum(dx * (fhat * gpost))
        dn = dx * mod_ref[5:6, :]
        red_ref[1:2, :] += _colsum(dn * fhat)
        dfb = _norm_bwd(dn * gpost, fhat, rstd).astype(BF16)
        df_ref[...] = dfb
        for s in range(N_SHARD):
            dact = _nt(dfb, wd_ref[s])
            av = a_ref[s].astype(F32)
            bv = b_ref[s].astype(F32)
            sig = jax.nn.sigmoid(av)
            sl = av * sig
            act_ref[s] = (sl * bv).astype(BF16)
            da_ref[s] = (dact * bv * (sig * (1.0 + av * (1.0 - sig)))).astype(BF16)
            db_ref[s] = (dact * sl).astype(BF16)

    act_shape = _sds((N_SHARD, T, FF_SH), BF16)
    return pl.pallas_call(
        body, name="ffn_bwd_act", grid=(T // tm,),
        in_specs=[_rows(tm, D_MODEL), _rows(tm, D_MODEL), _sh_rows(tm), _sh_rows(tm),
                  _full((8, D_MODEL)), _full((8, D_MODEL)), _resident((N_SHARD, FF_SH, D_MODEL))],
        out_specs=[_rows(tm, D_MODEL), _sh_rows(tm), _sh_rows(tm), _sh_rows(tm), _full((8, D_MODEL))],
        out_shape=[_sds((T, D_MODEL), BF16), act_shape, act_shape, act_shape, _sds((8, D_MODEL), F32)],
        compiler_params=_cp("arbitrary"),
    )(dx2, f, a, b, mod8, g8, wd)


def _ffn_bwd_in(da, db, x1, dx2, mod8, g8, wg, wu):
    T = x1.shape[0]
    tm = min(T, 256)

    def body(da_ref, db_ref, x_ref, dx_ref, mod_ref, g_ref, wg_ref, wu_ref, dx1_ref, h2_ref, red_ref):
        @pl.when(pl.program_id(0) == 0)
        def _():
            red_ref[...] = jnp.zeros_like(red_ref)

        dh = jnp.zeros((tm, D_MODEL), F32)
        for s in range(N_SHARD):
            dh = dh + _nt(da_ref[s], wg_ref[s]) + _nt(db_ref[s], wu_ref[s])
        xf = x_ref[...]
        rstd = _rstd(xf)
        xhat = xf * rstd
        gpre = g_ref[2:3, :]
        scale1 = 1.0 + mod_ref[4:5, :]
        h2_ref[...] = ((xhat * gpre) * scale1 + mod_ref[3:4, :]).astype(BF16)
        red_ref[0:1, :] += _colsum(dh)
        red_ref[1:2, :] += _colsum(dh * (xhat * gpre))
        red_ref[2:3, :] += _colsum(dh * scale1 * xhat)
        dx1_ref[...] = dx_ref[...] + _norm_bwd(dh * scale1 * gpre, xhat, rstd)

    return pl.pallas_call(
        body, name="ffn_bwd_in", grid=(T // tm,),
        in_specs=[_sh_rows(tm), _sh_rows(tm), _rows(tm, D_MODEL), _rows(tm, D_MODEL),
                  _full((8, D_MODEL)), _full((8, D_MODEL)),
                  _resident((N_SHARD, D_MODEL, FF_SH)), _resident((N_SHARD, D_MODEL, FF_SH))],
        out_specs=[_rows(tm, D_MODEL), _rows(tm, D_MODEL), _full((8, D_MODEL))],
        out_shape=[_sds((T, D_MODEL), F32), _sds((T, D_MODEL), BF16), _sds((8, D_MODEL), F32)],
        compiler_params=_cp("arbitrary"),
    )(da, db, x1, dx2, mod8, g8, wg, wu)


def _wgrad(a, b, name):
    T, K = a.shape
    N = b.shape[1]
    tt = min(T, 512)
    tk = min(K, 512)

    def body(a_ref, b_ref, o_ref):
        @pl.when(pl.program_id(1) == 0)
        def _():
            o_ref[...] = jnp.zeros_like(o_ref)

        o_ref[...] += _tn(a_ref[...], b_ref[...])

    return pl.pallas_call(
        body, name=name, grid=(K // tk, T // tt),
        in_specs=[pl.BlockSpec((tt, tk), lambda i, t: (t, i)), pl.BlockSpec((tt, N), lambda i, t: (t, 0))],
        out_specs=pl.BlockSpec((tk, N), lambda i, t: (i, 0)),
        out_shape=_sds((K, N), F32),
        compiler_params=_cp("parallel", "arbitrary"),
    )(a, b)


def _wgrad_cols(a, b, name):
    T, K = a.shape
    n = b.shape[2]
    tt = min(T, 512)

    def body(a_ref, b_ref, o_ref):
        @pl.when(pl.program_id(1) == 0)
        def _():
            o_ref[...] = jnp.zeros_like(o_ref)

        o_ref[...] += _tn(a_ref[...], b_ref[...])

    return pl.pallas_call(
        body, name=name, grid=(N_SHARD, T // tt),
        in_specs=[pl.BlockSpec((tt, K), lambda s, t: (t, 0)), pl.BlockSpec((None, tt, n), lambda s, t: (s, t, 0))],
        out_specs=pl.BlockSpec((None, K, n), lambda s, t: (s, 0, 0)),
        out_shape=_sds((N_SHARD, K, n), F32),
        compiler_params=_cp("parallel", "arbitrary"),
    )(a, b)


def _wgrad_rows(a, b, name):
    T, N = b.shape
    k = a.shape[2]
    tt = min(T, 512)

    def body(a_ref, b_ref, o_ref):
        @pl.when(pl.program_id(1) == 0)
        def _():
            o_ref[...] = jnp.zeros_like(o_ref)

        o_ref[...] += _tn(a_ref[...], b_ref[...])

    return pl.pallas_call(
        body, name=name, grid=(N_SHARD, T // tt),
        in_specs=[pl.BlockSpec((None, tt, k), lambda s, t: (s, t, 0)), pl.BlockSpec((tt, N), lambda s, t: (t, 0))],
        out_specs=pl.BlockSpec((None, k, N), lambda s, t: (s, 0, 0)),
        out_shape=_sds((N_SHARD, k, N), F32),
        compiler_params=_cp("parallel", "arbitrary"),
    )(a, b)


def _mix_bwd(dx1, mix, mod8, g8, w_out):
    T = dx1.shape[0]
    tm = min(T, 512)

    def body(dx_ref, mix_ref, mod_ref, g_ref, w_ref, dmix_ref, da_ref, dp_ref, red_ref):
        @pl.when(pl.program_id(0) == 0)
        def _():
            red_ref[...] = jnp.zeros_like(red_ref)

        dx = dx_ref[...]
        mv = mix_ref[...]
        rstd = _rstd(mv)
        mhat = mv * rstd
        gpost = g_ref[1:2, :]
        red_ref[0:1, :] += _colsum(dx * (mhat * gpost))
        dn = dx * mod_ref[2:3, :]
        red_ref[1:2, :] += _colsum(dn * mhat)
        dmb = _norm_bwd(dn * gpost, mhat, rstd).astype(BF16)
        dmix_ref[...] = dmb
        da_ref[...] = _nt(dmb, w_ref[0:ATTN_W, :]).astype(BF16)
        dp_ref[...] = _nt(dmb, w_ref[ATTN_W:, :]).astype(BF16)

    return pl.pallas_call(
        body, name="mix_bwd", grid=(T // tm,),
        in_specs=[_rows(tm, D_MODEL), _rows(tm, D_MODEL), _full((8, D_MODEL)), _full((8, D_MODEL)),
                  _resident((D_MODEL, D_MODEL))],
        out_specs=[_rows(tm, D_MODEL), _rows(tm, ATTN_W), _rows(tm, POOL_W), _full((8, D_MODEL))],
        out_shape=[_sds((T, D_MODEL), BF16), _sds((T, ATTN_W), BF16), _sds((T, POOL_W), BF16),
                   _sds((8, D_MODEL), F32)],
        compiler_params=_cp("arbitrary"),
    )(dx1, mix, mod8, g8, w_out)


def _attn_bwd(q, kd, vd, lse, dattn, sink_b):
    T = q.shape[0]
    nb = T // BLK

    def body(q_ref, do_ref, lse_ref, kp_ref, kc_ref, vp_ref, vc_ref, sk_ref,
             dq_ref, dk_ref, dv_ref, dsk_ref, carry_k, carry_v):
        n = pl.program_id(0)

        @pl.when(n == 0)
        def _():
            carry_k[...] = jnp.zeros_like(carry_k)
            carry_v[...] = jnp.zeros_like(carry_v)
            dsk_ref[...] = jnp.zeros_like(dsk_ref)

        @pl.when(n < nb)
        def _():
            valid = jnp.concatenate([_band_mask(n)] * GROUP, axis=0)
            lane = lax.broadcasted_iota(jnp.int32, (BLK, 128), 1)
            lse_all = lse_ref[...]
            for j in range(N_HEADS // GROUP):
                lanes = slice(j * 128, (j + 1) * 128)
                kcat = jnp.concatenate([kp_ref[:, lanes], kc_ref[:, lanes]], axis=0)
                vcat = jnp.concatenate([vp_ref[:, lanes], vc_ref[:, lanes]], axis=0)
                qs = _stack_heads(q_ref, j)
                dos = _stack_heads(do_ref, j)
                lse = jnp.concatenate(
                    [jnp.sum(jnp.where(lane == GROUP * j + r, lse_all, 0.0), axis=-1, keepdims=True)
                     for r in range(GROUP)], axis=0)
                p = jnp.exp(jnp.where(valid, _nt(qs, kcat), NEG_INF) - lse)
                dp = _nt(dos, vcat)
                delta = jnp.sum(p * dp, axis=-1, keepdims=True)
                ds = (p * (dp - delta)).astype(BF16)
                sink_term = jnp.exp(_sink_rows(sk_ref, j) - lse) * delta
                for r in range(GROUP):
                    h = GROUP * j + r
                    dsk_ref[h:h + 1, :] += -jnp.sum(sink_term[r * BLK:(r + 1) * BLK], axis=0, keepdims=True)
                dq_ref[:, 2 * j * 128:(2 * j + 2) * 128] = jnp.concatenate(_unstack_heads(_mm(ds, kcat)), axis=1)
                dk = _tn(ds, qs)
                dv = _tn(p.astype(BF16), dos)
                dk_ref[:, lanes] = carry_k[:, lanes] + dk[0:BLK]
                dv_ref[:, lanes] = carry_v[:, lanes] + dv[0:BLK]
                carry_k[:, lanes] = dk[BLK:]
                carry_v[:, lanes] = dv[BLK:]

        @pl.when(n == nb)
        def _():
            dk_ref[...] = carry_k[...]
            dv_ref[...] = carry_v[...]

    cur = lambda n: (jnp.minimum(n, nb - 1), 0)
    prev = lambda n: (jnp.maximum(n - 1, 0), 0)
    return pl.pallas_call(
        body, name="attn_bwd", grid=(nb + 1,),
        in_specs=[pl.BlockSpec((BLK, ATTN_W), cur), pl.BlockSpec((BLK, ATTN_W), cur), pl.BlockSpec((BLK, 128), cur),
                  pl.BlockSpec((BLK, KVD_W), prev), pl.BlockSpec((BLK, KVD_W), cur),
                  pl.BlockSpec((BLK, KVD_W), prev), pl.BlockSpec((BLK, KVD_W), cur),
                  _full((8, 128))],
        out_specs=[pl.BlockSpec((BLK, ATTN_W), cur), pl.BlockSpec((BLK, KVD_W), prev),
                   pl.BlockSpec((BLK, KVD_W), prev), _full((8, 128))],
        out_shape=[_sds((T, ATTN_W), F32), _sds((T, KVD_W), F32), _sds((T, KVD_W), F32), _sds((8, 128), F32)],
        scratch_shapes=[pltpu.VMEM((BLK, KVD_W), F32), pltpu.VMEM((BLK, KVD_W), F32)],
        compiler_params=_cp("arbitrary"),
    )(q, dattn, lse, kd, kd, vd, vd, sink_b)


def _pool_bwd(dpool, pooled, pool_w, pool_scale):
    T = dpool.shape[0]
    tm = min(T, 512)
    nbk = T // tm
    ext_rows = tm + HALO

    def body(dp_ref, pl_ref, w_ref, sc_ref, du_ref, dw_ref, dsc_ref, halo):
        i = pl.program_id(0)

        @pl.when(i == 0)
        def _():
            halo[...] = jnp.zeros_like(halo)
            dw_ref[...] = jnp.zeros_like(dw_ref)
            dsc_ref[...] = jnp.zeros_like(dsc_ref)

        blk = nbk - 1 - i
        tpos = (blk * tm + lax.broadcasted_iota(jnp.int32, (tm, 1), 0)).astype(F32)
        for g, w in enumerate(POOL_WINDOWS):
            lanes = slice(g * 128, (g + 1) * 128)
            dp = dp_ref[:, lanes].astype(F32)
            pb = pl_ref[:, lanes]
            wg = w_ref[g].astype(BF16)
            z = _mm(pb, wg)
            dsc_ref[0:1, lanes] += _colsum(dp * z)
            dz = (dp * sc_ref[:, lanes]).astype(BF16)
            dw_ref[g] += _tn(pb, dz)
            dpl = _nt(dz, wg)
            e = dpl / jnp.minimum(tpos + 1.0, float(w))
            s = jnp.concatenate([e, halo[:, lanes]], axis=0)
            halo[:, lanes] = e[0:HALO, :]
            sh = 1
            while sh < w:
                s = s + pltpu.roll(s, ext_rows - sh, 0)
                sh *= 2
            du_ref[:, lanes] = s[0:tm, :] - dpl

    rev = lambda i: (nbk - 1 - i, 0)
    return pl.pallas_call(
        body, name="pool_bwd", grid=(nbk,),
        in_specs=[pl.BlockSpec((tm, POOL_W), rev), pl.BlockSpec((tm, POOL_W), rev),
                  _full((4, 128, 128)), _full((1, POOL_W))],
        out_specs=[pl.BlockSpec((tm, POOL_W), rev), _full((4, 128, 128)), _full((8, POOL_W))],
        out_shape=[_sds((T, POOL_W), F32), _sds((4, 128, 128), F32), _sds((8, POOL_W), F32)],
        scratch_shapes=[pltpu.VMEM((HALO, POOL_W), F32)],
        compiler_params=_cp("arbitrary"),
    )(dpool, pooled, pool_w, pool_scale)


def _in_bwd(dq, dk, dv, du, rc, rs1, rs2, x, dx1, mod8, g8, w_in):
    T = x.shape[0]
    tm = min(T, 512)

    def body(dq_ref, dk_ref, dv_ref, du_ref, c_ref, s1_ref, s2_ref, x_ref, dx1_ref, mod_ref, g_ref, w_ref,
             dx_ref, dproj_ref, red_ref, dbin_ref):
        @pl.when(pl.program_id(0) == 0)
        def _():
            red_ref[...] = jnp.zeros_like(red_ref)
            dbin_ref[...] = jnp.zeros_like(dbin_ref)

        c = c_ref[...]
        s1 = s1_ref[...]
        s2 = s2_ref[...]
        dqp = _rot_bwd(dq_ref[...] * (HEAD ** -0.5), jnp.tile(c, (1, 4)), jnp.tile(s1, (1, 4)), jnp.tile(s2, (1, 4)))
        dkp = _rot_bwd(_fold_dup(dk_ref[...]), c, s1, s2)
        pieces = ((0, ATTN_W, dqp), (ATTN_W, ATTN_W + KV_W, dkp),
                  (ATTN_W + KV_W, ATTN_W + 2 * KV_W, _fold_dup(dv_ref[...])), (ATTN_W + 2 * KV_W, IN_W, du_ref[...]))
        dh = jnp.zeros((tm, D_MODEL), F32)
        for lo, hi, val in pieces:
            dbin_ref[0:1, lo:hi] += _colsum(val)
            vb = val.astype(BF16)
            dproj_ref[:, lo:hi] = vb
            dh = dh + _nt(vb, w_ref[:, lo:hi])
        xf = x_ref[...]
        rstd = _rstd(xf)
        xhat = xf * rstd
        gpre = g_ref[0:1, :]
        scale1 = 1.0 + mod_ref[1:2, :]
        red_ref[0:1, :] += _colsum(dh)
        red_ref[1:2, :] += _colsum(dh * (xhat * gpre))
        red_ref[2:3, :] += _colsum(dh * scale1 * xhat)
        dx_ref[...] = dx1_ref[...] + _norm_bwd(dh * scale1 * gpre, xhat, rstd)

    return pl.pallas_call(
        body, name="in_bwd", grid=(T // tm,),
        in_specs=[_rows(tm, ATTN_W), _rows(tm, KVD_W), _rows(tm, KVD_W), _rows(tm, POOL_W),
                  _rows(tm, 128), _rows(tm, 128), _rows(tm, 128), _rows(tm, D_MODEL), _rows(tm, D_MODEL),
                  _full((8, D_MODEL)), _full((8, D_MODEL)), _resident((D_MODEL, IN_W))],
        out_specs=[_rows(tm, D_MODEL), _rows(tm, IN_W), _full((8, D_MODEL)), _full((8, IN_W))],
        out_shape=[_sds((T, D_MODEL), F32), _sds((T, IN_W), BF16), _sds((8, D_MODEL), F32), _sds((8, IN_W), F32)],
        compiler_params=_cp("arbitrary"),
    )(dq, dk, dv, du, rc, rs1, rs2, x, dx1, mod8, g8, w_in)


def _mod_fwd(c_all, ada_w, ada_b_sh):
    tn = 512

    def body(c_ref, w_ref, b_ref, o_ref):
        cv = c_ref[...]
        ca = (cv * jax.nn.sigmoid(cv)).astype(BF16)
        o_ref[...] = _mm(ca, w_ref[...].astype(BF16)) + b_ref[...]

    return pl.pallas_call(
        body, name="mod_fwd", grid=(2, ADA_SH // tn),
        in_specs=[_full((8, D_MODEL)), pl.BlockSpec((None, D_MODEL, tn), lambda l, j: (l, 0, j)),
                  pl.BlockSpec((None, 1, tn), lambda l, j: (l, 0, j))],
        out_specs=pl.BlockSpec((None, 8, tn), lambda l, j: (l, 0, j)),
        out_shape=_sds((2, 8, ADA_SH), F32),
        compiler_params=_cp("parallel", "parallel"),
    )(c_all, ada_w, ada_b_sh)


def _ada_wgrad(c_all_t, dmod_sh):
    tn = 512

    def body(c_ref, d_ref, o_ref):
        cv = c_ref[...]
        ca = cv * jax.nn.sigmoid(cv)
        o_ref[...] = jnp.dot(ca, d_ref[...], preferred_element_type=F32, precision=lax.Precision.HIGHEST)

    return pl.pallas_call(
        body, name="ada_wgrad", grid=(2, ADA_SH // tn),
        in_specs=[_full((D_MODEL, 8)), pl.BlockSpec((None, 8, tn), lambda l, j: (l, 0, j))],
        out_specs=pl.BlockSpec((None, D_MODEL, tn), lambda l, j: (l, 0, j)),
        out_shape=_sds((2, D_MODEL, ADA_SH), F32),
        compiler_params=_cp("parallel", "parallel"),
    )(c_all_t, dmod_sh)


def _sum_devices(g):
    R = g.shape[1]

    def body(g_ref, o_ref):
        acc = g_ref[0]
        for d in range(1, N_DEV):
            acc = acc + g_ref[d]
        o_ref[...] = acc

    return pl.pallas_call(
        body, name="sum_devices", grid=(1,),
        in_specs=[_full((N_DEV, R, 128))], out_specs=_full((R, 128)), out_shape=_sds((R, 128), F32),
        compiler_params=_cp("arbitrary"),
    )(g)


def _adamw(w, g, m, v, name):
    R, C = w.shape
    tr = R
    for cand in (256, 128, 64, 32, 16, 8):
        if R % cand == 0 and cand * C * 4 <= 2 * 1024 * 1024:
            tr = cand
            break

    def body(w_ref, g_ref, m_ref, v_ref, d_ref, nm_ref, nv_ref):
        gv = g_ref[...]
        mn = ADAM_B1 * m_ref[...] + (1.0 - ADAM_B1) * gv
        vn = ADAM_B2 * v_ref[...] + (1.0 - ADAM_B2) * (gv * gv)
        m_hat = mn / (1.0 - ADAM_B1 ** ADAM_STEP)
        v_hat = vn / (1.0 - ADAM_B2 ** ADAM_STEP)
        d_ref[...] = -ADAM_LR * (m_hat / (jnp.sqrt(v_hat) + ADAM_EPS) + ADAM_WD * w_ref[...])
        nm_ref[...] = mn
        nv_ref[...] = vn

    spec = pl.BlockSpec((tr, C), lambda i: (i, 0))
    out = _sds((R, C), F32)
    return pl.pallas_call(
        body, name=name, grid=(R // tr,),
        in_specs=[spec] * 4, out_specs=[spec] * 3, out_shape=[out] * 3,
        compiler_params=_cp("parallel"),
    )(w, g, m, v)


def _adamw_nd(w, g, m, v, name):
    shape = w.shape
    if w.ndim == 2 and shape[1] < 128:
        view = (1, shape[0] * shape[1])
    else:
        view = (-1, shape[-1])
    outs = _adamw(*[t.reshape(view) for t in (w, g, m, v)], name=name)
    return [o.reshape(shape) for o in outs]


def _coords():
    return lax.axis_index("x"), lax.axis_index("y"), lax.axis_index("c")


def _other_chips(x, y):
    return [(1 - x, y), (x, 1 - y), (1 - x, 1 - y)]


def _allgather8(blk, name):
    m_per, n = blk.shape

    def body(x_ref, out_ref, send_sems, recv_sems, local_sem):
        x, y, c = _coords()
        me, sibling = (x, y, c), (x, y, 1 - c)
        chips = _other_chips(x, y)

        def rows(px, py, pc):
            return out_ref.at[pl.ds((4 * px + 2 * py + pc) * m_per, m_per), :]

        def copy(k, block, to, src=None):
            return pltpu.make_async_remote_copy(
                src_ref=rows(*block) if src is None else src, dst_ref=rows(*block),
                send_sem=send_sems.at[k], recv_sem=recv_sems.at[k], device_id=to, device_id_type=MESH)

        mine = pltpu.make_async_copy(x_ref, rows(*me), local_sem)
        mine.start()
        first = [copy(0, me, sibling, src=x_ref)]
        first += [copy(1 + j, me, (*chip, c), src=x_ref) for j, chip in enumerate(chips)]
        for cp in first:
            cp.start()
        passed = [copy(4 + j, (*chip, c), sibling) for j, chip in enumerate(chips)]
        for j, chip in enumerate(chips):
            copy(1 + j, (*chip, c), me).wait_recv()
            passed[j].start()
        copy(0, sibling, me).wait_recv()
        for j, chip in enumerate(chips):
            copy(4 + j, (*chip, 1 - c), me).wait_recv()
        for cp in first + passed:
            cp.wait_send()
        mine.wait()

    return pl.pallas_call(
        body, name=name,
        out_shape=_sds((N_DEV * m_per, n), blk.dtype),
        in_specs=[pl.BlockSpec(memory_space=pltpu.VMEM)],
        out_specs=pl.BlockSpec(memory_space=pltpu.VMEM),
        scratch_shapes=[pltpu.SemaphoreType.DMA((7,)), pltpu.SemaphoreType.DMA((7,)), pltpu.SemaphoreType.DMA],
        compiler_params=pltpu.CompilerParams(vmem_limit_bytes=VMEM_LIMIT),
    )(blk)


def _row_tile(r, n):
    for cand in (512, 256, 128, 64, 32, 16):
        if r % cand == 0 and cand * n * 4 <= 2 * 1024 * 1024:
            return cand
    return r


def _cast_slot(w, chip, name):
    r, n = w.shape
    tr = _row_tile(r, n)

    def body(chip_ref, w_ref, o_ref):
        o_ref[...] = w_ref[...].astype(BF16)

    grid_spec = pltpu.PrefetchScalarGridSpec(
        num_scalar_prefetch=1, grid=(r // tr,),
        in_specs=[pl.BlockSpec((tr, n), lambda i, ch: (i, 0))],
        out_specs=pl.BlockSpec((None, tr, n), lambda i, ch: (ch[0], i, 0)))
    return pl.pallas_call(
        body, name=name, grid_spec=grid_spec, out_shape=_sds((N_SHARD, r, n), BF16),
        compiler_params=_cp("arbitrary"),
    )(chip, w)


def _allgather_weights(bufs, name):
    nt = len(bufs)
    hom = [pl.BlockSpec(memory_space=pl.ANY)] * nt

    def body(*refs):
        outs = refs[nt:2 * nt]
        send_sems, recv_sems = refs[2 * nt:]
        x, y, c = _coords()
        sibling = (x, y, 1 - c)
        chips = _other_chips(x, y)

        def copy(t, k, block_chip, hc, to):
            r = outs[t].shape[1] // 2
            blk = outs[t].at[2 * block_chip[0] + block_chip[1], pl.ds(hc * r, r)]
            return pltpu.make_async_remote_copy(
                src_ref=blk, dst_ref=blk,
                send_sem=send_sems.at[t, k], recv_sem=recv_sems.at[t, k], device_id=to, device_id_type=MESH)

        started = []
        for t in range(nt):
            for j, chip in enumerate(chips):
                cp = copy(t, j, (x, y), c, (*chip, c))
                cp.start()
                started.append(cp)
        for t in range(nt):
            for j, chip in enumerate(chips):
                copy(t, j, chip, c, sibling).wait_recv()
                fw = copy(t, 3 + j, chip, c, sibling)
                fw.start()
                started.append(fw)
        for t in range(nt):
            for j, chip in enumerate(chips):
                copy(t, 3 + j, chip, 1 - c, sibling).wait_recv()
        for cp in started:
            cp.wait_send()

    return pl.pallas_call(
        body, name=name,
        out_shape=[_sds(b.shape, b.dtype) for b in bufs],
        in_specs=hom, out_specs=hom,
        input_output_aliases={t: t for t in range(nt)},
        scratch_shapes=[pltpu.SemaphoreType.DMA((nt, 6)), pltpu.SemaphoreType.DMA((nt, 6))],
    )(*bufs)


def _swap_halves(grads, name):
    nt = len(grads)
    hom = [pl.BlockSpec(memory_space=pl.ANY)] * nt

    def body(*refs):
        ins = refs[:nt]
        outs = refs[nt:2 * nt]
        send_sems, recv_sems = refs[2 * nt:]
        x, y, c = _coords()
        sibling = (x, y, 1 - c)
        cps = []
        for t in range(nt):
            r = ins[t].shape[1] // 2
            cp = pltpu.make_async_remote_copy(
                src_ref=ins[t].at[:, pl.ds((1 - c) * r, r)], dst_ref=outs[t],
                send_sem=send_sems.at[t], recv_sem=recv_sems.at[t], device_id=sibling, device_id_type=MESH)
            cp.start()
            cps.append(cp)
        for cp in cps:
            cp.wait()

    return pl.pallas_call(
        body, name=name,
        out_shape=[_sds((N_SHARD, g.shape[1] // 2, g.shape[2]), g.dtype) for g in grads],
        in_specs=hom, out_specs=hom,
        scratch_shapes=[pltpu.SemaphoreType.DMA((nt,)), pltpu.SemaphoreType.DMA((nt,))],
    )(*grads)


def _scatter_chips(sums, name):
    nt = len(sums)
    hom = [pl.BlockSpec(memory_space=pl.ANY)] * nt

    def body(*refs):
        ins = refs[:nt]
        outs = refs[nt:2 * nt]
        send_sems, recv_sems = refs[2 * nt:]
        x, y, c = _coords()
        chips = _other_chips(x, y)
        cps = []
        for t in range(nt):
            for j, chip in enumerate(chips):
                cp = pltpu.make_async_remote_copy(
                    src_ref=ins[t].at[2 * chip[0] + chip[1]], dst_ref=outs[t].at[j],
                    send_sem=send_sems.at[t, j], recv_sem=recv_sems.at[t, j],
                    device_id=(*chip, c), device_id_type=MESH)
                cp.start()
                cps.append(cp)
        for cp in cps:
            cp.wait()

    return pl.pallas_call(
        body, name=name,
        out_shape=[_sds((3,) + s.shape[1:], s.dtype) for s in sums],
        in_specs=hom, out_specs=hom,
        scratch_shapes=[pltpu.SemaphoreType.DMA((nt, 3)), pltpu.SemaphoreType.DMA((nt, 3))],
    )(*sums)


def _join_halves(tots, name):
    nt = len(tots)
    hom = [pl.BlockSpec(memory_space=pl.ANY)] * nt

    def body(*refs):
        outs = refs[nt:2 * nt]
        send_sems, recv_sems = refs[2 * nt:]
        x, y, c = _coords()
        sibling = (x, y, 1 - c)
        cps = []
        for t in range(nt):
            cp = pltpu.make_async_remote_copy(
                src_ref=outs[t].at[c], dst_ref=outs[t].at[c],
                send_sem=send_sems.at[t], recv_sem=recv_sems.at[t], device_id=sibling, device_id_type=MESH)
            cp.start()
            cps.append(cp)
        for t in range(nt):
            pltpu.make_async_remote_copy(
                src_ref=outs[t].at[c], dst_ref=outs[t].at[1 - c],
                send_sem=send_sems.at[t], recv_sem=recv_sems.at[t], device_id=sibling, device_id_type=MESH).wait_recv()
        for cp in cps:
            cp.wait_send()

    return pl.pallas_call(
        body, name=name,
        out_shape=[_sds(t.shape, t.dtype) for t in tots],
        in_specs=hom, out_specs=hom,
        input_output_aliases={t: t for t in range(nt)},
        scratch_shapes=[pltpu.SemaphoreType.DMA((nt,)), pltpu.SemaphoreType.DMA((nt,))],
    )(*tots)


def _pair_sum(g, recv, core, chip, name):
    _, _, r, n = g.shape
    tr = _row_tile(r, n)

    def body(core_ref, chip_ref, g_ref, r_ref, sb_ref, own_ref):
        tot = g_ref[...] + r_ref[...]
        sb_ref[...] = tot.astype(BF16)

        @pl.when(pl.program_id(1) == chip_ref[0])
        def _():
            own_ref[...] = tot

    grid_spec = pltpu.PrefetchScalarGridSpec(
        num_scalar_prefetch=2, grid=(r // tr, N_SHARD),
        in_specs=[pl.BlockSpec((None, None, tr, n), lambda i, s, co, ch: (s, co[0], i, 0)),
                  pl.BlockSpec((None, tr, n), lambda i, s, co, ch: (s, i, 0))],
        out_specs=[pl.BlockSpec((None, tr, n), lambda i, s, co, ch: (s, i, 0)),
                   pl.BlockSpec((tr, n), lambda i, s, co, ch: (i, 0))])
    return pl.pallas_call(
        body, name=name, grid_spec=grid_spec,
        out_shape=[_sds((N_SHARD, r, n), BF16), _sds((r, n), F32)],
        compiler_params=_cp("arbitrary", "arbitrary"),
    )(core, chip, g, recv)


def _chip_sum(own, recv, core, name):
    r, n = own.shape
    tr = _row_tile(r, n)

    def body(core_ref, o_ref, r_ref, t_ref):
        acc = o_ref[...]
        for j in range(3):
            acc = acc + r_ref[j].astype(F32)
        t_ref[...] = acc

    grid_spec = pltpu.PrefetchScalarGridSpec(
        num_scalar_prefetch=1, grid=(r // tr,),
        in_specs=[pl.BlockSpec((tr, n), lambda i, co: (i, 0)), pl.BlockSpec((3, tr, n), lambda i, co: (0, i, 0))],
        out_specs=pl.BlockSpec((None, tr, n), lambda i, co: (co[0], i, 0)))
    return pl.pallas_call(
        body, name=name, grid_spec=grid_spec, out_shape=_sds((2, r, n), F32),
        compiler_params=_cp("arbitrary"),
    )(core, own, recv)


_HBM = pl.BlockSpec(memory_space=pltpu.HBM)
_SEM = pl.BlockSpec(memory_space=pltpu.SEMAPHORE)
_EFFECT = pltpu.SideEffectType.DATAFLOW_SIDE_EFFECTING


def _ici_copies(srcs, dsts, send_sems, recv_sems, send_view, recv_view):
    x, y, c = _coords()
    out = []
    for t in range(len(srcs)):
        for j, chip in enumerate(_other_chips(x, y)):
            out.append(pltpu.make_async_remote_copy(
                src_ref=send_view(srcs[t], chip, j, (x, y), c), dst_ref=recv_view(dsts[t], chip, j, (x, y), c),
                send_sem=send_sems.at[3 * t + j], recv_sem=recv_sems.at[3 * t + j],
                device_id=(*chip, c), device_id_type=MESH))
    return out


def _ici_start(srcs, dsts, after, send_view, recv_view, name):
    nt = len(srcs)
    inplace = dsts is None
    nbuf = nt if inplace else 2 * nt

    def body(*refs):
        send_sems, recv_sems = refs[nbuf + 1], refs[nbuf + 2]
        s_out = refs[nbuf + 3:nbuf + 3 + nt]
        d_out = s_out if inplace else refs[nbuf + 3 + nt:nbuf + 3 + 2 * nt]
        token = refs[-1]
        for cp in _ici_copies(s_out, d_out, send_sems, recv_sems, send_view, recv_view):
            cp.start()
        token[...] = jnp.zeros_like(token)

    bufs = list(srcs) + ([] if inplace else list(dsts))
    res = pl.pallas_call(
        body, name=name,
        out_shape=(pltpu.SemaphoreType.DMA((3 * nt,)), pltpu.SemaphoreType.DMA((3 * nt,)),
                   *[pltpu.HBM(b.shape, b.dtype) for b in bufs], _sds((8, 128), F32)),
        in_specs=[_HBM] * nbuf + [pl.BlockSpec(memory_space=pl.ANY)],
        out_specs=(_SEM, _SEM, *[_HBM] * nbuf, pl.BlockSpec(memory_space=pltpu.VMEM)),
        input_output_aliases={i: 2 + i for i in range(nbuf)},
        compiler_params=pltpu.CompilerParams(has_side_effects=_EFFECT),
    )(*[pltpu.with_memory_space_constraint(b, pltpu.HBM) for b in bufs], after)
    send_sems, recv_sems = res[0], res[1]
    s_thru = list(res[2:2 + nt])
    d_thru = s_thru if inplace else list(res[2 + nt:2 + 2 * nt])
    return send_sems, recv_sems, s_thru, d_thru, res[-1]


def _ici_wait(send_sems, recv_sems, srcs, dsts, after, send_view, recv_view, name):
    nt = len(srcs)
    inplace = dsts is None
    nbuf = nt if inplace else 2 * nt

    def body(*refs):
        send_ref, recv_ref = refs[nbuf], refs[nbuf + 1]
        s_out = refs[nbuf + 3:nbuf + 3 + nt]
        d_out = s_out if inplace else refs[nbuf + 3 + nt:nbuf + 3 + 2 * nt]
        for cp in _ici_copies(s_out, d_out, send_ref, recv_ref, send_view, recv_view):
            cp.wait_send()
            cp.wait_recv()

    bufs = list(srcs) + ([] if inplace else list(dsts))
    res = pl.pallas_call(
        body, name=name,
        out_shape=tuple(pltpu.HBM(b.shape, b.dtype) for b in bufs),
        in_specs=[_HBM] * nbuf + [_SEM, _SEM, pl.BlockSpec(memory_space=pl.ANY)],
        out_specs=tuple([_HBM] * nbuf),
        input_output_aliases={i: i for i in range(nbuf)},
        compiler_params=pltpu.CompilerParams(has_side_effects=_EFFECT),
    )(*bufs, send_sems, recv_sems, after)
    return list(res[:nt]) if inplace else list(res[nt:])


def _w_half(buf, chip, c):
    r = buf.shape[1] // 2
    return buf.at[2 * chip[0] + chip[1], pl.ds(c * r, r)]


def _ag_send_view(buf, chip, j, me, c):
    return _w_half(buf, me, c)


def _ag_recv_view(buf, chip, j, me, c):
    return _w_half(buf, me, c)


def _rs_send_view(buf, chip, j, me, c):
    return buf.at[2 * chip[0] + chip[1]]


def _rs_recv_view(buf, chip, j, me, c):
    return buf.at[j]


def _ag_forward(bufs, name):
    nt = len(bufs)
    hom = [pl.BlockSpec(memory_space=pl.ANY)] * nt

    def body(*refs):
        outs = refs[nt:2 * nt]
        send_sems, recv_sems = refs[2 * nt:]
        x, y, c = _coords()
        sibling = (x, y, 1 - c)
        chips = _other_chips(x, y)

        def copy(t, j, hc):
            blk = _w_half(outs[t], chips[j], hc)
            return pltpu.make_async_remote_copy(
                src_ref=blk, dst_ref=blk, send_sem=send_sems.at[t, j], recv_sem=recv_sems.at[t, j],
                device_id=sibling, device_id_type=MESH)

        started = [copy(t, j, c) for t in range(nt) for j in range(3)]
        for cp in started:
            cp.start()
        for t in range(nt):
            for j in range(3):
                copy(t, j, 1 - c).wait_recv()
        for cp in started:
            cp.wait_send()

    return pl.pallas_call(
        body, name=name,
        out_shape=[_sds(b.shape, b.dtype) for b in bufs],
        in_specs=hom, out_specs=hom,
        input_output_aliases={t: t for t in range(nt)},
        scratch_shapes=[pltpu.SemaphoreType.DMA((nt, 3)), pltpu.SemaphoreType.DMA((nt, 3))],
    )(*bufs)


def _rs_begin(grads, after, tag):
    x, y, c = _coords()
    core = jnp.reshape(c, (1,)).astype(jnp.int32)
    chip = jnp.reshape(2 * x + y, (1,)).astype(jnp.int32)
    recv = _swap_halves(grads, name="rs_swap_" + tag)
    sums, owns = [], []
    for t, (g, rv) in enumerate(zip(grads, recv)):
        r = g.shape[1] // 2
        sb, own = _pair_sum(g.reshape(N_SHARD, 2, r, g.shape[2]), rv, core, chip, name=f"rs_pair_{tag}_{t}")
        sums.append(sb)
        owns.append(own)
    land = [lax.empty((3,) + s.shape[1:], s.dtype) for s in sums]
    send_sems, recv_sems, s_thru, d_thru, token = _ici_start(
        sums, land, after, _rs_send_view, _rs_recv_view, name="rs_start_" + tag)
    return dict(sems=(send_sems, recv_sems), sums=s_thru, land=d_thru, owns=owns, core=core, tag=tag), token


def _rs_end(state, after):
    tag = state["tag"]
    got = _ici_wait(*state["sems"], state["sums"], state["land"], after, _rs_send_view, _rs_recv_view,
                    name="rs_wait_" + tag)
    tots = [_chip_sum(o, gt, state["core"], name=f"rs_chip_{tag}_{t}")
            for t, (o, gt) in enumerate(zip(state["owns"], got))]
    full = _join_halves(tots, name="rs_join_" + tag)
    return [f.reshape(2 * f.shape[1], f.shape[2]) for f in full]


def _rope_lane_table():
    d = jnp.arange(128) % HEAD
    inv_freq = ROPE_THETA ** (-jnp.arange(0, ROT, 2, dtype=F32) / ROT)
    rot = d < ROT
    rows = [jnp.where(rot, inv_freq[d % (ROT // 2)], 0.0), rot.astype(F32),
            (d < ROT // 2).astype(F32), jnp.logical_and(d >= ROT // 2, rot).astype(F32)]
    return jnp.concatenate([jnp.stack(rows), jnp.zeros((4, 128), F32)], axis=0)


def _pad8(rows):
    return jnp.concatenate([rows, jnp.zeros((8 - rows.shape[0], rows.shape[1]), F32)], axis=0)


def kernel(x, c, positions, ada_w, ada_b, w_in, b_in, sinks, pool_w, pool_scale, w_out, w_gate, w_up, w_down, g_pre_mix, g_post_mix, g_pre_ffn, g_post_ffn, loss_target, m_ada_w, m_ada_b, m_w_in, m_b_in, m_sinks, m_pool_w, m_pool_scale, m_w_out, m_w_gate, m_w_up, m_w_down, m_g_pre_mix, m_g_post_mix, m_g_pre_ffn, m_g_post_ffn, v_ada_w, v_ada_b, v_w_in, v_b_in, v_sinks, v_pool_w, v_pool_scale, v_w_out, v_w_gate, v_w_up, v_w_down, v_g_pre_mix, v_g_post_mix, v_g_pre_ffn, v_g_post_ffn):
    T = x.shape[1]
    n_layers = ada_w.shape[0]
    ax, ay, ac = _coords()
    my_dev = 4 * ax + 2 * ay + ac
    my_chip = 2 * ax + ay
    x0 = x.reshape(T, D_MODEL)
    target = loss_target.reshape(T, D_MODEL)

    c_all = _allgather8(c.reshape(8, 128), name="ag_c").reshape(N_DEV, D_MODEL)
    ada_b_sh = lax.dynamic_slice_in_dim(ada_b, my_chip * ADA_SH, ADA_SH, axis=1).reshape(n_layers, 1, ADA_SH)
    mod_part = _mod_fwd(c_all, ada_w, ada_b_sh)
    mod_all = _allgather8(mod_part.reshape(n_layers * 8, ADA_SH), name="ag_mod")
    mod_all = mod_all.reshape(N_DEV, n_layers, 8, ADA_SH)[0::2]
    mod_mine = lax.dynamic_index_in_dim(mod_all, my_dev, axis=2, keepdims=False)
    mod = jnp.transpose(mod_mine, (1, 0, 2)).reshape(n_layers, 6, D_MODEL)

    pos_b = jnp.broadcast_to(positions.reshape(T, 1), (T, 128))
    rc, rs1, rs2 = _rope_tables(pos_b, _rope_lane_table())

    chip1 = jnp.reshape(my_chip, (1,)).astype(jnp.int32)

    def cast_layer(l):
        return [_cast_slot(w[l], chip1, name=f"cast_{nm}{l}")
                for nm, w in (("w_in", w_in), ("w_out", w_out), ("w_gate", w_gate), ("w_up", w_up), ("w_down", w_down))]

    def as_operands(bufs):
        gin, gout, gg, gu, gd = bufs
        win_full = jnp.transpose(gin, (1, 0, 2)).reshape(D_MODEL, IN_W)
        return win_full, gout.reshape(D_MODEL, D_MODEL), gg, gu, gd

    weights = [as_operands(_allgather_weights(cast_layer(0), name="ag_w0"))]

    saved = []
    xl = x0
    for l in range(n_layers):
        win, wout, wg, wu, wd = weights[l]
        mod8 = _pad8(mod[l])
        if l + 1 < n_layers:
            ag_send, ag_recv, ag_bufs, _, ag_token = _ici_start(
                cast_layer(l + 1), None, wd, _ag_send_view, _ag_recv_view, name=f"ag_start_{l + 1}")
            mod8 = mod8 + ag_token[0, 0]
        g8 = _pad8(jnp.stack([g_pre_mix[l], g_post_mix[l], g_pre_ffn[l], g_post_ffn[l]]))
        sink_b = jnp.broadcast_to(sinks[l][:, None], (N_HEADS, 128))
        psc = pool_scale[l].reshape(1, POOL_W)
        h, q, k, v, u = _fwd_in(xl, mod8, g8, win, b_in[l].reshape(1, IN_W), rc, rs1, rs2)
        attn, lse = _attn_fwd(q, k, v, sink_b)
        pool, pooled = _pool_fwd(u, pool_w[l], psc)
        mix, x1 = _fwd_out(attn, pool, xl, wout, g8, mod8)
        a, b, f, x2 = _ffn_fwd(x1, mod8, g8, wg, wu, wd)
        saved.append(dict(x=xl, h=h, q=q, k=k, v=v, lse=lse, attn=attn, pool=pool, pooled=pooled, mix=mix,
                          x1=x1, a=a, b=b, f=f, mod8=mod8, g8=g8, sink_b=sink_b, psc=psc))
        xl = x2
        if l + 1 < n_layers:
            arrived = _ici_wait(ag_send, ag_recv, ag_bufs, None, x2, _ag_send_view, _ag_recv_view,
                                name=f"ag_wait_{l + 1}")
            weights.append(as_operands(_ag_forward(arrived, name=f"ag_fwd_{l + 1}")))

    dy, loss_tile = _loss_grad(xl, target)
    loss = lax.psum(loss_tile[0, 0], ("x", "y", "c"))

    small = [None] * n_layers
    dmod_rows = [None] * n_layers
    reduced = [None] * n_layers
    in_flight = None
    dx = dy
    for l in reversed(range(n_layers)):
        s = saved[l]
        win, wout, wg, wu, wd = weights[l]
        if in_flight is not None:
            s = dict(s, mod8=s["mod8"] + in_flight[1][0, 0])
        df, da, db, act, red_a = _ffn_bwd_act(dx, s["f"], s["a"], s["b"], s["mod8"], s["g8"], wd)
        dx1, h2, red_b = _ffn_bwd_in(da, db, s["x1"], dx, s["mod8"], s["g8"], wg, wu)
        g_wd = _wgrad_rows(act, df, name="wgrad_down")
        g_wg = _wgrad_cols(h2, da, name="wgrad_gate")
        g_wu = _wgrad_cols(h2, db, name="wgrad_up")
        dmix, dattn, dpool, red_c = _mix_bwd(dx1, s["mix"], s["mod8"], s["g8"], wout)
        g_wout = jnp.concatenate([_wgrad(s["attn"], dmix, name="wgrad_out_a"),
                                  _wgrad(s["pool"], dmix, name="wgrad_out_p")], axis=0)
        dq, dk, dv, dsink = _attn_bwd(s["q"], s["k"], s["v"], s["lse"], dattn, s["sink_b"])
        du, g_poolw, dpsc = _pool_bwd(dpool, s["pooled"], pool_w[l], s["psc"])
        dx, dproj, red_d, dbin = _in_bwd(dq, dk, dv, du, rc, rs1, rs2, s["x"], dx1, s["mod8"], s["g8"], win)
        g_win = _wgrad(s["h"], dproj, name="wgrad_in")
        g_win_sh = jnp.transpose(g_win.reshape(D_MODEL, N_SHARD, IN_SH), (1, 0, 2))
        if in_flight is not None:
            reduced[l + 1] = _rs_end(in_flight[0], dx)
        in_flight = _rs_begin([g_win_sh, g_wout.reshape(N_SHARD, OUT_SH, D_MODEL), g_wg, g_wu, g_wd], dx, tag=str(l))
        dmod_rows[l] = jnp.concatenate([red_d[0], red_d[1], red_c[0], red_b[0], red_b[1], red_a[0]])
        small[l] = jnp.concatenate([red_d[2], red_c[1], red_b[2], red_a[1], dbin[0], dpsc[0], dsink[:, 0],
                                    jnp.zeros((120,), F32), g_poolw.reshape(-1)])
    grad_x = dx.reshape(1, T, D_MODEL)

    per_layer = small[0].shape[0]
    rows_small = n_layers * per_layer // 128
    rows_mod = n_layers * 6 * D_MODEL // 128
    rows_pad = -(rows_small + rows_mod) % 8
    pack = jnp.concatenate(small + dmod_rows + [jnp.zeros((rows_pad * 128,), F32)]).reshape(-1, 128)
    pack = pack + in_flight[1][0, 0]
    gathered = _allgather8(pack, name="ag_small").reshape(N_DEV, pack.shape[0], 128)
    summed = _sum_devices(gathered)
    small_sum = summed[:rows_small].reshape(n_layers, per_layer)
    o = 0
    small_g = {}
    for nm, width in (("g_pre_mix", D_MODEL), ("g_post_mix", D_MODEL), ("g_pre_ffn", D_MODEL),
                      ("g_post_ffn", D_MODEL), ("b_in", IN_W), ("pool_scale", POOL_W), ("sinks", 128),
                      ("pool_w", 4 * 128 * 128)):
        small_g[nm] = small_sum[:, o:o + width]
        o += width
    small_g["sinks"] = small_g["sinks"][:, :N_HEADS]
    small_g["pool_w"] = small_g["pool_w"].reshape(n_layers, 4, 128, 128)
    small_g["ada_b"] = summed[rows_small:rows_small + rows_mod].reshape(n_layers, 6 * D_MODEL)
    dmod_all = gathered[:, rows_small:rows_small + rows_mod].reshape(N_DEV, n_layers, N_SHARD, ADA_SH)
    dmod_sh = lax.dynamic_index_in_dim(dmod_all, my_chip, axis=2, keepdims=False)
    g_ada_w = _ada_wgrad(jnp.transpose(c_all), jnp.transpose(dmod_sh, (1, 0, 2)))

    grads = dict(ada_w=g_ada_w, ada_b=small_g["ada_b"], b_in=small_g["b_in"], sinks=small_g["sinks"],
                 pool_w=small_g["pool_w"], pool_scale=small_g["pool_scale"], g_pre_mix=small_g["g_pre_mix"],
                 g_post_mix=small_g["g_post_mix"], g_pre_ffn=small_g["g_pre_ffn"], g_post_ffn=small_g["g_post_ffn"])
    params = dict(ada_w=(ada_w, m_ada_w, v_ada_w), ada_b=(ada_b, m_ada_b, v_ada_b), w_in=(w_in, m_w_in, v_w_in),
                  b_in=(b_in, m_b_in, v_b_in), sinks=(sinks, m_sinks, v_sinks), pool_w=(pool_w, m_pool_w, v_pool_w),
                  pool_scale=(pool_scale, m_pool_scale, v_pool_scale), w_out=(w_out, m_w_out, v_w_out),
                  w_gate=(w_gate, m_w_gate, v_w_gate), w_up=(w_up, m_w_up, v_w_up),
                  w_down=(w_down, m_w_down, v_w_down), g_pre_mix=(g_pre_mix, m_g_pre_mix, v_g_pre_mix),
                  g_post_mix=(g_post_mix, m_g_post_mix, v_g_post_mix), g_pre_ffn=(g_pre_ffn, m_g_pre_ffn, v_g_pre_ffn),
                  g_post_ffn=(g_post_ffn, m_g_post_ffn, v_g_post_ffn))
    names = list(params)
    updates = {nm: _adamw_nd(*params[nm][:1], grads[nm], *params[nm][1:], name="adamw_" + nm) for nm in grads}

    reduced[0] = _rs_end(in_flight[0], updates["ada_w"][0])
    for t, nm in enumerate(("w_in", "w_out", "w_gate", "w_up", "w_down")):
        grads[nm] = jnp.stack([reduced[l][t] for l in range(n_layers)])
        updates[nm] = _adamw_nd(*params[nm][:1], grads[nm], *params[nm][1:], name="adamw_" + nm)
    return (loss, grad_x, *[grads[nm] for nm in names], *[updates[nm][0] for nm in names],
            *[updates[nm][1] for nm in names], *[updates[nm][2] for nm in names])
```

```python
import functools

import jax
import jax.numpy as jnp
from jax import lax
from jax.experimental import pallas as pl
from jax.experimental.pallas import tpu as pltpu

F32 = jnp.float32
BF16 = jnp.bfloat16
MESH = pl.DeviceIdType.MESH

D_MODEL = 1024
ATTN_W = 512
KV_W = 128
KVD_W = 256
POOL_W = 512
IN_W = 1280
D_FF = 2816
N_SHARD = 4
FF_SH = D_FF // N_SHARD
IN_SH = IN_W // N_SHARD
OUT_SH = D_MODEL // N_SHARD
ADA_SH = 6 * D_MODEL // N_SHARD
HEAD = 64
N_HEADS = 8
GROUP = 4
BLK = 128
POOL_WINDOWS = (2, 4, 8, 16)
HALO = 16
ROT = 16
ROPE_THETA = 500000.0
EPS = 1e-6
NEG_INF = -1e30
N_DEV = 8

ADAM_LR = 0.001
ADAM_B1 = 0.9
ADAM_B2 = 0.999
ADAM_EPS = 1e-08
ADAM_WD = 0.01
ADAM_STEP = 10

VMEM_LIMIT = 48 * 1024 * 1024


def _cp(*sem):
    return pltpu.CompilerParams(dimension_semantics=sem, vmem_limit_bytes=VMEM_LIMIT)


def _full(shape):
    nd = len(shape)
    return pl.BlockSpec(shape, lambda *_: (0,) * nd)


def _resident(shape):
    nd = len(shape)
    return pl.BlockSpec(shape, lambda *_: (0,) * nd, pipeline_mode=pl.Buffered(1))


def _rows(tm, ncol):
    return pl.BlockSpec((tm, ncol), lambda i: (i, 0))


def _sds(shape, dtype):
    return jax.ShapeDtypeStruct(shape, dtype)


def _nt(a, b):
    return lax.dot_general(a, b, (((1,), (1,)), ((), ())), preferred_element_type=F32)


def _tn(a, b):
    return lax.dot_general(a, b, (((0,), (0,)), ((), ())), preferred_element_type=F32)


def _mm(a, b):
    return jnp.dot(a, b, preferred_element_type=F32)


def _rstd(x):
    return lax.rsqrt(jnp.mean(x * x, axis=-1, keepdims=True) + EPS)


def _colsum(x):
    return jnp.sum(x, axis=0, keepdims=True)


def _norm_bwd(dhat, xhat, rstd):
    return rstd * (dhat - xhat * jnp.mean(dhat * xhat, axis=-1, keepdims=True))


def _rope_tables(pos_b, lane_tab):
    T = pos_b.shape[0]
    tm = min(T, 1024)

    def body(pos_ref, tab_ref, c_ref, s1_ref, s2_ref):
        ang = pos_ref[...].astype(F32) * tab_ref[0:1, :]
        cs = jnp.cos(ang)
        sn = jnp.sin(ang)
        m_rot = tab_ref[1:2, :]
        c_ref[...] = cs * m_rot + (1.0 - m_rot)
        s1_ref[...] = -sn * tab_ref[2:3, :]
        s2_ref[...] = sn * tab_ref[3:4, :]

    out = _sds((T, 128), F32)
    return pl.pallas_call(
        body, name="rope_tables", grid=(T // tm,),
        in_specs=[_rows(tm, 128), _full((8, 128))],
        out_specs=[_rows(tm, 128)] * 3, out_shape=[out] * 3,
        compiler_params=_cp("parallel"),
    )(pos_b, lane_tab)


def _rot_fwd(t, c, s1, s2):
    w = t.shape[-1]
    return t * c + pltpu.roll(t, w - 8, 1) * s1 + pltpu.roll(t, 8, 1) * s2


def _rot_bwd(d, c, s1, s2):
    w = d.shape[-1]
    return d * c + pltpu.roll(d * s1, 8, 1) + pltpu.roll(d * s2, w - 8, 1)


def _store_dup(ref, t):
    low = lax.broadcasted_iota(jnp.int32, t.shape, 1) < HEAD
    sw = pltpu.roll(t, HEAD, 1)
    ref[:, 0:128] = jnp.where(low, t, sw).astype(BF16)
    ref[:, 128:256] = jnp.where(low, sw, t).astype(BF16)


def _fold_dup(d):
    low = lax.broadcasted_iota(jnp.int32, (d.shape[0], 128), 1) < HEAD
    d0 = d[:, 0:128]
    d1 = d[:, 128:256]
    return jnp.where(low, d0 + pltpu.roll(d0, HEAD, 1), d1 + pltpu.roll(d1, HEAD, 1))


def _fwd_in(x, mod8, g8, w_in, b_in, rc, rs1, rs2):
    T = x.shape[0]
    tm = min(T, 512)

    def body(x_ref, mod_ref, g_ref, w_ref, b_ref, c_ref, s1_ref, s2_ref,
             h_ref, q_ref, k_ref, v_ref, u_ref):
        xf = x_ref[...]
        h = (xf * _rstd(xf) * g_ref[0:1, :]) * (1.0 + mod_ref[1:2, :]) + mod_ref[0:1, :]
        hb = h.astype(BF16)
        h_ref[...] = hb
        c = c_ref[...]
        s1 = s1_ref[...]
        s2 = s2_ref[...]
        q = _mm(hb, w_ref[:, 0:ATTN_W]) + b_ref[:, 0:ATTN_W]
        q = _rot_fwd(q, jnp.tile(c, (1, 4)), jnp.tile(s1, (1, 4)), jnp.tile(s2, (1, 4)))
        q_ref[...] = (q * (HEAD ** -0.5)).astype(BF16)
        k = _mm(hb, w_ref[:, ATTN_W:ATTN_W + KV_W]) + b_ref[:, ATTN_W:ATTN_W + KV_W]
        _store_dup(k_ref, _rot_fwd(k, c, s1, s2))
        v = _mm(hb, w_ref[:, ATTN_W + KV_W:ATTN_W + 2 * KV_W]) + b_ref[:, ATTN_W + KV_W:ATTN_W + 2 * KV_W]
        _store_dup(v_ref, v)
        u_ref[...] = _mm(hb, w_ref[:, ATTN_W + 2 * KV_W:IN_W]) + b_ref[:, ATTN_W + 2 * KV_W:IN_W]

    return pl.pallas_call(
        body, name="fwd_in", grid=(T // tm,),
        in_specs=[_rows(tm, D_MODEL), _full((8, D_MODEL)), _full((8, D_MODEL)),
                  _resident((D_MODEL, IN_W)), _full((1, IN_W)),
                  _rows(tm, 128), _rows(tm, 128), _rows(tm, 128)],
        out_specs=[_rows(tm, D_MODEL), _rows(tm, ATTN_W), _rows(tm, KVD_W), _rows(tm, KVD_W), _rows(tm, POOL_W)],
        out_shape=[_sds((T, D_MODEL), BF16), _sds((T, ATTN_W), BF16), _sds((T, KVD_W), BF16),
                   _sds((T, KVD_W), BF16), _sds((T, POOL_W), F32)],
        compiler_params=_cp("parallel"),
    )(x, mod8, g8, w_in, b_in, rc, rs1, rs2)


def _band_mask(n):
    row = lax.broadcasted_iota(jnp.int32, (BLK, 2 * BLK), 0)
    col = lax.broadcasted_iota(jnp.int32, (BLK, 2 * BLK), 1)
    first = jnp.where(n > 0, 0, 2 * BLK)
    in_prev = jnp.logical_and(col < BLK, col > row + first)
    in_cur = jnp.logical_and(col >= BLK, (col - BLK) <= row)
    return jnp.logical_or(in_prev, in_cur)


def _stack_heads(x_ref, j):
    low = lax.broadcasted_iota(jnp.int32, (BLK, 128), 1) < HEAD
    parts = []
    for gp in (2 * j, 2 * j + 1):
        x2 = x_ref[:, gp * 128:(gp + 1) * 128]
        parts.append(jnp.where(low, x2, jnp.zeros_like(x2)))
        parts.append(jnp.where(low, jnp.zeros_like(x2), x2))
    return jnp.concatenate(parts, axis=0)


def _unstack_heads(o):
    low = lax.broadcasted_iota(jnp.int32, (BLK, 128), 1) < HEAD
    return [jnp.where(low, o[0:BLK], o[BLK:2 * BLK]), jnp.where(low, o[2 * BLK:3 * BLK], o[3 * BLK:4 * BLK])]


def _sink_rows(sk_ref, j):
    return jnp.concatenate([jnp.broadcast_to(sk_ref[GROUP * j + r:GROUP * j + r + 1, 0:1], (BLK, 1))
                            for r in range(GROUP)], axis=0)


def _attn_fwd(q, kd, vd, sink_b):
    T = q.shape[0]
    nb = T // BLK

    def body(q_ref, kp_ref, kc_ref, vp_ref, vc_ref, sk_ref, o_ref, lse_ref):
        n = pl.program_id(0)
        valid = jnp.concatenate([_band_mask(n)] * GROUP, axis=0)
        lane = lax.broadcasted_iota(jnp.int32, (BLK, 128), 1)
        lse_all = jnp.zeros((BLK, 128), F32)
        for j in range(N_HEADS // GROUP):
            lanes = slice(j * 128, (j + 1) * 128)
            kcat = jnp.concatenate([kp_ref[:, lanes], kc_ref[:, lanes]], axis=0)
            vcat = jnp.concatenate([vp_ref[:, lanes], vc_ref[:, lanes]], axis=0)
            s = jnp.where(valid, _nt(_stack_heads(q_ref, j), kcat), NEG_INF)
            sk = _sink_rows(sk_ref, j)
            m = jnp.maximum(jnp.max(s, axis=-1, keepdims=True), sk)
            p = jnp.exp(s - m)
            den = jnp.sum(p, axis=-1, keepdims=True) + jnp.exp(sk - m)
            p = p * (1.0 / den)
            o = _mm(p.astype(BF16), vcat)
            o_ref[:, 2 * j * 128:(2 * j + 2) * 128] = jnp.concatenate(_unstack_heads(o), axis=1).astype(BF16)
            lse = m + jnp.log(den)
            for r in range(GROUP):
                lse_all = jnp.where(lane == GROUP * j + r, lse[r * BLK:(r + 1) * BLK], lse_all)
        lse_ref[...] = lse_all

    prev = lambda n: (jnp.maximum(n - 1, 0), 0)
    cur = lambda n: (n, 0)
    return pl.pallas_call(
        body, name="attn_fwd", grid=(nb,),
        in_specs=[pl.BlockSpec((BLK, ATTN_W), cur),
                  pl.BlockSpec((BLK, KVD_W), prev), pl.BlockSpec((BLK, KVD_W), cur),
                  pl.BlockSpec((BLK, KVD_W), prev), pl.BlockSpec((BLK, KVD_W), cur),
                  _full((8, 128))],
        out_specs=[pl.BlockSpec((BLK, ATTN_W), cur), pl.BlockSpec((BLK, 128), cur)],
        out_shape=[_sds((T, ATTN_W), BF16), _sds((T, 128), F32)],
        compiler_params=_cp("parallel"),
    )(q, kd, kd, vd, vd, sink_b)


def _pool_fwd(u, pool_w, pool_scale):
    T = u.shape[0]
    tm = min(T, 512)

    def body(u_ref, w_ref, sc_ref, out_ref, pooled_ref, halo):
        i = pl.program_id(0)

        @pl.when(i == 0)
        def _():
            halo[...] = jnp.zeros_like(halo)

        ub = u_ref[...]
        ext = jnp.concatenate([halo[...], ub], axis=0)
        halo[...] = ub[tm - HALO:, :]
        tpos = (i * tm + lax.broadcasted_iota(jnp.int32, (tm, 1), 0)).astype(F32)
        for g, w in enumerate(POOL_WINDOWS):
            lanes = slice(g * 128, (g + 1) * 128)
            s = ext[:, lanes]
            sh = 1
            while sh < w:
                s = s + pltpu.roll(s, sh, 0)
                sh *= 2
            cnt = jnp.minimum(tpos + 1.0, float(w))
            pb = (s[HALO:, :] / cnt - ub[:, lanes]).astype(BF16)
            z = _mm(pb, w_ref[g].astype(BF16))
            out_ref[:, lanes] = (z * sc_ref[:, lanes]).astype(BF16)
            pooled_ref[:, lanes] = pb

    return pl.pallas_call(
        body, name="pool_fwd", grid=(T // tm,),
        in_specs=[_rows(tm, POOL_W), _full((4, 128, 128)), _full((1, POOL_W))],
        out_specs=[_rows(tm, POOL_W), _rows(tm, POOL_W)],
        out_shape=[_sds((T, POOL_W), BF16), _sds((T, POOL_W), BF16)],
        scratch_shapes=[pltpu.VMEM((HALO, POOL_W), F32)],
        compiler_params=_cp("arbitrary"),
    )(u, pool_w, pool_scale)


def _fwd_out(attn, pool, x, w_out, g8, mod8):
    T = x.shape[0]
    tm = min(T, 512)

    def body(a_ref, p_ref, x_ref, w_ref, g_ref, mod_ref, mix_ref, x1_ref):
        mix = _mm(a_ref[...], w_ref[0:ATTN_W, :]) + _mm(p_ref[...], w_ref[ATTN_W:, :])
        mix_ref[...] = mix
        x1_ref[...] = x_ref[...] + mod_ref[2:3, :] * (mix * _rstd(mix) * g_ref[1:2, :])

    return pl.pallas_call(
        body, name="fwd_out", grid=(T // tm,),
        in_specs=[_rows(tm, ATTN_W), _rows(tm, POOL_W), _rows(tm, D_MODEL),
                  _resident((D_MODEL, D_MODEL)), _full((8, D_MODEL)), _full((8, D_MODEL))],
        out_specs=[_rows(tm, D_MODEL), _rows(tm, D_MODEL)],
        out_shape=[_sds((T, D_MODEL), F32), _sds((T, D_MODEL), F32)],
        compiler_params=_cp("parallel"),
    )(attn, pool, x, w_out, g8, mod8)


def _sh_rows(tm):
    return pl.BlockSpec((N_SHARD, tm, FF_SH), lambda i: (0, i, 0))


def _ffn_fwd(x1, mod8, g8, wg, wu, wd):
    T = x1.shape[0]
    tm = min(T, 256)

    def body(x_ref, mod_ref, g_ref, wg_ref, wu_ref, wd_ref, h_ref, act_ref, ga_ref, gb_ref, f_ref, x2_ref):
        xf = x_ref[...]
        h = (xf * _rstd(xf) * g_ref[2:3, :]) * (1.0 + mod_ref[4:5, :]) + mod_ref[3:4, :]
        hb = h.astype(BF16)
        h_ref[...] = hb
        f = jnp.zeros((tm, D_MODEL), F32)
        for s in range(N_SHARD):
            a = _mm(hb, wg_ref[s])
            b = _mm(hb, wu_ref[s])
            sig = jax.nn.sigmoid(a)
            sl = a * sig
            act = (sl * b).astype(BF16)
            act_ref[s] = act
            ga_ref[s] = (b * (sig * (1.0 + a * (1.0 - sig)))).astype(BF16)
            gb_ref[s] = sl.astype(BF16)
            f = f + _mm(act, wd_ref[s])
        f_ref[...] = f
        x2_ref[...] = xf + mod_ref[5:6, :] * (f * _rstd(f) * g_ref[3:4, :])

    act_shape = _sds((N_SHARD, T, FF_SH), BF16)
    return pl.pallas_call(
        body, name="ffn_fwd", grid=(T // tm,),
        in_specs=[_rows(tm, D_MODEL), _full((8, D_MODEL)), _full((8, D_MODEL)),
                  _resident((N_SHARD, D_MODEL, FF_SH)), _resident((N_SHARD, D_MODEL, FF_SH)),
                  _resident((N_SHARD, FF_SH, D_MODEL))],
        out_specs=[_rows(tm, D_MODEL), _sh_rows(tm), _sh_rows(tm), _sh_rows(tm), _rows(tm, D_MODEL),
                   _rows(tm, D_MODEL)],
        out_shape=[_sds((T, D_MODEL), BF16), act_shape, act_shape, act_shape, _sds((T, D_MODEL), F32),
                   _sds((T, D_MODEL), F32)],
        compiler_params=_cp("parallel"),
    )(x1, mod8, g8, wg, wu, wd)


def _loss_grad(y, target):
    T = y.shape[0]
    tm = min(T, 1024)

    def body(y_ref, t_ref, dy_ref, loss_ref):
        @pl.when(pl.program_id(0) == 0)
        def _():
            loss_ref[...] = jnp.zeros_like(loss_ref)

        e = y_ref[...] - t_ref[...]
        dy_ref[...] = e * (1.0 / D_MODEL)
        part = 0.5 * jnp.sum(jnp.mean(e * e, axis=-1, keepdims=True), axis=0, keepdims=True)
        loss_ref[...] += part

    return pl.pallas_call(
        body, name="loss_grad", grid=(T // tm,),
        in_specs=[_rows(tm, D_MODEL), _rows(tm, D_MODEL)],
        out_specs=[_rows(tm, D_MODEL), _full((8, 128))],
        out_shape=[_sds((T, D_MODEL), F32), _sds((8, 128), F32)],
        compiler_params=_cp("arbitrary"),
    )(y, target)


def _ffn_bwd(dx2, f, ga, gb, x1, mod8, g8, wg, wu, wd):
    T = dx2.shape[0]
    tm = min(T, 256)

    def body(dx_ref, f_ref, ga_ref, gb_ref, x_ref, mod_ref, g_ref, wg_ref, wu_ref, wd_ref,
             dx1_ref, df_ref, da_ref, db_ref, red_ref):
        @pl.when(pl.program_id(0) == 0)
        def _():
            red_ref[...] = jnp.zeros_like(red_ref)

        dx = dx_ref[...]
        fv = f_ref[...]
        rstd = _rstd(fv)
        fhat = fv * rstd
        gpost = g_ref[3:4, :]
        red_ref[0:1, :] += _colsum(dx * (fhat * gpost))
        dn = dx * mod_ref[5:6, :]
        red_ref[1:2, :] += _colsum(dn * fhat)
        dfb = _norm_bwd(dn * gpost, fhat, rstd).astype(BF16)
        df_ref[...] = dfb
        dh = jnp.zeros((tm, D_MODEL), F32)
        for s in range(N_SHARD):
            dact = _nt(dfb, wd_ref[s])
            da = (dact * ga_ref[s].astype(F32)).astype(BF16)
            db = (dact * gb_ref[s].astype(F32)).astype(BF16)
            da_ref[s] = da
            db_ref[s] = db
            dh = dh + _nt(da, wg_ref[s]) + _nt(db, wu_ref[s])
        xf = x_ref[...]
        rstd1 = _rstd(xf)
        xhat = xf * rstd1
        gpre = g_ref[2:3, :]
        scale1 = 1.0 + mod_ref[4:5, :]
        red_ref[2:3, :] += _colsum(dh)
        red_ref[3:4, :] += _colsum(dh * (xhat * gpre))
        red_ref[4:5, :] += _colsum(dh * scale1 * xhat)
        dx1_ref[...] = dx + _norm_bwd(dh * scale1 * gpre, xhat, rstd1)

    act_shape = _sds((N_SHARD, T, FF_SH), BF16)
    return pl.pallas_call(
        body, name="ffn_bwd", grid=(T // tm,),
        in_specs=[_rows(tm, D_MODEL), _rows(tm, D_MODEL), _sh_rows(tm), _sh_rows(tm), _rows(tm, D_MODEL),
                  _full((8, D_MODEL)), _full((8, D_MODEL)),
                  _resident((N_SHARD, D_MODEL, FF_SH)), _resident((N_SHARD, D_MODEL, FF_SH)),
                  _resident((N_SHARD, FF_SH, D_MODEL))],
        out_specs=[_rows(tm, D_MODEL), _rows(tm, D_MODEL), _sh_rows(tm), _sh_rows(tm), _full((8, D_MODEL))],
        out_shape=[_sds((T, D_MODEL), F32), _sds((T, D_MODEL), BF16), act_shape, act_shape, _sds((8, D_MODEL), F32)],
        compiler_params=_cp("arbitrary"),
    )(dx2, f, ga, gb, x1, mod8, g8, wg, wu, wd)


def _wgrad(a, b, name):
    T, K = a.shape
    N = b.shape[1]
    tt = min(T, 1024)
    tk = min(K, 512)

    def body(a_ref, b_ref, o_ref):
        @pl.when(pl.program_id(1) == 0)
        def _():
            o_ref[...] = jnp.zeros_like(o_ref)

        o_ref[...] += _tn(a_ref[...], b_ref[...])

    return pl.pallas_call(
        body, name=name, grid=(K // tk, T // tt),
        in_specs=[pl.BlockSpec((tt, tk), lambda i, t: (t, i)), pl.BlockSpec((tt, N), lambda i, t: (t, 0))],
        out_specs=pl.BlockSpec((tk, N), lambda i, t: (i, 0)),
        out_shape=_sds((K, N), F32),
        compiler_params=_cp("parallel", "arbitrary"),
    )(a, b)


def _wgrad_cols(a, b, name):
    T, K = a.shape
    n = b.shape[2]
    tt = min(T, 1024)

    def body(a_ref, b_ref, o_ref):
        @pl.when(pl.program_id(1) == 0)
        def _():
            o_ref[...] = jnp.zeros_like(o_ref)

        o_ref[...] += _tn(a_ref[...], b_ref[...])

    return pl.pallas_call(
        body, name=name, grid=(N_SHARD, T // tt),
        in_specs=[pl.BlockSpec((tt, K), lambda s, t: (t, 0)), pl.BlockSpec((None, tt, n), lambda s, t: (s, t, 0))],
        out_specs=pl.BlockSpec((None, K, n), lambda s, t: (s, 0, 0)),
        out_shape=_sds((N_SHARD, K, n), F32),
        compiler_params=_cp("parallel", "arbitrary"),
    )(a, b)


def _wgrad_rows(a, b, name):
    T, N = b.shape
    k = a.shape[2]
    tt = min(T, 1024)

    def body(a_ref, b_ref, o_ref):
        @pl.when(pl.program_id(1) == 0)
        def _():
            o_ref[...] = jnp.zeros_like(o_ref)

        o_ref[...] += _tn(a_ref[...], b_ref[...])

    return pl.pallas_call(
        body, name=name, grid=(N_SHARD, T // tt),
        in_specs=[pl.BlockSpec((None, tt, k), lambda s, t: (s, t, 0)), pl.BlockSpec((tt, N), lambda s, t: (t, 0))],
        out_specs=pl.BlockSpec((None, k, N), lambda s, t: (s, 0, 0)),
        out_shape=_sds((N_SHARD, k, N), F32),
        compiler_params=_cp("parallel", "arbitrary"),
    )(a, b)


def _mix_bwd(dx1, mix, mod8, g8, w_out):
    T = dx1.shape[0]
    tm = min(T, 512)

    def body(dx_ref, mix_ref, mod_ref, g_ref, w_ref, dmix_ref, da_ref, dp_ref, red_ref):
        @pl.when(pl.program_id(0) == 0)
        def _():
            red_ref[...] = jnp.zeros_like(red_ref)

        dx = dx_ref[...]
        mv = mix_ref[...]
        rstd = _rstd(mv)
        mhat = mv * rstd
        gpost = g_ref[1:2, :]
        red_ref[0:1, :] += _colsum(dx * (mhat * gpost))
        dn = dx * mod_ref[2:3, :]
        red_ref[1:2, :] += _colsum(dn * mhat)
        dmb = _norm_bwd(dn * gpost, mhat, rstd).astype(BF16)
        dmix_ref[...] = dmb
        da_ref[...] = _nt(dmb, w_ref[0:ATTN_W, :]).astype(BF16)
        dp_ref[...] = _nt(dmb, w_ref[ATTN_W:, :]).astype(BF16)

    return pl.pallas_call(
        body, name="mix_bwd", grid=(T // tm,),
        in_specs=[_rows(tm, D_MODEL), _rows(tm, D_MODEL), _full((8, D_MODEL)), _full((8, D_MODEL)),
                  _resident((D_MODEL, D_MODEL))],
        out_specs=[_rows(tm, D_MODEL), _rows(tm, ATTN_W), _rows(tm, POOL_W), _full((8, D_MODEL))],
        out_shape=[_sds((T, D_MODEL), BF16), _sds((T, ATTN_W), BF16), _sds((T, POOL_W), BF16),
                   _sds((8, D_MODEL), F32)],
        compiler_params=_cp("arbitrary"),
    )(dx1, mix, mod8, g8, w_out)


def _attn_bwd(q, kd, vd, lse, dattn, sink_b):
    T = q.shape[0]
    nb = T // BLK

    def body(q_ref, do_ref, lse_ref, kp_ref, kc_ref, vp_ref, vc_ref, sk_ref,
             dq_ref, dk_ref, dv_ref, dsk_ref, carry_k, carry_v):
        n = pl.program_id(0)

        @pl.when(n == 0)
        def _():
            carry_k[...] = jnp.zeros_like(carry_k)
            carry_v[...] = jnp.zeros_like(carry_v)
            dsk_ref[...] = jnp.zeros_like(dsk_ref)

        @pl.when(n < nb)
        def _():
            valid = jnp.concatenate([_band_mask(n)] * GROUP, axis=0)
            lane = lax.broadcasted_iota(jnp.int32, (BLK, 128), 1)
            lse_all = lse_ref[...]
            for j in range(N_HEADS // GROUP):
                lanes = slice(j * 128, (j + 1) * 128)
                kcat = jnp.concatenate([kp_ref[:, lanes], kc_ref[:, lanes]], axis=0)
                vcat = jnp.concatenate([vp_ref[:, lanes], vc_ref[:, lanes]], axis=0)
                qs = _stack_heads(q_ref, j)
                dos = _stack_heads(do_ref, j)
                lse = jnp.concatenate(
                    [jnp.sum(jnp.where(lane == GROUP * j + r, lse_all, 0.0), axis=-1, keepdims=True)
                     for r in range(GROUP)], axis=0)
                p = jnp.exp(jnp.where(valid, _nt(qs, kcat), NEG_INF) - lse)
                dp = _nt(dos, vcat)
                delta = jnp.sum(p * dp, axis=-1, keepdims=True)
                ds = (p * (dp - delta)).astype(BF16)
                sink_term = jnp.exp(_sink_rows(sk_ref, j) - lse) * delta
                for r in range(GROUP):
                    h = GROUP * j + r
                    dsk_ref[h:h + 1, :] += -jnp.sum(sink_term[r * BLK:(r + 1) * BLK], axis=0, keepdims=True)
                dq_ref[:, 2 * j * 128:(2 * j + 2) * 128] = jnp.concatenate(_unstack_heads(_mm(ds, kcat)), axis=1)
                dk = _tn(ds, qs)
                dv = _tn(p.astype(BF16), dos)
                dk_ref[:, lanes] = carry_k[:, lanes] + dk[0:BLK]
                dv_ref[:, lanes] = carry_v[:, lanes] + dv[0:BLK]
                carry_k[:, lanes] = dk[BLK:]
                carry_v[:, lanes] = dv[BLK:]

        @pl.when(n == nb)
        def _():
            dk_ref[...] = carry_k[...]
            dv_ref[...] = carry_v[...]

    cur = lambda n: (jnp.minimum(n, nb - 1), 0)
    prev = lambda n: (jnp.maximum(n - 1, 0), 0)
    return pl.pallas_call(
        body, name="attn_bwd", grid=(nb + 1,),
        in_specs=[pl.BlockSpec((BLK, ATTN_W), cur), pl.BlockSpec((BLK, ATTN_W), cur), pl.BlockSpec((BLK, 128), cur),
                  pl.BlockSpec((BLK, KVD_W), prev), pl.BlockSpec((BLK, KVD_W), cur),
                  pl.BlockSpec((BLK, KVD_W), prev), pl.BlockSpec((BLK, KVD_W), cur),
                  _full((8, 128))],
        out_specs=[pl.BlockSpec((BLK, ATTN_W), cur), pl.BlockSpec((BLK, KVD_W), prev),
                   pl.BlockSpec((BLK, KVD_W), prev), _full((8, 128))],
        out_shape=[_sds((T, ATTN_W), F32), _sds((T, KVD_W), F32), _sds((T, KVD_W), F32), _sds((8, 128), F32)],
        scratch_shapes=[pltpu.VMEM((BLK, KVD_W), F32), pltpu.VMEM((BLK, KVD_W), F32)],
        compiler_params=_cp("arbitrary"),
    )(q, dattn, lse, kd, kd, vd, vd, sink_b)


def _pool_bwd(dpool, pooled, pool_w, pool_scale):
    T = dpool.shape[0]
    tm = min(T, 512)
    nbk = T // tm
    ext_rows = tm + HALO

    def body(dp_ref, pl_ref, w_ref, sc_ref, du_ref, dw_ref, dsc_ref, halo):
        i = pl.program_id(0)

        @pl.when(i == 0)
        def _():
            halo[...] = jnp.zeros_like(halo)
            dw_ref[...] = jnp.zeros_like(dw_ref)
            dsc_ref[...] = jnp.zeros_like(dsc_ref)

        blk = nbk - 1 - i
        tpos = (blk * tm + lax.broadcasted_iota(jnp.int32, (tm, 1), 0)).astype(F32)
        for g, w in enumerate(POOL_WINDOWS):
            lanes = slice(g * 128, (g + 1) * 128)
            dp = dp_ref[:, lanes].astype(F32)
            pb = pl_ref[:, lanes]
            wg = w_ref[g].astype(BF16)
            z = _mm(pb, wg)
            dsc_ref[0:1, lanes] += _colsum(dp * z)
            dz = (dp * sc_ref[:, lanes]).astype(BF16)
            dw_ref[g] += _tn(pb, dz)
            dpl = _nt(dz, wg)
            e = dpl / jnp.minimum(tpos + 1.0, float(w))
            s = jnp.concatenate([e, halo[:, lanes]], axis=0)
            halo[:, lanes] = e[0:HALO, :]
            sh = 1
            while sh < w:
                s = s + pltpu.roll(s, ext_rows - sh, 0)
                sh *= 2
            du_ref[:, lanes] = s[0:tm, :] - dpl

    rev = lambda i: (nbk - 1 - i, 0)
    return pl.pallas_call(
        body, name="pool_bwd", grid=(nbk,),
        in_specs=[pl.BlockSpec((tm, POOL_W), rev), pl.BlockSpec((tm, POOL_W), rev),
                  _full((4, 128, 128)), _full((1, POOL_W))],
        out_specs=[pl.BlockSpec((tm, POOL_W), rev), _full((4, 128, 128)), _full((8, POOL_W))],
        out_shape=[_sds((T, POOL_W), F32), _sds((4, 128, 128), F32), _sds((8, POOL_W), F32)],
        scratch_shapes=[pltpu.VMEM((HALO, POOL_W), F32)],
        compiler_params=_cp("arbitrary"),
    )(dpool, pooled, pool_w, pool_scale)


def _in_bwd(dq, dk, dv, du, rc, rs1, rs2, x, dx1, mod8, g8, w_in):
    T = x.shape[0]
    tm = min(T, 512)

    def body(dq_ref, dk_ref, dv_ref, du_ref, c_ref, s1_ref, s2_ref, x_ref, dx1_ref, mod_ref, g_ref, w_ref,
             dx_ref, dproj_ref, red_ref, dbin_ref):
        @pl.when(pl.program_id(0) == 0)
        def _():
            red_ref[...] = jnp.zeros_like(red_ref)
            dbin_ref[...] = jnp.zeros_like(dbin_ref)

        c = c_ref[...]
        s1 = s1_ref[...]
        s2 = s2_ref[...]
        dqp = _rot_bwd(dq_ref[...] * (HEAD ** -0.5), jnp.tile(c, (1, 4)), jnp.tile(s1, (1, 4)), jnp.tile(s2, (1, 4)))
        dkp = _rot_bwd(_fold_dup(dk_ref[...]), c, s1, s2)
        pieces = ((0, ATTN_W, dqp), (ATTN_W, ATTN_W + KV_W, dkp),
                  (ATTN_W + KV_W, ATTN_W + 2 * KV_W, _fold_dup(dv_ref[...])), (ATTN_W + 2 * KV_W, IN_W, du_ref[...]))
        dh = jnp.zeros((tm, D_MODEL), F32)
        for lo, hi, val in pieces:
            dbin_ref[0:1, lo:hi] += _colsum(val)
            vb = val.astype(BF16)
            dproj_ref[:, lo:hi] = vb
            dh = dh + _nt(vb, w_ref[:, lo:hi])
        xf = x_ref[...]
        rstd = _rstd(xf)
        xhat = xf * rstd
        gpre = g_ref[0:1, :]
        scale1 = 1.0 + mod_ref[1:2, :]
        red_ref[0:1, :] += _colsum(dh)
        red_ref[1:2, :] += _colsum(dh * (xhat * gpre))
        red_ref[2:3, :] += _colsum(dh * scale1 * xhat)
        dx_ref[...] = dx1_ref[...] + _norm_bwd(dh * scale1 * gpre, xhat, rstd)

    return pl.pallas_call(
        body, name="in_bwd", grid=(T // tm,),
        in_specs=[_rows(tm, ATTN_W), _rows(tm, KVD_W), _rows(tm, KVD_W), _rows(tm, POOL_W),
                  _rows(tm, 128), _rows(tm, 128), _rows(tm, 128), _rows(tm, D_MODEL), _rows(tm, D_MODEL),
                  _full((8, D_MODEL)), _full((8, D_MODEL)), _resident((D_MODEL, IN_W))],
        out_specs=[_rows(tm, D_MODEL), _rows(tm, IN_W), _full((8, D_MODEL)), _full((8, IN_W))],
        out_shape=[_sds((T, D_MODEL), F32), _sds((T, IN_W), BF16), _sds((8, D_MODEL), F32), _sds((8, IN_W), F32)],
        compiler_params=_cp("arbitrary"),
    )(dq, dk, dv, du, rc, rs1, rs2, x, dx1, mod8, g8, w_in)


def _mod_fwd(c_all, ada_w, ada_b_sh):
    tn = 512

    def body(c_ref, w_ref, b_ref, o_ref):
        cv = c_ref[...]
        ca = (cv * jax.nn.sigmoid(cv)).astype(BF16)
        o_ref[...] = _mm(ca, w_ref[...].astype(BF16)) + b_ref[...]

    return pl.pallas_call(
        body, name="mod_fwd", grid=(2, ADA_SH // tn),
        in_specs=[_full((8, D_MODEL)), pl.BlockSpec((None, D_MODEL, tn), lambda l, j: (l, 0, j)),
                  pl.BlockSpec((None, 1, tn), lambda l, j: (l, 0, j))],
        out_specs=pl.BlockSpec((None, 8, tn), lambda l, j: (l, 0, j)),
        out_shape=_sds((2, 8, ADA_SH), F32),
        compiler_params=_cp("parallel", "parallel"),
    )(c_all, ada_w, ada_b_sh)


def _ada_wgrad(c_all_t, dmod_sh):
    tn = 512

    def body(c_ref, d_ref, o_ref):
        cv = c_ref[...]
        ca = cv * jax.nn.sigmoid(cv)
        o_ref[...] = jnp.dot(ca, d_ref[...], preferred_element_type=F32, precision=lax.Precision.HIGHEST)

    return pl.pallas_call(
        body, name="ada_wgrad", grid=(2, ADA_SH // tn),
        in_specs=[_full((D_MODEL, 8)), pl.BlockSpec((None, 8, tn), lambda l, j: (l, 0, j))],
        out_specs=pl.BlockSpec((None, D_MODEL, tn), lambda l, j: (l, 0, j)),
        out_shape=_sds((2, D_MODEL, ADA_SH), F32),
        compiler_params=_cp("parallel", "parallel"),
    )(c_all_t, dmod_sh)


def _sum_devices(g):
    R = g.shape[1]

    def body(g_ref, o_ref):
        acc = g_ref[0]
        for d in range(1, N_DEV):
            acc = acc + g_ref[d]
        o_ref[...] = acc

    return pl.pallas_call(
        body, name="sum_devices", grid=(1,),
        in_specs=[_full((N_DEV, R, 128))], out_specs=_full((R, 128)), out_shape=_sds((R, 128), F32),
        compiler_params=_cp("arbitrary"),
    )(g)


def _adamw(w, g, m, v, name):
    R, C = w.shape
    tr = R
    for cand in (256, 128, 64, 32, 16, 8):
        if R % cand == 0 and cand * C * 4 <= 2 * 1024 * 1024:
            tr = cand
            break

    def body(w_ref, g_ref, m_ref, v_ref, d_ref, nm_ref, nv_ref):
        gv = g_ref[...]
        mn = ADAM_B1 * m_ref[...] + (1.0 - ADAM_B1) * gv
        vn = ADAM_B2 * v_ref[...] + (1.0 - ADAM_B2) * (gv * gv)
        m_hat = mn / (1.0 - ADAM_B1 ** ADAM_STEP)
        v_hat = vn / (1.0 - ADAM_B2 ** ADAM_STEP)
        d_ref[...] = -ADAM_LR * (m_hat / (jnp.sqrt(v_hat) + ADAM_EPS) + ADAM_WD * w_ref[...])
        nm_ref[...] = mn
        nv_ref[...] = vn

    spec = pl.BlockSpec((tr, C), lambda i: (i, 0))
    out = _sds((R, C), F32)
    return pl.pallas_call(
        body, name=name, grid=(R // tr,),
        in_specs=[spec] * 4, out_specs=[spec] * 3, out_shape=[out] * 3,
        compiler_params=_cp("parallel"),
    )(w, g, m, v)


def _adamw_nd(w, g, m, v, name):
    shape = w.shape
    if w.ndim == 2 and shape[1] < 128:
        view = (1, shape[0] * shape[1])
    else:
        view = (-1, shape[-1])
    outs = _adamw(*[t.reshape(view) for t in (w, g, m, v)], name=name)
    return [o.reshape(shape) for o in outs]


def _coords():
    return lax.axis_index("x"), lax.axis_index("y"), lax.axis_index("c")


def _other_chips(x, y):
    return [(1 - x, y), (x, 1 - y), (1 - x, 1 - y)]


def _allgather8(blk, name):
    m_per, n = blk.shape

    def body(x_ref, out_ref, send_sems, recv_sems, local_sem):
        x, y, c = _coords()
        me, sibling = (x, y, c), (x, y, 1 - c)
        chips = _other_chips(x, y)

        def rows(px, py, pc):
            return out_ref.at[pl.ds((4 * px + 2 * py + pc) * m_per, m_per), :]

        def copy(k, block, to, src=None):
            return pltpu.make_async_remote_copy(
                src_ref=rows(*block) if src is None else src, dst_ref=rows(*block),
                send_sem=send_sems.at[k], recv_sem=recv_sems.at[k], device_id=to, device_id_type=MESH)

        mine = pltpu.make_async_copy(x_ref, rows(*me), local_sem)
        mine.start()
        first = [copy(0, me, sibling, src=x_ref)]
        first += [copy(1 + j, me, (*chip, c), src=x_ref) for j, chip in enumerate(chips)]
        for cp in first:
            cp.start()
        passed = [copy(4 + j, (*chip, c), sibling) for j, chip in enumerate(chips)]
        for j, chip in enumerate(chips):
            copy(1 + j, (*chip, c), me).wait_recv()
            passed[j].start()
        copy(0, sibling, me).wait_recv()
        for j, chip in enumerate(chips):
            copy(4 + j, (*chip, 1 - c), me).wait_recv()
        for cp in first + passed:
            cp.wait_send()
        mine.wait()

    return pl.pallas_call(
        body, name=name,
        out_shape=_sds((N_DEV * m_per, n), blk.dtype),
        in_specs=[pl.BlockSpec(memory_space=pltpu.VMEM)],
        out_specs=pl.BlockSpec(memory_space=pltpu.VMEM),
        scratch_shapes=[pltpu.SemaphoreType.DMA((7,)), pltpu.SemaphoreType.DMA((7,)), pltpu.SemaphoreType.DMA],
        compiler_params=pltpu.CompilerParams(vmem_limit_bytes=VMEM_LIMIT),
    )(blk)


def _row_tile(r, n):
    for cand in (512, 256, 128, 64, 32, 16):
        if r % cand == 0 and cand * n * 4 <= 2 * 1024 * 1024:
            return cand
    return r


def _cast_slot(w, chip, name):
    r, n = w.shape
    tr = _row_tile(r, n)

    def body(chip_ref, w_ref, o_ref):
        o_ref[...] = w_ref[...].astype(BF16)

    grid_spec = pltpu.PrefetchScalarGridSpec(
        num_scalar_prefetch=1, grid=(r // tr,),
        in_specs=[pl.BlockSpec((tr, n), lambda i, ch: (i, 0))],
        out_specs=pl.BlockSpec((None, tr, n), lambda i, ch: (ch[0], i, 0)))
    return pl.pallas_call(
        body, name=name, grid_spec=grid_spec, out_shape=_sds((N_SHARD, r, n), BF16),
        compiler_params=_cp("arbitrary"),
    )(chip, w)


def _allgather_weights(bufs, name):
    nt = len(bufs)
    hom = [pl.BlockSpec(memory_space=pl.ANY)] * nt

    def body(*refs):
        outs = refs[nt:2 * nt]
        send_sems, recv_sems = refs[2 * nt:]
        x, y, c = _coords()
        sibling = (x, y, 1 - c)
        chips = _other_chips(x, y)

        def copy(t, k, block_chip, hc, to):
            r = outs[t].shape[1] // 2
            blk = outs[t].at[2 * block_chip[0] + block_chip[1], pl.ds(hc * r, r)]
            return pltpu.make_async_remote_copy(
                src_ref=blk, dst_ref=blk,
                send_sem=send_sems.at[t, k], recv_sem=recv_sems.at[t, k], device_id=to, device_id_type=MESH)

        started = []
        for t in range(nt):
            for j, chip in enumerate(chips):
                cp = copy(t, j, (x, y), c, (*chip, c))
                cp.start()
                started.append(cp)
        for t in range(nt):
            for j, chip in enumerate(chips):
                copy(t, j, chip, c, sibling).wait_recv()
                fw = copy(t, 3 + j, chip, c, sibling)
                fw.start()
                started.append(fw)
        for t in range(nt):
            for j, chip in enumerate(chips):
                copy(t, 3 + j, chip, 1 - c, sibling).wait_recv()
        for cp in started:
            cp.wait_send()

    return pl.pallas_call(
        body, name=name,
        out_shape=[_sds(b.shape, b.dtype) for b in bufs],
        in_specs=hom, out_specs=hom,
        input_output_aliases={t: t for t in range(nt)},
        scratch_shapes=[pltpu.SemaphoreType.DMA((nt, 6)), pltpu.SemaphoreType.DMA((nt, 6))],
    )(*bufs)


def _swap_halves(grads, name):
    nt = len(grads)
    hom = [pl.BlockSpec(memory_space=pl.ANY)] * nt

    def body(*refs):
        ins = refs[:nt]
        outs = refs[nt:2 * nt]
        send_sems, recv_sems = refs[2 * nt:]
        x, y, c = _coords()
        sibling = (x, y, 1 - c)
        cps = []
        for t in range(nt):
            r = ins[t].shape[1] // 2
            cp = pltpu.make_async_remote_copy(
                src_ref=ins[t].at[:, pl.ds((1 - c) * r, r)], dst_ref=outs[t],
                send_sem=send_sems.at[t], recv_sem=recv_sems.at[t], device_id=sibling, device_id_type=MESH)
            cp.start()
            cps.append(cp)
        for cp in cps:
            cp.wait()

    return pl.pallas_call(
        body, name=name,
        out_shape=[_sds((N_SHARD, g.shape[1] // 2, g.shape[2]), g.dtype) for g in grads],
        in_specs=hom, out_specs=hom,
        scratch_shapes=[pltpu.SemaphoreType.DMA((nt,)), pltpu.SemaphoreType.DMA((nt,))],
    )(*grads)


def _scatter_chips(sums, name):
    nt = len(sums)
    hom = [pl.BlockSpec(memory_space=pl.ANY)] * nt

    def body(*refs):
        ins = refs[:nt]
        outs = refs[nt:2 * nt]
        send_sems, recv_sems = refs[2 * nt:]
        x, y, c = _coords()
        chips = _other_chips(x, y)
        cps = []
        for t in range(nt):
            for j, chip in enumerate(chips):
                cp = pltpu.make_async_remote_copy(
                    src_ref=ins[t].at[2 * chip[0] + chip[1]], dst_ref=outs[t].at[j],
                    send_sem=send_sems.at[t, j], recv_sem=recv_sems.at[t, j],
                    device_id=(*chip, c), device_id_type=MESH)
                cp.start()
                cps.append(cp)
        for cp in cps:
            cp.wait()

    return pl.pallas_call(
        body, name=name,
        out_shape=[_sds((3,) + s.shape[1:], s.dtype) for s in sums],
        in_specs=hom, out_specs=hom,
        scratch_shapes=[pltpu.SemaphoreType.DMA((nt, 3)), pltpu.SemaphoreType.DMA((nt, 3))],
    )(*sums)


def _join_halves(tots, name):
    nt = len(tots)
    hom = [pl.BlockSpec(memory_space=pl.ANY)] * nt

    def body(*refs):
        outs = refs[nt:2 * nt]
        send_sems, recv_sems = refs[2 * nt:]
        x, y, c = _coords()
        sibling = (x, y, 1 - c)
        cps = []
        for t in range(nt):
            cp = pltpu.make_async_remote_copy(
                src_ref=outs[t].at[c], dst_ref=outs[t].at[c],
                send_sem=send_sems.at[t], recv_sem=recv_sems.at[t], device_id=sibling, device_id_type=MESH)
            cp.start()
            cps.append(cp)
        for t in range(nt):
            pltpu.make_async_remote_copy(
                src_ref=outs[t].at[c], dst_ref=outs[t].at[1 - c],
                send_sem=send_sems.at[t], recv_sem=recv_sems.at[t], device_id=sibling, device_id_type=MESH).wait_recv()
        for cp in cps:
            cp.wait_send()

    return pl.pallas_call(
        body, name=name,
        out_shape=[_sds(t.shape, t.dtype) for t in tots],
        in_specs=hom, out_specs=hom,
        input_output_aliases={t: t for t in range(nt)},
        scratch_shapes=[pltpu.SemaphoreType.DMA((nt,)), pltpu.SemaphoreType.DMA((nt,))],
    )(*tots)


def _pair_sum(g, recv, core, chip, name):
    _, _, r, n = g.shape
    tr = _row_tile(r, n)

    def body(core_ref, chip_ref, g_ref, r_ref, sb_ref, own_ref):
        tot = g_ref[...] + r_ref[...]
        sb_ref[...] = tot.astype(BF16)

        @pl.when(pl.program_id(1) == chip_ref[0])
        def _():
            own_ref[...] = tot

    grid_spec = pltpu.PrefetchScalarGridSpec(
        num_scalar_prefetch=2, grid=(r // tr, N_SHARD),
        in_specs=[pl.BlockSpec((None, None, tr, n), lambda i, s, co, ch: (s, co[0], i, 0)),
                  pl.BlockSpec((None, tr, n), lambda i, s, co, ch: (s, i, 0))],
        out_specs=[pl.BlockSpec((None, tr, n), lambda i, s, co, ch: (s, i, 0)),
                   pl.BlockSpec((tr, n), lambda i, s, co, ch: (i, 0))])
    return pl.pallas_call(
        body, name=name, grid_spec=grid_spec,
        out_shape=[_sds((N_SHARD, r, n), BF16), _sds((r, n), F32)],
        compiler_params=_cp("arbitrary", "arbitrary"),
    )(core, chip, g, recv)


def _chip_sum(own, recv, core, name):
    r, n = own.shape
    tr = _row_tile(r, n)

    def body(core_ref, o_ref, r_ref, t_ref):
        acc = o_ref[...]
        for j in range(3):
            acc = acc + r_ref[j].astype(F32)
        t_ref[...] = acc

    grid_spec = pltpu.PrefetchScalarGridSpec(
        num_scalar_prefetch=1, grid=(r // tr,),
        in_specs=[pl.BlockSpec((tr, n), lambda i, co: (i, 0)), pl.BlockSpec((3, tr, n), lambda i, co: (0, i, 0))],
        out_specs=pl.BlockSpec((None, tr, n), lambda i, co: (co[0], i, 0)))
    return pl.pallas_call(
        body, name=name, grid_spec=grid_spec, out_shape=_sds((2, r, n), F32),
        compiler_params=_cp("arbitrary"),
    )(core, own, recv)


_HBM = pl.BlockSpec(memory_space=pltpu.HBM)
_SEM = pl.BlockSpec(memory_space=pltpu.SEMAPHORE)
_EFFECT = pltpu.SideEffectType.DATAFLOW_SIDE_EFFECTING


def _ici_copies(srcs, dsts, send_sems, recv_sems, send_view, recv_view):
    x, y, c = _coords()
    out = []
    for t in range(len(srcs)):
        for j, chip in enumerate(_other_chips(x, y)):
            out.append(pltpu.make_async_remote_copy(
                src_ref=send_view(srcs[t], chip, j, (x, y), c), dst_ref=recv_view(dsts[t], chip, j, (x, y), c),
                send_sem=send_sems.at[3 * t + j], recv_sem=recv_sems.at[3 * t + j],
                device_id=(*chip, c), device_id_type=MESH))
    return out


def _ici_start(srcs, dsts, after, send_view, recv_view, name):
    nt = len(srcs)
    inplace = dsts is None
    nbuf = nt if inplace else 2 * nt

    def body(*refs):
        send_sems, recv_sems = refs[nbuf + 1], refs[nbuf + 2]
        s_out = refs[nbuf + 3:nbuf + 3 + nt]
        d_out = s_out if inplace else refs[nbuf + 3 + nt:nbuf + 3 + 2 * nt]
        token = refs[-1]
        for cp in _ici_copies(s_out, d_out, send_sems, recv_sems, send_view, recv_view):
            cp.start()
        token[...] = jnp.zeros_like(token)

    bufs = list(srcs) + ([] if inplace else list(dsts))
    res = pl.pallas_call(
        body, name=name,
        out_shape=(pltpu.SemaphoreType.DMA((3 * nt,)), pltpu.SemaphoreType.DMA((3 * nt,)),
                   *[pltpu.HBM(b.shape, b.dtype) for b in bufs], _sds((8, 128), F32)),
        in_specs=[_HBM] * nbuf + [pl.BlockSpec(memory_space=pl.ANY)],
        out_specs=(_SEM, _SEM, *[_HBM] * nbuf, pl.BlockSpec(memory_space=pltpu.VMEM)),
        input_output_aliases={i: 2 + i for i in range(nbuf)},
        compiler_params=pltpu.CompilerParams(has_side_effects=_EFFECT),
    )(*[pltpu.with_memory_space_constraint(b, pltpu.HBM) for b in bufs], after)
    send_sems, recv_sems = res[0], res[1]
    s_thru = list(res[2:2 + nt])
    d_thru = s_thru if inplace else list(res[2 + nt:2 + 2 * nt])
    return send_sems, recv_sems, s_thru, d_thru, res[-1]


def _ici_wait(send_sems, recv_sems, srcs, dsts, after, send_view, recv_view, name):
    nt = len(srcs)
    inplace = dsts is None
    nbuf = nt if inplace else 2 * nt

    def body(*refs):
        send_ref, recv_ref = refs[nbuf], refs[nbuf + 1]
        s_out = refs[nbuf + 3:nbuf + 3 + nt]
        d_out = s_out if inplace else refs[nbuf + 3 + nt:nbuf + 3 + 2 * nt]
        for cp in _ici_copies(s_out, d_out, send_ref, recv_ref, send_view, recv_view):
            cp.wait_send()
            cp.wait_recv()

    bufs = list(srcs) + ([] if inplace else list(dsts))
    res = pl.pallas_call(
        body, name=name,
        out_shape=tuple(pltpu.HBM(b.shape, b.dtype) for b in bufs),
        in_specs=[_HBM] * nbuf + [_SEM, _SEM, pl.BlockSpec(memory_space=pl.ANY)],
        out_specs=tuple([_HBM] * nbuf),
        input_output_aliases={i: i for i in range(nbuf)},
        compiler_params=pltpu.CompilerParams(has_side_effects=_EFFECT),
    )(*bufs, send_sems, recv_sems, after)
    return list(res[:nt]) if inplace else list(res[nt:])


def _w_half(buf, chip, c):
    r = buf.shape[1] // 2
    return buf.at[2 * chip[0] + chip[1], pl.ds(c * r, r)]


def _ag_send_view(buf, chip, j, me, c):
    return _w_half(buf, me, c)


def _ag_recv_view(buf, chip, j, me, c):
    return _w_half(buf, me, c)


def _rs_send_view(buf, chip, j, me, c):
    return buf.at[2 * chip[0] + chip[1]]


def _rs_recv_view(buf, chip, j, me, c):
    return buf.at[j]


def _ag_forward(bufs, name):
    nt = len(bufs)
    hom = [pl.BlockSpec(memory_space=pl.ANY)] * nt

    def body(*refs):
        outs = refs[nt:2 * nt]
        send_sems, recv_sems = refs[2 * nt:]
        x, y, c = _coords()
        sibling = (x, y, 1 - c)
        chips = _other_chips(x, y)

        def copy(t, j, hc):
            blk = _w_half(outs[t], chips[j], hc)
            return pltpu.make_async_remote_copy(
                src_ref=blk, dst_ref=blk, send_sem=send_sems.at[t, j], recv_sem=recv_sems.at[t, j],
                device_id=sibling, device_id_type=MESH)

        started = [copy(t, j, c) for t in range(nt) for j in range(3)]
        for cp in started:
            cp.start()
        for t in range(nt):
            for j in range(3):
                copy(t, j, 1 - c).wait_recv()
        for cp in started:
            cp.wait_send()

    return pl.pallas_call(
        body, name=name,
        out_shape=[_sds(b.shape, b.dtype) for b in bufs],
        in_specs=hom, out_specs=hom,
        input_output_aliases={t: t for t in range(nt)},
        scratch_shapes=[pltpu.SemaphoreType.DMA((nt, 3)), pltpu.SemaphoreType.DMA((nt, 3))],
    )(*bufs)


def _rs_begin(grads, after, tag):
    x, y, c = _coords()
    core = jnp.reshape(c, (1,)).astype(jnp.int32)
    chip = jnp.reshape(2 * x + y, (1,)).astype(jnp.int32)
    recv = _swap_halves(grads, name="rs_swap_" + tag)
    sums, owns = [], []
    for t, (g, rv) in enumerate(zip(grads, recv)):
        r = g.shape[1] // 2
        sb, own = _pair_sum(g.reshape(N_SHARD, 2, r, g.shape[2]), rv, core, chip, name=f"rs_pair_{tag}_{t}")
        sums.append(sb)
        owns.append(own)
    land = [lax.empty((3,) + s.shape[1:], s.dtype) for s in sums]
    send_sems, recv_sems, s_thru, d_thru, token = _ici_start(
        sums, land, after, _rs_send_view, _rs_recv_view, name="rs_start_" + tag)
    return dict(sems=(send_sems, recv_sems), sums=s_thru, land=d_thru, owns=owns, core=core, tag=tag), token


def _rs_end(state, after):
    tag = state["tag"]
    got = _ici_wait(*state["sems"], state["sums"], state["land"], after, _rs_send_view, _rs_recv_view,
                    name="rs_wait_" + tag)
    tots = [_chip_sum(o, gt, state["core"], name=f"rs_chip_{tag}_{t}")
            for t, (o, gt) in enumerate(zip(state["owns"], got))]
    full = _join_halves(tots, name="rs_join_" + tag)
    return [f.reshape(2 * f.shape[1], f.shape[2]) for f in full]


def _rope_lane_table():
    d = jnp.arange(128) % HEAD
    inv_freq = ROPE_THETA ** (-jnp.arange(0, ROT, 2, dtype=F32) / ROT)
    rot = d < ROT
    rows = [jnp.where(rot, inv_freq[d % (ROT // 2)], 0.0), rot.astype(F32),
            (d < ROT // 2).astype(F32), jnp.logical_and(d >= ROT // 2, rot).astype(F32)]
    return jnp.concatenate([jnp.stack(rows), jnp.zeros((4, 128), F32)], axis=0)


def _pad8(rows):
    return jnp.concatenate([rows, jnp.zeros((8 - rows.shape[0], rows.shape[1]), F32)], axis=0)


def kernel(x, c, positions, ada_w, ada_b, w_in, b_in, sinks, pool_w, pool_scale, w_out, w_gate, w_up, w_down, g_pre_mix, g_post_mix, g_pre_ffn, g_post_ffn, loss_target, m_ada_w, m_ada_b, m_w_in, m_b_in, m_sinks, m_pool_w, m_pool_scale, m_w_out, m_w_gate, m_w_up, m_w_down, m_g_pre_mix, m_g_post_mix, m_g_pre_ffn, m_g_post_ffn, v_ada_w, v_ada_b, v_w_in, v_b_in, v_sinks, v_pool_w, v_pool_scale, v_w_out, v_w_gate, v_w_up, v_w_down, v_g_pre_mix, v_g_post_mix, v_g_pre_ffn, v_g_post_ffn):
    T = x.shape[1]
    n_layers = ada_w.shape[0]
    ax, ay, ac = _coords()
    my_dev = 4 * ax + 2 * ay + ac
    my_chip = 2 * ax + ay
    x0 = x.reshape(T, D_MODEL)
    target = loss_target.reshape(T, D_MODEL)

    c_all = _allgather8(c.reshape(8, 128), name="ag_c").reshape(N_DEV, D_MODEL)
    ada_b_sh = lax.dynamic_slice_in_dim(ada_b, my_chip * ADA_SH, ADA_SH, axis=1).reshape(n_layers, 1, ADA_SH)
    mod_part = _mod_fwd(c_all, ada_w, ada_b_sh)
    mod_all = _allgather8(mod_part.reshape(n_layers * 8, ADA_SH), name="ag_mod")
    mod_all = mod_all.reshape(N_DEV, n_layers, 8, ADA_SH)[0::2]
    mod_mine = lax.dynamic_index_in_dim(mod_all, my_dev, axis=2, keepdims=False)
    mod = jnp.transpose(mod_mine, (1, 0, 2)).reshape(n_layers, 6, D_MODEL)

    pos_b = jnp.broadcast_to(positions.reshape(T, 1), (T, 128))
    rc, rs1, rs2 = _rope_tables(pos_b, _rope_lane_table())

    chip1 = jnp.reshape(my_chip, (1,)).astype(jnp.int32)

    def cast_layer(l):
        return [_cast_slot(w[l], chip1, name=f"cast_{nm}{l}")
                for nm, w in (("w_in", w_in), ("w_out", w_out), ("w_gate", w_gate), ("w_up", w_up), ("w_down", w_down))]

    def as_operands(bufs):
        gin, gout, gg, gu, gd = bufs
        win_full = jnp.transpose(gin, (1, 0, 2)).reshape(D_MODEL, IN_W)
        return win_full, gout.reshape(D_MODEL, D_MODEL), gg, gu, gd

    bufs0 = cast_layer(0)
    win0 = _allgather_weights(bufs0[:1], name="ag_w0_in")
    rest_send, rest_recv, rest_bufs, _, ag_token = _ici_start(
        bufs0[1:], None, win0[0], _ag_send_view, _ag_recv_view, name="ag_start_0")
    weights = [None] * n_layers

    saved = []
    xl = x0
    for l in range(n_layers):
        mod8 = _pad8(mod[l])
        if l + 1 < n_layers:
            ag_send, ag_recv, ag_bufs, _, ag_token = _ici_start(
                cast_layer(l + 1), None, ag_token, _ag_send_view, _ag_recv_view, name=f"ag_start_{l + 1}")
        if l == 0 or l + 1 < n_layers:
            mod8 = mod8 + ag_token[0, 0]
        g8 = _pad8(jnp.stack([g_pre_mix[l], g_post_mix[l], g_pre_ffn[l], g_post_ffn[l]]))
        sink_b = jnp.broadcast_to(sinks[l][:, None], (N_HEADS, 128))
        psc = pool_scale[l].reshape(1, POOL_W)
        win = jnp.transpose(win0[0], (1, 0, 2)).reshape(D_MODEL, IN_W) if l == 0 else weights[l][0]
        h, q, k, v, u = _fwd_in(xl, mod8, g8, win, b_in[l].reshape(1, IN_W), rc, rs1, rs2)
        attn, lse = _attn_fwd(q, k, v, sink_b)
        pool, pooled = _pool_fwd(u, pool_w[l], psc)
        if l == 0:
            arrived = _ici_wait(rest_send, rest_recv, rest_bufs, None, pool, _ag_send_view, _ag_recv_view,
                                name="ag_wait_0")
            weights[0] = as_operands(win0 + _ag_forward(arrived, name="ag_fwd_0"))
        win, wout, wg, wu, wd = weights[l]
        mix, x1 = _fwd_out(attn, pool, xl, wout, g8, mod8)
        h2, act, ga, gb, f, x2 = _ffn_fwd(x1, mod8, g8, wg, wu, wd)
        saved.append(dict(x=xl, h=h, q=q, k=k, v=v, lse=lse, attn=attn, pool=pool, pooled=pooled, mix=mix,
                          x1=x1, h2=h2, act=act, ga=ga, gb=gb, f=f, mod8=mod8, g8=g8, sink_b=sink_b, psc=psc))
        xl = x2
        if l + 1 < n_layers:
            arrived = _ici_wait(ag_send, ag_recv, ag_bufs, None, x2, _ag_send_view, _ag_recv_view,
                                name=f"ag_wait_{l + 1}")
            weights[l + 1] = as_operands(_ag_forward(arrived, name=f"ag_fwd_{l + 1}"))

    dy, loss_tile = _loss_grad(xl, target)
    loss = lax.psum(loss_tile[0, 0], ("x", "y", "c"))

    small = [None] * n_layers
    dmod_rows = [None] * n_layers
    reduced = [dict() for _ in range(n_layers)]
    in_flight = None
    dx = dy
    for l in reversed(range(n_layers)):
        s = saved[l]
        win, wout, wg, wu, wd = weights[l]
        if in_flight is not None:
            s = dict(s, mod8=s["mod8"] + in_flight[1][0, 0])
        dx1, df, da, db, red_f = _ffn_bwd(dx, s["f"], s["ga"], s["gb"], s["x1"], s["mod8"], s["g8"], wg, wu, wd)
        g_wd = _wgrad_rows(s["act"], df, name="wgrad_down")
        g_wg = _wgrad_cols(s["h2"], da, name="wgrad_gate")
        g_wu = _wgrad_cols(s["h2"], db, name="wgrad_up")
        if in_flight is not None:
            got = _rs_end(in_flight[0], g_wu)
            reduced[l + 1].update(w_in=got[0], w_out=got[1])
        ffn_flight = _rs_begin([g_wg, g_wu, g_wd], g_wu, tag=f"{l}f")
        s = dict(s, mod8=s["mod8"] + ffn_flight[1][0, 0])
        dmix, dattn, dpool, red_c = _mix_bwd(dx1, s["mix"], s["mod8"], s["g8"], wout)
        g_wout = jnp.concatenate([_wgrad(s["attn"], dmix, name="wgrad_out_a"),
                                  _wgrad(s["pool"], dmix, name="wgrad_out_p")], axis=0)
        dq, dk, dv, dsink = _attn_bwd(s["q"], s["k"], s["v"], s["lse"], dattn, s["sink_b"])
        du, g_poolw, dpsc = _pool_bwd(dpool, s["pooled"], pool_w[l], s["psc"])
        dx, dproj, red_d, dbin = _in_bwd(dq, dk, dv, du, rc, rs1, rs2, s["x"], dx1, s["mod8"], s["g8"], win)
        g_win = _wgrad(s["h"], dproj, name="wgrad_in")
        g_win_sh = jnp.transpose(g_win.reshape(D_MODEL, N_SHARD, IN_SH), (1, 0, 2))
        got = _rs_end(ffn_flight[0], g_win)
        reduced[l].update(w_gate=got[0], w_up=got[1], w_down=got[2])
        in_flight = _rs_begin([g_win_sh, g_wout.reshape(N_SHARD, OUT_SH, D_MODEL)], g_win, tag=f"{l}a")
        dmod_rows[l] = jnp.concatenate([red_d[0], red_d[1], red_c[0], red_f[2], red_f[3], red_f[0]])
        small[l] = jnp.concatenate([red_d[2], red_c[1], red_f[4], red_f[1], dbin[0], dpsc[0], dsink[:, 0],
                                    jnp.zeros((120,), F32), g_poolw.reshape(-1)])
    grad_x = dx.reshape(1, T, D_MODEL)

    per_layer = small[0].shape[0]
    rows_small = n_layers * per_layer // 128
    rows_mod = n_layers * 6 * D_MODEL // 128
    rows_pad = -(rows_small + rows_mod) % 8
    pack = jnp.concatenate(small + dmod_rows + [jnp.zeros((rows_pad * 128,), F32)]).reshape(-1, 128)
    pack = pack + in_flight[1][0, 0]
    gathered = _allgather8(pack, name="ag_small").reshape(N_DEV, pack.shape[0], 128)
    summed = _sum_devices(gathered)
    small_sum = summed[:rows_small].reshape(n_layers, per_layer)
    o = 0
    small_g = {}
    for nm, width in (("g_pre_mix", D_MODEL), ("g_post_mix", D_MODEL), ("g_pre_ffn", D_MODEL),
                      ("g_post_ffn", D_MODEL), ("b_in", IN_W), ("pool_scale", POOL_W), ("sinks", 128),
                      ("pool_w", 4 * 128 * 128)):
        small_g[nm] = small_sum[:, o:o + width]
        o += width
    small_g["sinks"] = small_g["sinks"][:, :N_HEADS]
    small_g["pool_w"] = small_g["pool_w"].reshape(n_layers, 4, 128, 128)
    small_g["ada_b"] = summed[rows_small:rows_small + rows_mod].reshape(n_layers, 6 * D_MODEL)
    dmod_all = gathered[:, rows_small:rows_small + rows_mod].reshape(N_DEV, n_layers, N_SHARD, ADA_SH)
    dmod_sh = lax.dynamic_index_in_dim(dmod_all, my_chip, axis=2, keepdims=False)
    g_ada_w = _ada_wgrad(jnp.transpose(c_all), jnp.transpose(dmod_sh, (1, 0, 2)))

    grads = dict(ada_w=g_ada_w, ada_b=small_g["ada_b"], b_in=small_g["b_in"], sinks=small_g["sinks"],
                 pool_w=small_g["pool_w"], pool_scale=small_g["pool_scale"], g_pre_mix=small_g["g_pre_mix"],
                 g_post_mix=small_g["g_post_mix"], g_pre_ffn=small_g["g_pre_ffn"], g_post_ffn=small_g["g_post_ffn"])
    params = dict(ada_w=(ada_w, m_ada_w, v_ada_w), ada_b=(ada_b, m_ada_b, v_ada_b), w_in=(w_in, m_w_in, v_w_in),
                  b_in=(b_in, m_b_in, v_b_in), sinks=(sinks, m_sinks, v_sinks), pool_w=(pool_w, m_pool_w, v_pool_w),
                  pool_scale=(pool_scale, m_pool_scale, v_pool_scale), w_out=(w_out, m_w_out, v_w_out),
                  w_gate=(w_gate, m_w_gate, v_w_gate), w_up=(w_up, m_w_up, v_w_up),
                  w_down=(w_down, m_w_down, v_w_down), g_pre_mix=(g_pre_mix, m_g_pre_mix, v_g_pre_mix),
                  g_post_mix=(g_post_mix, m_g_post_mix, v_g_post_mix), g_pre_ffn=(g_pre_ffn, m_g_pre_ffn, v_g_pre_ffn),
                  g_post_ffn=(g_post_ffn, m_g_post_ffn, v_g_post_ffn))
    names = list(params)
    updates = {nm: _adamw_nd(*params[nm][:1], grads[nm], *params[nm][1:], name="adamw_" + nm) for nm in grads}

    got = _rs_end(in_flight[0], updates["ada_w"][0])
    reduced[0].update(w_in=got[0], w_out=got[1])
    for nm in ("w_in", "w_out", "w_gate", "w_up", "w_down"):
        grads[nm] = jnp.stack([reduced[l][nm] for l in range(n_layers)])
        updates[nm] = _adamw_nd(*params[nm][:1], grads[nm], *params[nm][1:], name="adamw_" + nm)
    return (loss, grad_x, *[grads[nm] for nm in names], *[updates[nm][0] for nm in names],
            *[updates[nm][1] for nm in names], *[updates[nm][2] for nm in names])
```

```python
import functools

import jax
import jax.numpy as jnp
from jax import lax
from jax.experimental import pallas as pl
from jax.experimental.pallas import tpu as pltpu

F32 = jnp.float32
BF16 = jnp.bfloat16
MESH = pl.DeviceIdType.MESH

D_MODEL = 1024
ATTN_W = 512
KV_W = 128
KVD_W = 256
POOL_W = 512
IN_W = 1280
D_FF = 2816
N_SHARD = 4
FF_SH = D_FF // N_SHARD
IN_SH = IN_W // N_SHARD
OUT_SH = D_MODEL // N_SHARD
ADA_SH = 6 * D_MODEL // N_SHARD
HEAD = 64
N_HEADS = 8
GROUP = 4
BLK = 128
POOL_WINDOWS = (2, 4, 8, 16)
HALO = 16
ROT = 16
ROPE_THETA = 500000.0
EPS = 1e-6
NEG_INF = -1e30
N_DEV = 8

ADAM_LR = 0.001
ADAM_B1 = 0.9
ADAM_B2 = 0.999
ADAM_EPS = 1e-08
ADAM_WD = 0.01
ADAM_STEP = 10

VMEM_LIMIT = 48 * 1024 * 1024


def _cp(*sem):
    return pltpu.CompilerParams(dimension_semantics=sem, vmem_limit_bytes=VMEM_LIMIT)


def _full(shape):
    nd = len(shape)
    return pl.BlockSpec(shape, lambda *_: (0,) * nd)


def _resident(shape):
    nd = len(shape)
    return pl.BlockSpec(shape, lambda *_: (0,) * nd, pipeline_mode=pl.Buffered(1))


def _rows(tm, ncol):
    return pl.BlockSpec((tm, ncol), lambda i: (i, 0))


def _sds(shape, dtype):
    return jax.ShapeDtypeStruct(shape, dtype)


def _nt(a, b):
    return lax.dot_general(a, b, (((1,), (1,)), ((), ())), preferred_element_type=F32)


def _tn(a, b):
    return lax.dot_general(a, b, (((0,), (0,)), ((), ())), preferred_element_type=F32)


def _mm(a, b):
    return jnp.dot(a, b, preferred_element_type=F32)


def _rstd(x):
    return lax.rsqrt(jnp.mean(x * x, axis=-1, keepdims=True) + EPS)


def _colsum(x):
    return jnp.sum(x, axis=0, keepdims=True)


def _norm_bwd(dhat, xhat, rstd):
    return rstd * (dhat - xhat * jnp.mean(dhat * xhat, axis=-1, keepdims=True))


def _rope_tables(pos_b, lane_tab):
    T = pos_b.shape[0]
    tm = min(T, 1024)

    def body(pos_ref, tab_ref, c_ref, s1_ref, s2_ref):
        ang = pos_ref[...].astype(F32) * tab_ref[0:1, :]
        cs = jnp.cos(ang)
        sn = jnp.sin(ang)
        m_rot = tab_ref[1:2, :]
        c_ref[...] = cs * m_rot + (1.0 - m_rot)
        s1_ref[...] = -sn * tab_ref[2:3, :]
        s2_ref[...] = sn * tab_ref[3:4, :]

    out = _sds((T, 128), F32)
    return pl.pallas_call(
        body, name="rope_tables", grid=(T // tm,),
        in_specs=[_rows(tm, 128), _full((8, 128))],
        out_specs=[_rows(tm, 128)] * 3, out_shape=[out] * 3,
        compiler_params=_cp("parallel"),
    )(pos_b, lane_tab)


def _rot_fwd(t, c, s1, s2):
    w = t.shape[-1]
    return t * c + pltpu.roll(t, w - 8, 1) * s1 + pltpu.roll(t, 8, 1) * s2


def _rot_bwd(d, c, s1, s2):
    w = d.shape[-1]
    return d * c + pltpu.roll(d * s1, 8, 1) + pltpu.roll(d * s2, w - 8, 1)


def _store_dup(ref, t):
    low = lax.broadcasted_iota(jnp.int32, t.shape, 1) < HEAD
    sw = pltpu.roll(t, HEAD, 1)
    ref[:, 0:128] = jnp.where(low, t, sw).astype(BF16)
    ref[:, 128:256] = jnp.where(low, sw, t).astype(BF16)


def _fold_dup(d):
    low = lax.broadcasted_iota(jnp.int32, (d.shape[0], 128), 1) < HEAD
    d0 = d[:, 0:128]
    d1 = d[:, 128:256]
    return jnp.where(low, d0 + pltpu.roll(d0, HEAD, 1), d1 + pltpu.roll(d1, HEAD, 1))


def _fwd_in(x, mod8, g8, w_in, b_in, rc, rs1, rs2):
    T = x.shape[0]
    tm = min(T, 512)

    def body(x_ref, mod_ref, g_ref, w_ref, b_ref, c_ref, s1_ref, s2_ref,
             h_ref, q_ref, k_ref, v_ref, u_ref):
        xf = x_ref[...]
        h = (xf * _rstd(xf) * g_ref[0:1, :]) * (1.0 + mod_ref[1:2, :]) + mod_ref[0:1, :]
        hb = h.astype(BF16)
        h_ref[...] = hb
        c = c_ref[...]
        s1 = s1_ref[...]
        s2 = s2_ref[...]
        q = _mm(hb, w_ref[:, 0:ATTN_W]) + b_ref[:, 0:ATTN_W]
        q = _rot_fwd(q, jnp.tile(c, (1, 4)), jnp.tile(s1, (1, 4)), jnp.tile(s2, (1, 4)))
        q_ref[...] = (q * (HEAD ** -0.5)).astype(BF16)
        k = _mm(hb, w_ref[:, ATTN_W:ATTN_W + KV_W]) + b_ref[:, ATTN_W:ATTN_W + KV_W]
        _store_dup(k_ref, _rot_fwd(k, c, s1, s2))
        v = _mm(hb, w_ref[:, ATTN_W + KV_W:ATTN_W + 2 * KV_W]) + b_ref[:, ATTN_W + KV_W:ATTN_W + 2 * KV_W]
        _store_dup(v_ref, v)
        u_ref[...] = _mm(hb, w_ref[:, ATTN_W + 2 * KV_W:IN_W]) + b_ref[:, ATTN_W + 2 * KV_W:IN_W]

    return pl.pallas_call(
        body, name="fwd_in", grid=(T // tm,),
        in_specs=[_rows(tm, D_MODEL), _full((8, D_MODEL)), _full((8, D_MODEL)),
                  _resident((D_MODEL, IN_W)), _full((1, IN_W)),
                  _rows(tm, 128), _rows(tm, 128), _rows(tm, 128)],
        out_specs=[_rows(tm, D_MODEL), _rows(tm, ATTN_W), _rows(tm, KVD_W), _rows(tm, KVD_W), _rows(tm, POOL_W)],
        out_shape=[_sds((T, D_MODEL), BF16), _sds((T, ATTN_W), BF16), _sds((T, KVD_W), BF16),
                   _sds((T, KVD_W), BF16), _sds((T, POOL_W), F32)],
        compiler_params=_cp("parallel"),
    )(x, mod8, g8, w_in, b_in, rc, rs1, rs2)


def _band_mask(n):
    kk = lax.broadcasted_iota(jnp.int32, (2 * BLK, BLK), 0)
    qi = lax.broadcasted_iota(jnp.int32, (2 * BLK, BLK), 1)
    first = jnp.where(n > 0, 0, 2 * BLK)
    in_prev = jnp.logical_and(kk < BLK, kk > qi + first)
    in_cur = jnp.logical_and(kk >= BLK, (kk - BLK) <= qi)
    one = jnp.logical_or(in_prev, in_cur)
    return jnp.concatenate([one] * GROUP, axis=1)


def _head_row(ref, j):
    return jnp.concatenate([ref[GROUP * j + r:GROUP * j + r + 1, :] for r in range(GROUP)], axis=1)


def _stack_heads(x_ref, j):
    low = lax.broadcasted_iota(jnp.int32, (BLK, 128), 1) < HEAD
    parts = []
    for gp in (2 * j, 2 * j + 1):
        x2 = x_ref[:, gp * 128:(gp + 1) * 128]
        parts.append(jnp.where(low, x2, jnp.zeros_like(x2)))
        parts.append(jnp.where(low, jnp.zeros_like(x2), x2))
    return jnp.concatenate(parts, axis=0)


def _unstack_heads(o):
    low = lax.broadcasted_iota(jnp.int32, (BLK, 128), 1) < HEAD
    return [jnp.where(low, o[0:BLK], o[BLK:2 * BLK]), jnp.where(low, o[2 * BLK:3 * BLK], o[3 * BLK:4 * BLK])]


def _attn_fwd(q, kd, vd, sink_b):
    T = q.shape[0]
    nb = T // BLK

    def body(q_ref, kp_ref, kc_ref, vp_ref, vc_ref, sk_ref, o_ref, lse_ref):
        valid = _band_mask(pl.program_id(0))
        for j in range(N_HEADS // GROUP):
            lanes = slice(j * 128, (j + 1) * 128)
            kcat = jnp.concatenate([kp_ref[:, lanes], kc_ref[:, lanes]], axis=0)
            vcat = jnp.concatenate([vp_ref[:, lanes], vc_ref[:, lanes]], axis=0)
            s = jnp.where(valid, _nt(kcat, _stack_heads(q_ref, j)), NEG_INF)
            sk = _head_row(sk_ref, j)
            m = jnp.maximum(jnp.max(s, axis=0, keepdims=True), sk)
            p = jnp.exp(s - m)
            den = jnp.sum(p, axis=0, keepdims=True) + jnp.exp(sk - m)
            p = p * (1.0 / den)
            o = _tn(p.astype(BF16), vcat)
            o_ref[:, 2 * j * 128:(2 * j + 2) * 128] = jnp.concatenate(_unstack_heads(o), axis=1).astype(BF16)
            lse = m + jnp.log(den)
            for r in range(GROUP):
                lse_ref[GROUP * j + r:GROUP * j + r + 1, :] = lse[:, r * 128:(r + 1) * 128]

    prev = lambda n: (jnp.maximum(n - 1, 0), 0)
    cur = lambda n: (n, 0)
    return pl.pallas_call(
        body, name="attn_fwd", grid=(nb,),
        in_specs=[pl.BlockSpec((BLK, ATTN_W), cur),
                  pl.BlockSpec((BLK, KVD_W), prev), pl.BlockSpec((BLK, KVD_W), cur),
                  pl.BlockSpec((BLK, KVD_W), prev), pl.BlockSpec((BLK, KVD_W), cur),
                  _full((8, 128))],
        out_specs=[pl.BlockSpec((BLK, ATTN_W), cur), pl.BlockSpec((N_HEADS, 128), cur)],
        out_shape=[_sds((T, ATTN_W), BF16), _sds((nb * N_HEADS, 128), F32)],
        compiler_params=_cp("parallel"),
    )(q, kd, kd, vd, vd, sink_b)


def _pool_fwd(u, pool_w, pool_scale):
    T = u.shape[0]
    tm = min(T, 512)

    def body(u_ref, w_ref, sc_ref, out_ref, pooled_ref, halo):
        i = pl.program_id(0)

        @pl.when(i == 0)
        def _():
            halo[...] = jnp.zeros_like(halo)

        ub = u_ref[...]
        ext = jnp.concatenate([halo[...], ub], axis=0)
        halo[...] = ub[tm - HALO:, :]
        tpos = (i * tm + lax.broadcasted_iota(jnp.int32, (tm, 1), 0)).astype(F32)
        for g, w in enumerate(POOL_WINDOWS):
            lanes = slice(g * 128, (g + 1) * 128)
            s = ext[:, lanes]
            sh = 1
            while sh < w:
                s = s + pltpu.roll(s, sh, 0)
                sh *= 2
            cnt = jnp.minimum(tpos + 1.0, float(w))
            pb = (s[HALO:, :] / cnt - ub[:, lanes]).astype(BF16)
            z = _mm(pb, w_ref[g].astype(BF16))
            out_ref[:, lanes] = (z * sc_ref[:, lanes]).astype(BF16)
            pooled_ref[:, lanes] = pb

    return pl.pallas_call(
        body, name="pool_fwd", grid=(T // tm,),
        in_specs=[_rows(tm, POOL_W), _full((4, 128, 128)), _full((1, POOL_W))],
        out_specs=[_rows(tm, POOL_W), _rows(tm, POOL_W)],
        out_shape=[_sds((T, POOL_W), BF16), _sds((T, POOL_W), BF16)],
        scratch_shapes=[pltpu.VMEM((HALO, POOL_W), F32)],
        compiler_params=_cp("arbitrary"),
    )(u, pool_w, pool_scale)


def _fwd_out(attn, pool, x, w_out, g8, mod8):
    T = x.shape[0]
    tm = min(T, 512)

    def body(a_ref, p_ref, x_ref, w_ref, g_ref, mod_ref, mix_ref, x1_ref):
        mix = _mm(a_ref[...], w_ref[0:ATTN_W, :]) + _mm(p_ref[...], w_ref[ATTN_W:, :])
        mix_ref[...] = mix
        x1_ref[...] = x_ref[...] + mod_ref[2:3, :] * (mix * _rstd(mix) * g_ref[1:2, :])

    return pl.pallas_call(
        body, name="fwd_out", grid=(T // tm,),
        in_specs=[_rows(tm, ATTN_W), _rows(tm, POOL_W), _rows(tm, D_MODEL),
                  _resident((D_MODEL, D_MODEL)), _full((8, D_MODEL)), _full((8, D_MODEL))],
        out_specs=[_rows(tm, D_MODEL), _rows(tm, D_MODEL)],
        out_shape=[_sds((T, D_MODEL), F32), _sds((T, D_MODEL), F32)],
        compiler_params=_cp("parallel"),
    )(attn, pool, x, w_out, g8, mod8)


def _sh_rows(tm):
    return pl.BlockSpec((N_SHARD, tm, FF_SH), lambda i: (0, i, 0))


def _ffn_fwd(x1, mod8, g8, wg, wu, wd):
    T = x1.shape[0]
    tm = min(T, 256)

    def body(x_ref, mod_ref, g_ref, wg_ref, wu_ref, wd_ref, h_ref, act_ref, ga_ref, gb_ref, f_ref, x2_ref):
        xf = x_ref[...]
        h = (xf * _rstd(xf) * g_ref[2:3, :]) * (1.0 + mod_ref[4:5, :]) + mod_ref[3:4, :]
        hb = h.astype(BF16)
        h_ref[...] = hb
        f = jnp.zeros((tm, D_MODEL), F32)
        for s in range(N_SHARD):
            a = _mm(hb, wg_ref[s])
            b = _mm(hb, wu_ref[s])
            sig = jax.nn.sigmoid(a)
            sl = a * sig
            act = (sl * b).astype(BF16)
            act_ref[s] = act
            ga_ref[s] = (b * (sig * (1.0 + a * (1.0 - sig)))).astype(BF16)
            gb_ref[s] = sl.astype(BF16)
            f = f + _mm(act, wd_ref[s])
        f_ref[...] = f
        x2_ref[...] = xf + mod_ref[5:6, :] * (f * _rstd(f) * g_ref[3:4, :])

    act_shape = _sds((N_SHARD, T, FF_SH), BF16)
    return pl.pallas_call(
        body, name="ffn_fwd", grid=(T // tm,),
        in_specs=[_rows(tm, D_MODEL), _full((8, D_MODEL)), _full((8, D_MODEL)),
                  _resident((N_SHARD, D_MODEL, FF_SH)), _resident((N_SHARD, D_MODEL, FF_SH)),
                  _resident((N_SHARD, FF_SH, D_MODEL))],
        out_specs=[_rows(tm, D_MODEL), _sh_rows(tm), _sh_rows(tm), _sh_rows(tm), _rows(tm, D_MODEL),
                   _rows(tm, D_MODEL)],
        out_shape=[_sds((T, D_MODEL), BF16), act_shape, act_shape, act_shape, _sds((T, D_MODEL), F32),
                   _sds((T, D_MODEL), F32)],
        compiler_params=_cp("parallel"),
    )(x1, mod8, g8, wg, wu, wd)


def _loss_grad(y, target):
    T = y.shape[0]
    tm = min(T, 1024)

    def body(y_ref, t_ref, dy_ref, loss_ref):
        @pl.when(pl.program_id(0) == 0)
        def _():
            loss_ref[...] = jnp.zeros_like(loss_ref)

        e = y_ref[...] - t_ref[...]
        dy_ref[...] = e * (1.0 / D_MODEL)
        part = 0.5 * jnp.sum(jnp.mean(e * e, axis=-1, keepdims=True), axis=0, keepdims=True)
        loss_ref[...] += part

    return pl.pallas_call(
        body, name="loss_grad", grid=(T // tm,),
        in_specs=[_rows(tm, D_MODEL), _rows(tm, D_MODEL)],
        out_specs=[_rows(tm, D_MODEL), _full((8, 128))],
        out_shape=[_sds((T, D_MODEL), F32), _sds((8, 128), F32)],
        compiler_params=_cp("arbitrary"),
    )(y, target)


def _ffn_bwd(dx2, f, ga, gb, x1, mod8, g8, wg, wu, wd):
    T = dx2.shape[0]
    tm = min(T, 256)

    def body(dx_ref, f_ref, ga_ref, gb_ref, x_ref, mod_ref, g_ref, wg_ref, wu_ref, wd_ref,
             dx1_ref, df_ref, da_ref, db_ref, red_ref):
        @pl.when(pl.program_id(0) == 0)
        def _():
            red_ref[...] = jnp.zeros_like(red_ref)

        dx = dx_ref[...]
        fv = f_ref[...]
        rstd = _rstd(fv)
        fhat = fv * rstd
        gpost = g_ref[3:4, :]
        red_ref[0:1, :] += _colsum(dx * (fhat * gpost))
        dn = dx * mod_ref[5:6, :]
        red_ref[1:2, :] += _colsum(dn * fhat)
        dfb = _norm_bwd(dn * gpost, fhat, rstd).astype(BF16)
        df_ref[...] = dfb
        dh = jnp.zeros((tm, D_MODEL), F32)
        for s in range(N_SHARD):
            dact = _nt(dfb, wd_ref[s])
            da = (dact * ga_ref[s].astype(F32)).astype(BF16)
            db = (dact * gb_ref[s].astype(F32)).astype(BF16)
            da_ref[s] = da
            db_ref[s] = db
            dh = dh + _nt(da, wg_ref[s]) + _nt(db, wu_ref[s])
        xf = x_ref[...]
        rstd1 = _rstd(xf)
        xhat = xf * rstd1
        gpre = g_ref[2:3, :]
        scale1 = 1.0 + mod_ref[4:5, :]
        red_ref[2:3, :] += _colsum(dh)
        red_ref[3:4, :] += _colsum(dh * (xhat * gpre))
        red_ref[4:5, :] += _colsum(dh * scale1 * xhat)
        dx1_ref[...] = dx + _norm_bwd(dh * scale1 * gpre, xhat, rstd1)

    act_shape = _sds((N_SHARD, T, FF_SH), BF16)
    return pl.pallas_call(
        body, name="ffn_bwd", grid=(T // tm,),
        in_specs=[_rows(tm, D_MODEL), _rows(tm, D_MODEL), _sh_rows(tm), _sh_rows(tm), _rows(tm, D_MODEL),
                  _full((8, D_MODEL)), _full((8, D_MODEL)),
                  _resident((N_SHARD, D_MODEL, FF_SH)), _resident((N_SHARD, D_MODEL, FF_SH)),
                  _resident((N_SHARD, FF_SH, D_MODEL))],
        out_specs=[_rows(tm, D_MODEL), _rows(tm, D_MODEL), _sh_rows(tm), _sh_rows(tm), _full((8, D_MODEL))],
        out_shape=[_sds((T, D_MODEL), F32), _sds((T, D_MODEL), BF16), act_shape, act_shape, _sds((8, D_MODEL), F32)],
        compiler_params=_cp("arbitrary"),
    )(dx2, f, ga, gb, x1, mod8, g8, wg, wu, wd)


def _wgrad(a, b, name):
    T, K = a.shape
    N = b.shape[1]
    tt = min(T, 1024)
    tk = min(K, 512)

    def body(a_ref, b_ref, o_ref):
        @pl.when(pl.program_id(1) == 0)
        def _():
            o_ref[...] = jnp.zeros_like(o_ref)

        o_ref[...] += _tn(a_ref[...], b_ref[...])

    return pl.pallas_call(
        body, name=name, grid=(K // tk, T // tt),
        in_specs=[pl.BlockSpec((tt, tk), lambda i, t: (t, i)), pl.BlockSpec((tt, N), lambda i, t: (t, 0))],
        out_specs=pl.BlockSpec((tk, N), lambda i, t: (i, 0)),
        out_shape=_sds((K, N), F32),
        compiler_params=_cp("parallel", "arbitrary"),
    )(a, b)


def _wgrad_cols(a, b, name):
    T, K = a.shape
    n = b.shape[2]
    tt = min(T, 1024)

    def body(a_ref, b_ref, o_ref):
        @pl.when(pl.program_id(1) == 0)
        def _():
            o_ref[...] = jnp.zeros_like(o_ref)

        o_ref[...] += _tn(a_ref[...], b_ref[...])

    return pl.pallas_call(
        body, name=name, grid=(N_SHARD, T // tt),
        in_specs=[pl.BlockSpec((tt, K), lambda s, t: (t, 0)), pl.BlockSpec((None, tt, n), lambda s, t: (s, t, 0))],
        out_specs=pl.BlockSpec((None, K, n), lambda s, t: (s, 0, 0)),
        out_shape=_sds((N_SHARD, K, n), F32),
        compiler_params=_cp("parallel", "arbitrary"),
    )(a, b)


def _wgrad_rows(a, b, name):
    T, N = b.shape
    k = a.shape[2]
    tt = min(T, 1024)

    def body(a_ref, b_ref, o_ref):
        @pl.when(pl.program_id(1) == 0)
        def _():
            o_ref[...] = jnp.zeros_like(o_ref)

        o_ref[...] += _tn(a_ref[...], b_ref[...])

    return pl.pallas_call(
        body, name=name, grid=(N_SHARD, T // tt),
        in_specs=[pl.BlockSpec((None, tt, k), lambda s, t: (s, t, 0)), pl.BlockSpec((tt, N), lambda s, t: (t, 0))],
        out_specs=pl.BlockSpec((None, k, N), lambda s, t: (s, 0, 0)),
        out_shape=_sds((N_SHARD, k, N), F32),
        compiler_params=_cp("parallel", "arbitrary"),
    )(a, b)


def _mix_bwd(dx1, mix, mod8, g8, w_out):
    T = dx1.shape[0]
    tm = min(T, 512)

    def body(dx_ref, mix_ref, mod_ref, g_ref, w_ref, dmix_ref, da_ref, dp_ref, red_ref):
        @pl.when(pl.program_id(0) == 0)
        def _():
            red_ref[...] = jnp.zeros_like(red_ref)

        dx = dx_ref[...]
        mv = mix_ref[...]
        rstd = _rstd(mv)
        mhat = mv * rstd
        gpost = g_ref[1:2, :]
        red_ref[0:1, :] += _colsum(dx * (mhat * gpost))
        dn = dx * mod_ref[2:3, :]
        red_ref[1:2, :] += _colsum(dn * mhat)
        dmb = _norm_bwd(dn * gpost, mhat, rstd).astype(BF16)
        dmix_ref[...] = dmb
        da_ref[...] = _nt(dmb, w_ref[0:ATTN_W, :]).astype(BF16)
        dp_ref[...] = _nt(dmb, w_ref[ATTN_W:, :]).astype(BF16)

    return pl.pallas_call(
        body, name="mix_bwd", grid=(T // tm,),
        in_specs=[_rows(tm, D_MODEL), _rows(tm, D_MODEL), _full((8, D_MODEL)), _full((8, D_MODEL)),
                  _resident((D_MODEL, D_MODEL))],
        out_specs=[_rows(tm, D_MODEL), _rows(tm, ATTN_W), _rows(tm, POOL_W), _full((8, D_MODEL))],
        out_shape=[_sds((T, D_MODEL), BF16), _sds((T, ATTN_W), BF16), _sds((T, POOL_W), BF16),
                   _sds((8, D_MODEL), F32)],
        compiler_params=_cp("arbitrary"),
    )(dx1, mix, mod8, g8, w_out)


def _attn_bwd(q, kd, vd, lse, dattn, sink_b):
    T = q.shape[0]
    nb = T // BLK

    def body(q_ref, do_ref, lse_ref, kp_ref, kc_ref, vp_ref, vc_ref, sk_ref,
             dq_ref, dk_ref, dv_ref, dsk_ref, carry_k, carry_v):
        n = pl.program_id(0)

        @pl.when(n == 0)
        def _():
            carry_k[...] = jnp.zeros_like(carry_k)
            carry_v[...] = jnp.zeros_like(carry_v)
            dsk_ref[...] = jnp.zeros_like(dsk_ref)

        @pl.when(n < nb)
        def _():
            valid = _band_mask(n)
            for j in range(N_HEADS // GROUP):
                lanes = slice(j * 128, (j + 1) * 128)
                kcat = jnp.concatenate([kp_ref[:, lanes], kc_ref[:, lanes]], axis=0)
                vcat = jnp.concatenate([vp_ref[:, lanes], vc_ref[:, lanes]], axis=0)
                qs = _stack_heads(q_ref, j)
                dos = _stack_heads(do_ref, j)
                lse = _head_row(lse_ref, j)
                p = jnp.exp(jnp.where(valid, _nt(kcat, qs), NEG_INF) - lse)
                dp = _nt(vcat, dos)
                delta = jnp.sum(p * dp, axis=0, keepdims=True)
                ds = (p * (dp - delta)).astype(BF16)
                sink_term = jnp.exp(_head_row(sk_ref, j) - lse) * delta
                for r in range(GROUP):
                    h = GROUP * j + r
                    dsk_ref[h:h + 1, :] += -jnp.sum(sink_term[:, r * 128:(r + 1) * 128], axis=1, keepdims=True)
                dq_ref[:, 2 * j * 128:(2 * j + 2) * 128] = jnp.concatenate(_unstack_heads(_tn(ds, kcat)), axis=1)
                dk = _mm(ds, qs)
                dv = _mm(p.astype(BF16), dos)
                dk_ref[:, lanes] = carry_k[:, lanes] + dk[0:BLK]
                dv_ref[:, lanes] = carry_v[:, lanes] + dv[0:BLK]
                carry_k[:, lanes] = dk[BLK:]
                carry_v[:, lanes] = dv[BLK:]

        @pl.when(n == nb)
        def _():
            dk_ref[...] = carry_k[...]
            dv_ref[...] = carry_v[...]

    cur = lambda n: (jnp.minimum(n, nb - 1), 0)
    prev = lambda n: (jnp.maximum(n - 1, 0), 0)
    return pl.pallas_call(
        body, name="attn_bwd", grid=(nb + 1,),
        in_specs=[pl.BlockSpec((BLK, ATTN_W), cur), pl.BlockSpec((BLK, ATTN_W), cur), pl.BlockSpec((N_HEADS, 128), cur),
                  pl.BlockSpec((BLK, KVD_W), prev), pl.BlockSpec((BLK, KVD_W), cur),
                  pl.BlockSpec((BLK, KVD_W), prev), pl.BlockSpec((BLK, KVD_W), cur),
                  _full((8, 128))],
        out_specs=[pl.BlockSpec((BLK, ATTN_W), cur), pl.BlockSpec((BLK, KVD_W), prev),
                   pl.BlockSpec((BLK, KVD_W), prev), _full((8, 128))],
        out_shape=[_sds((T, ATTN_W), F32), _sds((T, KVD_W), F32), _sds((T, KVD_W), F32), _sds((8, 128), F32)],
        scratch_shapes=[pltpu.VMEM((BLK, KVD_W), F32), pltpu.VMEM((BLK, KVD_W), F32)],
        compiler_params=_cp("arbitrary"),
    )(q, dattn, lse, kd, kd, vd, vd, sink_b)


def _pool_bwd(dpool, pooled, pool_w, pool_scale):
    T = dpool.shape[0]
    tm = min(T, 512)
    nbk = T // tm
    ext_rows = tm + HALO

    def body(dp_ref, pl_ref, w_ref, sc_ref, du_ref, dw_ref, dsc_ref, halo):
        i = pl.program_id(0)

        @pl.when(i == 0)
        def _():
            halo[...] = jnp.zeros_like(halo)
            dw_ref[...] = jnp.zeros_like(dw_ref)
            dsc_ref[...] = jnp.zeros_like(dsc_ref)

        blk = nbk - 1 - i
        tpos = (blk * tm + lax.broadcasted_iota(jnp.int32, (tm, 1), 0)).astype(F32)
        for g, w in enumerate(POOL_WINDOWS):
            lanes = slice(g * 128, (g + 1) * 128)
            dp = dp_ref[:, lanes].astype(F32)
            pb = pl_ref[:, lanes]
            wg = w_ref[g].astype(BF16)
            z = _mm(pb, wg)
            dsc_ref[0:1, lanes] += _colsum(dp * z)
            dz = (dp * sc_ref[:, lanes]).astype(BF16)
            dw_ref[g] += _tn(pb, dz)
            dpl = _nt(dz, wg)
            e = dpl / jnp.minimum(tpos + 1.0, float(w))
            s = jnp.concatenate([e, halo[:, lanes]], axis=0)
            halo[:, lanes] = e[0:HALO, :]
            sh = 1
            while sh < w:
                s = s + pltpu.roll(s, ext_rows - sh, 0)
                sh *= 2
            du_ref[:, lanes] = s[0:tm, :] - dpl

    rev = lambda i: (nbk - 1 - i, 0)
    return pl.pallas_call(
        body, name="pool_bwd", grid=(nbk,),
        in_specs=[pl.BlockSpec((tm, POOL_W), rev), pl.BlockSpec((tm, POOL_W), rev),
                  _full((4, 128, 128)), _full((1, POOL_W))],
        out_specs=[pl.BlockSpec((tm, POOL_W), rev), _full((4, 128, 128)), _full((8, POOL_W))],
        out_shape=[_sds((T, POOL_W), F32), _sds((4, 128, 128), F32), _sds((8, POOL_W), F32)],
        scratch_shapes=[pltpu.VMEM((HALO, POOL_W), F32)],
        compiler_params=_cp("arbitrary"),
    )(dpool, pooled, pool_w, pool_scale)


def _in_bwd(dq, dk, dv, du, rc, rs1, rs2, x, dx1, mod8, g8, w_in):
    T = x.shape[0]
    tm = min(T, 512)

    def body(dq_ref, dk_ref, dv_ref, du_ref, c_ref, s1_ref, s2_ref, x_ref, dx1_ref, mod_ref, g_ref, w_ref,
             dx_ref, dproj_ref, red_ref, dbin_ref):
        @pl.when(pl.program_id(0) == 0)
        def _():
            red_ref[...] = jnp.zeros_like(red_ref)
            dbin_ref[...] = jnp.zeros_like(dbin_ref)

        c = c_ref[...]
        s1 = s1_ref[...]
        s2 = s2_ref[...]
        dqp = _rot_bwd(dq_ref[...] * (HEAD ** -0.5), jnp.tile(c, (1, 4)), jnp.tile(s1, (1, 4)), jnp.tile(s2, (1, 4)))
        dkp = _rot_bwd(_fold_dup(dk_ref[...]), c, s1, s2)
        pieces = ((0, ATTN_W, dqp), (ATTN_W, ATTN_W + KV_W, dkp),
                  (ATTN_W + KV_W, ATTN_W + 2 * KV_W, _fold_dup(dv_ref[...])), (ATTN_W + 2 * KV_W, IN_W, du_ref[...]))
        dh = jnp.zeros((tm, D_MODEL), F32)
        for lo, hi, val in pieces:
            dbin_ref[0:1, lo:hi] += _colsum(val)
            vb = val.astype(BF16)
            dproj_ref[:, lo:hi] = vb
            dh = dh + _nt(vb, w_ref[:, lo:hi])
        xf = x_ref[...]
        rstd = _rstd(xf)
        xhat = xf * rstd
        gpre = g_ref[0:1, :]
        scale1 = 1.0 + mod_ref[1:2, :]
        red_ref[0:1, :] += _colsum(dh)
        red_ref[1:2, :] += _colsum(dh * (xhat * gpre))
        red_ref[2:3, :] += _colsum(dh * scale1 * xhat)
        dx_ref[...] = dx1_ref[...] + _norm_bwd(dh * scale1 * gpre, xhat, rstd)

    return pl.pallas_call(
        body, name="in_bwd", grid=(T // tm,),
        in_specs=[_rows(tm, ATTN_W), _rows(tm, KVD_W), _rows(tm, KVD_W), _rows(tm, POOL_W),
                  _rows(tm, 128), _rows(tm, 128), _rows(tm, 128), _rows(tm, D_MODEL), _rows(tm, D_MODEL),
                  _full((8, D_MODEL)), _full((8, D_MODEL)), _resident((D_MODEL, IN_W))],
        out_specs=[_rows(tm, D_MODEL), _rows(tm, IN_W), _full((8, D_MODEL)), _full((8, IN_W))],
        out_shape=[_sds((T, D_MODEL), F32), _sds((T, IN_W), BF16), _sds((8, D_MODEL), F32), _sds((8, IN_W), F32)],
        compiler_params=_cp("arbitrary"),
    )(dq, dk, dv, du, rc, rs1, rs2, x, dx1, mod8, g8, w_in)


def _mod_fwd(c_all, ada_w, ada_b_sh):
    tn = 512

    def body(c_ref, w_ref, b_ref, o_ref):
        cv = c_ref[...]
        ca = (cv * jax.nn.sigmoid(cv)).astype(BF16)
        o_ref[...] = _mm(ca, w_ref[...].astype(BF16)) + b_ref[...]

    return pl.pallas_call(
        body, name="mod_fwd", grid=(2, ADA_SH // tn),
        in_specs=[_full((8, D_MODEL)), pl.BlockSpec((None, D_MODEL, tn), lambda l, j: (l, 0, j)),
                  pl.BlockSpec((None, 1, tn), lambda l, j: (l, 0, j))],
        out_specs=pl.BlockSpec((None, 8, tn), lambda l, j: (l, 0, j)),
        out_shape=_sds((2, 8, ADA_SH), F32),
        compiler_params=_cp("parallel", "parallel"),
    )(c_all, ada_w, ada_b_sh)


def _ada_wgrad(c_all_t, dmod_sh):
    tn = 512

    def body(c_ref, d_ref, o_ref):
        cv = c_ref[...]
        ca = cv * jax.nn.sigmoid(cv)
        o_ref[...] = jnp.dot(ca, d_ref[...], preferred_element_type=F32, precision=lax.Precision.HIGHEST)

    return pl.pallas_call(
        body, name="ada_wgrad", grid=(2, ADA_SH // tn),
        in_specs=[_full((D_MODEL, 8)), pl.BlockSpec((None, 8, tn), lambda l, j: (l, 0, j))],
        out_specs=pl.BlockSpec((None, D_MODEL, tn), lambda l, j: (l, 0, j)),
        out_shape=_sds((2, D_MODEL, ADA_SH), F32),
        compiler_params=_cp("parallel", "parallel"),
    )(c_all_t, dmod_sh)


def _sum_devices(g):
    R = g.shape[1]

    def body(g_ref, o_ref):
        acc = g_ref[0]
        for d in range(1, N_DEV):
            acc = acc + g_ref[d]
        o_ref[...] = acc

    return pl.pallas_call(
        body, name="sum_devices", grid=(1,),
        in_specs=[_full((N_DEV, R, 128))], out_specs=_full((R, 128)), out_shape=_sds((R, 128), F32),
        compiler_params=_cp("arbitrary"),
    )(g)


def _adamw(w, g, m, v, name):
    R, C = w.shape
    tr = R
    for cand in (256, 128, 64, 32, 16, 8):
        if R % cand == 0 and cand * C * 4 <= 2 * 1024 * 1024:
            tr = cand
            break

    def body(w_ref, g_ref, m_ref, v_ref, d_ref, nm_ref, nv_ref):
        gv = g_ref[...]
        mn = ADAM_B1 * m_ref[...] + (1.0 - ADAM_B1) * gv
        vn = ADAM_B2 * v_ref[...] + (1.0 - ADAM_B2) * (gv * gv)
        m_hat = mn / (1.0 - ADAM_B1 ** ADAM_STEP)
        v_hat = vn / (1.0 - ADAM_B2 ** ADAM_STEP)
        d_ref[...] = -ADAM_LR * (m_hat / (jnp.sqrt(v_hat) + ADAM_EPS) + ADAM_WD * w_ref[...])
        nm_ref[...] = mn
        nv_ref[...] = vn

    spec = pl.BlockSpec((tr, C), lambda i: (i, 0))
    out = _sds((R, C), F32)
    return pl.pallas_call(
        body, name=name, grid=(R // tr,),
        in_specs=[spec] * 4, out_specs=[spec] * 3, out_shape=[out] * 3,
        compiler_params=_cp("parallel"),
    )(w, g, m, v)


def _adamw_nd(w, g, m, v, name):
    shape = w.shape
    if w.ndim == 2 and shape[1] < 128:
        view = (1, shape[0] * shape[1])
    else:
        view = (-1, shape[-1])
    outs = _adamw(*[t.reshape(view) for t in (w, g, m, v)], name=name)
    return [o.reshape(shape) for o in outs]


def _coords():
    return lax.axis_index("x"), lax.axis_index("y"), lax.axis_index("c")


def _other_chips(x, y):
    return [(1 - x, y), (x, 1 - y), (1 - x, 1 - y)]


def _allgather8(blk, name):
    m_per, n = blk.shape

    def body(x_ref, out_ref, send_sems, recv_sems, local_sem):
        x, y, c = _coords()
        me, sibling = (x, y, c), (x, y, 1 - c)
        chips = _other_chips(x, y)

        def rows(px, py, pc):
            return out_ref.at[pl.ds((4 * px + 2 * py + pc) * m_per, m_per), :]

        def copy(k, block, to, src=None):
            return pltpu.make_async_remote_copy(
                src_ref=rows(*block) if src is None else src, dst_ref=rows(*block),
                send_sem=send_sems.at[k], recv_sem=recv_sems.at[k], device_id=to, device_id_type=MESH)

        mine = pltpu.make_async_copy(x_ref, rows(*me), local_sem)
        mine.start()
        first = [copy(0, me, sibling, src=x_ref)]
        first += [copy(1 + j, me, (*chip, c), src=x_ref) for j, chip in enumerate(chips)]
        for cp in first:
            cp.start()
        passed = [copy(4 + j, (*chip, c), sibling) for j, chip in enumerate(chips)]
        for j, chip in enumerate(chips):
            copy(1 + j, (*chip, c), me).wait_recv()
            passed[j].start()
        copy(0, sibling, me).wait_recv()
        for j, chip in enumerate(chips):
            copy(4 + j, (*chip, 1 - c), me).wait_recv()
        for cp in first + passed:
            cp.wait_send()
        mine.wait()

    return pl.pallas_call(
        body, name=name,
        out_shape=_sds((N_DEV * m_per, n), blk.dtype),
        in_specs=[pl.BlockSpec(memory_space=pltpu.VMEM)],
        out_specs=pl.BlockSpec(memory_space=pltpu.VMEM),
        scratch_shapes=[pltpu.SemaphoreType.DMA((7,)), pltpu.SemaphoreType.DMA((7,)), pltpu.SemaphoreType.DMA],
        compiler_params=pltpu.CompilerParams(vmem_limit_bytes=VMEM_LIMIT),
    )(blk)


def _row_tile(r, n):
    for cand in (512, 256, 128, 64, 32, 16):
        if r % cand == 0 and cand * n * 4 <= 2 * 1024 * 1024:
            return cand
    return r


def _cast_slot(w, chip, name):
    r, n = w.shape
    tr = _row_tile(r, n)

    def body(chip_ref, w_ref, o_ref):
        o_ref[...] = w_ref[...].astype(BF16)

    grid_spec = pltpu.PrefetchScalarGridSpec(
        num_scalar_prefetch=1, grid=(r // tr,),
        in_specs=[pl.BlockSpec((tr, n), lambda i, ch: (i, 0))],
        out_specs=pl.BlockSpec((None, tr, n), lambda i, ch: (ch[0], i, 0)))
    return pl.pallas_call(
        body, name=name, grid_spec=grid_spec, out_shape=_sds((N_SHARD, r, n), BF16),
        compiler_params=_cp("arbitrary"),
    )(chip, w)


def _allgather_weights(bufs, name):
    nt = len(bufs)
    hom = [pl.BlockSpec(memory_space=pl.ANY)] * nt

    def body(*refs):
        outs = refs[nt:2 * nt]
        send_sems, recv_sems = refs[2 * nt:]
        x, y, c = _coords()
        sibling = (x, y, 1 - c)
        chips = _other_chips(x, y)

        def copy(t, k, block_chip, hc, to):
            r = outs[t].shape[1] // 2
            blk = outs[t].at[2 * block_chip[0] + block_chip[1], pl.ds(hc * r, r)]
            return pltpu.make_async_remote_copy(
                src_ref=blk, dst_ref=blk,
                send_sem=send_sems.at[t, k], recv_sem=recv_sems.at[t, k], device_id=to, device_id_type=MESH)

        started = []
        for t in range(nt):
            for j, chip in enumerate(chips):
                cp = copy(t, j, (x, y), c, (*chip, c))
                cp.start()
                started.append(cp)
        for t in range(nt):
            for j, chip in enumerate(chips):
                copy(t, j, chip, c, sibling).wait_recv()
                fw = copy(t, 3 + j, chip, c, sibling)
                fw.start()
                started.append(fw)
        for t in range(nt):
            for j, chip in enumerate(chips):
                copy(t, 3 + j, chip, 1 - c, sibling).wait_recv()
        for cp in started:
            cp.wait_send()

    return pl.pallas_call(
        body, name=name,
        out_shape=[_sds(b.shape, b.dtype) for b in bufs],
        in_specs=hom, out_specs=hom,
        input_output_aliases={t: t for t in range(nt)},
        scratch_shapes=[pltpu.SemaphoreType.DMA((nt, 6)), pltpu.SemaphoreType.DMA((nt, 6))],
    )(*bufs)


def _swap_halves(grads, name):
    nt = len(grads)
    hom = [pl.BlockSpec(memory_space=pl.ANY)] * nt

    def body(*refs):
        ins = refs[:nt]
        outs = refs[nt:2 * nt]
        send_sems, recv_sems = refs[2 * nt:]
        x, y, c = _coords()
        sibling = (x, y, 1 - c)
        cps = []
        for t in range(nt):
            r = ins[t].shape[1] // 2
            cp = pltpu.make_async_remote_copy(
                src_ref=ins[t].at[:, pl.ds((1 - c) * r, r)], dst_ref=outs[t],
                send_sem=send_sems.at[t], recv_sem=recv_sems.at[t], device_id=sibling, device_id_type=MESH)
            cp.start()
            cps.append(cp)
        for cp in cps:
            cp.wait()

    return pl.pallas_call(
        body, name=name,
        out_shape=[_sds((N_SHARD, g.shape[1] // 2, g.shape[2]), g.dtype) for g in grads],
        in_specs=hom, out_specs=hom,
        scratch_shapes=[pltpu.SemaphoreType.DMA((nt,)), pltpu.SemaphoreType.DMA((nt,))],
    )(*grads)


def _scatter_chips(sums, name):
    nt = len(sums)
    hom = [pl.BlockSpec(memory_space=pl.ANY)] * nt

    def body(*refs):
        ins = refs[:nt]
        outs = refs[nt:2 * nt]
        send_sems, recv_sems = refs[2 * nt:]
        x, y, c = _coords()
        chips = _other_chips(x, y)
        cps = []
        for t in range(nt):
            for j, chip in enumerate(chips):
                cp = pltpu.make_async_remote_copy(
                    src_ref=ins[t].at[2 * chip[0] + chip[1]], dst_ref=outs[t].at[j],
                    send_sem=send_sems.at[t, j], recv_sem=recv_sems.at[t, j],
                    device_id=(*chip, c), device_id_type=MESH)
                cp.start()
                cps.append(cp)
        for cp in cps:
            cp.wait()

    return pl.pallas_call(
        body, name=name,
        out_shape=[_sds((3,) + s.shape[1:], s.dtype) for s in sums],
        in_specs=hom, out_specs=hom,
        scratch_shapes=[pltpu.SemaphoreType.DMA((nt, 3)), pltpu.SemaphoreType.DMA((nt, 3))],
    )(*sums)


def _join_halves(tots, name):
    nt = len(tots)
    hom = [pl.BlockSpec(memory_space=pl.ANY)] * nt

    def body(*refs):
        outs = refs[nt:2 * nt]
        send_sems, recv_sems = refs[2 * nt:]
        x, y, c = _coords()
        sibling = (x, y, 1 - c)
        cps = []
        for t in range(nt):
            cp = pltpu.make_async_remote_copy(
                src_ref=outs[t].at[c], dst_ref=outs[t].at[c],
                send_sem=send_sems.at[t], recv_sem=recv_sems.at[t], device_id=sibling, device_id_type=MESH)
            cp.start()
            cps.append(cp)
        for t in range(nt):
            pltpu.make_async_remote_copy(
                src_ref=outs[t].at[c], dst_ref=outs[t].at[1 - c],
                send_sem=send_sems.at[t], recv_sem=recv_sems.at[t], device_id=sibling, device_id_type=MESH).wait_recv()
        for cp in cps:
            cp.wait_send()

    return pl.pallas_call(
        body, name=name,
        out_shape=[_sds(t.shape, t.dtype) for t in tots],
        in_specs=hom, out_specs=hom,
        input_output_aliases={t: t for t in range(nt)},
        scratch_shapes=[pltpu.SemaphoreType.DMA((nt,)), pltpu.SemaphoreType.DMA((nt,))],
    )(*tots)


def _pair_sum(g, recv, core, chip, name):
    _, _, r, n = g.shape
    tr = _row_tile(r, n)

    def body(core_ref, chip_ref, g_ref, r_ref, sb_ref, own_ref):
        tot = g_ref[...] + r_ref[...]
        sb_ref[...] = tot.astype(BF16)

        @pl.when(pl.program_id(1) == chip_ref[0])
        def _():
            own_ref[...] = tot

    grid_spec = pltpu.PrefetchScalarGridSpec(
        num_scalar_prefetch=2, grid=(r // tr, N_SHARD),
        in_specs=[pl.BlockSpec((None, None, tr, n), lambda i, s, co, ch: (s, co[0], i, 0)),
                  pl.BlockSpec((None, tr, n), lambda i, s, co, ch: (s, i, 0))],
        out_specs=[pl.BlockSpec((None, tr, n), lambda i, s, co, ch: (s, i, 0)),
                   pl.BlockSpec((tr, n), lambda i, s, co, ch: (i, 0))])
    return pl.pallas_call(
        body, name=name, grid_spec=grid_spec,
        out_shape=[_sds((N_SHARD, r, n), BF16), _sds((r, n), F32)],
        compiler_params=_cp("arbitrary", "arbitrary"),
    )(core, chip, g, recv)


def _chip_sum(own, recv, core, name):
    r, n = own.shape
    tr = _row_tile(r, n)

    def body(core_ref, o_ref, r_ref, t_ref):
        acc = o_ref[...]
        for j in range(3):
            acc = acc + r_ref[j].astype(F32)
        t_ref[...] = acc

    grid_spec = pltpu.PrefetchScalarGridSpec(
        num_scalar_prefetch=1, grid=(r // tr,),
        in_specs=[pl.BlockSpec((tr, n), lambda i, co: (i, 0)), pl.BlockSpec((3, tr, n), lambda i, co: (0, i, 0))],
        out_specs=pl.BlockSpec((None, tr, n), lambda i, co: (co[0], i, 0)))
    return pl.pallas_call(
        body, name=name, grid_spec=grid_spec, out_shape=_sds((2, r, n), F32),
        compiler_params=_cp("arbitrary"),
    )(core, own, recv)


_HBM = pl.BlockSpec(memory_space=pltpu.HBM)
_SEM = pl.BlockSpec(memory_space=pltpu.SEMAPHORE)
_EFFECT = pltpu.SideEffectType.DATAFLOW_SIDE_EFFECTING


def _ici_copies(srcs, dsts, send_sems, recv_sems, send_view, recv_view):
    x, y, c = _coords()
    out = []
    for t in range(len(srcs)):
        for j, chip in enumerate(_other_chips(x, y)):
            out.append(pltpu.make_async_remote_copy(
                src_ref=send_view(srcs[t], chip, j, (x, y), c), dst_ref=recv_view(dsts[t], chip, j, (x, y), c),
                send_sem=send_sems.at[3 * t + j], recv_sem=recv_sems.at[3 * t + j],
                device_id=(*chip, c), device_id_type=MESH))
    return out


def _ici_start(srcs, dsts, after, send_view, recv_view, name):
    nt = len(srcs)
    inplace = dsts is None
    nbuf = nt if inplace else 2 * nt

    def body(*refs):
        send_sems, recv_sems = refs[nbuf + 1], refs[nbuf + 2]
        s_out = refs[nbuf + 3:nbuf + 3 + nt]
        d_out = s_out if inplace else refs[nbuf + 3 + nt:nbuf + 3 + 2 * nt]
        token = refs[-1]
        for cp in _ici_copies(s_out, d_out, send_sems, recv_sems, send_view, recv_view):
            cp.start()
        token[...] = jnp.zeros_like(token)

    bufs = list(srcs) + ([] if inplace else list(dsts))
    res = pl.pallas_call(
        body, name=name,
        out_shape=(pltpu.SemaphoreType.DMA((3 * nt,)), pltpu.SemaphoreType.DMA((3 * nt,)),
                   *[pltpu.HBM(b.shape, b.dtype) for b in bufs], _sds((8, 128), F32)),
        in_specs=[_HBM] * nbuf + [pl.BlockSpec(memory_space=pl.ANY)],
        out_specs=(_SEM, _SEM, *[_HBM] * nbuf, pl.BlockSpec(memory_space=pltpu.VMEM)),
        input_output_aliases={i: 2 + i for i in range(nbuf)},
        compiler_params=pltpu.CompilerParams(has_side_effects=_EFFECT),
    )(*[pltpu.with_memory_space_constraint(b, pltpu.HBM) for b in bufs], after)
    send_sems, recv_sems = res[0], res[1]
    s_thru = list(res[2:2 + nt])
    d_thru = s_thru if inplace else list(res[2 + nt:2 + 2 * nt])
    return send_sems, recv_sems, s_thru, d_thru, res[-1]


def _ici_wait(send_sems, recv_sems, srcs, dsts, after, send_view, recv_view, name):
    nt = len(srcs)
    inplace = dsts is None
    nbuf = nt if inplace else 2 * nt

    def body(*refs):
        send_ref, recv_ref = refs[nbuf], refs[nbuf + 1]
        s_out = refs[nbuf + 3:nbuf + 3 + nt]
        d_out = s_out if inplace else refs[nbuf + 3 + nt:nbuf + 3 + 2 * nt]
        for cp in _ici_copies(s_out, d_out, send_ref, recv_ref, send_view, recv_view):
            cp.wait_send()
            cp.wait_recv()

    bufs = list(srcs) + ([] if inplace else list(dsts))
    res = pl.pallas_call(
        body, name=name,
        out_shape=tuple(pltpu.HBM(b.shape, b.dtype) for b in bufs),
        in_specs=[_HBM] * nbuf + [_SEM, _SEM, pl.BlockSpec(memory_space=pl.ANY)],
        out_specs=tuple([_HBM] * nbuf),
        input_output_aliases={i: i for i in range(nbuf)},
        compiler_params=pltpu.CompilerParams(has_side_effects=_EFFECT),
    )(*bufs, send_sems, recv_sems, after)
    return list(res[:nt]) if inplace else list(res[nt:])


def _w_half(buf, chip, c):
    r = buf.shape[1] // 2
    return buf.at[2 * chip[0] + chip[1], pl.ds(c * r, r)]


def _ag_send_view(buf, chip, j, me, c):
    return _w_half(buf, me, c)


def _ag_recv_view(buf, chip, j, me, c):
    return _w_half(buf, me, c)


def _rs_send_view(buf, chip, j, me, c):
    return buf.at[2 * chip[0] + chip[1]]


def _rs_recv_view(buf, chip, j, me, c):
    return buf.at[j]


def _ag_forward(bufs, name):
    nt = len(bufs)
    hom = [pl.BlockSpec(memory_space=pl.ANY)] * nt

    def body(*refs):
        outs = refs[nt:2 * nt]
        send_sems, recv_sems = refs[2 * nt:]
        x, y, c = _coords()
        sibling = (x, y, 1 - c)
        chips = _other_chips(x, y)

        def copy(t, j, hc):
            blk = _w_half(outs[t], chips[j], hc)
            return pltpu.make_async_remote_copy(
                src_ref=blk, dst_ref=blk, send_sem=send_sems.at[t, j], recv_sem=recv_sems.at[t, j],
                device_id=sibling, device_id_type=MESH)

        started = [copy(t, j, c) for t in range(nt) for j in range(3)]
        for cp in started:
            cp.start()
        for t in range(nt):
            for j in range(3):
                copy(t, j, 1 - c).wait_recv()
        for cp in started:
            cp.wait_send()

    return pl.pallas_call(
        body, name=name,
        out_shape=[_sds(b.shape, b.dtype) for b in bufs],
        in_specs=hom, out_specs=hom,
        input_output_aliases={t: t for t in range(nt)},
        scratch_shapes=[pltpu.SemaphoreType.DMA((nt, 3)), pltpu.SemaphoreType.DMA((nt, 3))],
    )(*bufs)


def _rs_begin(grads, after, tag):
    x, y, c = _coords()
    core = jnp.reshape(c, (1,)).astype(jnp.int32)
    chip = jnp.reshape(2 * x + y, (1,)).astype(jnp.int32)
    recv = _swap_halves(grads, name="rs_swap_" + tag)
    sums, owns = [], []
    for t, (g, rv) in enumerate(zip(grads, recv)):
        r = g.shape[1] // 2
        sb, own = _pair_sum(g.reshape(N_SHARD, 2, r, g.shape[2]), rv, core, chip, name=f"rs_pair_{tag}_{t}")
        sums.append(sb)
        owns.append(own)
    land = [lax.empty((3,) + s.shape[1:], s.dtype) for s in sums]
    send_sems, recv_sems, s_thru, d_thru, token = _ici_start(
        sums, land, after, _rs_send_view, _rs_recv_view, name="rs_start_" + tag)
    return dict(sems=(send_sems, recv_sems), sums=s_thru, land=d_thru, owns=owns, core=core, tag=tag), token


def _rs_end(state, after):
    tag = state["tag"]
    got = _ici_wait(*state["sems"], state["sums"], state["land"], after, _rs_send_view, _rs_recv_view,
                    name="rs_wait_" + tag)
    tots = [_chip_sum(o, gt, state["core"], name=f"rs_chip_{tag}_{t}")
            for t, (o, gt) in enumerate(zip(state["owns"], got))]
    full = _join_halves(tots, name="rs_join_" + tag)
    return [f.reshape(2 * f.shape[1], f.shape[2]) for f in full]


def _rope_lane_table():
    d = jnp.arange(128) % HEAD
    inv_freq = ROPE_THETA ** (-jnp.arange(0, ROT, 2, dtype=F32) / ROT)
    rot = d < ROT
    rows = [jnp.where(rot, inv_freq[d % (ROT // 2)], 0.0), rot.astype(F32),
            (d < ROT // 2).astype(F32), jnp.logical_and(d >= ROT // 2, rot).astype(F32)]
    return jnp.concatenate([jnp.stack(rows), jnp.zeros((4, 128), F32)], axis=0)


def _pad8(rows):
    return jnp.concatenate([rows, jnp.zeros((8 - rows.shape[0], rows.shape[1]), F32)], axis=0)


def kernel(x, c, positions, ada_w, ada_b, w_in, b_in, sinks, pool_w, pool_scale, w_out, w_gate, w_up, w_down, g_pre_mix, g_post_mix, g_pre_ffn, g_post_ffn, loss_target, m_ada_w, m_ada_b, m_w_in, m_b_in, m_sinks, m_pool_w, m_pool_scale, m_w_out, m_w_gate, m_w_up, m_w_down, m_g_pre_mix, m_g_post_mix, m_g_pre_ffn, m_g_post_ffn, v_ada_w, v_ada_b, v_w_in, v_b_in, v_sinks, v_pool_w, v_pool_scale, v_w_out, v_w_gate, v_w_up, v_w_down, v_g_pre_mix, v_g_post_mix, v_g_pre_ffn, v_g_post_ffn):
    T = x.shape[1]
    n_layers = ada_w.shape[0]
    ax, ay, ac = _coords()
    my_dev = 4 * ax + 2 * ay + ac
    my_chip = 2 * ax + ay
    x0 = x.reshape(T, D_MODEL)
    target = loss_target.reshape(T, D_MODEL)

    c_all = _allgather8(c.reshape(8, 128), name="ag_c").reshape(N_DEV, D_MODEL)
    ada_b_sh = lax.dynamic_slice_in_dim(ada_b, my_chip * ADA_SH, ADA_SH, axis=1).reshape(n_layers, 1, ADA_SH)
    mod_part = _mod_fwd(c_all, ada_w, ada_b_sh)
    mod_all = _allgather8(mod_part.reshape(n_layers * 8, ADA_SH), name="ag_mod")
    mod_all = mod_all.reshape(N_DEV, n_layers, 8, ADA_SH)[0::2]
    mod_mine = lax.dynamic_index_in_dim(mod_all, my_dev, axis=2, keepdims=False)
    mod = jnp.transpose(mod_mine, (1, 0, 2)).reshape(n_layers, 6, D_MODEL)

    pos_b = jnp.broadcast_to(positions.reshape(T, 1), (T, 128))
    rc, rs1, rs2 = _rope_tables(pos_b, _rope_lane_table())

    chip1 = jnp.reshape(my_chip, (1,)).astype(jnp.int32)

    def cast_layer(l):
        return [_cast_slot(w[l], chip1, name=f"cast_{nm}{l}")
                for nm, w in (("w_in", w_in), ("w_out", w_out), ("w_gate", w_gate), ("w_up", w_up), ("w_down", w_down))]

    def as_operands(bufs):
        gin, gout, gg, gu, gd = bufs
        win_full = jnp.transpose(gin, (1, 0, 2)).reshape(D_MODEL, IN_W)
        return win_full, gout.reshape(D_MODEL, D_MODEL), gg, gu, gd

    bufs0 = cast_layer(0)
    win0 = _allgather_weights(bufs0[:1], name="ag_w0_in")
    rest_send, rest_recv, rest_bufs, _, ag_token = _ici_start(
        bufs0[1:], None, win0[0], _ag_send_view, _ag_recv_view, name="ag_start_0")
    weights = [None] * n_layers

    saved = []
    xl = x0
    for l in range(n_layers):
        mod8 = _pad8(mod[l])
        if l + 1 < n_layers:
            ag_send, ag_recv, ag_bufs, _, ag_token = _ici_start(
                cast_layer(l + 1), None, ag_token, _ag_send_view, _ag_recv_view, name=f"ag_start_{l + 1}")
        if l == 0 or l + 1 < n_layers:
            mod8 = mod8 + ag_token[0, 0]
        g8 = _pad8(jnp.stack([g_pre_mix[l], g_post_mix[l], g_pre_ffn[l], g_post_ffn[l]]))
        sink_b = jnp.broadcast_to(sinks[l][:, None], (N_HEADS, 128))
        psc = pool_scale[l].reshape(1, POOL_W)
        win = jnp.transpose(win0[0], (1, 0, 2)).reshape(D_MODEL, IN_W) if l == 0 else weights[l][0]
        h, q, k, v, u = _fwd_in(xl, mod8, g8, win, b_in[l].reshape(1, IN_W), rc, rs1, rs2)
        attn, lse = _attn_fwd(q, k, v, sink_b)
        pool, pooled = _pool_fwd(u, pool_w[l], psc)
        if l == 0:
            arrived = _ici_wait(rest_send, rest_recv, rest_bufs, None, pool, _ag_send_view, _ag_recv_view,
                                name="ag_wait_0")
            weights[0] = as_operands(win0 + _ag_forward(arrived, name="ag_fwd_0"))
        win, wout, wg, wu, wd = weights[l]
        mix, x1 = _fwd_out(attn, pool, xl, wout, g8, mod8)
        h2, act, ga, gb, f, x2 = _ffn_fwd(x1, mod8, g8, wg, wu, wd)
        saved.append(dict(x=xl, h=h, q=q, k=k, v=v, lse=lse, attn=attn, pool=pool, pooled=pooled, mix=mix,
                          x1=x1, h2=h2, act=act, ga=ga, gb=gb, f=f, mod8=mod8, g8=g8, sink_b=sink_b, psc=psc))
        xl = x2
        if l + 1 < n_layers:
            arrived = _ici_wait(ag_send, ag_recv, ag_bufs, None, x2, _ag_send_view, _ag_recv_view,
                                name=f"ag_wait_{l + 1}")
            weights[l + 1] = as_operands(_ag_forward(arrived, name=f"ag_fwd_{l + 1}"))

    dy, loss_tile = _loss_grad(xl, target)
    loss = lax.psum(loss_tile[0, 0], ("x", "y", "c"))

    small = [None] * n_layers
    dmod_rows = [None] * n_layers
    reduced = [dict() for _ in range(n_layers)]
    in_flight = None
    dx = dy
    for l in reversed(range(n_layers)):
        s = saved[l]
        win, wout, wg, wu, wd = weights[l]
        if in_flight is not None:
            s = dict(s, mod8=s["mod8"] + in_flight[1][0, 0])
        dx1, df, da, db, red_f = _ffn_bwd(dx, s["f"], s["ga"], s["gb"], s["x1"], s["mod8"], s["g8"], wg, wu, wd)
        g_wd = _wgrad_rows(s["act"], df, name="wgrad_down")
        g_wg = _wgrad_cols(s["h2"], da, name="wgrad_gate")
        g_wu = _wgrad_cols(s["h2"], db, name="wgrad_up")
        if in_flight is not None:
            got = _rs_end(in_flight[0], g_wu)
            reduced[l + 1].update(w_in=got[0], w_out=got[1])
        ffn_flight = _rs_begin([g_wg, g_wu, g_wd], g_wu, tag=f"{l}f")
        s = dict(s, mod8=s["mod8"] + ffn_flight[1][0, 0])
        dmix, dattn, dpool, red_c = _mix_bwd(dx1, s["mix"], s["mod8"], s["g8"], wout)
        g_wout = jnp.concatenate([_wgrad(s["attn"], dmix, name="wgrad_out_a"),
                                  _wgrad(s["pool"], dmix, name="wgrad_out_p")], axis=0)
        dq, dk, dv, dsink = _attn_bwd(s["q"], s["k"], s["v"], s["lse"], dattn, s["sink_b"])
        du, g_poolw, dpsc = _pool_bwd(dpool, s["pooled"], pool_w[l], s["psc"])
        dx, dproj, red_d, dbin = _in_bwd(dq, dk, dv, du, rc, rs1, rs2, s["x"], dx1, s["mod8"], s["g8"], win)
        g_win = _wgrad(s["h"], dproj, name="wgrad_in")
        g_win_sh = jnp.transpose(g_win.reshape(D_MODEL, N_SHARD, IN_SH), (1, 0, 2))
        got = _rs_end(ffn_flight[0], g_win)
        reduced[l].update(w_gate=got[0], w_up=got[1], w_down=got[2])
        in_flight = _rs_begin([g_win_sh, g_wout.reshape(N_SHARD, OUT_SH, D_MODEL)], g_win, tag=f"{l}a")
        dmod_rows[l] = jnp.concatenate([red_d[0], red_d[1], red_c[0], red_f[2], red_f[3], red_f[0]])
        small[l] = jnp.concatenate([red_d[2], red_c[1], red_f[4], red_f[1], dbin[0], dpsc[0], dsink[:, 0],
                                    jnp.zeros((120,), F32), g_poolw.reshape(-1)])
    grad_x = dx.reshape(1, T, D_MODEL)

    per_layer = small[0].shape[0]
    rows_small = n_layers * per_layer // 128
    rows_mod = n_layers * 6 * D_MODEL // 128
    rows_pad = -(rows_small + rows_mod) % 8
    pack = jnp.concatenate(small + dmod_rows + [jnp.zeros((rows_pad * 128,), F32)]).reshape(-1, 128)
    pack = pack + in_flight[1][0, 0]
    gathered = _allgather8(pack, name="ag_small").reshape(N_DEV, pack.shape[0], 128)
    summed = _sum_devices(gathered)
    small_sum = summed[:rows_small].reshape(n_layers, per_layer)
    o = 0
    small_g = {}
    for nm, width in (("g_pre_mix", D_MODEL), ("g_post_mix", D_MODEL), ("g_pre_ffn", D_MODEL),
                      ("g_post_ffn", D_MODEL), ("b_in", IN_W), ("pool_scale", POOL_W), ("sinks", 128),
                      ("pool_w", 4 * 128 * 128)):
        small_g[nm] = small_sum[:, o:o + width]
        o += width
    small_g["sinks"] = small_g["sinks"][:, :N_HEADS]
    small_g["pool_w"] = small_g["pool_w"].reshape(n_layers, 4, 128, 128)
    small_g["ada_b"] = summed[rows_small:rows_small + rows_mod].reshape(n_layers, 6 * D_MODEL)
    dmod_all = gathered[:, rows_small:rows_small + rows_mod].reshape(N_DEV, n_layers, N_SHARD, ADA_SH)
    dmod_sh = lax.dynamic_index_in_dim(dmod_all, my_chip, axis=2, keepdims=False)
    g_ada_w = _ada_wgrad(jnp.transpose(c_all), jnp.transpose(dmod_sh, (1, 0, 2)))

    grads = dict(ada_w=g_ada_w, ada_b=small_g["ada_b"], b_in=small_g["b_in"], sinks=small_g["sinks"],
                 pool_w=small_g["pool_w"], pool_scale=small_g["pool_scale"], g_pre_mix=small_g["g_pre_mix"],
                 g_post_mix=small_g["g_post_mix"], g_pre_ffn=small_g["g_pre_ffn"], g_post_ffn=small_g["g_post_ffn"])
    params = dict(ada_w=(ada_w, m_ada_w, v_ada_w), ada_b=(ada_b, m_ada_b, v_ada_b), w_in=(w_in, m_w_in, v_w_in),
                  b_in=(b_in, m_b_in, v_b_in), sinks=(sinks, m_sinks, v_sinks), pool_w=(pool_w, m_pool_w, v_pool_w),
                  pool_scale=(pool_scale, m_pool_scale, v_pool_scale), w_out=(w_out, m_w_out, v_w_out),
                  w_gate=(w_gate, m_w_gate, v_w_gate), w_up=(w_up, m_w_up, v_w_up),
                  w_down=(w_down, m_w_down, v_w_down), g_pre_mix=(g_pre_mix, m_g_pre_mix, v_g_pre_mix),
                  g_post_mix=(g_post_mix, m_g_post_mix, v_g_post_mix), g_pre_ffn=(g_pre_ffn, m_g_pre_ffn, v_g_pre_ffn),
                  g_post_ffn=(g_post_ffn, m_g_post_ffn, v_g_post_ffn))
    names = list(params)
    updates = {nm: _adamw_nd(*params[nm][:1], grads[nm], *params[nm][1:], name="adamw_" + nm) for nm in grads}

    got = _rs_end(in_flight[0], updates["ada_w"][0])
    reduced[0].update(w_in=got[0], w_out=got[1])
    for nm in ("w_in", "w_out", "w_gate", "w_up", "w_down"):
        grads[nm] = jnp.stack([reduced[l][nm] for l in range(n_layers)])
        updates[nm] = _adamw_nd(*params[nm][:1], grads[nm], *params[nm][1:], name="adamw_" + nm)
    return (loss, grad_x, *[grads[nm] for nm in names], *[updates[nm][0] for nm in names],
            *[updates[nm][1] for nm in names], *[updates[nm][2] for nm in names])
```

```python
import functools

import jax
import jax.numpy as jnp
from jax import lax
from jax.experimental import pallas as pl
from jax.experimental.pallas import tpu as pltpu

F32 = jnp.float32
BF16 = jnp.bfloat16
MESH = pl.DeviceIdType.MESH

D_MODEL = 1024
ATTN_W = 512
KV_W = 128
KVD_W = 256
POOL_W = 512
IN_W = 1280
D_FF = 2816
N_SHARD = 4
FF_SH = D_FF // N_SHARD
IN_SH = IN_W // N_SHARD
OUT_SH = D_MODEL // N_SHARD
ADA_SH = 6 * D_MODEL // N_SHARD
HEAD = 64
N_HEADS = 8
GROUP = 4
BLK = 128
POOL_WINDOWS = (2, 4, 8, 16)
HALO = 16
ROT = 16
ROPE_THETA = 500000.0
EPS = 1e-6
NEG_INF = -1e30
N_DEV = 8

ADAM_LR = 0.001
ADAM_B1 = 0.9
ADAM_B2 = 0.999
ADAM_EPS = 1e-08
ADAM_WD = 0.01
ADAM_STEP = 10

VMEM_LIMIT = 48 * 1024 * 1024
WGRAD_TOKENS = 2048


def _cp(*sem):
    return pltpu.CompilerParams(dimension_semantics=sem, vmem_limit_bytes=VMEM_LIMIT)


def _full(shape):
    nd = len(shape)
    return pl.BlockSpec(shape, lambda *_: (0,) * nd)


def _resident(shape):
    nd = len(shape)
    return pl.BlockSpec(shape, lambda *_: (0,) * nd, pipeline_mode=pl.Buffered(1))


def _rows(tm, ncol):
    return pl.BlockSpec((tm, ncol), lambda i: (i, 0))


def _sds(shape, dtype):
    return jax.ShapeDtypeStruct(shape, dtype)


def _nt(a, b):
    return lax.dot_general(a, b, (((1,), (1,)), ((), ())), preferred_element_type=F32)


def _tn(a, b):
    return lax.dot_general(a, b, (((0,), (0,)), ((), ())), preferred_element_type=F32)


def _mm(a, b):
    return jnp.dot(a, b, preferred_element_type=F32)


def _rstd(x):
    return lax.rsqrt(jnp.mean(x * x, axis=-1, keepdims=True) + EPS)


def _colsum(x):
    return jnp.sum(x, axis=0, keepdims=True)


def _norm_bwd(dhat, xhat, rstd):
    return rstd * (dhat - xhat * jnp.mean(dhat * xhat, axis=-1, keepdims=True))


def _rope_tables(pos_b, lane_tab):
    T = pos_b.shape[0]
    tm = min(T, 1024)

    def body(pos_ref, tab_ref, c_ref, s1_ref, s2_ref):
        ang = pos_ref[...].astype(F32) * tab_ref[0:1, :]
        cs = jnp.cos(ang)
        sn = jnp.sin(ang)
        m_rot = tab_ref[1:2, :]
        c_ref[...] = cs * m_rot + (1.0 - m_rot)
        s1_ref[...] = -sn * tab_ref[2:3, :]
        s2_ref[...] = sn * tab_ref[3:4, :]

    out = _sds((T, 128), F32)
    return pl.pallas_call(
        body, name="rope_tables", grid=(T // tm,),
        in_specs=[_rows(tm, 128), _full((8, 128))],
        out_specs=[_rows(tm, 128)] * 3, out_shape=[out] * 3,
        compiler_params=_cp("parallel"),
    )(pos_b, lane_tab)


def _rot_fwd(t, c, s1, s2):
    w = t.shape[-1]
    return t * c + pltpu.roll(t, w - 8, 1) * s1 + pltpu.roll(t, 8, 1) * s2


def _rot_bwd(d, c, s1, s2):
    w = d.shape[-1]
    return d * c + pltpu.roll(d * s1, 8, 1) + pltpu.roll(d * s2, w - 8, 1)


def _store_dup(ref, t):
    low = lax.broadcasted_iota(jnp.int32, t.shape, 1) < HEAD
    sw = pltpu.roll(t, HEAD, 1)
    ref[:, 0:128] = jnp.where(low, t, sw).astype(BF16)
    ref[:, 128:256] = jnp.where(low, sw, t).astype(BF16)


def _fold_dup(d):
    low = lax.broadcasted_iota(jnp.int32, (d.shape[0], 128), 1) < HEAD
    d0 = d[:, 0:128]
    d1 = d[:, 128:256]
    return jnp.where(low, d0 + pltpu.roll(d0, HEAD, 1), d1 + pltpu.roll(d1, HEAD, 1))


def _fwd_in(x, mod8, g8, w_in, b_in, rc, rs1, rs2):
    T = x.shape[0]
    tm = min(T, 512)

    def body(x_ref, mod_ref, g_ref, w_ref, b_ref, c_ref, s1_ref, s2_ref,
             h_ref, q_ref, k_ref, v_ref, u_ref):
        xf = x_ref[...]
        h = (xf * _rstd(xf) * g_ref[0:1, :]) * (1.0 + mod_ref[1:2, :]) + mod_ref[0:1, :]
        hb = h.astype(BF16)
        h_ref[...] = hb
        c = c_ref[...]
        s1 = s1_ref[...]
        s2 = s2_ref[...]
        q = _nt(hb, w_ref[0:ATTN_W, :]) + b_ref[:, 0:ATTN_W]
        q = _rot_fwd(q, jnp.tile(c, (1, 4)), jnp.tile(s1, (1, 4)), jnp.tile(s2, (1, 4)))
        q_ref[...] = (q * (HEAD ** -0.5)).astype(BF16)
        k = _nt(hb, w_ref[ATTN_W:ATTN_W + KV_W, :]) + b_ref[:, ATTN_W:ATTN_W + KV_W]
        _store_dup(k_ref, _rot_fwd(k, c, s1, s2))
        v = _nt(hb, w_ref[ATTN_W + KV_W:ATTN_W + 2 * KV_W, :]) + b_ref[:, ATTN_W + KV_W:ATTN_W + 2 * KV_W]
        _store_dup(v_ref, v)
        u_ref[...] = _nt(hb, w_ref[ATTN_W + 2 * KV_W:IN_W, :]) + b_ref[:, ATTN_W + 2 * KV_W:IN_W]

    return pl.pallas_call(
        body, name="fwd_in", grid=(T // tm,),
        in_specs=[_rows(tm, D_MODEL), _full((8, D_MODEL)), _full((8, D_MODEL)),
                  _resident((IN_W, D_MODEL)), _full((1, IN_W)),
                  _rows(tm, 128), _rows(tm, 128), _rows(tm, 128)],
        out_specs=[_rows(tm, D_MODEL), _rows(tm, ATTN_W), _rows(tm, KVD_W), _rows(tm, KVD_W), _rows(tm, POOL_W)],
        out_shape=[_sds((T, D_MODEL), BF16), _sds((T, ATTN_W), BF16), _sds((T, KVD_W), BF16),
                   _sds((T, KVD_W), BF16), _sds((T, POOL_W), F32)],
        compiler_params=_cp("parallel"),
    )(x, mod8, g8, w_in, b_in, rc, rs1, rs2)


def _band_mask(n):
    kk = lax.broadcasted_iota(jnp.int32, (2 * BLK, BLK), 0)
    qi = lax.broadcasted_iota(jnp.int32, (2 * BLK, BLK), 1)
    first = jnp.where(n > 0, 0, 2 * BLK)
    in_prev = jnp.logical_and(kk < BLK, kk > qi + first)
    in_cur = jnp.logical_and(kk >= BLK, (kk - BLK) <= qi)
    one = jnp.logical_or(in_prev, in_cur)
    return jnp.concatenate([one] * GROUP, axis=1)


def _head_row(ref, j):
    return jnp.concatenate([ref[GROUP * j + r:GROUP * j + r + 1, :] for r in range(GROUP)], axis=1)


def _stack_heads(x_ref, j):
    low = lax.broadcasted_iota(jnp.int32, (BLK, 128), 1) < HEAD
    parts = []
    for gp in (2 * j, 2 * j + 1):
        x2 = x_ref[:, gp * 128:(gp + 1) * 128]
        parts.append(jnp.where(low, x2, jnp.zeros_like(x2)))
        parts.append(jnp.where(low, jnp.zeros_like(x2), x2))
    return jnp.concatenate(parts, axis=0)


def _unstack_heads(o):
    low = lax.broadcasted_iota(jnp.int32, (BLK, 128), 1) < HEAD
    return [jnp.where(low, o[0:BLK], o[BLK:2 * BLK]), jnp.where(low, o[2 * BLK:3 * BLK], o[3 * BLK:4 * BLK])]


def _attn_fwd(q, kd, vd, sink_b):
    T = q.shape[0]
    nb = T // BLK

    def body(q_ref, kp_ref, kc_ref, vp_ref, vc_ref, sk_ref, o_ref, lse_ref):
        valid = _band_mask(pl.program_id(0))
        for j in range(N_HEADS // GROUP):
            lanes = slice(j * 128, (j + 1) * 128)
            kcat = jnp.concatenate([kp_ref[:, lanes], kc_ref[:, lanes]], axis=0)
            vcat = jnp.concatenate([vp_ref[:, lanes], vc_ref[:, lanes]], axis=0)
            s = jnp.where(valid, _nt(kcat, _stack_heads(q_ref, j)), NEG_INF)
            sk = _head_row(sk_ref, j)
            m = jnp.maximum(jnp.max(s, axis=0, keepdims=True), sk)
            p = jnp.exp(s - m)
            den = jnp.sum(p, axis=0, keepdims=True) + jnp.exp(sk - m)
            p = p * (1.0 / den)
            o = _tn(p.astype(BF16), vcat)
            o_ref[:, 2 * j * 128:(2 * j + 2) * 128] = jnp.concatenate(_unstack_heads(o), axis=1).astype(BF16)
            lse = m + jnp.log(den)
            for r in range(GROUP):
                lse_ref[GROUP * j + r:GROUP * j + r + 1, :] = lse[:, r * 128:(r + 1) * 128]

    prev = lambda n: (jnp.maximum(n - 1, 0), 0)
    cur = lambda n: (n, 0)
    return pl.pallas_call(
        body, name="attn_fwd", grid=(nb,),
        in_specs=[pl.BlockSpec((BLK, ATTN_W), cur),
                  pl.BlockSpec((BLK, KVD_W), prev), pl.BlockSpec((BLK, KVD_W), cur),
                  pl.BlockSpec((BLK, KVD_W), prev), pl.BlockSpec((BLK, KVD_W), cur),
                  _full((8, 128))],
        out_specs=[pl.BlockSpec((BLK, ATTN_W), cur), pl.BlockSpec((N_HEADS, 128), cur)],
        out_shape=[_sds((T, ATTN_W), BF16), _sds((nb * N_HEADS, 128), F32)],
        compiler_params=_cp("parallel"),
    )(q, kd, kd, vd, vd, sink_b)


def _pool_fwd(u, pool_w, pool_scale):
    T = u.shape[0]
    tm = min(T, 512)

    def body(u_ref, w_ref, sc_ref, out_ref, pooled_ref, halo):
        i = pl.program_id(0)

        @pl.when(i == 0)
        def _():
            halo[...] = jnp.zeros_like(halo)

        ub = u_ref[...]
        ext = jnp.concatenate([halo[...], ub], axis=0)
        halo[...] = ub[tm - HALO:, :]
        tpos = (i * tm + lax.broadcasted_iota(jnp.int32, (tm, 1), 0)).astype(F32)
        for g, w in enumerate(POOL_WINDOWS):
            lanes = slice(g * 128, (g + 1) * 128)
            s = ext[:, lanes]
            sh = 1
            while sh < w:
                s = s + pltpu.roll(s, sh, 0)
                sh *= 2
            cnt = jnp.minimum(tpos + 1.0, float(w))
            pb = (s[HALO:, :] / cnt - ub[:, lanes]).astype(BF16)
            z = _mm(pb, w_ref[g].astype(BF16))
            out_ref[:, lanes] = (z * sc_ref[:, lanes]).astype(BF16)
            pooled_ref[:, lanes] = pb

    return pl.pallas_call(
        body, name="pool_fwd", grid=(T // tm,),
        in_specs=[_rows(tm, POOL_W), _full((4, 128, 128)), _full((1, POOL_W))],
        out_specs=[_rows(tm, POOL_W), _rows(tm, POOL_W)],
        out_shape=[_sds((T, POOL_W), BF16), _sds((T, POOL_W), BF16)],
        scratch_shapes=[pltpu.VMEM((HALO, POOL_W), F32)],
        compiler_params=_cp("arbitrary"),
    )(u, pool_w, pool_scale)


def _fwd_out(attn, pool, x, w_out, g8, mod8):
    T = x.shape[0]
    tm = min(T, 512)

    def body(a_ref, p_ref, x_ref, w_ref, g_ref, mod_ref, mix_ref, x1_ref):
        mix = _mm(a_ref[...], w_ref[0:ATTN_W, :]) + _mm(p_ref[...], w_ref[ATTN_W:, :])
        mix_ref[...] = mix
        x1_ref[...] = x_ref[...] + mod_ref[2:3, :] * (mix * _rstd(mix) * g_ref[1:2, :])

    return pl.pallas_call(
        body, name="fwd_out", grid=(T // tm,),
        in_specs=[_rows(tm, ATTN_W), _rows(tm, POOL_W), _rows(tm, D_MODEL),
                  _resident((D_MODEL, D_MODEL)), _full((8, D_MODEL)), _full((8, D_MODEL))],
        out_specs=[_rows(tm, D_MODEL), _rows(tm, D_MODEL)],
        out_shape=[_sds((T, D_MODEL), F32), _sds((T, D_MODEL), F32)],
        compiler_params=_cp("parallel"),
    )(attn, pool, x, w_out, g8, mod8)


def _sh_rows(tm):
    return pl.BlockSpec((N_SHARD, tm, FF_SH), lambda i: (0, i, 0))


def _ffn_fwd(x1, mod8, g8, wg, wu, wd):
    T = x1.shape[0]
    tm = min(T, 256)

    def body(x_ref, mod_ref, g_ref, wg_ref, wu_ref, wd_ref, h_ref, act_ref, ga_ref, gb_ref, f_ref, x2_ref):
        xf = x_ref[...]
        h = (xf * _rstd(xf) * g_ref[2:3, :]) * (1.0 + mod_ref[4:5, :]) + mod_ref[3:4, :]
        hb = h.astype(BF16)
        h_ref[...] = hb
        f = jnp.zeros((tm, D_MODEL), F32)
        for s in range(N_SHARD):
            a = _nt(hb, wg_ref[s])
            b = _nt(hb, wu_ref[s])
            sig = jax.nn.sigmoid(a)
            sl = a * sig
            act = (sl * b).astype(BF16)
            act_ref[s] = act
            ga_ref[s] = (b * (sig * (1.0 + a * (1.0 - sig)))).astype(BF16)
            gb_ref[s] = sl.astype(BF16)
            f = f + _mm(act, wd_ref[s])
        f_ref[...] = f
        x2_ref[...] = xf + mod_ref[5:6, :] * (f * _rstd(f) * g_ref[3:4, :])

    act_shape = _sds((N_SHARD, T, FF_SH), BF16)
    return pl.pallas_call(
        body, name="ffn_fwd", grid=(T // tm,),
        in_specs=[_rows(tm, D_MODEL), _full((8, D_MODEL)), _full((8, D_MODEL)),
                  _resident((N_SHARD, FF_SH, D_MODEL)), _resident((N_SHARD, FF_SH, D_MODEL)),
                  _resident((N_SHARD, FF_SH, D_MODEL))],
        out_specs=[_rows(tm, D_MODEL), _sh_rows(tm), _sh_rows(tm), _sh_rows(tm), _rows(tm, D_MODEL),
                   _rows(tm, D_MODEL)],
        out_shape=[_sds((T, D_MODEL), BF16), act_shape, act_shape, act_shape, _sds((T, D_MODEL), F32),
                   _sds((T, D_MODEL), F32)],
        compiler_params=_cp("parallel"),
    )(x1, mod8, g8, wg, wu, wd)


def _loss_grad(y, target):
    T = y.shape[0]
    tm = min(T, 1024)

    def body(y_ref, t_ref, dy_ref, loss_ref):
        @pl.when(pl.program_id(0) == 0)
        def _():
            loss_ref[...] = jnp.zeros_like(loss_ref)

        e = y_ref[...] - t_ref[...]
        dy_ref[...] = e * (1.0 / D_MODEL)
        part = 0.5 * jnp.sum(jnp.mean(e * e, axis=-1, keepdims=True), axis=0, keepdims=True)
        loss_ref[...] += part

    return pl.pallas_call(
        body, name="loss_grad", grid=(T // tm,),
        in_specs=[_rows(tm, D_MODEL), _rows(tm, D_MODEL)],
        out_specs=[_rows(tm, D_MODEL), _full((8, 128))],
        out_shape=[_sds((T, D_MODEL), F32), _sds((8, 128), F32)],
        compiler_params=_cp("arbitrary"),
    )(y, target)


def _ffn_bwd(dx2, f, ga, gb, x1, mod8, g8, wg, wu, wd):
    T = dx2.shape[0]
    tm = min(T, 256)

    def body(dx_ref, f_ref, ga_ref, gb_ref, x_ref, mod_ref, g_ref, wg_ref, wu_ref, wd_ref,
             dx1_ref, df_ref, da_ref, db_ref, red_ref):
        @pl.when(pl.program_id(0) == 0)
        def _():
            red_ref[...] = jnp.zeros_like(red_ref)

        dx = dx_ref[...]
        fv = f_ref[...]
        rstd = _rstd(fv)
        fhat = fv * rstd
        gpost = g_ref[3:4, :]
        red_ref[0:1, :] += _colsum(dx * (fhat * gpost))
        dn = dx * mod_ref[5:6, :]
        red_ref[1:2, :] += _colsum(dn * fhat)
        dfb = _norm_bwd(dn * gpost, fhat, rstd).astype(BF16)
        df_ref[...] = dfb
        dh = jnp.zeros((tm, D_MODEL), F32)
        for s in range(N_SHARD):
            dact = _nt(dfb, wd_ref[s])
            da = (dact * ga_ref[s].astype(F32)).astype(BF16)
            db = (dact * gb_ref[s].astype(F32)).astype(BF16)
            da_ref[s] = da
            db_ref[s] = db
            dh = dh + _mm(da, wg_ref[s]) + _mm(db, wu_ref[s])
        xf = x_ref[...]
        rstd1 = _rstd(xf)
        xhat = xf * rstd1
        gpre = g_ref[2:3, :]
        scale1 = 1.0 + mod_ref[4:5, :]
        red_ref[2:3, :] += _colsum(dh)
        red_ref[3:4, :] += _colsum(dh * (xhat * gpre))
        red_ref[4:5, :] += _colsum(dh * scale1 * xhat)
        dx1_ref[...] = dx + _norm_bwd(dh * scale1 * gpre, xhat, rstd1)

    act_shape = _sds((N_SHARD, T, FF_SH), BF16)
    return pl.pallas_call(
        body, name="ffn_bwd", grid=(T // tm,),
        in_specs=[_rows(tm, D_MODEL), _rows(tm, D_MODEL), _sh_rows(tm), _sh_rows(tm), _rows(tm, D_MODEL),
                  _full((8, D_MODEL)), _full((8, D_MODEL)),
                  _resident((N_SHARD, FF_SH, D_MODEL)), _resident((N_SHARD, FF_SH, D_MODEL)),
                  _resident((N_SHARD, FF_SH, D_MODEL))],
        out_specs=[_rows(tm, D_MODEL), _rows(tm, D_MODEL), _sh_rows(tm), _sh_rows(tm), _full((8, D_MODEL))],
        out_shape=[_sds((T, D_MODEL), F32), _sds((T, D_MODEL), BF16), act_shape, act_shape, _sds((8, D_MODEL), F32)],
        compiler_params=_cp("arbitrary"),
    )(dx2, f, ga, gb, x1, mod8, g8, wg, wu, wd)


def _wgrad(a, b, name):
    T, K = a.shape
    N = b.shape[1]
    tt = min(T, WGRAD_TOKENS)
    tk = next(c for c in (640, 512, 256, 128) if K % c == 0)

    def body(a_ref, b_ref, o_ref):
        @pl.when(pl.program_id(1) == 0)
        def _():
            o_ref[...] = jnp.zeros_like(o_ref)

        o_ref[...] += _tn(a_ref[...], b_ref[...])

    return pl.pallas_call(
        body, name=name, grid=(K // tk, T // tt),
        in_specs=[pl.BlockSpec((tt, tk), lambda i, t: (t, i)), pl.BlockSpec((tt, N), lambda i, t: (t, 0))],
        out_specs=pl.BlockSpec((tk, N), lambda i, t: (i, 0)),
        out_shape=_sds((K, N), F32),
        compiler_params=_cp("parallel", "arbitrary"),
    )(a, b)


def _wgrad_rows(a, b, name):
    T, N = b.shape
    k = a.shape[2]
    tt = min(T, WGRAD_TOKENS)

    def body(a_ref, b_ref, o_ref):
        @pl.when(pl.program_id(1) == 0)
        def _():
            o_ref[...] = jnp.zeros_like(o_ref)

        o_ref[...] += _tn(a_ref[...], b_ref[...])

    return pl.pallas_call(
        body, name=name, grid=(N_SHARD, T // tt),
        in_specs=[pl.BlockSpec((None, tt, k), lambda s, t: (s, t, 0)), pl.BlockSpec((tt, N), lambda s, t: (t, 0))],
        out_specs=pl.BlockSpec((None, k, N), lambda s, t: (s, 0, 0)),
        out_shape=_sds((N_SHARD, k, N), F32),
        compiler_params=_cp("parallel", "arbitrary"),
    )(a, b)


def _mix_bwd(dx1, mix, mod8, g8, w_out):
    T = dx1.shape[0]
    tm = min(T, 512)

    def body(dx_ref, mix_ref, mod_ref, g_ref, w_ref, dmix_ref, da_ref, dp_ref, red_ref):
        @pl.when(pl.program_id(0) == 0)
        def _():
            red_ref[...] = jnp.zeros_like(red_ref)

        dx = dx_ref[...]
        mv = mix_ref[...]
        rstd = _rstd(mv)
        mhat = mv * rstd
        gpost = g_ref[1:2, :]
        red_ref[0:1, :] += _colsum(dx * (mhat * gpost))
        dn = dx * mod_ref[2:3, :]
        red_ref[1:2, :] += _colsum(dn * mhat)
        dmb = _norm_bwd(dn * gpost, mhat, rstd).astype(BF16)
        dmix_ref[...] = dmb
        da_ref[...] = _nt(dmb, w_ref[0:ATTN_W, :]).astype(BF16)
        dp_ref[...] = _nt(dmb, w_ref[ATTN_W:, :]).astype(BF16)

    return pl.pallas_call(
        body, name="mix_bwd", grid=(T // tm,),
        in_specs=[_rows(tm, D_MODEL), _rows(tm, D_MODEL), _full((8, D_MODEL)), _full((8, D_MODEL)),
                  _resident((D_MODEL, D_MODEL))],
        out_specs=[_rows(tm, D_MODEL), _rows(tm, ATTN_W), _rows(tm, POOL_W), _full((8, D_MODEL))],
        out_shape=[_sds((T, D_MODEL), BF16), _sds((T, ATTN_W), BF16), _sds((T, POOL_W), BF16),
                   _sds((8, D_MODEL), F32)],
        compiler_params=_cp("arbitrary"),
    )(dx1, mix, mod8, g8, w_out)


def _attn_bwd(q, kd, vd, lse, dattn, sink_b):
    T = q.shape[0]
    nb = T // BLK

    def body(q_ref, do_ref, lse_ref, kp_ref, kc_ref, vp_ref, vc_ref, sk_ref,
             dq_ref, dk_ref, dv_ref, dsk_ref, carry_k, carry_v):
        n = pl.program_id(0)

        @pl.when(n == 0)
        def _():
            carry_k[...] = jnp.zeros_like(carry_k)
            carry_v[...] = jnp.zeros_like(carry_v)
            dsk_ref[...] = jnp.zeros_like(dsk_ref)

        @pl.when(n < nb)
        def _():
            valid = _band_mask(n)
            for j in range(N_HEADS // GROUP):
                lanes = slice(j * 128, (j + 1) * 128)
                kcat = jnp.concatenate([kp_ref[:, lanes], kc_ref[:, lanes]], axis=0)
                vcat = jnp.concatenate([vp_ref[:, lanes], vc_ref[:, lanes]], axis=0)
                qs = _stack_heads(q_ref, j)
                dos = _stack_heads(do_ref, j)
                lse = _head_row(lse_ref, j)
                p = jnp.exp(jnp.where(valid, _nt(kcat, qs), NEG_INF) - lse)
                dp = _nt(vcat, dos)
                delta = jnp.sum(p * dp, axis=0, keepdims=True)
                ds = (p * (dp - delta)).astype(BF16)
                sink_term = jnp.exp(_head_row(sk_ref, j) - lse) * delta
                for r in range(GROUP):
                    h = GROUP * j + r
                    dsk_ref[h:h + 1, :] += -jnp.sum(sink_term[:, r * 128:(r + 1) * 128], axis=1, keepdims=True)
                dq_ref[:, 2 * j * 128:(2 * j + 2) * 128] = jnp.concatenate(_unstack_heads(_tn(ds, kcat)), axis=1)
                dk = _mm(ds, qs)
                dv = _mm(p.astype(BF16), dos)
                dk_ref[:, lanes] = carry_k[:, lanes] + dk[0:BLK]
                dv_ref[:, lanes] = carry_v[:, lanes] + dv[0:BLK]
                carry_k[:, lanes] = dk[BLK:]
                carry_v[:, lanes] = dv[BLK:]

        @pl.when(n == nb)
        def _():
            dk_ref[...] = carry_k[...]
            dv_ref[...] = carry_v[...]

    cur = lambda n: (jnp.minimum(n, nb - 1), 0)
    prev = lambda n: (jnp.maximum(n - 1, 0), 0)
    return pl.pallas_call(
        body, name="attn_bwd", grid=(nb + 1,),
        in_specs=[pl.BlockSpec((BLK, ATTN_W), cur), pl.BlockSpec((BLK, ATTN_W), cur), pl.BlockSpec((N_HEADS, 128), cur),
                  pl.BlockSpec((BLK, KVD_W), prev), pl.BlockSpec((BLK, KVD_W), cur),
                  pl.BlockSpec((BLK, KVD_W), prev), pl.BlockSpec((BLK, KVD_W), cur),
                  _full((8, 128))],
        out_specs=[pl.BlockSpec((BLK, ATTN_W), cur), pl.BlockSpec((BLK, KVD_W), prev),
                   pl.BlockSpec((BLK, KVD_W), prev), _full((8, 128))],
        out_shape=[_sds((T, ATTN_W), F32), _sds((T, KVD_W), F32), _sds((T, KVD_W), F32), _sds((8, 128), F32)],
        scratch_shapes=[pltpu.VMEM((BLK, KVD_W), F32), pltpu.VMEM((BLK, KVD_W), F32)],
        compiler_params=_cp("arbitrary"),
    )(q, dattn, lse, kd, kd, vd, vd, sink_b)


def _pool_bwd(dpool, pooled, pool_w, pool_scale):
    T = dpool.shape[0]
    tm = min(T, 512)
    nbk = T // tm
    ext_rows = tm + HALO

    def body(dp_ref, pl_ref, w_ref, sc_ref, du_ref, dw_ref, dsc_ref, halo):
        i = pl.program_id(0)

        @pl.when(i == 0)
        def _():
            halo[...] = jnp.zeros_like(halo)
            dw_ref[...] = jnp.zeros_like(dw_ref)
            dsc_ref[...] = jnp.zeros_like(dsc_ref)

        blk = nbk - 1 - i
        tpos = (blk * tm + lax.broadcasted_iota(jnp.int32, (tm, 1), 0)).astype(F32)
        for g, w in enumerate(POOL_WINDOWS):
            lanes = slice(g * 128, (g + 1) * 128)
            dp = dp_ref[:, lanes].astype(F32)
            pb = pl_ref[:, lanes]
            wg = w_ref[g].astype(BF16)
            z = _mm(pb, wg)
            dsc_ref[0:1, lanes] += _colsum(dp * z)
            dz = (dp * sc_ref[:, lanes]).astype(BF16)
            dw_ref[g] += _tn(pb, dz)
            dpl = _nt(dz, wg)
            e = dpl / jnp.minimum(tpos + 1.0, float(w))
            s = jnp.concatenate([e, halo[:, lanes]], axis=0)
            halo[:, lanes] = e[0:HALO, :]
            sh = 1
            while sh < w:
                s = s + pltpu.roll(s, ext_rows - sh, 0)
                sh *= 2
            du_ref[:, lanes] = s[0:tm, :] - dpl

    rev = lambda i: (nbk - 1 - i, 0)
    return pl.pallas_call(
        body, name="pool_bwd", grid=(nbk,),
        in_specs=[pl.BlockSpec((tm, POOL_W), rev), pl.BlockSpec((tm, POOL_W), rev),
                  _full((4, 128, 128)), _full((1, POOL_W))],
        out_specs=[pl.BlockSpec((tm, POOL_W), rev), _full((4, 128, 128)), _full((8, POOL_W))],
        out_shape=[_sds((T, POOL_W), F32), _sds((4, 128, 128), F32), _sds((8, POOL_W), F32)],
        scratch_shapes=[pltpu.VMEM((HALO, POOL_W), F32)],
        compiler_params=_cp("arbitrary"),
    )(dpool, pooled, pool_w, pool_scale)


def _in_bwd(dq, dk, dv, du, rc, rs1, rs2, x, dx1, mod8, g8, w_in):
    T = x.shape[0]
    tm = min(T, 512)

    def body(dq_ref, dk_ref, dv_ref, du_ref, c_ref, s1_ref, s2_ref, x_ref, dx1_ref, mod_ref, g_ref, w_ref,
             dx_ref, dproj_ref, red_ref, dbin_ref):
        @pl.when(pl.program_id(0) == 0)
        def _():
            red_ref[...] = jnp.zeros_like(red_ref)
            dbin_ref[...] = jnp.zeros_like(dbin_ref)

        c = c_ref[...]
        s1 = s1_ref[...]
        s2 = s2_ref[...]
        dqp = _rot_bwd(dq_ref[...] * (HEAD ** -0.5), jnp.tile(c, (1, 4)), jnp.tile(s1, (1, 4)), jnp.tile(s2, (1, 4)))
        dkp = _rot_bwd(_fold_dup(dk_ref[...]), c, s1, s2)
        pieces = ((0, ATTN_W, dqp), (ATTN_W, ATTN_W + KV_W, dkp),
                  (ATTN_W + KV_W, ATTN_W + 2 * KV_W, _fold_dup(dv_ref[...])), (ATTN_W + 2 * KV_W, IN_W, du_ref[...]))
        dh = jnp.zeros((tm, D_MODEL), F32)
        for lo, hi, val in pieces:
            dbin_ref[0:1, lo:hi] += _colsum(val)
            vb = val.astype(BF16)
            dproj_ref[:, lo:hi] = vb
            dh = dh + _mm(vb, w_ref[lo:hi, :])
        xf = x_ref[...]
        rstd = _rstd(xf)
        xhat = xf * rstd
        gpre = g_ref[0:1, :]
        scale1 = 1.0 + mod_ref[1:2, :]
        red_ref[0:1, :] += _colsum(dh)
        red_ref[1:2, :] += _colsum(dh * (xhat * gpre))
        red_ref[2:3, :] += _colsum(dh * scale1 * xhat)
        dx_ref[...] = dx1_ref[...] + _norm_bwd(dh * scale1 * gpre, xhat, rstd)

    return pl.pallas_call(
        body, name="in_bwd", grid=(T // tm,),
        in_specs=[_rows(tm, ATTN_W), _rows(tm, KVD_W), _rows(tm, KVD_W), _rows(tm, POOL_W),
                  _rows(tm, 128), _rows(tm, 128), _rows(tm, 128), _rows(tm, D_MODEL), _rows(tm, D_MODEL),
                  _full((8, D_MODEL)), _full((8, D_MODEL)), _resident((IN_W, D_MODEL))],
        out_specs=[_rows(tm, D_MODEL), _rows(tm, IN_W), _full((8, D_MODEL)), _full((8, IN_W))],
        out_shape=[_sds((T, D_MODEL), F32), _sds((T, IN_W), BF16), _sds((8, D_MODEL), F32), _sds((8, IN_W), F32)],
        compiler_params=_cp("arbitrary"),
    )(dq, dk, dv, du, rc, rs1, rs2, x, dx1, mod8, g8, w_in)


def _mod_fwd(c_all, ada_w, ada_b_sh):
    tn = 512

    def body(c_ref, w_ref, b_ref, o_ref):
        cv = c_ref[...]
        ca = (cv * jax.nn.sigmoid(cv)).astype(BF16)
        o_ref[...] = _mm(ca, w_ref[...].astype(BF16)) + b_ref[...]

    return pl.pallas_call(
        body, name="mod_fwd", grid=(2, ADA_SH // tn),
        in_specs=[_full((8, D_MODEL)), pl.BlockSpec((None, D_MODEL, tn), lambda l, j: (l, 0, j)),
                  pl.BlockSpec((None, 1, tn), lambda l, j: (l, 0, j))],
        out_specs=pl.BlockSpec((None, 8, tn), lambda l, j: (l, 0, j)),
        out_shape=_sds((2, 8, ADA_SH), F32),
        compiler_params=_cp("parallel", "parallel"),
    )(c_all, ada_w, ada_b_sh)


def _ada_wgrad(c_all_t, dmod_sh):
    tn = 512

    def body(c_ref, d_ref, o_ref):
        cv = c_ref[...]
        ca = cv * jax.nn.sigmoid(cv)
        o_ref[...] = jnp.dot(ca, d_ref[...], preferred_element_type=F32, precision=lax.Precision.HIGHEST)

    return pl.pallas_call(
        body, name="ada_wgrad", grid=(2, ADA_SH // tn),
        in_specs=[_full((D_MODEL, 8)), pl.BlockSpec((None, 8, tn), lambda l, j: (l, 0, j))],
        out_specs=pl.BlockSpec((None, D_MODEL, tn), lambda l, j: (l, 0, j)),
        out_shape=_sds((2, D_MODEL, ADA_SH), F32),
        compiler_params=_cp("parallel", "parallel"),
    )(c_all_t, dmod_sh)


def _sum_devices(g):
    R = g.shape[1]

    def body(g_ref, o_ref):
        acc = g_ref[0]
        for d in range(1, N_DEV):
            acc = acc + g_ref[d]
        o_ref[...] = acc

    return pl.pallas_call(
        body, name="sum_devices", grid=(1,),
        in_specs=[_full((N_DEV, R, 128))], out_specs=_full((R, 128)), out_shape=_sds((R, 128), F32),
        compiler_params=_cp("arbitrary"),
    )(g)


def _adamw(w, g, m, v, name):
    R, C = w.shape
    tr = R
    for cand in (256, 128, 64, 32, 16, 8):
        if R % cand == 0 and cand * C * 4 <= 2 * 1024 * 1024:
            tr = cand
            break

    def body(w_ref, g_ref, m_ref, v_ref, d_ref, nm_ref, nv_ref):
        gv = g_ref[...]
        mn = ADAM_B1 * m_ref[...] + (1.0 - ADAM_B1) * gv
        vn = ADAM_B2 * v_ref[...] + (1.0 - ADAM_B2) * (gv * gv)
        m_hat = mn / (1.0 - ADAM_B1 ** ADAM_STEP)
        v_hat = vn / (1.0 - ADAM_B2 ** ADAM_STEP)
        d_ref[...] = -ADAM_LR * (m_hat / (jnp.sqrt(v_hat) + ADAM_EPS) + ADAM_WD * w_ref[...])
        nm_ref[...] = mn
        nv_ref[...] = vn

    spec = pl.BlockSpec((tr, C), lambda i: (i, 0))
    out = _sds((R, C), F32)
    return pl.pallas_call(
        body, name=name, grid=(R // tr,),
        in_specs=[spec] * 4, out_specs=[spec] * 3, out_shape=[out] * 3,
        compiler_params=_cp("parallel"),
    )(w, g, m, v)


def _adamw_nd(w, g, m, v, name):
    shape = w.shape
    if w.ndim == 2 and shape[1] < 128:
        view = (1, shape[0] * shape[1])
    else:
        view = (-1, shape[-1])
    outs = _adamw(*[t.reshape(view) for t in (w, g, m, v)], name=name)
    return [o.reshape(shape) for o in outs]


def _coords():
    return lax.axis_index("x"), lax.axis_index("y"), lax.axis_index("c")


def _other_chips(x, y):
    return [(1 - x, y), (x, 1 - y), (1 - x, 1 - y)]


def _allgather8(blk, name):
    m_per, n = blk.shape

    def body(x_ref, out_ref, send_sems, recv_sems, local_sem):
        x, y, c = _coords()
        me, sibling = (x, y, c), (x, y, 1 - c)
        chips = _other_chips(x, y)

        def rows(px, py, pc):
            return out_ref.at[pl.ds((4 * px + 2 * py + pc) * m_per, m_per), :]

        def copy(k, block, to, src=None):
            return pltpu.make_async_remote_copy(
                src_ref=rows(*block) if src is None else src, dst_ref=rows(*block),
                send_sem=send_sems.at[k], recv_sem=recv_sems.at[k], device_id=to, device_id_type=MESH)

        mine = pltpu.make_async_copy(x_ref, rows(*me), local_sem)
        mine.start()
        first = [copy(0, me, sibling, src=x_ref)]
        first += [copy(1 + j, me, (*chip, c), src=x_ref) for j, chip in enumerate(chips)]
        for cp in first:
            cp.start()
        passed = [copy(4 + j, (*chip, c), sibling) for j, chip in enumerate(chips)]
        for j, chip in enumerate(chips):
            copy(1 + j, (*chip, c), me).wait_recv()
            passed[j].start()
        copy(0, sibling, me).wait_recv()
        for j, chip in enumerate(chips):
            copy(4 + j, (*chip, 1 - c), me).wait_recv()
        for cp in first + passed:
            cp.wait_send()
        mine.wait()

    return pl.pallas_call(
        body, name=name,
        out_shape=_sds((N_DEV * m_per, n), blk.dtype),
        in_specs=[pl.BlockSpec(memory_space=pltpu.VMEM)],
        out_specs=pl.BlockSpec(memory_space=pltpu.VMEM),
        scratch_shapes=[pltpu.SemaphoreType.DMA((7,)), pltpu.SemaphoreType.DMA((7,)), pltpu.SemaphoreType.DMA],
        compiler_params=pltpu.CompilerParams(vmem_limit_bytes=VMEM_LIMIT),
    )(blk)


def _row_tile(r, n):
    for cand in (512, 256, 128, 64, 32, 16):
        if r % cand == 0 and cand * n * 4 <= 2 * 1024 * 1024:
            return cand
    return r


def _cast_slot(w, chip, name):
    r, n = w.shape
    tr = _row_tile(r, n)

    def body(chip_ref, w_ref, o_ref):
        o_ref[...] = w_ref[...].astype(BF16)

    grid_spec = pltpu.PrefetchScalarGridSpec(
        num_scalar_prefetch=1, grid=(r // tr,),
        in_specs=[pl.BlockSpec((tr, n), lambda i, ch: (i, 0))],
        out_specs=pl.BlockSpec((None, tr, n), lambda i, ch: (ch[0], i, 0)))
    return pl.pallas_call(
        body, name=name, grid_spec=grid_spec, out_shape=_sds((N_SHARD, r, n), BF16),
        compiler_params=_cp("arbitrary"),
    )(chip, w)


def _allgather_weights(bufs, name):
    nt = len(bufs)
    hom = [pl.BlockSpec(memory_space=pl.ANY)] * nt

    def body(*refs):
        outs = refs[nt:2 * nt]
        send_sems, recv_sems = refs[2 * nt:]
        x, y, c = _coords()
        sibling = (x, y, 1 - c)
        chips = _other_chips(x, y)

        def copy(t, k, block_chip, hc, to):
            r = outs[t].shape[1] // 2
            blk = outs[t].at[2 * block_chip[0] + block_chip[1], pl.ds(hc * r, r)]
            return pltpu.make_async_remote_copy(
                src_ref=blk, dst_ref=blk,
                send_sem=send_sems.at[t, k], recv_sem=recv_sems.at[t, k], device_id=to, device_id_type=MESH)

        started = []
        for t in range(nt):
            for j, chip in enumerate(chips):
                cp = copy(t, j, (x, y), c, (*chip, c))
                cp.start()
                started.append(cp)
        for t in range(nt):
            for j, chip in enumerate(chips):
                copy(t, j, chip, c, sibling).wait_recv()
                fw = copy(t, 3 + j, chip, c, sibling)
                fw.start()
                started.append(fw)
        for t in range(nt):
            for j, chip in enumerate(chips):
                copy(t, 3 + j, chip, 1 - c, sibling).wait_recv()
        for cp in started:
            cp.wait_send()

    return pl.pallas_call(
        body, name=name,
        out_shape=[_sds(b.shape, b.dtype) for b in bufs],
        in_specs=hom, out_specs=hom,
        input_output_aliases={t: t for t in range(nt)},
        scratch_shapes=[pltpu.SemaphoreType.DMA((nt, 6)), pltpu.SemaphoreType.DMA((nt, 6))],
    )(*bufs)


def _swap_halves(grads, name):
    nt = len(grads)
    hom = [pl.BlockSpec(memory_space=pl.ANY)] * nt

    def body(*refs):
        ins = refs[:nt]
        outs = refs[nt:2 * nt]
        send_sems, recv_sems = refs[2 * nt:]
        x, y, c = _coords()
        sibling = (x, y, 1 - c)
        cps = []
        for t in range(nt):
            r = ins[t].shape[1] // 2
            cp = pltpu.make_async_remote_copy(
                src_ref=ins[t].at[:, pl.ds((1 - c) * r, r)], dst_ref=outs[t],
                send_sem=send_sems.at[t], recv_sem=recv_sems.at[t], device_id=sibling, device_id_type=MESH)
            cp.start()
            cps.append(cp)
        for cp in cps:
            cp.wait()

    return pl.pallas_call(
        body, name=name,
        out_shape=[_sds((N_SHARD, g.shape[1] // 2, g.shape[2]), g.dtype) for g in grads],
        in_specs=hom, out_specs=hom,
        scratch_shapes=[pltpu.SemaphoreType.DMA((nt,)), pltpu.SemaphoreType.DMA((nt,))],
    )(*grads)


def _scatter_chips(sums, name):
    nt = len(sums)
    hom = [pl.BlockSpec(memory_space=pl.ANY)] * nt

    def body(*refs):
        ins = refs[:nt]
        outs = refs[nt:2 * nt]
        send_sems, recv_sems = refs[2 * nt:]
        x, y, c = _coords()
        chips = _other_chips(x, y)
        cps = []
        for t in range(nt):
            for j, chip in enumerate(chips):
                cp = pltpu.make_async_remote_copy(
                    src_ref=ins[t].at[2 * chip[0] + chip[1]], dst_ref=outs[t].at[j],
                    send_sem=send_sems.at[t, j], recv_sem=recv_sems.at[t, j],
                    device_id=(*chip, c), device_id_type=MESH)
                cp.start()
                cps.append(cp)
        for cp in cps:
            cp.wait()

    return pl.pallas_call(
        body, name=name,
        out_shape=[_sds((3,) + s.shape[1:], s.dtype) for s in sums],
        in_specs=hom, out_specs=hom,
        scratch_shapes=[pltpu.SemaphoreType.DMA((nt, 3)), pltpu.SemaphoreType.DMA((nt, 3))],
    )(*sums)


def _join_halves(tots, name):
    nt = len(tots)
    hom = [pl.BlockSpec(memory_space=pl.ANY)] * nt

    def body(*refs):
        outs = refs[nt:2 * nt]
        send_sems, recv_sems = refs[2 * nt:]
        x, y, c = _coords()
        sibling = (x, y, 1 - c)
        cps = []
        for t in range(nt):
            cp = pltpu.make_async_remote_copy(
                src_ref=outs[t].at[c], dst_ref=outs[t].at[c],
                send_sem=send_sems.at[t], recv_sem=recv_sems.at[t], device_id=sibling, device_id_type=MESH)
            cp.start()
            cps.append(cp)
        for t in range(nt):
            pltpu.make_async_remote_copy(
                src_ref=outs[t].at[c], dst_ref=outs[t].at[1 - c],
                send_sem=send_sems.at[t], recv_sem=recv_sems.at[t], device_id=sibling, device_id_type=MESH).wait_recv()
        for cp in cps:
            cp.wait_send()

    return pl.pallas_call(
        body, name=name,
        out_shape=[_sds(t.shape, t.dtype) for t in tots],
        in_specs=hom, out_specs=hom,
        input_output_aliases={t: t for t in range(nt)},
        scratch_shapes=[pltpu.SemaphoreType.DMA((nt,)), pltpu.SemaphoreType.DMA((nt,))],
    )(*tots)


def _pair_sum(g, recv, core, chip, name):
    _, _, r, n = g.shape
    tr = _row_tile(r, n)

    def body(core_ref, chip_ref, g_ref, r_ref, sb_ref, own_ref):
        tot = g_ref[...] + r_ref[...]
        sb_ref[...] = tot.astype(BF16)

        @pl.when(pl.program_id(1) == chip_ref[0])
        def _():
            own_ref[...] = tot

    grid_spec = pltpu.PrefetchScalarGridSpec(
        num_scalar_prefetch=2, grid=(r // tr, N_SHARD),
        in_specs=[pl.BlockSpec((None, None, tr, n), lambda i, s, co, ch: (s, co[0], i, 0)),
                  pl.BlockSpec((None, tr, n), lambda i, s, co, ch: (s, i, 0))],
        out_specs=[pl.BlockSpec((None, tr, n), lambda i, s, co, ch: (s, i, 0)),
                   pl.BlockSpec((tr, n), lambda i, s, co, ch: (i, 0))])
    return pl.pallas_call(
        body, name=name, grid_spec=grid_spec,
        out_shape=[_sds((N_SHARD, r, n), BF16), _sds((r, n), F32)],
        compiler_params=_cp("arbitrary", "arbitrary"),
    )(core, chip, g, recv)


def _chip_sum(own, recv, core, name):
    r, n = own.shape
    tr = _row_tile(r, n)

    def body(core_ref, o_ref, r_ref, t_ref):
        acc = o_ref[...]
        for j in range(3):
            acc = acc + r_ref[j].astype(F32)
        t_ref[...] = acc

    grid_spec = pltpu.PrefetchScalarGridSpec(
        num_scalar_prefetch=1, grid=(r // tr,),
        in_specs=[pl.BlockSpec((tr, n), lambda i, co: (i, 0)), pl.BlockSpec((3, tr, n), lambda i, co: (0, i, 0))],
        out_specs=pl.BlockSpec((None, tr, n), lambda i, co: (co[0], i, 0)))
    return pl.pallas_call(
        body, name=name, grid_spec=grid_spec, out_shape=_sds((2, r, n), F32),
        compiler_params=_cp("arbitrary"),
    )(core, own, recv)


_HBM = pl.BlockSpec(memory_space=pltpu.HBM)
_SEM = pl.BlockSpec(memory_space=pltpu.SEMAPHORE)
_EFFECT = pltpu.SideEffectType.DATAFLOW_SIDE_EFFECTING


def _ici_copies(srcs, dsts, send_sems, recv_sems, send_view, recv_view):
    x, y, c = _coords()
    out = []
    for t in range(len(srcs)):
        for j, chip in enumerate(_other_chips(x, y)):
            out.append(pltpu.make_async_remote_copy(
                src_ref=send_view(srcs[t], chip, j, (x, y), c), dst_ref=recv_view(dsts[t], chip, j, (x, y), c),
                send_sem=send_sems.at[3 * t + j], recv_sem=recv_sems.at[3 * t + j],
                device_id=(*chip, c), device_id_type=MESH))
    return out


def _ici_start(srcs, dsts, after, send_view, recv_view, name):
    nt = len(srcs)
    inplace = dsts is None
    nbuf = nt if inplace else 2 * nt

    def body(*refs):
        send_sems, recv_sems = refs[nbuf + 1], refs[nbuf + 2]
        s_out = refs[nbuf + 3:nbuf + 3 + nt]
        d_out = s_out if inplace else refs[nbuf + 3 + nt:nbuf + 3 + 2 * nt]
        token = refs[-1]
        for cp in _ici_copies(s_out, d_out, send_sems, recv_sems, send_view, recv_view):
            cp.start()
        token[...] = jnp.zeros_like(token)

    bufs = list(srcs) + ([] if inplace else list(dsts))
    res = pl.pallas_call(
        body, name=name,
        out_shape=(pltpu.SemaphoreType.DMA((3 * nt,)), pltpu.SemaphoreType.DMA((3 * nt,)),
                   *[pltpu.HBM(b.shape, b.dtype) for b in bufs], _sds((8, 128), F32)),
        in_specs=[_HBM] * nbuf + [pl.BlockSpec(memory_space=pl.ANY)],
        out_specs=(_SEM, _SEM, *[_HBM] * nbuf, pl.BlockSpec(memory_space=pltpu.VMEM)),
        input_output_aliases={i: 2 + i for i in range(nbuf)},
        compiler_params=pltpu.CompilerParams(has_side_effects=_EFFECT),
    )(*[pltpu.with_memory_space_constraint(b, pltpu.HBM) for b in bufs], after)
    send_sems, recv_sems = res[0], res[1]
    s_thru = list(res[2:2 + nt])
    d_thru = s_thru if inplace else list(res[2 + nt:2 + 2 * nt])
    return send_sems, recv_sems, s_thru, d_thru, res[-1]


def _ici_wait(send_sems, recv_sems, srcs, dsts, after, send_view, recv_view, name):
    nt = len(srcs)
    inplace = dsts is None
    nbuf = nt if inplace else 2 * nt

    def body(*refs):
        send_ref, recv_ref = refs[nbuf], refs[nbuf + 1]
        s_out = refs[nbuf + 3:nbuf + 3 + nt]
        d_out = s_out if inplace else refs[nbuf + 3 + nt:nbuf + 3 + 2 * nt]
        for cp in _ici_copies(s_out, d_out, send_ref, recv_ref, send_view, recv_view):
            cp.wait_send()
            cp.wait_recv()

    bufs = list(srcs) + ([] if inplace else list(dsts))
    res = pl.pallas_call(
        body, name=name,
        out_shape=tuple(pltpu.HBM(b.shape, b.dtype) for b in bufs),
        in_specs=[_HBM] * nbuf + [_SEM, _SEM, pl.BlockSpec(memory_space=pl.ANY)],
        out_specs=tuple([_HBM] * nbuf),
        input_output_aliases={i: i for i in range(nbuf)},
        compiler_params=pltpu.CompilerParams(has_side_effects=_EFFECT),
    )(*bufs, send_sems, recv_sems, after)
    return list(res[:nt]) if inplace else list(res[nt:])


def _w_half(buf, chip, c):
    r = buf.shape[1] // 2
    return buf.at[2 * chip[0] + chip[1], pl.ds(c * r, r)]


def _ag_send_view(buf, chip, j, me, c):
    return _w_half(buf, me, c)


def _ag_recv_view(buf, chip, j, me, c):
    return _w_half(buf, me, c)


def _rs_send_view(buf, chip, j, me, c):
    return buf.at[2 * chip[0] + chip[1]]


def _rs_recv_view(buf, chip, j, me, c):
    return buf.at[j]


def _ag_forward(bufs, name):
    nt = len(bufs)
    hom = [pl.BlockSpec(memory_space=pl.ANY)] * nt

    def body(*refs):
        outs = refs[nt:2 * nt]
        send_sems, recv_sems = refs[2 * nt:]
        x, y, c = _coords()
        sibling = (x, y, 1 - c)
        chips = _other_chips(x, y)

        def copy(t, j, hc):
            blk = _w_half(outs[t], chips[j], hc)
            return pltpu.make_async_remote_copy(
                src_ref=blk, dst_ref=blk, send_sem=send_sems.at[t, j], recv_sem=recv_sems.at[t, j],
                device_id=sibling, device_id_type=MESH)

        started = [copy(t, j, c) for t in range(nt) for j in range(3)]
        for cp in started:
            cp.start()
        for t in range(nt):
            for j in range(3):
                copy(t, j, 1 - c).wait_recv()
        for cp in started:
            cp.wait_send()

    return pl.pallas_call(
        body, name=name,
        out_shape=[_sds(b.shape, b.dtype) for b in bufs],
        in_specs=hom, out_specs=hom,
        input_output_aliases={t: t for t in range(nt)},
        scratch_shapes=[pltpu.SemaphoreType.DMA((nt, 3)), pltpu.SemaphoreType.DMA((nt, 3))],
    )(*bufs)


def _rs_begin(grads, after, tag):
    x, y, c = _coords()
    core = jnp.reshape(c, (1,)).astype(jnp.int32)
    chip = jnp.reshape(2 * x + y, (1,)).astype(jnp.int32)
    recv = _swap_halves(grads, name="rs_swap_" + tag)
    sums, owns = [], []
    for t, (g, rv) in enumerate(zip(grads, recv)):
        r = g.shape[1] // 2
        sb, own = _pair_sum(g.reshape(N_SHARD, 2, r, g.shape[2]), rv, core, chip, name=f"rs_pair_{tag}_{t}")
        sums.append(sb)
        owns.append(own)
    land = [lax.empty((3,) + s.shape[1:], s.dtype) for s in sums]
    send_sems, recv_sems, s_thru, d_thru, token = _ici_start(
        sums, land, after, _rs_send_view, _rs_recv_view, name="rs_start_" + tag)
    return dict(sems=(send_sems, recv_sems), sums=s_thru, land=d_thru, owns=owns, core=core, tag=tag), token


def _rs_end(state, after):
    tag = state["tag"]
    got = _ici_wait(*state["sems"], state["sums"], state["land"], after, _rs_send_view, _rs_recv_view,
                    name="rs_wait_" + tag)
    tots = [_chip_sum(o, gt, state["core"], name=f"rs_chip_{tag}_{t}")
            for t, (o, gt) in enumerate(zip(state["owns"], got))]
    full = _join_halves(tots, name="rs_join_" + tag)
    return [f.reshape(2 * f.shape[1], f.shape[2]) for f in full]


def _rope_lane_table():
    d = jnp.arange(128) % HEAD
    inv_freq = ROPE_THETA ** (-jnp.arange(0, ROT, 2, dtype=F32) / ROT)
    rot = d < ROT
    rows = [jnp.where(rot, inv_freq[d % (ROT // 2)], 0.0), rot.astype(F32),
            (d < ROT // 2).astype(F32), jnp.logical_and(d >= ROT // 2, rot).astype(F32)]
    return jnp.concatenate([jnp.stack(rows), jnp.zeros((4, 128), F32)], axis=0)


def _pad8(rows):
    return jnp.concatenate([rows, jnp.zeros((8 - rows.shape[0], rows.shape[1]), F32)], axis=0)


def kernel(x, c, positions, ada_w, ada_b, w_in, b_in, sinks, pool_w, pool_scale, w_out, w_gate, w_up, w_down, g_pre_mix, g_post_mix, g_pre_ffn, g_post_ffn, loss_target, m_ada_w, m_ada_b, m_w_in, m_b_in, m_sinks, m_pool_w, m_pool_scale, m_w_out, m_w_gate, m_w_up, m_w_down, m_g_pre_mix, m_g_post_mix, m_g_pre_ffn, m_g_post_ffn, v_ada_w, v_ada_b, v_w_in, v_b_in, v_sinks, v_pool_w, v_pool_scale, v_w_out, v_w_gate, v_w_up, v_w_down, v_g_pre_mix, v_g_post_mix, v_g_pre_ffn, v_g_post_ffn):
    T = x.shape[1]
    n_layers = ada_w.shape[0]
    ax, ay, ac = _coords()
    my_dev = 4 * ax + 2 * ay + ac
    my_chip = 2 * ax + ay
    x0 = x.reshape(T, D_MODEL)
    target = loss_target.reshape(T, D_MODEL)

    c_all = _allgather8(c.reshape(8, 128), name="ag_c").reshape(N_DEV, D_MODEL)
    ada_b_sh = lax.dynamic_slice_in_dim(ada_b, my_chip * ADA_SH, ADA_SH, axis=1).reshape(n_layers, 1, ADA_SH)
    mod_part = _mod_fwd(c_all, ada_w, ada_b_sh)
    mod_all = _allgather8(mod_part.reshape(n_layers * 8, ADA_SH), name="ag_mod")
    mod_all = mod_all.reshape(N_DEV, n_layers, 8, ADA_SH)[0::2]
    mod_mine = lax.dynamic_index_in_dim(mod_all, my_dev, axis=2, keepdims=False)
    mod = jnp.transpose(mod_mine, (1, 0, 2)).reshape(n_layers, 6, D_MODEL)

    pos_b = jnp.broadcast_to(positions.reshape(T, 1), (T, 128))
    rc, rs1, rs2 = _rope_tables(pos_b, _rope_lane_table())

    chip1 = jnp.reshape(my_chip, (1,)).astype(jnp.int32)

    def tr(t):
        return jnp.transpose(t, (0, 2, 1))

    w_in_t, w_gate_t, w_up_t = tr(w_in), tr(w_gate), tr(w_up)

    def cast_layer(l):
        return [_cast_slot(w[l], chip1, name=f"cast_{nm}{l}")
                for nm, w in (("w_in", w_in_t), ("w_out", w_out), ("w_gate", w_gate_t), ("w_up", w_up_t),
                              ("w_down", w_down))]

    def as_operands(bufs):
        gin, gout, gg, gu, gd = bufs
        return gin.reshape(IN_W, D_MODEL), gout.reshape(D_MODEL, D_MODEL), gg, gu, gd

    bufs0 = cast_layer(0)
    win0 = _allgather_weights(bufs0[:1], name="ag_w0_in")
    rest_send, rest_recv, rest_bufs, _, ag_token = _ici_start(
        bufs0[1:], None, win0[0], _ag_send_view, _ag_recv_view, name="ag_start_0")
    weights = [None] * n_layers

    saved = []
    xl = x0
    for l in range(n_layers):
        mod8 = _pad8(mod[l])
        if l + 1 < n_layers:
            ag_send, ag_recv, ag_bufs, _, ag_token = _ici_start(
                cast_layer(l + 1), None, ag_token, _ag_send_view, _ag_recv_view, name=f"ag_start_{l + 1}")
        if l == 0 or l + 1 < n_layers:
            mod8 = mod8 + ag_token[0, 0]
        g8 = _pad8(jnp.stack([g_pre_mix[l], g_post_mix[l], g_pre_ffn[l], g_post_ffn[l]]))
        sink_b = jnp.broadcast_to(sinks[l][:, None], (N_HEADS, 128))
        psc = pool_scale[l].reshape(1, POOL_W)
        win = win0[0].reshape(IN_W, D_MODEL) if l == 0 else weights[l][0]
        h, q, k, v, u = _fwd_in(xl, mod8, g8, win, b_in[l].reshape(1, IN_W), rc, rs1, rs2)
        attn, lse = _attn_fwd(q, k, v, sink_b)
        pool, pooled = _pool_fwd(u, pool_w[l], psc)
        if l == 0:
            arrived = _ici_wait(rest_send, rest_recv, rest_bufs, None, pool, _ag_send_view, _ag_recv_view,
                                name="ag_wait_0")
            weights[0] = as_operands(win0 + _ag_forward(arrived, name="ag_fwd_0"))
        win, wout, wg, wu, wd = weights[l]
        mix, x1 = _fwd_out(attn, pool, xl, wout, g8, mod8)
        h2, act, ga, gb, f, x2 = _ffn_fwd(x1, mod8, g8, wg, wu, wd)
        saved.append(dict(x=xl, h=h, q=q, k=k, v=v, lse=lse, attn=attn, pool=pool, pooled=pooled, mix=mix,
                          x1=x1, h2=h2, act=act, ga=ga, gb=gb, f=f, mod8=mod8, g8=g8, sink_b=sink_b, psc=psc))
        xl = x2
        if l + 1 < n_layers:
            arrived = _ici_wait(ag_send, ag_recv, ag_bufs, None, x2, _ag_send_view, _ag_recv_view,
                                name=f"ag_wait_{l + 1}")
            weights[l + 1] = as_operands(_ag_forward(arrived, name=f"ag_fwd_{l + 1}"))

    dy, loss_tile = _loss_grad(xl, target)
    loss = lax.psum(loss_tile[0, 0], ("x", "y", "c"))

    small = [None] * n_layers
    dmod_rows = [None] * n_layers
    reduced = [dict() for _ in range(n_layers)]
    in_flight = None
    dx = dy
    for l in reversed(range(n_layers)):
        s = saved[l]
        win, wout, wg, wu, wd = weights[l]
        if in_flight is not None:
            s = dict(s, mod8=s["mod8"] + in_flight[1][0, 0])
        dx1, df, da, db, red_f = _ffn_bwd(dx, s["f"], s["ga"], s["gb"], s["x1"], s["mod8"], s["g8"], wg, wu, wd)
        g_wd = _wgrad_rows(s["act"], df, name="wgrad_down")
        g_wg = _wgrad_rows(da, s["h2"], name="wgrad_gate")
        g_wu = _wgrad_rows(db, s["h2"], name="wgrad_up")
        if in_flight is not None:
            got = _rs_end(in_flight[0], g_wu)
            reduced[l + 1].update(w_in=got[0], w_out=got[1])
        ffn_flight = _rs_begin([g_wg, g_wu, g_wd], g_wu, tag=f"{l}f")
        s = dict(s, mod8=s["mod8"] + ffn_flight[1][0, 0])
        dmix, dattn, dpool, red_c = _mix_bwd(dx1, s["mix"], s["mod8"], s["g8"], wout)
        g_wout = jnp.concatenate([_wgrad(s["attn"], dmix, name="wgrad_out_a"),
                                  _wgrad(s["pool"], dmix, name="wgrad_out_p")], axis=0)
        dq, dk, dv, dsink = _attn_bwd(s["q"], s["k"], s["v"], s["lse"], dattn, s["sink_b"])
        du, g_poolw, dpsc = _pool_bwd(dpool, s["pooled"], pool_w[l], s["psc"])
        dx, dproj, red_d, dbin = _in_bwd(dq, dk, dv, du, rc, rs1, rs2, s["x"], dx1, s["mod8"], s["g8"], win)
        g_win = _wgrad(dproj, s["h"], name="wgrad_in")
        g_win_sh = g_win.reshape(N_SHARD, IN_SH, D_MODEL)
        got = _rs_end(ffn_flight[0], g_win)
        reduced[l].update(w_gate=got[0], w_up=got[1], w_down=got[2])
        in_flight = _rs_begin([g_win_sh, g_wout.reshape(N_SHARD, OUT_SH, D_MODEL)], g_win, tag=f"{l}a")
        dmod_rows[l] = jnp.concatenate([red_d[0], red_d[1], red_c[0], red_f[2], red_f[3], red_f[0]])
        small[l] = jnp.concatenate([red_d[2], red_c[1], red_f[4], red_f[1], dbin[0], dpsc[0], dsink[:, 0],
                                    jnp.zeros((120,), F32), g_poolw.reshape(-1)])
    grad_x = dx.reshape(1, T, D_MODEL)

    per_layer = small[0].shape[0]
    rows_small = n_layers * per_layer // 128
    rows_mod = n_layers * 6 * D_MODEL // 128
    rows_pad = -(rows_small + rows_mod) % 8
    pack = jnp.concatenate(small + dmod_rows + [jnp.zeros((rows_pad * 128,), F32)]).reshape(-1, 128)
    pack = pack + in_flight[1][0, 0]
    gathered = _allgather8(pack, name="ag_small").reshape(N_DEV, pack.shape[0], 128)
    summed = _sum_devices(gathered)
    small_sum = summed[:rows_small].reshape(n_layers, per_layer)
    o = 0
    small_g = {}
    for nm, width in (("g_pre_mix", D_MODEL), ("g_post_mix", D_MODEL), ("g_pre_ffn", D_MODEL),
                      ("g_post_ffn", D_MODEL), ("b_in", IN_W), ("pool_scale", POOL_W), ("sinks", 128),
                      ("pool_w", 4 * 128 * 128)):
        small_g[nm] = small_sum[:, o:o + width]
        o += width
    small_g["sinks"] = small_g["sinks"][:, :N_HEADS]
    small_g["pool_w"] = small_g["pool_w"].reshape(n_layers, 4, 128, 128)
    small_g["ada_b"] = summed[rows_small:rows_small + rows_mod].reshape(n_layers, 6 * D_MODEL)
    dmod_all = gathered[:, rows_small:rows_small + rows_mod].reshape(N_DEV, n_layers, N_SHARD, ADA_SH)
    dmod_sh = lax.dynamic_index_in_dim(dmod_all, my_chip, axis=2, keepdims=False)
    g_ada_w = _ada_wgrad(jnp.transpose(c_all), jnp.transpose(dmod_sh, (1, 0, 2)))

    grads = dict(ada_w=g_ada_w, ada_b=small_g["ada_b"], b_in=small_g["b_in"], sinks=small_g["sinks"],
                 pool_w=small_g["pool_w"], pool_scale=small_g["pool_scale"], g_pre_mix=small_g["g_pre_mix"],
                 g_post_mix=small_g["g_post_mix"], g_pre_ffn=small_g["g_pre_ffn"], g_post_ffn=small_g["g_post_ffn"])
    params = dict(ada_w=(ada_w, m_ada_w, v_ada_w), ada_b=(ada_b, m_ada_b, v_ada_b), w_in=(w_in, m_w_in, v_w_in),
                  b_in=(b_in, m_b_in, v_b_in), sinks=(sinks, m_sinks, v_sinks), pool_w=(pool_w, m_pool_w, v_pool_w),
                  pool_scale=(pool_scale, m_pool_scale, v_pool_scale), w_out=(w_out, m_w_out, v_w_out),
                  w_gate=(w_gate, m_w_gate, v_w_gate), w_up=(w_up, m_w_up, v_w_up),
                  w_down=(w_down, m_w_down, v_w_down), g_pre_mix=(g_pre_mix, m_g_pre_mix, v_g_pre_mix),
                  g_post_mix=(g_post_mix, m_g_post_mix, v_g_post_mix), g_pre_ffn=(g_pre_ffn, m_g_pre_ffn, v_g_pre_ffn),
                  g_post_ffn=(g_post_ffn, m_g_post_ffn, v_g_post_ffn))
    names = list(params)
    updates = {nm: _adamw_nd(*params[nm][:1], grads[nm], *params[nm][1:], name="adamw_" + nm) for nm in grads}

    got = _rs_end(in_flight[0], updates["ada_w"][0])
    reduced[0].update(w_in=got[0], w_out=got[1])
    for nm in ("w_in", "w_out", "w_gate", "w_up", "w_down"):
        g = jnp.stack([reduced[l][nm] for l in range(n_layers)])
        if nm in ("w_in", "w_gate", "w_up"):
            upd = _adamw_nd(tr(params[nm][0]), g, tr(params[nm][1]), tr(params[nm][2]), name="adamw_" + nm)
            grads[nm], updates[nm] = tr(g), [tr(u) for u in upd]
        else:
            grads[nm], updates[nm] = g, _adamw_nd(params[nm][0], g, *params[nm][1:], name="adamw_" + nm)
    return (loss, grad_x, *[grads[nm] for nm in names], *[updates[nm][0] for nm in names],
            *[updates[nm][1] for nm in names], *[updates[nm][2] for nm in names])
```

```python
import functools

import jax
import jax.numpy as jnp
from jax import lax
from jax.experimental import pallas as pl
from jax.experimental.pallas import tpu as pltpu

F32 = jnp.float32
BF16 = jnp.bfloat16
MESH = pl.DeviceIdType.MESH

D_MODEL = 1024
ATTN_W = 512
KV_W = 128
KVD_W = 256
POOL_W = 512
IN_W = 1280
D_FF = 2816
N_SHARD = 4
FF_SH = D_FF // N_SHARD
IN_SH = IN_W // N_SHARD
OUT_SH = D_MODEL // N_SHARD
ADA_SH = 6 * D_MODEL // N_SHARD
HEAD = 64
N_HEADS = 8
GROUP = 4
BLK = 128
POOL_WINDOWS = (2, 4, 8, 16)
HALO = 16
ROT = 16
ROPE_THETA = 500000.0
EPS = 1e-6
NEG_INF = -1e30
N_DEV = 8

ADAM_LR = 0.001
ADAM_B1 = 0.9
ADAM_B2 = 0.999
ADAM_EPS = 1e-08
ADAM_WD = 0.01
ADAM_STEP = 10

VMEM_LIMIT = 48 * 1024 * 1024
FFN_VMEM_LIMIT = 60 * 1024 * 1024
WGRAD_TOKENS = 2048


def _cp(*sem, vmem=VMEM_LIMIT):
    return pltpu.CompilerParams(dimension_semantics=sem, vmem_limit_bytes=vmem)


def _full(shape):
    nd = len(shape)
    return pl.BlockSpec(shape, lambda *_: (0,) * nd)


def _resident(shape):
    nd = len(shape)
    return pl.BlockSpec(shape, lambda *_: (0,) * nd, pipeline_mode=pl.Buffered(1))


def _rows(tm, ncol):
    return pl.BlockSpec((tm, ncol), lambda i: (i, 0))


def _sds(shape, dtype):
    return jax.ShapeDtypeStruct(shape, dtype)


def _nt(a, b):
    return lax.dot_general(a, b, (((1,), (1,)), ((), ())), preferred_element_type=F32)


def _tn(a, b):
    return lax.dot_general(a, b, (((0,), (0,)), ((), ())), preferred_element_type=F32)


def _mm(a, b):
    return jnp.dot(a, b, preferred_element_type=F32)


def _rstd(x):
    return lax.rsqrt(jnp.mean(x * x, axis=-1, keepdims=True) + EPS)


def _colsum(x):
    return jnp.sum(x, axis=0, keepdims=True)


def _norm_gain_bwd(dy, xhat, rstd, gain):
    p = dy * xhat
    dx = rstd * (dy * gain - xhat * jnp.mean(p * gain, axis=-1, keepdims=True))
    return dx, _colsum(p)


def _rope_tables(pos_b, lane_tab):
    T = pos_b.shape[0]
    tm = min(T, 1024)

    def body(pos_ref, tab_ref, c_ref, s1_ref, s2_ref):
        ang = pos_ref[...].astype(F32) * tab_ref[0:1, :]
        cs = jnp.cos(ang)
        sn = jnp.sin(ang)
        m_rot = tab_ref[1:2, :]
        c_ref[...] = cs * m_rot + (1.0 - m_rot)
        s1_ref[...] = -sn * tab_ref[2:3, :]
        s2_ref[...] = sn * tab_ref[3:4, :]

    out = _sds((T, 128), F32)
    return pl.pallas_call(
        body, name="rope_tables", grid=(T // tm,),
        in_specs=[_rows(tm, 128), _full((8, 128))],
        out_specs=[_rows(tm, 128)] * 3, out_shape=[out] * 3,
        compiler_params=_cp("parallel"),
    )(pos_b, lane_tab)


def _rot_fwd(t, c, s1, s2):
    w = t.shape[-1]
    return t * c + pltpu.roll(t, w - 8, 1) * s1 + pltpu.roll(t, 8, 1) * s2


def _rot_bwd(d, c, s1, s2):
    w = d.shape[-1]
    return d * c + pltpu.roll(d * s1, 8, 1) + pltpu.roll(d * s2, w - 8, 1)


def _store_dup(ref, t):
    low = lax.broadcasted_iota(jnp.int32, t.shape, 1) < HEAD
    sw = pltpu.roll(t, HEAD, 1)
    ref[:, 0:128] = jnp.where(low, t, sw).astype(BF16)
    ref[:, 128:256] = jnp.where(low, sw, t).astype(BF16)


def _fold_dup(d):
    low = lax.broadcasted_iota(jnp.int32, (d.shape[0], 128), 1) < HEAD
    d0 = d[:, 0:128]
    d1 = d[:, 128:256]
    return jnp.where(low, d0 + pltpu.roll(d0, HEAD, 1), d1 + pltpu.roll(d1, HEAD, 1))


def _fwd_in(x, mod8, g8, w_in, b_in, rc, rs1, rs2):
    T = x.shape[0]
    tm = min(T, 512)

    def body(x_ref, mod_ref, g_ref, w_ref, b_ref, c_ref, s1_ref, s2_ref,
             h_ref, q_ref, k_ref, v_ref, u_ref):
        xf = x_ref[...]
        h = (xf * _rstd(xf) * g_ref[0:1, :]) * (1.0 + mod_ref[1:2, :]) + mod_ref[0:1, :]
        hb = h.astype(BF16)
        h_ref[...] = hb
        c = c_ref[...]
        s1 = s1_ref[...]
        s2 = s2_ref[...]
        q = _nt(hb, w_ref[0:ATTN_W, :]) + b_ref[:, 0:ATTN_W]
        q = _rot_fwd(q, jnp.tile(c, (1, 4)), jnp.tile(s1, (1, 4)), jnp.tile(s2, (1, 4)))
        q_ref[...] = (q * (HEAD ** -0.5)).astype(BF16)
        k = _nt(hb, w_ref[ATTN_W:ATTN_W + KV_W, :]) + b_ref[:, ATTN_W:ATTN_W + KV_W]
        _store_dup(k_ref, _rot_fwd(k, c, s1, s2))
        v = _nt(hb, w_ref[ATTN_W + KV_W:ATTN_W + 2 * KV_W, :]) + b_ref[:, ATTN_W + KV_W:ATTN_W + 2 * KV_W]
        _store_dup(v_ref, v)
        u_ref[...] = _nt(hb, w_ref[ATTN_W + 2 * KV_W:IN_W, :]) + b_ref[:, ATTN_W + 2 * KV_W:IN_W]

    return pl.pallas_call(
        body, name="fwd_in", grid=(T // tm,),
        in_specs=[_rows(tm, D_MODEL), _full((8, D_MODEL)), _full((8, D_MODEL)),
                  _resident((IN_W, D_MODEL)), _full((1, IN_W)),
                  _rows(tm, 128), _rows(tm, 128), _rows(tm, 128)],
        out_specs=[_rows(tm, D_MODEL), _rows(tm, ATTN_W), _rows(tm, KVD_W), _rows(tm, KVD_W), _rows(tm, POOL_W)],
        out_shape=[_sds((T, D_MODEL), BF16), _sds((T, ATTN_W), BF16), _sds((T, KVD_W), BF16),
                   _sds((T, KVD_W), BF16), _sds((T, POOL_W), F32)],
        compiler_params=_cp("parallel"),
    )(x, mod8, g8, w_in, b_in, rc, rs1, rs2)


def _band_mask(n):
    kk = lax.broadcasted_iota(jnp.int32, (2 * BLK, BLK), 0)
    qi = lax.broadcasted_iota(jnp.int32, (2 * BLK, BLK), 1)
    first = jnp.where(n > 0, 0, 2 * BLK)
    in_prev = jnp.logical_and(kk < BLK, kk > qi + first)
    in_cur = jnp.logical_and(kk >= BLK, (kk - BLK) <= qi)
    one = jnp.logical_or(in_prev, in_cur)
    return jnp.concatenate([one] * GROUP, axis=1)


def _head_row(ref, j):
    return jnp.concatenate([ref[GROUP * j + r:GROUP * j + r + 1, :] for r in range(GROUP)], axis=1)


def _stack_heads(x_ref, j):
    low = lax.broadcasted_iota(jnp.int32, (BLK, 128), 1) < HEAD
    parts = []
    for gp in (2 * j, 2 * j + 1):
        x2 = x_ref[:, gp * 128:(gp + 1) * 128]
        parts.append(jnp.where(low, x2, jnp.zeros_like(x2)))
        parts.append(jnp.where(low, jnp.zeros_like(x2), x2))
    return jnp.concatenate(parts, axis=0)


def _unstack_heads(o):
    low = lax.broadcasted_iota(jnp.int32, (BLK, 128), 1) < HEAD
    return [jnp.where(low, o[0:BLK], o[BLK:2 * BLK]), jnp.where(low, o[2 * BLK:3 * BLK], o[3 * BLK:4 * BLK])]


def _attn_fwd(q, kd, vd, sink_b):
    T = q.shape[0]
    nb = T // BLK

    def body(q_ref, kp_ref, kc_ref, vp_ref, vc_ref, sk_ref, o_ref, lse_ref):
        valid = _band_mask(pl.program_id(0))
        for j in range(N_HEADS // GROUP):
            lanes = slice(j * 128, (j + 1) * 128)
            kcat = jnp.concatenate([kp_ref[:, lanes], kc_ref[:, lanes]], axis=0)
            vcat = jnp.concatenate([vp_ref[:, lanes], vc_ref[:, lanes]], axis=0)
            s = jnp.where(valid, _nt(kcat, _stack_heads(q_ref, j)), NEG_INF)
            sk = _head_row(sk_ref, j)
            m = jnp.maximum(jnp.max(s, axis=0, keepdims=True), sk)
            p = jnp.exp(s - m)
            den = jnp.sum(p, axis=0, keepdims=True) + jnp.exp(sk - m)
            p = p * (1.0 / den)
            o = _tn(p.astype(BF16), vcat)
            o_ref[:, 2 * j * 128:(2 * j + 2) * 128] = jnp.concatenate(_unstack_heads(o), axis=1).astype(BF16)
            lse = m + jnp.log(den)
            for r in range(GROUP):
                lse_ref[GROUP * j + r:GROUP * j + r + 1, :] = lse[:, r * 128:(r + 1) * 128]

    prev = lambda n: (jnp.maximum(n - 1, 0), 0)
    cur = lambda n: (n, 0)
    return pl.pallas_call(
        body, name="attn_fwd", grid=(nb,),
        in_specs=[pl.BlockSpec((BLK, ATTN_W), cur),
                  pl.BlockSpec((BLK, KVD_W), prev), pl.BlockSpec((BLK, KVD_W), cur),
                  pl.BlockSpec((BLK, KVD_W), prev), pl.BlockSpec((BLK, KVD_W), cur),
                  _full((8, 128))],
        out_specs=[pl.BlockSpec((BLK, ATTN_W), cur), pl.BlockSpec((N_HEADS, 128), cur)],
        out_shape=[_sds((T, ATTN_W), BF16), _sds((nb * N_HEADS, 128), F32)],
        compiler_params=_cp("parallel"),
    )(q, kd, kd, vd, vd, sink_b)


def _pool_fwd(u, pool_w, pool_scale):
    T = u.shape[0]
    tm = min(T, 512)

    def body(u_ref, w_ref, sc_ref, out_ref, pooled_ref, halo):
        i = pl.program_id(0)

        @pl.when(i == 0)
        def _():
            halo[...] = jnp.zeros_like(halo)

        ub = u_ref[...]
        ext = jnp.concatenate([halo[...], ub], axis=0)
        halo[...] = ub[tm - HALO:, :]
        tpos = (i * tm + lax.broadcasted_iota(jnp.int32, (tm, 1), 0)).astype(F32)
        for g, w in enumerate(POOL_WINDOWS):
            lanes = slice(g * 128, (g + 1) * 128)
            s = ext[:, lanes]
            sh = 1
            while sh < w:
                s = s + pltpu.roll(s, sh, 0)
                sh *= 2
            cnt = jnp.minimum(tpos + 1.0, float(w))
            pb = (s[HALO:, :] / cnt - ub[:, lanes]).astype(BF16)
            z = _mm(pb, w_ref[g].astype(BF16))
            out_ref[:, lanes] = (z * sc_ref[:, lanes]).astype(BF16)
            pooled_ref[:, lanes] = pb

    return pl.pallas_call(
        body, name="pool_fwd", grid=(T // tm,),
        in_specs=[_rows(tm, POOL_W), _full((4, 128, 128)), _full((1, POOL_W))],
        out_specs=[_rows(tm, POOL_W), _rows(tm, POOL_W)],
        out_shape=[_sds((T, POOL_W), BF16), _sds((T, POOL_W), BF16)],
        scratch_shapes=[pltpu.VMEM((HALO, POOL_W), F32)],
        compiler_params=_cp("arbitrary"),
    )(u, pool_w, pool_scale)


def _fwd_out(attn, pool, x, w_out, g8, mod8):
    T = x.shape[0]
    tm = min(T, 512)

    def body(a_ref, p_ref, x_ref, w_ref, g_ref, mod_ref, mix_ref, x1_ref):
        mix = _mm(a_ref[...], w_ref[0:ATTN_W, :]) + _mm(p_ref[...], w_ref[ATTN_W:, :])
        mix_ref[...] = mix
        x1_ref[...] = x_ref[...] + mod_ref[2:3, :] * (mix * _rstd(mix) * g_ref[1:2, :])

    return pl.pallas_call(
        body, name="fwd_out", grid=(T // tm,),
        in_specs=[_rows(tm, ATTN_W), _rows(tm, POOL_W), _rows(tm, D_MODEL),
                  _resident((D_MODEL, D_MODEL)), _full((8, D_MODEL)), _full((8, D_MODEL))],
        out_specs=[_rows(tm, D_MODEL), _rows(tm, D_MODEL)],
        out_shape=[_sds((T, D_MODEL), F32), _sds((T, D_MODEL), F32)],
        compiler_params=_cp("parallel"),
    )(attn, pool, x, w_out, g8, mod8)


def _sh_rows(tm):
    return pl.BlockSpec((N_SHARD, tm, FF_SH), lambda i: (0, i, 0))


def _ffn_fwd(x1, mod8, g8, wg, wu, wd, target=None):
    T = x1.shape[0]
    tm = min(T, 512)
    last = target is not None

    def body(*refs):
        x_ref, mod_ref, g_ref, wg_ref, wu_ref, wd_ref = refs[:6]
        t_ref = refs[6] if last else None
        h_ref, act_ref, ga_ref, gb_ref, f_ref, x2_ref = refs[6 + last:12 + last]
        xf = x_ref[...]
        h = (xf * _rstd(xf) * g_ref[2:3, :]) * (1.0 + mod_ref[4:5, :]) + mod_ref[3:4, :]
        hb = h.astype(BF16)
        h_ref[...] = hb
        f = jnp.zeros((tm, D_MODEL), F32)
        for s in range(N_SHARD):
            a = _nt(hb, wg_ref[s])
            b = _nt(hb, wu_ref[s])
            sig = jax.nn.sigmoid(a)
            sl = a * sig
            act = (sl * b).astype(BF16)
            act_ref[s] = act
            ga_ref[s] = (b * (sig * (1.0 + a * (1.0 - sig)))).astype(BF16)
            gb_ref[s] = sl.astype(BF16)
            f = f + _mm(act, wd_ref[s])
        f_ref[...] = f
        x2 = xf + mod_ref[5:6, :] * (f * _rstd(f) * g_ref[3:4, :])
        if not last:
            x2_ref[...] = x2
        else:
            loss_ref = refs[13]

            @pl.when(pl.program_id(0) == 0)
            def _():
                loss_ref[...] = jnp.zeros_like(loss_ref)

            e = x2 - t_ref[...]
            x2_ref[...] = e * (1.0 / D_MODEL)
            loss_ref[...] += 0.5 * jnp.sum(jnp.mean(e * e, axis=-1, keepdims=True), axis=0, keepdims=True)

    act_shape = _sds((N_SHARD, T, FF_SH), BF16)
    weights = [_resident((N_SHARD, FF_SH, D_MODEL))] * 3
    return pl.pallas_call(
        body, name="ffn_fwd_loss" if last else "ffn_fwd", grid=(T // tm,),
        in_specs=[_rows(tm, D_MODEL), _full((8, D_MODEL)), _full((8, D_MODEL)), *weights]
        + ([_rows(tm, D_MODEL)] if last else []),
        out_specs=[_rows(tm, D_MODEL), _sh_rows(tm), _sh_rows(tm), _sh_rows(tm), _rows(tm, D_MODEL),
                   _rows(tm, D_MODEL)] + ([_full((8, 128))] if last else []),
        out_shape=[_sds((T, D_MODEL), BF16), act_shape, act_shape, act_shape, _sds((T, D_MODEL), F32),
                   _sds((T, D_MODEL), F32)] + ([_sds((8, 128), F32)] if last else []),
        compiler_params=_cp("arbitrary" if last else "parallel", vmem=FFN_VMEM_LIMIT),
    )(x1, mod8, g8, wg, wu, wd, *([target] if last else []))


def _ffn_bwd(dx2, f, ga, gb, x1, mod8, g8, wg, wu, wd):
    T = dx2.shape[0]
    tm = min(T, 256)

    def body(dx_ref, f_ref, ga_ref, gb_ref, x_ref, mod_ref, g_ref, wg_ref, wu_ref, wd_ref,
             dx1_ref, df_ref, da_ref, db_ref, red_ref):
        @pl.when(pl.program_id(0) == 0)
        def _():
            red_ref[...] = jnp.zeros_like(red_ref)

        dx = dx_ref[...]
        fv = f_ref[...]
        rstd = _rstd(fv)
        fhat = fv * rstd
        gpost = g_ref[3:4, :]
        gate = mod_ref[5:6, :]
        df, s_post = _norm_gain_bwd(dx, fhat, rstd, gate * gpost)
        red_ref[0:1, :] += gpost * s_post
        red_ref[1:2, :] += gate * s_post
        dfb = df.astype(BF16)
        df_ref[...] = dfb
        dh = jnp.zeros((tm, D_MODEL), F32)
        for s in range(N_SHARD):
            dact = _nt(dfb, wd_ref[s])
            da = (dact * ga_ref[s].astype(F32)).astype(BF16)
            db = (dact * gb_ref[s].astype(F32)).astype(BF16)
            da_ref[s] = da
            db_ref[s] = db
            dh = dh + _mm(da, wg_ref[s]) + _mm(db, wu_ref[s])
        xf = x_ref[...]
        rstd1 = _rstd(xf)
        xhat = xf * rstd1
        gpre = g_ref[2:3, :]
        scale1 = 1.0 + mod_ref[4:5, :]
        dxn, s_pre = _norm_gain_bwd(dh, xhat, rstd1, scale1 * gpre)
        red_ref[2:3, :] += _colsum(dh)
        red_ref[3:4, :] += gpre * s_pre
        red_ref[4:5, :] += scale1 * s_pre
        dx1_ref[...] = dx + dxn

    act_shape = _sds((N_SHARD, T, FF_SH), BF16)
    return pl.pallas_call(
        body, name="ffn_bwd", grid=(T // tm,),
        in_specs=[_rows(tm, D_MODEL), _rows(tm, D_MODEL), _sh_rows(tm), _sh_rows(tm), _rows(tm, D_MODEL),
                  _full((8, D_MODEL)), _full((8, D_MODEL)),
                  _resident((N_SHARD, FF_SH, D_MODEL)), _resident((N_SHARD, FF_SH, D_MODEL)),
                  _resident((N_SHARD, FF_SH, D_MODEL))],
        out_specs=[_rows(tm, D_MODEL), _rows(tm, D_MODEL), _sh_rows(tm), _sh_rows(tm), _full((8, D_MODEL))],
        out_shape=[_sds((T, D_MODEL), F32), _sds((T, D_MODEL), BF16), act_shape, act_shape, _sds((8, D_MODEL), F32)],
        compiler_params=_cp("arbitrary"),
    )(dx2, f, ga, gb, x1, mod8, g8, wg, wu, wd)


def _wgrad(a, b, name):
    T, K = a.shape
    N = b.shape[1]
    tt = min(T, WGRAD_TOKENS)
    tk = next(c for c in (640, 512, 256, 128) if K % c == 0)

    def body(a_ref, b_ref, o_ref):
        @pl.when(pl.program_id(1) == 0)
        def _():
            o_ref[...] = jnp.zeros_like(o_ref)

        o_ref[...] += _tn(a_ref[...], b_ref[...])

    return pl.pallas_call(
        body, name=name, grid=(K // tk, T // tt),
        in_specs=[pl.BlockSpec((tt, tk), lambda i, t: (t, i)), pl.BlockSpec((tt, N), lambda i, t: (t, 0))],
        out_specs=pl.BlockSpec((tk, N), lambda i, t: (i, 0)),
        out_shape=_sds((K, N), F32),
        compiler_params=_cp("parallel", "arbitrary"),
    )(a, b)


def _wgrad_rows(a, b, name):
    T, N = b.shape
    k = a.shape[2]
    tt = min(T, WGRAD_TOKENS)

    def body(a_ref, b_ref, o_ref):
        @pl.when(pl.program_id(1) == 0)
        def _():
            o_ref[...] = jnp.zeros_like(o_ref)

        o_ref[...] += _tn(a_ref[...], b_ref[...])

    return pl.pallas_call(
        body, name=name, grid=(N_SHARD, T // tt),
        in_specs=[pl.BlockSpec((None, tt, k), lambda s, t: (s, t, 0)), pl.BlockSpec((tt, N), lambda s, t: (t, 0))],
        out_specs=pl.BlockSpec((None, k, N), lambda s, t: (s, 0, 0)),
        out_shape=_sds((N_SHARD, k, N), F32),
        compiler_params=_cp("parallel", "arbitrary"),
    )(a, b)


def _mix_bwd(dx1, mix, mod8, g8, w_out):
    T = dx1.shape[0]
    tm = min(T, 512)

    def body(dx_ref, mix_ref, mod_ref, g_ref, w_ref, dmix_ref, da_ref, dp_ref, red_ref):
        @pl.when(pl.program_id(0) == 0)
        def _():
            red_ref[...] = jnp.zeros_like(red_ref)

        dx = dx_ref[...]
        mv = mix_ref[...]
        rstd = _rstd(mv)
        mhat = mv * rstd
        gpost = g_ref[1:2, :]
        gate = mod_ref[2:3, :]
        dm, s_post = _norm_gain_bwd(dx, mhat, rstd, gate * gpost)
        red_ref[0:1, :] += gpost * s_post
        red_ref[1:2, :] += gate * s_post
        dmb = dm.astype(BF16)
        dmix_ref[...] = dmb
        da_ref[...] = _nt(dmb, w_ref[0:ATTN_W, :]).astype(BF16)
        dp_ref[...] = _nt(dmb, w_ref[ATTN_W:, :]).astype(BF16)

    return pl.pallas_call(
        body, name="mix_bwd", grid=(T // tm,),
        in_specs=[_rows(tm, D_MODEL), _rows(tm, D_MODEL), _full((8, D_MODEL)), _full((8, D_MODEL)),
                  _resident((D_MODEL, D_MODEL))],
        out_specs=[_rows(tm, D_MODEL), _rows(tm, ATTN_W), _rows(tm, POOL_W), _full((8, D_MODEL))],
        out_shape=[_sds((T, D_MODEL), BF16), _sds((T, ATTN_W), BF16), _sds((T, POOL_W), BF16),
                   _sds((8, D_MODEL), F32)],
        compiler_params=_cp("arbitrary"),
    )(dx1, mix, mod8, g8, w_out)


def _attn_bwd(q, kd, vd, lse, dattn, sink_b):
    T = q.shape[0]
    nb = T // BLK

    def body(q_ref, do_ref, lse_ref, kp_ref, kc_ref, vp_ref, vc_ref, sk_ref,
             dq_ref, dk_ref, dv_ref, dsk_ref, carry_k, carry_v):
        n = pl.program_id(0)

        @pl.when(n == 0)
        def _():
            carry_k[...] = jnp.zeros_like(carry_k)
            carry_v[...] = jnp.zeros_like(carry_v)
            dsk_ref[...] = jnp.zeros_like(dsk_ref)

        @pl.when(n < nb)
        def _():
            valid = _band_mask(n)
            for j in range(N_HEADS // GROUP):
                lanes = slice(j * 128, (j + 1) * 128)
                kcat = jnp.concatenate([kp_ref[:, lanes], kc_ref[:, lanes]], axis=0)
                vcat = jnp.concatenate([vp_ref[:, lanes], vc_ref[:, lanes]], axis=0)
                qs = _stack_heads(q_ref, j)
                dos = _stack_heads(do_ref, j)
                lse = _head_row(lse_ref, j)
                p = jnp.exp(jnp.where(valid, _nt(kcat, qs), NEG_INF) - lse)
                dp = _nt(vcat, dos)
                delta = jnp.sum(p * dp, axis=0, keepdims=True)
                ds = (p * (dp - delta)).astype(BF16)
                sink_term = jnp.exp(_head_row(sk_ref, j) - lse) * delta
                for r in range(GROUP):
                    h = GROUP * j + r
                    dsk_ref[h:h + 1, :] += -jnp.sum(sink_term[:, r * 128:(r + 1) * 128], axis=1, keepdims=True)
                dq_ref[:, 2 * j * 128:(2 * j + 2) * 128] = jnp.concatenate(_unstack_heads(_tn(ds, kcat)), axis=1)
                dk = _mm(ds, qs)
                dv = _mm(p.astype(BF16), dos)
                dk_ref[:, lanes] = carry_k[:, lanes] + dk[0:BLK]
                dv_ref[:, lanes] = carry_v[:, lanes] + dv[0:BLK]
                carry_k[:, lanes] = dk[BLK:]
                carry_v[:, lanes] = dv[BLK:]

        @pl.when(n == nb)
        def _():
            dk_ref[...] = carry_k[...]
            dv_ref[...] = carry_v[...]

    cur = lambda n: (jnp.minimum(n, nb - 1), 0)
    prev = lambda n: (jnp.maximum(n - 1, 0), 0)
    return pl.pallas_call(
        body, name="attn_bwd", grid=(nb + 1,),
        in_specs=[pl.BlockSpec((BLK, ATTN_W), cur), pl.BlockSpec((BLK, ATTN_W), cur), pl.BlockSpec((N_HEADS, 128), cur),
                  pl.BlockSpec((BLK, KVD_W), prev), pl.BlockSpec((BLK, KVD_W), cur),
                  pl.BlockSpec((BLK, KVD_W), prev), pl.BlockSpec((BLK, KVD_W), cur),
                  _full((8, 128))],
        out_specs=[pl.BlockSpec((BLK, ATTN_W), cur), pl.BlockSpec((BLK, KVD_W), prev),
                   pl.BlockSpec((BLK, KVD_W), prev), _full((8, 128))],
        out_shape=[_sds((T, ATTN_W), F32), _sds((T, KVD_W), F32), _sds((T, KVD_W), F32), _sds((8, 128), F32)],
        scratch_shapes=[pltpu.VMEM((BLK, KVD_W), F32), pltpu.VMEM((BLK, KVD_W), F32)],
        compiler_params=_cp("arbitrary"),
    )(q, dattn, lse, kd, kd, vd, vd, sink_b)


def _pool_bwd(dpool, pooled, pool_w, pool_scale):
    T = dpool.shape[0]
    tm = min(T, 512)
    nbk = T // tm
    ext_rows = tm + HALO

    def body(dp_ref, pl_ref, w_ref, sc_ref, du_ref, dw_ref, dsc_ref, halo):
        i = pl.program_id(0)

        @pl.when(i == 0)
        def _():
            halo[...] = jnp.zeros_like(halo)
            dw_ref[...] = jnp.zeros_like(dw_ref)
            dsc_ref[...] = jnp.zeros_like(dsc_ref)

        blk = nbk - 1 - i
        tpos = (blk * tm + lax.broadcasted_iota(jnp.int32, (tm, 1), 0)).astype(F32)
        for g, w in enumerate(POOL_WINDOWS):
            lanes = slice(g * 128, (g + 1) * 128)
            dp = dp_ref[:, lanes].astype(F32)
            pb = pl_ref[:, lanes]
            wg = w_ref[g].astype(BF16)
            z = _mm(pb, wg)
            dsc_ref[0:1, lanes] += _colsum(dp * z)
            dz = (dp * sc_ref[:, lanes]).astype(BF16)
            dw_ref[g] += _tn(pb, dz)
            dpl = _nt(dz, wg)
            e = dpl / jnp.minimum(tpos + 1.0, float(w))
            s = jnp.concatenate([e, halo[:, lanes]], axis=0)
            halo[:, lanes] = e[0:HALO, :]
            sh = 1
            while sh < w:
                s = s + pltpu.roll(s, ext_rows - sh, 0)
                sh *= 2
            du_ref[:, lanes] = s[0:tm, :] - dpl

    rev = lambda i: (nbk - 1 - i, 0)
    return pl.pallas_call(
        body, name="pool_bwd", grid=(nbk,),
        in_specs=[pl.BlockSpec((tm, POOL_W), rev), pl.BlockSpec((tm, POOL_W), rev),
                  _full((4, 128, 128)), _full((1, POOL_W))],
        out_specs=[pl.BlockSpec((tm, POOL_W), rev), _full((4, 128, 128)), _full((8, POOL_W))],
        out_shape=[_sds((T, POOL_W), F32), _sds((4, 128, 128), F32), _sds((8, POOL_W), F32)],
        scratch_shapes=[pltpu.VMEM((HALO, POOL_W), F32)],
        compiler_params=_cp("arbitrary"),
    )(dpool, pooled, pool_w, pool_scale)


def _in_bwd(dq, dk, dv, du, rc, rs1, rs2, x, dx1, mod8, g8, w_in):
    T = x.shape[0]
    tm = min(T, 512)

    def body(dq_ref, dk_ref, dv_ref, du_ref, c_ref, s1_ref, s2_ref, x_ref, dx1_ref, mod_ref, g_ref, w_ref,
             dx_ref, dproj_ref, red_ref, dbin_ref):
        @pl.when(pl.program_id(0) == 0)
        def _():
            red_ref[...] = jnp.zeros_like(red_ref)
            dbin_ref[...] = jnp.zeros_like(dbin_ref)

        c = c_ref[...]
        s1 = s1_ref[...]
        s2 = s2_ref[...]
        dqp = _rot_bwd(dq_ref[...] * (HEAD ** -0.5), jnp.tile(c, (1, 4)), jnp.tile(s1, (1, 4)), jnp.tile(s2, (1, 4)))
        dkp = _rot_bwd(_fold_dup(dk_ref[...]), c, s1, s2)
        pieces = ((0, ATTN_W, dqp), (ATTN_W, ATTN_W + KV_W, dkp),
                  (ATTN_W + KV_W, ATTN_W + 2 * KV_W, _fold_dup(dv_ref[...])), (ATTN_W + 2 * KV_W, IN_W, du_ref[...]))
        dh = jnp.zeros((tm, D_MODEL), F32)
        for lo, hi, val in pieces:
            dbin_ref[0:1, lo:hi] += _colsum(val)
            vb = val.astype(BF16)
            dproj_ref[:, lo:hi] = vb
            dh = dh + _mm(vb, w_ref[lo:hi, :])
        xf = x_ref[...]
        rstd = _rstd(xf)
        xhat = xf * rstd
        gpre = g_ref[0:1, :]
        scale1 = 1.0 + mod_ref[1:2, :]
        dxn, s_pre = _norm_gain_bwd(dh, xhat, rstd, scale1 * gpre)
        red_ref[0:1, :] += _colsum(dh)
        red_ref[1:2, :] += gpre * s_pre
        red_ref[2:3, :] += scale1 * s_pre
        dx_ref[...] = dx1_ref[...] + dxn

    return pl.pallas_call(
        body, name="in_bwd", grid=(T // tm,),
        in_specs=[_rows(tm, ATTN_W), _rows(tm, KVD_W), _rows(tm, KVD_W), _rows(tm, POOL_W),
                  _rows(tm, 128), _rows(tm, 128), _rows(tm, 128), _rows(tm, D_MODEL), _rows(tm, D_MODEL),
                  _full((8, D_MODEL)), _full((8, D_MODEL)), _resident((IN_W, D_MODEL))],
        out_specs=[_rows(tm, D_MODEL), _rows(tm, IN_W), _full((8, D_MODEL)), _full((8, IN_W))],
        out_shape=[_sds((T, D_MODEL), F32), _sds((T, IN_W), BF16), _sds((8, D_MODEL), F32), _sds((8, IN_W), F32)],
        compiler_params=_cp("arbitrary"),
    )(dq, dk, dv, du, rc, rs1, rs2, x, dx1, mod8, g8, w_in)


def _mod_fwd(c_all, ada_w, ada_b_sh):
    tn = 512

    def body(c_ref, w_ref, b_ref, o_ref):
        cv = c_ref[...]
        ca = (cv * jax.nn.sigmoid(cv)).astype(BF16)
        o_ref[...] = _mm(ca, w_ref[...].astype(BF16)) + b_ref[...]

    return pl.pallas_call(
        body, name="mod_fwd", grid=(2, ADA_SH // tn),
        in_specs=[_full((8, D_MODEL)), pl.BlockSpec((None, D_MODEL, tn), lambda l, j: (l, 0, j)),
                  pl.BlockSpec((None, 1, tn), lambda l, j: (l, 0, j))],
        out_specs=pl.BlockSpec((None, 8, tn), lambda l, j: (l, 0, j)),
        out_shape=_sds((2, 8, ADA_SH), F32),
        compiler_params=_cp("parallel", "parallel"),
    )(c_all, ada_w, ada_b_sh)


def _ada_wgrad(c_all_t, dmod_sh):
    tn = 512

    def body(c_ref, d_ref, o_ref):
        cv = c_ref[...]
        ca = cv * jax.nn.sigmoid(cv)
        o_ref[...] = jnp.dot(ca, d_ref[...], preferred_element_type=F32, precision=lax.Precision.HIGHEST)

    return pl.pallas_call(
        body, name="ada_wgrad", grid=(2, ADA_SH // tn),
        in_specs=[_full((D_MODEL, 8)), pl.BlockSpec((None, 8, tn), lambda l, j: (l, 0, j))],
        out_specs=pl.BlockSpec((None, D_MODEL, tn), lambda l, j: (l, 0, j)),
        out_shape=_sds((2, D_MODEL, ADA_SH), F32),
        compiler_params=_cp("parallel", "parallel"),
    )(c_all_t, dmod_sh)


def _sum_devices(g):
    R = g.shape[1]

    def body(g_ref, o_ref):
        acc = g_ref[0]
        for d in range(1, N_DEV):
            acc = acc + g_ref[d]
        o_ref[...] = acc

    return pl.pallas_call(
        body, name="sum_devices", grid=(1,),
        in_specs=[_full((N_DEV, R, 128))], out_specs=_full((R, 128)), out_shape=_sds((R, 128), F32),
        compiler_params=_cp("arbitrary"),
    )(g)


def _adamw(w, g, m, v, name):
    R, C = w.shape
    tr = R
    for cand in (256, 128, 64, 32, 16, 8):
        if R % cand == 0 and cand * C * 4 <= 2 * 1024 * 1024:
            tr = cand
            break

    def body(w_ref, g_ref, m_ref, v_ref, d_ref, nm_ref, nv_ref):
        gv = g_ref[...]
        mn = ADAM_B1 * m_ref[...] + (1.0 - ADAM_B1) * gv
        vn = ADAM_B2 * v_ref[...] + (1.0 - ADAM_B2) * (gv * gv)
        m_hat = mn / (1.0 - ADAM_B1 ** ADAM_STEP)
        v_hat = vn / (1.0 - ADAM_B2 ** ADAM_STEP)
        d_ref[...] = -ADAM_LR * (m_hat / (jnp.sqrt(v_hat) + ADAM_EPS) + ADAM_WD * w_ref[...])
        nm_ref[...] = mn
        nv_ref[...] = vn

    spec = pl.BlockSpec((tr, C), lambda i: (i, 0))
    out = _sds((R, C), F32)
    return pl.pallas_call(
        body, name=name, grid=(R // tr,),
        in_specs=[spec] * 4, out_specs=[spec] * 3, out_shape=[out] * 3,
        compiler_params=_cp("parallel"),
    )(w, g, m, v)


def _adamw_nd(w, g, m, v, name):
    shape = w.shape
    if w.ndim == 2 and shape[1] < 128:
        view = (1, shape[0] * shape[1])
    else:
        view = (-1, shape[-1])
    outs = _adamw(*[t.reshape(view) for t in (w, g, m, v)], name=name)
    return [o.reshape(shape) for o in outs]


def _coords():
    return lax.axis_index("x"), lax.axis_index("y"), lax.axis_index("c")


def _other_chips(x, y):
    return [(1 - x, y), (x, 1 - y), (1 - x, 1 - y)]


def _allgather8(blk, name):
    m_per, n = blk.shape

    def body(x_ref, out_ref, send_sems, recv_sems, local_sem):
        x, y, c = _coords()
        me, sibling = (x, y, c), (x, y, 1 - c)
        chips = _other_chips(x, y)

        def rows(px, py, pc):
            return out_ref.at[pl.ds((4 * px + 2 * py + pc) * m_per, m_per), :]

        def copy(k, block, to, src=None):
            return pltpu.make_async_remote_copy(
                src_ref=rows(*block) if src is None else src, dst_ref=rows(*block),
                send_sem=send_sems.at[k], recv_sem=recv_sems.at[k], device_id=to, device_id_type=MESH)

        mine = pltpu.make_async_copy(x_ref, rows(*me), local_sem)
        mine.start()
        first = [copy(0, me, sibling, src=x_ref)]
        first += [copy(1 + j, me, (*chip, c), src=x_ref) for j, chip in enumerate(chips)]
        for cp in first:
            cp.start()
        passed = [copy(4 + j, (*chip, c), sibling) for j, chip in enumerate(chips)]
        for j, chip in enumerate(chips):
            copy(1 + j, (*chip, c), me).wait_recv()
            passed[j].start()
        copy(0, sibling, me).wait_recv()
        for j, chip in enumerate(chips):
            copy(4 + j, (*chip, 1 - c), me).wait_recv()
        for cp in first + passed:
            cp.wait_send()
        mine.wait()

    return pl.pallas_call(
        body, name=name,
        out_shape=_sds((N_DEV * m_per, n), blk.dtype),
        in_specs=[pl.BlockSpec(memory_space=pltpu.VMEM)],
        out_specs=pl.BlockSpec(memory_space=pltpu.VMEM),
        scratch_shapes=[pltpu.SemaphoreType.DMA((7,)), pltpu.SemaphoreType.DMA((7,)), pltpu.SemaphoreType.DMA],
        compiler_params=pltpu.CompilerParams(vmem_limit_bytes=VMEM_LIMIT),
    )(blk)


def _row_tile(r, n):
    for cand in range(r, 15, -16):
        if r % cand == 0 and cand % 16 == 0 and cand * n * 4 <= 2 * 1024 * 1024:
            return cand
    return r


def _cast_slot(w, chip, name):
    r, n = w.shape
    tr = _row_tile(r, n)

    def body(chip_ref, w_ref, o_ref):
        o_ref[...] = w_ref[...].astype(BF16)

    grid_spec = pltpu.PrefetchScalarGridSpec(
        num_scalar_prefetch=1, grid=(r // tr,),
        in_specs=[pl.BlockSpec((tr, n), lambda i, ch: (i, 0))],
        out_specs=pl.BlockSpec((None, tr, n), lambda i, ch: (ch[0], i, 0)))
    return pl.pallas_call(
        body, name=name, grid_spec=grid_spec, out_shape=_sds((N_SHARD, r, n), BF16),
        compiler_params=_cp("arbitrary"),
    )(chip, w)


def _allgather_weights(bufs, name):
    nt = len(bufs)
    hom = [pl.BlockSpec(memory_space=pl.ANY)] * nt

    def body(*refs):
        outs = refs[nt:2 * nt]
        send_sems, recv_sems = refs[2 * nt:]
        x, y, c = _coords()
        sibling = (x, y, 1 - c)
        chips = _other_chips(x, y)

        def copy(t, k, block_chip, hc, to):
            r = outs[t].shape[1] // 2
            blk = outs[t].at[2 * block_chip[0] + block_chip[1], pl.ds(hc * r, r)]
            return pltpu.make_async_remote_copy(
                src_ref=blk, dst_ref=blk,
                send_sem=send_sems.at[t, k], recv_sem=recv_sems.at[t, k], device_id=to, device_id_type=MESH)

        started = []
        for t in range(nt):
            for j, chip in enumerate(chips):
                cp = copy(t, j, (x, y), c, (*chip, c))
                cp.start()
                started.append(cp)
        for t in range(nt):
            for j, chip in enumerate(chips):
                copy(t, j, chip, c, sibling).wait_recv()
                fw = copy(t, 3 + j, chip, c, sibling)
                fw.start()
                started.append(fw)
        for t in range(nt):
            for j, chip in enumerate(chips):
                copy(t, 3 + j, chip, 1 - c, sibling).wait_recv()
        for cp in started:
            cp.wait_send()

    return pl.pallas_call(
        body, name=name,
        out_shape=[_sds(b.shape, b.dtype) for b in bufs],
        in_specs=hom, out_specs=hom,
        input_output_aliases={t: t for t in range(nt)},
        scratch_shapes=[pltpu.SemaphoreType.DMA((nt, 6)), pltpu.SemaphoreType.DMA((nt, 6))],
    )(*bufs)


def _swap_halves(grads, name):
    nt = len(grads)
    hom = [pl.BlockSpec(memory_space=pl.ANY)] * nt

    def body(*refs):
        ins = refs[:nt]
        outs = refs[nt:2 * nt]
        send_sems, recv_sems = refs[2 * nt:]
        x, y, c = _coords()
        sibling = (x, y, 1 - c)
        cps = []
        for t in range(nt):
            r = ins[t].shape[1] // 2
            cp = pltpu.make_async_remote_copy(
                src_ref=ins[t].at[:, pl.ds((1 - c) * r, r)], dst_ref=outs[t],
                send_sem=send_sems.at[t], recv_sem=recv_sems.at[t], device_id=sibling, device_id_type=MESH)
            cp.start()
            cps.append(cp)
        for cp in cps:
            cp.wait()

    return pl.pallas_call(
        body, name=name,
        out_shape=[_sds((N_SHARD, g.shape[1] // 2, g.shape[2]), g.dtype) for g in grads],
        in_specs=hom, out_specs=hom,
        scratch_shapes=[pltpu.SemaphoreType.DMA((nt,)), pltpu.SemaphoreType.DMA((nt,))],
    )(*grads)


def _scatter_chips(sums, name):
    nt = len(sums)
    hom = [pl.BlockSpec(memory_space=pl.ANY)] * nt

    def body(*refs):
        ins = refs[:nt]
        outs = refs[nt:2 * nt]
        send_sems, recv_sems = refs[2 * nt:]
        x, y, c = _coords()
        chips = _other_chips(x, y)
        cps = []
        for t in range(nt):
            for j, chip in enumerate(chips):
                cp = pltpu.make_async_remote_copy(
                    src_ref=ins[t].at[2 * chip[0] + chip[1]], dst_ref=outs[t].at[j],
                    send_sem=send_sems.at[t, j], recv_sem=recv_sems.at[t, j],
                    device_id=(*chip, c), device_id_type=MESH)
                cp.start()
                cps.append(cp)
        for cp in cps:
            cp.wait()

    return pl.pallas_call(
        body, name=name,
        out_shape=[_sds((3,) + s.shape[1:], s.dtype) for s in sums],
        in_specs=hom, out_specs=hom,
        scratch_shapes=[pltpu.SemaphoreType.DMA((nt, 3)), pltpu.SemaphoreType.DMA((nt, 3))],
    )(*sums)


def _join_halves(tots, name):
    nt = len(tots)
    hom = [pl.BlockSpec(memory_space=pl.ANY)] * nt

    def body(*refs):
        outs = refs[nt:2 * nt]
        send_sems, recv_sems = refs[2 * nt:]
        x, y, c = _coords()
        sibling = (x, y, 1 - c)
        cps = []
        for t in range(nt):
            cp = pltpu.make_async_remote_copy(
                src_ref=outs[t].at[c], dst_ref=outs[t].at[c],
                send_sem=send_sems.at[t], recv_sem=recv_sems.at[t], device_id=sibling, device_id_type=MESH)
            cp.start()
            cps.append(cp)
        for t in range(nt):
            pltpu.make_async_remote_copy(
                src_ref=outs[t].at[c], dst_ref=outs[t].at[1 - c],
                send_sem=send_sems.at[t], recv_sem=recv_sems.at[t], device_id=sibling, device_id_type=MESH).wait_recv()
        for cp in cps:
            cp.wait_send()

    return pl.pallas_call(
        body, name=name,
        out_shape=[_sds(t.shape, t.dtype) for t in tots],
        in_specs=hom, out_specs=hom,
        input_output_aliases={t: t for t in range(nt)},
        scratch_shapes=[pltpu.SemaphoreType.DMA((nt,)), pltpu.SemaphoreType.DMA((nt,))],
    )(*tots)


def _pair_sum(g, recv, core, chip, name):
    _, _, r, n = g.shape
    tr = _row_tile(r, n)

    def body(core_ref, chip_ref, g_ref, r_ref, sb_ref, own_ref):
        tot = g_ref[...] + r_ref[...]
        sb_ref[...] = tot.astype(BF16)

        @pl.when(pl.program_id(1) == chip_ref[0])
        def _():
            own_ref[...] = tot

    grid_spec = pltpu.PrefetchScalarGridSpec(
        num_scalar_prefetch=2, grid=(r // tr, N_SHARD),
        in_specs=[pl.BlockSpec((None, None, tr, n), lambda i, s, co, ch: (s, co[0], i, 0)),
                  pl.BlockSpec((None, tr, n), lambda i, s, co, ch: (s, i, 0))],
        out_specs=[pl.BlockSpec((None, tr, n), lambda i, s, co, ch: (s, i, 0)),
                   pl.BlockSpec((tr, n), lambda i, s, co, ch: (i, 0))])
    return pl.pallas_call(
        body, name=name, grid_spec=grid_spec,
        out_shape=[_sds((N_SHARD, r, n), BF16), _sds((r, n), F32)],
        compiler_params=_cp("arbitrary", "arbitrary"),
    )(core, chip, g, recv)


def _chip_sum(own, recv, core, name):
    r, n = own.shape
    tr = _row_tile(r, n)

    def body(core_ref, o_ref, r_ref, t_ref):
        acc = o_ref[...]
        for j in range(3):
            acc = acc + r_ref[j].astype(F32)
        t_ref[...] = acc

    grid_spec = pltpu.PrefetchScalarGridSpec(
        num_scalar_prefetch=1, grid=(r // tr,),
        in_specs=[pl.BlockSpec((tr, n), lambda i, co: (i, 0)), pl.BlockSpec((3, tr, n), lambda i, co: (0, i, 0))],
        out_specs=pl.BlockSpec((None, tr, n), lambda i, co: (co[0], i, 0)))
    return pl.pallas_call(
        body, name=name, grid_spec=grid_spec, out_shape=_sds((2, r, n), F32),
        compiler_params=_cp("arbitrary"),
    )(core, own, recv)


_HBM = pl.BlockSpec(memory_space=pltpu.HBM)
_SEM = pl.BlockSpec(memory_space=pltpu.SEMAPHORE)
_EFFECT = pltpu.SideEffectType.DATAFLOW_SIDE_EFFECTING


def _ici_copies(srcs, dsts, send_sems, recv_sems, send_view, recv_view):
    x, y, c = _coords()
    out = []
    for t in range(len(srcs)):
        for j, chip in enumerate(_other_chips(x, y)):
            out.append(pltpu.make_async_remote_copy(
                src_ref=send_view(srcs[t], chip, j, (x, y), c), dst_ref=recv_view(dsts[t], chip, j, (x, y), c),
                send_sem=send_sems.at[3 * t + j], recv_sem=recv_sems.at[3 * t + j],
                device_id=(*chip, c), device_id_type=MESH))
    return out


def _ici_start(srcs, dsts, after, send_view, recv_view, name):
    nt = len(srcs)
    inplace = dsts is None
    nbuf = nt if inplace else 2 * nt

    def body(*refs):
        send_sems, recv_sems = refs[nbuf + 1], refs[nbuf + 2]
        s_out = refs[nbuf + 3:nbuf + 3 + nt]
        d_out = s_out if inplace else refs[nbuf + 3 + nt:nbuf + 3 + 2 * nt]
        token = refs[-1]
        for cp in _ici_copies(s_out, d_out, send_sems, recv_sems, send_view, recv_view):
            cp.start()
        token[...] = jnp.zeros_like(token)

    bufs = list(srcs) + ([] if inplace else list(dsts))
    res = pl.pallas_call(
        body, name=name,
        out_shape=(pltpu.SemaphoreType.DMA((3 * nt,)), pltpu.SemaphoreType.DMA((3 * nt,)),
                   *[pltpu.HBM(b.shape, b.dtype) for b in bufs], _sds((8, 128), F32)),
        in_specs=[_HBM] * nbuf + [pl.BlockSpec(memory_space=pl.ANY)],
        out_specs=(_SEM, _SEM, *[_HBM] * nbuf, pl.BlockSpec(memory_space=pltpu.VMEM)),
        input_output_aliases={i: 2 + i for i in range(nbuf)},
        compiler_params=pltpu.CompilerParams(has_side_effects=_EFFECT),
    )(*[pltpu.with_memory_space_constraint(b, pltpu.HBM) for b in bufs], after)
    send_sems, recv_sems = res[0], res[1]
    s_thru = list(res[2:2 + nt])
    d_thru = s_thru if inplace else list(res[2 + nt:2 + 2 * nt])
    return send_sems, recv_sems, s_thru, d_thru, res[-1]


def _ici_wait(send_sems, recv_sems, srcs, dsts, after, send_view, recv_view, name):
    nt = len(srcs)
    inplace = dsts is None
    nbuf = nt if inplace else 2 * nt

    def body(*refs):
        send_ref, recv_ref = refs[nbuf], refs[nbuf + 1]
        s_out = refs[nbuf + 3:nbuf + 3 + nt]
        d_out = s_out if inplace else refs[nbuf + 3 + nt:nbuf + 3 + 2 * nt]
        for cp in _ici_copies(s_out, d_out, send_ref, recv_ref, send_view, recv_view):
            cp.wait_send()
            cp.wait_recv()

    bufs = list(srcs) + ([] if inplace else list(dsts))
    res = pl.pallas_call(
        body, name=name,
        out_shape=tuple(pltpu.HBM(b.shape, b.dtype) for b in bufs),
        in_specs=[_HBM] * nbuf + [_SEM, _SEM, pl.BlockSpec(memory_space=pl.ANY)],
        out_specs=tuple([_HBM] * nbuf),
        input_output_aliases={i: i for i in range(nbuf)},
        compiler_params=pltpu.CompilerParams(has_side_effects=_EFFECT),
    )(*bufs, send_sems, recv_sems, after)
    return list(res[:nt]) if inplace else list(res[nt:])


def _w_half(buf, chip, c):
    r = buf.shape[1] // 2
    return buf.at[2 * chip[0] + chip[1], pl.ds(c * r, r)]


def _ag_send_view(buf, chip, j, me, c):
    return _w_half(buf, me, c)


def _ag_recv_view(buf, chip, j, me, c):
    return _w_half(buf, me, c)


def _rs_send_view(buf, chip, j, me, c):
    return buf.at[2 * chip[0] + chip[1]]


def _rs_recv_view(buf, chip, j, me, c):
    return buf.at[j]


def _ag_forward(bufs, name):
    nt = len(bufs)
    hom = [pl.BlockSpec(memory_space=pl.ANY)] * nt

    def body(*refs):
        outs = refs[nt:2 * nt]
        send_sems, recv_sems = refs[2 * nt:]
        x, y, c = _coords()
        sibling = (x, y, 1 - c)
        chips = _other_chips(x, y)

        def copy(t, j, hc):
            blk = _w_half(outs[t], chips[j], hc)
            return pltpu.make_async_remote_copy(
                src_ref=blk, dst_ref=blk, send_sem=send_sems.at[t, j], recv_sem=recv_sems.at[t, j],
                device_id=sibling, device_id_type=MESH)

        started = [copy(t, j, c) for t in range(nt) for j in range(3)]
        for cp in started:
            cp.start()
        for t in range(nt):
            for j in range(3):
                copy(t, j, 1 - c).wait_recv()
        for cp in started:
            cp.wait_send()

    return pl.pallas_call(
        body, name=name,
        out_shape=[_sds(b.shape, b.dtype) for b in bufs],
        in_specs=hom, out_specs=hom,
        input_output_aliases={t: t for t in range(nt)},
        scratch_shapes=[pltpu.SemaphoreType.DMA((nt, 3)), pltpu.SemaphoreType.DMA((nt, 3))],
    )(*bufs)


def _rs_begin(grads, after, tag):
    x, y, c = _coords()
    core = jnp.reshape(c, (1,)).astype(jnp.int32)
    chip = jnp.reshape(2 * x + y, (1,)).astype(jnp.int32)
    recv = _swap_halves(grads, name="rs_swap_" + tag)
    sums, owns = [], []
    for t, (g, rv) in enumerate(zip(grads, recv)):
        r = g.shape[1] // 2
        sb, own = _pair_sum(g.reshape(N_SHARD, 2, r, g.shape[2]), rv, core, chip, name=f"rs_pair_{tag}_{t}")
        sums.append(sb)
        owns.append(own)
    land = [lax.empty((3,) + s.shape[1:], s.dtype) for s in sums]
    send_sems, recv_sems, s_thru, d_thru, token = _ici_start(
        sums, land, after, _rs_send_view, _rs_recv_view, name="rs_start_" + tag)
    return dict(sems=(send_sems, recv_sems), sums=s_thru, land=d_thru, owns=owns, core=core, tag=tag), token


def _rs_end(state, after):
    tag = state["tag"]
    got = _ici_wait(*state["sems"], state["sums"], state["land"], after, _rs_send_view, _rs_recv_view,
                    name="rs_wait_" + tag)
    tots = [_chip_sum(o, gt, state["core"], name=f"rs_chip_{tag}_{t}")
            for t, (o, gt) in enumerate(zip(state["owns"], got))]
    full = _join_halves(tots, name="rs_join_" + tag)
    return [f.reshape(2 * f.shape[1], f.shape[2]) for f in full]


def _rope_lane_table():
    d = jnp.arange(128) % HEAD
    inv_freq = ROPE_THETA ** (-jnp.arange(0, ROT, 2, dtype=F32) / ROT)
    rot = d < ROT
    rows = [jnp.where(rot, inv_freq[d % (ROT // 2)], 0.0), rot.astype(F32),
            (d < ROT // 2).astype(F32), jnp.logical_and(d >= ROT // 2, rot).astype(F32)]
    return jnp.concatenate([jnp.stack(rows), jnp.zeros((4, 128), F32)], axis=0)


def _pad8(rows):
    return jnp.concatenate([rows, jnp.zeros((8 - rows.shape[0], rows.shape[1]), F32)], axis=0)


def kernel(x, c, positions, ada_w, ada_b, w_in, b_in, sinks, pool_w, pool_scale, w_out, w_gate, w_up, w_down, g_pre_mix, g_post_mix, g_pre_ffn, g_post_ffn, loss_target, m_ada_w, m_ada_b, m_w_in, m_b_in, m_sinks, m_pool_w, m_pool_scale, m_w_out, m_w_gate, m_w_up, m_w_down, m_g_pre_mix, m_g_post_mix, m_g_pre_ffn, m_g_post_ffn, v_ada_w, v_ada_b, v_w_in, v_b_in, v_sinks, v_pool_w, v_pool_scale, v_w_out, v_w_gate, v_w_up, v_w_down, v_g_pre_mix, v_g_post_mix, v_g_pre_ffn, v_g_post_ffn):
    T = x.shape[1]
    n_layers = ada_w.shape[0]
    ax, ay, ac = _coords()
    my_dev = 4 * ax + 2 * ay + ac
    my_chip = 2 * ax + ay
    x0 = x.reshape(T, D_MODEL)
    target = loss_target.reshape(T, D_MODEL)

    c_all = _allgather8(c.reshape(8, 128), name="ag_c").reshape(N_DEV, D_MODEL)
    ada_b_sh = lax.dynamic_slice_in_dim(ada_b, my_chip * ADA_SH, ADA_SH, axis=1).reshape(n_layers, 1, ADA_SH)
    mod_part = _mod_fwd(c_all, ada_w, ada_b_sh)
    mod_all = _allgather8(mod_part.reshape(n_layers * 8, ADA_SH), name="ag_mod")
    mod_all = mod_all.reshape(N_DEV, n_layers, 8, ADA_SH)[0::2]
    mod_mine = lax.dynamic_index_in_dim(mod_all, my_dev, axis=2, keepdims=False)
    mod = jnp.transpose(mod_mine, (1, 0, 2)).reshape(n_layers, 6, D_MODEL)

    pos_b = jnp.broadcast_to(positions.reshape(T, 1), (T, 128))
    rc, rs1, rs2 = _rope_tables(pos_b, _rope_lane_table())

    chip1 = jnp.reshape(my_chip, (1,)).astype(jnp.int32)

    def tr(t):
        return jnp.transpose(t, (0, 2, 1))

    w_in_t, w_gate_t, w_up_t = tr(w_in), tr(w_gate), tr(w_up)

    def cast_layer(l):
        return [_cast_slot(w[l], chip1, name=f"cast_{nm}{l}")
                for nm, w in (("w_in", w_in_t), ("w_out", w_out), ("w_gate", w_gate_t), ("w_up", w_up_t),
                              ("w_down", w_down))]

    def as_operands(bufs):
        gin, gout, gg, gu, gd = bufs
        return gin.reshape(IN_W, D_MODEL), gout.reshape(D_MODEL, D_MODEL), gg, gu, gd

    bufs0 = cast_layer(0)
    win0 = _allgather_weights(bufs0[:1], name="ag_w0_in")
    rest_send, rest_recv, rest_bufs, _, ag_token = _ici_start(
        bufs0[1:], None, win0[0], _ag_send_view, _ag_recv_view, name="ag_start_0")
    weights = [None] * n_layers

    saved = []
    xl = x0
    for l in range(n_layers):
        mod8 = _pad8(mod[l])
        if l + 1 < n_layers:
            ag_send, ag_recv, ag_bufs, _, ag_token = _ici_start(
                cast_layer(l + 1), None, ag_token, _ag_send_view, _ag_recv_view, name=f"ag_start_{l + 1}")
        if l == 0 or l + 1 < n_layers:
            mod8 = mod8 + ag_token[0, 0]
        g8 = _pad8(jnp.stack([g_pre_mix[l], g_post_mix[l], g_pre_ffn[l], g_post_ffn[l]]))
        sink_b = jnp.broadcast_to(sinks[l][:, None], (N_HEADS, 128))
        psc = pool_scale[l].reshape(1, POOL_W)
        win = win0[0].reshape(IN_W, D_MODEL) if l == 0 else weights[l][0]
        h, q, k, v, u = _fwd_in(xl, mod8, g8, win, b_in[l].reshape(1, IN_W), rc, rs1, rs2)
        attn, lse = _attn_fwd(q, k, v, sink_b)
        pool, pooled = _pool_fwd(u, pool_w[l], psc)
        if l == 0:
            arrived = _ici_wait(rest_send, rest_recv, rest_bufs, None, pool, _ag_send_view, _ag_recv_view,
                                name="ag_wait_0")
            weights[0] = as_operands(win0 + _ag_forward(arrived, name="ag_fwd_0"))
        win, wout, wg, wu, wd = weights[l]
        mix, x1 = _fwd_out(attn, pool, xl, wout, g8, mod8)
        if l + 1 < n_layers:
            h2, act, ga, gb, f, x2 = _ffn_fwd(x1, mod8, g8, wg, wu, wd)
        else:
            h2, act, ga, gb, f, x2, loss_tile = _ffn_fwd(x1, mod8, g8, wg, wu, wd, target=target)
        saved.append(dict(x=xl, h=h, q=q, k=k, v=v, lse=lse, attn=attn, pool=pool, pooled=pooled, mix=mix,
                          x1=x1, h2=h2, act=act, ga=ga, gb=gb, f=f, mod8=mod8, g8=g8, sink_b=sink_b, psc=psc))
        xl = x2
        if l + 1 < n_layers:
            arrived = _ici_wait(ag_send, ag_recv, ag_bufs, None, x2, _ag_send_view, _ag_recv_view,
                                name=f"ag_wait_{l + 1}")
            weights[l + 1] = as_operands(_ag_forward(arrived, name=f"ag_fwd_{l + 1}"))

    dy = xl
    loss = lax.psum(loss_tile[0, 0], ("x", "y", "c"))

    small = [None] * n_layers
    dmod_rows = [None] * n_layers
    reduced = [dict() for _ in range(n_layers)]
    in_flight = None
    dx = dy
    for l in reversed(range(n_layers)):
        s = saved[l]
        win, wout, wg, wu, wd = weights[l]
        if in_flight is not None:
            s = dict(s, mod8=s["mod8"] + in_flight[1][0, 0])
        dx1, df, da, db, red_f = _ffn_bwd(dx, s["f"], s["ga"], s["gb"], s["x1"], s["mod8"], s["g8"], wg, wu, wd)
        g_wd = _wgrad_rows(s["act"], df, name="wgrad_down")
        g_wg = _wgrad_rows(da, s["h2"], name="wgrad_gate")
        g_wu = _wgrad_rows(db, s["h2"], name="wgrad_up")
        if in_flight is not None:
            got = _rs_end(in_flight[0], g_wu)
            reduced[l + 1].update(w_in=got[0], w_out=got[1])
        ffn_flight = _rs_begin([g_wg, g_wu, g_wd], g_wu, tag=f"{l}f")
        s = dict(s, mod8=s["mod8"] + ffn_flight[1][0, 0])
        dmix, dattn, dpool, red_c = _mix_bwd(dx1, s["mix"], s["mod8"], s["g8"], wout)
        g_wout = jnp.concatenate([_wgrad(s["attn"], dmix, name="wgrad_out_a"),
                                  _wgrad(s["pool"], dmix, name="wgrad_out_p")], axis=0)
        dq, dk, dv, dsink = _attn_bwd(s["q"], s["k"], s["v"], s["lse"], dattn, s["sink_b"])
        du, g_poolw, dpsc = _pool_bwd(dpool, s["pooled"], pool_w[l], s["psc"])
        dx, dproj, red_d, dbin = _in_bwd(dq, dk, dv, du, rc, rs1, rs2, s["x"], dx1, s["mod8"], s["g8"], win)
        g_win = _wgrad(dproj, s["h"], name="wgrad_in")
        g_win_sh = g_win.reshape(N_SHARD, IN_SH, D_MODEL)
        got = _rs_end(ffn_flight[0], g_win)
        reduced[l].update(w_gate=got[0], w_up=got[1], w_down=got[2])
        in_flight = _rs_begin([g_win_sh, g_wout.reshape(N_SHARD, OUT_SH, D_MODEL)], g_win, tag=f"{l}a")
        dmod_rows[l] = jnp.concatenate([red_d[0], red_d[1], red_c[0], red_f[2], red_f[3], red_f[0]])
        small[l] = jnp.concatenate([red_d[2], red_c[1], red_f[4], red_f[1], dbin[0], dpsc[0], dsink[:, 0],
                                    jnp.zeros((120,), F32), g_poolw.reshape(-1)])
    grad_x = dx.reshape(1, T, D_MODEL)

    per_layer = small[0].shape[0]
    rows_small = n_layers * per_layer // 128
    rows_mod = n_layers * 6 * D_MODEL // 128
    rows_pad = -(rows_small + rows_mod) % 8
    pack = jnp.concatenate(small + dmod_rows + [jnp.zeros((rows_pad * 128,), F32)]).reshape(-1, 128)
    pack = pack + in_flight[1][0, 0]
    gathered = _allgather8(pack, name="ag_small").reshape(N_DEV, pack.shape[0], 128)
    summed = _sum_devices(gathered)
    small_sum = summed[:rows_small].reshape(n_layers, per_layer)
    o = 0
    small_g = {}
    for nm, width in (("g_pre_mix", D_MODEL), ("g_post_mix", D_MODEL), ("g_pre_ffn", D_MODEL),
                      ("g_post_ffn", D_MODEL), ("b_in", IN_W), ("pool_scale", POOL_W), ("sinks", 128),
                      ("pool_w", 4 * 128 * 128)):
        small_g[nm] = small_sum[:, o:o + width]
        o += width
    small_g["sinks"] = small_g["sinks"][:, :N_HEADS]
    small_g["pool_w"] = small_g["pool_w"].reshape(n_layers, 4, 128, 128)
    small_g["ada_b"] = summed[rows_small:rows_small + rows_mod].reshape(n_layers, 6 * D_MODEL)
    dmod_all = gathered[:, rows_small:rows_small + rows_mod].reshape(N_DEV, n_layers, N_SHARD, ADA_SH)
    dmod_sh = lax.dynamic_index_in_dim(dmod_all, my_chip, axis=2, keepdims=False)
    g_ada_w = _ada_wgrad(jnp.transpose(c_all), jnp.transpose(dmod_sh, (1, 0, 2)))

    grads = dict(ada_w=g_ada_w, ada_b=small_g["ada_b"], b_in=small_g["b_in"], sinks=small_g["sinks"],
                 pool_w=small_g["pool_w"], pool_scale=small_g["pool_scale"], g_pre_mix=small_g["g_pre_mix"],
                 g_post_mix=small_g["g_post_mix"], g_pre_ffn=small_g["g_pre_ffn"], g_post_ffn=small_g["g_post_ffn"])
    params = dict(ada_w=(ada_w, m_ada_w, v_ada_w), ada_b=(ada_b, m_ada_b, v_ada_b), w_in=(w_in, m_w_in, v_w_in),
                  b_in=(b_in, m_b_in, v_b_in), sinks=(sinks, m_sinks, v_sinks), pool_w=(pool_w, m_pool_w, v_pool_w),
                  pool_scale=(pool_scale, m_pool_scale, v_pool_scale), w_out=(w_out, m_w_out, v_w_out),
                  w_gate=(w_gate, m_w_gate, v_w_gate), w_up=(w_up, m_w_up, v_w_up),
                  w_down=(w_down, m_w_down, v_w_down), g_pre_mix=(g_pre_mix, m_g_pre_mix, v_g_pre_mix),
                  g_post_mix=(g_post_mix, m_g_post_mix, v_g_post_mix), g_pre_ffn=(g_pre_ffn, m_g_pre_ffn, v_g_pre_ffn),
                  g_post_ffn=(g_post_ffn, m_g_post_ffn, v_g_post_ffn))
    names = list(params)
    updates = {nm: _adamw_nd(*params[nm][:1], grads[nm], *params[nm][1:], name="adamw_" + nm) for nm in grads}

    got = _rs_end(in_flight[0], updates["ada_w"][0])
    reduced[0].update(w_in=got[0], w_out=got[1])
    for nm in ("w_in", "w_out", "w_gate", "w_up", "w_down"):
        g = jnp.stack([reduced[l][nm] for l in range(n_layers)])
        if nm in ("w_in", "w_gate", "w_up"):
            upd = _adamw_nd(tr(params[nm][0]), g, tr(params[nm][1]), tr(params[nm][2]), name="adamw_" + nm)
            grads[nm], updates[nm] = tr(g), [tr(u) for u in upd]
        else:
            grads[nm], updates[nm] = g, _adamw_nd(params[nm][0], g, *params[nm][1:], name="adamw_" + nm)
    return (loss, grad_x, *[grads[nm] for nm in names], *[updates[nm][0] for nm in names],
            *[updates[nm][1] for nm in names], *[updates[nm][2] for nm in names])
```

```python
import functools

import jax
import jax.numpy as jnp
from jax import lax
from jax.experimental import pallas as pl
from jax.experimental.pallas import tpu as pltpu

F32 = jnp.float32
BF16 = jnp.bfloat16
MESH = pl.DeviceIdType.MESH

D_MODEL = 1024
ATTN_W = 512
KV_W = 128
KVD_W = 256
POOL_W = 512
IN_W = 1280
D_FF = 2816
N_SHARD = 4
FF_SH = D_FF // N_SHARD
IN_SH = IN_W // N_SHARD
OUT_SH = D_MODEL // N_SHARD
ADA_SH = 6 * D_MODEL // N_SHARD
HEAD = 64
N_HEADS = 8
GROUP = 4
BLK = 128
POOL_WINDOWS = (2, 4, 8, 16)
HALO = 16
ROT = 16
ROPE_THETA = 500000.0
EPS = 1e-6
NEG_INF = -1e30
N_DEV = 8

ADAM_LR = 0.001
ADAM_B1 = 0.9
ADAM_B2 = 0.999
ADAM_EPS = 1e-08
ADAM_WD = 0.01
ADAM_STEP = 10

VMEM_LIMIT = 48 * 1024 * 1024
FFN_VMEM_LIMIT = 60 * 1024 * 1024
WGRAD_TOKENS = 2048


def _cp(*sem, vmem=VMEM_LIMIT):
    return pltpu.CompilerParams(dimension_semantics=sem, vmem_limit_bytes=vmem)


def _full(shape):
    nd = len(shape)
    return pl.BlockSpec(shape, lambda *_: (0,) * nd)


def _resident(shape):
    nd = len(shape)
    return pl.BlockSpec(shape, lambda *_: (0,) * nd, pipeline_mode=pl.Buffered(1))


def _rows(tm, ncol):
    return pl.BlockSpec((tm, ncol), lambda i: (i, 0))


def _sds(shape, dtype):
    return jax.ShapeDtypeStruct(shape, dtype)


def _nt(a, b):
    return lax.dot_general(a, b, (((1,), (1,)), ((), ())), preferred_element_type=F32)


def _tn(a, b):
    return lax.dot_general(a, b, (((0,), (0,)), ((), ())), preferred_element_type=F32)


def _mm(a, b):
    return jnp.dot(a, b, preferred_element_type=F32)


def _rstd(x):
    return lax.rsqrt(jnp.mean(x * x, axis=-1, keepdims=True) + EPS)


def _colsum(x):
    return jnp.sum(x, axis=0, keepdims=True)


def _norm_gain_bwd(dy, xhat, rstd, gain):
    p = dy * xhat
    dx = rstd * (dy * gain - xhat * jnp.mean(p * gain, axis=-1, keepdims=True))
    return dx, _colsum(p)


def _rope_tables(pos_b, lane_tab):
    T = pos_b.shape[0]
    tm = min(T, 1024)

    def body(pos_ref, tab_ref, c_ref, s1_ref, s2_ref):
        ang = pos_ref[...].astype(F32) * tab_ref[0:1, :]
        cs = jnp.cos(ang)
        sn = jnp.sin(ang)
        m_rot = tab_ref[1:2, :]
        c_ref[...] = cs * m_rot + (1.0 - m_rot)
        s1_ref[...] = -sn * tab_ref[2:3, :]
        s2_ref[...] = sn * tab_ref[3:4, :]

    out = _sds((T, 128), F32)
    return pl.pallas_call(
        body, name="rope_tables", grid=(T // tm,),
        in_specs=[_rows(tm, 128), _full((8, 128))],
        out_specs=[_rows(tm, 128)] * 3, out_shape=[out] * 3,
        compiler_params=_cp("parallel"),
    )(pos_b, lane_tab)


def _rot_fwd(t, c, s1, s2):
    w = t.shape[-1]
    return t * c + pltpu.roll(t, w - 8, 1) * s1 + pltpu.roll(t, 8, 1) * s2


def _rot_bwd(d, c, s1, s2):
    w = d.shape[-1]
    return d * c + pltpu.roll(d * s1, 8, 1) + pltpu.roll(d * s2, w - 8, 1)


def _store_dup(ref, t):
    low = lax.broadcasted_iota(jnp.int32, t.shape, 1) < HEAD
    sw = pltpu.roll(t, HEAD, 1)
    ref[:, 0:128] = jnp.where(low, t, sw).astype(BF16)
    ref[:, 128:256] = jnp.where(low, sw, t).astype(BF16)


def _fold_dup(d):
    low = lax.broadcasted_iota(jnp.int32, (d.shape[0], 128), 1) < HEAD
    d0 = d[:, 0:128]
    d1 = d[:, 128:256]
    return jnp.where(low, d0 + pltpu.roll(d0, HEAD, 1), d1 + pltpu.roll(d1, HEAD, 1))


def _fwd_in(x, mod8, g8, w_in, b_in, rc, rs1, rs2):
    T = x.shape[0]
    tm = min(T, 512)

    def body(x_ref, mod_ref, g_ref, w_ref, b_ref, c_ref, s1_ref, s2_ref,
             h_ref, q_ref, k_ref, v_ref, u_ref):
        xf = x_ref[...]
        h = (xf * _rstd(xf) * g_ref[0:1, :]) * (1.0 + mod_ref[1:2, :]) + mod_ref[0:1, :]
        hb = h.astype(BF16)
        h_ref[...] = hb
        c = c_ref[...]
        s1 = s1_ref[...]
        s2 = s2_ref[...]
        q = _nt(hb, w_ref[0:ATTN_W, :]) + b_ref[:, 0:ATTN_W]
        q = _rot_fwd(q, jnp.tile(c, (1, 4)), jnp.tile(s1, (1, 4)), jnp.tile(s2, (1, 4)))
        q_ref[...] = (q * (HEAD ** -0.5)).astype(BF16)
        k = _nt(hb, w_ref[ATTN_W:ATTN_W + KV_W, :]) + b_ref[:, ATTN_W:ATTN_W + KV_W]
        _store_dup(k_ref, _rot_fwd(k, c, s1, s2))
        v = _nt(hb, w_ref[ATTN_W + KV_W:ATTN_W + 2 * KV_W, :]) + b_ref[:, ATTN_W + KV_W:ATTN_W + 2 * KV_W]
        _store_dup(v_ref, v)
        u_ref[...] = _nt(hb, w_ref[ATTN_W + 2 * KV_W:IN_W, :]) + b_ref[:, ATTN_W + 2 * KV_W:IN_W]

    return pl.pallas_call(
        body, name="fwd_in", grid=(T // tm,),
        in_specs=[_rows(tm, D_MODEL), _full((8, D_MODEL)), _full((8, D_MODEL)),
                  _resident((IN_W, D_MODEL)), _full((1, IN_W)),
                  _rows(tm, 128), _rows(tm, 128), _rows(tm, 128)],
        out_specs=[_rows(tm, D_MODEL), _rows(tm, ATTN_W), _rows(tm, KVD_W), _rows(tm, KVD_W), _rows(tm, POOL_W)],
        out_shape=[_sds((T, D_MODEL), BF16), _sds((T, ATTN_W), BF16), _sds((T, KVD_W), BF16),
                   _sds((T, KVD_W), BF16), _sds((T, POOL_W), F32)],
        compiler_params=_cp("parallel"),
    )(x, mod8, g8, w_in, b_in, rc, rs1, rs2)


def _band_mask(n):
    kk = lax.broadcasted_iota(jnp.int32, (2 * BLK, BLK), 0)
    qi = lax.broadcasted_iota(jnp.int32, (2 * BLK, BLK), 1)
    first = jnp.where(n > 0, 0, 2 * BLK)
    in_prev = jnp.logical_and(kk < BLK, kk > qi + first)
    in_cur = jnp.logical_and(kk >= BLK, (kk - BLK) <= qi)
    one = jnp.logical_or(in_prev, in_cur)
    return jnp.concatenate([one] * GROUP, axis=1)


def _head_row(ref, j):
    return jnp.concatenate([ref[GROUP * j + r:GROUP * j + r + 1, :] for r in range(GROUP)], axis=1)


def _stack_heads(x_ref, j):
    low = lax.broadcasted_iota(jnp.int32, (BLK, 128), 1) < HEAD
    parts = []
    for gp in (2 * j, 2 * j + 1):
        x2 = x_ref[:, gp * 128:(gp + 1) * 128]
        parts.append(jnp.where(low, x2, jnp.zeros_like(x2)))
        parts.append(jnp.where(low, jnp.zeros_like(x2), x2))
    return jnp.concatenate(parts, axis=0)


def _unstack_heads(o):
    low = lax.broadcasted_iota(jnp.int32, (BLK, 128), 1) < HEAD
    return [jnp.where(low, o[0:BLK], o[BLK:2 * BLK]), jnp.where(low, o[2 * BLK:3 * BLK], o[3 * BLK:4 * BLK])]


def _attn_fwd(q, kd, vd, sink_b):
    T = q.shape[0]
    nb = T // BLK

    def body(q_ref, kp_ref, kc_ref, vp_ref, vc_ref, sk_ref, o_ref, lse_ref):
        valid = _band_mask(pl.program_id(0))
        for j in range(N_HEADS // GROUP):
            lanes = slice(j * 128, (j + 1) * 128)
            kcat = jnp.concatenate([kp_ref[:, lanes], kc_ref[:, lanes]], axis=0)
            vcat = jnp.concatenate([vp_ref[:, lanes], vc_ref[:, lanes]], axis=0)
            s = jnp.where(valid, _nt(kcat, _stack_heads(q_ref, j)), NEG_INF)
            sk = _head_row(sk_ref, j)
            m = jnp.maximum(jnp.max(s, axis=0, keepdims=True), sk)
            p = jnp.exp(s - m)
            den = jnp.sum(p, axis=0, keepdims=True) + jnp.exp(sk - m)
            p = p * (1.0 / den)
            o = _tn(p.astype(BF16), vcat)
            o_ref[:, 2 * j * 128:(2 * j + 2) * 128] = jnp.concatenate(_unstack_heads(o), axis=1).astype(BF16)
            lse = m + jnp.log(den)
            for r in range(GROUP):
                lse_ref[GROUP * j + r:GROUP * j + r + 1, :] = lse[:, r * 128:(r + 1) * 128]

    prev = lambda n: (jnp.maximum(n - 1, 0), 0)
    cur = lambda n: (n, 0)
    return pl.pallas_call(
        body, name="attn_fwd", grid=(nb,),
        in_specs=[pl.BlockSpec((BLK, ATTN_W), cur),
                  pl.BlockSpec((BLK, KVD_W), prev), pl.BlockSpec((BLK, KVD_W), cur),
                  pl.BlockSpec((BLK, KVD_W), prev), pl.BlockSpec((BLK, KVD_W), cur),
                  _full((8, 128))],
        out_specs=[pl.BlockSpec((BLK, ATTN_W), cur), pl.BlockSpec((N_HEADS, 128), cur)],
        out_shape=[_sds((T, ATTN_W), BF16), _sds((nb * N_HEADS, 128), F32)],
        compiler_params=_cp("parallel"),
    )(q, kd, kd, vd, vd, sink_b)


def _pool_fwd(u, pool_w, pool_scale):
    T = u.shape[0]
    tm = min(T, 512)

    def body(u_ref, w_ref, sc_ref, out_ref, pooled_ref, halo):
        i = pl.program_id(0)

        @pl.when(i == 0)
        def _():
            halo[...] = jnp.zeros_like(halo)

        ub = u_ref[...]
        ext = jnp.concatenate([halo[...], ub], axis=0)
        halo[...] = ub[tm - HALO:, :]
        tpos = (i * tm + lax.broadcasted_iota(jnp.int32, (tm, 1), 0)).astype(F32)
        for g, w in enumerate(POOL_WINDOWS):
            lanes = slice(g * 128, (g + 1) * 128)
            s = ext[:, lanes]
            sh = 1
            while sh < w:
                s = s + pltpu.roll(s, sh, 0)
                sh *= 2
            cnt = jnp.minimum(tpos + 1.0, float(w))
            pb = (s[HALO:, :] / cnt - ub[:, lanes]).astype(BF16)
            z = _mm(pb, w_ref[g].astype(BF16))
            out_ref[:, lanes] = (z * sc_ref[:, lanes]).astype(BF16)
            pooled_ref[:, lanes] = pb

    return pl.pallas_call(
        body, name="pool_fwd", grid=(T // tm,),
        in_specs=[_rows(tm, POOL_W), _full((4, 128, 128)), _full((1, POOL_W))],
        out_specs=[_rows(tm, POOL_W), _rows(tm, POOL_W)],
        out_shape=[_sds((T, POOL_W), BF16), _sds((T, POOL_W), BF16)],
        scratch_shapes=[pltpu.VMEM((HALO, POOL_W), F32)],
        compiler_params=_cp("arbitrary"),
    )(u, pool_w, pool_scale)


def _fwd_out(attn, pool, x, w_out, g8, mod8):
    T = x.shape[0]
    tm = min(T, 512)

    def body(a_ref, p_ref, x_ref, w_ref, g_ref, mod_ref, mix_ref, x1_ref):
        mix = _mm(a_ref[...], w_ref[0:ATTN_W, :]) + _mm(p_ref[...], w_ref[ATTN_W:, :])
        mix_ref[...] = mix
        x1_ref[...] = x_ref[...] + mod_ref[2:3, :] * (mix * _rstd(mix) * g_ref[1:2, :])

    return pl.pallas_call(
        body, name="fwd_out", grid=(T // tm,),
        in_specs=[_rows(tm, ATTN_W), _rows(tm, POOL_W), _rows(tm, D_MODEL),
                  _resident((D_MODEL, D_MODEL)), _full((8, D_MODEL)), _full((8, D_MODEL))],
        out_specs=[_rows(tm, D_MODEL), _rows(tm, D_MODEL)],
        out_shape=[_sds((T, D_MODEL), F32), _sds((T, D_MODEL), F32)],
        compiler_params=_cp("parallel"),
    )(attn, pool, x, w_out, g8, mod8)


def _sh_rows(tm):
    return pl.BlockSpec((N_SHARD, tm, FF_SH), lambda i: (0, i, 0))


def _ffn_fwd(x1, mod8, g8, wg, wu, wd, target=None):
    T = x1.shape[0]
    tm = min(T, 512)
    last = target is not None

    def body(*refs):
        x_ref, mod_ref, g_ref, wg_ref, wu_ref, wd_ref = refs[:6]
        t_ref = refs[6] if last else None
        h_ref, act_ref, ga_ref, gb_ref, f_ref, x2_ref = refs[6 + last:12 + last]
        xf = x_ref[...]
        h = (xf * _rstd(xf) * g_ref[2:3, :]) * (1.0 + mod_ref[4:5, :]) + mod_ref[3:4, :]
        hb = h.astype(BF16)
        h_ref[...] = hb
        f = jnp.zeros((tm, D_MODEL), F32)
        for s in range(N_SHARD):
            a = _nt(hb, wg_ref[s])
            b = _nt(hb, wu_ref[s])
            sig = jax.nn.sigmoid(a)
            sl = a * sig
            act = (sl * b).astype(BF16)
            act_ref[s] = act
            ga_ref[s] = (b * (sig * (1.0 + a * (1.0 - sig)))).astype(BF16)
            gb_ref[s] = sl.astype(BF16)
            f = f + _mm(act, wd_ref[s])
        f_ref[...] = f
        x2 = xf + mod_ref[5:6, :] * (f * _rstd(f) * g_ref[3:4, :])
        if not last:
            x2_ref[...] = x2
        else:
            loss_ref = refs[13]

            @pl.when(pl.program_id(0) == 0)
            def _():
                loss_ref[...] = jnp.zeros_like(loss_ref)

            e = x2 - t_ref[...]
            x2_ref[...] = e * (1.0 / D_MODEL)
            loss_ref[...] += 0.5 * jnp.sum(jnp.mean(e * e, axis=-1, keepdims=True), axis=0, keepdims=True)

    act_shape = _sds((N_SHARD, T, FF_SH), BF16)
    weights = [_resident((N_SHARD, FF_SH, D_MODEL))] * 3
    return pl.pallas_call(
        body, name="ffn_fwd_loss" if last else "ffn_fwd", grid=(T // tm,),
        in_specs=[_rows(tm, D_MODEL), _full((8, D_MODEL)), _full((8, D_MODEL)), *weights]
        + ([_rows(tm, D_MODEL)] if last else []),
        out_specs=[_rows(tm, D_MODEL), _sh_rows(tm), _sh_rows(tm), _sh_rows(tm), _rows(tm, D_MODEL),
                   _rows(tm, D_MODEL)] + ([_full((8, 128))] if last else []),
        out_shape=[_sds((T, D_MODEL), BF16), act_shape, act_shape, act_shape, _sds((T, D_MODEL), F32),
                   _sds((T, D_MODEL), F32)] + ([_sds((8, 128), F32)] if last else []),
        compiler_params=_cp("arbitrary" if last else "parallel", vmem=FFN_VMEM_LIMIT),
    )(x1, mod8, g8, wg, wu, wd, *([target] if last else []))


def _ffn_bwd(dx2, f, ga, gb, x1, mod8, g8, wg, wu, wd):
    T = dx2.shape[0]
    tm = min(T, 256)

    def body(dx_ref, f_ref, ga_ref, gb_ref, x_ref, mod_ref, g_ref, wg_ref, wu_ref, wd_ref,
             dx1_ref, df_ref, da_ref, db_ref, red_ref):
        @pl.when(pl.program_id(0) == 0)
        def _():
            red_ref[...] = jnp.zeros_like(red_ref)

        dx = dx_ref[...]
        fv = f_ref[...]
        rstd = _rstd(fv)
        fhat = fv * rstd
        gpost = g_ref[3:4, :]
        gate = mod_ref[5:6, :]
        df, s_post = _norm_gain_bwd(dx, fhat, rstd, gate * gpost)
        red_ref[0:1, :] += gpost * s_post
        red_ref[1:2, :] += gate * s_post
        dfb = df.astype(BF16)
        df_ref[...] = dfb
        dh = jnp.zeros((tm, D_MODEL), F32)
        for s in range(N_SHARD):
            dact = _nt(dfb, wd_ref[s])
            da = (dact * ga_ref[s].astype(F32)).astype(BF16)
            db = (dact * gb_ref[s].astype(F32)).astype(BF16)
            da_ref[s] = da
            db_ref[s] = db
            dh = dh + _mm(da, wg_ref[s]) + _mm(db, wu_ref[s])
        xf = x_ref[...]
        rstd1 = _rstd(xf)
        xhat = xf * rstd1
        gpre = g_ref[2:3, :]
        scale1 = 1.0 + mod_ref[4:5, :]
        dxn, s_pre = _norm_gain_bwd(dh, xhat, rstd1, scale1 * gpre)
        red_ref[2:3, :] += _colsum(dh)
        red_ref[3:4, :] += gpre * s_pre
        red_ref[4:5, :] += scale1 * s_pre
        dx1_ref[...] = dx + dxn

    act_shape = _sds((N_SHARD, T, FF_SH), BF16)
    return pl.pallas_call(
        body, name="ffn_bwd", grid=(T // tm,),
        in_specs=[_rows(tm, D_MODEL), _rows(tm, D_MODEL), _sh_rows(tm), _sh_rows(tm), _rows(tm, D_MODEL),
                  _full((8, D_MODEL)), _full((8, D_MODEL)),
                  _resident((N_SHARD, FF_SH, D_MODEL)), _resident((N_SHARD, FF_SH, D_MODEL)),
                  _resident((N_SHARD, FF_SH, D_MODEL))],
        out_specs=[_rows(tm, D_MODEL), _rows(tm, D_MODEL), _sh_rows(tm), _sh_rows(tm), _full((8, D_MODEL))],
        out_shape=[_sds((T, D_MODEL), F32), _sds((T, D_MODEL), BF16), act_shape, act_shape, _sds((8, D_MODEL), F32)],
        compiler_params=_cp("arbitrary"),
    )(dx2, f, ga, gb, x1, mod8, g8, wg, wu, wd)


def _wgrad(a, b, name):
    T, K = a.shape
    N = b.shape[1]
    tt = min(T, WGRAD_TOKENS)
    tk = next(c for c in (640, 512, 256, 128) if K % c == 0)

    def body(a_ref, b_ref, o_ref):
        @pl.when(pl.program_id(1) == 0)
        def _():
            o_ref[...] = jnp.zeros_like(o_ref)

        o_ref[...] += _tn(a_ref[...], b_ref[...])

    return pl.pallas_call(
        body, name=name, grid=(K // tk, T // tt),
        in_specs=[pl.BlockSpec((tt, tk), lambda i, t: (t, i)), pl.BlockSpec((tt, N), lambda i, t: (t, 0))],
        out_specs=pl.BlockSpec((tk, N), lambda i, t: (i, 0)),
        out_shape=_sds((K, N), F32),
        compiler_params=_cp("parallel", "arbitrary"),
    )(a, b)


def _wgrad_rows(a, b, name):
    T, N = b.shape
    k = a.shape[2]
    tt = min(T, WGRAD_TOKENS)

    def body(a_ref, b_ref, o_ref):
        @pl.when(pl.program_id(1) == 0)
        def _():
            o_ref[...] = jnp.zeros_like(o_ref)

        o_ref[...] += _tn(a_ref[...], b_ref[...])

    return pl.pallas_call(
        body, name=name, grid=(N_SHARD, T // tt),
        in_specs=[pl.BlockSpec((None, tt, k), lambda s, t: (s, t, 0)), pl.BlockSpec((tt, N), lambda s, t: (t, 0))],
        out_specs=pl.BlockSpec((None, k, N), lambda s, t: (s, 0, 0)),
        out_shape=_sds((N_SHARD, k, N), F32),
        compiler_params=_cp("parallel", "arbitrary"),
    )(a, b)


def _mix_bwd(dx1, mix, mod8, g8, w_out):
    T = dx1.shape[0]
    tm = min(T, 512)

    def body(dx_ref, mix_ref, mod_ref, g_ref, w_ref, dmix_ref, da_ref, dp_ref, red_ref):
        @pl.when(pl.program_id(0) == 0)
        def _():
            red_ref[...] = jnp.zeros_like(red_ref)

        dx = dx_ref[...]
        mv = mix_ref[...]
        rstd = _rstd(mv)
        mhat = mv * rstd
        gpost = g_ref[1:2, :]
        gate = mod_ref[2:3, :]
        dm, s_post = _norm_gain_bwd(dx, mhat, rstd, gate * gpost)
        red_ref[0:1, :] += gpost * s_post
        red_ref[1:2, :] += gate * s_post
        dmb = dm.astype(BF16)
        dmix_ref[...] = dmb
        da_ref[...] = _nt(dmb, w_ref[0:ATTN_W, :]).astype(BF16)
        dp_ref[...] = _nt(dmb, w_ref[ATTN_W:, :]).astype(BF16)

    return pl.pallas_call(
        body, name="mix_bwd", grid=(T // tm,),
        in_specs=[_rows(tm, D_MODEL), _rows(tm, D_MODEL), _full((8, D_MODEL)), _full((8, D_MODEL)),
                  _resident((D_MODEL, D_MODEL))],
        out_specs=[_rows(tm, D_MODEL), _rows(tm, ATTN_W), _rows(tm, POOL_W), _full((8, D_MODEL))],
        out_shape=[_sds((T, D_MODEL), BF16), _sds((T, ATTN_W), BF16), _sds((T, POOL_W), BF16),
                   _sds((8, D_MODEL), F32)],
        compiler_params=_cp("arbitrary"),
    )(dx1, mix, mod8, g8, w_out)


def _attn_bwd(q, kd, vd, lse, dattn, sink_b):
    T = q.shape[0]
    nb = T // BLK

    def body(q_ref, do_ref, lse_ref, kp_ref, kc_ref, vp_ref, vc_ref, sk_ref,
             dq_ref, dk_ref, dv_ref, dsk_ref, carry_k, carry_v):
        n = pl.program_id(0)

        @pl.when(n == 0)
        def _():
            carry_k[...] = jnp.zeros_like(carry_k)
            carry_v[...] = jnp.zeros_like(carry_v)
            dsk_ref[...] = jnp.zeros_like(dsk_ref)

        @pl.when(n < nb)
        def _():
            valid = _band_mask(n)
            for j in range(N_HEADS // GROUP):
                lanes = slice(j * 128, (j + 1) * 128)
                kcat = jnp.concatenate([kp_ref[:, lanes], kc_ref[:, lanes]], axis=0)
                vcat = jnp.concatenate([vp_ref[:, lanes], vc_ref[:, lanes]], axis=0)
                qs = _stack_heads(q_ref, j)
                dos = _stack_heads(do_ref, j)
                lse = _head_row(lse_ref, j)
                p = jnp.exp(jnp.where(valid, _nt(kcat, qs), NEG_INF) - lse)
                dp = _nt(vcat, dos)
                delta = jnp.sum(p * dp, axis=0, keepdims=True)
                ds = (p * (dp - delta)).astype(BF16)
                sink_term = jnp.exp(_head_row(sk_ref, j) - lse) * delta
                for r in range(GROUP):
                    h = GROUP * j + r
                    dsk_ref[h:h + 1, :] += -jnp.sum(sink_term[:, r * 128:(r + 1) * 128], axis=1, keepdims=True)
                dq_ref[:, 2 * j * 128:(2 * j + 2) * 128] = jnp.concatenate(_unstack_heads(_tn(ds, kcat)), axis=1)
                dk = _mm(ds, qs)
                dv = _mm(p.astype(BF16), dos)
                dk_ref[:, lanes] = carry_k[:, lanes] + dk[0:BLK]
                dv_ref[:, lanes] = carry_v[:, lanes] + dv[0:BLK]
                carry_k[:, lanes] = dk[BLK:]
                carry_v[:, lanes] = dv[BLK:]

        @pl.when(n == nb)
        def _():
            dk_ref[...] = carry_k[...]
            dv_ref[...] = carry_v[...]

    cur = lambda n: (jnp.minimum(n, nb - 1), 0)
    prev = lambda n: (jnp.maximum(n - 1, 0), 0)
    return pl.pallas_call(
        body, name="attn_bwd", grid=(nb + 1,),
        in_specs=[pl.BlockSpec((BLK, ATTN_W), cur), pl.BlockSpec((BLK, ATTN_W), cur), pl.BlockSpec((N_HEADS, 128), cur),
                  pl.BlockSpec((BLK, KVD_W), prev), pl.BlockSpec((BLK, KVD_W), cur),
                  pl.BlockSpec((BLK, KVD_W), prev), pl.BlockSpec((BLK, KVD_W), cur),
                  _full((8, 128))],
        out_specs=[pl.BlockSpec((BLK, ATTN_W), cur), pl.BlockSpec((BLK, KVD_W), prev),
                   pl.BlockSpec((BLK, KVD_W), prev), _full((8, 128))],
        out_shape=[_sds((T, ATTN_W), F32), _sds((T, KVD_W), F32), _sds((T, KVD_W), F32), _sds((8, 128), F32)],
        scratch_shapes=[pltpu.VMEM((BLK, KVD_W), F32), pltpu.VMEM((BLK, KVD_W), F32)],
        compiler_params=_cp("arbitrary"),
    )(q, dattn, lse, kd, kd, vd, vd, sink_b)


def _pool_bwd(dpool, pooled, pool_w, pool_scale):
    T = dpool.shape[0]
    tm = min(T, 512)
    nbk = T // tm
    ext_rows = tm + HALO

    def body(dp_ref, pl_ref, w_ref, sc_ref, du_ref, dw_ref, dsc_ref, halo):
        i = pl.program_id(0)

        @pl.when(i == 0)
        def _():
            halo[...] = jnp.zeros_like(halo)
            dw_ref[...] = jnp.zeros_like(dw_ref)
            dsc_ref[...] = jnp.zeros_like(dsc_ref)

        blk = nbk - 1 - i
        tpos = (blk * tm + lax.broadcasted_iota(jnp.int32, (tm, 1), 0)).astype(F32)
        for g, w in enumerate(POOL_WINDOWS):
            lanes = slice(g * 128, (g + 1) * 128)
            dp = dp_ref[:, lanes].astype(F32)
            pb = pl_ref[:, lanes]
            wg = w_ref[g].astype(BF16)
            z = _mm(pb, wg)
            dsc_ref[0:1, lanes] += _colsum(dp * z)
            dz = (dp * sc_ref[:, lanes]).astype(BF16)
            dw_ref[g] += _tn(pb, dz)
            dpl = _nt(dz, wg)
            e = dpl / jnp.minimum(tpos + 1.0, float(w))
            s = jnp.concatenate([e, halo[:, lanes]], axis=0)
            halo[:, lanes] = e[0:HALO, :]
            sh = 1
            while sh < w:
                s = s + pltpu.roll(s, ext_rows - sh, 0)
                sh *= 2
            du_ref[:, lanes] = s[0:tm, :] - dpl

    rev = lambda i: (nbk - 1 - i, 0)
    return pl.pallas_call(
        body, name="pool_bwd", grid=(nbk,),
        in_specs=[pl.BlockSpec((tm, POOL_W), rev), pl.BlockSpec((tm, POOL_W), rev),
                  _full((4, 128, 128)), _full((1, POOL_W))],
        out_specs=[pl.BlockSpec((tm, POOL_W), rev), _full((4, 128, 128)), _full((8, POOL_W))],
        out_shape=[_sds((T, POOL_W), F32), _sds((4, 128, 128), F32), _sds((8, POOL_W), F32)],
        scratch_shapes=[pltpu.VMEM((HALO, POOL_W), F32)],
        compiler_params=_cp("arbitrary"),
    )(dpool, pooled, pool_w, pool_scale)


def _in_bwd(dq, dk, dv, du, rc, rs1, rs2, x, dx1, mod8, g8, w_in):
    T = x.shape[0]
    tm = min(T, 512)

    def body(dq_ref, dk_ref, dv_ref, du_ref, c_ref, s1_ref, s2_ref, x_ref, dx1_ref, mod_ref, g_ref, w_ref,
             dx_ref, dproj_ref, red_ref, dbin_ref):
        @pl.when(pl.program_id(0) == 0)
        def _():
            red_ref[...] = jnp.zeros_like(red_ref)
            dbin_ref[...] = jnp.zeros_like(dbin_ref)

        c = c_ref[...]
        s1 = s1_ref[...]
        s2 = s2_ref[...]
        dqp = _rot_bwd(dq_ref[...] * (HEAD ** -0.5), jnp.tile(c, (1, 4)), jnp.tile(s1, (1, 4)), jnp.tile(s2, (1, 4)))
        dkp = _rot_bwd(_fold_dup(dk_ref[...]), c, s1, s2)
        pieces = ((0, ATTN_W, dqp), (ATTN_W, ATTN_W + KV_W, dkp),
                  (ATTN_W + KV_W, ATTN_W + 2 * KV_W, _fold_dup(dv_ref[...])), (ATTN_W + 2 * KV_W, IN_W, du_ref[...]))
        dh = jnp.zeros((tm, D_MODEL), F32)
        for lo, hi, val in pieces:
            dbin_ref[0:1, lo:hi] += _colsum(val)
            vb = val.astype(BF16)
            dproj_ref[:, lo:hi] = vb
            dh = dh + _mm(vb, w_ref[lo:hi, :])
        xf = x_ref[...]
        rstd = _rstd(xf)
        xhat = xf * rstd
        gpre = g_ref[0:1, :]
        scale1 = 1.0 + mod_ref[1:2, :]
        dxn, s_pre = _norm_gain_bwd(dh, xhat, rstd, scale1 * gpre)
        red_ref[0:1, :] += _colsum(dh)
        red_ref[1:2, :] += gpre * s_pre
        red_ref[2:3, :] += scale1 * s_pre
        dx_ref[...] = dx1_ref[...] + dxn

    return pl.pallas_call(
        body, name="in_bwd", grid=(T // tm,),
        in_specs=[_rows(tm, ATTN_W), _rows(tm, KVD_W), _rows(tm, KVD_W), _rows(tm, POOL_W),
                  _rows(tm, 128), _rows(tm, 128), _rows(tm, 128), _rows(tm, D_MODEL), _rows(tm, D_MODEL),
                  _full((8, D_MODEL)), _full((8, D_MODEL)), _resident((IN_W, D_MODEL))],
        out_specs=[_rows(tm, D_MODEL), _rows(tm, IN_W), _full((8, D_MODEL)), _full((8, IN_W))],
        out_shape=[_sds((T, D_MODEL), F32), _sds((T, IN_W), BF16), _sds((8, D_MODEL), F32), _sds((8, IN_W), F32)],
        compiler_params=_cp("arbitrary"),
    )(dq, dk, dv, du, rc, rs1, rs2, x, dx1, mod8, g8, w_in)


def _mod_fwd(c_all, ada_w, ada_b_sh):
    tn = 512

    def body(c_ref, w_ref, b_ref, o_ref):
        cv = c_ref[...]
        ca = (cv * jax.nn.sigmoid(cv)).astype(BF16)
        o_ref[...] = _mm(ca, w_ref[...].astype(BF16)) + b_ref[...]

    return pl.pallas_call(
        body, name="mod_fwd", grid=(2, ADA_SH // tn),
        in_specs=[_full((8, D_MODEL)), pl.BlockSpec((None, D_MODEL, tn), lambda l, j: (l, 0, j)),
                  pl.BlockSpec((None, 1, tn), lambda l, j: (l, 0, j))],
        out_specs=pl.BlockSpec((None, 8, tn), lambda l, j: (l, 0, j)),
        out_shape=_sds((2, 8, ADA_SH), F32),
        compiler_params=_cp("parallel", "parallel"),
    )(c_all, ada_w, ada_b_sh)


def _ada_wgrad(c_all_t, dmod_sh):
    tn = 512

    def body(c_ref, d_ref, o_ref):
        cv = c_ref[...]
        ca = cv * jax.nn.sigmoid(cv)
        o_ref[...] = jnp.dot(ca, d_ref[...], preferred_element_type=F32, precision=lax.Precision.HIGHEST)

    return pl.pallas_call(
        body, name="ada_wgrad", grid=(2, ADA_SH // tn),
        in_specs=[_full((D_MODEL, 8)), pl.BlockSpec((None, 8, tn), lambda l, j: (l, 0, j))],
        out_specs=pl.BlockSpec((None, D_MODEL, tn), lambda l, j: (l, 0, j)),
        out_shape=_sds((2, D_MODEL, ADA_SH), F32),
        compiler_params=_cp("parallel", "parallel"),
    )(c_all_t, dmod_sh)


def _sum_devices(g):
    R = g.shape[1]

    def body(g_ref, o_ref):
        acc = g_ref[0]
        for d in range(1, N_DEV):
            acc = acc + g_ref[d]
        o_ref[...] = acc

    return pl.pallas_call(
        body, name="sum_devices", grid=(1,),
        in_specs=[_full((N_DEV, R, 128))], out_specs=_full((R, 128)), out_shape=_sds((R, 128), F32),
        compiler_params=_cp("arbitrary"),
    )(g)


def _adamw(w, g, m, v, name):
    R, C = w.shape
    tr = R
    for cand in (256, 128, 64, 32, 16, 8):
        if R % cand == 0 and cand * C * 4 <= 2 * 1024 * 1024:
            tr = cand
            break

    def body(w_ref, g_ref, m_ref, v_ref, d_ref, nm_ref, nv_ref):
        gv = g_ref[...]
        mn = ADAM_B1 * m_ref[...] + (1.0 - ADAM_B1) * gv
        vn = ADAM_B2 * v_ref[...] + (1.0 - ADAM_B2) * (gv * gv)
        m_hat = mn / (1.0 - ADAM_B1 ** ADAM_STEP)
        v_hat = vn / (1.0 - ADAM_B2 ** ADAM_STEP)
        d_ref[...] = -ADAM_LR * (m_hat / (jnp.sqrt(v_hat) + ADAM_EPS) + ADAM_WD * w_ref[...])
        nm_ref[...] = mn
        nv_ref[...] = vn

    spec = pl.BlockSpec((tr, C), lambda i: (i, 0))
    out = _sds((R, C), F32)
    return pl.pallas_call(
        body, name=name, grid=(R // tr,),
        in_specs=[spec] * 4, out_specs=[spec] * 3, out_shape=[out] * 3,
        compiler_params=_cp("parallel"),
    )(w, g, m, v)


def _adamw_nd(w, g, m, v, name):
    shape = w.shape
    if w.ndim == 2 and shape[1] < 128:
        view = (1, shape[0] * shape[1])
    else:
        view = (-1, shape[-1])
    outs = _adamw(*[t.reshape(view) for t in (w, g, m, v)], name=name)
    return [o.reshape(shape) for o in outs]


def _coords():
    return lax.axis_index("x"), lax.axis_index("y"), lax.axis_index("c")


def _other_chips(x, y):
    return [(1 - x, y), (x, 1 - y), (1 - x, 1 - y)]


def _allgather8(blk, name):
    m_per, n = blk.shape

    def body(x_ref, out_ref, send_sems, recv_sems, local_sem):
        x, y, c = _coords()
        me, sibling = (x, y, c), (x, y, 1 - c)
        chips = _other_chips(x, y)

        def rows(px, py, pc):
            return out_ref.at[pl.ds((4 * px + 2 * py + pc) * m_per, m_per), :]

        def copy(k, block, to, src=None):
            return pltpu.make_async_remote_copy(
                src_ref=rows(*block) if src is None else src, dst_ref=rows(*block),
                send_sem=send_sems.at[k], recv_sem=recv_sems.at[k], device_id=to, device_id_type=MESH)

        mine = pltpu.make_async_copy(x_ref, rows(*me), local_sem)
        mine.start()
        first = [copy(0, me, sibling, src=x_ref)]
        first += [copy(1 + j, me, (*chip, c), src=x_ref) for j, chip in enumerate(chips)]
        for cp in first:
            cp.start()
        passed = [copy(4 + j, (*chip, c), sibling) for j, chip in enumerate(chips)]
        for j, chip in enumerate(chips):
            copy(1 + j, (*chip, c), me).wait_recv()
            passed[j].start()
        copy(0, sibling, me).wait_recv()
        for j, chip in enumerate(chips):
            copy(4 + j, (*chip, 1 - c), me).wait_recv()
        for cp in first + passed:
            cp.wait_send()
        mine.wait()

    return pl.pallas_call(
        body, name=name,
        out_shape=_sds((N_DEV * m_per, n), blk.dtype),
        in_specs=[pl.BlockSpec(memory_space=pltpu.VMEM)],
        out_specs=pl.BlockSpec(memory_space=pltpu.VMEM),
        scratch_shapes=[pltpu.SemaphoreType.DMA((7,)), pltpu.SemaphoreType.DMA((7,)), pltpu.SemaphoreType.DMA],
        compiler_params=pltpu.CompilerParams(vmem_limit_bytes=VMEM_LIMIT),
    )(blk)


def _row_tile(r, n):
    for cand in range(r, 15, -16):
        if r % cand == 0 and cand % 16 == 0 and cand * n * 4 <= 2 * 1024 * 1024:
            return cand
    return r


def _cast_slot(w, chip, name):
    r, n = w.shape
    tr = _row_tile(r, n)

    def body(chip_ref, w_ref, o_ref):
        o_ref[...] = w_ref[...].astype(BF16)

    grid_spec = pltpu.PrefetchScalarGridSpec(
        num_scalar_prefetch=1, grid=(r // tr,),
        in_specs=[pl.BlockSpec((tr, n), lambda i, ch: (i, 0))],
        out_specs=pl.BlockSpec((None, tr, n), lambda i, ch: (ch[0], i, 0)))
    return pl.pallas_call(
        body, name=name, grid_spec=grid_spec, out_shape=_sds((N_SHARD, r, n), BF16),
        compiler_params=_cp("arbitrary"),
    )(chip, w)


def _allgather_weights(bufs, name):
    nt = len(bufs)
    hom = [pl.BlockSpec(memory_space=pl.ANY)] * nt

    def body(*refs):
        outs = refs[nt:2 * nt]
        send_sems, recv_sems = refs[2 * nt:]
        x, y, c = _coords()
        sibling = (x, y, 1 - c)
        chips = _other_chips(x, y)

        def copy(t, k, block_chip, hc, to):
            r = outs[t].shape[1] // 2
            blk = outs[t].at[2 * block_chip[0] + block_chip[1], pl.ds(hc * r, r)]
            return pltpu.make_async_remote_copy(
                src_ref=blk, dst_ref=blk,
                send_sem=send_sems.at[t, k], recv_sem=recv_sems.at[t, k], device_id=to, device_id_type=MESH)

        started = []
        for t in range(nt):
            for j, chip in enumerate(chips):
                cp = copy(t, j, (x, y), c, (*chip, c))
                cp.start()
                started.append(cp)
        for t in range(nt):
            for j, chip in enumerate(chips):
                copy(t, j, chip, c, sibling).wait_recv()
                fw = copy(t, 3 + j, chip, c, sibling)
                fw.start()
                started.append(fw)
        for t in range(nt):
            for j, chip in enumerate(chips):
                copy(t, 3 + j, chip, 1 - c, sibling).wait_recv()
        for cp in started:
            cp.wait_send()

    return pl.pallas_call(
        body, name=name,
        out_shape=[_sds(b.shape, b.dtype) for b in bufs],
        in_specs=hom, out_specs=hom,
        input_output_aliases={t: t for t in range(nt)},
        scratch_shapes=[pltpu.SemaphoreType.DMA((nt, 6)), pltpu.SemaphoreType.DMA((nt, 6))],
    )(*bufs)


def _join_halves(tots, name):
    nt = len(tots)
    hom = [pl.BlockSpec(memory_space=pl.ANY)] * nt

    def body(*refs):
        outs = refs[nt:2 * nt]
        send_sems, recv_sems = refs[2 * nt:]
        x, y, c = _coords()
        sibling = (x, y, 1 - c)
        cps = []
        for t in range(nt):
            cp = pltpu.make_async_remote_copy(
                src_ref=outs[t].at[c], dst_ref=outs[t].at[c],
                send_sem=send_sems.at[t], recv_sem=recv_sems.at[t], device_id=sibling, device_id_type=MESH)
            cp.start()
            cps.append(cp)
        for t in range(nt):
            pltpu.make_async_remote_copy(
                src_ref=outs[t].at[c], dst_ref=outs[t].at[1 - c],
                send_sem=send_sems.at[t], recv_sem=recv_sems.at[t], device_id=sibling, device_id_type=MESH).wait_recv()
        for cp in cps:
            cp.wait_send()

    return pl.pallas_call(
        body, name=name,
        out_shape=[_sds(t.shape, t.dtype) for t in tots],
        in_specs=hom, out_specs=hom,
        input_output_aliases={t: t for t in range(nt)},
        scratch_shapes=[pltpu.SemaphoreType.DMA((nt,)), pltpu.SemaphoreType.DMA((nt,))],
    )(*tots)


def _pair_sum(g, recv, core, chip, name):
    _, _, r, n = g.shape
    tr = _row_tile(r, n)

    def body(core_ref, chip_ref, g_ref, r_ref, sb_ref, own_ref):
        tot = g_ref[...] + r_ref[...]
        sb_ref[...] = tot.astype(BF16)

        @pl.when(pl.program_id(1) == chip_ref[0])
        def _():
            own_ref[...] = tot

    grid_spec = pltpu.PrefetchScalarGridSpec(
        num_scalar_prefetch=2, grid=(r // tr, N_SHARD),
        in_specs=[pl.BlockSpec((None, None, tr, n), lambda i, s, co, ch: (s, co[0], i, 0)),
                  pl.BlockSpec((None, tr, n), lambda i, s, co, ch: (s, i, 0))],
        out_specs=[pl.BlockSpec((None, tr, n), lambda i, s, co, ch: (s, i, 0)),
                   pl.BlockSpec((tr, n), lambda i, s, co, ch: (i, 0))])
    return pl.pallas_call(
        body, name=name, grid_spec=grid_spec,
        out_shape=[_sds((N_SHARD, r, n), BF16), _sds((r, n), F32)],
        compiler_params=_cp("arbitrary", "arbitrary"),
    )(core, chip, g, recv)


def _chip_sum(own, recv, core, name):
    r, n = own.shape
    tr = _row_tile(r, n)

    def body(core_ref, o_ref, r_ref, t_ref):
        acc = o_ref[...]
        for j in range(3):
            acc = acc + r_ref[j].astype(F32)
        t_ref[...] = acc

    grid_spec = pltpu.PrefetchScalarGridSpec(
        num_scalar_prefetch=1, grid=(r // tr,),
        in_specs=[pl.BlockSpec((tr, n), lambda i, co: (i, 0)), pl.BlockSpec((3, tr, n), lambda i, co: (0, i, 0))],
        out_specs=pl.BlockSpec((None, tr, n), lambda i, co: (co[0], i, 0)))
    return pl.pallas_call(
        body, name=name, grid_spec=grid_spec, out_shape=_sds((2, r, n), F32),
        compiler_params=_cp("arbitrary"),
    )(core, own, recv)


_HBM = pl.BlockSpec(memory_space=pltpu.HBM)
_SEM = pl.BlockSpec(memory_space=pltpu.SEMAPHORE)
_EFFECT = pltpu.SideEffectType.DATAFLOW_SIDE_EFFECTING


def _ici_copies(srcs, dsts, send_sems, recv_sems, send_view, recv_view):
    x, y, c = _coords()
    out = []
    if send_view is None:
        for t in range(len(srcs)):
            r = srcs[t].shape[1] // 2
            out.append(pltpu.make_async_remote_copy(
                src_ref=srcs[t].at[:, pl.ds((1 - c) * r, r)], dst_ref=dsts[t],
                send_sem=send_sems.at[3 * t], recv_sem=recv_sems.at[3 * t],
                device_id=(x, y, 1 - c), device_id_type=MESH))
        return out
    for t in range(len(srcs)):
        for j, chip in enumerate(_other_chips(x, y)):
            out.append(pltpu.make_async_remote_copy(
                src_ref=send_view(srcs[t], chip, j, (x, y), c), dst_ref=recv_view(dsts[t], chip, j, (x, y), c),
                send_sem=send_sems.at[3 * t + j], recv_sem=recv_sems.at[3 * t + j],
                device_id=(*chip, c), device_id_type=MESH))
    return out


def _ici_start(srcs, dsts, after, send_view, recv_view, name):
    nt = len(srcs)
    inplace = dsts is None
    nbuf = nt if inplace else 2 * nt

    def body(*refs):
        send_sems, recv_sems = refs[nbuf + 1], refs[nbuf + 2]
        s_out = refs[nbuf + 3:nbuf + 3 + nt]
        d_out = s_out if inplace else refs[nbuf + 3 + nt:nbuf + 3 + 2 * nt]
        token = refs[-1]
        for cp in _ici_copies(s_out, d_out, send_sems, recv_sems, send_view, recv_view):
            cp.start()
        token[...] = jnp.zeros_like(token)

    bufs = list(srcs) + ([] if inplace else list(dsts))
    res = pl.pallas_call(
        body, name=name,
        out_shape=(pltpu.SemaphoreType.DMA((3 * nt,)), pltpu.SemaphoreType.DMA((3 * nt,)),
                   *[pltpu.HBM(b.shape, b.dtype) for b in bufs], _sds((8, 128), F32)),
        in_specs=[_HBM] * nbuf + [pl.BlockSpec(memory_space=pl.ANY)],
        out_specs=(_SEM, _SEM, *[_HBM] * nbuf, pl.BlockSpec(memory_space=pltpu.VMEM)),
        input_output_aliases={i: 2 + i for i in range(nbuf)},
        compiler_params=pltpu.CompilerParams(has_side_effects=_EFFECT),
    )(*[pltpu.with_memory_space_constraint(b, pltpu.HBM) for b in bufs], after)
    send_sems, recv_sems = res[0], res[1]
    s_thru = list(res[2:2 + nt])
    d_thru = s_thru if inplace else list(res[2 + nt:2 + 2 * nt])
    return send_sems, recv_sems, s_thru, d_thru, res[-1]


def _ici_wait(send_sems, recv_sems, srcs, dsts, after, send_view, recv_view, name):
    nt = len(srcs)
    inplace = dsts is None
    nbuf = nt if inplace else 2 * nt

    def body(*refs):
        send_ref, recv_ref = refs[nbuf], refs[nbuf + 1]
        s_out = refs[nbuf + 3:nbuf + 3 + nt]
        d_out = s_out if inplace else refs[nbuf + 3 + nt:nbuf + 3 + 2 * nt]
        for cp in _ici_copies(s_out, d_out, send_ref, recv_ref, send_view, recv_view):
            cp.wait_send()
            cp.wait_recv()

    bufs = list(srcs) + ([] if inplace else list(dsts))
    res = pl.pallas_call(
        body, name=name,
        out_shape=tuple(pltpu.HBM(b.shape, b.dtype) for b in bufs),
        in_specs=[_HBM] * nbuf + [_SEM, _SEM, pl.BlockSpec(memory_space=pl.ANY)],
        out_specs=tuple([_HBM] * nbuf),
        input_output_aliases={i: i for i in range(nbuf)},
        compiler_params=pltpu.CompilerParams(has_side_effects=_EFFECT),
    )(*bufs, send_sems, recv_sems, after)
    return list(res[:nt]) if inplace else (list(res[:nt]), list(res[nt:]))


def _w_half(buf, chip, c):
    r = buf.shape[1] // 2
    return buf.at[2 * chip[0] + chip[1], pl.ds(c * r, r)]


def _ag_send_view(buf, chip, j, me, c):
    return _w_half(buf, me, c)


def _ag_recv_view(buf, chip, j, me, c):
    return _w_half(buf, me, c)


def _rs_send_view(buf, chip, j, me, c):
    return buf.at[2 * chip[0] + chip[1]]


def _rs_recv_view(buf, chip, j, me, c):
    return buf.at[j]


def _ag_forward(bufs, name):
    nt = len(bufs)
    hom = [pl.BlockSpec(memory_space=pl.ANY)] * nt

    def body(*refs):
        outs = refs[nt:2 * nt]
        send_sems, recv_sems = refs[2 * nt:]
        x, y, c = _coords()
        sibling = (x, y, 1 - c)
        chips = _other_chips(x, y)

        def copy(t, j, hc):
            blk = _w_half(outs[t], chips[j], hc)
            return pltpu.make_async_remote_copy(
                src_ref=blk, dst_ref=blk, send_sem=send_sems.at[t, j], recv_sem=recv_sems.at[t, j],
                device_id=sibling, device_id_type=MESH)

        started = [copy(t, j, c) for t in range(nt) for j in range(3)]
        for cp in started:
            cp.start()
        for t in range(nt):
            for j in range(3):
                copy(t, j, 1 - c).wait_recv()
        for cp in started:
            cp.wait_send()

    return pl.pallas_call(
        body, name=name,
        out_shape=[_sds(b.shape, b.dtype) for b in bufs],
        in_specs=hom, out_specs=hom,
        input_output_aliases={t: t for t in range(nt)},
        scratch_shapes=[pltpu.SemaphoreType.DMA((nt, 3)), pltpu.SemaphoreType.DMA((nt, 3))],
    )(*bufs)


def _rs_swap_begin(grads, after, tag):
    land = [lax.empty((N_SHARD, g.shape[1] // 2, g.shape[2]), g.dtype) for g in grads]
    send_sems, recv_sems, s_thru, d_thru, token = _ici_start(grads, land, after, None, None, name="rs_swapgo_" + tag)
    return dict(sems=(send_sems, recv_sems), grads=s_thru, land=d_thru, tag=tag), token


def _rs_scatter_begin(swap, after):
    tag = swap["tag"]
    x, y, c = _coords()
    core = jnp.reshape(c, (1,)).astype(jnp.int32)
    chip = jnp.reshape(2 * x + y, (1,)).astype(jnp.int32)
    grads, recv = _ici_wait(*swap["sems"], swap["grads"], swap["land"], after, None, None, name="rs_swapend_" + tag)
    sums, owns = [], []
    for t, (g, rv) in enumerate(zip(grads, recv)):
        r = g.shape[1] // 2
        sb, own = _pair_sum(g.reshape(N_SHARD, 2, r, g.shape[2]), rv, core, chip, name=f"rs_pair_{tag}_{t}")
        sums.append(sb)
        owns.append(own)
    land = [lax.empty((3,) + s.shape[1:], s.dtype) for s in sums]
    send_sems, recv_sems, s_thru, d_thru, token = _ici_start(
        sums, land, after, _rs_send_view, _rs_recv_view, name="rs_start_" + tag)
    return dict(sems=(send_sems, recv_sems), sums=s_thru, land=d_thru, owns=owns, core=core, tag=tag), token


def _rs_end(state, after):
    tag = state["tag"]
    _, got = _ici_wait(*state["sems"], state["sums"], state["land"], after, _rs_send_view, _rs_recv_view,
                       name="rs_wait_" + tag)
    tots = [_chip_sum(o, gt, state["core"], name=f"rs_chip_{tag}_{t}")
            for t, (o, gt) in enumerate(zip(state["owns"], got))]
    full = _join_halves(tots, name="rs_join_" + tag)
    return [f.reshape(2 * f.shape[1], f.shape[2]) for f in full]


def _rope_lane_table():
    d = jnp.arange(128) % HEAD
    inv_freq = ROPE_THETA ** (-jnp.arange(0, ROT, 2, dtype=F32) / ROT)
    rot = d < ROT
    rows = [jnp.where(rot, inv_freq[d % (ROT // 2)], 0.0), rot.astype(F32),
            (d < ROT // 2).astype(F32), jnp.logical_and(d >= ROT // 2, rot).astype(F32)]
    return jnp.concatenate([jnp.stack(rows), jnp.zeros((4, 128), F32)], axis=0)


def _pad8(rows):
    return jnp.concatenate([rows, jnp.zeros((8 - rows.shape[0], rows.shape[1]), F32)], axis=0)


def kernel(x, c, positions, ada_w, ada_b, w_in, b_in, sinks, pool_w, pool_scale, w_out, w_gate, w_up, w_down, g_pre_mix, g_post_mix, g_pre_ffn, g_post_ffn, loss_target, m_ada_w, m_ada_b, m_w_in, m_b_in, m_sinks, m_pool_w, m_pool_scale, m_w_out, m_w_gate, m_w_up, m_w_down, m_g_pre_mix, m_g_post_mix, m_g_pre_ffn, m_g_post_ffn, v_ada_w, v_ada_b, v_w_in, v_b_in, v_sinks, v_pool_w, v_pool_scale, v_w_out, v_w_gate, v_w_up, v_w_down, v_g_pre_mix, v_g_post_mix, v_g_pre_ffn, v_g_post_ffn):
    T = x.shape[1]
    n_layers = ada_w.shape[0]
    ax, ay, ac = _coords()
    my_dev = 4 * ax + 2 * ay + ac
    my_chip = 2 * ax + ay
    x0 = x.reshape(T, D_MODEL)
    target = loss_target.reshape(T, D_MODEL)

    c_all = _allgather8(c.reshape(8, 128), name="ag_c").reshape(N_DEV, D_MODEL)
    ada_b_sh = lax.dynamic_slice_in_dim(ada_b, my_chip * ADA_SH, ADA_SH, axis=1).reshape(n_layers, 1, ADA_SH)
    mod_part = _mod_fwd(c_all, ada_w, ada_b_sh)
    mod_all = _allgather8(mod_part.reshape(n_layers * 8, ADA_SH), name="ag_mod")
    mod_all = mod_all.reshape(N_DEV, n_layers, 8, ADA_SH)[0::2]
    mod_mine = lax.dynamic_index_in_dim(mod_all, my_dev, axis=2, keepdims=False)
    mod = jnp.transpose(mod_mine, (1, 0, 2)).reshape(n_layers, 6, D_MODEL)

    pos_b = jnp.broadcast_to(positions.reshape(T, 1), (T, 128))
    rc, rs1, rs2 = _rope_tables(pos_b, _rope_lane_table())

    chip1 = jnp.reshape(my_chip, (1,)).astype(jnp.int32)

    def tr(t):
        return jnp.transpose(t, (0, 2, 1))

    w_in_t, w_gate_t, w_up_t = tr(w_in), tr(w_gate), tr(w_up)

    def cast_layer(l):
        return [_cast_slot(w[l], chip1, name=f"cast_{nm}{l}")
                for nm, w in (("w_in", w_in_t), ("w_out", w_out), ("w_gate", w_gate_t), ("w_up", w_up_t),
                              ("w_down", w_down))]

    def as_operands(bufs):
        gin, gout, gg, gu, gd = bufs
        return gin.reshape(IN_W, D_MODEL), gout.reshape(D_MODEL, D_MODEL), gg, gu, gd

    bufs0 = cast_layer(0)
    win0 = _allgather_weights(bufs0[:1], name="ag_w0_in")
    rest_send, rest_recv, rest_bufs, _, ag_token = _ici_start(
        bufs0[1:], None, win0[0], _ag_send_view, _ag_recv_view, name="ag_start_0")
    weights = [None] * n_layers

    saved = []
    xl = x0
    for l in range(n_layers):
        mod8 = _pad8(mod[l])
        if l + 1 < n_layers:
            ag_send, ag_recv, ag_bufs, _, ag_token = _ici_start(
                cast_layer(l + 1), None, ag_token, _ag_send_view, _ag_recv_view, name=f"ag_start_{l + 1}")
        if l == 0 or l + 1 < n_layers:
            mod8 = mod8 + ag_token[0, 0]
        g8 = _pad8(jnp.stack([g_pre_mix[l], g_post_mix[l], g_pre_ffn[l], g_post_ffn[l]]))
        sink_b = jnp.broadcast_to(sinks[l][:, None], (N_HEADS, 128))
        psc = pool_scale[l].reshape(1, POOL_W)
        win = win0[0].reshape(IN_W, D_MODEL) if l == 0 else weights[l][0]
        h, q, k, v, u = _fwd_in(xl, mod8, g8, win, b_in[l].reshape(1, IN_W), rc, rs1, rs2)
        attn, lse = _attn_fwd(q, k, v, sink_b)
        pool, pooled = _pool_fwd(u, pool_w[l], psc)
        if l == 0:
            arrived = _ici_wait(rest_send, rest_recv, rest_bufs, None, pool, _ag_send_view, _ag_recv_view,
                                name="ag_wait_0")
            weights[0] = as_operands(win0 + _ag_forward(arrived, name="ag_fwd_0"))
        win, wout, wg, wu, wd = weights[l]
        mix, x1 = _fwd_out(attn, pool, xl, wout, g8, mod8)
        if l + 1 < n_layers:
            h2, act, ga, gb, f, x2 = _ffn_fwd(x1, mod8, g8, wg, wu, wd)
        else:
            h2, act, ga, gb, f, x2, loss_tile = _ffn_fwd(x1, mod8, g8, wg, wu, wd, target=target)
        saved.append(dict(x=xl, h=h, q=q, k=k, v=v, lse=lse, attn=attn, pool=pool, pooled=pooled, mix=mix,
                          x1=x1, h2=h2, act=act, ga=ga, gb=gb, f=f, mod8=mod8, g8=g8, sink_b=sink_b, psc=psc))
        xl = x2
        if l + 1 < n_layers:
            arrived = _ici_wait(ag_send, ag_recv, ag_bufs, None, x2, _ag_send_view, _ag_recv_view,
                                name=f"ag_wait_{l + 1}")
            weights[l + 1] = as_operands(_ag_forward(arrived, name=f"ag_fwd_{l + 1}"))

    dy = xl

    small = [None] * n_layers
    dmod_rows = [None] * n_layers
    reduced = [dict() for _ in range(n_layers)]
    att_swap = None
    dx = dy
    for l in reversed(range(n_layers)):
        s = saved[l]
        win, wout, wg, wu, wd = weights[l]
        if att_swap is not None:
            s = dict(s, mod8=s["mod8"] + att_swap[1][0, 0])
        dx1, df, da, db, red_f = _ffn_bwd(dx, s["f"], s["ga"], s["gb"], s["x1"], s["mod8"], s["g8"], wg, wu, wd)
        if att_swap is not None:
            att_scatter = _rs_scatter_begin(att_swap[0], dx1)
        g_wd = _wgrad_rows(s["act"], df, name="wgrad_down")
        g_wg = _wgrad_rows(da, s["h2"], name="wgrad_gate")
        g_wu = _wgrad_rows(db, s["h2"], name="wgrad_up")
        if att_swap is not None:
            got = _rs_end(att_scatter[0], g_wu)
            reduced[l + 1].update(w_in=got[0], w_out=got[1])
        ffn_swap = _rs_swap_begin([g_wg, g_wu, g_wd], g_wu, tag=f"{l}f")
        s = dict(s, mod8=s["mod8"] + ffn_swap[1][0, 0])
        dmix, dattn, dpool, red_c = _mix_bwd(dx1, s["mix"], s["mod8"], s["g8"], wout)
        g_wout = jnp.concatenate([_wgrad(s["attn"], dmix, name="wgrad_out_a"),
                                  _wgrad(s["pool"], dmix, name="wgrad_out_p")], axis=0)
        ffn_scatter = _rs_scatter_begin(ffn_swap[0], g_wout)
        dq, dk, dv, dsink = _attn_bwd(s["q"], s["k"], s["v"], s["lse"], dattn, s["sink_b"] + ffn_scatter[1][0:1, :])
        du, g_poolw, dpsc = _pool_bwd(dpool, s["pooled"], pool_w[l], s["psc"])
        dx, dproj, red_d, dbin = _in_bwd(dq, dk, dv, du, rc, rs1, rs2, s["x"], dx1, s["mod8"], s["g8"], win)
        g_win = _wgrad(dproj, s["h"], name="wgrad_in")
        g_win_sh = g_win.reshape(N_SHARD, IN_SH, D_MODEL)
        got = _rs_end(ffn_scatter[0], g_win)
        reduced[l].update(w_gate=got[0], w_up=got[1], w_down=got[2])
        att_swap = _rs_swap_begin([g_win_sh, g_wout.reshape(N_SHARD, OUT_SH, D_MODEL)], g_win, tag=f"{l}a")
        dmod_rows[l] = jnp.concatenate([red_d[0], red_d[1], red_c[0], red_f[2], red_f[3], red_f[0]])
        small[l] = jnp.concatenate([red_d[2], red_c[1], red_f[4], red_f[1], dbin[0], dpsc[0], dsink[:, 0],
                                    jnp.zeros((120,), F32), g_poolw.reshape(-1)])
    grad_x = dx.reshape(1, T, D_MODEL)

    per_layer = small[0].shape[0]
    rows_small = n_layers * per_layer // 128
    rows_mod = n_layers * 6 * D_MODEL // 128
    rows_pad = -(rows_small + rows_mod + 1) % 8 + 1
    pack = jnp.concatenate(small + dmod_rows + [loss_tile[0, 0:1], jnp.zeros((rows_pad * 128 - 1,), F32)]).reshape(-1, 128)
    pack = pack + att_swap[1][0, 0]
    gathered = _allgather8(pack, name="ag_small").reshape(N_DEV, pack.shape[0], 128)
    summed = _sum_devices(gathered)
    att_scatter = _rs_scatter_begin(att_swap[0], summed)
    loss = summed[rows_small + rows_mod, 0]
    small_sum = summed[:rows_small].reshape(n_layers, per_layer)
    o = 0
    small_g = {}
    for nm, width in (("g_pre_mix", D_MODEL), ("g_post_mix", D_MODEL), ("g_pre_ffn", D_MODEL),
                      ("g_post_ffn", D_MODEL), ("b_in", IN_W), ("pool_scale", POOL_W), ("sinks", 128),
                      ("pool_w", 4 * 128 * 128)):
        small_g[nm] = small_sum[:, o:o + width]
        o += width
    small_g["sinks"] = small_g["sinks"][:, :N_HEADS]
    small_g["pool_w"] = small_g["pool_w"].reshape(n_layers, 4, 128, 128)
    small_g["ada_b"] = summed[rows_small:rows_small + rows_mod].reshape(n_layers, 6 * D_MODEL)
    dmod_all = gathered[:, rows_small:rows_small + rows_mod].reshape(N_DEV, n_layers, N_SHARD, ADA_SH)
    dmod_sh = lax.dynamic_index_in_dim(dmod_all, my_chip, axis=2, keepdims=False)
    g_ada_w = _ada_wgrad(jnp.transpose(c_all), jnp.transpose(dmod_sh, (1, 0, 2)))

    grads = dict(ada_w=g_ada_w, ada_b=small_g["ada_b"], b_in=small_g["b_in"], sinks=small_g["sinks"],
                 pool_w=small_g["pool_w"], pool_scale=small_g["pool_scale"], g_pre_mix=small_g["g_pre_mix"],
                 g_post_mix=small_g["g_post_mix"], g_pre_ffn=small_g["g_pre_ffn"], g_post_ffn=small_g["g_post_ffn"])
    params = dict(ada_w=(ada_w, m_ada_w, v_ada_w), ada_b=(ada_b, m_ada_b, v_ada_b), w_in=(w_in, m_w_in, v_w_in),
                  b_in=(b_in, m_b_in, v_b_in), sinks=(sinks, m_sinks, v_sinks), pool_w=(pool_w, m_pool_w, v_pool_w),
                  pool_scale=(pool_scale, m_pool_scale, v_pool_scale), w_out=(w_out, m_w_out, v_w_out),
                  w_gate=(w_gate, m_w_gate, v_w_gate), w_up=(w_up, m_w_up, v_w_up),
                  w_down=(w_down, m_w_down, v_w_down), g_pre_mix=(g_pre_mix, m_g_pre_mix, v_g_pre_mix),
                  g_post_mix=(g_post_mix, m_g_post_mix, v_g_post_mix), g_pre_ffn=(g_pre_ffn, m_g_pre_ffn, v_g_pre_ffn),
                  g_post_ffn=(g_post_ffn, m_g_post_ffn, v_g_post_ffn))
    names = list(params)
    updates = {nm: _adamw_nd(*params[nm][:1], grads[nm], *params[nm][1:], name="adamw_" + nm) for nm in grads}

    got = _rs_end(att_scatter[0], updates["ada_w"][0])
    reduced[0].update(w_in=got[0], w_out=got[1])
    for nm in ("w_in", "w_out", "w_gate", "w_up", "w_down"):
        g = jnp.stack([reduced[l][nm] for l in range(n_layers)])
        if nm in ("w_in", "w_gate", "w_up"):
            upd = _adamw_nd(tr(params[nm][0]), g, tr(params[nm][1]), tr(params[nm][2]), name="adamw_" + nm)
            grads[nm], updates[nm] = tr(g), [tr(u) for u in upd]
        else:
            grads[nm], updates[nm] = g, _adamw_nd(params[nm][0], g, *params[nm][1:], name="adamw_" + nm)
    return (loss, grad_x, *[grads[nm] for nm in names], *[updates[nm][0] for nm in names],
            *[updates[nm][1] for nm in names], *[updates[nm][2] for nm in names])
```

```python
import functools

import jax
import jax.numpy as jnp
from jax import lax
from jax.experimental import pallas as pl
from jax.experimental.pallas import tpu as pltpu

F32 = jnp.float32
BF16 = jnp.bfloat16
MESH = pl.DeviceIdType.MESH

D_MODEL = 1024
ATTN_W = 512
KV_W = 128
KVD_W = 256
POOL_W = 512
IN_W = 1280
D_FF = 2816
N_SHARD = 4
FF_SH = D_FF // N_SHARD
IN_SH = IN_W // N_SHARD
OUT_SH = D_MODEL // N_SHARD
ADA_SH = 6 * D_MODEL // N_SHARD
HEAD = 64
N_HEADS = 8
GROUP = 4
BLK = 128
POOL_WINDOWS = (2, 4, 8, 16)
HALO = 16
ROT = 16
ROPE_THETA = 500000.0
EPS = 1e-6
NEG_INF = -1e30
N_DEV = 8

ADAM_LR = 0.001
ADAM_B1 = 0.9
ADAM_B2 = 0.999
ADAM_EPS = 1e-08
ADAM_WD = 0.01
ADAM_STEP = 10

VMEM_LIMIT = 48 * 1024 * 1024
FFN_VMEM_LIMIT = 60 * 1024 * 1024
WGRAD_TOKENS = 2048


def _cp(*sem, vmem=VMEM_LIMIT):
    return pltpu.CompilerParams(dimension_semantics=sem, vmem_limit_bytes=vmem)


def _full(shape):
    nd = len(shape)
    return pl.BlockSpec(shape, lambda *_: (0,) * nd)


def _resident(shape):
    nd = len(shape)
    return pl.BlockSpec(shape, lambda *_: (0,) * nd, pipeline_mode=pl.Buffered(1))


def _rows(tm, ncol):
    return pl.BlockSpec((tm, ncol), lambda i: (i, 0))


def _sds(shape, dtype):
    return jax.ShapeDtypeStruct(shape, dtype)


def _nt(a, b):
    return lax.dot_general(a, b, (((1,), (1,)), ((), ())), preferred_element_type=F32)


def _tn(a, b):
    return lax.dot_general(a, b, (((0,), (0,)), ((), ())), preferred_element_type=F32)


def _mm(a, b):
    return jnp.dot(a, b, preferred_element_type=F32)


def _rstd(x):
    return lax.rsqrt(jnp.mean(x * x, axis=-1, keepdims=True) + EPS)


def _colsum(x):
    return jnp.sum(x, axis=0, keepdims=True)


def _norm_gain_bwd(dy, xhat, rstd, gain):
    p = dy * xhat
    dx = rstd * (dy * gain - xhat * jnp.mean(p * gain, axis=-1, keepdims=True))
    return dx, _colsum(p)


def _rope_tables(pos_b, lane_tab):
    T = pos_b.shape[0]
    tm = min(T, 1024)

    def body(pos_ref, tab_ref, c_ref, s1_ref, s2_ref):
        ang = pos_ref[...].astype(F32) * tab_ref[0:1, :]
        cs = jnp.cos(ang)
        sn = jnp.sin(ang)
        m_rot = tab_ref[1:2, :]
        c_ref[...] = cs * m_rot + (1.0 - m_rot)
        s1_ref[...] = -sn * tab_ref[2:3, :]
        s2_ref[...] = sn * tab_ref[3:4, :]

    out = _sds((T, 128), F32)
    return pl.pallas_call(
        body, name="rope_tables", grid=(T // tm,),
        in_specs=[_rows(tm, 128), _full((8, 128))],
        out_specs=[_rows(tm, 128)] * 3, out_shape=[out] * 3,
        compiler_params=_cp("parallel"),
    )(pos_b, lane_tab)


def _rot_fwd(t, c, s1, s2):
    w = t.shape[-1]
    return t * c + pltpu.roll(t, w - 8, 1) * s1 + pltpu.roll(t, 8, 1) * s2


def _rot_bwd(d, c, s1, s2):
    w = d.shape[-1]
    return d * c + pltpu.roll(d * s1, 8, 1) + pltpu.roll(d * s2, w - 8, 1)


def _store_dup(ref, t):
    low = lax.broadcasted_iota(jnp.int32, t.shape, 1) < HEAD
    sw = pltpu.roll(t, HEAD, 1)
    ref[:, 0:128] = jnp.where(low, t, sw).astype(BF16)
    ref[:, 128:256] = jnp.where(low, sw, t).astype(BF16)


def _fold_dup(d):
    low = lax.broadcasted_iota(jnp.int32, (d.shape[0], 128), 1) < HEAD
    d0 = d[:, 0:128]
    d1 = d[:, 128:256]
    return jnp.where(low, d0 + pltpu.roll(d0, HEAD, 1), d1 + pltpu.roll(d1, HEAD, 1))


def _fwd_in(x, mod8, g8, w_in, b_in, rc, rs1, rs2):
    T = x.shape[0]
    tm = min(T, 512)

    def body(x_ref, mod_ref, g_ref, w_ref, b_ref, c_ref, s1_ref, s2_ref,
             h_ref, q_ref, k_ref, v_ref, u_ref):
        xf = x_ref[...]
        h = (xf * _rstd(xf) * g_ref[0:1, :]) * (1.0 + mod_ref[1:2, :]) + mod_ref[0:1, :]
        hb = h.astype(BF16)
        h_ref[...] = hb
        c = c_ref[...]
        s1 = s1_ref[...]
        s2 = s2_ref[...]
        q = _nt(hb, w_ref[0:ATTN_W, :]) + b_ref[:, 0:ATTN_W]
        q = _rot_fwd(q, jnp.tile(c, (1, 4)), jnp.tile(s1, (1, 4)), jnp.tile(s2, (1, 4)))
        q_ref[...] = (q * (HEAD ** -0.5)).astype(BF16)
        k = _nt(hb, w_ref[ATTN_W:ATTN_W + KV_W, :]) + b_ref[:, ATTN_W:ATTN_W + KV_W]
        _store_dup(k_ref, _rot_fwd(k, c, s1, s2))
        v = _nt(hb, w_ref[ATTN_W + KV_W:ATTN_W + 2 * KV_W, :]) + b_ref[:, ATTN_W + KV_W:ATTN_W + 2 * KV_W]
        _store_dup(v_ref, v)
        u_ref[...] = _nt(hb, w_ref[ATTN_W + 2 * KV_W:IN_W, :]) + b_ref[:, ATTN_W + 2 * KV_W:IN_W]

    return pl.pallas_call(
        body, name="fwd_in", grid=(T // tm,),
        in_specs=[_rows(tm, D_MODEL), _full((8, D_MODEL)), _full((8, D_MODEL)),
                  _resident((IN_W, D_MODEL)), _full((1, IN_W)),
                  _rows(tm, 128), _rows(tm, 128), _rows(tm, 128)],
        out_specs=[_rows(tm, D_MODEL), _rows(tm, ATTN_W), _rows(tm, KVD_W), _rows(tm, KVD_W), _rows(tm, POOL_W)],
        out_shape=[_sds((T, D_MODEL), BF16), _sds((T, ATTN_W), BF16), _sds((T, KVD_W), BF16),
                   _sds((T, KVD_W), BF16), _sds((T, POOL_W), F32)],
        compiler_params=_cp("parallel"),
    )(x, mod8, g8, w_in, b_in, rc, rs1, rs2)


def _band_mask(n):
    kk = lax.broadcasted_iota(jnp.int32, (2 * BLK, BLK), 0)
    qi = lax.broadcasted_iota(jnp.int32, (2 * BLK, BLK), 1)
    first = jnp.where(n > 0, 0, 2 * BLK)
    in_prev = jnp.logical_and(kk < BLK, kk > qi + first)
    in_cur = jnp.logical_and(kk >= BLK, (kk - BLK) <= qi)
    one = jnp.logical_or(in_prev, in_cur)
    return jnp.concatenate([one] * GROUP, axis=1)


def _head_row(ref, j):
    return jnp.concatenate([ref[GROUP * j + r:GROUP * j + r + 1, :] for r in range(GROUP)], axis=1)


def _stack_heads(x_ref, j):
    low = lax.broadcasted_iota(jnp.int32, (BLK, 128), 1) < HEAD
    parts = []
    for gp in (2 * j, 2 * j + 1):
        x2 = x_ref[:, gp * 128:(gp + 1) * 128]
        parts.append(jnp.where(low, x2, jnp.zeros_like(x2)))
        parts.append(jnp.where(low, jnp.zeros_like(x2), x2))
    return jnp.concatenate(parts, axis=0)


def _unstack_heads(o):
    low = lax.broadcasted_iota(jnp.int32, (BLK, 128), 1) < HEAD
    return [jnp.where(low, o[0:BLK], o[BLK:2 * BLK]), jnp.where(low, o[2 * BLK:3 * BLK], o[3 * BLK:4 * BLK])]


def _attn_fwd(q, kd, vd, sink_b):
    T = q.shape[0]
    nb = T // BLK

    def body(q_ref, kp_ref, kc_ref, vp_ref, vc_ref, sk_ref, o_ref, lse_ref):
        valid = _band_mask(pl.program_id(0))
        for j in range(N_HEADS // GROUP):
            lanes = slice(j * 128, (j + 1) * 128)
            kcat = jnp.concatenate([kp_ref[:, lanes], kc_ref[:, lanes]], axis=0)
            vcat = jnp.concatenate([vp_ref[:, lanes], vc_ref[:, lanes]], axis=0)
            s = jnp.where(valid, _nt(kcat, _stack_heads(q_ref, j)), NEG_INF)
            sk = _head_row(sk_ref, j)
            m = jnp.maximum(jnp.max(s, axis=0, keepdims=True), sk)
            p = jnp.exp(s - m)
            den = jnp.sum(p, axis=0, keepdims=True) + jnp.exp(sk - m)
            p = p * (1.0 / den)
            o = _tn(p.astype(BF16), vcat)
            o_ref[:, 2 * j * 128:(2 * j + 2) * 128] = jnp.concatenate(_unstack_heads(o), axis=1).astype(BF16)
            lse = m + jnp.log(den)
            for r in range(GROUP):
                lse_ref[GROUP * j + r:GROUP * j + r + 1, :] = lse[:, r * 128:(r + 1) * 128]

    prev = lambda n: (jnp.maximum(n - 1, 0), 0)
    cur = lambda n: (n, 0)
    return pl.pallas_call(
        body, name="attn_fwd", grid=(nb,),
        in_specs=[pl.BlockSpec((BLK, ATTN_W), cur),
                  pl.BlockSpec((BLK, KVD_W), prev), pl.BlockSpec((BLK, KVD_W), cur),
                  pl.BlockSpec((BLK, KVD_W), prev), pl.BlockSpec((BLK, KVD_W), cur),
                  _full((8, 128))],
        out_specs=[pl.BlockSpec((BLK, ATTN_W), cur), pl.BlockSpec((N_HEADS, 128), cur)],
        out_shape=[_sds((T, ATTN_W), BF16), _sds((nb * N_HEADS, 128), F32)],
        compiler_params=_cp("parallel"),
    )(q, kd, kd, vd, vd, sink_b)


def _pool_fwd(u, pool_w, pool_scale):
    T = u.shape[0]
    tm = min(T, 512)

    def body(u_ref, w_ref, sc_ref, out_ref, pooled_ref, halo):
        i = pl.program_id(0)

        @pl.when(i == 0)
        def _():
            halo[...] = jnp.zeros_like(halo)

        ub = u_ref[...]
        ext = jnp.concatenate([halo[...], ub], axis=0)
        halo[...] = ub[tm - HALO:, :]
        tpos = (i * tm + lax.broadcasted_iota(jnp.int32, (tm, 1), 0)).astype(F32)
        for g, w in enumerate(POOL_WINDOWS):
            lanes = slice(g * 128, (g + 1) * 128)
            s = ext[:, lanes]
            sh = 1
            while sh < w:
                s = s + pltpu.roll(s, sh, 0)
                sh *= 2
            cnt = jnp.minimum(tpos + 1.0, float(w))
            pb = (s[HALO:, :] / cnt - ub[:, lanes]).astype(BF16)
            z = _mm(pb, w_ref[g].astype(BF16))
            out_ref[:, lanes] = (z * sc_ref[:, lanes]).astype(BF16)
            pooled_ref[:, lanes] = pb

    return pl.pallas_call(
        body, name="pool_fwd", grid=(T // tm,),
        in_specs=[_rows(tm, POOL_W), _full((4, 128, 128)), _full((1, POOL_W))],
        out_specs=[_rows(tm, POOL_W), _rows(tm, POOL_W)],
        out_shape=[_sds((T, POOL_W), BF16), _sds((T, POOL_W), BF16)],
        scratch_shapes=[pltpu.VMEM((HALO, POOL_W), F32)],
        compiler_params=_cp("arbitrary"),
    )(u, pool_w, pool_scale)


def _fwd_out(attn, pool, x, w_out, g8, mod8):
    T = x.shape[0]
    tm = min(T, 512)

    def body(a_ref, p_ref, x_ref, w_ref, g_ref, mod_ref, mix_ref, x1_ref):
        mix = _mm(a_ref[...], w_ref[0:ATTN_W, :]) + _mm(p_ref[...], w_ref[ATTN_W:, :])
        mix_ref[...] = mix
        x1_ref[...] = x_ref[...] + mod_ref[2:3, :] * (mix * _rstd(mix) * g_ref[1:2, :])

    return pl.pallas_call(
        body, name="fwd_out", grid=(T // tm,),
        in_specs=[_rows(tm, ATTN_W), _rows(tm, POOL_W), _rows(tm, D_MODEL),
                  _resident((D_MODEL, D_MODEL)), _full((8, D_MODEL)), _full((8, D_MODEL))],
        out_specs=[_rows(tm, D_MODEL), _rows(tm, D_MODEL)],
        out_shape=[_sds((T, D_MODEL), F32), _sds((T, D_MODEL), F32)],
        compiler_params=_cp("parallel"),
    )(attn, pool, x, w_out, g8, mod8)


def _sh_rows(tm):
    return pl.BlockSpec((N_SHARD, tm, FF_SH), lambda i: (0, i, 0))


def _ffn_fwd(x1, mod8, g8, wg, wu, wd, target=None):
    T = x1.shape[0]
    tm = min(T, 512)
    last = target is not None

    def body(*refs):
        x_ref, mod_ref, g_ref, wg_ref, wu_ref, wd_ref = refs[:6]
        t_ref = refs[6] if last else None
        h_ref, act_ref, ga_ref, gb_ref, f_ref, x2_ref = refs[6 + last:12 + last]
        xf = x_ref[...]
        h = (xf * _rstd(xf) * g_ref[2:3, :]) * (1.0 + mod_ref[4:5, :]) + mod_ref[3:4, :]
        hb = h.astype(BF16)
        h_ref[...] = hb
        f = jnp.zeros((tm, D_MODEL), F32)
        for s in range(N_SHARD):
            a = _nt(hb, wg_ref[s])
            b = _nt(hb, wu_ref[s])
            sig = jax.nn.sigmoid(a)
            sl = a * sig
            act = (sl * b).astype(BF16)
            act_ref[s] = act
            ga_ref[s] = (b * (sig * (1.0 + a * (1.0 - sig)))).astype(BF16)
            gb_ref[s] = sl.astype(BF16)
            f = f + _mm(act, wd_ref[s])
        f_ref[...] = f
        x2 = xf + mod_ref[5:6, :] * (f * _rstd(f) * g_ref[3:4, :])
        if not last:
            x2_ref[...] = x2
        else:
            loss_ref = refs[13]

            @pl.when(pl.program_id(0) == 0)
            def _():
                loss_ref[...] = jnp.zeros_like(loss_ref)

            e = x2 - t_ref[...]
            x2_ref[...] = e * (1.0 / D_MODEL)
            loss_ref[...] += 0.5 * jnp.sum(jnp.mean(e * e, axis=-1, keepdims=True), axis=0, keepdims=True)

    act_shape = _sds((N_SHARD, T, FF_SH), BF16)
    weights = [_resident((N_SHARD, FF_SH, D_MODEL))] * 3
    return pl.pallas_call(
        body, name="ffn_fwd_loss" if last else "ffn_fwd", grid=(T // tm,),
        in_specs=[_rows(tm, D_MODEL), _full((8, D_MODEL)), _full((8, D_MODEL)), *weights]
        + ([_rows(tm, D_MODEL)] if last else []),
        out_specs=[_rows(tm, D_MODEL), _sh_rows(tm), _sh_rows(tm), _sh_rows(tm), _rows(tm, D_MODEL),
                   _rows(tm, D_MODEL)] + ([_full((8, 128))] if last else []),
        out_shape=[_sds((T, D_MODEL), BF16), act_shape, act_shape, act_shape, _sds((T, D_MODEL), F32),
                   _sds((T, D_MODEL), F32)] + ([_sds((8, 128), F32)] if last else []),
        compiler_params=_cp("arbitrary" if last else "parallel", vmem=FFN_VMEM_LIMIT),
    )(x1, mod8, g8, wg, wu, wd, *([target] if last else []))


def _ffn_bwd(dx2, f, ga, gb, x1, mod8, g8, wg, wu, wd):
    T = dx2.shape[0]
    tm = min(T, 256)

    def body(dx_ref, f_ref, ga_ref, gb_ref, x_ref, mod_ref, g_ref, wg_ref, wu_ref, wd_ref,
             dx1_ref, df_ref, da_ref, db_ref, red_ref):
        @pl.when(pl.program_id(0) == 0)
        def _():
            red_ref[...] = jnp.zeros_like(red_ref)

        dx = dx_ref[...]
        fv = f_ref[...]
        rstd = _rstd(fv)
        fhat = fv * rstd
        gpost = g_ref[3:4, :]
        gate = mod_ref[5:6, :]
        df, s_post = _norm_gain_bwd(dx, fhat, rstd, gate * gpost)
        red_ref[0:1, :] += gpost * s_post
        red_ref[1:2, :] += gate * s_post
        dfb = df.astype(BF16)
        df_ref[...] = dfb
        dh = jnp.zeros((tm, D_MODEL), F32)
        for s in range(N_SHARD):
            dact = _nt(dfb, wd_ref[s])
            da = (dact * ga_ref[s].astype(F32)).astype(BF16)
            db = (dact * gb_ref[s].astype(F32)).astype(BF16)
            da_ref[s] = da
            db_ref[s] = db
            dh = dh + _mm(da, wg_ref[s]) + _mm(db, wu_ref[s])
        xf = x_ref[...]
        rstd1 = _rstd(xf)
        xhat = xf * rstd1
        gpre = g_ref[2:3, :]
        scale1 = 1.0 + mod_ref[4:5, :]
        dxn, s_pre = _norm_gain_bwd(dh, xhat, rstd1, scale1 * gpre)
        red_ref[2:3, :] += _colsum(dh)
        red_ref[3:4, :] += gpre * s_pre
        red_ref[4:5, :] += scale1 * s_pre
        dx1_ref[...] = dx + dxn

    act_shape = _sds((N_SHARD, T, FF_SH), BF16)
    return pl.pallas_call(
        body, name="ffn_bwd", grid=(T // tm,),
        in_specs=[_rows(tm, D_MODEL), _rows(tm, D_MODEL), _sh_rows(tm), _sh_rows(tm), _rows(tm, D_MODEL),
                  _full((8, D_MODEL)), _full((8, D_MODEL)),
                  _resident((N_SHARD, FF_SH, D_MODEL)), _resident((N_SHARD, FF_SH, D_MODEL)),
                  _resident((N_SHARD, FF_SH, D_MODEL))],
        out_specs=[_rows(tm, D_MODEL), _rows(tm, D_MODEL), _sh_rows(tm), _sh_rows(tm), _full((8, D_MODEL))],
        out_shape=[_sds((T, D_MODEL), F32), _sds((T, D_MODEL), BF16), act_shape, act_shape, _sds((8, D_MODEL), F32)],
        compiler_params=_cp("arbitrary"),
    )(dx2, f, ga, gb, x1, mod8, g8, wg, wu, wd)


def _wgrad(a, b, name):
    T, K = a.shape
    N = b.shape[1]
    tt = min(T, WGRAD_TOKENS)
    tk = next(c for c in (640, 512, 256, 128) if K % c == 0)

    def body(a_ref, b_ref, o_ref):
        @pl.when(pl.program_id(1) == 0)
        def _():
            o_ref[...] = jnp.zeros_like(o_ref)

        o_ref[...] += _tn(a_ref[...], b_ref[...])

    return pl.pallas_call(
        body, name=name, grid=(K // tk, T // tt),
        in_specs=[pl.BlockSpec((tt, tk), lambda i, t: (t, i)), pl.BlockSpec((tt, N), lambda i, t: (t, 0))],
        out_specs=pl.BlockSpec((tk, N), lambda i, t: (i, 0)),
        out_shape=_sds((K, N), F32),
        compiler_params=_cp("parallel", "arbitrary"),
    )(a, b)


def _wgrad_rows(a, b, name):
    T, N = b.shape
    k = a.shape[2]
    tt = min(T, WGRAD_TOKENS)

    def body(a_ref, b_ref, o_ref):
        @pl.when(pl.program_id(1) == 0)
        def _():
            o_ref[...] = jnp.zeros_like(o_ref)

        o_ref[...] += _tn(a_ref[...], b_ref[...])

    return pl.pallas_call(
        body, name=name, grid=(N_SHARD, T // tt),
        in_specs=[pl.BlockSpec((None, tt, k), lambda s, t: (s, t, 0)), pl.BlockSpec((tt, N), lambda s, t: (t, 0))],
        out_specs=pl.BlockSpec((None, k, N), lambda s, t: (s, 0, 0)),
        out_shape=_sds((N_SHARD, k, N), F32),
        compiler_params=_cp("parallel", "arbitrary"),
    )(a, b)


def _mix_bwd(dx1, mix, mod8, g8, w_out):
    T = dx1.shape[0]
    tm = min(T, 512)

    def body(dx_ref, mix_ref, mod_ref, g_ref, w_ref, dmix_ref, da_ref, dp_ref, red_ref):
        @pl.when(pl.program_id(0) == 0)
        def _():
            red_ref[...] = jnp.zeros_like(red_ref)

        dx = dx_ref[...]
        mv = mix_ref[...]
        rstd = _rstd(mv)
        mhat = mv * rstd
        gpost = g_ref[1:2, :]
        gate = mod_ref[2:3, :]
        dm, s_post = _norm_gain_bwd(dx, mhat, rstd, gate * gpost)
        red_ref[0:1, :] += gpost * s_post
        red_ref[1:2, :] += gate * s_post
        dmb = dm.astype(BF16)
        dmix_ref[...] = dmb
        da_ref[...] = _nt(dmb, w_ref[0:ATTN_W, :]).astype(BF16)
        dp_ref[...] = _nt(dmb, w_ref[ATTN_W:, :]).astype(BF16)

    return pl.pallas_call(
        body, name="mix_bwd", grid=(T // tm,),
        in_specs=[_rows(tm, D_MODEL), _rows(tm, D_MODEL), _full((8, D_MODEL)), _full((8, D_MODEL)),
                  _resident((D_MODEL, D_MODEL))],
        out_specs=[_rows(tm, D_MODEL), _rows(tm, ATTN_W), _rows(tm, POOL_W), _full((8, D_MODEL))],
        out_shape=[_sds((T, D_MODEL), BF16), _sds((T, ATTN_W), BF16), _sds((T, POOL_W), BF16),
                   _sds((8, D_MODEL), F32)],
        compiler_params=_cp("arbitrary"),
    )(dx1, mix, mod8, g8, w_out)


def _attn_bwd(q, kd, vd, lse, dattn, sink_b):
    T = q.shape[0]
    nb = T // BLK

    def body(q_ref, do_ref, lse_ref, kp_ref, kc_ref, vp_ref, vc_ref, sk_ref,
             dq_ref, dk_ref, dv_ref, dsk_ref, carry_k, carry_v):
        n = pl.program_id(0)

        @pl.when(n == 0)
        def _():
            carry_k[...] = jnp.zeros_like(carry_k)
            carry_v[...] = jnp.zeros_like(carry_v)
            dsk_ref[...] = jnp.zeros_like(dsk_ref)

        @pl.when(n < nb)
        def _():
            valid = _band_mask(n)
            for j in range(N_HEADS // GROUP):
                lanes = slice(j * 128, (j + 1) * 128)
                kcat = jnp.concatenate([kp_ref[:, lanes], kc_ref[:, lanes]], axis=0)
                vcat = jnp.concatenate([vp_ref[:, lanes], vc_ref[:, lanes]], axis=0)
                qs = _stack_heads(q_ref, j)
                dos = _stack_heads(do_ref, j)
                lse = _head_row(lse_ref, j)
                p = jnp.exp(jnp.where(valid, _nt(kcat, qs), NEG_INF) - lse)
                dp = _nt(vcat, dos)
                delta = jnp.sum(p * dp, axis=0, keepdims=True)
                ds = (p * (dp - delta)).astype(BF16)
                sink_term = jnp.exp(_head_row(sk_ref, j) - lse) * delta
                for r in range(GROUP):
                    h = GROUP * j + r
                    dsk_ref[h:h + 1, :] += -jnp.sum(sink_term[:, r * 128:(r + 1) * 128], axis=1, keepdims=True)
                dq_ref[:, 2 * j * 128:(2 * j + 2) * 128] = jnp.concatenate(_unstack_heads(_tn(ds, kcat)), axis=1)
                dk = _mm(ds, qs)
                dv = _mm(p.astype(BF16), dos)
                dk_ref[:, lanes] = carry_k[:, lanes] + dk[0:BLK]
                dv_ref[:, lanes] = carry_v[:, lanes] + dv[0:BLK]
                carry_k[:, lanes] = dk[BLK:]
                carry_v[:, lanes] = dv[BLK:]

        @pl.when(n == nb)
        def _():
            dk_ref[...] = carry_k[...]
            dv_ref[...] = carry_v[...]

    cur = lambda n: (jnp.minimum(n, nb - 1), 0)
    prev = lambda n: (jnp.maximum(n - 1, 0), 0)
    return pl.pallas_call(
        body, name="attn_bwd", grid=(nb + 1,),
        in_specs=[pl.BlockSpec((BLK, ATTN_W), cur), pl.BlockSpec((BLK, ATTN_W), cur), pl.BlockSpec((N_HEADS, 128), cur),
                  pl.BlockSpec((BLK, KVD_W), prev), pl.BlockSpec((BLK, KVD_W), cur),
                  pl.BlockSpec((BLK, KVD_W), prev), pl.BlockSpec((BLK, KVD_W), cur),
                  _full((8, 128))],
        out_specs=[pl.BlockSpec((BLK, ATTN_W), cur), pl.BlockSpec((BLK, KVD_W), prev),
                   pl.BlockSpec((BLK, KVD_W), prev), _full((8, 128))],
        out_shape=[_sds((T, ATTN_W), F32), _sds((T, KVD_W), F32), _sds((T, KVD_W), F32), _sds((8, 128), F32)],
        scratch_shapes=[pltpu.VMEM((BLK, KVD_W), F32), pltpu.VMEM((BLK, KVD_W), F32)],
        compiler_params=_cp("arbitrary"),
    )(q, dattn, lse, kd, kd, vd, vd, sink_b)


def _pool_bwd(dpool, pooled, pool_w, pool_scale):
    T = dpool.shape[0]
    tm = min(T, 512)
    nbk = T // tm
    ext_rows = tm + HALO

    def body(dp_ref, pl_ref, w_ref, sc_ref, du_ref, dw_ref, dsc_ref, halo):
        i = pl.program_id(0)

        @pl.when(i == 0)
        def _():
            halo[...] = jnp.zeros_like(halo)
            dw_ref[...] = jnp.zeros_like(dw_ref)
            dsc_ref[...] = jnp.zeros_like(dsc_ref)

        blk = nbk - 1 - i
        tpos = (blk * tm + lax.broadcasted_iota(jnp.int32, (tm, 1), 0)).astype(F32)
        for g, w in enumerate(POOL_WINDOWS):
            lanes = slice(g * 128, (g + 1) * 128)
            dp = dp_ref[:, lanes].astype(F32)
            pb = pl_ref[:, lanes]
            wg = w_ref[g].astype(BF16)
            z = _mm(pb, wg)
            dsc_ref[0:1, lanes] += _colsum(dp * z)
            dz = (dp * sc_ref[:, lanes]).astype(BF16)
            dw_ref[g] += _tn(pb, dz)
            dpl = _nt(dz, wg)
            e = dpl / jnp.minimum(tpos + 1.0, float(w))
            s = jnp.concatenate([e, halo[:, lanes]], axis=0)
            halo[:, lanes] = e[0:HALO, :]
            sh = 1
            while sh < w:
                s = s + pltpu.roll(s, ext_rows - sh, 0)
                sh *= 2
            du_ref[:, lanes] = s[0:tm, :] - dpl

    rev = lambda i: (nbk - 1 - i, 0)
    return pl.pallas_call(
        body, name="pool_bwd", grid=(nbk,),
        in_specs=[pl.BlockSpec((tm, POOL_W), rev), pl.BlockSpec((tm, POOL_W), rev),
                  _full((4, 128, 128)), _full((1, POOL_W))],
        out_specs=[pl.BlockSpec((tm, POOL_W), rev), _full((4, 128, 128)), _full((8, POOL_W))],
        out_shape=[_sds((T, POOL_W), F32), _sds((4, 128, 128), F32), _sds((8, POOL_W), F32)],
        scratch_shapes=[pltpu.VMEM((HALO, POOL_W), F32)],
        compiler_params=_cp("arbitrary"),
    )(dpool, pooled, pool_w, pool_scale)


def _in_bwd(dq, dk, dv, du, rc, rs1, rs2, x, dx1, mod8, g8, w_in):
    T = x.shape[0]
    tm = min(T, 512)

    def body(dq_ref, dk_ref, dv_ref, du_ref, c_ref, s1_ref, s2_ref, x_ref, dx1_ref, mod_ref, g_ref, w_ref,
             dx_ref, dproj_ref, red_ref, dbin_ref):
        @pl.when(pl.program_id(0) == 0)
        def _():
            red_ref[...] = jnp.zeros_like(red_ref)
            dbin_ref[...] = jnp.zeros_like(dbin_ref)

        c = c_ref[...]
        s1 = s1_ref[...]
        s2 = s2_ref[...]
        dqp = _rot_bwd(dq_ref[...] * (HEAD ** -0.5), jnp.tile(c, (1, 4)), jnp.tile(s1, (1, 4)), jnp.tile(s2, (1, 4)))
        dkp = _rot_bwd(_fold_dup(dk_ref[...]), c, s1, s2)
        pieces = ((0, ATTN_W, dqp), (ATTN_W, ATTN_W + KV_W, dkp),
                  (ATTN_W + KV_W, ATTN_W + 2 * KV_W, _fold_dup(dv_ref[...])), (ATTN_W + 2 * KV_W, IN_W, du_ref[...]))
        dh = jnp.zeros((tm, D_MODEL), F32)
        for lo, hi, val in pieces:
            dbin_ref[0:1, lo:hi] += _colsum(val)
            vb = val.astype(BF16)
            dproj_ref[:, lo:hi] = vb
            dh = dh + _mm(vb, w_ref[lo:hi, :])
        xf = x_ref[...]
        rstd = _rstd(xf)
        xhat = xf * rstd
        gpre = g_ref[0:1, :]
        scale1 = 1.0 + mod_ref[1:2, :]
        dxn, s_pre = _norm_gain_bwd(dh, xhat, rstd, scale1 * gpre)
        red_ref[0:1, :] += _colsum(dh)
        red_ref[1:2, :] += gpre * s_pre
        red_ref[2:3, :] += scale1 * s_pre
        dx_ref[...] = dx1_ref[...] + dxn

    return pl.pallas_call(
        body, name="in_bwd", grid=(T // tm,),
        in_specs=[_rows(tm, ATTN_W), _rows(tm, KVD_W), _rows(tm, KVD_W), _rows(tm, POOL_W),
                  _rows(tm, 128), _rows(tm, 128), _rows(tm, 128), _rows(tm, D_MODEL), _rows(tm, D_MODEL),
                  _full((8, D_MODEL)), _full((8, D_MODEL)), _resident((IN_W, D_MODEL))],
        out_specs=[_rows(tm, D_MODEL), _rows(tm, IN_W), _full((8, D_MODEL)), _full((8, IN_W))],
        out_shape=[_sds((T, D_MODEL), F32), _sds((T, IN_W), BF16), _sds((8, D_MODEL), F32), _sds((8, IN_W), F32)],
        compiler_params=_cp("arbitrary"),
    )(dq, dk, dv, du, rc, rs1, rs2, x, dx1, mod8, g8, w_in)


def _mod_fwd(c_all, ada_w, ada_b_sh):
    tn = 512

    def body(c_ref, w_ref, b_ref, o_ref):
        cv = c_ref[...]
        ca = (cv * jax.nn.sigmoid(cv)).astype(BF16)
        o_ref[...] = _mm(ca, w_ref[...].astype(BF16)) + b_ref[...]

    return pl.pallas_call(
        body, name="mod_fwd", grid=(2, ADA_SH // tn),
        in_specs=[_full((8, D_MODEL)), pl.BlockSpec((None, D_MODEL, tn), lambda l, j: (l, 0, j)),
                  pl.BlockSpec((None, 1, tn), lambda l, j: (l, 0, j))],
        out_specs=pl.BlockSpec((None, 8, tn), lambda l, j: (l, 0, j)),
        out_shape=_sds((2, 8, ADA_SH), F32),
        compiler_params=_cp("parallel", "parallel"),
    )(c_all, ada_w, ada_b_sh)


def _ada_wgrad(c_all_t, dmod_sh):
    tn = 512

    def body(c_ref, d_ref, o_ref):
        cv = c_ref[...]
        ca = cv * jax.nn.sigmoid(cv)
        o_ref[...] = jnp.dot(ca, d_ref[...], preferred_element_type=F32, precision=lax.Precision.HIGHEST)

    return pl.pallas_call(
        body, name="ada_wgrad", grid=(2, ADA_SH // tn),
        in_specs=[_full((D_MODEL, 8)), pl.BlockSpec((None, 8, tn), lambda l, j: (l, 0, j))],
        out_specs=pl.BlockSpec((None, D_MODEL, tn), lambda l, j: (l, 0, j)),
        out_shape=_sds((2, D_MODEL, ADA_SH), F32),
        compiler_params=_cp("parallel", "parallel"),
    )(c_all_t, dmod_sh)


def _sum_devices(g):
    R = g.shape[1]

    def body(g_ref, o_ref):
        acc = g_ref[0]
        for d in range(1, N_DEV):
            acc = acc + g_ref[d]
        o_ref[...] = acc

    return pl.pallas_call(
        body, name="sum_devices", grid=(1,),
        in_specs=[_full((N_DEV, R, 128))], out_specs=_full((R, 128)), out_shape=_sds((R, 128), F32),
        compiler_params=_cp("arbitrary"),
    )(g)


def _adamw(w, g, m, v, name):
    R, C = w.shape
    tr = R
    for cand in (256, 128, 64, 32, 16, 8):
        if R % cand == 0 and cand * C * 4 <= 2 * 1024 * 1024:
            tr = cand
            break

    def body(w_ref, g_ref, m_ref, v_ref, d_ref, nm_ref, nv_ref):
        gv = g_ref[...]
        mn = ADAM_B1 * m_ref[...] + (1.0 - ADAM_B1) * gv
        vn = ADAM_B2 * v_ref[...] + (1.0 - ADAM_B2) * (gv * gv)
        m_hat = mn / (1.0 - ADAM_B1 ** ADAM_STEP)
        v_hat = vn / (1.0 - ADAM_B2 ** ADAM_STEP)
        d_ref[...] = -ADAM_LR * (m_hat / (jnp.sqrt(v_hat) + ADAM_EPS) + ADAM_WD * w_ref[...])
        nm_ref[...] = mn
        nv_ref[...] = vn

    spec = pl.BlockSpec((tr, C), lambda i: (i, 0))
    out = _sds((R, C), F32)
    return pl.pallas_call(
        body, name=name, grid=(R // tr,),
        in_specs=[spec] * 4, out_specs=[spec] * 3, out_shape=[out] * 3,
        compiler_params=_cp("parallel"),
    )(w, g, m, v)


def _adamw_nd(w, g, m, v, name):
    shape = w.shape
    if w.ndim == 2 and shape[1] < 128:
        view = (1, shape[0] * shape[1])
    else:
        view = (-1, shape[-1])
    outs = _adamw(*[t.reshape(view) for t in (w, g, m, v)], name=name)
    return [o.reshape(shape) for o in outs]


def _coords():
    return lax.axis_index("x"), lax.axis_index("y"), lax.axis_index("c")


def _other_chips(x, y):
    return [(1 - x, y), (x, 1 - y), (1 - x, 1 - y)]


def _allgather8(blk, name):
    m_per, n = blk.shape

    def body(x_ref, out_ref, send_sems, recv_sems, local_sem):
        x, y, c = _coords()
        me, sibling = (x, y, c), (x, y, 1 - c)
        chips = _other_chips(x, y)

        def rows(px, py, pc):
            return out_ref.at[pl.ds((4 * px + 2 * py + pc) * m_per, m_per), :]

        def copy(k, block, to, src=None):
            return pltpu.make_async_remote_copy(
                src_ref=rows(*block) if src is None else src, dst_ref=rows(*block),
                send_sem=send_sems.at[k], recv_sem=recv_sems.at[k], device_id=to, device_id_type=MESH)

        mine = pltpu.make_async_copy(x_ref, rows(*me), local_sem)
        mine.start()
        first = [copy(0, me, sibling, src=x_ref)]
        first += [copy(1 + j, me, (*chip, c), src=x_ref) for j, chip in enumerate(chips)]
        for cp in first:
            cp.start()
        passed = [copy(4 + j, (*chip, c), sibling) for j, chip in enumerate(chips)]
        for j, chip in enumerate(chips):
            copy(1 + j, (*chip, c), me).wait_recv()
            passed[j].start()
        copy(0, sibling, me).wait_recv()
        for j, chip in enumerate(chips):
            copy(4 + j, (*chip, 1 - c), me).wait_recv()
        for cp in first + passed:
            cp.wait_send()
        mine.wait()

    return pl.pallas_call(
        body, name=name,
        out_shape=_sds((N_DEV * m_per, n), blk.dtype),
        in_specs=[pl.BlockSpec(memory_space=pltpu.VMEM)],
        out_specs=pl.BlockSpec(memory_space=pltpu.VMEM),
        scratch_shapes=[pltpu.SemaphoreType.DMA((7,)), pltpu.SemaphoreType.DMA((7,)), pltpu.SemaphoreType.DMA],
        compiler_params=pltpu.CompilerParams(vmem_limit_bytes=VMEM_LIMIT),
    )(blk)


def _row_tile(r, n):
    for cand in range(r, 15, -16):
        if r % cand == 0 and cand % 16 == 0 and cand * n * 4 <= 2 * 1024 * 1024:
            return cand
    return r


def _cast_slot(w, chip, name):
    r, n = w.shape
    tr = _row_tile(r, n)

    def body(chip_ref, w_ref, o_ref):
        o_ref[...] = w_ref[...].astype(BF16)

    grid_spec = pltpu.PrefetchScalarGridSpec(
        num_scalar_prefetch=1, grid=(r // tr,),
        in_specs=[pl.BlockSpec((tr, n), lambda i, ch: (i, 0))],
        out_specs=pl.BlockSpec((None, tr, n), lambda i, ch: (ch[0], i, 0)))
    return pl.pallas_call(
        body, name=name, grid_spec=grid_spec, out_shape=_sds((N_SHARD, r, n), BF16),
        compiler_params=_cp("arbitrary"),
    )(chip, w)


def _allgather_weights(bufs, name):
    nt = len(bufs)
    hom = [pl.BlockSpec(memory_space=pl.ANY)] * nt

    def body(*refs):
        outs = refs[nt:2 * nt]
        send_sems, recv_sems = refs[2 * nt:]
        x, y, c = _coords()
        sibling = (x, y, 1 - c)
        chips = _other_chips(x, y)

        def copy(t, k, block_chip, hc, to):
            r = outs[t].shape[1] // 2
            blk = outs[t].at[2 * block_chip[0] + block_chip[1], pl.ds(hc * r, r)]
            return pltpu.make_async_remote_copy(
                src_ref=blk, dst_ref=blk,
                send_sem=send_sems.at[t, k], recv_sem=recv_sems.at[t, k], device_id=to, device_id_type=MESH)

        started = []
        for t in range(nt):
            for j, chip in enumerate(chips):
                cp = copy(t, j, (x, y), c, (*chip, c))
                cp.start()
                started.append(cp)
        for t in range(nt):
            for j, chip in enumerate(chips):
                copy(t, j, chip, c, sibling).wait_recv()
                fw = copy(t, 3 + j, chip, c, sibling)
                fw.start()
                started.append(fw)
        for t in range(nt):
            for j, chip in enumerate(chips):
                copy(t, 3 + j, chip, 1 - c, sibling).wait_recv()
        for cp in started:
            cp.wait_send()

    return pl.pallas_call(
        body, name=name,
        out_shape=[_sds(b.shape, b.dtype) for b in bufs],
        in_specs=hom, out_specs=hom,
        input_output_aliases={t: t for t in range(nt)},
        scratch_shapes=[pltpu.SemaphoreType.DMA((nt, 6)), pltpu.SemaphoreType.DMA((nt, 6))],
    )(*bufs)


def _join_halves(tots, name):
    nt = len(tots)
    hom = [pl.BlockSpec(memory_space=pl.ANY)] * nt

    def body(*refs):
        outs = refs[nt:2 * nt]
        send_sems, recv_sems = refs[2 * nt:]
        x, y, c = _coords()
        sibling = (x, y, 1 - c)
        cps = []
        for t in range(nt):
            cp = pltpu.make_async_remote_copy(
                src_ref=outs[t].at[c], dst_ref=outs[t].at[c],
                send_sem=send_sems.at[t], recv_sem=recv_sems.at[t], device_id=sibling, device_id_type=MESH)
            cp.start()
            cps.append(cp)
        for t in range(nt):
            pltpu.make_async_remote_copy(
                src_ref=outs[t].at[c], dst_ref=outs[t].at[1 - c],
                send_sem=send_sems.at[t], recv_sem=recv_sems.at[t], device_id=sibling, device_id_type=MESH).wait_recv()
        for cp in cps:
            cp.wait_send()

    return pl.pallas_call(
        body, name=name,
        out_shape=[_sds(t.shape, t.dtype) for t in tots],
        in_specs=hom, out_specs=hom,
        input_output_aliases={t: t for t in range(nt)},
        scratch_shapes=[pltpu.SemaphoreType.DMA((nt,)), pltpu.SemaphoreType.DMA((nt,))],
    )(*tots)


def _pair_sum(g, recv, core, chip, name):
    _, _, r, n = g.shape
    tr = _row_tile(r, n)

    def body(core_ref, chip_ref, g_ref, r_ref, sb_ref, own_ref):
        tot = g_ref[...] + r_ref[...]
        sb_ref[...] = tot.astype(BF16)

        @pl.when(pl.program_id(1) == chip_ref[0])
        def _():
            own_ref[...] = tot

    grid_spec = pltpu.PrefetchScalarGridSpec(
        num_scalar_prefetch=2, grid=(r // tr, N_SHARD),
        in_specs=[pl.BlockSpec((None, None, tr, n), lambda i, s, co, ch: (s, co[0], i, 0)),
                  pl.BlockSpec((None, tr, n), lambda i, s, co, ch: (s, i, 0))],
        out_specs=[pl.BlockSpec((None, tr, n), lambda i, s, co, ch: (s, i, 0)),
                   pl.BlockSpec((tr, n), lambda i, s, co, ch: (i, 0))])
    return pl.pallas_call(
        body, name=name, grid_spec=grid_spec,
        out_shape=[_sds((N_SHARD, r, n), BF16), _sds((r, n), F32)],
        compiler_params=_cp("arbitrary", "arbitrary"),
    )(core, chip, g, recv)


def _chip_sum(own, recv, core, name):
    r, n = own.shape
    tr = _row_tile(r, n)

    def body(core_ref, o_ref, r_ref, t_ref):
        acc = o_ref[...]
        for j in range(3):
            acc = acc + r_ref[j].astype(F32)
        t_ref[...] = acc

    grid_spec = pltpu.PrefetchScalarGridSpec(
        num_scalar_prefetch=1, grid=(r // tr,),
        in_specs=[pl.BlockSpec((tr, n), lambda i, co: (i, 0)), pl.BlockSpec((3, tr, n), lambda i, co: (0, i, 0))],
        out_specs=pl.BlockSpec((None, tr, n), lambda i, co: (co[0], i, 0)))
    return pl.pallas_call(
        body, name=name, grid_spec=grid_spec, out_shape=_sds((2, r, n), F32),
        compiler_params=_cp("arbitrary"),
    )(core, own, recv)


_HBM = pl.BlockSpec(memory_space=pltpu.HBM)
_SEM = pl.BlockSpec(memory_space=pltpu.SEMAPHORE)
_EFFECT = pltpu.SideEffectType.DATAFLOW_SIDE_EFFECTING


def _ici_copies(srcs, dsts, send_sems, recv_sems, send_view, recv_view):
    x, y, c = _coords()
    out = []
    if send_view is None:
        for t in range(len(srcs)):
            r = srcs[t].shape[1] // 2
            out.append(pltpu.make_async_remote_copy(
                src_ref=srcs[t].at[:, pl.ds((1 - c) * r, r)], dst_ref=dsts[t],
                send_sem=send_sems.at[3 * t], recv_sem=recv_sems.at[3 * t],
                device_id=(x, y, 1 - c), device_id_type=MESH))
        return out
    for t in range(len(srcs)):
        for j, chip in enumerate(_other_chips(x, y)):
            out.append(pltpu.make_async_remote_copy(
                src_ref=send_view(srcs[t], chip, j, (x, y), c), dst_ref=recv_view(dsts[t], chip, j, (x, y), c),
                send_sem=send_sems.at[3 * t + j], recv_sem=recv_sems.at[3 * t + j],
                device_id=(*chip, c), device_id_type=MESH))
    return out


def _ici_start(srcs, dsts, after, send_view, recv_view, name):
    nt = len(srcs)
    inplace = dsts is None
    nbuf = nt if inplace else 2 * nt

    def body(*refs):
        send_sems, recv_sems = refs[nbuf + 1], refs[nbuf + 2]
        s_out = refs[nbuf + 3:nbuf + 3 + nt]
        d_out = s_out if inplace else refs[nbuf + 3 + nt:nbuf + 3 + 2 * nt]
        token = refs[-1]
        for cp in _ici_copies(s_out, d_out, send_sems, recv_sems, send_view, recv_view):
            cp.start()
        token[...] = jnp.zeros_like(token)

    bufs = list(srcs) + ([] if inplace else list(dsts))
    res = pl.pallas_call(
        body, name=name,
        out_shape=(pltpu.SemaphoreType.DMA((3 * nt,)), pltpu.SemaphoreType.DMA((3 * nt,)),
                   *[pltpu.HBM(b.shape, b.dtype) for b in bufs], _sds((8, 128), F32)),
        in_specs=[_HBM] * nbuf + [pl.BlockSpec(memory_space=pl.ANY)],
        out_specs=(_SEM, _SEM, *[_HBM] * nbuf, pl.BlockSpec(memory_space=pltpu.VMEM)),
        input_output_aliases={i: 2 + i for i in range(nbuf)},
        compiler_params=pltpu.CompilerParams(has_side_effects=_EFFECT),
    )(*[pltpu.with_memory_space_constraint(b, pltpu.HBM) for b in bufs], after)
    send_sems, recv_sems = res[0], res[1]
    s_thru = list(res[2:2 + nt])
    d_thru = s_thru if inplace else list(res[2 + nt:2 + 2 * nt])
    return send_sems, recv_sems, s_thru, d_thru, res[-1]


def _ici_wait(send_sems, recv_sems, srcs, dsts, after, send_view, recv_view, name):
    nt = len(srcs)
    inplace = dsts is None
    nbuf = nt if inplace else 2 * nt

    def body(*refs):
        send_ref, recv_ref = refs[nbuf], refs[nbuf + 1]
        s_out = refs[nbuf + 3:nbuf + 3 + nt]
        d_out = s_out if inplace else refs[nbuf + 3 + nt:nbuf + 3 + 2 * nt]
        for cp in _ici_copies(s_out, d_out, send_ref, recv_ref, send_view, recv_view):
            cp.wait_send()
            cp.wait_recv()

    bufs = list(srcs) + ([] if inplace else list(dsts))
    res = pl.pallas_call(
        body, name=name,
        out_shape=tuple(pltpu.HBM(b.shape, b.dtype) for b in bufs),
        in_specs=[_HBM] * nbuf + [_SEM, _SEM, pl.BlockSpec(memory_space=pl.ANY)],
        out_specs=tuple([_HBM] * nbuf),
        input_output_aliases={i: i for i in range(nbuf)},
        compiler_params=pltpu.CompilerParams(has_side_effects=_EFFECT),
    )(*bufs, send_sems, recv_sems, after)
    return list(res[:nt]) if inplace else (list(res[:nt]), list(res[nt:]))


def _w_half(buf, chip, c):
    r = buf.shape[1] // 2
    return buf.at[2 * chip[0] + chip[1], pl.ds(c * r, r)]


def _ag_send_view(buf, chip, j, me, c):
    return _w_half(buf, me, c)


def _ag_recv_view(buf, chip, j, me, c):
    return _w_half(buf, me, c)


def _rs_send_view(buf, chip, j, me, c):
    return buf.at[2 * chip[0] + chip[1]]


def _rs_recv_view(buf, chip, j, me, c):
    return buf.at[j]


def _ag_forward(bufs, name):
    nt = len(bufs)
    hom = [pl.BlockSpec(memory_space=pl.ANY)] * nt

    def body(*refs):
        outs = refs[nt:2 * nt]
        send_sems, recv_sems = refs[2 * nt:]
        x, y, c = _coords()
        sibling = (x, y, 1 - c)
        chips = _other_chips(x, y)

        def copy(t, j, hc):
            blk = _w_half(outs[t], chips[j], hc)
            return pltpu.make_async_remote_copy(
                src_ref=blk, dst_ref=blk, send_sem=send_sems.at[t, j], recv_sem=recv_sems.at[t, j],
                device_id=sibling, device_id_type=MESH)

        started = [copy(t, j, c) for t in range(nt) for j in range(3)]
        for cp in started:
            cp.start()
        for t in range(nt):
            for j in range(3):
                copy(t, j, 1 - c).wait_recv()
        for cp in started:
            cp.wait_send()

    return pl.pallas_call(
        body, name=name,
        out_shape=[_sds(b.shape, b.dtype) for b in bufs],
        in_specs=hom, out_specs=hom,
        input_output_aliases={t: t for t in range(nt)},
        scratch_shapes=[pltpu.SemaphoreType.DMA((nt, 3)), pltpu.SemaphoreType.DMA((nt, 3))],
    )(*bufs)


def _rs_swap_begin(grads, after, tag):
    land = [lax.empty((N_SHARD, g.shape[1] // 2, g.shape[2]), g.dtype) for g in grads]
    send_sems, recv_sems, s_thru, d_thru, token = _ici_start(grads, land, after, None, None, name="rs_swapgo_" + tag)
    return dict(sems=(send_sems, recv_sems), grads=s_thru, land=d_thru, tag=tag), token


def _rs_scatter_begin(swap, after):
    tag = swap["tag"]
    x, y, c = _coords()
    core = jnp.reshape(c, (1,)).astype(jnp.int32)
    chip = jnp.reshape(2 * x + y, (1,)).astype(jnp.int32)
    grads, recv = _ici_wait(*swap["sems"], swap["grads"], swap["land"], after, None, None, name="rs_swapend_" + tag)
    sums, owns = [], []
    for t, (g, rv) in enumerate(zip(grads, recv)):
        r = g.shape[1] // 2
        sb, own = _pair_sum(g.reshape(N_SHARD, 2, r, g.shape[2]), rv, core, chip, name=f"rs_pair_{tag}_{t}")
        sums.append(sb)
        owns.append(own)
    land = [lax.empty((3,) + s.shape[1:], s.dtype) for s in sums]
    send_sems, recv_sems, s_thru, d_thru, token = _ici_start(
        sums, land, after, _rs_send_view, _rs_recv_view, name="rs_start_" + tag)
    return dict(sems=(send_sems, recv_sems), sums=s_thru, land=d_thru, owns=owns, core=core, tag=tag), token


def _rs_end(state, after):
    tag = state["tag"]
    _, got = _ici_wait(*state["sems"], state["sums"], state["land"], after, _rs_send_view, _rs_recv_view,
                       name="rs_wait_" + tag)
    tots = [_chip_sum(o, gt, state["core"], name=f"rs_chip_{tag}_{t}")
            for t, (o, gt) in enumerate(zip(state["owns"], got))]
    full = _join_halves(tots, name="rs_join_" + tag)
    return [f.reshape(2 * f.shape[1], f.shape[2]) for f in full]


def _rope_lane_table():
    d = jnp.arange(128) % HEAD
    inv_freq = ROPE_THETA ** (-jnp.arange(0, ROT, 2, dtype=F32) / ROT)
    rot = d < ROT
    rows = [jnp.where(rot, inv_freq[d % (ROT // 2)], 0.0), rot.astype(F32),
            (d < ROT // 2).astype(F32), jnp.logical_and(d >= ROT // 2, rot).astype(F32)]
    return jnp.concatenate([jnp.stack(rows), jnp.zeros((4, 128), F32)], axis=0)


def _pad8(rows):
    return jnp.concatenate([rows, jnp.zeros((8 - rows.shape[0], rows.shape[1]), F32)], axis=0)


def kernel(x, c, positions, ada_w, ada_b, w_in, b_in, sinks, pool_w, pool_scale, w_out, w_gate, w_up, w_down, g_pre_mix, g_post_mix, g_pre_ffn, g_post_ffn, loss_target, m_ada_w, m_ada_b, m_w_in, m_b_in, m_sinks, m_pool_w, m_pool_scale, m_w_out, m_w_gate, m_w_up, m_w_down, m_g_pre_mix, m_g_post_mix, m_g_pre_ffn, m_g_post_ffn, v_ada_w, v_ada_b, v_w_in, v_b_in, v_sinks, v_pool_w, v_pool_scale, v_w_out, v_w_gate, v_w_up, v_w_down, v_g_pre_mix, v_g_post_mix, v_g_pre_ffn, v_g_post_ffn):
    T = x.shape[1]
    n_layers = ada_w.shape[0]
    ax, ay, ac = _coords()
    my_dev = 4 * ax + 2 * ay + ac
    my_chip = 2 * ax + ay
    x0 = x.reshape(T, D_MODEL)
    target = loss_target.reshape(T, D_MODEL)

    c_all = _allgather8(c.reshape(8, 128), name="ag_c").reshape(N_DEV, D_MODEL)
    ada_b_sh = lax.dynamic_slice_in_dim(ada_b, my_chip * ADA_SH, ADA_SH, axis=1).reshape(n_layers, 1, ADA_SH)
    mod_part = _mod_fwd(c_all, ada_w, ada_b_sh)
    mod_all = _allgather8(mod_part.reshape(n_layers * 8, ADA_SH), name="ag_mod")
    mod_all = mod_all.reshape(N_DEV, n_layers, 8, ADA_SH)[0::2]
    mod_mine = lax.dynamic_index_in_dim(mod_all, my_dev, axis=2, keepdims=False)
    mod = jnp.transpose(mod_mine, (1, 0, 2)).reshape(n_layers, 6, D_MODEL)

    pos_b = jnp.broadcast_to(positions.reshape(T, 1), (T, 128))
    rc, rs1, rs2 = _rope_tables(pos_b, _rope_lane_table())

    chip1 = jnp.reshape(my_chip, (1,)).astype(jnp.int32)

    def tr(t):
        return jnp.transpose(t, (0, 2, 1))

    w_in_t, w_gate_t, w_up_t = tr(w_in), tr(w_gate), tr(w_up)

    def cast_layer(l):
        return [_cast_slot(w[l], chip1, name=f"cast_{nm}{l}")
                for nm, w in (("w_in", w_in_t), ("w_out", w_out), ("w_gate", w_gate_t), ("w_up", w_up_t),
                              ("w_down", w_down))]

    def as_operands(bufs):
        gin, gout, gg, gu, gd = bufs
        return gin.reshape(IN_W, D_MODEL), gout.reshape(D_MODEL, D_MODEL), gg, gu, gd

    bufs0 = cast_layer(0)
    win0 = _allgather_weights(bufs0[:1], name="ag_w0_in")
    rest_send, rest_recv, rest_bufs, _, ag_token = _ici_start(
        bufs0[1:], None, win0[0], _ag_send_view, _ag_recv_view, name="ag_start_0")
    weights = [None] * n_layers

    saved = []
    xl = x0
    for l in range(n_layers):
        mod8 = _pad8(mod[l])
        if l + 1 < n_layers:
            ag_send, ag_recv, ag_bufs, _, ag_token = _ici_start(
                cast_layer(l + 1), None, ag_token, _ag_send_view, _ag_recv_view, name=f"ag_start_{l + 1}")
        if l == 0 or l + 1 < n_layers:
            mod8 = mod8 + ag_token[0, 0]
        g8 = _pad8(jnp.stack([g_pre_mix[l], g_post_mix[l], g_pre_ffn[l], g_post_ffn[l]]))
        sink_b = jnp.broadcast_to(sinks[l][:, None], (N_HEADS, 128))
        psc = pool_scale[l].reshape(1, POOL_W)
        win = win0[0].reshape(IN_W, D_MODEL) if l == 0 else weights[l][0]
        h, q, k, v, u = _fwd_in(xl, mod8, g8, win, b_in[l].reshape(1, IN_W), rc, rs1, rs2)
        attn, lse = _attn_fwd(q, k, v, sink_b)
        pool, pooled = _pool_fwd(u, pool_w[l], psc)
        if l == 0:
            arrived = _ici_wait(rest_send, rest_recv, rest_bufs, None, pool, _ag_send_view, _ag_recv_view,
                                name="ag_wait_0")
            weights[0] = as_operands(win0 + _ag_forward(arrived, name="ag_fwd_0"))
        win, wout, wg, wu, wd = weights[l]
        mix, x1 = _fwd_out(attn, pool, xl, wout, g8, mod8)
        if l + 1 < n_layers:
            h2, act, ga, gb, f, x2 = _ffn_fwd(x1, mod8, g8, wg, wu, wd)
        else:
            h2, act, ga, gb, f, x2, loss_tile = _ffn_fwd(x1, mod8, g8, wg, wu, wd, target=target)
        saved.append(dict(x=xl, h=h, q=q, k=k, v=v, lse=lse, attn=attn, pool=pool, pooled=pooled, mix=mix,
                          x1=x1, h2=h2, act=act, ga=ga, gb=gb, f=f, mod8=mod8, g8=g8, sink_b=sink_b, psc=psc))
        xl = x2
        if l + 1 < n_layers:
            arrived = _ici_wait(ag_send, ag_recv, ag_bufs, None, x2, _ag_send_view, _ag_recv_view,
                                name=f"ag_wait_{l + 1}")
            weights[l + 1] = as_operands(_ag_forward(arrived, name=f"ag_fwd_{l + 1}"))

    dy = xl
    loss = lax.psum(loss_tile[0, 0], ("x", "y", "c"))

    small = [None] * n_layers
    dmod_rows = [None] * n_layers
    reduced = [dict() for _ in range(n_layers)]
    att_swap = None
    dx = dy
    for l in reversed(range(n_layers)):
        s = saved[l]
        win, wout, wg, wu, wd = weights[l]
        if att_swap is not None:
            s = dict(s, mod8=s["mod8"] + att_swap[1][0, 0])
        dx1, df, da, db, red_f = _ffn_bwd(dx, s["f"], s["ga"], s["gb"], s["x1"], s["mod8"], s["g8"], wg, wu, wd)
        if att_swap is not None:
            att_scatter = _rs_scatter_begin(att_swap[0], dx1)
        g_wd = _wgrad_rows(s["act"], df, name="wgrad_down")
        g_wg = _wgrad_rows(da, s["h2"], name="wgrad_gate")
        g_wu = _wgrad_rows(db, s["h2"], name="wgrad_up")
        ffn_swap = _rs_swap_begin([g_wg, g_wu, g_wd], dx1, tag=f"{l}f")
        if att_swap is not None:
            got = _rs_end(att_scatter[0], ffn_swap[1])
            reduced[l + 1].update(w_in=got[0], w_out=got[1])
        s = dict(s, mod8=s["mod8"] + ffn_swap[1][0, 0])
        dmix, dattn, dpool, red_c = _mix_bwd(dx1, s["mix"], s["mod8"], s["g8"], wout)
        g_wout = jnp.concatenate([_wgrad(s["attn"], dmix, name="wgrad_out_a"),
                                  _wgrad(s["pool"], dmix, name="wgrad_out_p")], axis=0)
        ffn_scatter = _rs_scatter_begin(ffn_swap[0], dattn)
        dq, dk, dv, dsink = _attn_bwd(s["q"], s["k"], s["v"], s["lse"], dattn, s["sink_b"] + ffn_scatter[1][0:1, :])
        du, g_poolw, dpsc = _pool_bwd(dpool, s["pooled"], pool_w[l], s["psc"])
        dx, dproj, red_d, dbin = _in_bwd(dq, dk, dv, du, rc, rs1, rs2, s["x"], dx1, s["mod8"], s["g8"], win)
        g_win = _wgrad(dproj, s["h"], name="wgrad_in")
        g_win_sh = g_win.reshape(N_SHARD, IN_SH, D_MODEL)
        got = _rs_end(ffn_scatter[0], dproj)
        reduced[l].update(w_gate=got[0], w_up=got[1], w_down=got[2])
        att_swap = _rs_swap_begin([g_win_sh, g_wout.reshape(N_SHARD, OUT_SH, D_MODEL)], dx, tag=f"{l}a")
        dmod_rows[l] = jnp.concatenate([red_d[0], red_d[1], red_c[0], red_f[2], red_f[3], red_f[0]])
        small[l] = jnp.concatenate([red_d[2], red_c[1], red_f[4], red_f[1], dbin[0], dpsc[0], dsink[:, 0],
                                    jnp.zeros((120,), F32), g_poolw.reshape(-1)])
    grad_x = dx.reshape(1, T, D_MODEL)

    per_layer = small[0].shape[0]
    rows_small = n_layers * per_layer // 128
    rows_mod = n_layers * 6 * D_MODEL // 128
    rows_pad = -(rows_small + rows_mod) % 8
    pack = jnp.concatenate(small + dmod_rows + [jnp.zeros((rows_pad * 128,), F32)]).reshape(-1, 128)
    pack = pack + att_swap[1][0, 0]
    gathered = _allgather8(pack, name="ag_small").reshape(N_DEV, pack.shape[0], 128)
    summed = _sum_devices(gathered)
    att_scatter = _rs_scatter_begin(att_swap[0], summed)
    small_sum = summed[:rows_small].reshape(n_layers, per_layer)
    o = 0
    small_g = {}
    for nm, width in (("g_pre_mix", D_MODEL), ("g_post_mix", D_MODEL), ("g_pre_ffn", D_MODEL),
                      ("g_post_ffn", D_MODEL), ("b_in", IN_W), ("pool_scale", POOL_W), ("sinks", 128),
                      ("pool_w", 4 * 128 * 128)):
        small_g[nm] = small_sum[:, o:o + width]
        o += width
    small_g["sinks"] = small_g["sinks"][:, :N_HEADS]
    small_g["pool_w"] = small_g["pool_w"].reshape(n_layers, 4, 128, 128)
    small_g["ada_b"] = summed[rows_small:rows_small + rows_mod].reshape(n_layers, 6 * D_MODEL)
    dmod_all = gathered[:, rows_small:rows_small + rows_mod].reshape(N_DEV, n_layers, N_SHARD, ADA_SH)
    dmod_sh = lax.dynamic_index_in_dim(dmod_all, my_chip, axis=2, keepdims=False)
    g_ada_w = _ada_wgrad(jnp.transpose(c_all), jnp.transpose(dmod_sh, (1, 0, 2)))

    grads = dict(ada_w=g_ada_w, ada_b=small_g["ada_b"], b_in=small_g["b_in"], sinks=small_g["sinks"],
                 pool_w=small_g["pool_w"], pool_scale=small_g["pool_scale"], g_pre_mix=small_g["g_pre_mix"],
                 g_post_mix=small_g["g_post_mix"], g_pre_ffn=small_g["g_pre_ffn"], g_post_ffn=small_g["g_post_ffn"])
    params = dict(ada_w=(ada_w, m_ada_w, v_ada_w), ada_b=(ada_b, m_ada_b, v_ada_b), w_in=(w_in, m_w_in, v_w_in),
                  b_in=(b_in, m_b_in, v_b_in), sinks=(sinks, m_sinks, v_sinks), pool_w=(pool_w, m_pool_w, v_pool_w),
                  pool_scale=(pool_scale, m_pool_scale, v_pool_scale), w_out=(w_out, m_w_out, v_w_out),
                  w_gate=(w_gate, m_w_gate, v_w_gate), w_up=(w_up, m_w_up, v_w_up),
                  w_down=(w_down, m_w_down, v_w_down), g_pre_mix=(g_pre_mix, m_g_pre_mix, v_g_pre_mix),
                  g_post_mix=(g_post_mix, m_g_post_mix, v_g_post_mix), g_pre_ffn=(g_pre_ffn, m_g_pre_ffn, v_g_pre_ffn),
                  g_post_ffn=(g_post_ffn, m_g_post_ffn, v_g_post_ffn))
    names = list(params)
    updates = {nm: _adamw_nd(*params[nm][:1], grads[nm], *params[nm][1:], name="adamw_" + nm) for nm in grads}

    got = _rs_end(att_scatter[0], updates["ada_w"][0])
    reduced[0].update(w_in=got[0], w_out=got[1])
    for nm in ("w_in", "w_out", "w_gate", "w_up", "w_down"):
        g = jnp.stack([reduced[l][nm] for l in range(n_layers)])
        if nm in ("w_in", "w_gate", "w_up"):
            upd = _adamw_nd(tr(params[nm][0]), g, tr(params[nm][1]), tr(params[nm][2]), name="adamw_" + nm)
            grads[nm], updates[nm] = tr(g), [tr(u) for u in upd]
        else:
            grads[nm], updates[nm] = g, _adamw_nd(params[nm][0], g, *params[nm][1:], name="adamw_" + nm)
    return (loss, grad_x, *[grads[nm] for nm in names], *[updates[nm][0] for nm in names],
            *[updates[nm][1] for nm in names], *[updates[nm][2] for nm in names])
```

```python
import functools

import jax
import jax.numpy as jnp
from jax import lax
from jax.experimental import pallas as pl
from jax.experimental.pallas import tpu as pltpu

F32 = jnp.float32
BF16 = jnp.bfloat16
MESH = pl.DeviceIdType.MESH

D_MODEL = 1024
ATTN_W = 512
KV_W = 128
KVD_W = 256
POOL_W = 512
IN_W = 1280
D_FF = 2816
N_SHARD = 4
FF_SH = D_FF // N_SHARD
IN_SH = IN_W // N_SHARD
OUT_SH = D_MODEL // N_SHARD
ADA_SH = 6 * D_MODEL // N_SHARD
HEAD = 64
N_HEADS = 8
GROUP = 4
BLK = 128
POOL_WINDOWS = (2, 4, 8, 16)
HALO = 16
ROT = 16
ROPE_THETA = 500000.0
EPS = 1e-6
NEG_INF = -1e30
N_DEV = 8

ADAM_LR = 0.001
ADAM_B1 = 0.9
ADAM_B2 = 0.999
ADAM_EPS = 1e-08
ADAM_WD = 0.01
ADAM_STEP = 10

VMEM_LIMIT = 48 * 1024 * 1024
FFN_VMEM_LIMIT = 60 * 1024 * 1024
WGRAD_TOKENS = 2048


def _cp(*sem, vmem=VMEM_LIMIT):
    return pltpu.CompilerParams(dimension_semantics=sem, vmem_limit_bytes=vmem)


def _full(shape):
    nd = len(shape)
    return pl.BlockSpec(shape, lambda *_: (0,) * nd)


def _resident(shape):
    nd = len(shape)
    return pl.BlockSpec(shape, lambda *_: (0,) * nd, pipeline_mode=pl.Buffered(1))


def _rows(tm, ncol):
    return pl.BlockSpec((tm, ncol), lambda i: (i, 0))


def _sds(shape, dtype):
    return jax.ShapeDtypeStruct(shape, dtype)


def _nt(a, b):
    return lax.dot_general(a, b, (((1,), (1,)), ((), ())), preferred_element_type=F32)


def _tn(a, b):
    return lax.dot_general(a, b, (((0,), (0,)), ((), ())), preferred_element_type=F32)


def _mm(a, b):
    return jnp.dot(a, b, preferred_element_type=F32)


def _rstd(x):
    return lax.rsqrt(jnp.mean(x * x, axis=-1, keepdims=True) + EPS)


def _colsum(x):
    return jnp.sum(x, axis=0, keepdims=True)


def _norm_gain_bwd(dy, xhat, rstd, gain):
    p = dy * xhat
    dx = rstd * (dy * gain - xhat * jnp.mean(p * gain, axis=-1, keepdims=True))
    return dx, _colsum(p)


def _rope_tables(pos_b, lane_tab):
    T = pos_b.shape[0]
    tm = min(T, 1024)

    def body(pos_ref, tab_ref, c_ref, s1_ref, s2_ref):
        ang = pos_ref[...].astype(F32) * tab_ref[0:1, :]
        cs = jnp.cos(ang)
        sn = jnp.sin(ang)
        m_rot = tab_ref[1:2, :]
        c_ref[...] = cs * m_rot + (1.0 - m_rot)
        s1_ref[...] = -sn * tab_ref[2:3, :]
        s2_ref[...] = sn * tab_ref[3:4, :]

    out = _sds((T, 128), F32)
    return pl.pallas_call(
        body, name="rope_tables", grid=(T // tm,),
        in_specs=[_rows(tm, 128), _full((8, 128))],
        out_specs=[_rows(tm, 128)] * 3, out_shape=[out] * 3,
        compiler_params=_cp("parallel"),
    )(pos_b, lane_tab)


def _rot_fwd(t, c, s1, s2):
    w = t.shape[-1]
    return t * c + pltpu.roll(t, w - 8, 1) * s1 + pltpu.roll(t, 8, 1) * s2


def _rot_bwd(d, c, s1, s2):
    w = d.shape[-1]
    return d * c + pltpu.roll(d * s1, 8, 1) + pltpu.roll(d * s2, w - 8, 1)


def _store_dup(ref, t):
    low = lax.broadcasted_iota(jnp.int32, t.shape, 1) < HEAD
    sw = pltpu.roll(t, HEAD, 1)
    ref[:, 0:128] = jnp.where(low, t, sw).astype(BF16)
    ref[:, 128:256] = jnp.where(low, sw, t).astype(BF16)


def _fold_dup(d):
    low = lax.broadcasted_iota(jnp.int32, (d.shape[0], 128), 1) < HEAD
    d0 = d[:, 0:128]
    d1 = d[:, 128:256]
    return jnp.where(low, d0 + pltpu.roll(d0, HEAD, 1), d1 + pltpu.roll(d1, HEAD, 1))


def _fwd_in(x, mod8, g8, w_in, b_in, rc, rs1, rs2):
    T = x.shape[0]
    tm = min(T, 512)

    def body(x_ref, mod_ref, g_ref, w_ref, b_ref, c_ref, s1_ref, s2_ref,
             h_ref, q_ref, k_ref, v_ref, u_ref):
        xf = x_ref[...]
        h = (xf * _rstd(xf) * g_ref[0:1, :]) * (1.0 + mod_ref[1:2, :]) + mod_ref[0:1, :]
        hb = h.astype(BF16)
        h_ref[...] = hb
        c = c_ref[...]
        s1 = s1_ref[...]
        s2 = s2_ref[...]
        q = _nt(hb, w_ref[0:ATTN_W, :]) + b_ref[:, 0:ATTN_W]
        q = _rot_fwd(q, jnp.tile(c, (1, 4)), jnp.tile(s1, (1, 4)), jnp.tile(s2, (1, 4)))
        q_ref[...] = (q * (HEAD ** -0.5)).astype(BF16)
        k = _nt(hb, w_ref[ATTN_W:ATTN_W + KV_W, :]) + b_ref[:, ATTN_W:ATTN_W + KV_W]
        _store_dup(k_ref, _rot_fwd(k, c, s1, s2))
        v = _nt(hb, w_ref[ATTN_W + KV_W:ATTN_W + 2 * KV_W, :]) + b_ref[:, ATTN_W + KV_W:ATTN_W + 2 * KV_W]
        _store_dup(v_ref, v)
        u_ref[...] = _nt(hb, w_ref[ATTN_W + 2 * KV_W:IN_W, :]) + b_ref[:, ATTN_W + 2 * KV_W:IN_W]

    return pl.pallas_call(
        body, name="fwd_in", grid=(T // tm,),
        in_specs=[_rows(tm, D_MODEL), _full((8, D_MODEL)), _full((8, D_MODEL)),
                  _resident((IN_W, D_MODEL)), _full((1, IN_W)),
                  _rows(tm, 128), _rows(tm, 128), _rows(tm, 128)],
        out_specs=[_rows(tm, D_MODEL), _rows(tm, ATTN_W), _rows(tm, KVD_W), _rows(tm, KVD_W), _rows(tm, POOL_W)],
        out_shape=[_sds((T, D_MODEL), BF16), _sds((T, ATTN_W), BF16), _sds((T, KVD_W), BF16),
                   _sds((T, KVD_W), BF16), _sds((T, POOL_W), F32)],
        compiler_params=_cp("parallel"),
    )(x, mod8, g8, w_in, b_in, rc, rs1, rs2)


def _band_mask(n):
    kk = lax.broadcasted_iota(jnp.int32, (2 * BLK, BLK), 0)
    qi = lax.broadcasted_iota(jnp.int32, (2 * BLK, BLK), 1)
    first = jnp.where(n > 0, 0, 2 * BLK)
    in_prev = jnp.logical_and(kk < BLK, kk > qi + first)
    in_cur = jnp.logical_and(kk >= BLK, (kk - BLK) <= qi)
    one = jnp.logical_or(in_prev, in_cur)
    return jnp.concatenate([one] * GROUP, axis=1)


def _head_row(ref, j):
    return jnp.concatenate([ref[GROUP * j + r:GROUP * j + r + 1, :] for r in range(GROUP)], axis=1)


def _stack_heads(x_ref, j):
    low = lax.broadcasted_iota(jnp.int32, (BLK, 128), 1) < HEAD
    parts = []
    for gp in (2 * j, 2 * j + 1):
        x2 = x_ref[:, gp * 128:(gp + 1) * 128]
        parts.append(jnp.where(low, x2, jnp.zeros_like(x2)))
        parts.append(jnp.where(low, jnp.zeros_like(x2), x2))
    return jnp.concatenate(parts, axis=0)


def _unstack_heads(o):
    low = lax.broadcasted_iota(jnp.int32, (BLK, 128), 1) < HEAD
    return [jnp.where(low, o[0:BLK], o[BLK:2 * BLK]), jnp.where(low, o[2 * BLK:3 * BLK], o[3 * BLK:4 * BLK])]


def _attn_fwd(q, kd, vd, sink_b):
    T = q.shape[0]
    nb = T // BLK

    def body(q_ref, kp_ref, kc_ref, vp_ref, vc_ref, sk_ref, o_ref, lse_ref):
        valid = _band_mask(pl.program_id(0))
        for j in range(N_HEADS // GROUP):
            lanes = slice(j * 128, (j + 1) * 128)
            kcat = jnp.concatenate([kp_ref[:, lanes], kc_ref[:, lanes]], axis=0)
            vcat = jnp.concatenate([vp_ref[:, lanes], vc_ref[:, lanes]], axis=0)
            s = jnp.where(valid, _nt(kcat, _stack_heads(q_ref, j)), NEG_INF)
            sk = _head_row(sk_ref, j)
            m = jnp.maximum(jnp.max(s, axis=0, keepdims=True), sk)
            p = jnp.exp(s - m)
            den = jnp.sum(p, axis=0, keepdims=True) + jnp.exp(sk - m)
            p = p * (1.0 / den)
            o = _tn(p.astype(BF16), vcat)
            o_ref[:, 2 * j * 128:(2 * j + 2) * 128] = jnp.concatenate(_unstack_heads(o), axis=1).astype(BF16)
            lse = m + jnp.log(den)
            for r in range(GROUP):
                lse_ref[GROUP * j + r:GROUP * j + r + 1, :] = lse[:, r * 128:(r + 1) * 128]

    prev = lambda n: (jnp.maximum(n - 1, 0), 0)
    cur = lambda n: (n, 0)
    return pl.pallas_call(
        body, name="attn_fwd", grid=(nb,),
        in_specs=[pl.BlockSpec((BLK, ATTN_W), cur),
                  pl.BlockSpec((BLK, KVD_W), prev), pl.BlockSpec((BLK, KVD_W), cur),
                  pl.BlockSpec((BLK, KVD_W), prev), pl.BlockSpec((BLK, KVD_W), cur),
                  _full((8, 128))],
        out_specs=[pl.BlockSpec((BLK, ATTN_W), cur), pl.BlockSpec((N_HEADS, 128), cur)],
        out_shape=[_sds((T, ATTN_W), BF16), _sds((nb * N_HEADS, 128), F32)],
        compiler_params=_cp("parallel"),
    )(q, kd, kd, vd, vd, sink_b)


def _pool_fwd(u, pool_w, pool_scale):
    T = u.shape[0]
    tm = min(T, 512)

    def body(u_ref, w_ref, sc_ref, out_ref, pooled_ref, halo):
        i = pl.program_id(0)

        @pl.when(i == 0)
        def _():
            halo[...] = jnp.zeros_like(halo)

        ub = u_ref[...]
        ext = jnp.concatenate([halo[...], ub], axis=0)
        halo[...] = ub[tm - HALO:, :]
        tpos = (i * tm + lax.broadcasted_iota(jnp.int32, (tm, 1), 0)).astype(F32)
        for g, w in enumerate(POOL_WINDOWS):
            lanes = slice(g * 128, (g + 1) * 128)
            s = ext[:, lanes]
            sh = 1
            while sh < w:
                s = s + pltpu.roll(s, sh, 0)
                sh *= 2
            cnt = jnp.minimum(tpos + 1.0, float(w))
            pb = (s[HALO:, :] / cnt - ub[:, lanes]).astype(BF16)
            z = _mm(pb, w_ref[g].astype(BF16))
            out_ref[:, lanes] = (z * sc_ref[:, lanes]).astype(BF16)
            pooled_ref[:, lanes] = pb

    return pl.pallas_call(
        body, name="pool_fwd", grid=(T // tm,),
        in_specs=[_rows(tm, POOL_W), _full((4, 128, 128)), _full((1, POOL_W))],
        out_specs=[_rows(tm, POOL_W), _rows(tm, POOL_W)],
        out_shape=[_sds((T, POOL_W), BF16), _sds((T, POOL_W), BF16)],
        scratch_shapes=[pltpu.VMEM((HALO, POOL_W), F32)],
        compiler_params=_cp("arbitrary"),
    )(u, pool_w, pool_scale)


def _fwd_out(attn, pool, x, w_out, g8, mod8):
    T = x.shape[0]
    tm = min(T, 512)

    def body(a_ref, p_ref, x_ref, w_ref, g_ref, mod_ref, mix_ref, x1_ref):
        mix = _mm(a_ref[...], w_ref[0:ATTN_W, :]) + _mm(p_ref[...], w_ref[ATTN_W:, :])
        mix_ref[...] = mix
        x1_ref[...] = x_ref[...] + mod_ref[2:3, :] * (mix * _rstd(mix) * g_ref[1:2, :])

    return pl.pallas_call(
        body, name="fwd_out", grid=(T // tm,),
        in_specs=[_rows(tm, ATTN_W), _rows(tm, POOL_W), _rows(tm, D_MODEL),
                  _resident((D_MODEL, D_MODEL)), _full((8, D_MODEL)), _full((8, D_MODEL))],
        out_specs=[_rows(tm, D_MODEL), _rows(tm, D_MODEL)],
        out_shape=[_sds((T, D_MODEL), F32), _sds((T, D_MODEL), F32)],
        compiler_params=_cp("parallel"),
    )(attn, pool, x, w_out, g8, mod8)


FF_CHUNKS = ((0, 768), (768, 1536), (1536, 2304), (2304, D_FF))


def _ffn_fwd(x1, mod8, g8, wg, wu, wd, target=None):
    T = x1.shape[0]
    tm = min(T, 512)
    last = target is not None

    def body(*refs):
        x_ref, mod_ref, g_ref, wg_ref, wu_ref, wd_ref = refs[:6]
        t_ref = refs[6] if last else None
        h_ref, act_ref, ga_ref, gb_ref, f_ref, x2_ref = refs[6 + last:12 + last]
        xf = x_ref[...]
        h = (xf * _rstd(xf) * g_ref[2:3, :]) * (1.0 + mod_ref[4:5, :]) + mod_ref[3:4, :]
        hb = h.astype(BF16)
        h_ref[...] = hb
        f = jnp.zeros((tm, D_MODEL), F32)
        for lo, hi in FF_CHUNKS:
            a = _nt(hb, wg_ref[lo:hi, :])
            b = _nt(hb, wu_ref[lo:hi, :])
            sig = jax.nn.sigmoid(a)
            sl = a * sig
            act = (sl * b).astype(BF16)
            act_ref[:, lo:hi] = act
            ga_ref[:, lo:hi] = (b * (sig * (1.0 + a * (1.0 - sig)))).astype(BF16)
            gb_ref[:, lo:hi] = sl.astype(BF16)
            f = f + _mm(act, wd_ref[lo:hi, :])
        f_ref[...] = f
        x2 = xf + mod_ref[5:6, :] * (f * _rstd(f) * g_ref[3:4, :])
        if not last:
            x2_ref[...] = x2
        else:
            loss_ref = refs[13]

            @pl.when(pl.program_id(0) == 0)
            def _():
                loss_ref[...] = jnp.zeros_like(loss_ref)

            e = x2 - t_ref[...]
            x2_ref[...] = e * (1.0 / D_MODEL)
            loss_ref[...] += 0.5 * jnp.sum(jnp.mean(e * e, axis=-1, keepdims=True), axis=0, keepdims=True)

    act_shape = _sds((T, D_FF), BF16)
    weights = [_resident((D_FF, D_MODEL))] * 3
    return pl.pallas_call(
        body, name="ffn_fwd_loss" if last else "ffn_fwd", grid=(T // tm,),
        in_specs=[_rows(tm, D_MODEL), _full((8, D_MODEL)), _full((8, D_MODEL)), *weights]
        + ([_rows(tm, D_MODEL)] if last else []),
        out_specs=[_rows(tm, D_MODEL), _rows(tm, D_FF), _rows(tm, D_FF), _rows(tm, D_FF), _rows(tm, D_MODEL),
                   _rows(tm, D_MODEL)] + ([_full((8, 128))] if last else []),
        out_shape=[_sds((T, D_MODEL), BF16), act_shape, act_shape, act_shape, _sds((T, D_MODEL), F32),
                   _sds((T, D_MODEL), F32)] + ([_sds((8, 128), F32)] if last else []),
        compiler_params=_cp("arbitrary" if last else "parallel", vmem=FFN_VMEM_LIMIT),
    )(x1, mod8, g8, wg, wu, wd, *([target] if last else []))


def _ffn_bwd(dx2, f, ga, gb, x1, mod8, g8, wg, wu, wd):
    T = dx2.shape[0]
    tm = min(T, 256)

    def body(dx_ref, f_ref, ga_ref, gb_ref, x_ref, mod_ref, g_ref, wg_ref, wu_ref, wd_ref,
             dx1_ref, df_ref, da_ref, db_ref, red_ref):
        @pl.when(pl.program_id(0) == 0)
        def _():
            red_ref[...] = jnp.zeros_like(red_ref)

        dx = dx_ref[...]
        fv = f_ref[...]
        rstd = _rstd(fv)
        fhat = fv * rstd
        gpost = g_ref[3:4, :]
        gate = mod_ref[5:6, :]
        df, s_post = _norm_gain_bwd(dx, fhat, rstd, gate * gpost)
        red_ref[0:1, :] += gpost * s_post
        red_ref[1:2, :] += gate * s_post
        dfb = df.astype(BF16)
        df_ref[...] = dfb
        dh = jnp.zeros((tm, D_MODEL), F32)
        for lo, hi in FF_CHUNKS:
            dact = _nt(dfb, wd_ref[lo:hi, :])
            da = (dact * ga_ref[:, lo:hi].astype(F32)).astype(BF16)
            db = (dact * gb_ref[:, lo:hi].astype(F32)).astype(BF16)
            da_ref[:, lo:hi] = da
            db_ref[:, lo:hi] = db
            dh = dh + _mm(da, wg_ref[lo:hi, :]) + _mm(db, wu_ref[lo:hi, :])
        xf = x_ref[...]
        rstd1 = _rstd(xf)
        xhat = xf * rstd1
        gpre = g_ref[2:3, :]
        scale1 = 1.0 + mod_ref[4:5, :]
        dxn, s_pre = _norm_gain_bwd(dh, xhat, rstd1, scale1 * gpre)
        red_ref[2:3, :] += _colsum(dh)
        red_ref[3:4, :] += gpre * s_pre
        red_ref[4:5, :] += scale1 * s_pre
        dx1_ref[...] = dx + dxn

    act_shape = _sds((T, D_FF), BF16)
    return pl.pallas_call(
        body, name="ffn_bwd", grid=(T // tm,),
        in_specs=[_rows(tm, D_MODEL), _rows(tm, D_MODEL), _rows(tm, D_FF), _rows(tm, D_FF), _rows(tm, D_MODEL),
                  _full((8, D_MODEL)), _full((8, D_MODEL)),
                  _resident((D_FF, D_MODEL)), _resident((D_FF, D_MODEL)), _resident((D_FF, D_MODEL))],
        out_specs=[_rows(tm, D_MODEL), _rows(tm, D_MODEL), _rows(tm, D_FF), _rows(tm, D_FF), _full((8, D_MODEL))],
        out_shape=[_sds((T, D_MODEL), F32), _sds((T, D_MODEL), BF16), act_shape, act_shape, _sds((8, D_MODEL), F32)],
        compiler_params=_cp("arbitrary"),
    )(dx2, f, ga, gb, x1, mod8, g8, wg, wu, wd)


def _wgrad(a, b, name, after=None):
    T, K = a.shape
    N = b.shape[1]
    tt = min(T, WGRAD_TOKENS)
    tk = next(c for c in (1408, 640, 512, 256, 128) if K % c == 0)

    def body(a_ref, b_ref, *rest):
        o_ref = rest[-1]

        @pl.when(pl.program_id(1) == 0)
        def _():
            o_ref[...] = jnp.zeros_like(o_ref)

        o_ref[...] += _tn(a_ref[...], b_ref[...])

    extra = [] if after is None else [after]
    return pl.pallas_call(
        body, name=name, grid=(K // tk, T // tt),
        in_specs=[pl.BlockSpec((tt, tk), lambda i, t: (t, i)), pl.BlockSpec((tt, N), lambda i, t: (t, 0))]
        + [pl.BlockSpec(memory_space=pl.ANY)] * len(extra),
        out_specs=pl.BlockSpec((tk, N), lambda i, t: (i, 0)),
        out_shape=_sds((K, N), F32),
        compiler_params=_cp("parallel", "arbitrary"),
    )(a, b, *extra)


def _mix_bwd(dx1, mix, mod8, g8, w_out):
    T = dx1.shape[0]
    tm = min(T, 512)

    def body(dx_ref, mix_ref, mod_ref, g_ref, w_ref, dmix_ref, da_ref, dp_ref, red_ref):
        @pl.when(pl.program_id(0) == 0)
        def _():
            red_ref[...] = jnp.zeros_like(red_ref)

        dx = dx_ref[...]
        mv = mix_ref[...]
        rstd = _rstd(mv)
        mhat = mv * rstd
        gpost = g_ref[1:2, :]
        gate = mod_ref[2:3, :]
        dm, s_post = _norm_gain_bwd(dx, mhat, rstd, gate * gpost)
        red_ref[0:1, :] += gpost * s_post
        red_ref[1:2, :] += gate * s_post
        dmb = dm.astype(BF16)
        dmix_ref[...] = dmb
        da_ref[...] = _nt(dmb, w_ref[0:ATTN_W, :]).astype(BF16)
        dp_ref[...] = _nt(dmb, w_ref[ATTN_W:, :]).astype(BF16)

    return pl.pallas_call(
        body, name="mix_bwd", grid=(T // tm,),
        in_specs=[_rows(tm, D_MODEL), _rows(tm, D_MODEL), _full((8, D_MODEL)), _full((8, D_MODEL)),
                  _resident((D_MODEL, D_MODEL))],
        out_specs=[_rows(tm, D_MODEL), _rows(tm, ATTN_W), _rows(tm, POOL_W), _full((8, D_MODEL))],
        out_shape=[_sds((T, D_MODEL), BF16), _sds((T, ATTN_W), BF16), _sds((T, POOL_W), BF16),
                   _sds((8, D_MODEL), F32)],
        compiler_params=_cp("arbitrary"),
    )(dx1, mix, mod8, g8, w_out)


def _attn_bwd(q, kd, vd, lse, dattn, sink_b):
    T = q.shape[0]
    nb = T // BLK

    def body(q_ref, do_ref, lse_ref, kp_ref, kc_ref, vp_ref, vc_ref, sk_ref,
             dq_ref, dk_ref, dv_ref, dsk_ref, carry_k, carry_v):
        n = pl.program_id(0)

        @pl.when(n == 0)
        def _():
            carry_k[...] = jnp.zeros_like(carry_k)
            carry_v[...] = jnp.zeros_like(carry_v)
            dsk_ref[...] = jnp.zeros_like(dsk_ref)

        @pl.when(n < nb)
        def _():
            valid = _band_mask(n)
            for j in range(N_HEADS // GROUP):
                lanes = slice(j * 128, (j + 1) * 128)
                kcat = jnp.concatenate([kp_ref[:, lanes], kc_ref[:, lanes]], axis=0)
                vcat = jnp.concatenate([vp_ref[:, lanes], vc_ref[:, lanes]], axis=0)
                qs = _stack_heads(q_ref, j)
                dos = _stack_heads(do_ref, j)
                lse = _head_row(lse_ref, j)
                p = jnp.exp(jnp.where(valid, _nt(kcat, qs), NEG_INF) - lse)
                dp = _nt(vcat, dos)
                delta = jnp.sum(p * dp, axis=0, keepdims=True)
                ds = (p * (dp - delta)).astype(BF16)
                sink_term = jnp.exp(_head_row(sk_ref, j) - lse) * delta
                for r in range(GROUP):
                    h = GROUP * j + r
                    dsk_ref[h:h + 1, :] += -jnp.sum(sink_term[:, r * 128:(r + 1) * 128], axis=1, keepdims=True)
                dq_ref[:, 2 * j * 128:(2 * j + 2) * 128] = jnp.concatenate(_unstack_heads(_tn(ds, kcat)), axis=1)
                dk = _mm(ds, qs)
                dv = _mm(p.astype(BF16), dos)
                dk_ref[:, lanes] = carry_k[:, lanes] + dk[0:BLK]
                dv_ref[:, lanes] = carry_v[:, lanes] + dv[0:BLK]
                carry_k[:, lanes] = dk[BLK:]
                carry_v[:, lanes] = dv[BLK:]

        @pl.when(n == nb)
        def _():
            dk_ref[...] = carry_k[...]
            dv_ref[...] = carry_v[...]

    cur = lambda n: (jnp.minimum(n, nb - 1), 0)
    prev = lambda n: (jnp.maximum(n - 1, 0), 0)
    return pl.pallas_call(
        body, name="attn_bwd", grid=(nb + 1,),
        in_specs=[pl.BlockSpec((BLK, ATTN_W), cur), pl.BlockSpec((BLK, ATTN_W), cur), pl.BlockSpec((N_HEADS, 128), cur),
                  pl.BlockSpec((BLK, KVD_W), prev), pl.BlockSpec((BLK, KVD_W), cur),
                  pl.BlockSpec((BLK, KVD_W), prev), pl.BlockSpec((BLK, KVD_W), cur),
                  _full((8, 128))],
        out_specs=[pl.BlockSpec((BLK, ATTN_W), cur), pl.BlockSpec((BLK, KVD_W), prev),
                   pl.BlockSpec((BLK, KVD_W), prev), _full((8, 128))],
        out_shape=[_sds((T, ATTN_W), F32), _sds((T, KVD_W), F32), _sds((T, KVD_W), F32), _sds((8, 128), F32)],
        scratch_shapes=[pltpu.VMEM((BLK, KVD_W), F32), pltpu.VMEM((BLK, KVD_W), F32)],
        compiler_params=_cp("arbitrary"),
    )(q, dattn, lse, kd, kd, vd, vd, sink_b)


def _pool_bwd(dpool, pooled, pool_w, pool_scale):
    T = dpool.shape[0]
    tm = min(T, 512)
    nbk = T // tm
    ext_rows = tm + HALO

    def body(dp_ref, pl_ref, w_ref, sc_ref, du_ref, dw_ref, dsc_ref, halo):
        i = pl.program_id(0)

        @pl.when(i == 0)
        def _():
            halo[...] = jnp.zeros_like(halo)
            dw_ref[...] = jnp.zeros_like(dw_ref)
            dsc_ref[...] = jnp.zeros_like(dsc_ref)

        blk = nbk - 1 - i
        tpos = (blk * tm + lax.broadcasted_iota(jnp.int32, (tm, 1), 0)).astype(F32)
        for g, w in enumerate(POOL_WINDOWS):
            lanes = slice(g * 128, (g + 1) * 128)
            dp = dp_ref[:, lanes].astype(F32)
            pb = pl_ref[:, lanes]
            wg = w_ref[g].astype(BF16)
            z = _mm(pb, wg)
            dsc_ref[0:1, lanes] += _colsum(dp * z)
            dz = (dp * sc_ref[:, lanes]).astype(BF16)
            dw_ref[g] += _tn(pb, dz)
            dpl = _nt(dz, wg)
            e = dpl / jnp.minimum(tpos + 1.0, float(w))
            s = jnp.concatenate([e, halo[:, lanes]], axis=0)
            halo[:, lanes] = e[0:HALO, :]
            sh = 1
            while sh < w:
                s = s + pltpu.roll(s, ext_rows - sh, 0)
                sh *= 2
            du_ref[:, lanes] = s[0:tm, :] - dpl

    rev = lambda i: (nbk - 1 - i, 0)
    return pl.pallas_call(
        body, name="pool_bwd", grid=(nbk,),
        in_specs=[pl.BlockSpec((tm, POOL_W), rev), pl.BlockSpec((tm, POOL_W), rev),
                  _full((4, 128, 128)), _full((1, POOL_W))],
        out_specs=[pl.BlockSpec((tm, POOL_W), rev), _full((4, 128, 128)), _full((8, POOL_W))],
        out_shape=[_sds((T, POOL_W), F32), _sds((4, 128, 128), F32), _sds((8, POOL_W), F32)],
        scratch_shapes=[pltpu.VMEM((HALO, POOL_W), F32)],
        compiler_params=_cp("arbitrary"),
    )(dpool, pooled, pool_w, pool_scale)


def _in_bwd(dq, dk, dv, du, rc, rs1, rs2, x, dx1, mod8, g8, w_in):
    T = x.shape[0]
    tm = min(T, 512)

    def body(dq_ref, dk_ref, dv_ref, du_ref, c_ref, s1_ref, s2_ref, x_ref, dx1_ref, mod_ref, g_ref, w_ref,
             dx_ref, dproj_ref, red_ref, dbin_ref):
        @pl.when(pl.program_id(0) == 0)
        def _():
            red_ref[...] = jnp.zeros_like(red_ref)
            dbin_ref[...] = jnp.zeros_like(dbin_ref)

        c = c_ref[...]
        s1 = s1_ref[...]
        s2 = s2_ref[...]
        dqp = _rot_bwd(dq_ref[...] * (HEAD ** -0.5), jnp.tile(c, (1, 4)), jnp.tile(s1, (1, 4)), jnp.tile(s2, (1, 4)))
        dkp = _rot_bwd(_fold_dup(dk_ref[...]), c, s1, s2)
        pieces = ((0, ATTN_W, dqp), (ATTN_W, ATTN_W + KV_W, dkp),
                  (ATTN_W + KV_W, ATTN_W + 2 * KV_W, _fold_dup(dv_ref[...])), (ATTN_W + 2 * KV_W, IN_W, du_ref[...]))
        dh = jnp.zeros((tm, D_MODEL), F32)
        for lo, hi, val in pieces:
            dbin_ref[0:1, lo:hi] += _colsum(val)
            vb = val.astype(BF16)
            dproj_ref[:, lo:hi] = vb
            dh = dh + _mm(vb, w_ref[lo:hi, :])
        xf = x_ref[...]
        rstd = _rstd(xf)
        xhat = xf * rstd
        gpre = g_ref[0:1, :]
        scale1 = 1.0 + mod_ref[1:2, :]
        dxn, s_pre = _norm_gain_bwd(dh, xhat, rstd, scale1 * gpre)
        red_ref[0:1, :] += _colsum(dh)
        red_ref[1:2, :] += gpre * s_pre
        red_ref[2:3, :] += scale1 * s_pre
        dx_ref[...] = dx1_ref[...] + dxn

    return pl.pallas_call(
        body, name="in_bwd", grid=(T // tm,),
        in_specs=[_rows(tm, ATTN_W), _rows(tm, KVD_W), _rows(tm, KVD_W), _rows(tm, POOL_W),
                  _rows(tm, 128), _rows(tm, 128), _rows(tm, 128), _rows(tm, D_MODEL), _rows(tm, D_MODEL),
                  _full((8, D_MODEL)), _full((8, D_MODEL)), _resident((IN_W, D_MODEL))],
        out_specs=[_rows(tm, D_MODEL), _rows(tm, IN_W), _full((8, D_MODEL)), _full((8, IN_W))],
        out_shape=[_sds((T, D_MODEL), F32), _sds((T, IN_W), BF16), _sds((8, D_MODEL), F32), _sds((8, IN_W), F32)],
        compiler_params=_cp("arbitrary"),
    )(dq, dk, dv, du, rc, rs1, rs2, x, dx1, mod8, g8, w_in)


def _mod_fwd(c_all, ada_w, ada_b_sh):
    tn = 512

    def body(c_ref, w_ref, b_ref, o_ref):
        cv = c_ref[...]
        ca = (cv * jax.nn.sigmoid(cv)).astype(BF16)
        o_ref[...] = _mm(ca, w_ref[...].astype(BF16)) + b_ref[...]

    return pl.pallas_call(
        body, name="mod_fwd", grid=(2, ADA_SH // tn),
        in_specs=[_full((8, D_MODEL)), pl.BlockSpec((None, D_MODEL, tn), lambda l, j: (l, 0, j)),
                  pl.BlockSpec((None, 1, tn), lambda l, j: (l, 0, j))],
        out_specs=pl.BlockSpec((None, 8, tn), lambda l, j: (l, 0, j)),
        out_shape=_sds((2, 8, ADA_SH), F32),
        compiler_params=_cp("parallel", "parallel"),
    )(c_all, ada_w, ada_b_sh)


def _ada_wgrad(c_all_t, dmod_sh):
    tn = 512

    def body(c_ref, d_ref, o_ref):
        cv = c_ref[...]
        ca = cv * jax.nn.sigmoid(cv)
        o_ref[...] = jnp.dot(ca, d_ref[...], preferred_element_type=F32, precision=lax.Precision.HIGHEST)

    return pl.pallas_call(
        body, name="ada_wgrad", grid=(2, ADA_SH // tn),
        in_specs=[_full((D_MODEL, 8)), pl.BlockSpec((None, 8, tn), lambda l, j: (l, 0, j))],
        out_specs=pl.BlockSpec((None, D_MODEL, tn), lambda l, j: (l, 0, j)),
        out_shape=_sds((2, D_MODEL, ADA_SH), F32),
        compiler_params=_cp("parallel", "parallel"),
    )(c_all_t, dmod_sh)


def _sum_devices(g):
    R = g.shape[1]

    def body(g_ref, o_ref):
        acc = g_ref[0]
        for d in range(1, N_DEV):
            acc = acc + g_ref[d]
        o_ref[...] = acc

    return pl.pallas_call(
        body, name="sum_devices", grid=(1,),
        in_specs=[_full((N_DEV, R, 128))], out_specs=_full((R, 128)), out_shape=_sds((R, 128), F32),
        compiler_params=_cp("arbitrary"),
    )(g)


def _adamw(w, g, m, v, name):
    R, C = w.shape
    tr = R
    for cand in (256, 128, 64, 32, 16, 8):
        if R % cand == 0 and cand * C * 4 <= 2 * 1024 * 1024:
            tr = cand
            break

    def body(w_ref, g_ref, m_ref, v_ref, d_ref, nm_ref, nv_ref):
        gv = g_ref[...]
        mn = ADAM_B1 * m_ref[...] + (1.0 - ADAM_B1) * gv
        vn = ADAM_B2 * v_ref[...] + (1.0 - ADAM_B2) * (gv * gv)
        m_hat = mn / (1.0 - ADAM_B1 ** ADAM_STEP)
        v_hat = vn / (1.0 - ADAM_B2 ** ADAM_STEP)
        d_ref[...] = -ADAM_LR * (m_hat / (jnp.sqrt(v_hat) + ADAM_EPS) + ADAM_WD * w_ref[...])
        nm_ref[...] = mn
        nv_ref[...] = vn

    spec = pl.BlockSpec((tr, C), lambda i: (i, 0))
    out = _sds((R, C), F32)
    return pl.pallas_call(
        body, name=name, grid=(R // tr,),
        in_specs=[spec] * 4, out_specs=[spec] * 3, out_shape=[out] * 3,
        compiler_params=_cp("parallel"),
    )(w, g, m, v)


def _adamw_nd(w, g, m, v, name):
    shape = w.shape
    if w.ndim == 2 and shape[1] < 128:
        view = (1, shape[0] * shape[1])
    else:
        view = (-1, shape[-1])
    outs = _adamw(*[t.reshape(view) for t in (w, g, m, v)], name=name)
    return [o.reshape(shape) for o in outs]


def _coords():
    return lax.axis_index("x"), lax.axis_index("y"), lax.axis_index("c")


def _other_chips(x, y):
    return [(1 - x, y), (x, 1 - y), (1 - x, 1 - y)]


def _allgather8(blk, name):
    m_per, n = blk.shape

    def body(x_ref, out_ref, send_sems, recv_sems, local_sem):
        x, y, c = _coords()
        me, sibling = (x, y, c), (x, y, 1 - c)
        chips = _other_chips(x, y)

        def rows(px, py, pc):
            return out_ref.at[pl.ds((4 * px + 2 * py + pc) * m_per, m_per), :]

        def copy(k, block, to, src=None):
            return pltpu.make_async_remote_copy(
                src_ref=rows(*block) if src is None else src, dst_ref=rows(*block),
                send_sem=send_sems.at[k], recv_sem=recv_sems.at[k], device_id=to, device_id_type=MESH)

        mine = pltpu.make_async_copy(x_ref, rows(*me), local_sem)
        mine.start()
        first = [copy(0, me, sibling, src=x_ref)]
        first += [copy(1 + j, me, (*chip, c), src=x_ref) for j, chip in enumerate(chips)]
        for cp in first:
            cp.start()
        passed = [copy(4 + j, (*chip, c), sibling) for j, chip in enumerate(chips)]
        for j, chip in enumerate(chips):
            copy(1 + j, (*chip, c), me).wait_recv()
            passed[j].start()
        copy(0, sibling, me).wait_recv()
        for j, chip in enumerate(chips):
            copy(4 + j, (*chip, 1 - c), me).wait_recv()
        for cp in first + passed:
            cp.wait_send()
        mine.wait()

    return pl.pallas_call(
        body, name=name,
        out_shape=_sds((N_DEV * m_per, n), blk.dtype),
        in_specs=[pl.BlockSpec(memory_space=pltpu.VMEM)],
        out_specs=pl.BlockSpec(memory_space=pltpu.VMEM),
        scratch_shapes=[pltpu.SemaphoreType.DMA((7,)), pltpu.SemaphoreType.DMA((7,)), pltpu.SemaphoreType.DMA],
        compiler_params=pltpu.CompilerParams(vmem_limit_bytes=VMEM_LIMIT),
    )(blk)


def _row_tile(r, n):
    for cand in range(r, 15, -16):
        if r % cand == 0 and cand % 16 == 0 and cand * n * 4 <= 2 * 1024 * 1024:
            return cand
    return r


def _cast_slot(w, chip, name):
    r, n = w.shape
    tr = _row_tile(r, n)

    def body(chip_ref, w_ref, o_ref):
        o_ref[...] = w_ref[...].astype(BF16)

    grid_spec = pltpu.PrefetchScalarGridSpec(
        num_scalar_prefetch=1, grid=(r // tr,),
        in_specs=[pl.BlockSpec((tr, n), lambda i, ch: (i, 0))],
        out_specs=pl.BlockSpec((None, tr, n), lambda i, ch: (ch[0], i, 0)))
    return pl.pallas_call(
        body, name=name, grid_spec=grid_spec, out_shape=_sds((N_SHARD, r, n), BF16),
        compiler_params=_cp("arbitrary"),
    )(chip, w)


def _allgather_weights(bufs, name):
    nt = len(bufs)
    hom = [pl.BlockSpec(memory_space=pl.ANY)] * nt

    def body(*refs):
        outs = refs[nt:2 * nt]
        send_sems, recv_sems = refs[2 * nt:]
        x, y, c = _coords()
        sibling = (x, y, 1 - c)
        chips = _other_chips(x, y)

        def copy(t, k, block_chip, hc, to):
            r = outs[t].shape[1] // 2
            blk = outs[t].at[2 * block_chip[0] + block_chip[1], pl.ds(hc * r, r)]
            return pltpu.make_async_remote_copy(
                src_ref=blk, dst_ref=blk,
                send_sem=send_sems.at[t, k], recv_sem=recv_sems.at[t, k], device_id=to, device_id_type=MESH)

        started = []
        for t in range(nt):
            for j, chip in enumerate(chips):
                cp = copy(t, j, (x, y), c, (*chip, c))
                cp.start()
                started.append(cp)
        for t in range(nt):
            for j, chip in enumerate(chips):
                copy(t, j, chip, c, sibling).wait_recv()
                fw = copy(t, 3 + j, chip, c, sibling)
                fw.start()
                started.append(fw)
        for t in range(nt):
            for j, chip in enumerate(chips):
                copy(t, 3 + j, chip, 1 - c, sibling).wait_recv()
        for cp in started:
            cp.wait_send()

    return pl.pallas_call(
        body, name=name,
        out_shape=[_sds(b.shape, b.dtype) for b in bufs],
        in_specs=hom, out_specs=hom,
        input_output_aliases={t: t for t in range(nt)},
        scratch_shapes=[pltpu.SemaphoreType.DMA((nt, 6)), pltpu.SemaphoreType.DMA((nt, 6))],
    )(*bufs)


def _join_halves(tots, name):
    nt = len(tots)
    hom = [pl.BlockSpec(memory_space=pl.ANY)] * nt

    def body(*refs):
        outs = refs[nt:2 * nt]
        send_sems, recv_sems = refs[2 * nt:]
        x, y, c = _coords()
        sibling = (x, y, 1 - c)
        cps = []
        for t in range(nt):
            cp = pltpu.make_async_remote_copy(
                src_ref=outs[t].at[c], dst_ref=outs[t].at[c],
                send_sem=send_sems.at[t], recv_sem=recv_sems.at[t], device_id=sibling, device_id_type=MESH)
            cp.start()
            cps.append(cp)
        for t in range(nt):
            pltpu.make_async_remote_copy(
                src_ref=outs[t].at[c], dst_ref=outs[t].at[1 - c],
                send_sem=send_sems.at[t], recv_sem=recv_sems.at[t], device_id=sibling, device_id_type=MESH).wait_recv()
        for cp in cps:
            cp.wait_send()

    return pl.pallas_call(
        body, name=name,
        out_shape=[_sds(t.shape, t.dtype) for t in tots],
        in_specs=hom, out_specs=hom,
        input_output_aliases={t: t for t in range(nt)},
        scratch_shapes=[pltpu.SemaphoreType.DMA((nt,)), pltpu.SemaphoreType.DMA((nt,))],
    )(*tots)


def _pair_sum(g, recv, core, chip, name):
    _, _, r, n = g.shape
    tr = _row_tile(r, n)

    def body(core_ref, chip_ref, g_ref, r_ref, sb_ref, own_ref):
        tot = g_ref[...] + r_ref[...]
        sb_ref[...] = tot.astype(BF16)

        @pl.when(pl.program_id(1) == chip_ref[0])
        def _():
            own_ref[...] = tot

    grid_spec = pltpu.PrefetchScalarGridSpec(
        num_scalar_prefetch=2, grid=(r // tr, N_SHARD),
        in_specs=[pl.BlockSpec((None, None, tr, n), lambda i, s, co, ch: (s, co[0], i, 0)),
                  pl.BlockSpec((None, tr, n), lambda i, s, co, ch: (s, i, 0))],
        out_specs=[pl.BlockSpec((None, tr, n), lambda i, s, co, ch: (s, i, 0)),
                   pl.BlockSpec((tr, n), lambda i, s, co, ch: (i, 0))])
    return pl.pallas_call(
        body, name=name, grid_spec=grid_spec,
        out_shape=[_sds((N_SHARD, r, n), BF16), _sds((r, n), F32)],
        compiler_params=_cp("arbitrary", "arbitrary"),
    )(core, chip, g, recv)


def _chip_sum(own, recv, core, name):
    r, n = own.shape
    tr = _row_tile(r, n)

    def body(core_ref, o_ref, r_ref, t_ref):
        acc = o_ref[...]
        for j in range(3):
            acc = acc + r_ref[j].astype(F32)
        t_ref[...] = acc

    grid_spec = pltpu.PrefetchScalarGridSpec(
        num_scalar_prefetch=1, grid=(r // tr,),
        in_specs=[pl.BlockSpec((tr, n), lambda i, co: (i, 0)), pl.BlockSpec((3, tr, n), lambda i, co: (0, i, 0))],
        out_specs=pl.BlockSpec((None, tr, n), lambda i, co: (co[0], i, 0)))
    return pl.pallas_call(
        body, name=name, grid_spec=grid_spec, out_shape=_sds((2, r, n), F32),
        compiler_params=_cp("arbitrary"),
    )(core, own, recv)


_HBM = pl.BlockSpec(memory_space=pltpu.HBM)
_SEM = pl.BlockSpec(memory_space=pltpu.SEMAPHORE)
_EFFECT = pltpu.SideEffectType.DATAFLOW_SIDE_EFFECTING


def _ici_copies(srcs, dsts, send_sems, recv_sems, send_view, recv_view):
    x, y, c = _coords()
    out = []
    if send_view is None:
        for t in range(len(srcs)):
            r = srcs[t].shape[1] // 2
            out.append(pltpu.make_async_remote_copy(
                src_ref=srcs[t].at[:, pl.ds((1 - c) * r, r)], dst_ref=dsts[t],
                send_sem=send_sems.at[3 * t], recv_sem=recv_sems.at[3 * t],
                device_id=(x, y, 1 - c), device_id_type=MESH))
        return out
    for t in range(len(srcs)):
        for j, chip in enumerate(_other_chips(x, y)):
            out.append(pltpu.make_async_remote_copy(
                src_ref=send_view(srcs[t], chip, j, (x, y), c), dst_ref=recv_view(dsts[t], chip, j, (x, y), c),
                send_sem=send_sems.at[3 * t + j], recv_sem=recv_sems.at[3 * t + j],
                device_id=(*chip, c), device_id_type=MESH))
    return out


def _ici_start(srcs, dsts, after, send_view, recv_view, name):
    nt = len(srcs)
    inplace = dsts is None
    nbuf = nt if inplace else 2 * nt

    def body(*refs):
        send_sems, recv_sems = refs[nbuf + 1], refs[nbuf + 2]
        s_out = refs[nbuf + 3:nbuf + 3 + nt]
        d_out = s_out if inplace else refs[nbuf + 3 + nt:nbuf + 3 + 2 * nt]
        token = refs[-1]
        for cp in _ici_copies(s_out, d_out, send_sems, recv_sems, send_view, recv_view):
            cp.start()
        token[...] = jnp.zeros_like(token)

    bufs = list(srcs) + ([] if inplace else list(dsts))
    res = pl.pallas_call(
        body, name=name,
        out_shape=(pltpu.SemaphoreType.DMA((3 * nt,)), pltpu.SemaphoreType.DMA((3 * nt,)),
                   *[pltpu.HBM(b.shape, b.dtype) for b in bufs], _sds((8, 128), F32)),
        in_specs=[_HBM] * nbuf + [pl.BlockSpec(memory_space=pl.ANY)],
        out_specs=(_SEM, _SEM, *[_HBM] * nbuf, pl.BlockSpec(memory_space=pltpu.VMEM)),
        input_output_aliases={i: 2 + i for i in range(nbuf)},
        compiler_params=pltpu.CompilerParams(has_side_effects=_EFFECT),
    )(*[pltpu.with_memory_space_constraint(b, pltpu.HBM) for b in bufs], after)
    send_sems, recv_sems = res[0], res[1]
    s_thru = list(res[2:2 + nt])
    d_thru = s_thru if inplace else list(res[2 + nt:2 + 2 * nt])
    return send_sems, recv_sems, s_thru, d_thru, res[-1]


def _ici_wait(send_sems, recv_sems, srcs, dsts, after, send_view, recv_view, name):
    nt = len(srcs)
    inplace = dsts is None
    nbuf = nt if inplace else 2 * nt

    def body(*refs):
        send_ref, recv_ref = refs[nbuf], refs[nbuf + 1]
        s_out = refs[nbuf + 3:nbuf + 3 + nt]
        d_out = s_out if inplace else refs[nbuf + 3 + nt:nbuf + 3 + 2 * nt]
        for cp in _ici_copies(s_out, d_out, send_ref, recv_ref, send_view, recv_view):
            cp.wait_send()
            cp.wait_recv()

    bufs = list(srcs) + ([] if inplace else list(dsts))
    res = pl.pallas_call(
        body, name=name,
        out_shape=tuple(pltpu.HBM(b.shape, b.dtype) for b in bufs),
        in_specs=[_HBM] * nbuf + [_SEM, _SEM, pl.BlockSpec(memory_space=pl.ANY)],
        out_specs=tuple([_HBM] * nbuf),
        input_output_aliases={i: i for i in range(nbuf)},
        compiler_params=pltpu.CompilerParams(has_side_effects=_EFFECT),
    )(*bufs, send_sems, recv_sems, after)
    return list(res[:nt]) if inplace else (list(res[:nt]), list(res[nt:]))


def _w_half(buf, chip, c):
    r = buf.shape[1] // 2
    return buf.at[2 * chip[0] + chip[1], pl.ds(c * r, r)]


def _ag_send_view(buf, chip, j, me, c):
    return _w_half(buf, me, c)


def _ag_recv_view(buf, chip, j, me, c):
    return _w_half(buf, me, c)


def _rs_send_view(buf, chip, j, me, c):
    return buf.at[2 * chip[0] + chip[1]]


def _rs_recv_view(buf, chip, j, me, c):
    return buf.at[j]


def _ag_forward(bufs, name):
    nt = len(bufs)
    hom = [pl.BlockSpec(memory_space=pl.ANY)] * nt

    def body(*refs):
        outs = refs[nt:2 * nt]
        send_sems, recv_sems = refs[2 * nt:]
        x, y, c = _coords()
        sibling = (x, y, 1 - c)
        chips = _other_chips(x, y)

        def copy(t, j, hc):
            blk = _w_half(outs[t], chips[j], hc)
            return pltpu.make_async_remote_copy(
                src_ref=blk, dst_ref=blk, send_sem=send_sems.at[t, j], recv_sem=recv_sems.at[t, j],
                device_id=sibling, device_id_type=MESH)

        started = [copy(t, j, c) for t in range(nt) for j in range(3)]
        for cp in started:
            cp.start()
        for t in range(nt):
            for j in range(3):
                copy(t, j, 1 - c).wait_recv()
        for cp in started:
            cp.wait_send()

    return pl.pallas_call(
        body, name=name,
        out_shape=[_sds(b.shape, b.dtype) for b in bufs],
        in_specs=hom, out_specs=hom,
        input_output_aliases={t: t for t in range(nt)},
        scratch_shapes=[pltpu.SemaphoreType.DMA((nt, 3)), pltpu.SemaphoreType.DMA((nt, 3))],
    )(*bufs)


def _rs_swap_begin(grads, after, tag):
    land = [lax.empty((N_SHARD, g.shape[1] // 2, g.shape[2]), g.dtype) for g in grads]
    send_sems, recv_sems, s_thru, d_thru, token = _ici_start(grads, land, after, None, None, name="rs_swapgo_" + tag)
    return dict(sems=(send_sems, recv_sems), grads=s_thru, land=d_thru, tag=tag), token


def _rs_scatter_begin(swap, after):
    tag = swap["tag"]
    x, y, c = _coords()
    core = jnp.reshape(c, (1,)).astype(jnp.int32)
    chip = jnp.reshape(2 * x + y, (1,)).astype(jnp.int32)
    grads, recv = _ici_wait(*swap["sems"], swap["grads"], swap["land"], after, None, None, name="rs_swapend_" + tag)
    sums, owns = [], []
    for t, (g, rv) in enumerate(zip(grads, recv)):
        r = g.shape[1] // 2
        sb, own = _pair_sum(g.reshape(N_SHARD, 2, r, g.shape[2]), rv, core, chip, name=f"rs_pair_{tag}_{t}")
        sums.append(sb)
        owns.append(own)
    land = [lax.empty((3,) + s.shape[1:], s.dtype) for s in sums]
    send_sems, recv_sems, s_thru, d_thru, token = _ici_start(
        sums, land, after, _rs_send_view, _rs_recv_view, name="rs_start_" + tag)
    return dict(sems=(send_sems, recv_sems), sums=s_thru, land=d_thru, owns=owns, core=core, tag=tag), token


def _rs_end(state, after):
    tag = state["tag"]
    _, got = _ici_wait(*state["sems"], state["sums"], state["land"], after, _rs_send_view, _rs_recv_view,
                       name="rs_wait_" + tag)
    tots = [_chip_sum(o, gt, state["core"], name=f"rs_chip_{tag}_{t}")
            for t, (o, gt) in enumerate(zip(state["owns"], got))]
    full = _join_halves(tots, name="rs_join_" + tag)
    return [f.reshape(2 * f.shape[1], f.shape[2]) for f in full]


def _rope_lane_table():
    d = jnp.arange(128) % HEAD
    inv_freq = ROPE_THETA ** (-jnp.arange(0, ROT, 2, dtype=F32) / ROT)
    rot = d < ROT
    rows = [jnp.where(rot, inv_freq[d % (ROT // 2)], 0.0), rot.astype(F32),
            (d < ROT // 2).astype(F32), jnp.logical_and(d >= ROT // 2, rot).astype(F32)]
    return jnp.concatenate([jnp.stack(rows), jnp.zeros((4, 128), F32)], axis=0)


def _pad8(rows):
    return jnp.concatenate([rows, jnp.zeros((8 - rows.shape[0], rows.shape[1]), F32)], axis=0)


def kernel(x, c, positions, ada_w, ada_b, w_in, b_in, sinks, pool_w, pool_scale, w_out, w_gate, w_up, w_down, g_pre_mix, g_post_mix, g_pre_ffn, g_post_ffn, loss_target, m_ada_w, m_ada_b, m_w_in, m_b_in, m_sinks, m_pool_w, m_pool_scale, m_w_out, m_w_gate, m_w_up, m_w_down, m_g_pre_mix, m_g_post_mix, m_g_pre_ffn, m_g_post_ffn, v_ada_w, v_ada_b, v_w_in, v_b_in, v_sinks, v_pool_w, v_pool_scale, v_w_out, v_w_gate, v_w_up, v_w_down, v_g_pre_mix, v_g_post_mix, v_g_pre_ffn, v_g_post_ffn):
    T = x.shape[1]
    n_layers = ada_w.shape[0]
    ax, ay, ac = _coords()
    my_dev = 4 * ax + 2 * ay + ac
    my_chip = 2 * ax + ay
    x0 = x.reshape(T, D_MODEL)
    target = loss_target.reshape(T, D_MODEL)

    c_all = _allgather8(c.reshape(8, 128), name="ag_c").reshape(N_DEV, D_MODEL)
    ada_b_sh = lax.dynamic_slice_in_dim(ada_b, my_chip * ADA_SH, ADA_SH, axis=1).reshape(n_layers, 1, ADA_SH)
    mod_part = _mod_fwd(c_all, ada_w, ada_b_sh)
    mod_all = _allgather8(mod_part.reshape(n_layers * 8, ADA_SH), name="ag_mod")
    mod_all = mod_all.reshape(N_DEV, n_layers, 8, ADA_SH)[0::2]
    mod_mine = lax.dynamic_index_in_dim(mod_all, my_dev, axis=2, keepdims=False)
    mod = jnp.transpose(mod_mine, (1, 0, 2)).reshape(n_layers, 6, D_MODEL)

    pos_b = jnp.broadcast_to(positions.reshape(T, 1), (T, 128))
    rc, rs1, rs2 = _rope_tables(pos_b, _rope_lane_table())

    chip1 = jnp.reshape(my_chip, (1,)).astype(jnp.int32)

    def tr(t):
        return jnp.transpose(t, (0, 2, 1))

    w_in_t, w_gate_t, w_up_t = tr(w_in), tr(w_gate), tr(w_up)

    def cast_layer(l):
        return [_cast_slot(w[l], chip1, name=f"cast_{nm}{l}")
                for nm, w in (("w_in", w_in_t), ("w_out", w_out), ("w_gate", w_gate_t), ("w_up", w_up_t),
                              ("w_down", w_down))]

    def as_operands(bufs):
        gin, gout, gg, gu, gd = bufs
        return (gin.reshape(IN_W, D_MODEL), gout.reshape(D_MODEL, D_MODEL), gg.reshape(D_FF, D_MODEL),
                gu.reshape(D_FF, D_MODEL), gd.reshape(D_FF, D_MODEL))

    bufs0 = cast_layer(0)
    win0 = _allgather_weights(bufs0[:1], name="ag_w0_in")
    rest_send, rest_recv, rest_bufs, _, ag_token = _ici_start(
        bufs0[1:], None, win0[0], _ag_send_view, _ag_recv_view, name="ag_start_0")
    weights = [None] * n_layers

    saved = []
    xl = x0
    for l in range(n_layers):
        mod8 = _pad8(mod[l])
        if l + 1 < n_layers:
            ag_send, ag_recv, ag_bufs, _, ag_token = _ici_start(
                cast_layer(l + 1), None, ag_token, _ag_send_view, _ag_recv_view, name=f"ag_start_{l + 1}")
        if l == 0 or l + 1 < n_layers:
            mod8 = mod8 + ag_token[0, 0]
        g8 = _pad8(jnp.stack([g_pre_mix[l], g_post_mix[l], g_pre_ffn[l], g_post_ffn[l]]))
        sink_b = jnp.broadcast_to(sinks[l][:, None], (N_HEADS, 128))
        psc = pool_scale[l].reshape(1, POOL_W)
        win = win0[0].reshape(IN_W, D_MODEL) if l == 0 else weights[l][0]
        h, q, k, v, u = _fwd_in(xl, mod8, g8, win, b_in[l].reshape(1, IN_W), rc, rs1, rs2)
        attn, lse = _attn_fwd(q, k, v, sink_b)
        pool, pooled = _pool_fwd(u, pool_w[l], psc)
        if l == 0:
            arrived = _ici_wait(rest_send, rest_recv, rest_bufs, None, pool, _ag_send_view, _ag_recv_view,
                                name="ag_wait_0")
            weights[0] = as_operands(win0 + _ag_forward(arrived, name="ag_fwd_0"))
        win, wout, wg, wu, wd = weights[l]
        mix, x1 = _fwd_out(attn, pool, xl, wout, g8, mod8)
        if l + 1 < n_layers:
            h2, act, ga, gb, f, x2 = _ffn_fwd(x1, mod8, g8, wg, wu, wd)
        else:
            h2, act, ga, gb, f, x2, loss_tile = _ffn_fwd(x1, mod8, g8, wg, wu, wd, target=target)
        saved.append(dict(x=xl, h=h, q=q, k=k, v=v, lse=lse, attn=attn, pool=pool, pooled=pooled, mix=mix,
                          x1=x1, h2=h2, act=act, ga=ga, gb=gb, f=f, mod8=mod8, g8=g8, sink_b=sink_b, psc=psc))
        xl = x2
        if l + 1 < n_layers:
            arrived = _ici_wait(ag_send, ag_recv, ag_bufs, None, x2, _ag_send_view, _ag_recv_view,
                                name=f"ag_wait_{l + 1}")
            weights[l + 1] = as_operands(_ag_forward(arrived, name=f"ag_fwd_{l + 1}"))

    dy = xl
    loss = lax.psum(loss_tile[0, 0], ("x", "y", "c"))

    small = [None] * n_layers
    dmod_rows = [None] * n_layers
    reduced = [dict() for _ in range(n_layers)]
    att_swap = None
    dx = dy
    for l in reversed(range(n_layers)):
        s = saved[l]
        win, wout, wg, wu, wd = weights[l]
        if att_swap is not None:
            s = dict(s, mod8=s["mod8"] + att_swap[1][0, 0])
        dx1, df, da, db, red_f = _ffn_bwd(dx, s["f"], s["ga"], s["gb"], s["x1"], s["mod8"], s["g8"], wg, wu, wd)
        token = None
        if att_swap is not None:
            att_scatter = _rs_scatter_begin(att_swap[0], dx1)
            token = att_scatter[1]
        ffn_shards = (N_SHARD, FF_SH, D_MODEL)
        g_wd = _wgrad(s["act"], df, name="wgrad_down", after=token).reshape(ffn_shards)
        g_wg = _wgrad(da, s["h2"], name="wgrad_gate").reshape(ffn_shards)
        g_wu = _wgrad(db, s["h2"], name="wgrad_up").reshape(ffn_shards)
        ffn_swap = _rs_swap_begin([g_wg, g_wu, g_wd], dx1, tag=f"{l}f")
        if att_swap is not None:
            got = _rs_end(att_scatter[0], ffn_swap[1])
            reduced[l + 1].update(w_in=got[0], w_out=got[1])
        s = dict(s, mod8=s["mod8"] + ffn_swap[1][0, 0])
        dmix, dattn, dpool, red_c = _mix_bwd(dx1, s["mix"], s["mod8"], s["g8"], wout)
        g_wout = jnp.concatenate([_wgrad(s["attn"], dmix, name="wgrad_out_a"),
                                  _wgrad(s["pool"], dmix, name="wgrad_out_p")], axis=0)
        ffn_scatter = _rs_scatter_begin(ffn_swap[0], dattn)
        dq, dk, dv, dsink = _attn_bwd(s["q"], s["k"], s["v"], s["lse"], dattn, s["sink_b"] + ffn_scatter[1][0:1, :])
        du, g_poolw, dpsc = _pool_bwd(dpool, s["pooled"], pool_w[l], s["psc"])
        dx, dproj, red_d, dbin = _in_bwd(dq, dk, dv, du, rc, rs1, rs2, s["x"], dx1, s["mod8"], s["g8"], win)
        g_win = _wgrad(dproj, s["h"], name="wgrad_in")
        g_win_sh = g_win.reshape(N_SHARD, IN_SH, D_MODEL)
        got = _rs_end(ffn_scatter[0], dproj)
        reduced[l].update(w_gate=got[0], w_up=got[1], w_down=got[2])
        att_swap = _rs_swap_begin([g_win_sh, g_wout.reshape(N_SHARD, OUT_SH, D_MODEL)], dx, tag=f"{l}a")
        dmod_rows[l] = jnp.concatenate([red_d[0], red_d[1], red_c[0], red_f[2], red_f[3], red_f[0]])
        small[l] = jnp.concatenate([red_d[2], red_c[1], red_f[4], red_f[1], dbin[0], dpsc[0], dsink[:, 0],
                                    jnp.zeros((120,), F32), g_poolw.reshape(-1)])
    grad_x = dx.reshape(1, T, D_MODEL)

    per_layer = small[0].shape[0]
    rows_small = n_layers * per_layer // 128
    rows_mod = n_layers * 6 * D_MODEL // 128
    rows_pad = -(rows_small + rows_mod) % 8
    pack = jnp.concatenate(small + dmod_rows + [jnp.zeros((rows_pad * 128,), F32)]).reshape(-1, 128)
    pack = pack + att_swap[1][0, 0]
    gathered = _allgather8(pack, name="ag_small").reshape(N_DEV, pack.shape[0], 128)
    summed = _sum_devices(gathered)
    att_scatter = _rs_scatter_begin(att_swap[0], summed)
    small_sum = summed[:rows_small].reshape(n_layers, per_layer)
    o = 0
    small_g = {}
    for nm, width in (("g_pre_mix", D_MODEL), ("g_post_mix", D_MODEL), ("g_pre_ffn", D_MODEL),
                      ("g_post_ffn", D_MODEL), ("b_in", IN_W), ("pool_scale", POOL_W), ("sinks", 128),
                      ("pool_w", 4 * 128 * 128)):
        small_g[nm] = small_sum[:, o:o + width]
        o += width
    small_g["sinks"] = small_g["sinks"][:, :N_HEADS]
    small_g["pool_w"] = small_g["pool_w"].reshape(n_layers, 4, 128, 128)
    small_g["ada_b"] = summed[rows_small:rows_small + rows_mod].reshape(n_layers, 6 * D_MODEL)
    dmod_all = gathered[:, rows_small:rows_small + rows_mod].reshape(N_DEV, n_layers, N_SHARD, ADA_SH)
    dmod_sh = lax.dynamic_index_in_dim(dmod_all, my_chip, axis=2, keepdims=False)
    g_ada_w = _ada_wgrad(jnp.transpose(c_all), jnp.transpose(dmod_sh, (1, 0, 2)))

    grads = dict(ada_w=g_ada_w, ada_b=small_g["ada_b"], b_in=small_g["b_in"], sinks=small_g["sinks"],
                 pool_w=small_g["pool_w"], pool_scale=small_g["pool_scale"], g_pre_mix=small_g["g_pre_mix"],
                 g_post_mix=small_g["g_post_mix"], g_pre_ffn=small_g["g_pre_ffn"], g_post_ffn=small_g["g_post_ffn"])
    params = dict(ada_w=(ada_w, m_ada_w, v_ada_w), ada_b=(ada_b, m_ada_b, v_ada_b), w_in=(w_in, m_w_in, v_w_in),
                  b_in=(b_in, m_b_in, v_b_in), sinks=(sinks, m_sinks, v_sinks), pool_w=(pool_w, m_pool_w, v_pool_w),
                  pool_scale=(pool_scale, m_pool_scale, v_pool_scale), w_out=(w_out, m_w_out, v_w_out),
                  w_gate=(w_gate, m_w_gate, v_w_gate), w_up=(w_up, m_w_up, v_w_up),
                  w_down=(w_down, m_w_down, v_w_down), g_pre_mix=(g_pre_mix, m_g_pre_mix, v_g_pre_mix),
                  g_post_mix=(g_post_mix, m_g_post_mix, v_g_post_mix), g_pre_ffn=(g_pre_ffn, m_g_pre_ffn, v_g_pre_ffn),
                  g_post_ffn=(g_post_ffn, m_g_post_ffn, v_g_post_ffn))
    names = list(params)
    updates = {nm: _adamw_nd(*params[nm][:1], grads[nm], *params[nm][1:], name="adamw_" + nm) for nm in grads}

    got = _rs_end(att_scatter[0], updates["ada_w"][0])
    reduced[0].update(w_in=got[0], w_out=got[1])
    for nm in ("w_in", "w_out", "w_gate", "w_up", "w_down"):
        g = jnp.stack([reduced[l][nm] for l in range(n_layers)])
        if nm in ("w_in", "w_gate", "w_up"):
            upd = _adamw_nd(tr(params[nm][0]), g, tr(params[nm][1]), tr(params[nm][2]), name="adamw_" + nm)
            grads[nm], updates[nm] = tr(g), [tr(u) for u in upd]
        else:
            grads[nm], updates[nm] = g, _adamw_nd(params[nm][0], g, *params[nm][1:], name="adamw_" + nm)
    return (loss, grad_x, *[grads[nm] for nm in names], *[updates[nm][0] for nm in names],
            *[updates[nm][1] for nm in names], *[updates[nm][2] for nm in names])
```

```python
import functools

import jax
import jax.numpy as jnp
from jax import lax
from jax.experimental import pallas as pl
from jax.experimental.pallas import tpu as pltpu

F32 = jnp.float32
BF16 = jnp.bfloat16
MESH = pl.DeviceIdType.MESH

D_MODEL = 1024
ATTN_W = 512
KV_W = 128
KVD_W = 256
POOL_W = 512
IN_W = 1280
D_FF = 2816
N_SHARD = 4
FF_SH = D_FF // N_SHARD
IN_SH = IN_W // N_SHARD
OUT_SH = D_MODEL // N_SHARD
ADA_SH = 6 * D_MODEL // N_SHARD
HEAD = 64
N_HEADS = 8
GROUP = 4
BLK = 128
POOL_WINDOWS = (2, 4, 8, 16)
HALO = 16
ROT = 16
ROPE_THETA = 500000.0
EPS = 1e-6
NEG_INF = -1e30
N_DEV = 8

ADAM_LR = 0.001
ADAM_B1 = 0.9
ADAM_B2 = 0.999
ADAM_EPS = 1e-08
ADAM_WD = 0.01
ADAM_STEP = 10

VMEM_LIMIT = 48 * 1024 * 1024
FFN_VMEM_LIMIT = 60 * 1024 * 1024
WGRAD_TOKENS = 2048


def _cp(*sem, vmem=VMEM_LIMIT):
    return pltpu.CompilerParams(dimension_semantics=sem, vmem_limit_bytes=vmem)


def _full(shape):
    nd = len(shape)
    return pl.BlockSpec(shape, lambda *_: (0,) * nd)


def _resident(shape):
    nd = len(shape)
    return pl.BlockSpec(shape, lambda *_: (0,) * nd, pipeline_mode=pl.Buffered(1))


def _rows(tm, ncol):
    return pl.BlockSpec((tm, ncol), lambda i: (i, 0))


def _sds(shape, dtype):
    return jax.ShapeDtypeStruct(shape, dtype)


def _nt(a, b):
    return lax.dot_general(a, b, (((1,), (1,)), ((), ())), preferred_element_type=F32)


def _tn(a, b):
    return lax.dot_general(a, b, (((0,), (0,)), ((), ())), preferred_element_type=F32)


def _mm(a, b):
    return jnp.dot(a, b, preferred_element_type=F32)


def _rstd(x):
    return lax.rsqrt(jnp.mean(x * x, axis=-1, keepdims=True) + EPS)


def _colsum(x):
    return jnp.sum(x, axis=0, keepdims=True)


def _norm_gain_bwd(dy, xhat, rstd, gain):
    p = dy * xhat
    dx = rstd * (dy * gain - xhat * jnp.mean(p * gain, axis=-1, keepdims=True))
    return dx, _colsum(p)


def _rope_tables(pos_b, lane_tab):
    T = pos_b.shape[0]
    tm = min(T, 1024)

    def body(pos_ref, tab_ref, c_ref, s1_ref, s2_ref):
        ang = pos_ref[...].astype(F32) * tab_ref[0:1, :]
        cs = jnp.cos(ang)
        sn = jnp.sin(ang)
        m_rot = tab_ref[1:2, :]
        c_ref[...] = cs * m_rot + (1.0 - m_rot)
        s1_ref[...] = -sn * tab_ref[2:3, :]
        s2_ref[...] = sn * tab_ref[3:4, :]

    out = _sds((T, 128), F32)
    return pl.pallas_call(
        body, name="rope_tables", grid=(T // tm,),
        in_specs=[_rows(tm, 128), _full((8, 128))],
        out_specs=[_rows(tm, 128)] * 3, out_shape=[out] * 3,
        compiler_params=_cp("parallel"),
    )(pos_b, lane_tab)


def _rot_fwd(t, c, s1, s2):
    w = t.shape[-1]
    return t * c + pltpu.roll(t, w - 8, 1) * s1 + pltpu.roll(t, 8, 1) * s2


def _rot_bwd(d, c, s1, s2):
    w = d.shape[-1]
    return d * c + pltpu.roll(d * s1, 8, 1) + pltpu.roll(d * s2, w - 8, 1)


def _store_dup(ref, t):
    low = lax.broadcasted_iota(jnp.int32, t.shape, 1) < HEAD
    sw = pltpu.roll(t, HEAD, 1)
    ref[:, 0:128] = jnp.where(low, t, sw).astype(BF16)
    ref[:, 128:256] = jnp.where(low, sw, t).astype(BF16)


def _fold_dup(d):
    low = lax.broadcasted_iota(jnp.int32, (d.shape[0], 128), 1) < HEAD
    d0 = d[:, 0:128]
    d1 = d[:, 128:256]
    return jnp.where(low, d0 + pltpu.roll(d0, HEAD, 1), d1 + pltpu.roll(d1, HEAD, 1))


def _fwd_in(x, mod8, g8, w_in, b_in, rc, rs1, rs2):
    T = x.shape[0]
    tm = min(T, 512)

    def body(x_ref, mod_ref, g_ref, w_ref, b_ref, c_ref, s1_ref, s2_ref,
             h_ref, q_ref, k_ref, v_ref, u_ref):
        xf = x_ref[...]
        h = (xf * _rstd(xf) * g_ref[0:1, :]) * (1.0 + mod_ref[1:2, :]) + mod_ref[0:1, :]
        hb = h.astype(BF16)
        h_ref[...] = hb
        c = c_ref[...]
        s1 = s1_ref[...]
        s2 = s2_ref[...]
        q = _nt(hb, w_ref[0:ATTN_W, :]) + b_ref[:, 0:ATTN_W]
        q = _rot_fwd(q, jnp.tile(c, (1, 4)), jnp.tile(s1, (1, 4)), jnp.tile(s2, (1, 4)))
        q_ref[...] = (q * (HEAD ** -0.5)).astype(BF16)
        k = _nt(hb, w_ref[ATTN_W:ATTN_W + KV_W, :]) + b_ref[:, ATTN_W:ATTN_W + KV_W]
        _store_dup(k_ref, _rot_fwd(k, c, s1, s2))
        v = _nt(hb, w_ref[ATTN_W + KV_W:ATTN_W + 2 * KV_W, :]) + b_ref[:, ATTN_W + KV_W:ATTN_W + 2 * KV_W]
        _store_dup(v_ref, v)
        u_ref[...] = _nt(hb, w_ref[ATTN_W + 2 * KV_W:IN_W, :]) + b_ref[:, ATTN_W + 2 * KV_W:IN_W]

    return pl.pallas_call(
        body, name="fwd_in", grid=(T // tm,),
        in_specs=[_rows(tm, D_MODEL), _full((8, D_MODEL)), _full((8, D_MODEL)),
                  _resident((IN_W, D_MODEL)), _full((1, IN_W)),
                  _rows(tm, 128), _rows(tm, 128), _rows(tm, 128)],
        out_specs=[_rows(tm, D_MODEL), _rows(tm, ATTN_W), _rows(tm, KVD_W), _rows(tm, KVD_W), _rows(tm, POOL_W)],
        out_shape=[_sds((T, D_MODEL), BF16), _sds((T, ATTN_W), BF16), _sds((T, KVD_W), BF16),
                   _sds((T, KVD_W), BF16), _sds((T, POOL_W), F32)],
        compiler_params=_cp("parallel"),
    )(x, mod8, g8, w_in, b_in, rc, rs1, rs2)


def _band_mask(n):
    kk = lax.broadcasted_iota(jnp.int32, (2 * BLK, BLK), 0)
    qi = lax.broadcasted_iota(jnp.int32, (2 * BLK, BLK), 1)
    first = jnp.where(n > 0, 0, 2 * BLK)
    in_prev = jnp.logical_and(kk < BLK, kk > qi + first)
    in_cur = jnp.logical_and(kk >= BLK, (kk - BLK) <= qi)
    one = jnp.logical_or(in_prev, in_cur)
    return jnp.concatenate([one] * GROUP, axis=1)


def _head_row(ref, j):
    return jnp.concatenate([ref[GROUP * j + r:GROUP * j + r + 1, :] for r in range(GROUP)], axis=1)


def _stack_heads(x_ref, j):
    low = lax.broadcasted_iota(jnp.int32, (BLK, 128), 1) < HEAD
    parts = []
    for gp in (2 * j, 2 * j + 1):
        x2 = x_ref[:, gp * 128:(gp + 1) * 128]
        parts.append(jnp.where(low, x2, jnp.zeros_like(x2)))
        parts.append(jnp.where(low, jnp.zeros_like(x2), x2))
    return jnp.concatenate(parts, axis=0)


def _unstack_heads(o):
    low = lax.broadcasted_iota(jnp.int32, (BLK, 128), 1) < HEAD
    return [jnp.where(low, o[0:BLK], o[BLK:2 * BLK]), jnp.where(low, o[2 * BLK:3 * BLK], o[3 * BLK:4 * BLK])]


def _attn_fwd(q, kd, vd, sink_b):
    T = q.shape[0]
    nb = T // BLK

    def body(q_ref, kp_ref, kc_ref, vp_ref, vc_ref, sk_ref, o_ref, lse_ref):
        valid = _band_mask(pl.program_id(0))
        for j in range(N_HEADS // GROUP):
            lanes = slice(j * 128, (j + 1) * 128)
            kcat = jnp.concatenate([kp_ref[:, lanes], kc_ref[:, lanes]], axis=0)
            vcat = jnp.concatenate([vp_ref[:, lanes], vc_ref[:, lanes]], axis=0)
            s = jnp.where(valid, _nt(kcat, _stack_heads(q_ref, j)), NEG_INF)
            sk = _head_row(sk_ref, j)
            m = jnp.maximum(jnp.max(s, axis=0, keepdims=True), sk)
            p = jnp.exp(s - m)
            den = jnp.sum(p, axis=0, keepdims=True) + jnp.exp(sk - m)
            p = p * (1.0 / den)
            o = _tn(p.astype(BF16), vcat)
            o_ref[:, 2 * j * 128:(2 * j + 2) * 128] = jnp.concatenate(_unstack_heads(o), axis=1).astype(BF16)
            lse = m + jnp.log(den)
            for r in range(GROUP):
                lse_ref[GROUP * j + r:GROUP * j + r + 1, :] = lse[:, r * 128:(r + 1) * 128]

    prev = lambda n: (jnp.maximum(n - 1, 0), 0)
    cur = lambda n: (n, 0)
    return pl.pallas_call(
        body, name="attn_fwd", grid=(nb,),
        in_specs=[pl.BlockSpec((BLK, ATTN_W), cur),
                  pl.BlockSpec((BLK, KVD_W), prev), pl.BlockSpec((BLK, KVD_W), cur),
                  pl.BlockSpec((BLK, KVD_W), prev), pl.BlockSpec((BLK, KVD_W), cur),
                  _full((8, 128))],
        out_specs=[pl.BlockSpec((BLK, ATTN_W), cur), pl.BlockSpec((N_HEADS, 128), cur)],
        out_shape=[_sds((T, ATTN_W), BF16), _sds((nb * N_HEADS, 128), F32)],
        compiler_params=_cp("parallel"),
    )(q, kd, kd, vd, vd, sink_b)


def _pool_fwd(u, pool_w, pool_scale):
    T = u.shape[0]
    tm = min(T, 512)

    def body(u_ref, w_ref, sc_ref, out_ref, pooled_ref, halo):
        i = pl.program_id(0)

        @pl.when(i == 0)
        def _():
            halo[...] = jnp.zeros_like(halo)

        ub = u_ref[...]
        ext = jnp.concatenate([halo[...], ub], axis=0)
        halo[...] = ub[tm - HALO:, :]
        tpos = (i * tm + lax.broadcasted_iota(jnp.int32, (tm, 1), 0)).astype(F32)
        for g, w in enumerate(POOL_WINDOWS):
            lanes = slice(g * 128, (g + 1) * 128)
            s = ext[:, lanes]
            sh = 1
            while sh < w:
                s = s + pltpu.roll(s, sh, 0)
                sh *= 2
            cnt = jnp.minimum(tpos + 1.0, float(w))
            pb = (s[HALO:, :] / cnt - ub[:, lanes]).astype(BF16)
            z = _mm(pb, w_ref[g].astype(BF16))
            out_ref[:, lanes] = (z * sc_ref[:, lanes]).astype(BF16)
            pooled_ref[:, lanes] = pb

    return pl.pallas_call(
        body, name="pool_fwd", grid=(T // tm,),
        in_specs=[_rows(tm, POOL_W), _full((4, 128, 128)), _full((1, POOL_W))],
        out_specs=[_rows(tm, POOL_W), _rows(tm, POOL_W)],
        out_shape=[_sds((T, POOL_W), BF16), _sds((T, POOL_W), BF16)],
        scratch_shapes=[pltpu.VMEM((HALO, POOL_W), F32)],
        compiler_params=_cp("arbitrary"),
    )(u, pool_w, pool_scale)


def _fwd_out(attn, pool, x, w_out, g8, mod8):
    T = x.shape[0]
    tm = min(T, 512)

    def body(a_ref, p_ref, x_ref, w_ref, g_ref, mod_ref, mix_ref, x1_ref):
        mix = _mm(a_ref[...], w_ref[0:ATTN_W, :]) + _mm(p_ref[...], w_ref[ATTN_W:, :])
        mix_ref[...] = mix.astype(BF16)
        x1_ref[...] = x_ref[...] + mod_ref[2:3, :] * (mix * _rstd(mix) * g_ref[1:2, :])

    return pl.pallas_call(
        body, name="fwd_out", grid=(T // tm,),
        in_specs=[_rows(tm, ATTN_W), _rows(tm, POOL_W), _rows(tm, D_MODEL),
                  _resident((D_MODEL, D_MODEL)), _full((8, D_MODEL)), _full((8, D_MODEL))],
        out_specs=[_rows(tm, D_MODEL), _rows(tm, D_MODEL)],
        out_shape=[_sds((T, D_MODEL), BF16), _sds((T, D_MODEL), F32)],
        compiler_params=_cp("parallel"),
    )(attn, pool, x, w_out, g8, mod8)


FF_CHUNKS = ((0, 768), (768, 1536), (1536, 2304), (2304, D_FF))


def _ffn_fwd(x1, mod8, g8, wg, wu, wd, target=None):
    T = x1.shape[0]
    tm = min(T, 512)
    last = target is not None

    def body(*refs):
        x_ref, mod_ref, g_ref, wg_ref, wu_ref, wd_ref = refs[:6]
        t_ref = refs[6] if last else None
        h_ref, act_ref, ga_ref, gb_ref, f_ref, x2_ref = refs[6 + last:12 + last]
        xf = x_ref[...]
        h = (xf * _rstd(xf) * g_ref[2:3, :]) * (1.0 + mod_ref[4:5, :]) + mod_ref[3:4, :]
        hb = h.astype(BF16)
        h_ref[...] = hb
        f = jnp.zeros((tm, D_MODEL), F32)
        for lo, hi in FF_CHUNKS:
            a = _nt(hb, wg_ref[lo:hi, :])
            b = _nt(hb, wu_ref[lo:hi, :])
            sig = jax.nn.sigmoid(a)
            sl = a * sig
            act = (sl * b).astype(BF16)
            act_ref[:, lo:hi] = act
            ga_ref[:, lo:hi] = (b * (sig * (1.0 + a * (1.0 - sig)))).astype(BF16)
            gb_ref[:, lo:hi] = sl.astype(BF16)
            f = f + _mm(act, wd_ref[lo:hi, :])
        f_ref[...] = f
        x2 = xf + mod_ref[5:6, :] * (f * _rstd(f) * g_ref[3:4, :])
        if not last:
            x2_ref[...] = x2
        else:
            loss_ref = refs[13]

            @pl.when(pl.program_id(0) == 0)
            def _():
                loss_ref[...] = jnp.zeros_like(loss_ref)

            e = x2 - t_ref[...]
            x2_ref[...] = e * (1.0 / D_MODEL)
            loss_ref[...] += 0.5 * jnp.sum(jnp.mean(e * e, axis=-1, keepdims=True), axis=0, keepdims=True)

    act_shape = _sds((T, D_FF), BF16)
    weights = [_resident((D_FF, D_MODEL))] * 3
    return pl.pallas_call(
        body, name="ffn_fwd_loss" if last else "ffn_fwd", grid=(T // tm,),
        in_specs=[_rows(tm, D_MODEL), _full((8, D_MODEL)), _full((8, D_MODEL)), *weights]
        + ([_rows(tm, D_MODEL)] if last else []),
        out_specs=[_rows(tm, D_MODEL), _rows(tm, D_FF), _rows(tm, D_FF), _rows(tm, D_FF), _rows(tm, D_MODEL),
                   _rows(tm, D_MODEL)] + ([_full((8, 128))] if last else []),
        out_shape=[_sds((T, D_MODEL), BF16), act_shape, act_shape, act_shape, _sds((T, D_MODEL), F32),
                   _sds((T, D_MODEL), F32)] + ([_sds((8, 128), F32)] if last else []),
        compiler_params=_cp("arbitrary" if last else "parallel", vmem=FFN_VMEM_LIMIT),
    )(x1, mod8, g8, wg, wu, wd, *([target] if last else []))


def _ffn_bwd(dx2, f, ga, gb, x1, mod8, g8, wg, wu, wd):
    T = dx2.shape[0]
    tm = min(T, 256)

    def body(dx_ref, f_ref, ga_ref, gb_ref, x_ref, mod_ref, g_ref, wg_ref, wu_ref, wd_ref,
             dx1_ref, df_ref, da_ref, db_ref, red_ref):
        @pl.when(pl.program_id(0) == 0)
        def _():
            red_ref[...] = jnp.zeros_like(red_ref)

        dx = dx_ref[...]
        fv = f_ref[...]
        rstd = _rstd(fv)
        fhat = fv * rstd
        gpost = g_ref[3:4, :]
        gate = mod_ref[5:6, :]
        df, s_post = _norm_gain_bwd(dx, fhat, rstd, gate * gpost)
        red_ref[0:1, :] += gpost * s_post
        red_ref[1:2, :] += gate * s_post
        dfb = df.astype(BF16)
        df_ref[...] = dfb
        dh = jnp.zeros((tm, D_MODEL), F32)
        for lo, hi in FF_CHUNKS:
            dact = _nt(dfb, wd_ref[lo:hi, :])
            da = (dact * ga_ref[:, lo:hi].astype(F32)).astype(BF16)
            db = (dact * gb_ref[:, lo:hi].astype(F32)).astype(BF16)
            da_ref[:, lo:hi] = da
            db_ref[:, lo:hi] = db
            dh = dh + _mm(da, wg_ref[lo:hi, :]) + _mm(db, wu_ref[lo:hi, :])
        xf = x_ref[...]
        rstd1 = _rstd(xf)
        xhat = xf * rstd1
        gpre = g_ref[2:3, :]
        scale1 = 1.0 + mod_ref[4:5, :]
        dxn, s_pre = _norm_gain_bwd(dh, xhat, rstd1, scale1 * gpre)
        red_ref[2:3, :] += _colsum(dh)
        red_ref[3:4, :] += gpre * s_pre
        red_ref[4:5, :] += scale1 * s_pre
        dx1_ref[...] = dx + dxn

    act_shape = _sds((T, D_FF), BF16)
    return pl.pallas_call(
        body, name="ffn_bwd", grid=(T // tm,),
        in_specs=[_rows(tm, D_MODEL), _rows(tm, D_MODEL), _rows(tm, D_FF), _rows(tm, D_FF), _rows(tm, D_MODEL),
                  _full((8, D_MODEL)), _full((8, D_MODEL)),
                  _resident((D_FF, D_MODEL)), _resident((D_FF, D_MODEL)), _resident((D_FF, D_MODEL))],
        out_specs=[_rows(tm, D_MODEL), _rows(tm, D_MODEL), _rows(tm, D_FF), _rows(tm, D_FF), _full((8, D_MODEL))],
        out_shape=[_sds((T, D_MODEL), F32), _sds((T, D_MODEL), BF16), act_shape, act_shape, _sds((8, D_MODEL), F32)],
        compiler_params=_cp("arbitrary"),
    )(dx2, f, ga, gb, x1, mod8, g8, wg, wu, wd)


def _wgrad(a, b, name, after=None):
    T, K = a.shape
    N = b.shape[1]
    tt = min(T, WGRAD_TOKENS)
    tk = next(c for c in (1408, 640, 512, 256, 128) if K % c == 0)

    def body(a_ref, b_ref, *rest):
        o_ref = rest[-1]

        @pl.when(pl.program_id(1) == 0)
        def _():
            o_ref[...] = jnp.zeros_like(o_ref)

        o_ref[...] += _tn(a_ref[...], b_ref[...])

    extra = [] if after is None else [after]
    return pl.pallas_call(
        body, name=name, grid=(K // tk, T // tt),
        in_specs=[pl.BlockSpec((tt, tk), lambda i, t: (t, i)), pl.BlockSpec((tt, N), lambda i, t: (t, 0))]
        + [pl.BlockSpec(memory_space=pl.ANY)] * len(extra),
        out_specs=pl.BlockSpec((tk, N), lambda i, t: (i, 0)),
        out_shape=_sds((K, N), F32),
        compiler_params=_cp("parallel", "arbitrary"),
    )(a, b, *extra)


def _mix_bwd(dx1, mix, mod8, g8, w_out):
    T = dx1.shape[0]
    tm = min(T, 512)

    def body(dx_ref, mix_ref, mod_ref, g_ref, w_ref, dmix_ref, da_ref, dp_ref, red_ref):
        @pl.when(pl.program_id(0) == 0)
        def _():
            red_ref[...] = jnp.zeros_like(red_ref)

        dx = dx_ref[...]
        mv = mix_ref[...].astype(F32)
        rstd = _rstd(mv)
        mhat = mv * rstd
        gpost = g_ref[1:2, :]
        gate = mod_ref[2:3, :]
        dm, s_post = _norm_gain_bwd(dx, mhat, rstd, gate * gpost)
        red_ref[0:1, :] += gpost * s_post
        red_ref[1:2, :] += gate * s_post
        dmb = dm.astype(BF16)
        dmix_ref[...] = dmb
        da_ref[...] = _nt(dmb, w_ref[0:ATTN_W, :]).astype(BF16)
        dp_ref[...] = _nt(dmb, w_ref[ATTN_W:, :]).astype(BF16)

    return pl.pallas_call(
        body, name="mix_bwd", grid=(T // tm,),
        in_specs=[_rows(tm, D_MODEL), _rows(tm, D_MODEL), _full((8, D_MODEL)), _full((8, D_MODEL)),
                  _resident((D_MODEL, D_MODEL))],
        out_specs=[_rows(tm, D_MODEL), _rows(tm, ATTN_W), _rows(tm, POOL_W), _full((8, D_MODEL))],
        out_shape=[_sds((T, D_MODEL), BF16), _sds((T, ATTN_W), BF16), _sds((T, POOL_W), BF16),
                   _sds((8, D_MODEL), F32)],
        compiler_params=_cp("arbitrary"),
    )(dx1, mix, mod8, g8, w_out)


def _attn_bwd(q, kd, vd, lse, dattn, sink_b):
    T = q.shape[0]
    nb = T // BLK

    def body(q_ref, do_ref, lse_ref, kp_ref, kc_ref, vp_ref, vc_ref, sk_ref,
             dq_ref, dk_ref, dv_ref, dsk_ref, carry_k, carry_v):
        n = pl.program_id(0)

        @pl.when(n == 0)
        def _():
            carry_k[...] = jnp.zeros_like(carry_k)
            carry_v[...] = jnp.zeros_like(carry_v)
            dsk_ref[...] = jnp.zeros_like(dsk_ref)

        @pl.when(n < nb)
        def _():
            valid = _band_mask(n)
            for j in range(N_HEADS // GROUP):
                lanes = slice(j * 128, (j + 1) * 128)
                kcat = jnp.concatenate([kp_ref[:, lanes], kc_ref[:, lanes]], axis=0)
                vcat = jnp.concatenate([vp_ref[:, lanes], vc_ref[:, lanes]], axis=0)
                qs = _stack_heads(q_ref, j)
                dos = _stack_heads(do_ref, j)
                lse = _head_row(lse_ref, j)
                p = jnp.exp(jnp.where(valid, _nt(kcat, qs), NEG_INF) - lse)
                dp = _nt(vcat, dos)
                delta = jnp.sum(p * dp, axis=0, keepdims=True)
                ds = (p * (dp - delta)).astype(BF16)
                sink_term = jnp.exp(_head_row(sk_ref, j) - lse) * delta
                for r in range(GROUP):
                    h = GROUP * j + r
                    dsk_ref[h:h + 1, :] += -jnp.sum(sink_term[:, r * 128:(r + 1) * 128], axis=1, keepdims=True)
                dq = jnp.concatenate(_unstack_heads(_tn(ds, kcat)), axis=1)
                dq_ref[:, 2 * j * 128:(2 * j + 2) * 128] = dq.astype(BF16)
                dk = _mm(ds, qs)
                dv = _mm(p.astype(BF16), dos)
                dk_ref[:, lanes] = (carry_k[:, lanes] + dk[0:BLK]).astype(BF16)
                dv_ref[:, lanes] = (carry_v[:, lanes] + dv[0:BLK]).astype(BF16)
                carry_k[:, lanes] = dk[BLK:]
                carry_v[:, lanes] = dv[BLK:]

        @pl.when(n == nb)
        def _():
            dk_ref[...] = carry_k[...].astype(BF16)
            dv_ref[...] = carry_v[...].astype(BF16)

    cur = lambda n: (jnp.minimum(n, nb - 1), 0)
    prev = lambda n: (jnp.maximum(n - 1, 0), 0)
    return pl.pallas_call(
        body, name="attn_bwd", grid=(nb + 1,),
        in_specs=[pl.BlockSpec((BLK, ATTN_W), cur), pl.BlockSpec((BLK, ATTN_W), cur), pl.BlockSpec((N_HEADS, 128), cur),
                  pl.BlockSpec((BLK, KVD_W), prev), pl.BlockSpec((BLK, KVD_W), cur),
                  pl.BlockSpec((BLK, KVD_W), prev), pl.BlockSpec((BLK, KVD_W), cur),
                  _full((8, 128))],
        out_specs=[pl.BlockSpec((BLK, ATTN_W), cur), pl.BlockSpec((BLK, KVD_W), prev),
                   pl.BlockSpec((BLK, KVD_W), prev), _full((8, 128))],
        out_shape=[_sds((T, ATTN_W), BF16), _sds((T, KVD_W), BF16), _sds((T, KVD_W), BF16), _sds((8, 128), F32)],
        scratch_shapes=[pltpu.VMEM((BLK, KVD_W), F32), pltpu.VMEM((BLK, KVD_W), F32)],
        compiler_params=_cp("arbitrary"),
    )(q, dattn, lse, kd, kd, vd, vd, sink_b)


def _pool_bwd(dpool, pooled, pool_w, pool_scale):
    T = dpool.shape[0]
    tm = min(T, 512)
    nbk = T // tm
    ext_rows = tm + HALO

    def body(dp_ref, pl_ref, w_ref, sc_ref, du_ref, dw_ref, dsc_ref, halo):
        i = pl.program_id(0)

        @pl.when(i == 0)
        def _():
            halo[...] = jnp.zeros_like(halo)
            dw_ref[...] = jnp.zeros_like(dw_ref)
            dsc_ref[...] = jnp.zeros_like(dsc_ref)

        blk = nbk - 1 - i
        tpos = (blk * tm + lax.broadcasted_iota(jnp.int32, (tm, 1), 0)).astype(F32)
        for g, w in enumerate(POOL_WINDOWS):
            lanes = slice(g * 128, (g + 1) * 128)
            dp = dp_ref[:, lanes].astype(F32)
            pb = pl_ref[:, lanes]
            wg = w_ref[g].astype(BF16)
            z = _mm(pb, wg)
            dsc_ref[0:1, lanes] += _colsum(dp * z)
            dz = (dp * sc_ref[:, lanes]).astype(BF16)
            dw_ref[g] += _tn(pb, dz)
            dpl = _nt(dz, wg)
            e = dpl / jnp.minimum(tpos + 1.0, float(w))
            s = jnp.concatenate([e, halo[:, lanes]], axis=0)
            halo[:, lanes] = e[0:HALO, :]
            sh = 1
            while sh < w:
                s = s + pltpu.roll(s, ext_rows - sh, 0)
                sh *= 2
            du_ref[:, lanes] = (s[0:tm, :] - dpl).astype(BF16)

    rev = lambda i: (nbk - 1 - i, 0)
    return pl.pallas_call(
        body, name="pool_bwd", grid=(nbk,),
        in_specs=[pl.BlockSpec((tm, POOL_W), rev), pl.BlockSpec((tm, POOL_W), rev),
                  _full((4, 128, 128)), _full((1, POOL_W))],
        out_specs=[pl.BlockSpec((tm, POOL_W), rev), _full((4, 128, 128)), _full((8, POOL_W))],
        out_shape=[_sds((T, POOL_W), BF16), _sds((4, 128, 128), F32), _sds((8, POOL_W), F32)],
        scratch_shapes=[pltpu.VMEM((HALO, POOL_W), F32)],
        compiler_params=_cp("arbitrary"),
    )(dpool, pooled, pool_w, pool_scale)


def _in_bwd(dq, dk, dv, du, rc, rs1, rs2, x, dx1, mod8, g8, w_in):
    T = x.shape[0]
    tm = min(T, 512)

    def body(dq_ref, dk_ref, dv_ref, du_ref, c_ref, s1_ref, s2_ref, x_ref, dx1_ref, mod_ref, g_ref, w_ref,
             dx_ref, dproj_ref, red_ref, dbin_ref):
        @pl.when(pl.program_id(0) == 0)
        def _():
            red_ref[...] = jnp.zeros_like(red_ref)
            dbin_ref[...] = jnp.zeros_like(dbin_ref)

        c = c_ref[...]
        s1 = s1_ref[...]
        s2 = s2_ref[...]
        dqp = _rot_bwd(dq_ref[...].astype(F32) * (HEAD ** -0.5),
                       jnp.tile(c, (1, 4)), jnp.tile(s1, (1, 4)), jnp.tile(s2, (1, 4)))
        dkp = _rot_bwd(_fold_dup(dk_ref[...].astype(F32)), c, s1, s2)
        pieces = ((0, ATTN_W, dqp), (ATTN_W, ATTN_W + KV_W, dkp),
                  (ATTN_W + KV_W, ATTN_W + 2 * KV_W, _fold_dup(dv_ref[...].astype(F32))),
                  (ATTN_W + 2 * KV_W, IN_W, du_ref[...].astype(F32)))
        dh = jnp.zeros((tm, D_MODEL), F32)
        for lo, hi, val in pieces:
            dbin_ref[0:1, lo:hi] += _colsum(val)
            vb = val.astype(BF16)
            dproj_ref[:, lo:hi] = vb
            dh = dh + _mm(vb, w_ref[lo:hi, :])
        xf = x_ref[...]
        rstd = _rstd(xf)
        xhat = xf * rstd
        gpre = g_ref[0:1, :]
        scale1 = 1.0 + mod_ref[1:2, :]
        dxn, s_pre = _norm_gain_bwd(dh, xhat, rstd, scale1 * gpre)
        red_ref[0:1, :] += _colsum(dh)
        red_ref[1:2, :] += gpre * s_pre
        red_ref[2:3, :] += scale1 * s_pre
        dx_ref[...] = dx1_ref[...] + dxn

    return pl.pallas_call(
        body, name="in_bwd", grid=(T // tm,),
        in_specs=[_rows(tm, ATTN_W), _rows(tm, KVD_W), _rows(tm, KVD_W), _rows(tm, POOL_W),
                  _rows(tm, 128), _rows(tm, 128), _rows(tm, 128), _rows(tm, D_MODEL), _rows(tm, D_MODEL),
                  _full((8, D_MODEL)), _full((8, D_MODEL)), _resident((IN_W, D_MODEL))],
        out_specs=[_rows(tm, D_MODEL), _rows(tm, IN_W), _full((8, D_MODEL)), _full((8, IN_W))],
        out_shape=[_sds((T, D_MODEL), F32), _sds((T, IN_W), BF16), _sds((8, D_MODEL), F32), _sds((8, IN_W), F32)],
        compiler_params=_cp("arbitrary"),
    )(dq, dk, dv, du, rc, rs1, rs2, x, dx1, mod8, g8, w_in)


def _mod_fwd(c_all, ada_w, ada_b_sh):
    tn = 512

    def body(c_ref, w_ref, b_ref, o_ref):
        cv = c_ref[...]
        ca = (cv * jax.nn.sigmoid(cv)).astype(BF16)
        o_ref[...] = _mm(ca, w_ref[...].astype(BF16)) + b_ref[...]

    return pl.pallas_call(
        body, name="mod_fwd", grid=(2, ADA_SH // tn),
        in_specs=[_full((8, D_MODEL)), pl.BlockSpec((None, D_MODEL, tn), lambda l, j: (l, 0, j)),
                  pl.BlockSpec((None, 1, tn), lambda l, j: (l, 0, j))],
        out_specs=pl.BlockSpec((None, 8, tn), lambda l, j: (l, 0, j)),
        out_shape=_sds((2, 8, ADA_SH), F32),
        compiler_params=_cp("parallel", "parallel"),
    )(c_all, ada_w, ada_b_sh)


def _ada_wgrad(c_all_t, dmod_sh):
    tn = 512

    def body(c_ref, d_ref, o_ref):
        cv = c_ref[...]
        ca = cv * jax.nn.sigmoid(cv)
        o_ref[...] = jnp.dot(ca, d_ref[...], preferred_element_type=F32, precision=lax.Precision.HIGHEST)

    return pl.pallas_call(
        body, name="ada_wgrad", grid=(2, ADA_SH // tn),
        in_specs=[_full((D_MODEL, 8)), pl.BlockSpec((None, 8, tn), lambda l, j: (l, 0, j))],
        out_specs=pl.BlockSpec((None, D_MODEL, tn), lambda l, j: (l, 0, j)),
        out_shape=_sds((2, D_MODEL, ADA_SH), F32),
        compiler_params=_cp("parallel", "parallel"),
    )(c_all_t, dmod_sh)


def _sum_devices(g):
    R = g.shape[1]

    def body(g_ref, o_ref):
        acc = g_ref[0]
        for d in range(1, N_DEV):
            acc = acc + g_ref[d]
        o_ref[...] = acc

    return pl.pallas_call(
        body, name="sum_devices", grid=(1,),
        in_specs=[_full((N_DEV, R, 128))], out_specs=_full((R, 128)), out_shape=_sds((R, 128), F32),
        compiler_params=_cp("arbitrary"),
    )(g)


def _adamw(w, g, m, v, name):
    R, C = w.shape
    tr = R
    for cand in (256, 128, 64, 32, 16, 8):
        if R % cand == 0 and cand * C * 4 <= 2 * 1024 * 1024:
            tr = cand
            break

    def body(w_ref, g_ref, m_ref, v_ref, d_ref, nm_ref, nv_ref):
        gv = g_ref[...]
        mn = ADAM_B1 * m_ref[...] + (1.0 - ADAM_B1) * gv
        vn = ADAM_B2 * v_ref[...] + (1.0 - ADAM_B2) * (gv * gv)
        m_hat = mn / (1.0 - ADAM_B1 ** ADAM_STEP)
        v_hat = vn / (1.0 - ADAM_B2 ** ADAM_STEP)
        d_ref[...] = -ADAM_LR * (m_hat / (jnp.sqrt(v_hat) + ADAM_EPS) + ADAM_WD * w_ref[...])
        nm_ref[...] = mn
        nv_ref[...] = vn

    spec = pl.BlockSpec((tr, C), lambda i: (i, 0))
    out = _sds((R, C), F32)
    return pl.pallas_call(
        body, name=name, grid=(R // tr,),
        in_specs=[spec] * 4, out_specs=[spec] * 3, out_shape=[out] * 3,
        compiler_params=_cp("parallel"),
    )(w, g, m, v)


def _adamw_nd(w, g, m, v, name):
    shape = w.shape
    if w.ndim == 2 and shape[1] < 128:
        view = (1, shape[0] * shape[1])
    else:
        view = (-1, shape[-1])
    outs = _adamw(*[t.reshape(view) for t in (w, g, m, v)], name=name)
    return [o.reshape(shape) for o in outs]


def _coords():
    return lax.axis_index("x"), lax.axis_index("y"), lax.axis_index("c")


def _other_chips(x, y):
    return [(1 - x, y), (x, 1 - y), (1 - x, 1 - y)]


def _allgather8(blk, name):
    m_per, n = blk.shape

    def body(x_ref, out_ref, send_sems, recv_sems, local_sem):
        x, y, c = _coords()
        me, sibling = (x, y, c), (x, y, 1 - c)
        chips = _other_chips(x, y)

        def rows(px, py, pc):
            return out_ref.at[pl.ds((4 * px + 2 * py + pc) * m_per, m_per), :]

        def copy(k, block, to, src=None):
            return pltpu.make_async_remote_copy(
                src_ref=rows(*block) if src is None else src, dst_ref=rows(*block),
                send_sem=send_sems.at[k], recv_sem=recv_sems.at[k], device_id=to, device_id_type=MESH)

        mine = pltpu.make_async_copy(x_ref, rows(*me), local_sem)
        mine.start()
        first = [copy(0, me, sibling, src=x_ref)]
        first += [copy(1 + j, me, (*chip, c), src=x_ref) for j, chip in enumerate(chips)]
        for cp in first:
            cp.start()
        passed = [copy(4 + j, (*chip, c), sibling) for j, chip in enumerate(chips)]
        for j, chip in enumerate(chips):
            copy(1 + j, (*chip, c), me).wait_recv()
            passed[j].start()
        copy(0, sibling, me).wait_recv()
        for j, chip in enumerate(chips):
            copy(4 + j, (*chip, 1 - c), me).wait_recv()
        for cp in first + passed:
            cp.wait_send()
        mine.wait()

    return pl.pallas_call(
        body, name=name,
        out_shape=_sds((N_DEV * m_per, n), blk.dtype),
        in_specs=[pl.BlockSpec(memory_space=pltpu.VMEM)],
        out_specs=pl.BlockSpec(memory_space=pltpu.VMEM),
        scratch_shapes=[pltpu.SemaphoreType.DMA((7,)), pltpu.SemaphoreType.DMA((7,)), pltpu.SemaphoreType.DMA],
        compiler_params=pltpu.CompilerParams(vmem_limit_bytes=VMEM_LIMIT),
    )(blk)


def _row_tile(r, n):
    for cand in range(r, 15, -16):
        if r % cand == 0 and cand % 16 == 0 and cand * n * 4 <= 2 * 1024 * 1024:
            return cand
    return r


def _cast_slot(w, chip, name):
    r, n = w.shape
    tr = _row_tile(r, n)

    def body(chip_ref, w_ref, o_ref):
        o_ref[...] = w_ref[...].astype(BF16)

    grid_spec = pltpu.PrefetchScalarGridSpec(
        num_scalar_prefetch=1, grid=(r // tr,),
        in_specs=[pl.BlockSpec((tr, n), lambda i, ch: (i, 0))],
        out_specs=pl.BlockSpec((None, tr, n), lambda i, ch: (ch[0], i, 0)))
    return pl.pallas_call(
        body, name=name, grid_spec=grid_spec, out_shape=_sds((N_SHARD, r, n), BF16),
        compiler_params=_cp("arbitrary"),
    )(chip, w)


def _allgather_weights(bufs, name):
    nt = len(bufs)
    hom = [pl.BlockSpec(memory_space=pl.ANY)] * nt

    def body(*refs):
        outs = refs[nt:2 * nt]
        send_sems, recv_sems = refs[2 * nt:]
        x, y, c = _coords()
        sibling = (x, y, 1 - c)
        chips = _other_chips(x, y)

        def copy(t, k, block_chip, hc, to):
            r = outs[t].shape[1] // 2
            blk = outs[t].at[2 * block_chip[0] + block_chip[1], pl.ds(hc * r, r)]
            return pltpu.make_async_remote_copy(
                src_ref=blk, dst_ref=blk,
                send_sem=send_sems.at[t, k], recv_sem=recv_sems.at[t, k], device_id=to, device_id_type=MESH)

        started = []
        for t in range(nt):
            for j, chip in enumerate(chips):
                cp = copy(t, j, (x, y), c, (*chip, c))
                cp.start()
                started.append(cp)
        for t in range(nt):
            for j, chip in enumerate(chips):
                copy(t, j, chip, c, sibling).wait_recv()
                fw = copy(t, 3 + j, chip, c, sibling)
                fw.start()
                started.append(fw)
        for t in range(nt):
            for j, chip in enumerate(chips):
                copy(t, 3 + j, chip, 1 - c, sibling).wait_recv()
        for cp in started:
            cp.wait_send()

    return pl.pallas_call(
        body, name=name,
        out_shape=[_sds(b.shape, b.dtype) for b in bufs],
        in_specs=hom, out_specs=hom,
        input_output_aliases={t: t for t in range(nt)},
        scratch_shapes=[pltpu.SemaphoreType.DMA((nt, 6)), pltpu.SemaphoreType.DMA((nt, 6))],
    )(*bufs)


def _join_halves(tots, name):
    nt = len(tots)
    hom = [pl.BlockSpec(memory_space=pl.ANY)] * nt

    def body(*refs):
        outs = refs[nt:2 * nt]
        send_sems, recv_sems = refs[2 * nt:]
        x, y, c = _coords()
        sibling = (x, y, 1 - c)
        cps = []
        for t in range(nt):
            cp = pltpu.make_async_remote_copy(
                src_ref=outs[t].at[c], dst_ref=outs[t].at[c],
                send_sem=send_sems.at[t], recv_sem=recv_sems.at[t], device_id=sibling, device_id_type=MESH)
            cp.start()
            cps.append(cp)
        for t in range(nt):
            pltpu.make_async_remote_copy(
                src_ref=outs[t].at[c], dst_ref=outs[t].at[1 - c],
                send_sem=send_sems.at[t], recv_sem=recv_sems.at[t], device_id=sibling, device_id_type=MESH).wait_recv()
        for cp in cps:
            cp.wait_send()

    return pl.pallas_call(
        body, name=name,
        out_shape=[_sds(t.shape, t.dtype) for t in tots],
        in_specs=hom, out_specs=hom,
        input_output_aliases={t: t for t in range(nt)},
        scratch_shapes=[pltpu.SemaphoreType.DMA((nt,)), pltpu.SemaphoreType.DMA((nt,))],
    )(*tots)


def _pair_sum(g, recv, core, chip, name):
    _, _, r, n = g.shape
    tr = _row_tile(r, n)

    def body(core_ref, chip_ref, g_ref, r_ref, sb_ref, own_ref):
        tot = g_ref[...] + r_ref[...]
        sb_ref[...] = tot.astype(BF16)

        @pl.when(pl.program_id(1) == chip_ref[0])
        def _():
            own_ref[...] = tot

    grid_spec = pltpu.PrefetchScalarGridSpec(
        num_scalar_prefetch=2, grid=(r // tr, N_SHARD),
        in_specs=[pl.BlockSpec((None, None, tr, n), lambda i, s, co, ch: (s, co[0], i, 0)),
                  pl.BlockSpec((None, tr, n), lambda i, s, co, ch: (s, i, 0))],
        out_specs=[pl.BlockSpec((None, tr, n), lambda i, s, co, ch: (s, i, 0)),
                   pl.BlockSpec((tr, n), lambda i, s, co, ch: (i, 0))])
    return pl.pallas_call(
        body, name=name, grid_spec=grid_spec,
        out_shape=[_sds((N_SHARD, r, n), BF16), _sds((r, n), F32)],
        compiler_params=_cp("arbitrary", "arbitrary"),
    )(core, chip, g, recv)


def _chip_sum(own, recv, core, name):
    r, n = own.shape
    tr = _row_tile(r, n)

    def body(core_ref, o_ref, r_ref, t_ref):
        acc = o_ref[...]
        for j in range(3):
            acc = acc + r_ref[j].astype(F32)
        t_ref[...] = acc

    grid_spec = pltpu.PrefetchScalarGridSpec(
        num_scalar_prefetch=1, grid=(r // tr,),
        in_specs=[pl.BlockSpec((tr, n), lambda i, co: (i, 0)), pl.BlockSpec((3, tr, n), lambda i, co: (0, i, 0))],
        out_specs=pl.BlockSpec((None, tr, n), lambda i, co: (co[0], i, 0)))
    return pl.pallas_call(
        body, name=name, grid_spec=grid_spec, out_shape=_sds((2, r, n), F32),
        compiler_params=_cp("arbitrary"),
    )(core, own, recv)


_HBM = pl.BlockSpec(memory_space=pltpu.HBM)
_SEM = pl.BlockSpec(memory_space=pltpu.SEMAPHORE)
_EFFECT = pltpu.SideEffectType.DATAFLOW_SIDE_EFFECTING


def _ici_copies(srcs, dsts, send_sems, recv_sems, send_view, recv_view):
    x, y, c = _coords()
    out = []
    if send_view is None:
        for t in range(len(srcs)):
            r = srcs[t].shape[1] // 2
            out.append(pltpu.make_async_remote_copy(
                src_ref=srcs[t].at[:, pl.ds((1 - c) * r, r)], dst_ref=dsts[t],
                send_sem=send_sems.at[3 * t], recv_sem=recv_sems.at[3 * t],
                device_id=(x, y, 1 - c), device_id_type=MESH))
        return out
    for t in range(len(srcs)):
        for j, chip in enumerate(_other_chips(x, y)):
            out.append(pltpu.make_async_remote_copy(
                src_ref=send_view(srcs[t], chip, j, (x, y), c), dst_ref=recv_view(dsts[t], chip, j, (x, y), c),
                send_sem=send_sems.at[3 * t + j], recv_sem=recv_sems.at[3 * t + j],
                device_id=(*chip, c), device_id_type=MESH))
    return out


def _ici_start(srcs, dsts, after, send_view, recv_view, name):
    nt = len(srcs)
    inplace = dsts is None
    nbuf = nt if inplace else 2 * nt

    def body(*refs):
        send_sems, recv_sems = refs[nbuf + 1], refs[nbuf + 2]
        s_out = refs[nbuf + 3:nbuf + 3 + nt]
        d_out = s_out if inplace else refs[nbuf + 3 + nt:nbuf + 3 + 2 * nt]
        token = refs[-1]
        for cp in _ici_copies(s_out, d_out, send_sems, recv_sems, send_view, recv_view):
            cp.start()
        token[...] = jnp.zeros_like(token)

    bufs = list(srcs) + ([] if inplace else list(dsts))
    res = pl.pallas_call(
        body, name=name,
        out_shape=(pltpu.SemaphoreType.DMA((3 * nt,)), pltpu.SemaphoreType.DMA((3 * nt,)),
                   *[pltpu.HBM(b.shape, b.dtype) for b in bufs], _sds((8, 128), F32)),
        in_specs=[_HBM] * nbuf + [pl.BlockSpec(memory_space=pl.ANY)],
        out_specs=(_SEM, _SEM, *[_HBM] * nbuf, pl.BlockSpec(memory_space=pltpu.VMEM)),
        input_output_aliases={i: 2 + i for i in range(nbuf)},
        compiler_params=pltpu.CompilerParams(has_side_effects=_EFFECT),
    )(*[pltpu.with_memory_space_constraint(b, pltpu.HBM) for b in bufs], after)
    send_sems, recv_sems = res[0], res[1]
    s_thru = list(res[2:2 + nt])
    d_thru = s_thru if inplace else list(res[2 + nt:2 + 2 * nt])
    return send_sems, recv_sems, s_thru, d_thru, res[-1]


def _ici_wait(send_sems, recv_sems, srcs, dsts, after, send_view, recv_view, name):
    nt = len(srcs)
    inplace = dsts is None
    nbuf = nt if inplace else 2 * nt

    def body(*refs):
        send_ref, recv_ref = refs[nbuf], refs[nbuf + 1]
        s_out = refs[nbuf + 3:nbuf + 3 + nt]
        d_out = s_out if inplace else refs[nbuf + 3 + nt:nbuf + 3 + 2 * nt]
        for cp in _ici_copies(s_out, d_out, send_ref, recv_ref, send_view, recv_view):
            cp.wait_send()
            cp.wait_recv()

    bufs = list(srcs) + ([] if inplace else list(dsts))
    res = pl.pallas_call(
        body, name=name,
        out_shape=tuple(pltpu.HBM(b.shape, b.dtype) for b in bufs),
        in_specs=[_HBM] * nbuf + [_SEM, _SEM, pl.BlockSpec(memory_space=pl.ANY)],
        out_specs=tuple([_HBM] * nbuf),
        input_output_aliases={i: i for i in range(nbuf)},
        compiler_params=pltpu.CompilerParams(has_side_effects=_EFFECT),
    )(*bufs, send_sems, recv_sems, after)
    return list(res[:nt]) if inplace else (list(res[:nt]), list(res[nt:]))


def _w_half(buf, chip, c):
    r = buf.shape[1] // 2
    return buf.at[2 * chip[0] + chip[1], pl.ds(c * r, r)]


def _ag_send_view(buf, chip, j, me, c):
    return _w_half(buf, me, c)


def _ag_recv_view(buf, chip, j, me, c):
    return _w_half(buf, me, c)


def _rs_send_view(buf, chip, j, me, c):
    return buf.at[2 * chip[0] + chip[1]]


def _rs_recv_view(buf, chip, j, me, c):
    return buf.at[j]


def _ag_forward(bufs, name):
    nt = len(bufs)
    hom = [pl.BlockSpec(memory_space=pl.ANY)] * nt

    def body(*refs):
        outs = refs[nt:2 * nt]
        send_sems, recv_sems = refs[2 * nt:]
        x, y, c = _coords()
        sibling = (x, y, 1 - c)
        chips = _other_chips(x, y)

        def copy(t, j, hc):
            blk = _w_half(outs[t], chips[j], hc)
            return pltpu.make_async_remote_copy(
                src_ref=blk, dst_ref=blk, send_sem=send_sems.at[t, j], recv_sem=recv_sems.at[t, j],
                device_id=sibling, device_id_type=MESH)

        started = [copy(t, j, c) for t in range(nt) for j in range(3)]
        for cp in started:
            cp.start()
        for t in range(nt):
            for j in range(3):
                copy(t, j, 1 - c).wait_recv()
        for cp in started:
            cp.wait_send()

    return pl.pallas_call(
        body, name=name,
        out_shape=[_sds(b.shape, b.dtype) for b in bufs],
        in_specs=hom, out_specs=hom,
        input_output_aliases={t: t for t in range(nt)},
        scratch_shapes=[pltpu.SemaphoreType.DMA((nt, 3)), pltpu.SemaphoreType.DMA((nt, 3))],
    )(*bufs)


def _rs_swap_begin(grads, after, tag):
    land = [lax.empty((N_SHARD, g.shape[1] // 2, g.shape[2]), g.dtype) for g in grads]
    send_sems, recv_sems, s_thru, d_thru, token = _ici_start(grads, land, after, None, None, name="rs_swapgo_" + tag)
    return dict(sems=(send_sems, recv_sems), grads=s_thru, land=d_thru, tag=tag), token


def _rs_scatter_begin(swap, after):
    tag = swap["tag"]
    x, y, c = _coords()
    core = jnp.reshape(c, (1,)).astype(jnp.int32)
    chip = jnp.reshape(2 * x + y, (1,)).astype(jnp.int32)
    grads, recv = _ici_wait(*swap["sems"], swap["grads"], swap["land"], after, None, None, name="rs_swapend_" + tag)
    sums, owns = [], []
    for t, (g, rv) in enumerate(zip(grads, recv)):
        r = g.shape[1] // 2
        sb, own = _pair_sum(g.reshape(N_SHARD, 2, r, g.shape[2]), rv, core, chip, name=f"rs_pair_{tag}_{t}")
        sums.append(sb)
        owns.append(own)
    land = [lax.empty((3,) + s.shape[1:], s.dtype) for s in sums]
    send_sems, recv_sems, s_thru, d_thru, token = _ici_start(
        sums, land, after, _rs_send_view, _rs_recv_view, name="rs_start_" + tag)
    return dict(sems=(send_sems, recv_sems), sums=s_thru, land=d_thru, owns=owns, core=core, tag=tag), token


def _rs_end(state, after):
    tag = state["tag"]
    _, got = _ici_wait(*state["sems"], state["sums"], state["land"], after, _rs_send_view, _rs_recv_view,
                       name="rs_wait_" + tag)
    tots = [_chip_sum(o, gt, state["core"], name=f"rs_chip_{tag}_{t}")
            for t, (o, gt) in enumerate(zip(state["owns"], got))]
    full = _join_halves(tots, name="rs_join_" + tag)
    return [f.reshape(2 * f.shape[1], f.shape[2]) for f in full]


def _rope_lane_table():
    d = jnp.arange(128) % HEAD
    inv_freq = ROPE_THETA ** (-jnp.arange(0, ROT, 2, dtype=F32) / ROT)
    rot = d < ROT
    rows = [jnp.where(rot, inv_freq[d % (ROT // 2)], 0.0), rot.astype(F32),
            (d < ROT // 2).astype(F32), jnp.logical_and(d >= ROT // 2, rot).astype(F32)]
    return jnp.concatenate([jnp.stack(rows), jnp.zeros((4, 128), F32)], axis=0)


def _pad8(rows):
    return jnp.concatenate([rows, jnp.zeros((8 - rows.shape[0], rows.shape[1]), F32)], axis=0)


def kernel(x, c, positions, ada_w, ada_b, w_in, b_in, sinks, pool_w, pool_scale, w_out, w_gate, w_up, w_down, g_pre_mix, g_post_mix, g_pre_ffn, g_post_ffn, loss_target, m_ada_w, m_ada_b, m_w_in, m_b_in, m_sinks, m_pool_w, m_pool_scale, m_w_out, m_w_gate, m_w_up, m_w_down, m_g_pre_mix, m_g_post_mix, m_g_pre_ffn, m_g_post_ffn, v_ada_w, v_ada_b, v_w_in, v_b_in, v_sinks, v_pool_w, v_pool_scale, v_w_out, v_w_gate, v_w_up, v_w_down, v_g_pre_mix, v_g_post_mix, v_g_pre_ffn, v_g_post_ffn):
    T = x.shape[1]
    n_layers = ada_w.shape[0]
    ax, ay, ac = _coords()
    my_dev = 4 * ax + 2 * ay + ac
    my_chip = 2 * ax + ay
    x0 = x.reshape(T, D_MODEL)
    target = loss_target.reshape(T, D_MODEL)

    c_all = _allgather8(c.reshape(8, 128), name="ag_c").reshape(N_DEV, D_MODEL)
    ada_b_sh = lax.dynamic_slice_in_dim(ada_b, my_chip * ADA_SH, ADA_SH, axis=1).reshape(n_layers, 1, ADA_SH)
    mod_part = _mod_fwd(c_all, ada_w, ada_b_sh)
    mod_all = _allgather8(mod_part.reshape(n_layers * 8, ADA_SH), name="ag_mod")
    mod_all = mod_all.reshape(N_DEV, n_layers, 8, ADA_SH)[0::2]
    mod_mine = lax.dynamic_index_in_dim(mod_all, my_dev, axis=2, keepdims=False)
    mod = jnp.transpose(mod_mine, (1, 0, 2)).reshape(n_layers, 6, D_MODEL)

    pos_b = jnp.broadcast_to(positions.reshape(T, 1), (T, 128))
    rc, rs1, rs2 = _rope_tables(pos_b, _rope_lane_table())

    chip1 = jnp.reshape(my_chip, (1,)).astype(jnp.int32)

    def tr(t):
        return jnp.transpose(t, (0, 2, 1))

    w_in_t, w_gate_t, w_up_t = tr(w_in), tr(w_gate), tr(w_up)

    def cast_layer(l):
        return [_cast_slot(w[l], chip1, name=f"cast_{nm}{l}")
                for nm, w in (("w_in", w_in_t), ("w_out", w_out), ("w_gate", w_gate_t), ("w_up", w_up_t),
                              ("w_down", w_down))]

    def as_operands(bufs):
        gin, gout, gg, gu, gd = bufs
        return (gin.reshape(IN_W, D_MODEL), gout.reshape(D_MODEL, D_MODEL), gg.reshape(D_FF, D_MODEL),
                gu.reshape(D_FF, D_MODEL), gd.reshape(D_FF, D_MODEL))

    bufs0 = cast_layer(0)
    win0 = _allgather_weights(bufs0[:1], name="ag_w0_in")
    rest_send, rest_recv, rest_bufs, _, ag_token = _ici_start(
        bufs0[1:], None, win0[0], _ag_send_view, _ag_recv_view, name="ag_start_0")
    weights = [None] * n_layers

    saved = []
    xl = x0
    for l in range(n_layers):
        mod8 = _pad8(mod[l])
        if l + 1 < n_layers:
            ag_send, ag_recv, ag_bufs, _, ag_token = _ici_start(
                cast_layer(l + 1), None, ag_token, _ag_send_view, _ag_recv_view, name=f"ag_start_{l + 1}")
        if l == 0 or l + 1 < n_layers:
            mod8 = mod8 + ag_token[0, 0]
        g8 = _pad8(jnp.stack([g_pre_mix[l], g_post_mix[l], g_pre_ffn[l], g_post_ffn[l]]))
        sink_b = jnp.broadcast_to(sinks[l][:, None], (N_HEADS, 128))
        psc = pool_scale[l].reshape(1, POOL_W)
        win = win0[0].reshape(IN_W, D_MODEL) if l == 0 else weights[l][0]
        h, q, k, v, u = _fwd_in(xl, mod8, g8, win, b_in[l].reshape(1, IN_W), rc, rs1, rs2)
        attn, lse = _attn_fwd(q, k, v, sink_b)
        pool, pooled = _pool_fwd(u, pool_w[l], psc)
        if l == 0:
            arrived = _ici_wait(rest_send, rest_recv, rest_bufs, None, pool, _ag_send_view, _ag_recv_view,
                                name="ag_wait_0")
            weights[0] = as_operands(win0 + _ag_forward(arrived, name="ag_fwd_0"))
        win, wout, wg, wu, wd = weights[l]
        mix, x1 = _fwd_out(attn, pool, xl, wout, g8, mod8)
        if l + 1 < n_layers:
            h2, act, ga, gb, f, x2 = _ffn_fwd(x1, mod8, g8, wg, wu, wd)
        else:
            h2, act, ga, gb, f, x2, loss_tile = _ffn_fwd(x1, mod8, g8, wg, wu, wd, target=target)
        saved.append(dict(x=xl, h=h, q=q, k=k, v=v, lse=lse, attn=attn, pool=pool, pooled=pooled, mix=mix,
                          x1=x1, h2=h2, act=act, ga=ga, gb=gb, f=f, mod8=mod8, g8=g8, sink_b=sink_b, psc=psc))
        xl = x2
        if l + 1 < n_layers:
            arrived = _ici_wait(ag_send, ag_recv, ag_bufs, None, x2, _ag_send_view, _ag_recv_view,
                                name=f"ag_wait_{l + 1}")
            weights[l + 1] = as_operands(_ag_forward(arrived, name=f"ag_fwd_{l + 1}"))

    dy = xl
    loss = lax.psum(loss_tile[0, 0], ("x", "y", "c"))

    small = [None] * n_layers
    dmod_rows = [None] * n_layers
    reduced = [dict() for _ in range(n_layers)]
    att_swap = None
    dx = dy
    for l in reversed(range(n_layers)):
        s = saved[l]
        win, wout, wg, wu, wd = weights[l]
        if att_swap is not None:
            s = dict(s, mod8=s["mod8"] + att_swap[1][0, 0])
        dx1, df, da, db, red_f = _ffn_bwd(dx, s["f"], s["ga"], s["gb"], s["x1"], s["mod8"], s["g8"], wg, wu, wd)
        token = None
        if att_swap is not None:
            att_scatter = _rs_scatter_begin(att_swap[0], dx1)
            token = att_scatter[1]
        ffn_shards = (N_SHARD, FF_SH, D_MODEL)
        g_wd = _wgrad(s["act"], df, name="wgrad_down", after=token).reshape(ffn_shards)
        g_wg = _wgrad(da, s["h2"], name="wgrad_gate").reshape(ffn_shards)
        g_wu = _wgrad(db, s["h2"], name="wgrad_up").reshape(ffn_shards)
        ffn_swap = _rs_swap_begin([g_wg, g_wu, g_wd], dx1, tag=f"{l}f")
        if att_swap is not None:
            got = _rs_end(att_scatter[0], ffn_swap[1])
            reduced[l + 1].update(w_in=got[0], w_out=got[1])
        s = dict(s, mod8=s["mod8"] + ffn_swap[1][0, 0])
        dmix, dattn, dpool, red_c = _mix_bwd(dx1, s["mix"], s["mod8"], s["g8"], wout)
        g_wout = jnp.concatenate([_wgrad(s["attn"], dmix, name="wgrad_out_a"),
                                  _wgrad(s["pool"], dmix, name="wgrad_out_p")], axis=0)
        ffn_scatter = _rs_scatter_begin(ffn_swap[0], dattn)
        dq, dk, dv, dsink = _attn_bwd(s["q"], s["k"], s["v"], s["lse"], dattn, s["sink_b"] + ffn_scatter[1][0:1, :])
        du, g_poolw, dpsc = _pool_bwd(dpool, s["pooled"], pool_w[l], s["psc"])
        dx, dproj, red_d, dbin = _in_bwd(dq, dk, dv, du, rc, rs1, rs2, s["x"], dx1, s["mod8"], s["g8"], win)
        g_win = _wgrad(dproj, s["h"], name="wgrad_in")
        g_win_sh = g_win.reshape(N_SHARD, IN_SH, D_MODEL)
        got = _rs_end(ffn_scatter[0], dproj)
        reduced[l].update(w_gate=got[0], w_up=got[1], w_down=got[2])
        att_swap = _rs_swap_begin([g_win_sh, g_wout.reshape(N_SHARD, OUT_SH, D_MODEL)], dx, tag=f"{l}a")
        dmod_rows[l] = jnp.concatenate([red_d[0], red_d[1], red_c[0], red_f[2], red_f[3], red_f[0]])
        small[l] = jnp.concatenate([red_d[2], red_c[1], red_f[4], red_f[1], dbin[0], dpsc[0], dsink[:, 0],
                                    jnp.zeros((120,), F32), g_poolw.reshape(-1)])
    grad_x = dx.reshape(1, T, D_MODEL)

    per_layer = small[0].shape[0]
    rows_small = n_layers * per_layer // 128
    rows_mod = n_layers * 6 * D_MODEL // 128
    rows_pad = -(rows_small + rows_mod) % 8
    pack = jnp.concatenate(small + dmod_rows + [jnp.zeros((rows_pad * 128,), F32)]).reshape(-1, 128)
    pack = pack + att_swap[1][0, 0]
    gathered = _allgather8(pack, name="ag_small").reshape(N_DEV, pack.shape[0], 128)
    summed = _sum_devices(gathered)
    att_scatter = _rs_scatter_begin(att_swap[0], summed)
    small_sum = summed[:rows_small].reshape(n_layers, per_layer)
    o = 0
    small_g = {}
    for nm, width in (("g_pre_mix", D_MODEL), ("g_post_mix", D_MODEL), ("g_pre_ffn", D_MODEL),
                      ("g_post_ffn", D_MODEL), ("b_in", IN_W), ("pool_scale", POOL_W), ("sinks", 128),
                      ("pool_w", 4 * 128 * 128)):
        small_g[nm] = small_sum[:, o:o + width]
        o += width
    small_g["sinks"] = small_g["sinks"][:, :N_HEADS]
    small_g["pool_w"] = small_g["pool_w"].reshape(n_layers, 4, 128, 128)
    small_g["ada_b"] = summed[rows_small:rows_small + rows_mod].reshape(n_layers, 6 * D_MODEL)
    dmod_all = gathered[:, rows_small:rows_small + rows_mod].reshape(N_DEV, n_layers, N_SHARD, ADA_SH)
    dmod_sh = lax.dynamic_index_in_dim(dmod_all, my_chip, axis=2, keepdims=False)
    g_ada_w = _ada_wgrad(jnp.transpose(c_all), jnp.transpose(dmod_sh, (1, 0, 2)))

    grads = dict(ada_w=g_ada_w, ada_b=small_g["ada_b"], b_in=small_g["b_in"], sinks=small_g["sinks"],
                 pool_w=small_g["pool_w"], pool_scale=small_g["pool_scale"], g_pre_mix=small_g["g_pre_mix"],
                 g_post_mix=small_g["g_post_mix"], g_pre_ffn=small_g["g_pre_ffn"], g_post_ffn=small_g["g_post_ffn"])
    params = dict(ada_w=(ada_w, m_ada_w, v_ada_w), ada_b=(ada_b, m_ada_b, v_ada_b), w_in=(w_in, m_w_in, v_w_in),
                  b_in=(b_in, m_b_in, v_b_in), sinks=(sinks, m_sinks, v_sinks), pool_w=(pool_w, m_pool_w, v_pool_w),
                  pool_scale=(pool_scale, m_pool_scale, v_pool_scale), w_out=(w_out, m_w_out, v_w_out),
                  w_gate=(w_gate, m_w_gate, v_w_gate), w_up=(w_up, m_w_up, v_w_up),
                  w_down=(w_down, m_w_down, v_w_down), g_pre_mix=(g_pre_mix, m_g_pre_mix, v_g_pre_mix),
                  g_post_mix=(g_post_mix, m_g_post_mix, v_g_post_mix), g_pre_ffn=(g_pre_ffn, m_g_pre_ffn, v_g_pre_ffn),
                  g_post_ffn=(g_post_ffn, m_g_post_ffn, v_g_post_ffn))
    names = list(params)
    updates = {nm: _adamw_nd(*params[nm][:1], grads[nm], *params[nm][1:], name="adamw_" + nm) for nm in grads}

    got = _rs_end(att_scatter[0], updates["ada_w"][0])
    reduced[0].update(w_in=got[0], w_out=got[1])
    for nm in ("w_in", "w_out", "w_gate", "w_up", "w_down"):
        g = jnp.stack([reduced[l][nm] for l in range(n_layers)])
        if nm in ("w_in", "w_gate", "w_up"):
            upd = _adamw_nd(tr(params[nm][0]), g, tr(params[nm][1]), tr(params[nm][2]), name="adamw_" + nm)
            grads[nm], updates[nm] = tr(g), [tr(u) for u in upd]
        else:
            grads[nm], updates[nm] = g, _adamw_nd(params[nm][0], g, *params[nm][1:], name="adamw_" + nm)
    return (loss, grad_x, *[grads[nm] for nm in names], *[updates[nm][0] for nm in names],
            *[updates[nm][1] for nm in names], *[updates[nm][2] for nm in names])
```

```python
import functools

import jax
import jax.numpy as jnp
from jax import lax
from jax.experimental import pallas as pl
from jax.experimental.pallas import tpu as pltpu

F32 = jnp.float32
BF16 = jnp.bfloat16
MESH = pl.DeviceIdType.MESH

D_MODEL = 1024
ATTN_W = 512
KV_W = 128
KVD_W = 256
POOL_W = 512
IN_W = 1280
D_FF = 2816
N_SHARD = 4
FF_SH = D_FF // N_SHARD
IN_SH = IN_W // N_SHARD
OUT_SH = D_MODEL // N_SHARD
ADA_SH = 6 * D_MODEL // N_SHARD
HEAD = 64
N_HEADS = 8
GROUP = 4
BLK = 128
POOL_WINDOWS = (2, 4, 8, 16)
HALO = 16
ROT = 16
ROPE_THETA = 500000.0
EPS = 1e-6
NEG_INF = -1e30
N_DEV = 8

ADAM_LR = 0.001
ADAM_B1 = 0.9
ADAM_B2 = 0.999
ADAM_EPS = 1e-08
ADAM_WD = 0.01
ADAM_STEP = 10

VMEM_LIMIT = 48 * 1024 * 1024
FFN_VMEM_LIMIT = 60 * 1024 * 1024
WGRAD_TOKENS = 2048


def _cp(*sem, vmem=VMEM_LIMIT):
    return pltpu.CompilerParams(dimension_semantics=sem, vmem_limit_bytes=vmem)


def _full(shape):
    nd = len(shape)
    return pl.BlockSpec(shape, lambda *_: (0,) * nd)


def _resident(shape):
    nd = len(shape)
    return pl.BlockSpec(shape, lambda *_: (0,) * nd, pipeline_mode=pl.Buffered(1))


def _rows(tm, ncol):
    return pl.BlockSpec((tm, ncol), lambda i: (i, 0))


def _sds(shape, dtype):
    return jax.ShapeDtypeStruct(shape, dtype)


def _nt(a, b):
    return lax.dot_general(a, b, (((1,), (1,)), ((), ())), preferred_element_type=F32)


def _tn(a, b):
    return lax.dot_general(a, b, (((0,), (0,)), ((), ())), preferred_element_type=F32)


def _mm(a, b):
    return jnp.dot(a, b, preferred_element_type=F32)


def _rstd(x):
    return lax.rsqrt(jnp.mean(x * x, axis=-1, keepdims=True) + EPS)


def _colsum(x):
    return jnp.sum(x, axis=0, keepdims=True)


def _norm_gain_bwd(dy, xhat, rstd, gain):
    p = dy * xhat
    dx = rstd * (dy * gain - xhat * jnp.mean(p * gain, axis=-1, keepdims=True))
    return dx, _colsum(p)


def _rope_tables(pos_b, lane_tab):
    T = pos_b.shape[0]
    tm = min(T, 1024)

    def body(pos_ref, tab_ref, c_ref, s1_ref, s2_ref):
        ang = pos_ref[...].astype(F32) * tab_ref[0:1, :]
        cs = jnp.cos(ang)
        sn = jnp.sin(ang)
        m_rot = tab_ref[1:2, :]
        c_ref[...] = cs * m_rot + (1.0 - m_rot)
        s1_ref[...] = -sn * tab_ref[2:3, :]
        s2_ref[...] = sn * tab_ref[3:4, :]

    out = _sds((T, 128), F32)
    return pl.pallas_call(
        body, name="rope_tables", grid=(T // tm,),
        in_specs=[_rows(tm, 128), _full((8, 128))],
        out_specs=[_rows(tm, 128)] * 3, out_shape=[out] * 3,
        compiler_params=_cp("parallel"),
    )(pos_b, lane_tab)


def _rot_fwd(t, c, s1, s2):
    w = t.shape[-1]
    return t * c + pltpu.roll(t, w - 8, 1) * s1 + pltpu.roll(t, 8, 1) * s2


def _rot_bwd(d, c, s1, s2):
    w = d.shape[-1]
    return d * c + pltpu.roll(d * s1, 8, 1) + pltpu.roll(d * s2, w - 8, 1)


def _store_dup(ref, t):
    low = lax.broadcasted_iota(jnp.int32, t.shape, 1) < HEAD
    sw = pltpu.roll(t, HEAD, 1)
    ref[:, 0:128] = jnp.where(low, t, sw).astype(BF16)
    ref[:, 128:256] = jnp.where(low, sw, t).astype(BF16)


def _fold_dup(d):
    low = lax.broadcasted_iota(jnp.int32, (d.shape[0], 128), 1) < HEAD
    d0 = d[:, 0:128]
    d1 = d[:, 128:256]
    return jnp.where(low, d0 + pltpu.roll(d0, HEAD, 1), d1 + pltpu.roll(d1, HEAD, 1))


def _fwd_in(x, mod8, g8, w_in, b_in, rc, rs1, rs2):
    T = x.shape[0]
    tm = min(T, 512)

    def body(x_ref, mod_ref, g_ref, w_ref, b_ref, c_ref, s1_ref, s2_ref,
             h_ref, q_ref, k_ref, v_ref, u_ref):
        xf = x_ref[...]
        h = (xf * _rstd(xf) * g_ref[0:1, :]) * (1.0 + mod_ref[1:2, :]) + mod_ref[0:1, :]
        hb = h.astype(BF16)
        h_ref[...] = hb
        c = c_ref[...]
        s1 = s1_ref[...]
        s2 = s2_ref[...]
        q = _nt(hb, w_ref[0:ATTN_W, :]) + b_ref[:, 0:ATTN_W]
        q = _rot_fwd(q, jnp.tile(c, (1, 4)), jnp.tile(s1, (1, 4)), jnp.tile(s2, (1, 4)))
        q_ref[...] = (q * (HEAD ** -0.5)).astype(BF16)
        k = _nt(hb, w_ref[ATTN_W:ATTN_W + KV_W, :]) + b_ref[:, ATTN_W:ATTN_W + KV_W]
        _store_dup(k_ref, _rot_fwd(k, c, s1, s2))
        v = _nt(hb, w_ref[ATTN_W + KV_W:ATTN_W + 2 * KV_W, :]) + b_ref[:, ATTN_W + KV_W:ATTN_W + 2 * KV_W]
        _store_dup(v_ref, v)
        u_ref[...] = _nt(hb, w_ref[ATTN_W + 2 * KV_W:IN_W, :]) + b_ref[:, ATTN_W + 2 * KV_W:IN_W]

    return pl.pallas_call(
        body, name="fwd_in", grid=(T // tm,),
        in_specs=[_rows(tm, D_MODEL), _full((8, D_MODEL)), _full((8, D_MODEL)),
                  _resident((IN_W, D_MODEL)), _full((1, IN_W)),
                  _rows(tm, 128), _rows(tm, 128), _rows(tm, 128)],
        out_specs=[_rows(tm, D_MODEL), _rows(tm, ATTN_W), _rows(tm, KVD_W), _rows(tm, KVD_W), _rows(tm, POOL_W)],
        out_shape=[_sds((T, D_MODEL), BF16), _sds((T, ATTN_W), BF16), _sds((T, KVD_W), BF16),
                   _sds((T, KVD_W), BF16), _sds((T, POOL_W), F32)],
        compiler_params=_cp("parallel"),
    )(x, mod8, g8, w_in, b_in, rc, rs1, rs2)


def _band_mask(n):
    kk = lax.broadcasted_iota(jnp.int32, (2 * BLK, BLK), 0)
    qi = lax.broadcasted_iota(jnp.int32, (2 * BLK, BLK), 1)
    first = jnp.where(n > 0, 0, 2 * BLK)
    in_prev = jnp.logical_and(kk < BLK, kk > qi + first)
    in_cur = jnp.logical_and(kk >= BLK, (kk - BLK) <= qi)
    one = jnp.logical_or(in_prev, in_cur)
    return jnp.concatenate([one] * GROUP, axis=1)


def _head_row(ref, j):
    return jnp.concatenate([ref[GROUP * j + r:GROUP * j + r + 1, :] for r in range(GROUP)], axis=1)


def _stack_heads(x_ref, j):
    low = lax.broadcasted_iota(jnp.int32, (BLK, 128), 1) < HEAD
    parts = []
    for gp in (2 * j, 2 * j + 1):
        x2 = x_ref[:, gp * 128:(gp + 1) * 128]
        parts.append(jnp.where(low, x2, jnp.zeros_like(x2)))
        parts.append(jnp.where(low, jnp.zeros_like(x2), x2))
    return jnp.concatenate(parts, axis=0)


def _unstack_heads(o):
    low = lax.broadcasted_iota(jnp.int32, (BLK, 128), 1) < HEAD
    return [jnp.where(low, o[0:BLK], o[BLK:2 * BLK]), jnp.where(low, o[2 * BLK:3 * BLK], o[3 * BLK:4 * BLK])]


def _attn_fwd(q, kd, vd, sink_b):
    T = q.shape[0]
    nb = T // BLK

    def body(q_ref, kp_ref, kc_ref, vp_ref, vc_ref, sk_ref, o_ref, lse_ref):
        valid = _band_mask(pl.program_id(0))
        for j in range(N_HEADS // GROUP):
            lanes = slice(j * 128, (j + 1) * 128)
            kcat = jnp.concatenate([kp_ref[:, lanes], kc_ref[:, lanes]], axis=0)
            vcat = jnp.concatenate([vp_ref[:, lanes], vc_ref[:, lanes]], axis=0)
            s = jnp.where(valid, _nt(kcat, _stack_heads(q_ref, j)), NEG_INF)
            sk = _head_row(sk_ref, j)
            m = jnp.maximum(jnp.max(s, axis=0, keepdims=True), sk)
            p = jnp.exp(s - m)
            den = jnp.sum(p, axis=0, keepdims=True) + jnp.exp(sk - m)
            p = p * (1.0 / den)
            o = _tn(p.astype(BF16), vcat)
            o_ref[:, 2 * j * 128:(2 * j + 2) * 128] = jnp.concatenate(_unstack_heads(o), axis=1).astype(BF16)
            lse = m + jnp.log(den)
            for r in range(GROUP):
                lse_ref[GROUP * j + r:GROUP * j + r + 1, :] = lse[:, r * 128:(r + 1) * 128]

    prev = lambda n: (jnp.maximum(n - 1, 0), 0)
    cur = lambda n: (n, 0)
    return pl.pallas_call(
        body, name="attn_fwd", grid=(nb,),
        in_specs=[pl.BlockSpec((BLK, ATTN_W), cur),
                  pl.BlockSpec((BLK, KVD_W), prev), pl.BlockSpec((BLK, KVD_W), cur),
                  pl.BlockSpec((BLK, KVD_W), prev), pl.BlockSpec((BLK, KVD_W), cur),
                  _full((8, 128))],
        out_specs=[pl.BlockSpec((BLK, ATTN_W), cur), pl.BlockSpec((N_HEADS, 128), cur)],
        out_shape=[_sds((T, ATTN_W), BF16), _sds((nb * N_HEADS, 128), F32)],
        compiler_params=_cp("parallel"),
    )(q, kd, kd, vd, vd, sink_b)


def _pool_fwd(u, pool_w, pool_scale):
    T = u.shape[0]
    tm = min(T, 512)

    def body(u_ref, w_ref, sc_ref, out_ref, pooled_ref, halo):
        i = pl.program_id(0)

        @pl.when(i == 0)
        def _():
            halo[...] = jnp.zeros_like(halo)

        ub = u_ref[...]
        ext = jnp.concatenate([halo[...], ub], axis=0)
        halo[...] = ub[tm - HALO:, :]
        tpos = (i * tm + lax.broadcasted_iota(jnp.int32, (tm, 1), 0)).astype(F32)
        for g, w in enumerate(POOL_WINDOWS):
            lanes = slice(g * 128, (g + 1) * 128)
            s = ext[:, lanes]
            sh = 1
            while sh < w:
                s = s + pltpu.roll(s, sh, 0)
                sh *= 2
            cnt = jnp.minimum(tpos + 1.0, float(w))
            pb = (s[HALO:, :] / cnt - ub[:, lanes]).astype(BF16)
            z = _mm(pb, w_ref[g].astype(BF16))
            out_ref[:, lanes] = (z * sc_ref[:, lanes]).astype(BF16)
            pooled_ref[:, lanes] = pb

    return pl.pallas_call(
        body, name="pool_fwd", grid=(T // tm,),
        in_specs=[_rows(tm, POOL_W), _full((4, 128, 128)), _full((1, POOL_W))],
        out_specs=[_rows(tm, POOL_W), _rows(tm, POOL_W)],
        out_shape=[_sds((T, POOL_W), BF16), _sds((T, POOL_W), BF16)],
        scratch_shapes=[pltpu.VMEM((HALO, POOL_W), F32)],
        compiler_params=_cp("arbitrary"),
    )(u, pool_w, pool_scale)


FF_CHUNKS = ((0, 768), (768, 1536), (1536, 2304), (2304, D_FF))


def _out_ffn_fwd(attn, pool, x, w_out, mod8, g8, wg, wu, wd, target=None):
    T = x.shape[0]
    tm = min(T, 256)
    last = target is not None

    def body(*refs):
        a_ref, p_ref, xin_ref, wo_ref, mod_ref, g_ref, wg_ref, wu_ref, wd_ref = refs[:9]
        t_ref = refs[9] if last else None
        mix_ref, x1_ref, h_ref, act_ref, ga_ref, gb_ref, f_ref, x2_ref = refs[9 + last:17 + last]
        mix = _mm(a_ref[...], wo_ref[0:ATTN_W, :]) + _mm(p_ref[...], wo_ref[ATTN_W:, :])
        mix_ref[...] = mix
        xf = xin_ref[...] + mod_ref[2:3, :] * (mix * _rstd(mix) * g_ref[1:2, :])
        x1_ref[...] = xf
        h = (xf * _rstd(xf) * g_ref[2:3, :]) * (1.0 + mod_ref[4:5, :]) + mod_ref[3:4, :]
        hb = h.astype(BF16)
        h_ref[...] = hb
        f = jnp.zeros((tm, D_MODEL), F32)
        for lo, hi in FF_CHUNKS:
            a = _nt(hb, wg_ref[lo:hi, :])
            b = _nt(hb, wu_ref[lo:hi, :])
            sig = jax.nn.sigmoid(a)
            sl = a * sig
            act = (sl * b).astype(BF16)
            act_ref[:, lo:hi] = act
            ga_ref[:, lo:hi] = (b * (sig * (1.0 + a * (1.0 - sig)))).astype(BF16)
            gb_ref[:, lo:hi] = sl.astype(BF16)
            f = f + _mm(act, wd_ref[lo:hi, :])
        f_ref[...] = f
        x2 = xf + mod_ref[5:6, :] * (f * _rstd(f) * g_ref[3:4, :])
        if not last:
            x2_ref[...] = x2
        else:
            loss_ref = refs[18]

            @pl.when(pl.program_id(0) == 0)
            def _():
                loss_ref[...] = jnp.zeros_like(loss_ref)

            e = x2 - t_ref[...]
            x2_ref[...] = e * (1.0 / D_MODEL)
            loss_ref[...] += 0.5 * jnp.sum(jnp.mean(e * e, axis=-1, keepdims=True), axis=0, keepdims=True)

    act_shape = _sds((T, D_FF), BF16)
    wide = _sds((T, D_MODEL), F32)
    weights = [_resident((D_FF, D_MODEL))] * 3
    return pl.pallas_call(
        body, name="out_ffn_fwd_loss" if last else "out_ffn_fwd", grid=(T // tm,),
        in_specs=[_rows(tm, ATTN_W), _rows(tm, POOL_W), _rows(tm, D_MODEL), _resident((D_MODEL, D_MODEL)),
                  _full((8, D_MODEL)), _full((8, D_MODEL)), *weights]
        + ([_rows(tm, D_MODEL)] if last else []),
        out_specs=[_rows(tm, D_MODEL), _rows(tm, D_MODEL), _rows(tm, D_MODEL), _rows(tm, D_FF), _rows(tm, D_FF),
                   _rows(tm, D_FF), _rows(tm, D_MODEL), _rows(tm, D_MODEL)] + ([_full((8, 128))] if last else []),
        out_shape=[wide, wide, _sds((T, D_MODEL), BF16), act_shape, act_shape, act_shape, wide, wide]
        + ([_sds((8, 128), F32)] if last else []),
        compiler_params=_cp("arbitrary" if last else "parallel", vmem=FFN_VMEM_LIMIT),
    )(attn, pool, x, w_out, mod8, g8, wg, wu, wd, *([target] if last else []))


def _ffn_bwd(dx2, f, ga, gb, x1, mod8, g8, wg, wu, wd):
    T = dx2.shape[0]
    tm = min(T, 256)

    def body(dx_ref, f_ref, ga_ref, gb_ref, x_ref, mod_ref, g_ref, wg_ref, wu_ref, wd_ref,
             dx1_ref, df_ref, da_ref, db_ref, red_ref):
        @pl.when(pl.program_id(0) == 0)
        def _():
            red_ref[...] = jnp.zeros_like(red_ref)

        dx = dx_ref[...]
        fv = f_ref[...]
        rstd = _rstd(fv)
        fhat = fv * rstd
        gpost = g_ref[3:4, :]
        gate = mod_ref[5:6, :]
        df, s_post = _norm_gain_bwd(dx, fhat, rstd, gate * gpost)
        red_ref[0:1, :] += gpost * s_post
        red_ref[1:2, :] += gate * s_post
        dfb = df.astype(BF16)
        df_ref[...] = dfb
        dh = jnp.zeros((tm, D_MODEL), F32)
        for lo, hi in FF_CHUNKS:
            dact = _nt(dfb, wd_ref[lo:hi, :])
            da = (dact * ga_ref[:, lo:hi].astype(F32)).astype(BF16)
            db = (dact * gb_ref[:, lo:hi].astype(F32)).astype(BF16)
            da_ref[:, lo:hi] = da
            db_ref[:, lo:hi] = db
            dh = dh + _mm(da, wg_ref[lo:hi, :]) + _mm(db, wu_ref[lo:hi, :])
        xf = x_ref[...]
        rstd1 = _rstd(xf)
        xhat = xf * rstd1
        gpre = g_ref[2:3, :]
        scale1 = 1.0 + mod_ref[4:5, :]
        dxn, s_pre = _norm_gain_bwd(dh, xhat, rstd1, scale1 * gpre)
        red_ref[2:3, :] += _colsum(dh)
        red_ref[3:4, :] += gpre * s_pre
        red_ref[4:5, :] += scale1 * s_pre
        dx1_ref[...] = dx + dxn

    act_shape = _sds((T, D_FF), BF16)
    return pl.pallas_call(
        body, name="ffn_bwd", grid=(T // tm,),
        in_specs=[_rows(tm, D_MODEL), _rows(tm, D_MODEL), _rows(tm, D_FF), _rows(tm, D_FF), _rows(tm, D_MODEL),
                  _full((8, D_MODEL)), _full((8, D_MODEL)),
                  _resident((D_FF, D_MODEL)), _resident((D_FF, D_MODEL)), _resident((D_FF, D_MODEL))],
        out_specs=[_rows(tm, D_MODEL), _rows(tm, D_MODEL), _rows(tm, D_FF), _rows(tm, D_FF), _full((8, D_MODEL))],
        out_shape=[_sds((T, D_MODEL), F32), _sds((T, D_MODEL), BF16), act_shape, act_shape, _sds((8, D_MODEL), F32)],
        compiler_params=_cp("arbitrary"),
    )(dx2, f, ga, gb, x1, mod8, g8, wg, wu, wd)


def _wgrad(a, b, name, after=None):
    T, K = a.shape
    N = b.shape[1]
    tt = min(T, WGRAD_TOKENS)
    tk = next(c for c in (1408, 640, 512, 256, 128) if K % c == 0)

    def body(a_ref, b_ref, *rest):
        o_ref = rest[-1]

        @pl.when(pl.program_id(1) == 0)
        def _():
            o_ref[...] = jnp.zeros_like(o_ref)

        o_ref[...] += _tn(a_ref[...], b_ref[...])

    extra = [] if after is None else [after]
    return pl.pallas_call(
        body, name=name, grid=(K // tk, T // tt),
        in_specs=[pl.BlockSpec((tt, tk), lambda i, t: (t, i)), pl.BlockSpec((tt, N), lambda i, t: (t, 0))]
        + [pl.BlockSpec(memory_space=pl.ANY)] * len(extra),
        out_specs=pl.BlockSpec((tk, N), lambda i, t: (i, 0)),
        out_shape=_sds((K, N), F32),
        compiler_params=_cp("parallel", "arbitrary"),
    )(a, b, *extra)


def _mix_bwd(dx1, mix, mod8, g8, w_out):
    T = dx1.shape[0]
    tm = min(T, 512)

    def body(dx_ref, mix_ref, mod_ref, g_ref, w_ref, dmix_ref, da_ref, dp_ref, red_ref):
        @pl.when(pl.program_id(0) == 0)
        def _():
            red_ref[...] = jnp.zeros_like(red_ref)

        dx = dx_ref[...]
        mv = mix_ref[...]
        rstd = _rstd(mv)
        mhat = mv * rstd
        gpost = g_ref[1:2, :]
        gate = mod_ref[2:3, :]
        dm, s_post = _norm_gain_bwd(dx, mhat, rstd, gate * gpost)
        red_ref[0:1, :] += gpost * s_post
        red_ref[1:2, :] += gate * s_post
        dmb = dm.astype(BF16)
        dmix_ref[...] = dmb
        da_ref[...] = _nt(dmb, w_ref[0:ATTN_W, :]).astype(BF16)
        dp_ref[...] = _nt(dmb, w_ref[ATTN_W:, :]).astype(BF16)

    return pl.pallas_call(
        body, name="mix_bwd", grid=(T // tm,),
        in_specs=[_rows(tm, D_MODEL), _rows(tm, D_MODEL), _full((8, D_MODEL)), _full((8, D_MODEL)),
                  _resident((D_MODEL, D_MODEL))],
        out_specs=[_rows(tm, D_MODEL), _rows(tm, ATTN_W), _rows(tm, POOL_W), _full((8, D_MODEL))],
        out_shape=[_sds((T, D_MODEL), BF16), _sds((T, ATTN_W), BF16), _sds((T, POOL_W), BF16),
                   _sds((8, D_MODEL), F32)],
        compiler_params=_cp("arbitrary"),
    )(dx1, mix, mod8, g8, w_out)


def _attn_bwd(q, kd, vd, lse, dattn, sink_b):
    T = q.shape[0]
    nb = T // BLK

    def body(q_ref, do_ref, lse_ref, kp_ref, kc_ref, vp_ref, vc_ref, sk_ref,
             dq_ref, dk_ref, dv_ref, dsk_ref, carry_k, carry_v):
        n = pl.program_id(0)

        @pl.when(n == 0)
        def _():
            carry_k[...] = jnp.zeros_like(carry_k)
            carry_v[...] = jnp.zeros_like(carry_v)
            dsk_ref[...] = jnp.zeros_like(dsk_ref)

        @pl.when(n < nb)
        def _():
            valid = _band_mask(n)
            for j in range(N_HEADS // GROUP):
                lanes = slice(j * 128, (j + 1) * 128)
                kcat = jnp.concatenate([kp_ref[:, lanes], kc_ref[:, lanes]], axis=0)
                vcat = jnp.concatenate([vp_ref[:, lanes], vc_ref[:, lanes]], axis=0)
                qs = _stack_heads(q_ref, j)
                dos = _stack_heads(do_ref, j)
                lse = _head_row(lse_ref, j)
                p = jnp.exp(jnp.where(valid, _nt(kcat, qs), NEG_INF) - lse)
                dp = _nt(vcat, dos)
                delta = jnp.sum(p * dp, axis=0, keepdims=True)
                ds = (p * (dp - delta)).astype(BF16)
                sink_term = jnp.exp(_head_row(sk_ref, j) - lse) * delta
                for r in range(GROUP):
                    h = GROUP * j + r
                    dsk_ref[h:h + 1, :] += -jnp.sum(sink_term[:, r * 128:(r + 1) * 128], axis=1, keepdims=True)
                dq_ref[:, 2 * j * 128:(2 * j + 2) * 128] = jnp.concatenate(_unstack_heads(_tn(ds, kcat)), axis=1)
                dk = _mm(ds, qs)
                dv = _mm(p.astype(BF16), dos)
                dk_ref[:, lanes] = carry_k[:, lanes] + dk[0:BLK]
                dv_ref[:, lanes] = carry_v[:, lanes] + dv[0:BLK]
                carry_k[:, lanes] = dk[BLK:]
                carry_v[:, lanes] = dv[BLK:]

        @pl.when(n == nb)
        def _():
            dk_ref[...] = carry_k[...]
            dv_ref[...] = carry_v[...]

    cur = lambda n: (jnp.minimum(n, nb - 1), 0)
    prev = lambda n: (jnp.maximum(n - 1, 0), 0)
    return pl.pallas_call(
        body, name="attn_bwd", grid=(nb + 1,),
        in_specs=[pl.BlockSpec((BLK, ATTN_W), cur), pl.BlockSpec((BLK, ATTN_W), cur), pl.BlockSpec((N_HEADS, 128), cur),
                  pl.BlockSpec((BLK, KVD_W), prev), pl.BlockSpec((BLK, KVD_W), cur),
                  pl.BlockSpec((BLK, KVD_W), prev), pl.BlockSpec((BLK, KVD_W), cur),
                  _full((8, 128))],
        out_specs=[pl.BlockSpec((BLK, ATTN_W), cur), pl.BlockSpec((BLK, KVD_W), prev),
                   pl.BlockSpec((BLK, KVD_W), prev), _full((8, 128))],
        out_shape=[_sds((T, ATTN_W), F32), _sds((T, KVD_W), F32), _sds((T, KVD_W), F32), _sds((8, 128), F32)],
        scratch_shapes=[pltpu.VMEM((BLK, KVD_W), F32), pltpu.VMEM((BLK, KVD_W), F32)],
        compiler_params=_cp("arbitrary"),
    )(q, dattn, lse, kd, kd, vd, vd, sink_b)


def _pool_bwd(dpool, pooled, pool_w, pool_scale):
    T = dpool.shape[0]
    tm = min(T, 512)
    nbk = T // tm
    ext_rows = tm + HALO

    def body(dp_ref, pl_ref, w_ref, sc_ref, du_ref, dw_ref, dsc_ref, halo):
        i = pl.program_id(0)

        @pl.when(i == 0)
        def _():
            halo[...] = jnp.zeros_like(halo)
            dw_ref[...] = jnp.zeros_like(dw_ref)
            dsc_ref[...] = jnp.zeros_like(dsc_ref)

        blk = nbk - 1 - i
        tpos = (blk * tm + lax.broadcasted_iota(jnp.int32, (tm, 1), 0)).astype(F32)
        for g, w in enumerate(POOL_WINDOWS):
            lanes = slice(g * 128, (g + 1) * 128)
            dp = dp_ref[:, lanes].astype(F32)
            pb = pl_ref[:, lanes]
            wg = w_ref[g].astype(BF16)
            z = _mm(pb, wg)
            dsc_ref[0:1, lanes] += _colsum(dp * z)
            dz = (dp * sc_ref[:, lanes]).astype(BF16)
            dw_ref[g] += _tn(pb, dz)
            dpl = _nt(dz, wg)
            e = dpl / jnp.minimum(tpos + 1.0, float(w))
            s = jnp.concatenate([e, halo[:, lanes]], axis=0)
            halo[:, lanes] = e[0:HALO, :]
            sh = 1
            while sh < w:
                s = s + pltpu.roll(s, ext_rows - sh, 0)
                sh *= 2
            du_ref[:, lanes] = s[0:tm, :] - dpl

    rev = lambda i: (nbk - 1 - i, 0)
    return pl.pallas_call(
        body, name="pool_bwd", grid=(nbk,),
        in_specs=[pl.BlockSpec((tm, POOL_W), rev), pl.BlockSpec((tm, POOL_W), rev),
                  _full((4, 128, 128)), _full((1, POOL_W))],
        out_specs=[pl.BlockSpec((tm, POOL_W), rev), _full((4, 128, 128)), _full((8, POOL_W))],
        out_shape=[_sds((T, POOL_W), F32), _sds((4, 128, 128), F32), _sds((8, POOL_W), F32)],
        scratch_shapes=[pltpu.VMEM((HALO, POOL_W), F32)],
        compiler_params=_cp("arbitrary"),
    )(dpool, pooled, pool_w, pool_scale)


def _in_bwd(dq, dk, dv, du, rc, rs1, rs2, x, dx1, mod8, g8, w_in):
    T = x.shape[0]
    tm = min(T, 512)

    def body(dq_ref, dk_ref, dv_ref, du_ref, c_ref, s1_ref, s2_ref, x_ref, dx1_ref, mod_ref, g_ref, w_ref,
             dx_ref, dproj_ref, red_ref, dbin_ref):
        @pl.when(pl.program_id(0) == 0)
        def _():
            red_ref[...] = jnp.zeros_like(red_ref)
            dbin_ref[...] = jnp.zeros_like(dbin_ref)

        c = c_ref[...]
        s1 = s1_ref[...]
        s2 = s2_ref[...]
        dqp = _rot_bwd(dq_ref[...] * (HEAD ** -0.5), jnp.tile(c, (1, 4)), jnp.tile(s1, (1, 4)), jnp.tile(s2, (1, 4)))
        dkp = _rot_bwd(_fold_dup(dk_ref[...]), c, s1, s2)
        pieces = ((0, ATTN_W, dqp), (ATTN_W, ATTN_W + KV_W, dkp),
                  (ATTN_W + KV_W, ATTN_W + 2 * KV_W, _fold_dup(dv_ref[...])), (ATTN_W + 2 * KV_W, IN_W, du_ref[...]))
        dh = jnp.zeros((tm, D_MODEL), F32)
        for lo, hi, val in pieces:
            dbin_ref[0:1, lo:hi] += _colsum(val)
            vb = val.astype(BF16)
            dproj_ref[:, lo:hi] = vb
            dh = dh + _mm(vb, w_ref[lo:hi, :])
        xf = x_ref[...]
        rstd = _rstd(xf)
        xhat = xf * rstd
        gpre = g_ref[0:1, :]
        scale1 = 1.0 + mod_ref[1:2, :]
        dxn, s_pre = _norm_gain_bwd(dh, xhat, rstd, scale1 * gpre)
        red_ref[0:1, :] += _colsum(dh)
        red_ref[1:2, :] += gpre * s_pre
        red_ref[2:3, :] += scale1 * s_pre
        dx_ref[...] = dx1_ref[...] + dxn

    return pl.pallas_call(
        body, name="in_bwd", grid=(T // tm,),
        in_specs=[_rows(tm, ATTN_W), _rows(tm, KVD_W), _rows(tm, KVD_W), _rows(tm, POOL_W),
                  _rows(tm, 128), _rows(tm, 128), _rows(tm, 128), _rows(tm, D_MODEL), _rows(tm, D_MODEL),
                  _full((8, D_MODEL)), _full((8, D_MODEL)), _resident((IN_W, D_MODEL))],
        out_specs=[_rows(tm, D_MODEL), _rows(tm, IN_W), _full((8, D_MODEL)), _full((8, IN_W))],
        out_shape=[_sds((T, D_MODEL), F32), _sds((T, IN_W), BF16), _sds((8, D_MODEL), F32), _sds((8, IN_W), F32)],
        compiler_params=_cp("arbitrary"),
    )(dq, dk, dv, du, rc, rs1, rs2, x, dx1, mod8, g8, w_in)


def _mod_fwd(c_all, ada_w, ada_b_sh):
    tn = 512

    def body(c_ref, w_ref, b_ref, o_ref):
        cv = c_ref[...]
        ca = (cv * jax.nn.sigmoid(cv)).astype(BF16)
        o_ref[...] = _mm(ca, w_ref[...].astype(BF16)) + b_ref[...]

    return pl.pallas_call(
        body, name="mod_fwd", grid=(2, ADA_SH // tn),
        in_specs=[_full((8, D_MODEL)), pl.BlockSpec((None, D_MODEL, tn), lambda l, j: (l, 0, j)),
                  pl.BlockSpec((None, 1, tn), lambda l, j: (l, 0, j))],
        out_specs=pl.BlockSpec((None, 8, tn), lambda l, j: (l, 0, j)),
        out_shape=_sds((2, 8, ADA_SH), F32),
        compiler_params=_cp("parallel", "parallel"),
    )(c_all, ada_w, ada_b_sh)


def _ada_wgrad(c_all_t, dmod_sh):
    tn = 512

    def body(c_ref, d_ref, o_ref):
        cv = c_ref[...]
        ca = cv * jax.nn.sigmoid(cv)
        o_ref[...] = jnp.dot(ca, d_ref[...], preferred_element_type=F32, precision=lax.Precision.HIGHEST)

    return pl.pallas_call(
        body, name="ada_wgrad", grid=(2, ADA_SH // tn),
        in_specs=[_full((D_MODEL, 8)), pl.BlockSpec((None, 8, tn), lambda l, j: (l, 0, j))],
        out_specs=pl.BlockSpec((None, D_MODEL, tn), lambda l, j: (l, 0, j)),
        out_shape=_sds((2, D_MODEL, ADA_SH), F32),
        compiler_params=_cp("parallel", "parallel"),
    )(c_all_t, dmod_sh)


def _sum_devices(g):
    R = g.shape[1]

    def body(g_ref, o_ref):
        acc = g_ref[0]
        for d in range(1, N_DEV):
            acc = acc + g_ref[d]
        o_ref[...] = acc

    return pl.pallas_call(
        body, name="sum_devices", grid=(1,),
        in_specs=[_full((N_DEV, R, 128))], out_specs=_full((R, 128)), out_shape=_sds((R, 128), F32),
        compiler_params=_cp("arbitrary"),
    )(g)


def _adamw(w, g, m, v, name):
    R, C = w.shape
    tr = R
    for cand in (256, 128, 64, 32, 16, 8):
        if R % cand == 0 and cand * C * 4 <= 2 * 1024 * 1024:
            tr = cand
            break

    def body(w_ref, g_ref, m_ref, v_ref, d_ref, nm_ref, nv_ref):
        gv = g_ref[...]
        mn = ADAM_B1 * m_ref[...] + (1.0 - ADAM_B1) * gv
        vn = ADAM_B2 * v_ref[...] + (1.0 - ADAM_B2) * (gv * gv)
        m_hat = mn / (1.0 - ADAM_B1 ** ADAM_STEP)
        v_hat = vn / (1.0 - ADAM_B2 ** ADAM_STEP)
        d_ref[...] = -ADAM_LR * (m_hat / (jnp.sqrt(v_hat) + ADAM_EPS) + ADAM_WD * w_ref[...])
        nm_ref[...] = mn
        nv_ref[...] = vn

    spec = pl.BlockSpec((tr, C), lambda i: (i, 0))
    out = _sds((R, C), F32)
    return pl.pallas_call(
        body, name=name, grid=(R // tr,),
        in_specs=[spec] * 4, out_specs=[spec] * 3, out_shape=[out] * 3,
        compiler_params=_cp("parallel"),
    )(w, g, m, v)


def _adamw_nd(w, g, m, v, name):
    shape = w.shape
    if w.ndim == 2 and shape[1] < 128:
        view = (1, shape[0] * shape[1])
    else:
        view = (-1, shape[-1])
    outs = _adamw(*[t.reshape(view) for t in (w, g, m, v)], name=name)
    return [o.reshape(shape) for o in outs]


def _coords():
    return lax.axis_index("x"), lax.axis_index("y"), lax.axis_index("c")


def _other_chips(x, y):
    return [(1 - x, y), (x, 1 - y), (1 - x, 1 - y)]


def _allgather8(blk, name):
    m_per, n = blk.shape

    def body(x_ref, out_ref, send_sems, recv_sems, local_sem):
        x, y, c = _coords()
        me, sibling = (x, y, c), (x, y, 1 - c)
        chips = _other_chips(x, y)

        def rows(px, py, pc):
            return out_ref.at[pl.ds((4 * px + 2 * py + pc) * m_per, m_per), :]

        def copy(k, block, to, src=None):
            return pltpu.make_async_remote_copy(
                src_ref=rows(*block) if src is None else src, dst_ref=rows(*block),
                send_sem=send_sems.at[k], recv_sem=recv_sems.at[k], device_id=to, device_id_type=MESH)

        mine = pltpu.make_async_copy(x_ref, rows(*me), local_sem)
        mine.start()
        first = [copy(0, me, sibling, src=x_ref)]
        first += [copy(1 + j, me, (*chip, c), src=x_ref) for j, chip in enumerate(chips)]
        for cp in first:
            cp.start()
        passed = [copy(4 + j, (*chip, c), sibling) for j, chip in enumerate(chips)]
        for j, chip in enumerate(chips):
            copy(1 + j, (*chip, c), me).wait_recv()
            passed[j].start()
        copy(0, sibling, me).wait_recv()
        for j, chip in enumerate(chips):
            copy(4 + j, (*chip, 1 - c), me).wait_recv()
        for cp in first + passed:
            cp.wait_send()
        mine.wait()

    return pl.pallas_call(
        body, name=name,
        out_shape=_sds((N_DEV * m_per, n), blk.dtype),
        in_specs=[pl.BlockSpec(memory_space=pltpu.VMEM)],
        out_specs=pl.BlockSpec(memory_space=pltpu.VMEM),
        scratch_shapes=[pltpu.SemaphoreType.DMA((7,)), pltpu.SemaphoreType.DMA((7,)), pltpu.SemaphoreType.DMA],
        compiler_params=pltpu.CompilerParams(vmem_limit_bytes=VMEM_LIMIT),
    )(blk)


def _row_tile(r, n):
    for cand in range(r, 15, -16):
        if r % cand == 0 and cand % 16 == 0 and cand * n * 4 <= 2 * 1024 * 1024:
            return cand
    return r


def _cast_slot(w, chip, name):
    r, n = w.shape
    tr = _row_tile(r, n)

    def body(chip_ref, w_ref, o_ref):
        o_ref[...] = w_ref[...].astype(BF16)

    grid_spec = pltpu.PrefetchScalarGridSpec(
        num_scalar_prefetch=1, grid=(r // tr,),
        in_specs=[pl.BlockSpec((tr, n), lambda i, ch: (i, 0))],
        out_specs=pl.BlockSpec((None, tr, n), lambda i, ch: (ch[0], i, 0)))
    return pl.pallas_call(
        body, name=name, grid_spec=grid_spec, out_shape=_sds((N_SHARD, r, n), BF16),
        compiler_params=_cp("arbitrary"),
    )(chip, w)


def _allgather_weights(bufs, name):
    nt = len(bufs)
    hom = [pl.BlockSpec(memory_space=pl.ANY)] * nt

    def body(*refs):
        outs = refs[nt:2 * nt]
        send_sems, recv_sems = refs[2 * nt:]
        x, y, c = _coords()
        sibling = (x, y, 1 - c)
        chips = _other_chips(x, y)

        def copy(t, k, block_chip, hc, to):
            r = outs[t].shape[1] // 2
            blk = outs[t].at[2 * block_chip[0] + block_chip[1], pl.ds(hc * r, r)]
            return pltpu.make_async_remote_copy(
                src_ref=blk, dst_ref=blk,
                send_sem=send_sems.at[t, k], recv_sem=recv_sems.at[t, k], device_id=to, device_id_type=MESH)

        started = []
        for t in range(nt):
            for j, chip in enumerate(chips):
                cp = copy(t, j, (x, y), c, (*chip, c))
                cp.start()
                started.append(cp)
        for t in range(nt):
            for j, chip in enumerate(chips):
                copy(t, j, chip, c, sibling).wait_recv()
                fw = copy(t, 3 + j, chip, c, sibling)
                fw.start()
                started.append(fw)
        for t in range(nt):
            for j, chip in enumerate(chips):
                copy(t, 3 + j, chip, 1 - c, sibling).wait_recv()
        for cp in started:
            cp.wait_send()

    return pl.pallas_call(
        body, name=name,
        out_shape=[_sds(b.shape, b.dtype) for b in bufs],
        in_specs=hom, out_specs=hom,
        input_output_aliases={t: t for t in range(nt)},
        scratch_shapes=[pltpu.SemaphoreType.DMA((nt, 6)), pltpu.SemaphoreType.DMA((nt, 6))],
    )(*bufs)


def _join_halves(tots, name):
    nt = len(tots)
    hom = [pl.BlockSpec(memory_space=pl.ANY)] * nt

    def body(*refs):
        outs = refs[nt:2 * nt]
        send_sems, recv_sems = refs[2 * nt:]
        x, y, c = _coords()
        sibling = (x, y, 1 - c)
        cps = []
        for t in range(nt):
            cp = pltpu.make_async_remote_copy(
                src_ref=outs[t].at[c], dst_ref=outs[t].at[c],
                send_sem=send_sems.at[t], recv_sem=recv_sems.at[t], device_id=sibling, device_id_type=MESH)
            cp.start()
            cps.append(cp)
        for t in range(nt):
            pltpu.make_async_remote_copy(
                src_ref=outs[t].at[c], dst_ref=outs[t].at[1 - c],
                send_sem=send_sems.at[t], recv_sem=recv_sems.at[t], device_id=sibling, device_id_type=MESH).wait_recv()
        for cp in cps:
            cp.wait_send()

    return pl.pallas_call(
        body, name=name,
        out_shape=[_sds(t.shape, t.dtype) for t in tots],
        in_specs=hom, out_specs=hom,
        input_output_aliases={t: t for t in range(nt)},
        scratch_shapes=[pltpu.SemaphoreType.DMA((nt,)), pltpu.SemaphoreType.DMA((nt,))],
    )(*tots)


def _pair_sum(g, recv, core, chip, name):
    _, _, r, n = g.shape
    tr = _row_tile(r, n)

    def body(core_ref, chip_ref, g_ref, r_ref, sb_ref, own_ref):
        tot = g_ref[...] + r_ref[...]
        sb_ref[...] = tot.astype(BF16)

        @pl.when(pl.program_id(1) == chip_ref[0])
        def _():
            own_ref[...] = tot

    grid_spec = pltpu.PrefetchScalarGridSpec(
        num_scalar_prefetch=2, grid=(r // tr, N_SHARD),
        in_specs=[pl.BlockSpec((None, None, tr, n), lambda i, s, co, ch: (s, co[0], i, 0)),
                  pl.BlockSpec((None, tr, n), lambda i, s, co, ch: (s, i, 0))],
        out_specs=[pl.BlockSpec((None, tr, n), lambda i, s, co, ch: (s, i, 0)),
                   pl.BlockSpec((tr, n), lambda i, s, co, ch: (i, 0))])
    return pl.pallas_call(
        body, name=name, grid_spec=grid_spec,
        out_shape=[_sds((N_SHARD, r, n), BF16), _sds((r, n), F32)],
        compiler_params=_cp("arbitrary", "arbitrary"),
    )(core, chip, g, recv)


def _chip_sum(own, recv, core, name):
    r, n = own.shape
    tr = _row_tile(r, n)

    def body(core_ref, o_ref, r_ref, t_ref):
        acc = o_ref[...]
        for j in range(3):
            acc = acc + r_ref[j].astype(F32)
        t_ref[...] = acc

    grid_spec = pltpu.PrefetchScalarGridSpec(
        num_scalar_prefetch=1, grid=(r // tr,),
        in_specs=[pl.BlockSpec((tr, n), lambda i, co: (i, 0)), pl.BlockSpec((3, tr, n), lambda i, co: (0, i, 0))],
        out_specs=pl.BlockSpec((None, tr, n), lambda i, co: (co[0], i, 0)))
    return pl.pallas_call(
        body, name=name, grid_spec=grid_spec, out_shape=_sds((2, r, n), F32),
        compiler_params=_cp("arbitrary"),
    )(core, own, recv)


_HBM = pl.BlockSpec(memory_space=pltpu.HBM)
_SEM = pl.BlockSpec(memory_space=pltpu.SEMAPHORE)
_EFFECT = pltpu.SideEffectType.DATAFLOW_SIDE_EFFECTING


def _ici_copies(srcs, dsts, send_sems, recv_sems, send_view, recv_view):
    x, y, c = _coords()
    out = []
    if send_view is None:
        for t in range(len(srcs)):
            r = srcs[t].shape[1] // 2
            out.append(pltpu.make_async_remote_copy(
                src_ref=srcs[t].at[:, pl.ds((1 - c) * r, r)], dst_ref=dsts[t],
                send_sem=send_sems.at[3 * t], recv_sem=recv_sems.at[3 * t],
                device_id=(x, y, 1 - c), device_id_type=MESH))
        return out
    for t in range(len(srcs)):
        for j, chip in enumerate(_other_chips(x, y)):
            out.append(pltpu.make_async_remote_copy(
                src_ref=send_view(srcs[t], chip, j, (x, y), c), dst_ref=recv_view(dsts[t], chip, j, (x, y), c),
                send_sem=send_sems.at[3 * t + j], recv_sem=recv_sems.at[3 * t + j],
                device_id=(*chip, c), device_id_type=MESH))
    return out


def _ici_start(srcs, dsts, after, send_view, recv_view, name):
    nt = len(srcs)
    inplace = dsts is None
    nbuf = nt if inplace else 2 * nt

    def body(*refs):
        send_sems, recv_sems = refs[nbuf + 1], refs[nbuf + 2]
        s_out = refs[nbuf + 3:nbuf + 3 + nt]
        d_out = s_out if inplace else refs[nbuf + 3 + nt:nbuf + 3 + 2 * nt]
        token = refs[-1]
        for cp in _ici_copies(s_out, d_out, send_sems, recv_sems, send_view, recv_view):
            cp.start()
        token[...] = jnp.zeros_like(token)

    bufs = list(srcs) + ([] if inplace else list(dsts))
    res = pl.pallas_call(
        body, name=name,
        out_shape=(pltpu.SemaphoreType.DMA((3 * nt,)), pltpu.SemaphoreType.DMA((3 * nt,)),
                   *[pltpu.HBM(b.shape, b.dtype) for b in bufs], _sds((8, 128), F32)),
        in_specs=[_HBM] * nbuf + [pl.BlockSpec(memory_space=pl.ANY)],
        out_specs=(_SEM, _SEM, *[_HBM] * nbuf, pl.BlockSpec(memory_space=pltpu.VMEM)),
        input_output_aliases={i: 2 + i for i in range(nbuf)},
        compiler_params=pltpu.CompilerParams(has_side_effects=_EFFECT),
    )(*[pltpu.with_memory_space_constraint(b, pltpu.HBM) for b in bufs], after)
    send_sems, recv_sems = res[0], res[1]
    s_thru = list(res[2:2 + nt])
    d_thru = s_thru if inplace else list(res[2 + nt:2 + 2 * nt])
    return send_sems, recv_sems, s_thru, d_thru, res[-1]


def _ici_wait(send_sems, recv_sems, srcs, dsts, after, send_view, recv_view, name):
    nt = len(srcs)
    inplace = dsts is None
    nbuf = nt if inplace else 2 * nt

    def body(*refs):
        send_ref, recv_ref = refs[nbuf], refs[nbuf + 1]
        s_out = refs[nbuf + 3:nbuf + 3 + nt]
        d_out = s_out if inplace else refs[nbuf + 3 + nt:nbuf + 3 + 2 * nt]
        for cp in _ici_copies(s_out, d_out, send_ref, recv_ref, send_view, recv_view):
            cp.wait_send()
            cp.wait_recv()

    bufs = list(srcs) + ([] if inplace else list(dsts))
    res = pl.pallas_call(
        body, name=name,
        out_shape=tuple(pltpu.HBM(b.shape, b.dtype) for b in bufs),
        in_specs=[_HBM] * nbuf + [_SEM, _SEM, pl.BlockSpec(memory_space=pl.ANY)],
        out_specs=tuple([_HBM] * nbuf),
        input_output_aliases={i: i for i in range(nbuf)},
        compiler_params=pltpu.CompilerParams(has_side_effects=_EFFECT),
    )(*bufs, send_sems, recv_sems, after)
    return list(res[:nt]) if inplace else (list(res[:nt]), list(res[nt:]))


def _w_half(buf, chip, c):
    r = buf.shape[1] // 2
    return buf.at[2 * chip[0] + chip[1], pl.ds(c * r, r)]


def _ag_send_view(buf, chip, j, me, c):
    return _w_half(buf, me, c)


def _ag_recv_view(buf, chip, j, me, c):
    return _w_half(buf, me, c)


def _rs_send_view(buf, chip, j, me, c):
    return buf.at[2 * chip[0] + chip[1]]


def _rs_recv_view(buf, chip, j, me, c):
    return buf.at[j]


def _ag_forward(bufs, name):
    nt = len(bufs)
    hom = [pl.BlockSpec(memory_space=pl.ANY)] * nt

    def body(*refs):
        outs = refs[nt:2 * nt]
        send_sems, recv_sems = refs[2 * nt:]
        x, y, c = _coords()
        sibling = (x, y, 1 - c)
        chips = _other_chips(x, y)

        def copy(t, j, hc):
            blk = _w_half(outs[t], chips[j], hc)
            return pltpu.make_async_remote_copy(
                src_ref=blk, dst_ref=blk, send_sem=send_sems.at[t, j], recv_sem=recv_sems.at[t, j],
                device_id=sibling, device_id_type=MESH)

        started = [copy(t, j, c) for t in range(nt) for j in range(3)]
        for cp in started:
            cp.start()
        for t in range(nt):
            for j in range(3):
                copy(t, j, 1 - c).wait_recv()
        for cp in started:
            cp.wait_send()

    return pl.pallas_call(
        body, name=name,
        out_shape=[_sds(b.shape, b.dtype) for b in bufs],
        in_specs=hom, out_specs=hom,
        input_output_aliases={t: t for t in range(nt)},
        scratch_shapes=[pltpu.SemaphoreType.DMA((nt, 3)), pltpu.SemaphoreType.DMA((nt, 3))],
    )(*bufs)


def _rs_swap_begin(grads, after, tag):
    land = [lax.empty((N_SHARD, g.shape[1] // 2, g.shape[2]), g.dtype) for g in grads]
    send_sems, recv_sems, s_thru, d_thru, token = _ici_start(grads, land, after, None, None, name="rs_swapgo_" + tag)
    return dict(sems=(send_sems, recv_sems), grads=s_thru, land=d_thru, tag=tag), token


def _rs_scatter_begin(swap, after):
    tag = swap["tag"]
    x, y, c = _coords()
    core = jnp.reshape(c, (1,)).astype(jnp.int32)
    chip = jnp.reshape(2 * x + y, (1,)).astype(jnp.int32)
    grads, recv = _ici_wait(*swap["sems"], swap["grads"], swap["land"], after, None, None, name="rs_swapend_" + tag)
    sums, owns = [], []
    for t, (g, rv) in enumerate(zip(grads, recv)):
        r = g.shape[1] // 2
        sb, own = _pair_sum(g.reshape(N_SHARD, 2, r, g.shape[2]), rv, core, chip, name=f"rs_pair_{tag}_{t}")
        sums.append(sb)
        owns.append(own)
    land = [lax.empty((3,) + s.shape[1:], s.dtype) for s in sums]
    send_sems, recv_sems, s_thru, d_thru, token = _ici_start(
        sums, land, after, _rs_send_view, _rs_recv_view, name="rs_start_" + tag)
    return dict(sems=(send_sems, recv_sems), sums=s_thru, land=d_thru, owns=owns, core=core, tag=tag), token


def _rs_end(state, after):
    tag = state["tag"]
    _, got = _ici_wait(*state["sems"], state["sums"], state["land"], after, _rs_send_view, _rs_recv_view,
                       name="rs_wait_" + tag)
    tots = [_chip_sum(o, gt, state["core"], name=f"rs_chip_{tag}_{t}")
            for t, (o, gt) in enumerate(zip(state["owns"], got))]
    full = _join_halves(tots, name="rs_join_" + tag)
    return [f.reshape(2 * f.shape[1], f.shape[2]) for f in full]


def _rope_lane_table():
    d = jnp.arange(128) % HEAD
    inv_freq = ROPE_THETA ** (-jnp.arange(0, ROT, 2, dtype=F32) / ROT)
    rot = d < ROT
    rows = [jnp.where(rot, inv_freq[d % (ROT // 2)], 0.0), rot.astype(F32),
            (d < ROT // 2).astype(F32), jnp.logical_and(d >= ROT // 2, rot).astype(F32)]
    return jnp.concatenate([jnp.stack(rows), jnp.zeros((4, 128), F32)], axis=0)


def _pad8(rows):
    return jnp.concatenate([rows, jnp.zeros((8 - rows.shape[0], rows.shape[1]), F32)], axis=0)


def kernel(x, c, positions, ada_w, ada_b, w_in, b_in, sinks, pool_w, pool_scale, w_out, w_gate, w_up, w_down, g_pre_mix, g_post_mix, g_pre_ffn, g_post_ffn, loss_target, m_ada_w, m_ada_b, m_w_in, m_b_in, m_sinks, m_pool_w, m_pool_scale, m_w_out, m_w_gate, m_w_up, m_w_down, m_g_pre_mix, m_g_post_mix, m_g_pre_ffn, m_g_post_ffn, v_ada_w, v_ada_b, v_w_in, v_b_in, v_sinks, v_pool_w, v_pool_scale, v_w_out, v_w_gate, v_w_up, v_w_down, v_g_pre_mix, v_g_post_mix, v_g_pre_ffn, v_g_post_ffn):
    T = x.shape[1]
    n_layers = ada_w.shape[0]
    ax, ay, ac = _coords()
    my_dev = 4 * ax + 2 * ay + ac
    my_chip = 2 * ax + ay
    x0 = x.reshape(T, D_MODEL)
    target = loss_target.reshape(T, D_MODEL)

    c_all = _allgather8(c.reshape(8, 128), name="ag_c").reshape(N_DEV, D_MODEL)
    ada_b_sh = lax.dynamic_slice_in_dim(ada_b, my_chip * ADA_SH, ADA_SH, axis=1).reshape(n_layers, 1, ADA_SH)
    mod_part = _mod_fwd(c_all, ada_w, ada_b_sh)
    mod_all = _allgather8(mod_part.reshape(n_layers * 8, ADA_SH), name="ag_mod")
    mod_all = mod_all.reshape(N_DEV, n_layers, 8, ADA_SH)[0::2]
    mod_mine = lax.dynamic_index_in_dim(mod_all, my_dev, axis=2, keepdims=False)
    mod = jnp.transpose(mod_mine, (1, 0, 2)).reshape(n_layers, 6, D_MODEL)

    pos_b = jnp.broadcast_to(positions.reshape(T, 1), (T, 128))
    rc, rs1, rs2 = _rope_tables(pos_b, _rope_lane_table())

    chip1 = jnp.reshape(my_chip, (1,)).astype(jnp.int32)

    def tr(t):
        return jnp.transpose(t, (0, 2, 1))

    w_in_t, w_gate_t, w_up_t = tr(w_in), tr(w_gate), tr(w_up)

    def cast_layer(l):
        return [_cast_slot(w[l], chip1, name=f"cast_{nm}{l}")
                for nm, w in (("w_in", w_in_t), ("w_out", w_out), ("w_gate", w_gate_t), ("w_up", w_up_t),
                              ("w_down", w_down))]

    def as_operands(bufs):
        gin, gout, gg, gu, gd = bufs
        return (gin.reshape(IN_W, D_MODEL), gout.reshape(D_MODEL, D_MODEL), gg.reshape(D_FF, D_MODEL),
                gu.reshape(D_FF, D_MODEL), gd.reshape(D_FF, D_MODEL))

    bufs0 = cast_layer(0)
    win0 = _allgather_weights(bufs0[:1], name="ag_w0_in")
    rest_send, rest_recv, rest_bufs, _, ag_token = _ici_start(
        bufs0[1:], None, win0[0], _ag_send_view, _ag_recv_view, name="ag_start_0")
    weights = [None] * n_layers

    saved = []
    xl = x0
    for l in range(n_layers):
        mod8 = _pad8(mod[l])
        if l + 1 < n_layers:
            ag_send, ag_recv, ag_bufs, _, ag_token = _ici_start(
                cast_layer(l + 1), None, ag_token, _ag_send_view, _ag_recv_view, name=f"ag_start_{l + 1}")
        if l == 0 or l + 1 < n_layers:
            mod8 = mod8 + ag_token[0, 0]
        g8 = _pad8(jnp.stack([g_pre_mix[l], g_post_mix[l], g_pre_ffn[l], g_post_ffn[l]]))
        sink_b = jnp.broadcast_to(sinks[l][:, None], (N_HEADS, 128))
        psc = pool_scale[l].reshape(1, POOL_W)
        win = win0[0].reshape(IN_W, D_MODEL) if l == 0 else weights[l][0]
        h, q, k, v, u = _fwd_in(xl, mod8, g8, win, b_in[l].reshape(1, IN_W), rc, rs1, rs2)
        attn, lse = _attn_fwd(q, k, v, sink_b)
        pool, pooled = _pool_fwd(u, pool_w[l], psc)
        if l == 0:
            arrived = _ici_wait(rest_send, rest_recv, rest_bufs, None, pool, _ag_send_view, _ag_recv_view,
                                name="ag_wait_0")
            weights[0] = as_operands(win0 + _ag_forward(arrived, name="ag_fwd_0"))
        win, wout, wg, wu, wd = weights[l]
        if l + 1 < n_layers:
            mix, x1, h2, act, ga, gb, f, x2 = _out_ffn_fwd(attn, pool, xl, wout, mod8, g8, wg, wu, wd)
        else:
            mix, x1, h2, act, ga, gb, f, x2, loss_tile = _out_ffn_fwd(attn, pool, xl, wout, mod8, g8, wg, wu, wd,
                                                                      target=target)
        saved.append(dict(x=xl, h=h, q=q, k=k, v=v, lse=lse, attn=attn, pool=pool, pooled=pooled, mix=mix,
                          x1=x1, h2=h2, act=act, ga=ga, gb=gb, f=f, mod8=mod8, g8=g8, sink_b=sink_b, psc=psc))
        xl = x2
        if l + 1 < n_layers:
            arrived = _ici_wait(ag_send, ag_recv, ag_bufs, None, x2, _ag_send_view, _ag_recv_view,
                                name=f"ag_wait_{l + 1}")
            weights[l + 1] = as_operands(_ag_forward(arrived, name=f"ag_fwd_{l + 1}"))

    dy = xl
    loss = lax.psum(loss_tile[0, 0], ("x", "y", "c"))

    small = [None] * n_layers
    dmod_rows = [None] * n_layers
    reduced = [dict() for _ in range(n_layers)]
    att_swap = None
    dx = dy
    for l in reversed(range(n_layers)):
        s = saved[l]
        win, wout, wg, wu, wd = weights[l]
        if att_swap is not None:
            s = dict(s, mod8=s["mod8"] + att_swap[1][0, 0])
        dx1, df, da, db, red_f = _ffn_bwd(dx, s["f"], s["ga"], s["gb"], s["x1"], s["mod8"], s["g8"], wg, wu, wd)
        token = None
        if att_swap is not None:
            att_scatter = _rs_scatter_begin(att_swap[0], dx1)
            token = att_scatter[1]
        ffn_shards = (N_SHARD, FF_SH, D_MODEL)
        g_wd = _wgrad(s["act"], df, name="wgrad_down", after=token).reshape(ffn_shards)
        g_wg = _wgrad(da, s["h2"], name="wgrad_gate").reshape(ffn_shards)
        g_wu = _wgrad(db, s["h2"], name="wgrad_up").reshape(ffn_shards)
        ffn_swap = _rs_swap_begin([g_wg, g_wu, g_wd], dx1, tag=f"{l}f")
        if att_swap is not None:
            got = _rs_end(att_scatter[0], ffn_swap[1])
            reduced[l + 1].update(w_in=got[0], w_out=got[1])
        s = dict(s, mod8=s["mod8"] + ffn_swap[1][0, 0])
        dmix, dattn, dpool, red_c = _mix_bwd(dx1, s["mix"], s["mod8"], s["g8"], wout)
        g_wout = jnp.concatenate([_wgrad(s["attn"], dmix, name="wgrad_out_a"),
                                  _wgrad(s["pool"], dmix, name="wgrad_out_p")], axis=0)
        ffn_scatter = _rs_scatter_begin(ffn_swap[0], dattn)
        dq, dk, dv, dsink = _attn_bwd(s["q"], s["k"], s["v"], s["lse"], dattn, s["sink_b"] + ffn_scatter[1][0:1, :])
        du, g_poolw, dpsc = _pool_bwd(dpool, s["pooled"], pool_w[l], s["psc"])
        dx, dproj, red_d, dbin = _in_bwd(dq, dk, dv, du, rc, rs1, rs2, s["x"], dx1, s["mod8"], s["g8"], win)
        g_win = _wgrad(dproj, s["h"], name="wgrad_in")
        g_win_sh = g_win.reshape(N_SHARD, IN_SH, D_MODEL)
        got = _rs_end(ffn_scatter[0], dproj)
        reduced[l].update(w_gate=got[0], w_up=got[1], w_down=got[2])
        att_swap = _rs_swap_begin([g_win_sh, g_wout.reshape(N_SHARD, OUT_SH, D_MODEL)], dx, tag=f"{l}a")
        dmod_rows[l] = jnp.concatenate([red_d[0], red_d[1], red_c[0], red_f[2], red_f[3], red_f[0]])
        small[l] = jnp.concatenate([red_d[2], red_c[1], red_f[4], red_f[1], dbin[0], dpsc[0], dsink[:, 0],
                                    jnp.zeros((120,), F32), g_poolw.reshape(-1)])
    grad_x = dx.reshape(1, T, D_MODEL)

    per_layer = small[0].shape[0]
    rows_small = n_layers * per_layer // 128
    rows_mod = n_layers * 6 * D_MODEL // 128
    rows_pad = -(rows_small + rows_mod) % 8
    pack = jnp.concatenate(small + dmod_rows + [jnp.zeros((rows_pad * 128,), F32)]).reshape(-1, 128)
    pack = pack + att_swap[1][0, 0]
    gathered = _allgather8(pack, name="ag_small").reshape(N_DEV, pack.shape[0], 128)
    summed = _sum_devices(gathered)
    att_scatter = _rs_scatter_begin(att_swap[0], summed)
    small_sum = summed[:rows_small].reshape(n_layers, per_layer)
    o = 0
    small_g = {}
    for nm, width in (("g_pre_mix", D_MODEL), ("g_post_mix", D_MODEL), ("g_pre_ffn", D_MODEL),
                      ("g_post_ffn", D_MODEL), ("b_in", IN_W), ("pool_scale", POOL_W), ("sinks", 128),
                      ("pool_w", 4 * 128 * 128)):
        small_g[nm] = small_sum[:, o:o + width]
        o += width
    small_g["sinks"] = small_g["sinks"][:, :N_HEADS]
    small_g["pool_w"] = small_g["pool_w"].reshape(n_layers, 4, 128, 128)
    small_g["ada_b"] = summed[rows_small:rows_small + rows_mod].reshape(n_layers, 6 * D_MODEL)
    dmod_all = gathered[:, rows_small:rows_small + rows_mod].reshape(N_DEV, n_layers, N_SHARD, ADA_SH)
    dmod_sh = lax.dynamic_index_in_dim(dmod_all, my_chip, axis=2, keepdims=False)
    g_ada_w = _ada_wgrad(jnp.transpose(c_all), jnp.transpose(dmod_sh, (1, 0, 2)))

    grads = dict(ada_w=g_ada_w, ada_b=small_g["ada_b"], b_in=small_g["b_in"], sinks=small_g["sinks"],
                 pool_w=small_g["pool_w"], pool_scale=small_g["pool_scale"], g_pre_mix=small_g["g_pre_mix"],
                 g_post_mix=small_g["g_post_mix"], g_pre_ffn=small_g["g_pre_ffn"], g_post_ffn=small_g["g_post_ffn"])
    params = dict(ada_w=(ada_w, m_ada_w, v_ada_w), ada_b=(ada_b, m_ada_b, v_ada_b), w_in=(w_in, m_w_in, v_w_in),
                  b_in=(b_in, m_b_in, v_b_in), sinks=(sinks, m_sinks, v_sinks), pool_w=(pool_w, m_pool_w, v_pool_w),
                  pool_scale=(pool_scale, m_pool_scale, v_pool_scale), w_out=(w_out, m_w_out, v_w_out),
                  w_gate=(w_gate, m_w_gate, v_w_gate), w_up=(w_up, m_w_up, v_w_up),
                  w_down=(w_down, m_w_down, v_w_down), g_pre_mix=(g_pre_mix, m_g_pre_mix, v_g_pre_mix),
                  g_post_mix=(g_post_mix, m_g_post_mix, v_g_post_mix), g_pre_ffn=(g_pre_ffn, m_g_pre_ffn, v_g_pre_ffn),
                  g_post_ffn=(g_post_ffn, m_g_post_ffn, v_g_post_ffn))
    names = list(params)
    updates = {nm: _adamw_nd(*params[nm][:1], grads[nm], *params[nm][1:], name="adamw_" + nm) for nm in grads}

    got = _rs_end(att_scatter[0], updates["ada_w"][0])
    reduced[0].update(w_in=got[0], w_out=got[1])
    for nm in ("w_in", "w_out", "w_gate", "w_up", "w_down"):
        g = jnp.stack([reduced[l][nm] for l in range(n_layers)])
        if nm in ("w_in", "w_gate", "w_up"):
            upd = _adamw_nd(tr(params[nm][0]), g, tr(params[nm][1]), tr(params[nm][2]), name="adamw_" + nm)
            grads[nm], updates[nm] = tr(g), [tr(u) for u in upd]
        else:
            grads[nm], updates[nm] = g, _adamw_nd(params[nm][0], g, *params[nm][1:], name="adamw_" + nm)
    return (loss, grad_x, *[grads[nm] for nm in names], *[updates[nm][0] for nm in names],
            *[updates[nm][1] for nm in names], *[updates[nm][2] for nm in names])
```

```python
import functools

import jax
import jax.numpy as jnp
from jax import lax
from jax.experimental import pallas as pl
from jax.experimental.pallas import tpu as pltpu

F32 = jnp.float32
BF16 = jnp.bfloat16
MESH = pl.DeviceIdType.MESH

D_MODEL = 1024
ATTN_W = 512
KV_W = 128
KVD_W = 256
POOL_W = 512
IN_W = 1280
D_FF = 2816
N_SHARD = 4
FF_SH = D_FF // N_SHARD
IN_SH = IN_W // N_SHARD
OUT_SH = D_MODEL // N_SHARD
ADA_SH = 6 * D_MODEL // N_SHARD
HEAD = 64
N_HEADS = 8
GROUP = 4
BLK = 128
POOL_WINDOWS = (2, 4, 8, 16)
HALO = 16
ROT = 16
ROPE_THETA = 500000.0
EPS = 1e-6
NEG_INF = -1e30
N_DEV = 8

ADAM_LR = 0.001
ADAM_B1 = 0.9
ADAM_B2 = 0.999
ADAM_EPS = 1e-08
ADAM_WD = 0.01
ADAM_STEP = 10

VMEM_LIMIT = 48 * 1024 * 1024
FFN_VMEM_LIMIT = 60 * 1024 * 1024
WGRAD_TOKENS = 2048


def _cp(*sem, vmem=VMEM_LIMIT):
    return pltpu.CompilerParams(dimension_semantics=sem, vmem_limit_bytes=vmem)


def _full(shape):
    nd = len(shape)
    return pl.BlockSpec(shape, lambda *_: (0,) * nd)


def _resident(shape):
    nd = len(shape)
    return pl.BlockSpec(shape, lambda *_: (0,) * nd, pipeline_mode=pl.Buffered(1))


def _rows(tm, ncol):
    return pl.BlockSpec((tm, ncol), lambda i: (i, 0))


def _sds(shape, dtype):
    return jax.ShapeDtypeStruct(shape, dtype)


def _nt(a, b):
    return lax.dot_general(a, b, (((1,), (1,)), ((), ())), preferred_element_type=F32)


def _tn(a, b):
    return lax.dot_general(a, b, (((0,), (0,)), ((), ())), preferred_element_type=F32)


def _mm(a, b):
    return jnp.dot(a, b, preferred_element_type=F32)


def _rstd(x):
    return lax.rsqrt(jnp.mean(x * x, axis=-1, keepdims=True) + EPS)


def _colsum(x):
    return jnp.sum(x, axis=0, keepdims=True)


def _norm_gain_bwd(dy, xhat, rstd, gain):
    p = dy * xhat
    dx = rstd * (dy * gain - xhat * jnp.mean(p * gain, axis=-1, keepdims=True))
    return dx, _colsum(p)


def _rope_tables(pos_b, lane_tab):
    T = pos_b.shape[0]
    tm = min(T, 1024)

    def body(pos_ref, tab_ref, c_ref, s1_ref, s2_ref):
        ang = pos_ref[...].astype(F32) * tab_ref[0:1, :]
        cs = jnp.cos(ang)
        sn = jnp.sin(ang)
        m_rot = tab_ref[1:2, :]
        c_ref[...] = cs * m_rot + (1.0 - m_rot)
        s1_ref[...] = -sn * tab_ref[2:3, :]
        s2_ref[...] = sn * tab_ref[3:4, :]

    out = _sds((T, 128), F32)
    return pl.pallas_call(
        body, name="rope_tables", grid=(T // tm,),
        in_specs=[_rows(tm, 128), _full((8, 128))],
        out_specs=[_rows(tm, 128)] * 3, out_shape=[out] * 3,
        compiler_params=_cp("parallel"),
    )(pos_b, lane_tab)


def _rot_fwd(t, c, s1, s2):
    w = t.shape[-1]
    return t * c + pltpu.roll(t, w - 8, 1) * s1 + pltpu.roll(t, 8, 1) * s2


def _rot_bwd(d, c, s1, s2):
    w = d.shape[-1]
    return d * c + pltpu.roll(d * s1, 8, 1) + pltpu.roll(d * s2, w - 8, 1)


def _store_dup(ref, t):
    low = lax.broadcasted_iota(jnp.int32, t.shape, 1) < HEAD
    sw = pltpu.roll(t, HEAD, 1)
    ref[:, 0:128] = jnp.where(low, t, sw).astype(BF16)
    ref[:, 128:256] = jnp.where(low, sw, t).astype(BF16)


def _fold_dup(d):
    low = lax.broadcasted_iota(jnp.int32, (d.shape[0], 128), 1) < HEAD
    d0 = d[:, 0:128]
    d1 = d[:, 128:256]
    return jnp.where(low, d0 + pltpu.roll(d0, HEAD, 1), d1 + pltpu.roll(d1, HEAD, 1))


def _fwd_in(x, mod8, g8, w_in, b_in, rc, rs1, rs2):
    T = x.shape[0]
    tm = min(T, 1024)

    def body(x_ref, mod_ref, g_ref, w_ref, b_ref, c_ref, s1_ref, s2_ref,
             h_ref, q_ref, k_ref, v_ref, u_ref):
        xf = x_ref[...]
        h = (xf * _rstd(xf) * g_ref[0:1, :]) * (1.0 + mod_ref[1:2, :]) + mod_ref[0:1, :]
        hb = h.astype(BF16)
        h_ref[...] = hb
        c = c_ref[...]
        s1 = s1_ref[...]
        s2 = s2_ref[...]
        q = _nt(hb, w_ref[0:ATTN_W, :]) + b_ref[:, 0:ATTN_W]
        q = _rot_fwd(q, jnp.tile(c, (1, 4)), jnp.tile(s1, (1, 4)), jnp.tile(s2, (1, 4)))
        q_ref[...] = (q * (HEAD ** -0.5)).astype(BF16)
        k = _nt(hb, w_ref[ATTN_W:ATTN_W + KV_W, :]) + b_ref[:, ATTN_W:ATTN_W + KV_W]
        _store_dup(k_ref, _rot_fwd(k, c, s1, s2))
        v = _nt(hb, w_ref[ATTN_W + KV_W:ATTN_W + 2 * KV_W, :]) + b_ref[:, ATTN_W + KV_W:ATTN_W + 2 * KV_W]
        _store_dup(v_ref, v)
        u_ref[...] = _nt(hb, w_ref[ATTN_W + 2 * KV_W:IN_W, :]) + b_ref[:, ATTN_W + 2 * KV_W:IN_W]

    return pl.pallas_call(
        body, name="fwd_in", grid=(T // tm,),
        in_specs=[_rows(tm, D_MODEL), _full((8, D_MODEL)), _full((8, D_MODEL)),
                  _resident((IN_W, D_MODEL)), _full((1, IN_W)),
                  _rows(tm, 128), _rows(tm, 128), _rows(tm, 128)],
        out_specs=[_rows(tm, D_MODEL), _rows(tm, ATTN_W), _rows(tm, KVD_W), _rows(tm, KVD_W), _rows(tm, POOL_W)],
        out_shape=[_sds((T, D_MODEL), BF16), _sds((T, ATTN_W), BF16), _sds((T, KVD_W), BF16),
                   _sds((T, KVD_W), BF16), _sds((T, POOL_W), F32)],
        compiler_params=_cp("parallel"),
    )(x, mod8, g8, w_in, b_in, rc, rs1, rs2)


def _band_mask(n):
    kk = lax.broadcasted_iota(jnp.int32, (2 * BLK, BLK), 0)
    qi = lax.broadcasted_iota(jnp.int32, (2 * BLK, BLK), 1)
    first = jnp.where(n > 0, 0, 2 * BLK)
    in_prev = jnp.logical_and(kk < BLK, kk > qi + first)
    in_cur = jnp.logical_and(kk >= BLK, (kk - BLK) <= qi)
    one = jnp.logical_or(in_prev, in_cur)
    return jnp.concatenate([one] * GROUP, axis=1)


def _head_row(ref, j):
    return jnp.concatenate([ref[GROUP * j + r:GROUP * j + r + 1, :] for r in range(GROUP)], axis=1)


def _stack_heads(x_ref, j, rows=slice(None)):
    low = lax.broadcasted_iota(jnp.int32, (BLK, 128), 1) < HEAD
    parts = []
    for gp in (2 * j, 2 * j + 1):
        x2 = x_ref[rows, gp * 128:(gp + 1) * 128]
        parts.append(jnp.where(low, x2, jnp.zeros_like(x2)))
        parts.append(jnp.where(low, jnp.zeros_like(x2), x2))
    return jnp.concatenate(parts, axis=0)


def _unstack_heads(o):
    low = lax.broadcasted_iota(jnp.int32, (BLK, 128), 1) < HEAD
    return [jnp.where(low, o[0:BLK], o[BLK:2 * BLK]), jnp.where(low, o[2 * BLK:3 * BLK], o[3 * BLK:4 * BLK])]


def _attn_fwd(q, kd, vd, sink_b):
    T = q.shape[0]
    nb = T // BLK
    assert nb % 2 == 0

    def body(q_ref, kp_ref, kc_ref, vp_ref, vc_ref, sk_ref, o_ref, lse_ref):
        for sub in range(2):
            rows = slice(sub * BLK, (sub + 1) * BLK)
            valid = _band_mask(2 * pl.program_id(0) + sub)
            for j in range(N_HEADS // GROUP):
                lanes = slice(j * 128, (j + 1) * 128)
                k_prev = kp_ref[:, lanes] if sub == 0 else kc_ref[0:BLK, lanes]
                v_prev = vp_ref[:, lanes] if sub == 0 else vc_ref[0:BLK, lanes]
                kcat = jnp.concatenate([k_prev, kc_ref[rows, lanes]], axis=0)
                vcat = jnp.concatenate([v_prev, vc_ref[rows, lanes]], axis=0)
                s = jnp.where(valid, _nt(kcat, _stack_heads(q_ref, j, rows)), NEG_INF)
                sk = _head_row(sk_ref, j)
                m = jnp.maximum(jnp.max(s, axis=0, keepdims=True), sk)
                p = jnp.exp(s - m)
                den = jnp.sum(p, axis=0, keepdims=True) + jnp.exp(sk - m)
                p = p * (1.0 / den)
                o = _tn(p.astype(BF16), vcat)
                o_ref[rows, 2 * j * 128:(2 * j + 2) * 128] = jnp.concatenate(_unstack_heads(o), axis=1).astype(BF16)
                lse = m + jnp.log(den)
                for r in range(GROUP):
                    h = sub * N_HEADS + GROUP * j + r
                    lse_ref[h:h + 1, :] = lse[:, r * 128:(r + 1) * 128]

    prev = lambda i: (jnp.maximum(2 * i - 1, 0), 0)
    cur = lambda i: (i, 0)
    return pl.pallas_call(
        body, name="attn_fwd", grid=(nb // 2,),
        in_specs=[pl.BlockSpec((2 * BLK, ATTN_W), cur),
                  pl.BlockSpec((BLK, KVD_W), prev), pl.BlockSpec((2 * BLK, KVD_W), cur),
                  pl.BlockSpec((BLK, KVD_W), prev), pl.BlockSpec((2 * BLK, KVD_W), cur),
                  _full((8, 128))],
        out_specs=[pl.BlockSpec((2 * BLK, ATTN_W), cur), pl.BlockSpec((2 * N_HEADS, 128), cur)],
        out_shape=[_sds((T, ATTN_W), BF16), _sds((nb * N_HEADS, 128), F32)],
        compiler_params=_cp("parallel"),
    )(q, kd, kd, vd, vd, sink_b)


def _pool_fwd(u, pool_w, pool_scale):
    T = u.shape[0]
    tm = min(T, 1024)

    def body(u_ref, w_ref, sc_ref, out_ref, pooled_ref, halo):
        i = pl.program_id(0)

        @pl.when(i == 0)
        def _():
            halo[...] = jnp.zeros_like(halo)

        ub = u_ref[...]
        ext = jnp.concatenate([halo[...], ub], axis=0)
        halo[...] = ub[tm - HALO:, :]
        tpos = (i * tm + lax.broadcasted_iota(jnp.int32, (tm, 1), 0)).astype(F32)
        for g, w in enumerate(POOL_WINDOWS):
            lanes = slice(g * 128, (g + 1) * 128)
            s = ext[:, lanes]
            sh = 1
            while sh < w:
                s = s + pltpu.roll(s, sh, 0)
                sh *= 2
            cnt = jnp.minimum(tpos + 1.0, float(w))
            pb = (s[HALO:, :] / cnt - ub[:, lanes]).astype(BF16)
            z = _mm(pb, w_ref[g].astype(BF16))
            out_ref[:, lanes] = (z * sc_ref[:, lanes]).astype(BF16)
            pooled_ref[:, lanes] = pb

    return pl.pallas_call(
        body, name="pool_fwd", grid=(T // tm,),
        in_specs=[_rows(tm, POOL_W), _full((4, 128, 128)), _full((1, POOL_W))],
        out_specs=[_rows(tm, POOL_W), _rows(tm, POOL_W)],
        out_shape=[_sds((T, POOL_W), BF16), _sds((T, POOL_W), BF16)],
        scratch_shapes=[pltpu.VMEM((HALO, POOL_W), F32)],
        compiler_params=_cp("arbitrary"),
    )(u, pool_w, pool_scale)


FF_CHUNKS = ((0, 768), (768, 1536), (1536, 2304), (2304, D_FF))


def _out_ffn_fwd(attn, pool, x, w_out, mod8, g8, wg, wu, wd, target=None):
    T = x.shape[0]
    tm = min(T, 256)
    last = target is not None

    def body(*refs):
        a_ref, p_ref, xin_ref, wo_ref, mod_ref, g_ref, wg_ref, wu_ref, wd_ref = refs[:9]
        t_ref = refs[9] if last else None
        mix_ref, x1_ref, h_ref, act_ref, ga_ref, gb_ref, f_ref, x2_ref = refs[9 + last:17 + last]
        mix = _mm(a_ref[...], wo_ref[0:ATTN_W, :]) + _mm(p_ref[...], wo_ref[ATTN_W:, :])
        mix_ref[...] = mix
        xf = xin_ref[...] + mod_ref[2:3, :] * (mix * _rstd(mix) * g_ref[1:2, :])
        x1_ref[...] = xf
        h = (xf * _rstd(xf) * g_ref[2:3, :]) * (1.0 + mod_ref[4:5, :]) + mod_ref[3:4, :]
        hb = h.astype(BF16)
        h_ref[...] = hb
        f = jnp.zeros((tm, D_MODEL), F32)
        for lo, hi in FF_CHUNKS:
            a = _nt(hb, wg_ref[lo:hi, :])
            b = _nt(hb, wu_ref[lo:hi, :])
            sig = jax.nn.sigmoid(a)
            sl = a * sig
            act = (sl * b).astype(BF16)
            act_ref[:, lo:hi] = act
            ga_ref[:, lo:hi] = (b * (sig * (1.0 + a * (1.0 - sig)))).astype(BF16)
            gb_ref[:, lo:hi] = sl.astype(BF16)
            f = f + _mm(act, wd_ref[lo:hi, :])
        f_ref[...] = f
        x2 = xf + mod_ref[5:6, :] * (f * _rstd(f) * g_ref[3:4, :])
        if not last:
            x2_ref[...] = x2
        else:
            loss_ref = refs[18]

            @pl.when(pl.program_id(0) == 0)
            def _():
                loss_ref[...] = jnp.zeros_like(loss_ref)

            e = x2 - t_ref[...]
            x2_ref[...] = e * (1.0 / D_MODEL)
            loss_ref[...] += 0.5 * jnp.sum(jnp.mean(e * e, axis=-1, keepdims=True), axis=0, keepdims=True)

    act_shape = _sds((T, D_FF), BF16)
    wide = _sds((T, D_MODEL), F32)
    weights = [_resident((D_FF, D_MODEL))] * 3
    return pl.pallas_call(
        body, name="out_ffn_fwd_loss" if last else "out_ffn_fwd", grid=(T // tm,),
        in_specs=[_rows(tm, ATTN_W), _rows(tm, POOL_W), _rows(tm, D_MODEL), _resident((D_MODEL, D_MODEL)),
                  _full((8, D_MODEL)), _full((8, D_MODEL)), *weights]
        + ([_rows(tm, D_MODEL)] if last else []),
        out_specs=[_rows(tm, D_MODEL), _rows(tm, D_MODEL), _rows(tm, D_MODEL), _rows(tm, D_FF), _rows(tm, D_FF),
                   _rows(tm, D_FF), _rows(tm, D_MODEL), _rows(tm, D_MODEL)] + ([_full((8, 128))] if last else []),
        out_shape=[wide, wide, _sds((T, D_MODEL), BF16), act_shape, act_shape, act_shape, wide, wide]
        + ([_sds((8, 128), F32)] if last else []),
        compiler_params=_cp("arbitrary" if last else "parallel", vmem=FFN_VMEM_LIMIT),
    )(attn, pool, x, w_out, mod8, g8, wg, wu, wd, *([target] if last else []))


def _ffn_bwd(dx2, f, ga, gb, x1, mod8, g8, wg, wu, wd):
    T = dx2.shape[0]
    tm = min(T, 256)

    def body(dx_ref, f_ref, ga_ref, gb_ref, x_ref, mod_ref, g_ref, wg_ref, wu_ref, wd_ref,
             dx1_ref, df_ref, da_ref, db_ref, red_ref):
        @pl.when(pl.program_id(0) == 0)
        def _():
            red_ref[...] = jnp.zeros_like(red_ref)

        dx = dx_ref[...]
        fv = f_ref[...]
        rstd = _rstd(fv)
        fhat = fv * rstd
        gpost = g_ref[3:4, :]
        gate = mod_ref[5:6, :]
        df, s_post = _norm_gain_bwd(dx, fhat, rstd, gate * gpost)
        red_ref[0:1, :] += gpost * s_post
        red_ref[1:2, :] += gate * s_post
        dfb = df.astype(BF16)
        df_ref[...] = dfb
        dh = jnp.zeros((tm, D_MODEL), F32)
        for lo, hi in FF_CHUNKS:
            dact = _nt(dfb, wd_ref[lo:hi, :])
            da = (dact * ga_ref[:, lo:hi].astype(F32)).astype(BF16)
            db = (dact * gb_ref[:, lo:hi].astype(F32)).astype(BF16)
            da_ref[:, lo:hi] = da
            db_ref[:, lo:hi] = db
            dh = dh + _mm(da, wg_ref[lo:hi, :]) + _mm(db, wu_ref[lo:hi, :])
        xf = x_ref[...]
        rstd1 = _rstd(xf)
        xhat = xf * rstd1
        gpre = g_ref[2:3, :]
        scale1 = 1.0 + mod_ref[4:5, :]
        dxn, s_pre = _norm_gain_bwd(dh, xhat, rstd1, scale1 * gpre)
        red_ref[2:3, :] += _colsum(dh)
        red_ref[3:4, :] += gpre * s_pre
        red_ref[4:5, :] += scale1 * s_pre
        dx1_ref[...] = dx + dxn

    act_shape = _sds((T, D_FF), BF16)
    return pl.pallas_call(
        body, name="ffn_bwd", grid=(T // tm,),
        in_specs=[_rows(tm, D_MODEL), _rows(tm, D_MODEL), _rows(tm, D_FF), _rows(tm, D_FF), _rows(tm, D_MODEL),
                  _full((8, D_MODEL)), _full((8, D_MODEL)),
                  _resident((D_FF, D_MODEL)), _resident((D_FF, D_MODEL)), _resident((D_FF, D_MODEL))],
        out_specs=[_rows(tm, D_MODEL), _rows(tm, D_MODEL), _rows(tm, D_FF), _rows(tm, D_FF), _full((8, D_MODEL))],
        out_shape=[_sds((T, D_MODEL), F32), _sds((T, D_MODEL), BF16), act_shape, act_shape, _sds((8, D_MODEL), F32)],
        compiler_params=_cp("arbitrary"),
    )(dx2, f, ga, gb, x1, mod8, g8, wg, wu, wd)


def _wgrad(a, b, name, after=None):
    T, K = a.shape
    N = b.shape[1]
    tt = min(T, WGRAD_TOKENS)
    tk = next(c for c in (1408, 640, 512, 256, 128) if K % c == 0)

    def body(a_ref, b_ref, *rest):
        o_ref = rest[-1]

        @pl.when(pl.program_id(1) == 0)
        def _():
            o_ref[...] = jnp.zeros_like(o_ref)

        o_ref[...] += _tn(a_ref[...], b_ref[...])

    extra = [] if after is None else [after]
    return pl.pallas_call(
        body, name=name, grid=(K // tk, T // tt),
        in_specs=[pl.BlockSpec((tt, tk), lambda i, t: (t, i)), pl.BlockSpec((tt, N), lambda i, t: (t, 0))]
        + [pl.BlockSpec(memory_space=pl.ANY)] * len(extra),
        out_specs=pl.BlockSpec((tk, N), lambda i, t: (i, 0)),
        out_shape=_sds((K, N), F32),
        compiler_params=_cp("parallel", "arbitrary"),
    )(a, b, *extra)


def _mix_bwd(dx1, mix, mod8, g8, w_out):
    T = dx1.shape[0]
    tm = min(T, 1024)

    def body(dx_ref, mix_ref, mod_ref, g_ref, w_ref, dmix_ref, da_ref, dp_ref, red_ref):
        @pl.when(pl.program_id(0) == 0)
        def _():
            red_ref[...] = jnp.zeros_like(red_ref)

        dx = dx_ref[...]
        mv = mix_ref[...]
        rstd = _rstd(mv)
        mhat = mv * rstd
        gpost = g_ref[1:2, :]
        gate = mod_ref[2:3, :]
        dm, s_post = _norm_gain_bwd(dx, mhat, rstd, gate * gpost)
        red_ref[0:1, :] += gpost * s_post
        red_ref[1:2, :] += gate * s_post
        dmb = dm.astype(BF16)
        dmix_ref[...] = dmb
        da_ref[...] = _nt(dmb, w_ref[0:ATTN_W, :]).astype(BF16)
        dp_ref[...] = _nt(dmb, w_ref[ATTN_W:, :]).astype(BF16)

    return pl.pallas_call(
        body, name="mix_bwd", grid=(T // tm,),
        in_specs=[_rows(tm, D_MODEL), _rows(tm, D_MODEL), _full((8, D_MODEL)), _full((8, D_MODEL)),
                  _resident((D_MODEL, D_MODEL))],
        out_specs=[_rows(tm, D_MODEL), _rows(tm, ATTN_W), _rows(tm, POOL_W), _full((8, D_MODEL))],
        out_shape=[_sds((T, D_MODEL), BF16), _sds((T, ATTN_W), BF16), _sds((T, POOL_W), BF16),
                   _sds((8, D_MODEL), F32)],
        compiler_params=_cp("arbitrary"),
    )(dx1, mix, mod8, g8, w_out)


def _attn_bwd(q, kd, vd, lse, dattn, sink_b):
    T = q.shape[0]
    nb = T // BLK

    def body(q_ref, do_ref, lse_ref, kp_ref, kc_ref, vp_ref, vc_ref, sk_ref,
             dq_ref, dk_ref, dv_ref, dsk_ref, carry_k, carry_v):
        n = pl.program_id(0)

        @pl.when(n == 0)
        def _():
            carry_k[...] = jnp.zeros_like(carry_k)
            carry_v[...] = jnp.zeros_like(carry_v)
            dsk_ref[...] = jnp.zeros_like(dsk_ref)

        @pl.when(n < nb)
        def _():
            valid = _band_mask(n)
            for j in range(N_HEADS // GROUP):
                lanes = slice(j * 128, (j + 1) * 128)
                kcat = jnp.concatenate([kp_ref[:, lanes], kc_ref[:, lanes]], axis=0)
                vcat = jnp.concatenate([vp_ref[:, lanes], vc_ref[:, lanes]], axis=0)
                qs = _stack_heads(q_ref, j)
                dos = _stack_heads(do_ref, j)
                lse = _head_row(lse_ref, j)
                p = jnp.exp(jnp.where(valid, _nt(kcat, qs), NEG_INF) - lse)
                dp = _nt(vcat, dos)
                delta = jnp.sum(p * dp, axis=0, keepdims=True)
                ds = (p * (dp - delta)).astype(BF16)
                sink_term = jnp.exp(_head_row(sk_ref, j) - lse) * delta
                for r in range(GROUP):
                    h = GROUP * j + r
                    dsk_ref[h:h + 1, :] += -jnp.sum(sink_term[:, r * 128:(r + 1) * 128], axis=1, keepdims=True)
                dq_ref[:, 2 * j * 128:(2 * j + 2) * 128] = jnp.concatenate(_unstack_heads(_tn(ds, kcat)), axis=1)
                dk = _mm(ds, qs)
                dv = _mm(p.astype(BF16), dos)
                dk_ref[:, lanes] = carry_k[:, lanes] + dk[0:BLK]
                dv_ref[:, lanes] = carry_v[:, lanes] + dv[0:BLK]
                carry_k[:, lanes] = dk[BLK:]
                carry_v[:, lanes] = dv[BLK:]

        @pl.when(n == nb)
        def _():
            dk_ref[...] = carry_k[...]
            dv_ref[...] = carry_v[...]

    cur = lambda n: (jnp.minimum(n, nb - 1), 0)
    prev = lambda n: (jnp.maximum(n - 1, 0), 0)
    return pl.pallas_call(
        body, name="attn_bwd", grid=(nb + 1,),
        in_specs=[pl.BlockSpec((BLK, ATTN_W), cur), pl.BlockSpec((BLK, ATTN_W), cur), pl.BlockSpec((N_HEADS, 128), cur),
                  pl.BlockSpec((BLK, KVD_W), prev), pl.BlockSpec((BLK, KVD_W), cur),
                  pl.BlockSpec((BLK, KVD_W), prev), pl.BlockSpec((BLK, KVD_W), cur),
                  _full((8, 128))],
        out_specs=[pl.BlockSpec((BLK, ATTN_W), cur), pl.BlockSpec((BLK, KVD_W), prev),
                   pl.BlockSpec((BLK, KVD_W), prev), _full((8, 128))],
        out_shape=[_sds((T, ATTN_W), F32), _sds((T, KVD_W), F32), _sds((T, KVD_W), F32), _sds((8, 128), F32)],
        scratch_shapes=[pltpu.VMEM((BLK, KVD_W), F32), pltpu.VMEM((BLK, KVD_W), F32)],
        compiler_params=_cp("arbitrary"),
    )(q, dattn, lse, kd, kd, vd, vd, sink_b)


def _pool_bwd(dpool, pooled, pool_w, pool_scale):
    T = dpool.shape[0]
    tm = min(T, 1024)
    nbk = T // tm
    ext_rows = tm + HALO

    def body(dp_ref, pl_ref, w_ref, sc_ref, du_ref, dw_ref, dsc_ref, halo):
        i = pl.program_id(0)

        @pl.when(i == 0)
        def _():
            halo[...] = jnp.zeros_like(halo)
            dw_ref[...] = jnp.zeros_like(dw_ref)
            dsc_ref[...] = jnp.zeros_like(dsc_ref)

        blk = nbk - 1 - i
        tpos = (blk * tm + lax.broadcasted_iota(jnp.int32, (tm, 1), 0)).astype(F32)
        for g, w in enumerate(POOL_WINDOWS):
            lanes = slice(g * 128, (g + 1) * 128)
            dp = dp_ref[:, lanes].astype(F32)
            pb = pl_ref[:, lanes]
            wg = w_ref[g].astype(BF16)
            z = _mm(pb, wg)
            dsc_ref[0:1, lanes] += _colsum(dp * z)
            dz = (dp * sc_ref[:, lanes]).astype(BF16)
            dw_ref[g] += _tn(pb, dz)
            dpl = _nt(dz, wg)
            e = dpl / jnp.minimum(tpos + 1.0, float(w))
            s = jnp.concatenate([e, halo[:, lanes]], axis=0)
            halo[:, lanes] = e[0:HALO, :]
            sh = 1
            while sh < w:
                s = s + pltpu.roll(s, ext_rows - sh, 0)
                sh *= 2
            du_ref[:, lanes] = s[0:tm, :] - dpl

    rev = lambda i: (nbk - 1 - i, 0)
    return pl.pallas_call(
        body, name="pool_bwd", grid=(nbk,),
        in_specs=[pl.BlockSpec((tm, POOL_W), rev), pl.BlockSpec((tm, POOL_W), rev),
                  _full((4, 128, 128)), _full((1, POOL_W))],
        out_specs=[pl.BlockSpec((tm, POOL_W), rev), _full((4, 128, 128)), _full((8, POOL_W))],
        out_shape=[_sds((T, POOL_W), F32), _sds((4, 128, 128), F32), _sds((8, POOL_W), F32)],
        scratch_shapes=[pltpu.VMEM((HALO, POOL_W), F32)],
        compiler_params=_cp("arbitrary"),
    )(dpool, pooled, pool_w, pool_scale)


def _in_bwd(dq, dk, dv, du, rc, rs1, rs2, x, dx1, mod8, g8, w_in):
    T = x.shape[0]
    tm = min(T, 512)

    def body(dq_ref, dk_ref, dv_ref, du_ref, c_ref, s1_ref, s2_ref, x_ref, dx1_ref, mod_ref, g_ref, w_ref,
             dx_ref, dproj_ref, red_ref, dbin_ref):
        @pl.when(pl.program_id(0) == 0)
        def _():
            red_ref[...] = jnp.zeros_like(red_ref)
            dbin_ref[...] = jnp.zeros_like(dbin_ref)

        c = c_ref[...]
        s1 = s1_ref[...]
        s2 = s2_ref[...]
        dqp = _rot_bwd(dq_ref[...] * (HEAD ** -0.5), jnp.tile(c, (1, 4)), jnp.tile(s1, (1, 4)), jnp.tile(s2, (1, 4)))
        dkp = _rot_bwd(_fold_dup(dk_ref[...]), c, s1, s2)
        pieces = ((0, ATTN_W, dqp), (ATTN_W, ATTN_W + KV_W, dkp),
                  (ATTN_W + KV_W, ATTN_W + 2 * KV_W, _fold_dup(dv_ref[...])), (ATTN_W + 2 * KV_W, IN_W, du_ref[...]))
        dh = jnp.zeros((tm, D_MODEL), F32)
        for lo, hi, val in pieces:
            dbin_ref[0:1, lo:hi] += _colsum(val)
            vb = val.astype(BF16)
            dproj_ref[:, lo:hi] = vb
            dh = dh + _mm(vb, w_ref[lo:hi, :])
        xf = x_ref[...]
        rstd = _rstd(xf)
        xhat = xf * rstd
        gpre = g_ref[0:1, :]
        scale1 = 1.0 + mod_ref[1:2, :]
        dxn, s_pre = _norm_gain_bwd(dh, xhat, rstd, scale1 * gpre)
        red_ref[0:1, :] += _colsum(dh)
        red_ref[1:2, :] += gpre * s_pre
        red_ref[2:3, :] += scale1 * s_pre
        dx_ref[...] = dx1_ref[...] + dxn

    return pl.pallas_call(
        body, name="in_bwd", grid=(T // tm,),
        in_specs=[_rows(tm, ATTN_W), _rows(tm, KVD_W), _rows(tm, KVD_W), _rows(tm, POOL_W),
                  _rows(tm, 128), _rows(tm, 128), _rows(tm, 128), _rows(tm, D_MODEL), _rows(tm, D_MODEL),
                  _full((8, D_MODEL)), _full((8, D_MODEL)), _resident((IN_W, D_MODEL))],
        out_specs=[_rows(tm, D_MODEL), _rows(tm, IN_W), _full((8, D_MODEL)), _full((8, IN_W))],
        out_shape=[_sds((T, D_MODEL), F32), _sds((T, IN_W), BF16), _sds((8, D_MODEL), F32), _sds((8, IN_W), F32)],
        compiler_params=_cp("arbitrary"),
    )(dq, dk, dv, du, rc, rs1, rs2, x, dx1, mod8, g8, w_in)


def _mod_fwd(c_all, ada_w, ada_b_sh):
    tn = 512

    def body(c_ref, w_ref, b_ref, o_ref):
        cv = c_ref[...]
        ca = (cv * jax.nn.sigmoid(cv)).astype(BF16)
        o_ref[...] = _mm(ca, w_ref[...].astype(BF16)) + b_ref[...]

    return pl.pallas_call(
        body, name="mod_fwd", grid=(2, ADA_SH // tn),
        in_specs=[_full((8, D_MODEL)), pl.BlockSpec((None, D_MODEL, tn), lambda l, j: (l, 0, j)),
                  pl.BlockSpec((None, 1, tn), lambda l, j: (l, 0, j))],
        out_specs=pl.BlockSpec((None, 8, tn), lambda l, j: (l, 0, j)),
        out_shape=_sds((2, 8, ADA_SH), F32),
        compiler_params=_cp("parallel", "parallel"),
    )(c_all, ada_w, ada_b_sh)


def _ada_wgrad(c_all_t, dmod_sh):
    tn = 512

    def body(c_ref, d_ref, o_ref):
        cv = c_ref[...]
        ca = cv * jax.nn.sigmoid(cv)
        o_ref[...] = jnp.dot(ca, d_ref[...], preferred_element_type=F32, precision=lax.Precision.HIGHEST)

    return pl.pallas_call(
        body, name="ada_wgrad", grid=(2, ADA_SH // tn),
        in_specs=[_full((D_MODEL, 8)), pl.BlockSpec((None, 8, tn), lambda l, j: (l, 0, j))],
        out_specs=pl.BlockSpec((None, D_MODEL, tn), lambda l, j: (l, 0, j)),
        out_shape=_sds((2, D_MODEL, ADA_SH), F32),
        compiler_params=_cp("parallel", "parallel"),
    )(c_all_t, dmod_sh)


def _sum_devices(g):
    R = g.shape[1]

    def body(g_ref, o_ref):
        acc = g_ref[0]
        for d in range(1, N_DEV):
            acc = acc + g_ref[d]
        o_ref[...] = acc

    return pl.pallas_call(
        body, name="sum_devices", grid=(1,),
        in_specs=[_full((N_DEV, R, 128))], out_specs=_full((R, 128)), out_shape=_sds((R, 128), F32),
        compiler_params=_cp("arbitrary"),
    )(g)


def _adamw(w, g, m, v, name):
    R, C = w.shape
    tr = R
    for cand in (256, 128, 64, 32, 16, 8):
        if R % cand == 0 and cand * C * 4 <= 2 * 1024 * 1024:
            tr = cand
            break

    def body(w_ref, g_ref, m_ref, v_ref, d_ref, nm_ref, nv_ref):
        gv = g_ref[...]
        mn = ADAM_B1 * m_ref[...] + (1.0 - ADAM_B1) * gv
        vn = ADAM_B2 * v_ref[...] + (1.0 - ADAM_B2) * (gv * gv)
        m_hat = mn / (1.0 - ADAM_B1 ** ADAM_STEP)
        v_hat = vn / (1.0 - ADAM_B2 ** ADAM_STEP)
        d_ref[...] = -ADAM_LR * (m_hat / (jnp.sqrt(v_hat) + ADAM_EPS) + ADAM_WD * w_ref[...])
        nm_ref[...] = mn
        nv_ref[...] = vn

    spec = pl.BlockSpec((tr, C), lambda i: (i, 0))
    out = _sds((R, C), F32)
    return pl.pallas_call(
        body, name=name, grid=(R // tr,),
        in_specs=[spec] * 4, out_specs=[spec] * 3, out_shape=[out] * 3,
        compiler_params=_cp("parallel"),
    )(w, g, m, v)


def _adamw_nd(w, g, m, v, name):
    shape = w.shape
    if w.ndim == 2 and shape[1] < 128:
        view = (1, shape[0] * shape[1])
    else:
        view = (-1, shape[-1])
    outs = _adamw(*[t.reshape(view) for t in (w, g, m, v)], name=name)
    return [o.reshape(shape) for o in outs]


def _coords():
    return lax.axis_index("x"), lax.axis_index("y"), lax.axis_index("c")


def _other_chips(x, y):
    return [(1 - x, y), (x, 1 - y), (1 - x, 1 - y)]


def _allgather8(blk, name):
    m_per, n = blk.shape

    def body(x_ref, out_ref, send_sems, recv_sems, local_sem):
        x, y, c = _coords()
        me, sibling = (x, y, c), (x, y, 1 - c)
        chips = _other_chips(x, y)

        def rows(px, py, pc):
            return out_ref.at[pl.ds((4 * px + 2 * py + pc) * m_per, m_per), :]

        def copy(k, block, to, src=None):
            return pltpu.make_async_remote_copy(
                src_ref=rows(*block) if src is None else src, dst_ref=rows(*block),
                send_sem=send_sems.at[k], recv_sem=recv_sems.at[k], device_id=to, device_id_type=MESH)

        mine = pltpu.make_async_copy(x_ref, rows(*me), local_sem)
        mine.start()
        first = [copy(0, me, sibling, src=x_ref)]
        first += [copy(1 + j, me, (*chip, c), src=x_ref) for j, chip in enumerate(chips)]
        for cp in first:
            cp.start()
        passed = [copy(4 + j, (*chip, c), sibling) for j, chip in enumerate(chips)]
        for j, chip in enumerate(chips):
            copy(1 + j, (*chip, c), me).wait_recv()
            passed[j].start()
        copy(0, sibling, me).wait_recv()
        for j, chip in enumerate(chips):
            copy(4 + j, (*chip, 1 - c), me).wait_recv()
        for cp in first + passed:
            cp.wait_send()
        mine.wait()

    return pl.pallas_call(
        body, name=name,
        out_shape=_sds((N_DEV * m_per, n), blk.dtype),
        in_specs=[pl.BlockSpec(memory_space=pltpu.VMEM)],
        out_specs=pl.BlockSpec(memory_space=pltpu.VMEM),
        scratch_shapes=[pltpu.SemaphoreType.DMA((7,)), pltpu.SemaphoreType.DMA((7,)), pltpu.SemaphoreType.DMA],
        compiler_params=pltpu.CompilerParams(vmem_limit_bytes=VMEM_LIMIT),
    )(blk)


def _row_tile(r, n):
    for cand in range(r, 15, -16):
        if r % cand == 0 and cand % 16 == 0 and cand * n * 4 <= 2 * 1024 * 1024:
            return cand
    return r


def _cast_slot(w, chip, name):
    r, n = w.shape
    tr = _row_tile(r, n)

    def body(chip_ref, w_ref, o_ref):
        o_ref[...] = w_ref[...].astype(BF16)

    grid_spec = pltpu.PrefetchScalarGridSpec(
        num_scalar_prefetch=1, grid=(r // tr,),
        in_specs=[pl.BlockSpec((tr, n), lambda i, ch: (i, 0))],
        out_specs=pl.BlockSpec((None, tr, n), lambda i, ch: (ch[0], i, 0)))
    return pl.pallas_call(
        body, name=name, grid_spec=grid_spec, out_shape=_sds((N_SHARD, r, n), BF16),
        compiler_params=_cp("arbitrary"),
    )(chip, w)


def _allgather_weights(bufs, name):
    nt = len(bufs)
    hom = [pl.BlockSpec(memory_space=pl.ANY)] * nt

    def body(*refs):
        outs = refs[nt:2 * nt]
        send_sems, recv_sems = refs[2 * nt:]
        x, y, c = _coords()
        sibling = (x, y, 1 - c)
        chips = _other_chips(x, y)

        def copy(t, k, block_chip, hc, to):
            r = outs[t].shape[1] // 2
            blk = outs[t].at[2 * block_chip[0] + block_chip[1], pl.ds(hc * r, r)]
            return pltpu.make_async_remote_copy(
                src_ref=blk, dst_ref=blk,
                send_sem=send_sems.at[t, k], recv_sem=recv_sems.at[t, k], device_id=to, device_id_type=MESH)

        started = []
        for t in range(nt):
            for j, chip in enumerate(chips):
                cp = copy(t, j, (x, y), c, (*chip, c))
                cp.start()
                started.append(cp)
        for t in range(nt):
            for j, chip in enumerate(chips):
                copy(t, j, chip, c, sibling).wait_recv()
                fw = copy(t, 3 + j, chip, c, sibling)
                fw.start()
                started.append(fw)
        for t in range(nt):
            for j, chip in enumerate(chips):
                copy(t, 3 + j, chip, 1 - c, sibling).wait_recv()
        for cp in started:
            cp.wait_send()

    return pl.pallas_call(
        body, name=name,
        out_shape=[_sds(b.shape, b.dtype) for b in bufs],
        in_specs=hom, out_specs=hom,
        input_output_aliases={t: t for t in range(nt)},
        scratch_shapes=[pltpu.SemaphoreType.DMA((nt, 6)), pltpu.SemaphoreType.DMA((nt, 6))],
    )(*bufs)


def _join_halves(tots, name):
    nt = len(tots)
    hom = [pl.BlockSpec(memory_space=pl.ANY)] * nt

    def body(*refs):
        outs = refs[nt:2 * nt]
        send_sems, recv_sems = refs[2 * nt:]
        x, y, c = _coords()
        sibling = (x, y, 1 - c)
        cps = []
        for t in range(nt):
            cp = pltpu.make_async_remote_copy(
                src_ref=outs[t].at[c], dst_ref=outs[t].at[c],
                send_sem=send_sems.at[t], recv_sem=recv_sems.at[t], device_id=sibling, device_id_type=MESH)
            cp.start()
            cps.append(cp)
        for t in range(nt):
            pltpu.make_async_remote_copy(
                src_ref=outs[t].at[c], dst_ref=outs[t].at[1 - c],
                send_sem=send_sems.at[t], recv_sem=recv_sems.at[t], device_id=sibling, device_id_type=MESH).wait_recv()
        for cp in cps:
            cp.wait_send()

    return pl.pallas_call(
        body, name=name,
        out_shape=[_sds(t.shape, t.dtype) for t in tots],
        in_specs=hom, out_specs=hom,
        input_output_aliases={t: t for t in range(nt)},
        scratch_shapes=[pltpu.SemaphoreType.DMA((nt,)), pltpu.SemaphoreType.DMA((nt,))],
    )(*tots)


def _pair_sum(g, recv, core, chip, name):
    _, _, r, n = g.shape
    tr = _row_tile(r, n)

    def body(core_ref, chip_ref, g_ref, r_ref, sb_ref, own_ref):
        tot = g_ref[...] + r_ref[...]
        sb_ref[...] = tot.astype(BF16)

        @pl.when(pl.program_id(1) == chip_ref[0])
        def _():
            own_ref[...] = tot

    grid_spec = pltpu.PrefetchScalarGridSpec(
        num_scalar_prefetch=2, grid=(r // tr, N_SHARD),
        in_specs=[pl.BlockSpec((None, None, tr, n), lambda i, s, co, ch: (s, co[0], i, 0)),
                  pl.BlockSpec((None, tr, n), lambda i, s, co, ch: (s, i, 0))],
        out_specs=[pl.BlockSpec((None, tr, n), lambda i, s, co, ch: (s, i, 0)),
                   pl.BlockSpec((tr, n), lambda i, s, co, ch: (i, 0))])
    return pl.pallas_call(
        body, name=name, grid_spec=grid_spec,
        out_shape=[_sds((N_SHARD, r, n), BF16), _sds((r, n), F32)],
        compiler_params=_cp("arbitrary", "arbitrary"),
    )(core, chip, g, recv)


def _chip_sum(own, recv, core, name):
    r, n = own.shape
    tr = _row_tile(r, n)

    def body(core_ref, o_ref, r_ref, t_ref):
        acc = o_ref[...]
        for j in range(3):
            acc = acc + r_ref[j].astype(F32)
        t_ref[...] = acc

    grid_spec = pltpu.PrefetchScalarGridSpec(
        num_scalar_prefetch=1, grid=(r // tr,),
        in_specs=[pl.BlockSpec((tr, n), lambda i, co: (i, 0)), pl.BlockSpec((3, tr, n), lambda i, co: (0, i, 0))],
        out_specs=pl.BlockSpec((None, tr, n), lambda i, co: (co[0], i, 0)))
    return pl.pallas_call(
        body, name=name, grid_spec=grid_spec, out_shape=_sds((2, r, n), F32),
        compiler_params=_cp("arbitrary"),
    )(core, own, recv)


_HBM = pl.BlockSpec(memory_space=pltpu.HBM)
_SEM = pl.BlockSpec(memory_space=pltpu.SEMAPHORE)
_EFFECT = pltpu.SideEffectType.DATAFLOW_SIDE_EFFECTING


def _ici_copies(srcs, dsts, send_sems, recv_sems, send_view, recv_view):
    x, y, c = _coords()
    out = []
    if send_view is None:
        for t in range(len(srcs)):
            r = srcs[t].shape[1] // 2
            out.append(pltpu.make_async_remote_copy(
                src_ref=srcs[t].at[:, pl.ds((1 - c) * r, r)], dst_ref=dsts[t],
                send_sem=send_sems.at[3 * t], recv_sem=recv_sems.at[3 * t],
                device_id=(x, y, 1 - c), device_id_type=MESH))
        return out
    for t in range(len(srcs)):
        for j, chip in enumerate(_other_chips(x, y)):
            out.append(pltpu.make_async_remote_copy(
                src_ref=send_view(srcs[t], chip, j, (x, y), c), dst_ref=recv_view(dsts[t], chip, j, (x, y), c),
                send_sem=send_sems.at[3 * t + j], recv_sem=recv_sems.at[3 * t + j],
                device_id=(*chip, c), device_id_type=MESH))
    return out


def _ici_start(srcs, dsts, after, send_view, recv_view, name):
    nt = len(srcs)
    inplace = dsts is None
    nbuf = nt if inplace else 2 * nt

    def body(*refs):
        send_sems, recv_sems = refs[nbuf + 1], refs[nbuf + 2]
        s_out = refs[nbuf + 3:nbuf + 3 + nt]
        d_out = s_out if inplace else refs[nbuf + 3 + nt:nbuf + 3 + 2 * nt]
        token = refs[-1]
        for cp in _ici_copies(s_out, d_out, send_sems, recv_sems, send_view, recv_view):
            cp.start()
        token[...] = jnp.zeros_like(token)

    bufs = list(srcs) + ([] if inplace else list(dsts))
    res = pl.pallas_call(
        body, name=name,
        out_shape=(pltpu.SemaphoreType.DMA((3 * nt,)), pltpu.SemaphoreType.DMA((3 * nt,)),
                   *[pltpu.HBM(b.shape, b.dtype) for b in bufs], _sds((8, 128), F32)),
        in_specs=[_HBM] * nbuf + [pl.BlockSpec(memory_space=pl.ANY)],
        out_specs=(_SEM, _SEM, *[_HBM] * nbuf, pl.BlockSpec(memory_space=pltpu.VMEM)),
        input_output_aliases={i: 2 + i for i in range(nbuf)},
        compiler_params=pltpu.CompilerParams(has_side_effects=_EFFECT),
    )(*[pltpu.with_memory_space_constraint(b, pltpu.HBM) for b in bufs], after)
    send_sems, recv_sems = res[0], res[1]
    s_thru = list(res[2:2 + nt])
    d_thru = s_thru if inplace else list(res[2 + nt:2 + 2 * nt])
    return send_sems, recv_sems, s_thru, d_thru, res[-1]


def _ici_wait(send_sems, recv_sems, srcs, dsts, after, send_view, recv_view, name):
    nt = len(srcs)
    inplace = dsts is None
    nbuf = nt if inplace else 2 * nt

    def body(*refs):
        send_ref, recv_ref = refs[nbuf], refs[nbuf + 1]
        s_out = refs[nbuf + 3:nbuf + 3 + nt]
        d_out = s_out if inplace else refs[nbuf + 3 + nt:nbuf + 3 + 2 * nt]
        for cp in _ici_copies(s_out, d_out, send_ref, recv_ref, send_view, recv_view):
            cp.wait_send()
            cp.wait_recv()

    bufs = list(srcs) + ([] if inplace else list(dsts))
    res = pl.pallas_call(
        body, name=name,
        out_shape=tuple(pltpu.HBM(b.shape, b.dtype) for b in bufs),
        in_specs=[_HBM] * nbuf + [_SEM, _SEM, pl.BlockSpec(memory_space=pl.ANY)],
        out_specs=tuple([_HBM] * nbuf),
        input_output_aliases={i: i for i in range(nbuf)},
        compiler_params=pltpu.CompilerParams(has_side_effects=_EFFECT),
    )(*bufs, send_sems, recv_sems, after)
    return list(res[:nt]) if inplace else (list(res[:nt]), list(res[nt:]))


def _w_half(buf, chip, c):
    r = buf.shape[1] // 2
    return buf.at[2 * chip[0] + chip[1], pl.ds(c * r, r)]


def _ag_send_view(buf, chip, j, me, c):
    return _w_half(buf, me, c)


def _ag_recv_view(buf, chip, j, me, c):
    return _w_half(buf, me, c)


def _rs_send_view(buf, chip, j, me, c):
    return buf.at[2 * chip[0] + chip[1]]


def _rs_recv_view(buf, chip, j, me, c):
    return buf.at[j]


def _ag_forward(bufs, name):
    nt = len(bufs)
    hom = [pl.BlockSpec(memory_space=pl.ANY)] * nt

    def body(*refs):
        outs = refs[nt:2 * nt]
        send_sems, recv_sems = refs[2 * nt:]
        x, y, c = _coords()
        sibling = (x, y, 1 - c)
        chips = _other_chips(x, y)

        def copy(t, j, hc):
            blk = _w_half(outs[t], chips[j], hc)
            return pltpu.make_async_remote_copy(
                src_ref=blk, dst_ref=blk, send_sem=send_sems.at[t, j], recv_sem=recv_sems.at[t, j],
                device_id=sibling, device_id_type=MESH)

        started = [copy(t, j, c) for t in range(nt) for j in range(3)]
        for cp in started:
            cp.start()
        for t in range(nt):
            for j in range(3):
                copy(t, j, 1 - c).wait_recv()
        for cp in started:
            cp.wait_send()

    return pl.pallas_call(
        body, name=name,
        out_shape=[_sds(b.shape, b.dtype) for b in bufs],
        in_specs=hom, out_specs=hom,
        input_output_aliases={t: t for t in range(nt)},
        scratch_shapes=[pltpu.SemaphoreType.DMA((nt, 3)), pltpu.SemaphoreType.DMA((nt, 3))],
    )(*bufs)


def _rs_swap_begin(grads, after, tag):
    land = [lax.empty((N_SHARD, g.shape[1] // 2, g.shape[2]), g.dtype) for g in grads]
    send_sems, recv_sems, s_thru, d_thru, token = _ici_start(grads, land, after, None, None, name="rs_swapgo_" + tag)
    return dict(sems=(send_sems, recv_sems), grads=s_thru, land=d_thru, tag=tag), token


def _rs_scatter_begin(swap, after):
    tag = swap["tag"]
    x, y, c = _coords()
    core = jnp.reshape(c, (1,)).astype(jnp.int32)
    chip = jnp.reshape(2 * x + y, (1,)).astype(jnp.int32)
    grads, recv = _ici_wait(*swap["sems"], swap["grads"], swap["land"], after, None, None, name="rs_swapend_" + tag)
    sums, owns = [], []
    for t, (g, rv) in enumerate(zip(grads, recv)):
        r = g.shape[1] // 2
        sb, own = _pair_sum(g.reshape(N_SHARD, 2, r, g.shape[2]), rv, core, chip, name=f"rs_pair_{tag}_{t}")
        sums.append(sb)
        owns.append(own)
    land = [lax.empty((3,) + s.shape[1:], s.dtype) for s in sums]
    send_sems, recv_sems, s_thru, d_thru, token = _ici_start(
        sums, land, after, _rs_send_view, _rs_recv_view, name="rs_start_" + tag)
    return dict(sems=(send_sems, recv_sems), sums=s_thru, land=d_thru, owns=owns, core=core, tag=tag), token


def _rs_end(state, after):
    tag = state["tag"]
    _, got = _ici_wait(*state["sems"], state["sums"], state["land"], after, _rs_send_view, _rs_recv_view,
                       name="rs_wait_" + tag)
    tots = [_chip_sum(o, gt, state["core"], name=f"rs_chip_{tag}_{t}")
            for t, (o, gt) in enumerate(zip(state["owns"], got))]
    full = _join_halves(tots, name="rs_join_" + tag)
    return [f.reshape(2 * f.shape[1], f.shape[2]) for f in full]


def _rope_lane_table():
    d = jnp.arange(128) % HEAD
    inv_freq = ROPE_THETA ** (-jnp.arange(0, ROT, 2, dtype=F32) / ROT)
    rot = d < ROT
    rows = [jnp.where(rot, inv_freq[d % (ROT // 2)], 0.0), rot.astype(F32),
            (d < ROT // 2).astype(F32), jnp.logical_and(d >= ROT // 2, rot).astype(F32)]
    return jnp.concatenate([jnp.stack(rows), jnp.zeros((4, 128), F32)], axis=0)


def _pad8(rows):
    return jnp.concatenate([rows, jnp.zeros((8 - rows.shape[0], rows.shape[1]), F32)], axis=0)


def kernel(x, c, positions, ada_w, ada_b, w_in, b_in, sinks, pool_w, pool_scale, w_out, w_gate, w_up, w_down, g_pre_mix, g_post_mix, g_pre_ffn, g_post_ffn, loss_target, m_ada_w, m_ada_b, m_w_in, m_b_in, m_sinks, m_pool_w, m_pool_scale, m_w_out, m_w_gate, m_w_up, m_w_down, m_g_pre_mix, m_g_post_mix, m_g_pre_ffn, m_g_post_ffn, v_ada_w, v_ada_b, v_w_in, v_b_in, v_sinks, v_pool_w, v_pool_scale, v_w_out, v_w_gate, v_w_up, v_w_down, v_g_pre_mix, v_g_post_mix, v_g_pre_ffn, v_g_post_ffn):
    T = x.shape[1]
    n_layers = ada_w.shape[0]
    ax, ay, ac = _coords()
    my_dev = 4 * ax + 2 * ay + ac
    my_chip = 2 * ax + ay
    x0 = x.reshape(T, D_MODEL)
    target = loss_target.reshape(T, D_MODEL)

    c_all = _allgather8(c.reshape(8, 128), name="ag_c").reshape(N_DEV, D_MODEL)
    ada_b_sh = lax.dynamic_slice_in_dim(ada_b, my_chip * ADA_SH, ADA_SH, axis=1).reshape(n_layers, 1, ADA_SH)
    mod_part = _mod_fwd(c_all, ada_w, ada_b_sh)
    mod_all = _allgather8(mod_part.reshape(n_layers * 8, ADA_SH), name="ag_mod")
    mod_all = mod_all.reshape(N_DEV, n_layers, 8, ADA_SH)[0::2]
    mod_mine = lax.dynamic_index_in_dim(mod_all, my_dev, axis=2, keepdims=False)
    mod = jnp.transpose(mod_mine, (1, 0, 2)).reshape(n_layers, 6, D_MODEL)

    pos_b = jnp.broadcast_to(positions.reshape(T, 1), (T, 128))
    rc, rs1, rs2 = _rope_tables(pos_b, _rope_lane_table())

    chip1 = jnp.reshape(my_chip, (1,)).astype(jnp.int32)

    def tr(t):
        return jnp.transpose(t, (0, 2, 1))

    w_in_t, w_gate_t, w_up_t = tr(w_in), tr(w_gate), tr(w_up)

    def cast_layer(l):
        return [_cast_slot(w[l], chip1, name=f"cast_{nm}{l}")
                for nm, w in (("w_in", w_in_t), ("w_out", w_out), ("w_gate", w_gate_t), ("w_up", w_up_t),
                              ("w_down", w_down))]

    def as_operands(bufs):
        gin, gout, gg, gu, gd = bufs
        return (gin.reshape(IN_W, D_MODEL), gout.reshape(D_MODEL, D_MODEL), gg.reshape(D_FF, D_MODEL),
                gu.reshape(D_FF, D_MODEL), gd.reshape(D_FF, D_MODEL))

    bufs0 = cast_layer(0)
    win0 = _allgather_weights(bufs0[:1], name="ag_w0_in")
    rest_send, rest_recv, rest_bufs, _, ag_token = _ici_start(
        bufs0[1:], None, win0[0], _ag_send_view, _ag_recv_view, name="ag_start_0")
    weights = [None] * n_layers

    saved = []
    xl = x0
    for l in range(n_layers):
        mod8 = _pad8(mod[l])
        if l + 1 < n_layers:
            ag_send, ag_recv, ag_bufs, _, ag_token = _ici_start(
                cast_layer(l + 1), None, ag_token, _ag_send_view, _ag_recv_view, name=f"ag_start_{l + 1}")
        if l == 0 or l + 1 < n_layers:
            mod8 = mod8 + ag_token[0, 0]
        g8 = _pad8(jnp.stack([g_pre_mix[l], g_post_mix[l], g_pre_ffn[l], g_post_ffn[l]]))
        sink_b = jnp.broadcast_to(sinks[l][:, None], (N_HEADS, 128))
        psc = pool_scale[l].reshape(1, POOL_W)
        win = win0[0].reshape(IN_W, D_MODEL) if l == 0 else weights[l][0]
        h, q, k, v, u = _fwd_in(xl, mod8, g8, win, b_in[l].reshape(1, IN_W), rc, rs1, rs2)
        attn, lse = _attn_fwd(q, k, v, sink_b)
        pool, pooled = _pool_fwd(u, pool_w[l], psc)
        if l == 0:
            arrived = _ici_wait(rest_send, rest_recv, rest_bufs, None, pool, _ag_send_view, _ag_recv_view,
                                name="ag_wait_0")
            weights[0] = as_operands(win0 + _ag_forward(arrived, name="ag_fwd_0"))
        win, wout, wg, wu, wd = weights[l]
        if l + 1 < n_layers:
            mix, x1, h2, act, ga, gb, f, x2 = _out_ffn_fwd(attn, pool, xl, wout, mod8, g8, wg, wu, wd)
        else:
            mix, x1, h2, act, ga, gb, f, x2, loss_tile = _out_ffn_fwd(attn, pool, xl, wout, mod8, g8, wg, wu, wd,
                                                                      target=target)
        saved.append(dict(x=xl, h=h, q=q, k=k, v=v, lse=lse, attn=attn, pool=pool, pooled=pooled, mix=mix,
                          x1=x1, h2=h2, act=act, ga=ga, gb=gb, f=f, mod8=mod8, g8=g8, sink_b=sink_b, psc=psc))
        xl = x2
        if l + 1 < n_layers:
            arrived = _ici_wait(ag_send, ag_recv, ag_bufs, None, x2, _ag_send_view, _ag_recv_view,
                                name=f"ag_wait_{l + 1}")
            weights[l + 1] = as_operands(_ag_forward(arrived, name=f"ag_fwd_{l + 1}"))

    dy = xl
    loss = lax.psum(loss_tile[0, 0], ("x", "y", "c"))

    small = [None] * n_layers
    dmod_rows = [None] * n_layers
    reduced = [dict() for _ in range(n_layers)]
    att_swap = None
    dx = dy
    for l in reversed(range(n_layers)):
        s = saved[l]
        win, wout, wg, wu, wd = weights[l]
        if att_swap is not None:
            s = dict(s, mod8=s["mod8"] + att_swap[1][0, 0])
        dx1, df, da, db, red_f = _ffn_bwd(dx, s["f"], s["ga"], s["gb"], s["x1"], s["mod8"], s["g8"], wg, wu, wd)
        token = None
        if att_swap is not None:
            att_scatter = _rs_scatter_begin(att_swap[0], dx1)
            token = att_scatter[1]
        ffn_shards = (N_SHARD, FF_SH, D_MODEL)
        g_wd = _wgrad(s["act"], df, name="wgrad_down", after=token).reshape(ffn_shards)
        g_wg = _wgrad(da, s["h2"], name="wgrad_gate").reshape(ffn_shards)
        g_wu = _wgrad(db, s["h2"], name="wgrad_up").reshape(ffn_shards)
        ffn_swap = _rs_swap_begin([g_wg, g_wu, g_wd], dx1, tag=f"{l}f")
        if att_swap is not None:
            got = _rs_end(att_scatter[0], ffn_swap[1])
            reduced[l + 1].update(w_in=got[0], w_out=got[1])
        s = dict(s, mod8=s["mod8"] + ffn_swap[1][0, 0])
        dmix, dattn, dpool, red_c = _mix_bwd(dx1, s["mix"], s["mod8"], s["g8"], wout)
        g_wout = jnp.concatenate([_wgrad(s["attn"], dmix, name="wgrad_out_a"),
                                  _wgrad(s["pool"], dmix, name="wgrad_out_p")], axis=0)
        ffn_scatter = _rs_scatter_begin(ffn_swap[0], dattn)
        dq, dk, dv, dsink = _attn_bwd(s["q"], s["k"], s["v"], s["lse"], dattn, s["sink_b"] + ffn_scatter[1][0:1, :])
        du, g_poolw, dpsc = _pool_bwd(dpool, s["pooled"], pool_w[l], s["psc"])
        dx, dproj, red_d, dbin = _in_bwd(dq, dk, dv, du, rc, rs1, rs2, s["x"], dx1, s["mod8"], s["g8"], win)
        g_win = _wgrad(dproj, s["h"], name="wgrad_in")
        g_win_sh = g_win.reshape(N_SHARD, IN_SH, D_MODEL)
        got = _rs_end(ffn_scatter[0], dproj)
        reduced[l].update(w_gate=got[0], w_up=got[1], w_down=got[2])
        att_swap = _rs_swap_begin([g_win_sh, g_wout.reshape(N_SHARD, OUT_SH, D_MODEL)], dx, tag=f"{l}a")
        dmod_rows[l] = jnp.concatenate([red_d[0], red_d[1], red_c[0], red_f[2], red_f[3], red_f[0]])
        small[l] = jnp.concatenate([red_d[2], red_c[1], red_f[4], red_f[1], dbin[0], dpsc[0], dsink[:, 0],
                                    jnp.zeros((120,), F32), g_poolw.reshape(-1)])
    grad_x = dx.reshape(1, T, D_MODEL)

    per_layer = small[0].shape[0]
    rows_small = n_layers * per_layer // 128
    rows_mod = n_layers * 6 * D_MODEL // 128
    rows_pad = -(rows_small + rows_mod) % 8
    pack = jnp.concatenate(small + dmod_rows + [jnp.zeros((rows_pad * 128,), F32)]).reshape(-1, 128)
    pack = pack + att_swap[1][0, 0]
    gathered = _allgather8(pack, name="ag_small").reshape(N_DEV, pack.shape[0], 128)
    summed = _sum_devices(gathered)
    att_scatter = _rs_scatter_begin(att_swap[0], summed)
    small_sum = summed[:rows_small].reshape(n_layers, per_layer)
    o = 0
    small_g = {}
    for nm, width in (("g_pre_mix", D_MODEL), ("g_post_mix", D_MODEL), ("g_pre_ffn", D_MODEL),
                      ("g_post_ffn", D_MODEL), ("b_in", IN_W), ("pool_scale", POOL_W), ("sinks", 128),
                      ("pool_w", 4 * 128 * 128)):
        small_g[nm] = small_sum[:, o:o + width]
        o += width
    small_g["sinks"] = small_g["sinks"][:, :N_HEADS]
    small_g["pool_w"] = small_g["pool_w"].reshape(n_layers, 4, 128, 128)
    small_g["ada_b"] = summed[rows_small:rows_small + rows_mod].reshape(n_layers, 6 * D_MODEL)
    dmod_all = gathered[:, rows_small:rows_small + rows_mod].reshape(N_DEV, n_layers, N_SHARD, ADA_SH)
    dmod_sh = lax.dynamic_index_in_dim(dmod_all, my_chip, axis=2, keepdims=False)
    g_ada_w = _ada_wgrad(jnp.transpose(c_all), jnp.transpose(dmod_sh, (1, 0, 2)))

    grads = dict(ada_w=g_ada_w, ada_b=small_g["ada_b"], b_in=small_g["b_in"], sinks=small_g["sinks"],
                 pool_w=small_g["pool_w"], pool_scale=small_g["pool_scale"], g_pre_mix=small_g["g_pre_mix"],
                 g_post_mix=small_g["g_post_mix"], g_pre_ffn=small_g["g_pre_ffn"], g_post_ffn=small_g["g_post_ffn"])
    params = dict(ada_w=(ada_w, m_ada_w, v_ada_w), ada_b=(ada_b, m_ada_b, v_ada_b), w_in=(w_in, m_w_in, v_w_in),
                  b_in=(b_in, m_b_in, v_b_in), sinks=(sinks, m_sinks, v_sinks), pool_w=(pool_w, m_pool_w, v_pool_w),
                  pool_scale=(pool_scale, m_pool_scale, v_pool_scale), w_out=(w_out, m_w_out, v_w_out),
                  w_gate=(w_gate, m_w_gate, v_w_gate), w_up=(w_up, m_w_up, v_w_up),
                  w_down=(w_down, m_w_down, v_w_down), g_pre_mix=(g_pre_mix, m_g_pre_mix, v_g_pre_mix),
                  g_post_mix=(g_post_mix, m_g_post_mix, v_g_post_mix), g_pre_ffn=(g_pre_ffn, m_g_pre_ffn, v_g_pre_ffn),
                  g_post_ffn=(g_post_ffn, m_g_post_ffn, v_g_post_ffn))
    names = list(params)
    updates = {nm: _adamw_nd(*params[nm][:1], grads[nm], *params[nm][1:], name="adamw_" + nm) for nm in grads}

    got = _rs_end(att_scatter[0], updates["ada_w"][0])
    reduced[0].update(w_in=got[0], w_out=got[1])
    for nm in ("w_in", "w_out", "w_gate", "w_up", "w_down"):
        g = jnp.stack([reduced[l][nm] for l in range(n_layers)])
        if nm in ("w_in", "w_gate", "w_up"):
            upd = _adamw_nd(tr(params[nm][0]), g, tr(params[nm][1]), tr(params[nm][2]), name="adamw_" + nm)
            grads[nm], updates[nm] = tr(g), [tr(u) for u in upd]
        else:
            grads[nm], updates[nm] = g, _adamw_nd(params[nm][0], g, *params[nm][1:], name="adamw_" + nm)
    return (loss, grad_x, *[grads[nm] for nm in names], *[updates[nm][0] for nm in names],
            *[updates[nm][1] for nm in names], *[updates[nm][2] for nm in names])
```

```python
import functools

import jax
import jax.numpy as jnp
from jax import lax
from jax.experimental import pallas as pl
from jax.experimental.pallas import tpu as pltpu

F32 = jnp.float32
BF16 = jnp.bfloat16
MESH = pl.DeviceIdType.MESH

D_MODEL = 1024
ATTN_W = 512
KV_W = 128
KVD_W = 256
POOL_W = 512
IN_W = 1280
D_FF = 2816
N_SHARD = 4
FF_SH = D_FF // N_SHARD
IN_SH = IN_W // N_SHARD
OUT_SH = D_MODEL // N_SHARD
ADA_SH = 6 * D_MODEL // N_SHARD
HEAD = 64
N_HEADS = 8
GROUP = 4
BLK = 128
POOL_WINDOWS = (2, 4, 8, 16)
HALO = 16
ROT = 16
ROPE_THETA = 500000.0
EPS = 1e-6
NEG_INF = -1e30
N_DEV = 8

ADAM_LR = 0.001
ADAM_B1 = 0.9
ADAM_B2 = 0.999
ADAM_EPS = 1e-08
ADAM_WD = 0.01
ADAM_STEP = 10

VMEM_LIMIT = 48 * 1024 * 1024
FFN_VMEM_LIMIT = 60 * 1024 * 1024
WGRAD_TOKENS = 2048


def _cp(*sem, vmem=VMEM_LIMIT):
    return pltpu.CompilerParams(dimension_semantics=sem, vmem_limit_bytes=vmem)


def _full(shape):
    nd = len(shape)
    return pl.BlockSpec(shape, lambda *_: (0,) * nd)


def _resident(shape):
    nd = len(shape)
    return pl.BlockSpec(shape, lambda *_: (0,) * nd, pipeline_mode=pl.Buffered(1))


def _rows(tm, ncol):
    return pl.BlockSpec((tm, ncol), lambda i: (i, 0))


def _sds(shape, dtype):
    return jax.ShapeDtypeStruct(shape, dtype)


def _nt(a, b):
    return lax.dot_general(a, b, (((1,), (1,)), ((), ())), preferred_element_type=F32)


def _tn(a, b):
    return lax.dot_general(a, b, (((0,), (0,)), ((), ())), preferred_element_type=F32)


def _mm(a, b):
    return jnp.dot(a, b, preferred_element_type=F32)


def _rstd(x):
    return lax.rsqrt(jnp.mean(x * x, axis=-1, keepdims=True) + EPS)


def _colsum(x):
    return jnp.sum(x, axis=0, keepdims=True)


def _norm_gain_bwd(dy, xhat, rstd, gain):
    p = dy * xhat
    dx = rstd * (dy * gain - xhat * jnp.mean(p * gain, axis=-1, keepdims=True))
    return dx, _colsum(p)


def _rope_tables(pos_b, lane_tab):
    T = pos_b.shape[0]
    tm = min(T, 1024)

    def body(pos_ref, tab_ref, c_ref, s1_ref, s2_ref):
        ang = pos_ref[...].astype(F32) * tab_ref[0:1, :]
        cs = jnp.cos(ang)
        sn = jnp.sin(ang)
        m_rot = tab_ref[1:2, :]
        c_ref[...] = cs * m_rot + (1.0 - m_rot)
        s1_ref[...] = -sn * tab_ref[2:3, :]
        s2_ref[...] = sn * tab_ref[3:4, :]

    out = _sds((T, 128), F32)
    return pl.pallas_call(
        body, name="rope_tables", grid=(T // tm,),
        in_specs=[_rows(tm, 128), _full((8, 128))],
        out_specs=[_rows(tm, 128)] * 3, out_shape=[out] * 3,
        compiler_params=_cp("parallel"),
    )(pos_b, lane_tab)


def _rot_fwd(t, c, s1, s2):
    w = t.shape[-1]
    return t * c + pltpu.roll(t, w - 8, 1) * s1 + pltpu.roll(t, 8, 1) * s2


def _rot_bwd(d, c, s1, s2):
    w = d.shape[-1]
    return d * c + pltpu.roll(d * s1, 8, 1) + pltpu.roll(d * s2, w - 8, 1)


def _store_dup(ref, t):
    low = lax.broadcasted_iota(jnp.int32, t.shape, 1) < HEAD
    sw = pltpu.roll(t, HEAD, 1)
    ref[:, 0:128] = jnp.where(low, t, sw).astype(BF16)
    ref[:, 128:256] = jnp.where(low, sw, t).astype(BF16)


def _fold_dup(d):
    low = lax.broadcasted_iota(jnp.int32, (d.shape[0], 128), 1) < HEAD
    d0 = d[:, 0:128]
    d1 = d[:, 128:256]
    return jnp.where(low, d0 + pltpu.roll(d0, HEAD, 1), d1 + pltpu.roll(d1, HEAD, 1))


def _fwd_in(x, mod8, g8, w_in, b_in, rc, rs1, rs2):
    T = x.shape[0]
    tm = min(T, 1024)

    def body(x_ref, mod_ref, g_ref, w_ref, b_ref, c_ref, s1_ref, s2_ref,
             h_ref, q_ref, k_ref, v_ref, u_ref):
        xf = x_ref[...]
        h = (xf * _rstd(xf) * g_ref[0:1, :]) * (1.0 + mod_ref[1:2, :]) + mod_ref[0:1, :]
        hb = h.astype(BF16)
        h_ref[...] = hb
        c = c_ref[...]
        s1 = s1_ref[...]
        s2 = s2_ref[...]
        q = _nt(hb, w_ref[0:ATTN_W, :]) + b_ref[:, 0:ATTN_W]
        q = _rot_fwd(q, jnp.tile(c, (1, 4)), jnp.tile(s1, (1, 4)), jnp.tile(s2, (1, 4)))
        q_ref[...] = (q * (HEAD ** -0.5)).astype(BF16)
        k = _nt(hb, w_ref[ATTN_W:ATTN_W + KV_W, :]) + b_ref[:, ATTN_W:ATTN_W + KV_W]
        _store_dup(k_ref, _rot_fwd(k, c, s1, s2))
        v = _nt(hb, w_ref[ATTN_W + KV_W:ATTN_W + 2 * KV_W, :]) + b_ref[:, ATTN_W + KV_W:ATTN_W + 2 * KV_W]
        _store_dup(v_ref, v)
        u_ref[...] = _nt(hb, w_ref[ATTN_W + 2 * KV_W:IN_W, :]) + b_ref[:, ATTN_W + 2 * KV_W:IN_W]

    return pl.pallas_call(
        body, name="fwd_in", grid=(T // tm,),
        in_specs=[_rows(tm, D_MODEL), _full((8, D_MODEL)), _full((8, D_MODEL)),
                  _resident((IN_W, D_MODEL)), _full((1, IN_W)),
                  _rows(tm, 128), _rows(tm, 128), _rows(tm, 128)],
        out_specs=[_rows(tm, D_MODEL), _rows(tm, ATTN_W), _rows(tm, KVD_W), _rows(tm, KVD_W), _rows(tm, POOL_W)],
        out_shape=[_sds((T, D_MODEL), BF16), _sds((T, ATTN_W), BF16), _sds((T, KVD_W), BF16),
                   _sds((T, KVD_W), BF16), _sds((T, POOL_W), F32)],
        compiler_params=_cp("parallel"),
    )(x, mod8, g8, w_in, b_in, rc, rs1, rs2)


def _band_mask(n):
    kk = lax.broadcasted_iota(jnp.int32, (2 * BLK, BLK), 0)
    qi = lax.broadcasted_iota(jnp.int32, (2 * BLK, BLK), 1)
    first = jnp.where(n > 0, 0, 2 * BLK)
    in_prev = jnp.logical_and(kk < BLK, kk > qi + first)
    in_cur = jnp.logical_and(kk >= BLK, (kk - BLK) <= qi)
    one = jnp.logical_or(in_prev, in_cur)
    return jnp.concatenate([one] * GROUP, axis=1)


def _head_row(ref, j, base=0):
    return jnp.concatenate([ref[base + GROUP * j + r:base + GROUP * j + r + 1, :] for r in range(GROUP)], axis=1)


def _stack_heads(x_ref, j, rows=slice(None)):
    low = lax.broadcasted_iota(jnp.int32, (BLK, 128), 1) < HEAD
    parts = []
    for gp in (2 * j, 2 * j + 1):
        x2 = x_ref[rows, gp * 128:(gp + 1) * 128]
        parts.append(jnp.where(low, x2, jnp.zeros_like(x2)))
        parts.append(jnp.where(low, jnp.zeros_like(x2), x2))
    return jnp.concatenate(parts, axis=0)


def _unstack_heads(o):
    low = lax.broadcasted_iota(jnp.int32, (BLK, 128), 1) < HEAD
    return [jnp.where(low, o[0:BLK], o[BLK:2 * BLK]), jnp.where(low, o[2 * BLK:3 * BLK], o[3 * BLK:4 * BLK])]


def _attn_fwd(q, kd, vd, sink_b):
    T = q.shape[0]
    nb = T // BLK
    assert nb % 2 == 0

    def body(q_ref, kp_ref, kc_ref, vp_ref, vc_ref, sk_ref, o_ref, lse_ref):
        for sub in range(2):
            rows = slice(sub * BLK, (sub + 1) * BLK)
            valid = _band_mask(2 * pl.program_id(0) + sub)
            for j in range(N_HEADS // GROUP):
                lanes = slice(j * 128, (j + 1) * 128)
                k_prev = kp_ref[:, lanes] if sub == 0 else kc_ref[0:BLK, lanes]
                v_prev = vp_ref[:, lanes] if sub == 0 else vc_ref[0:BLK, lanes]
                kcat = jnp.concatenate([k_prev, kc_ref[rows, lanes]], axis=0)
                vcat = jnp.concatenate([v_prev, vc_ref[rows, lanes]], axis=0)
                s = jnp.where(valid, _nt(kcat, _stack_heads(q_ref, j, rows)), NEG_INF)
                sk = _head_row(sk_ref, j)
                m = jnp.maximum(jnp.max(s, axis=0, keepdims=True), sk)
                p = jnp.exp(s - m)
                den = jnp.sum(p, axis=0, keepdims=True) + jnp.exp(sk - m)
                p = p * (1.0 / den)
                o = _tn(p.astype(BF16), vcat)
                o_ref[rows, 2 * j * 128:(2 * j + 2) * 128] = jnp.concatenate(_unstack_heads(o), axis=1).astype(BF16)
                lse = m + jnp.log(den)
                for r in range(GROUP):
                    h = sub * N_HEADS + GROUP * j + r
                    lse_ref[h:h + 1, :] = lse[:, r * 128:(r + 1) * 128]

    prev = lambda i: (jnp.maximum(2 * i - 1, 0), 0)
    cur = lambda i: (i, 0)
    return pl.pallas_call(
        body, name="attn_fwd", grid=(nb // 2,),
        in_specs=[pl.BlockSpec((2 * BLK, ATTN_W), cur),
                  pl.BlockSpec((BLK, KVD_W), prev), pl.BlockSpec((2 * BLK, KVD_W), cur),
                  pl.BlockSpec((BLK, KVD_W), prev), pl.BlockSpec((2 * BLK, KVD_W), cur),
                  _full((8, 128))],
        out_specs=[pl.BlockSpec((2 * BLK, ATTN_W), cur), pl.BlockSpec((2 * N_HEADS, 128), cur)],
        out_shape=[_sds((T, ATTN_W), BF16), _sds((nb * N_HEADS, 128), F32)],
        compiler_params=_cp("parallel"),
    )(q, kd, kd, vd, vd, sink_b)


def _pool_fwd(u, pool_w, pool_scale):
    T = u.shape[0]
    tm = min(T, 1024)

    def body(u_ref, w_ref, sc_ref, out_ref, pooled_ref, halo):
        i = pl.program_id(0)

        @pl.when(i == 0)
        def _():
            halo[...] = jnp.zeros_like(halo)

        ub = u_ref[...]
        ext = jnp.concatenate([halo[...], ub], axis=0)
        halo[...] = ub[tm - HALO:, :]
        tpos = (i * tm + lax.broadcasted_iota(jnp.int32, (tm, 1), 0)).astype(F32)
        for g, w in enumerate(POOL_WINDOWS):
            lanes = slice(g * 128, (g + 1) * 128)
            s = ext[:, lanes]
            sh = 1
            while sh < w:
                s = s + pltpu.roll(s, sh, 0)
                sh *= 2
            cnt = jnp.minimum(tpos + 1.0, float(w))
            pb = (s[HALO:, :] / cnt - ub[:, lanes]).astype(BF16)
            z = _mm(pb, w_ref[g].astype(BF16))
            out_ref[:, lanes] = (z * sc_ref[:, lanes]).astype(BF16)
            pooled_ref[:, lanes] = pb

    return pl.pallas_call(
        body, name="pool_fwd", grid=(T // tm,),
        in_specs=[_rows(tm, POOL_W), _full((4, 128, 128)), _full((1, POOL_W))],
        out_specs=[_rows(tm, POOL_W), _rows(tm, POOL_W)],
        out_shape=[_sds((T, POOL_W), BF16), _sds((T, POOL_W), BF16)],
        scratch_shapes=[pltpu.VMEM((HALO, POOL_W), F32)],
        compiler_params=_cp("arbitrary"),
    )(u, pool_w, pool_scale)


FF_CHUNKS = ((0, 768), (768, 1536), (1536, 2304), (2304, D_FF))


def _out_ffn_fwd(attn, pool, x, w_out, mod8, g8, wg, wu, wd, target=None):
    T = x.shape[0]
    tm = min(T, 256)
    last = target is not None

    def body(*refs):
        a_ref, p_ref, xin_ref, wo_ref, mod_ref, g_ref, wg_ref, wu_ref, wd_ref = refs[:9]
        t_ref = refs[9] if last else None
        mix_ref, x1_ref, h_ref, act_ref, ga_ref, gb_ref, f_ref, x2_ref = refs[9 + last:17 + last]
        mix = _mm(a_ref[...], wo_ref[0:ATTN_W, :]) + _mm(p_ref[...], wo_ref[ATTN_W:, :])
        mix_ref[...] = mix
        xf = xin_ref[...] + mod_ref[2:3, :] * (mix * _rstd(mix) * g_ref[1:2, :])
        x1_ref[...] = xf
        h = (xf * _rstd(xf) * g_ref[2:3, :]) * (1.0 + mod_ref[4:5, :]) + mod_ref[3:4, :]
        hb = h.astype(BF16)
        h_ref[...] = hb
        f = jnp.zeros((tm, D_MODEL), F32)
        for lo, hi in FF_CHUNKS:
            a = _nt(hb, wg_ref[lo:hi, :])
            b = _nt(hb, wu_ref[lo:hi, :])
            sig = jax.nn.sigmoid(a)
            sl = a * sig
            act = (sl * b).astype(BF16)
            act_ref[:, lo:hi] = act
            ga_ref[:, lo:hi] = (b * (sig * (1.0 + a * (1.0 - sig)))).astype(BF16)
            gb_ref[:, lo:hi] = sl.astype(BF16)
            f = f + _mm(act, wd_ref[lo:hi, :])
        f_ref[...] = f
        x2 = xf + mod_ref[5:6, :] * (f * _rstd(f) * g_ref[3:4, :])
        if not last:
            x2_ref[...] = x2
        else:
            loss_ref = refs[18]

            @pl.when(pl.program_id(0) == 0)
            def _():
                loss_ref[...] = jnp.zeros_like(loss_ref)

            e = x2 - t_ref[...]
            x2_ref[...] = e * (1.0 / D_MODEL)
            loss_ref[...] += 0.5 * jnp.sum(jnp.mean(e * e, axis=-1, keepdims=True), axis=0, keepdims=True)

    act_shape = _sds((T, D_FF), BF16)
    wide = _sds((T, D_MODEL), F32)
    weights = [_resident((D_FF, D_MODEL))] * 3
    return pl.pallas_call(
        body, name="out_ffn_fwd_loss" if last else "out_ffn_fwd", grid=(T // tm,),
        in_specs=[_rows(tm, ATTN_W), _rows(tm, POOL_W), _rows(tm, D_MODEL), _resident((D_MODEL, D_MODEL)),
                  _full((8, D_MODEL)), _full((8, D_MODEL)), *weights]
        + ([_rows(tm, D_MODEL)] if last else []),
        out_specs=[_rows(tm, D_MODEL), _rows(tm, D_MODEL), _rows(tm, D_MODEL), _rows(tm, D_FF), _rows(tm, D_FF),
                   _rows(tm, D_FF), _rows(tm, D_MODEL), _rows(tm, D_MODEL)] + ([_full((8, 128))] if last else []),
        out_shape=[wide, wide, _sds((T, D_MODEL), BF16), act_shape, act_shape, act_shape, wide, wide]
        + ([_sds((8, 128), F32)] if last else []),
        compiler_params=_cp("arbitrary" if last else "parallel", vmem=FFN_VMEM_LIMIT),
    )(attn, pool, x, w_out, mod8, g8, wg, wu, wd, *([target] if last else []))


def _ffn_bwd(dx2, f, ga, gb, x1, mod8, g8, wg, wu, wd):
    T = dx2.shape[0]
    tm = min(T, 256)

    def body(dx_ref, f_ref, ga_ref, gb_ref, x_ref, mod_ref, g_ref, wg_ref, wu_ref, wd_ref,
             dx1_ref, df_ref, da_ref, db_ref, red_ref):
        @pl.when(pl.program_id(0) == 0)
        def _():
            red_ref[...] = jnp.zeros_like(red_ref)

        dx = dx_ref[...]
        fv = f_ref[...]
        rstd = _rstd(fv)
        fhat = fv * rstd
        gpost = g_ref[3:4, :]
        gate = mod_ref[5:6, :]
        df, s_post = _norm_gain_bwd(dx, fhat, rstd, gate * gpost)
        red_ref[0:1, :] += gpost * s_post
        red_ref[1:2, :] += gate * s_post
        dfb = df.astype(BF16)
        df_ref[...] = dfb
        dh = jnp.zeros((tm, D_MODEL), F32)
        for lo, hi in FF_CHUNKS:
            dact = _nt(dfb, wd_ref[lo:hi, :])
            da = (dact * ga_ref[:, lo:hi].astype(F32)).astype(BF16)
            db = (dact * gb_ref[:, lo:hi].astype(F32)).astype(BF16)
            da_ref[:, lo:hi] = da
            db_ref[:, lo:hi] = db
            dh = dh + _mm(da, wg_ref[lo:hi, :]) + _mm(db, wu_ref[lo:hi, :])
        xf = x_ref[...]
        rstd1 = _rstd(xf)
        xhat = xf * rstd1
        gpre = g_ref[2:3, :]
        scale1 = 1.0 + mod_ref[4:5, :]
        dxn, s_pre = _norm_gain_bwd(dh, xhat, rstd1, scale1 * gpre)
        red_ref[2:3, :] += _colsum(dh)
        red_ref[3:4, :] += gpre * s_pre
        red_ref[4:5, :] += scale1 * s_pre
        dx1_ref[...] = dx + dxn

    act_shape = _sds((T, D_FF), BF16)
    return pl.pallas_call(
        body, name="ffn_bwd", grid=(T // tm,),
        in_specs=[_rows(tm, D_MODEL), _rows(tm, D_MODEL), _rows(tm, D_FF), _rows(tm, D_FF), _rows(tm, D_MODEL),
                  _full((8, D_MODEL)), _full((8, D_MODEL)),
                  _resident((D_FF, D_MODEL)), _resident((D_FF, D_MODEL)), _resident((D_FF, D_MODEL))],
        out_specs=[_rows(tm, D_MODEL), _rows(tm, D_MODEL), _rows(tm, D_FF), _rows(tm, D_FF), _full((8, D_MODEL))],
        out_shape=[_sds((T, D_MODEL), F32), _sds((T, D_MODEL), BF16), act_shape, act_shape, _sds((8, D_MODEL), F32)],
        compiler_params=_cp("arbitrary"),
    )(dx2, f, ga, gb, x1, mod8, g8, wg, wu, wd)


def _wgrad(a, b, name, after=None):
    T, K = a.shape
    N = b.shape[1]
    tt = min(T, WGRAD_TOKENS)
    tk = next(c for c in (1408, 640, 512, 256, 128) if K % c == 0)

    def body(a_ref, b_ref, *rest):
        o_ref = rest[-1]

        @pl.when(pl.program_id(1) == 0)
        def _():
            o_ref[...] = jnp.zeros_like(o_ref)

        o_ref[...] += _tn(a_ref[...], b_ref[...])

    extra = [] if after is None else [after]
    return pl.pallas_call(
        body, name=name, grid=(K // tk, T // tt),
        in_specs=[pl.BlockSpec((tt, tk), lambda i, t: (t, i)), pl.BlockSpec((tt, N), lambda i, t: (t, 0))]
        + [pl.BlockSpec(memory_space=pl.ANY)] * len(extra),
        out_specs=pl.BlockSpec((tk, N), lambda i, t: (i, 0)),
        out_shape=_sds((K, N), F32),
        compiler_params=_cp("parallel", "arbitrary"),
    )(a, b, *extra)


def _mix_bwd(dx1, mix, mod8, g8, w_out):
    T = dx1.shape[0]
    tm = min(T, 1024)

    def body(dx_ref, mix_ref, mod_ref, g_ref, w_ref, dmix_ref, da_ref, dp_ref, red_ref):
        @pl.when(pl.program_id(0) == 0)
        def _():
            red_ref[...] = jnp.zeros_like(red_ref)

        dx = dx_ref[...]
        mv = mix_ref[...]
        rstd = _rstd(mv)
        mhat = mv * rstd
        gpost = g_ref[1:2, :]
        gate = mod_ref[2:3, :]
        dm, s_post = _norm_gain_bwd(dx, mhat, rstd, gate * gpost)
        red_ref[0:1, :] += gpost * s_post
        red_ref[1:2, :] += gate * s_post
        dmb = dm.astype(BF16)
        dmix_ref[...] = dmb
        da_ref[...] = _nt(dmb, w_ref[0:ATTN_W, :]).astype(BF16)
        dp_ref[...] = _nt(dmb, w_ref[ATTN_W:, :]).astype(BF16)

    return pl.pallas_call(
        body, name="mix_bwd", grid=(T // tm,),
        in_specs=[_rows(tm, D_MODEL), _rows(tm, D_MODEL), _full((8, D_MODEL)), _full((8, D_MODEL)),
                  _resident((D_MODEL, D_MODEL))],
        out_specs=[_rows(tm, D_MODEL), _rows(tm, ATTN_W), _rows(tm, POOL_W), _full((8, D_MODEL))],
        out_shape=[_sds((T, D_MODEL), BF16), _sds((T, ATTN_W), BF16), _sds((T, POOL_W), BF16),
                   _sds((8, D_MODEL), F32)],
        compiler_params=_cp("arbitrary"),
    )(dx1, mix, mod8, g8, w_out)


def _attn_bwd(q, kd, vd, lse, dattn, sink_b):
    T = q.shape[0]
    nb = T // BLK
    assert nb % 2 == 0
    npair = nb // 2

    def body(q_ref, do_ref, lse_ref, kp_ref, kc_ref, vp_ref, vc_ref, sk_ref,
             dq_ref, dke_ref, dko_ref, dve_ref, dvo_ref, dsk_ref, carry_k, carry_v):
        i = pl.program_id(0)

        @pl.when(i == 0)
        def _():
            carry_k[...] = jnp.zeros_like(carry_k)
            carry_v[...] = jnp.zeros_like(carry_v)
            dsk_ref[...] = jnp.zeros_like(dsk_ref)

        @pl.when(i < npair)
        def _():
            for j in range(N_HEADS // GROUP):
                lanes = slice(j * 128, (j + 1) * 128)
                parts_k, parts_v = [], []
                for sub in range(2):
                    rows = slice(sub * BLK, (sub + 1) * BLK)
                    valid = _band_mask(2 * i + sub)
                    k_prev = kp_ref[:, lanes] if sub == 0 else kc_ref[0:BLK, lanes]
                    v_prev = vp_ref[:, lanes] if sub == 0 else vc_ref[0:BLK, lanes]
                    kcat = jnp.concatenate([k_prev, kc_ref[rows, lanes]], axis=0)
                    vcat = jnp.concatenate([v_prev, vc_ref[rows, lanes]], axis=0)
                    qs = _stack_heads(q_ref, j, rows)
                    dos = _stack_heads(do_ref, j, rows)
                    lse = _head_row(lse_ref, j, sub * N_HEADS)
                    p = jnp.exp(jnp.where(valid, _nt(kcat, qs), NEG_INF) - lse)
                    dp = _nt(vcat, dos)
                    delta = jnp.sum(p * dp, axis=0, keepdims=True)
                    ds = (p * (dp - delta)).astype(BF16)
                    sink_term = jnp.exp(_head_row(sk_ref, j) - lse) * delta
                    for r in range(GROUP):
                        h = GROUP * j + r
                        dsk_ref[h:h + 1, :] += -jnp.sum(sink_term[:, r * 128:(r + 1) * 128], axis=1, keepdims=True)
                    dq_ref[rows, 2 * j * 128:(2 * j + 2) * 128] = jnp.concatenate(
                        _unstack_heads(_tn(ds, kcat)), axis=1)
                    parts_k.append(_mm(ds, qs))
                    parts_v.append(_mm(p.astype(BF16), dos))
                dko_ref[:, lanes] = carry_k[:, lanes] + parts_k[0][0:BLK]
                dvo_ref[:, lanes] = carry_v[:, lanes] + parts_v[0][0:BLK]
                dke_ref[:, lanes] = parts_k[0][BLK:] + parts_k[1][0:BLK]
                dve_ref[:, lanes] = parts_v[0][BLK:] + parts_v[1][0:BLK]
                carry_k[:, lanes] = parts_k[1][BLK:]
                carry_v[:, lanes] = parts_v[1][BLK:]

        @pl.when(i == npair)
        def _():
            dko_ref[...] = carry_k[...]
            dvo_ref[...] = carry_v[...]

    cur = lambda i: (jnp.minimum(i, npair - 1), 0)
    prev = lambda i: (jnp.minimum(jnp.maximum(2 * i - 1, 0), nb - 1), 0)
    odd = lambda i: (jnp.maximum(i - 1, 0), 0)
    half = _sds((npair * BLK, KVD_W), F32)
    return pl.pallas_call(
        body, name="attn_bwd", grid=(npair + 1,),
        in_specs=[pl.BlockSpec((2 * BLK, ATTN_W), cur), pl.BlockSpec((2 * BLK, ATTN_W), cur),
                  pl.BlockSpec((2 * N_HEADS, 128), cur),
                  pl.BlockSpec((BLK, KVD_W), prev), pl.BlockSpec((2 * BLK, KVD_W), cur),
                  pl.BlockSpec((BLK, KVD_W), prev), pl.BlockSpec((2 * BLK, KVD_W), cur),
                  _full((8, 128))],
        out_specs=[pl.BlockSpec((2 * BLK, ATTN_W), cur), pl.BlockSpec((BLK, KVD_W), cur), pl.BlockSpec((BLK, KVD_W), odd),
                   pl.BlockSpec((BLK, KVD_W), cur), pl.BlockSpec((BLK, KVD_W), odd), _full((8, 128))],
        out_shape=[_sds((T, ATTN_W), F32), half, half, half, half, _sds((8, 128), F32)],
        scratch_shapes=[pltpu.VMEM((BLK, KVD_W), F32), pltpu.VMEM((BLK, KVD_W), F32)],
        compiler_params=_cp("arbitrary"),
    )(q, dattn, lse, kd, kd, vd, vd, sink_b)


def _pool_bwd(dpool, pooled, pool_w, pool_scale):
    T = dpool.shape[0]
    tm = min(T, 1024)
    nbk = T // tm
    ext_rows = tm + HALO

    def body(dp_ref, pl_ref, w_ref, sc_ref, du_ref, dw_ref, dsc_ref, halo):
        i = pl.program_id(0)

        @pl.when(i == 0)
        def _():
            halo[...] = jnp.zeros_like(halo)
            dw_ref[...] = jnp.zeros_like(dw_ref)
            dsc_ref[...] = jnp.zeros_like(dsc_ref)

        blk = nbk - 1 - i
        tpos = (blk * tm + lax.broadcasted_iota(jnp.int32, (tm, 1), 0)).astype(F32)
        for g, w in enumerate(POOL_WINDOWS):
            lanes = slice(g * 128, (g + 1) * 128)
            dp = dp_ref[:, lanes].astype(F32)
            pb = pl_ref[:, lanes]
            wg = w_ref[g].astype(BF16)
            z = _mm(pb, wg)
            dsc_ref[0:1, lanes] += _colsum(dp * z)
            dz = (dp * sc_ref[:, lanes]).astype(BF16)
            dw_ref[g] += _tn(pb, dz)
            dpl = _nt(dz, wg)
            e = dpl / jnp.minimum(tpos + 1.0, float(w))
            s = jnp.concatenate([e, halo[:, lanes]], axis=0)
            halo[:, lanes] = e[0:HALO, :]
            sh = 1
            while sh < w:
                s = s + pltpu.roll(s, ext_rows - sh, 0)
                sh *= 2
            du_ref[:, lanes] = s[0:tm, :] - dpl

    rev = lambda i: (nbk - 1 - i, 0)
    return pl.pallas_call(
        body, name="pool_bwd", grid=(nbk,),
        in_specs=[pl.BlockSpec((tm, POOL_W), rev), pl.BlockSpec((tm, POOL_W), rev),
                  _full((4, 128, 128)), _full((1, POOL_W))],
        out_specs=[pl.BlockSpec((tm, POOL_W), rev), _full((4, 128, 128)), _full((8, POOL_W))],
        out_shape=[_sds((T, POOL_W), F32), _sds((4, 128, 128), F32), _sds((8, POOL_W), F32)],
        scratch_shapes=[pltpu.VMEM((HALO, POOL_W), F32)],
        compiler_params=_cp("arbitrary"),
    )(dpool, pooled, pool_w, pool_scale)


def _interleave_blocks(even, odd):
    parts = []
    for b in range(even.shape[0] // BLK):
        parts += [even[b * BLK:(b + 1) * BLK], odd[b * BLK:(b + 1) * BLK]]
    return jnp.concatenate(parts, axis=0)


def _in_bwd(dq, dk_eo, dv_eo, du, rc, rs1, rs2, x, dx1, mod8, g8, w_in):
    T = x.shape[0]
    tm = min(T, 512)

    def body(dq_ref, dke_ref, dko_ref, dve_ref, dvo_ref, du_ref, c_ref, s1_ref, s2_ref, x_ref, dx1_ref, mod_ref,
             g_ref, w_ref, dx_ref, dproj_ref, red_ref, dbin_ref):
        dk_all = _interleave_blocks(dke_ref[...], dko_ref[...])
        dv_all = _interleave_blocks(dve_ref[...], dvo_ref[...])
        @pl.when(pl.program_id(0) == 0)
        def _():
            red_ref[...] = jnp.zeros_like(red_ref)
            dbin_ref[...] = jnp.zeros_like(dbin_ref)

        c = c_ref[...]
        s1 = s1_ref[...]
        s2 = s2_ref[...]
        dqp = _rot_bwd(dq_ref[...] * (HEAD ** -0.5), jnp.tile(c, (1, 4)), jnp.tile(s1, (1, 4)), jnp.tile(s2, (1, 4)))
        dkp = _rot_bwd(_fold_dup(dk_all), c, s1, s2)
        pieces = ((0, ATTN_W, dqp), (ATTN_W, ATTN_W + KV_W, dkp),
                  (ATTN_W + KV_W, ATTN_W + 2 * KV_W, _fold_dup(dv_all)), (ATTN_W + 2 * KV_W, IN_W, du_ref[...]))
        dh = jnp.zeros((tm, D_MODEL), F32)
        for lo, hi, val in pieces:
            dbin_ref[0:1, lo:hi] += _colsum(val)
            vb = val.astype(BF16)
            dproj_ref[:, lo:hi] = vb
            dh = dh + _mm(vb, w_ref[lo:hi, :])
        xf = x_ref[...]
        rstd = _rstd(xf)
        xhat = xf * rstd
        gpre = g_ref[0:1, :]
        scale1 = 1.0 + mod_ref[1:2, :]
        dxn, s_pre = _norm_gain_bwd(dh, xhat, rstd, scale1 * gpre)
        red_ref[0:1, :] += _colsum(dh)
        red_ref[1:2, :] += gpre * s_pre
        red_ref[2:3, :] += scale1 * s_pre
        dx_ref[...] = dx1_ref[...] + dxn

    return pl.pallas_call(
        body, name="in_bwd", grid=(T // tm,),
        in_specs=[_rows(tm, ATTN_W), *[_rows(tm // 2, KVD_W)] * 4, _rows(tm, POOL_W),
                  _rows(tm, 128), _rows(tm, 128), _rows(tm, 128), _rows(tm, D_MODEL), _rows(tm, D_MODEL),
                  _full((8, D_MODEL)), _full((8, D_MODEL)), _resident((IN_W, D_MODEL))],
        out_specs=[_rows(tm, D_MODEL), _rows(tm, IN_W), _full((8, D_MODEL)), _full((8, IN_W))],
        out_shape=[_sds((T, D_MODEL), F32), _sds((T, IN_W), BF16), _sds((8, D_MODEL), F32), _sds((8, IN_W), F32)],
        compiler_params=_cp("arbitrary"),
    )(dq, *dk_eo, *dv_eo, du, rc, rs1, rs2, x, dx1, mod8, g8, w_in)


def _mod_fwd(c_all, ada_w, ada_b_sh):
    tn = 512

    def body(c_ref, w_ref, b_ref, o_ref):
        cv = c_ref[...]
        ca = (cv * jax.nn.sigmoid(cv)).astype(BF16)
        o_ref[...] = _mm(ca, w_ref[...].astype(BF16)) + b_ref[...]

    return pl.pallas_call(
        body, name="mod_fwd", grid=(2, ADA_SH // tn),
        in_specs=[_full((8, D_MODEL)), pl.BlockSpec((None, D_MODEL, tn), lambda l, j: (l, 0, j)),
                  pl.BlockSpec((None, 1, tn), lambda l, j: (l, 0, j))],
        out_specs=pl.BlockSpec((None, 8, tn), lambda l, j: (l, 0, j)),
        out_shape=_sds((2, 8, ADA_SH), F32),
        compiler_params=_cp("parallel", "parallel"),
    )(c_all, ada_w, ada_b_sh)


def _ada_wgrad(c_all_t, dmod_sh):
    tn = 512

    def body(c_ref, d_ref, o_ref):
        cv = c_ref[...]
        ca = cv * jax.nn.sigmoid(cv)
        o_ref[...] = jnp.dot(ca, d_ref[...], preferred_element_type=F32, precision=lax.Precision.HIGHEST)

    return pl.pallas_call(
        body, name="ada_wgrad", grid=(2, ADA_SH // tn),
        in_specs=[_full((D_MODEL, 8)), pl.BlockSpec((None, 8, tn), lambda l, j: (l, 0, j))],
        out_specs=pl.BlockSpec((None, D_MODEL, tn), lambda l, j: (l, 0, j)),
        out_shape=_sds((2, D_MODEL, ADA_SH), F32),
        compiler_params=_cp("parallel", "parallel"),
    )(c_all_t, dmod_sh)


def _sum_devices(g):
    R = g.shape[1]

    def body(g_ref, o_ref):
        acc = g_ref[0]
        for d in range(1, N_DEV):
            acc = acc + g_ref[d]
        o_ref[...] = acc

    return pl.pallas_call(
        body, name="sum_devices", grid=(1,),
        in_specs=[_full((N_DEV, R, 128))], out_specs=_full((R, 128)), out_shape=_sds((R, 128), F32),
        compiler_params=_cp("arbitrary"),
    )(g)


def _adamw(w, g, m, v, name):
    R, C = w.shape
    tr = R
    for cand in (256, 128, 64, 32, 16, 8):
        if R % cand == 0 and cand * C * 4 <= 2 * 1024 * 1024:
            tr = cand
            break

    def body(w_ref, g_ref, m_ref, v_ref, d_ref, nm_ref, nv_ref):
        gv = g_ref[...]
        mn = ADAM_B1 * m_ref[...] + (1.0 - ADAM_B1) * gv
        vn = ADAM_B2 * v_ref[...] + (1.0 - ADAM_B2) * (gv * gv)
        m_hat = mn / (1.0 - ADAM_B1 ** ADAM_STEP)
        v_hat = vn / (1.0 - ADAM_B2 ** ADAM_STEP)
        d_ref[...] = -ADAM_LR * (m_hat / (jnp.sqrt(v_hat) + ADAM_EPS) + ADAM_WD * w_ref[...])
        nm_ref[...] = mn
        nv_ref[...] = vn

    spec = pl.BlockSpec((tr, C), lambda i: (i, 0))
    out = _sds((R, C), F32)
    return pl.pallas_call(
        body, name=name, grid=(R // tr,),
        in_specs=[spec] * 4, out_specs=[spec] * 3, out_shape=[out] * 3,
        compiler_params=_cp("parallel"),
    )(w, g, m, v)


def _adamw_nd(w, g, m, v, name):
    shape = w.shape
    if w.ndim == 2 and shape[1] < 128:
        view = (1, shape[0] * shape[1])
    else:
        view = (-1, shape[-1])
    outs = _adamw(*[t.reshape(view) for t in (w, g, m, v)], name=name)
    return [o.reshape(shape) for o in outs]


def _coords():
    return lax.axis_index("x"), lax.axis_index("y"), lax.axis_index("c")


def _other_chips(x, y):
    return [(1 - x, y), (x, 1 - y), (1 - x, 1 - y)]


def _allgather8(blk, name):
    m_per, n = blk.shape

    def body(x_ref, out_ref, send_sems, recv_sems, local_sem):
        x, y, c = _coords()
        me, sibling = (x, y, c), (x, y, 1 - c)
        chips = _other_chips(x, y)

        def rows(px, py, pc):
            return out_ref.at[pl.ds((4 * px + 2 * py + pc) * m_per, m_per), :]

        def copy(k, block, to, src=None):
            return pltpu.make_async_remote_copy(
                src_ref=rows(*block) if src is None else src, dst_ref=rows(*block),
                send_sem=send_sems.at[k], recv_sem=recv_sems.at[k], device_id=to, device_id_type=MESH)

        mine = pltpu.make_async_copy(x_ref, rows(*me), local_sem)
        mine.start()
        first = [copy(0, me, sibling, src=x_ref)]
        first += [copy(1 + j, me, (*chip, c), src=x_ref) for j, chip in enumerate(chips)]
        for cp in first:
            cp.start()
        passed = [copy(4 + j, (*chip, c), sibling) for j, chip in enumerate(chips)]
        for j, chip in enumerate(chips):
            copy(1 + j, (*chip, c), me).wait_recv()
            passed[j].start()
        copy(0, sibling, me).wait_recv()
        for j, chip in enumerate(chips):
            copy(4 + j, (*chip, 1 - c), me).wait_recv()
        for cp in first + passed:
            cp.wait_send()
        mine.wait()

    return pl.pallas_call(
        body, name=name,
        out_shape=_sds((N_DEV * m_per, n), blk.dtype),
        in_specs=[pl.BlockSpec(memory_space=pltpu.VMEM)],
        out_specs=pl.BlockSpec(memory_space=pltpu.VMEM),
        scratch_shapes=[pltpu.SemaphoreType.DMA((7,)), pltpu.SemaphoreType.DMA((7,)), pltpu.SemaphoreType.DMA],
        compiler_params=pltpu.CompilerParams(vmem_limit_bytes=VMEM_LIMIT),
    )(blk)


def _row_tile(r, n):
    for cand in range(r, 15, -16):
        if r % cand == 0 and cand % 16 == 0 and cand * n * 4 <= 2 * 1024 * 1024:
            return cand
    return r


def _cast_slot(w, chip, name):
    r, n = w.shape
    tr = _row_tile(r, n)

    def body(chip_ref, w_ref, o_ref):
        o_ref[...] = w_ref[...].astype(BF16)

    grid_spec = pltpu.PrefetchScalarGridSpec(
        num_scalar_prefetch=1, grid=(r // tr,),
        in_specs=[pl.BlockSpec((tr, n), lambda i, ch: (i, 0))],
        out_specs=pl.BlockSpec((None, tr, n), lambda i, ch: (ch[0], i, 0)))
    return pl.pallas_call(
        body, name=name, grid_spec=grid_spec, out_shape=_sds((N_SHARD, r, n), BF16),
        compiler_params=_cp("arbitrary"),
    )(chip, w)


def _allgather_weights(bufs, name):
    nt = len(bufs)
    hom = [pl.BlockSpec(memory_space=pl.ANY)] * nt

    def body(*refs):
        outs = refs[nt:2 * nt]
        send_sems, recv_sems = refs[2 * nt:]
        x, y, c = _coords()
        sibling = (x, y, 1 - c)
        chips = _other_chips(x, y)

        def copy(t, k, block_chip, hc, to):
            r = outs[t].shape[1] // 2
            blk = outs[t].at[2 * block_chip[0] + block_chip[1], pl.ds(hc * r, r)]
            return pltpu.make_async_remote_copy(
                src_ref=blk, dst_ref=blk,
                send_sem=send_sems.at[t, k], recv_sem=recv_sems.at[t, k], device_id=to, device_id_type=MESH)

        started = []
        for t in range(nt):
            for j, chip in enumerate(chips):
                cp = copy(t, j, (x, y), c, (*chip, c))
                cp.start()
                started.append(cp)
        for t in range(nt):
            for j, chip in enumerate(chips):
                copy(t, j, chip, c, sibling).wait_recv()
                fw = copy(t, 3 + j, chip, c, sibling)
                fw.start()
                started.append(fw)
        for t in range(nt):
            for j, chip in enumerate(chips):
                copy(t, 3 + j, chip, 1 - c, sibling).wait_recv()
        for cp in started:
            cp.wait_send()

    return pl.pallas_call(
        body, name=name,
        out_shape=[_sds(b.shape, b.dtype) for b in bufs],
        in_specs=hom, out_specs=hom,
        input_output_aliases={t: t for t in range(nt)},
        scratch_shapes=[pltpu.SemaphoreType.DMA((nt, 6)), pltpu.SemaphoreType.DMA((nt, 6))],
    )(*bufs)


def _join_halves(tots, name):
    nt = len(tots)
    hom = [pl.BlockSpec(memory_space=pl.ANY)] * nt

    def body(*refs):
        outs = refs[nt:2 * nt]
        send_sems, recv_sems = refs[2 * nt:]
        x, y, c = _coords()
        sibling = (x, y, 1 - c)
        cps = []
        for t in range(nt):
            cp = pltpu.make_async_remote_copy(
                src_ref=outs[t].at[c], dst_ref=outs[t].at[c],
                send_sem=send_sems.at[t], recv_sem=recv_sems.at[t], device_id=sibling, device_id_type=MESH)
            cp.start()
            cps.append(cp)
        for t in range(nt):
            pltpu.make_async_remote_copy(
                src_ref=outs[t].at[c], dst_ref=outs[t].at[1 - c],
                send_sem=send_sems.at[t], recv_sem=recv_sems.at[t], device_id=sibling, device_id_type=MESH).wait_recv()
        for cp in cps:
            cp.wait_send()

    return pl.pallas_call(
        body, name=name,
        out_shape=[_sds(t.shape, t.dtype) for t in tots],
        in_specs=hom, out_specs=hom,
        input_output_aliases={t: t for t in range(nt)},
        scratch_shapes=[pltpu.SemaphoreType.DMA((nt,)), pltpu.SemaphoreType.DMA((nt,))],
    )(*tots)


def _pair_sum(g, recv, core, chip, name):
    _, _, r, n = g.shape
    tr = _row_tile(r, n)

    def body(core_ref, chip_ref, g_ref, r_ref, sb_ref, own_ref):
        tot = g_ref[...] + r_ref[...]
        sb_ref[...] = tot.astype(BF16)

        @pl.when(pl.program_id(1) == chip_ref[0])
        def _():
            own_ref[...] = tot

    grid_spec = pltpu.PrefetchScalarGridSpec(
        num_scalar_prefetch=2, grid=(r // tr, N_SHARD),
        in_specs=[pl.BlockSpec((None, None, tr, n), lambda i, s, co, ch: (s, co[0], i, 0)),
                  pl.BlockSpec((None, tr, n), lambda i, s, co, ch: (s, i, 0))],
        out_specs=[pl.BlockSpec((None, tr, n), lambda i, s, co, ch: (s, i, 0)),
                   pl.BlockSpec((tr, n), lambda i, s, co, ch: (i, 0))])
    return pl.pallas_call(
        body, name=name, grid_spec=grid_spec,
        out_shape=[_sds((N_SHARD, r, n), BF16), _sds((r, n), F32)],
        compiler_params=_cp("arbitrary", "arbitrary"),
    )(core, chip, g, recv)


def _chip_sum(own, recv, core, name):
    r, n = own.shape
    tr = _row_tile(r, n)

    def body(core_ref, o_ref, r_ref, t_ref):
        acc = o_ref[...]
        for j in range(3):
            acc = acc + r_ref[j].astype(F32)
        t_ref[...] = acc

    grid_spec = pltpu.PrefetchScalarGridSpec(
        num_scalar_prefetch=1, grid=(r // tr,),
        in_specs=[pl.BlockSpec((tr, n), lambda i, co: (i, 0)), pl.BlockSpec((3, tr, n), lambda i, co: (0, i, 0))],
        out_specs=pl.BlockSpec((None, tr, n), lambda i, co: (co[0], i, 0)))
    return pl.pallas_call(
        body, name=name, grid_spec=grid_spec, out_shape=_sds((2, r, n), F32),
        compiler_params=_cp("arbitrary"),
    )(core, own, recv)


_HBM = pl.BlockSpec(memory_space=pltpu.HBM)
_SEM = pl.BlockSpec(memory_space=pltpu.SEMAPHORE)
_EFFECT = pltpu.SideEffectType.DATAFLOW_SIDE_EFFECTING


def _ici_copies(srcs, dsts, send_sems, recv_sems, send_view, recv_view):
    x, y, c = _coords()
    out = []
    if send_view is None:
        for t in range(len(srcs)):
            r = srcs[t].shape[1] // 2
            out.append(pltpu.make_async_remote_copy(
                src_ref=srcs[t].at[:, pl.ds((1 - c) * r, r)], dst_ref=dsts[t],
                send_sem=send_sems.at[3 * t], recv_sem=recv_sems.at[3 * t],
                device_id=(x, y, 1 - c), device_id_type=MESH))
        return out
    for t in range(len(srcs)):
        for j, chip in enumerate(_other_chips(x, y)):
            out.append(pltpu.make_async_remote_copy(
                src_ref=send_view(srcs[t], chip, j, (x, y), c), dst_ref=recv_view(dsts[t], chip, j, (x, y), c),
                send_sem=send_sems.at[3 * t + j], recv_sem=recv_sems.at[3 * t + j],
                device_id=(*chip, c), device_id_type=MESH))
    return out


def _ici_start(srcs, dsts, after, send_view, recv_view, name):
    nt = len(srcs)
    inplace = dsts is None
    nbuf = nt if inplace else 2 * nt

    def body(*refs):
        send_sems, recv_sems = refs[nbuf + 1], refs[nbuf + 2]
        s_out = refs[nbuf + 3:nbuf + 3 + nt]
        d_out = s_out if inplace else refs[nbuf + 3 + nt:nbuf + 3 + 2 * nt]
        token = refs[-1]
        for cp in _ici_copies(s_out, d_out, send_sems, recv_sems, send_view, recv_view):
            cp.start()
        token[...] = jnp.zeros_like(token)

    bufs = list(srcs) + ([] if inplace else list(dsts))
    res = pl.pallas_call(
        body, name=name,
        out_shape=(pltpu.SemaphoreType.DMA((3 * nt,)), pltpu.SemaphoreType.DMA((3 * nt,)),
                   *[pltpu.HBM(b.shape, b.dtype) for b in bufs], _sds((8, 128), F32)),
        in_specs=[_HBM] * nbuf + [pl.BlockSpec(memory_space=pl.ANY)],
        out_specs=(_SEM, _SEM, *[_HBM] * nbuf, pl.BlockSpec(memory_space=pltpu.VMEM)),
        input_output_aliases={i: 2 + i for i in range(nbuf)},
        compiler_params=pltpu.CompilerParams(has_side_effects=_EFFECT),
    )(*[pltpu.with_memory_space_constraint(b, pltpu.HBM) for b in bufs], after)
    send_sems, recv_sems = res[0], res[1]
    s_thru = list(res[2:2 + nt])
    d_thru = s_thru if inplace else list(res[2 + nt:2 + 2 * nt])
    return send_sems, recv_sems, s_thru, d_thru, res[-1]


def _ici_wait(send_sems, recv_sems, srcs, dsts, after, send_view, recv_view, name):
    nt = len(srcs)
    inplace = dsts is None
    nbuf = nt if inplace else 2 * nt

    def body(*refs):
        send_ref, recv_ref = refs[nbuf], refs[nbuf + 1]
        s_out = refs[nbuf + 3:nbuf + 3 + nt]
        d_out = s_out if inplace else refs[nbuf + 3 + nt:nbuf + 3 + 2 * nt]
        for cp in _ici_copies(s_out, d_out, send_ref, recv_ref, send_view, recv_view):
            cp.wait_send()
            cp.wait_recv()

    bufs = list(srcs) + ([] if inplace else list(dsts))
    res = pl.pallas_call(
        body, name=name,
        out_shape=tuple(pltpu.HBM(b.shape, b.dtype) for b in bufs),
        in_specs=[_HBM] * nbuf + [_SEM, _SEM, pl.BlockSpec(memory_space=pl.ANY)],
        out_specs=tuple([_HBM] * nbuf),
        input_output_aliases={i: i for i in range(nbuf)},
        compiler_params=pltpu.CompilerParams(has_side_effects=_EFFECT),
    )(*bufs, send_sems, recv_sems, after)
    return list(res[:nt]) if inplace else (list(res[:nt]), list(res[nt:]))


def _w_half(buf, chip, c):
    r = buf.shape[1] // 2
    return buf.at[2 * chip[0] + chip[1], pl.ds(c * r, r)]


def _ag_send_view(buf, chip, j, me, c):
    return _w_half(buf, me, c)


def _ag_recv_view(buf, chip, j, me, c):
    return _w_half(buf, me, c)


def _rs_send_view(buf, chip, j, me, c):
    return buf.at[2 * chip[0] + chip[1]]


def _rs_recv_view(buf, chip, j, me, c):
    return buf.at[j]


def _ag_forward(bufs, name):
    nt = len(bufs)
    hom = [pl.BlockSpec(memory_space=pl.ANY)] * nt

    def body(*refs):
        outs = refs[nt:2 * nt]
        send_sems, recv_sems = refs[2 * nt:]
        x, y, c = _coords()
        sibling = (x, y, 1 - c)
        chips = _other_chips(x, y)

        def copy(t, j, hc):
            blk = _w_half(outs[t], chips[j], hc)
            return pltpu.make_async_remote_copy(
                src_ref=blk, dst_ref=blk, send_sem=send_sems.at[t, j], recv_sem=recv_sems.at[t, j],
                device_id=sibling, device_id_type=MESH)

        started = [copy(t, j, c) for t in range(nt) for j in range(3)]
        for cp in started:
            cp.start()
        for t in range(nt):
            for j in range(3):
                copy(t, j, 1 - c).wait_recv()
        for cp in started:
            cp.wait_send()

    return pl.pallas_call(
        body, name=name,
        out_shape=[_sds(b.shape, b.dtype) for b in bufs],
        in_specs=hom, out_specs=hom,
        input_output_aliases={t: t for t in range(nt)},
        scratch_shapes=[pltpu.SemaphoreType.DMA((nt, 3)), pltpu.SemaphoreType.DMA((nt, 3))],
    )(*bufs)


def _rs_swap_begin(grads, after, tag):
    land = [lax.empty((N_SHARD, g.shape[1] // 2, g.shape[2]), g.dtype) for g in grads]
    send_sems, recv_sems, s_thru, d_thru, token = _ici_start(grads, land, after, None, None, name="rs_swapgo_" + tag)
    return dict(sems=(send_sems, recv_sems), grads=s_thru, land=d_thru, tag=tag), token


def _rs_scatter_begin(swap, after):
    tag = swap["tag"]
    x, y, c = _coords()
    core = jnp.reshape(c, (1,)).astype(jnp.int32)
    chip = jnp.reshape(2 * x + y, (1,)).astype(jnp.int32)
    grads, recv = _ici_wait(*swap["sems"], swap["grads"], swap["land"], after, None, None, name="rs_swapend_" + tag)
    sums, owns = [], []
    for t, (g, rv) in enumerate(zip(grads, recv)):
        r = g.shape[1] // 2
        sb, own = _pair_sum(g.reshape(N_SHARD, 2, r, g.shape[2]), rv, core, chip, name=f"rs_pair_{tag}_{t}")
        sums.append(sb)
        owns.append(own)
    land = [lax.empty((3,) + s.shape[1:], s.dtype) for s in sums]
    send_sems, recv_sems, s_thru, d_thru, token = _ici_start(
        sums, land, after, _rs_send_view, _rs_recv_view, name="rs_start_" + tag)
    return dict(sems=(send_sems, recv_sems), sums=s_thru, land=d_thru, owns=owns, core=core, tag=tag), token


def _rs_end(state, after):
    tag = state["tag"]
    _, got = _ici_wait(*state["sems"], state["sums"], state["land"], after, _rs_send_view, _rs_recv_view,
                       name="rs_wait_" + tag)
    tots = [_chip_sum(o, gt, state["core"], name=f"rs_chip_{tag}_{t}")
            for t, (o, gt) in enumerate(zip(state["owns"], got))]
    full = _join_halves(tots, name="rs_join_" + tag)
    return [f.reshape(2 * f.shape[1], f.shape[2]) for f in full]


def _rope_lane_table():
    d = jnp.arange(128) % HEAD
    inv_freq = ROPE_THETA ** (-jnp.arange(0, ROT, 2, dtype=F32) / ROT)
    rot = d < ROT
    rows = [jnp.where(rot, inv_freq[d % (ROT // 2)], 0.0), rot.astype(F32),
            (d < ROT // 2).astype(F32), jnp.logical_and(d >= ROT // 2, rot).astype(F32)]
    return jnp.concatenate([jnp.stack(rows), jnp.zeros((4, 128), F32)], axis=0)


def _pad8(rows):
    return jnp.concatenate([rows, jnp.zeros((8 - rows.shape[0], rows.shape[1]), F32)], axis=0)


def kernel(x, c, positions, ada_w, ada_b, w_in, b_in, sinks, pool_w, pool_scale, w_out, w_gate, w_up, w_down, g_pre_mix, g_post_mix, g_pre_ffn, g_post_ffn, loss_target, m_ada_w, m_ada_b, m_w_in, m_b_in, m_sinks, m_pool_w, m_pool_scale, m_w_out, m_w_gate, m_w_up, m_w_down, m_g_pre_mix, m_g_post_mix, m_g_pre_ffn, m_g_post_ffn, v_ada_w, v_ada_b, v_w_in, v_b_in, v_sinks, v_pool_w, v_pool_scale, v_w_out, v_w_gate, v_w_up, v_w_down, v_g_pre_mix, v_g_post_mix, v_g_pre_ffn, v_g_post_ffn):
    T = x.shape[1]
    n_layers = ada_w.shape[0]
    ax, ay, ac = _coords()
    my_dev = 4 * ax + 2 * ay + ac
    my_chip = 2 * ax + ay
    x0 = x.reshape(T, D_MODEL)
    target = loss_target.reshape(T, D_MODEL)

    c_all = _allgather8(c.reshape(8, 128), name="ag_c").reshape(N_DEV, D_MODEL)
    ada_b_sh = lax.dynamic_slice_in_dim(ada_b, my_chip * ADA_SH, ADA_SH, axis=1).reshape(n_layers, 1, ADA_SH)
    mod_part = _mod_fwd(c_all, ada_w, ada_b_sh)
    mod_all = _allgather8(mod_part.reshape(n_layers * 8, ADA_SH), name="ag_mod")
    mod_all = mod_all.reshape(N_DEV, n_layers, 8, ADA_SH)[0::2]
    mod_mine = lax.dynamic_index_in_dim(mod_all, my_dev, axis=2, keepdims=False)
    mod = jnp.transpose(mod_mine, (1, 0, 2)).reshape(n_layers, 6, D_MODEL)

    pos_b = jnp.broadcast_to(positions.reshape(T, 1), (T, 128))
    rc, rs1, rs2 = _rope_tables(pos_b, _rope_lane_table())

    chip1 = jnp.reshape(my_chip, (1,)).astype(jnp.int32)

    def tr(t):
        return jnp.transpose(t, (0, 2, 1))

    w_in_t, w_gate_t, w_up_t = tr(w_in), tr(w_gate), tr(w_up)

    def cast_layer(l):
        return [_cast_slot(w[l], chip1, name=f"cast_{nm}{l}")
                for nm, w in (("w_in", w_in_t), ("w_out", w_out), ("w_gate", w_gate_t), ("w_up", w_up_t),
                              ("w_down", w_down))]

    def as_operands(bufs):
        gin, gout, gg, gu, gd = bufs
        return (gin.reshape(IN_W, D_MODEL), gout.reshape(D_MODEL, D_MODEL), gg.reshape(D_FF, D_MODEL),
                gu.reshape(D_FF, D_MODEL), gd.reshape(D_FF, D_MODEL))

    bufs0 = cast_layer(0)
    win0 = _allgather_weights(bufs0[:1], name="ag_w0_in")
    rest_send, rest_recv, rest_bufs, _, ag_token = _ici_start(
        bufs0[1:], None, win0[0], _ag_send_view, _ag_recv_view, name="ag_start_0")
    weights = [None] * n_layers

    saved = []
    xl = x0
    for l in range(n_layers):
        mod8 = _pad8(mod[l])
        if l + 1 < n_layers:
            ag_send, ag_recv, ag_bufs, _, ag_token = _ici_start(
                cast_layer(l + 1), None, ag_token, _ag_send_view, _ag_recv_view, name=f"ag_start_{l + 1}")
        if l == 0 or l + 1 < n_layers:
            mod8 = mod8 + ag_token[0, 0]
        g8 = _pad8(jnp.stack([g_pre_mix[l], g_post_mix[l], g_pre_ffn[l], g_post_ffn[l]]))
        sink_b = jnp.broadcast_to(sinks[l][:, None], (N_HEADS, 128))
        psc = pool_scale[l].reshape(1, POOL_W)
        win = win0[0].reshape(IN_W, D_MODEL) if l == 0 else weights[l][0]
        h, q, k, v, u = _fwd_in(xl, mod8, g8, win, b_in[l].reshape(1, IN_W), rc, rs1, rs2)
        attn, lse = _attn_fwd(q, k, v, sink_b)
        pool, pooled = _pool_fwd(u, pool_w[l], psc)
        if l == 0:
            arrived = _ici_wait(rest_send, rest_recv, rest_bufs, None, pool, _ag_send_view, _ag_recv_view,
                                name="ag_wait_0")
            weights[0] = as_operands(win0 + _ag_forward(arrived, name="ag_fwd_0"))
        win, wout, wg, wu, wd = weights[l]
        if l + 1 < n_layers:
            mix, x1, h2, act, ga, gb, f, x2 = _out_ffn_fwd(attn, pool, xl, wout, mod8, g8, wg, wu, wd)
        else:
            mix, x1, h2, act, ga, gb, f, x2, loss_tile = _out_ffn_fwd(attn, pool, xl, wout, mod8, g8, wg, wu, wd,
                                                                      target=target)
        saved.append(dict(x=xl, h=h, q=q, k=k, v=v, lse=lse, attn=attn, pool=pool, pooled=pooled, mix=mix,
                          x1=x1, h2=h2, act=act, ga=ga, gb=gb, f=f, mod8=mod8, g8=g8, sink_b=sink_b, psc=psc))
        xl = x2
        if l + 1 < n_layers:
            arrived = _ici_wait(ag_send, ag_recv, ag_bufs, None, x2, _ag_send_view, _ag_recv_view,
                                name=f"ag_wait_{l + 1}")
            weights[l + 1] = as_operands(_ag_forward(arrived, name=f"ag_fwd_{l + 1}"))

    dy = xl
    loss = lax.psum(loss_tile[0, 0], ("x", "y", "c"))

    small = [None] * n_layers
    dmod_rows = [None] * n_layers
    reduced = [dict() for _ in range(n_layers)]
    att_swap = None
    dx = dy
    for l in reversed(range(n_layers)):
        s = saved[l]
        win, wout, wg, wu, wd = weights[l]
        if att_swap is not None:
            s = dict(s, mod8=s["mod8"] + att_swap[1][0, 0])
        dx1, df, da, db, red_f = _ffn_bwd(dx, s["f"], s["ga"], s["gb"], s["x1"], s["mod8"], s["g8"], wg, wu, wd)
        token = None
        if att_swap is not None:
            att_scatter = _rs_scatter_begin(att_swap[0], dx1)
            token = att_scatter[1]
        ffn_shards = (N_SHARD, FF_SH, D_MODEL)
        g_wd = _wgrad(s["act"], df, name="wgrad_down", after=token).reshape(ffn_shards)
        g_wg = _wgrad(da, s["h2"], name="wgrad_gate").reshape(ffn_shards)
        g_wu = _wgrad(db, s["h2"], name="wgrad_up").reshape(ffn_shards)
        ffn_swap = _rs_swap_begin([g_wg, g_wu, g_wd], dx1, tag=f"{l}f")
        if att_swap is not None:
            got = _rs_end(att_scatter[0], ffn_swap[1])
            reduced[l + 1].update(w_in=got[0], w_out=got[1])
        s = dict(s, mod8=s["mod8"] + ffn_swap[1][0, 0])
        dmix, dattn, dpool, red_c = _mix_bwd(dx1, s["mix"], s["mod8"], s["g8"], wout)
        g_wout = jnp.concatenate([_wgrad(s["attn"], dmix, name="wgrad_out_a"),
                                  _wgrad(s["pool"], dmix, name="wgrad_out_p")], axis=0)
        ffn_scatter = _rs_scatter_begin(ffn_swap[0], dattn)
        dq, dk_e, dk_o, dv_e, dv_o, dsink = _attn_bwd(s["q"], s["k"], s["v"], s["lse"], dattn,
                                                      s["sink_b"] + ffn_scatter[1][0:1, :])
        du, g_poolw, dpsc = _pool_bwd(dpool, s["pooled"], pool_w[l], s["psc"])
        dx, dproj, red_d, dbin = _in_bwd(dq, (dk_e, dk_o), (dv_e, dv_o), du, rc, rs1, rs2, s["x"], dx1, s["mod8"],
                                         s["g8"], win)
        g_win = _wgrad(dproj, s["h"], name="wgrad_in")
        g_win_sh = g_win.reshape(N_SHARD, IN_SH, D_MODEL)
        got = _rs_end(ffn_scatter[0], dproj)
        reduced[l].update(w_gate=got[0], w_up=got[1], w_down=got[2])
        att_swap = _rs_swap_begin([g_win_sh, g_wout.reshape(N_SHARD, OUT_SH, D_MODEL)], dx, tag=f"{l}a")
        dmod_rows[l] = jnp.concatenate([red_d[0], red_d[1], red_c[0], red_f[2], red_f[3], red_f[0]])
        small[l] = jnp.concatenate([red_d[2], red_c[1], red_f[4], red_f[1], dbin[0], dpsc[0], dsink[:, 0],
                                    jnp.zeros((120,), F32), g_poolw.reshape(-1)])
    grad_x = dx.reshape(1, T, D_MODEL)

    per_layer = small[0].shape[0]
    rows_small = n_layers * per_layer // 128
    rows_mod = n_layers * 6 * D_MODEL // 128
    rows_pad = -(rows_small + rows_mod) % 8
    pack = jnp.concatenate(small + dmod_rows + [jnp.zeros((rows_pad * 128,), F32)]).reshape(-1, 128)
    pack = pack + att_swap[1][0, 0]
    gathered = _allgather8(pack, name="ag_small").reshape(N_DEV, pack.shape[0], 128)
    summed = _sum_devices(gathered)
    att_scatter = _rs_scatter_begin(att_swap[0], summed)
    small_sum = summed[:rows_small].reshape(n_layers, per_layer)
    o = 0
    small_g = {}
    for nm, width in (("g_pre_mix", D_MODEL), ("g_post_mix", D_MODEL), ("g_pre_ffn", D_MODEL),
                      ("g_post_ffn", D_MODEL), ("b_in", IN_W), ("pool_scale", POOL_W), ("sinks", 128),
                      ("pool_w", 4 * 128 * 128)):
        small_g[nm] = small_sum[:, o:o + width]
        o += width
    small_g["sinks"] = small_g["sinks"][:, :N_HEADS]
    small_g["pool_w"] = small_g["pool_w"].reshape(n_layers, 4, 128, 128)
    small_g["ada_b"] = summed[rows_small:rows_small + rows_mod].reshape(n_layers, 6 * D_MODEL)
    dmod_all = gathered[:, rows_small:rows_small + rows_mod].reshape(N_DEV, n_layers, N_SHARD, ADA_SH)
    dmod_sh = lax.dynamic_index_in_dim(dmod_all, my_chip, axis=2, keepdims=False)
    g_ada_w = _ada_wgrad(jnp.transpose(c_all), jnp.transpose(dmod_sh, (1, 0, 2)))

    grads = dict(ada_w=g_ada_w, ada_b=small_g["ada_b"], b_in=small_g["b_in"], sinks=small_g["sinks"],
                 pool_w=small_g["pool_w"], pool_scale=small_g["pool_scale"], g_pre_mix=small_g["g_pre_mix"],
                 g_post_mix=small_g["g_post_mix"], g_pre_ffn=small_g["g_pre_ffn"], g_post_ffn=small_g["g_post_ffn"])
    params = dict(ada_w=(ada_w, m_ada_w, v_ada_w), ada_b=(ada_b, m_ada_b, v_ada_b), w_in=(w_in, m_w_in, v_w_in),
                  b_in=(b_in, m_b_in, v_b_in), sinks=(sinks, m_sinks, v_sinks), pool_w=(pool_w, m_pool_w, v_pool_w),
                  pool_scale=(pool_scale, m_pool_scale, v_pool_scale), w_out=(w_out, m_w_out, v_w_out),
                  w_gate=(w_gate, m_w_gate, v_w_gate), w_up=(w_up, m_w_up, v_w_up),
                  w_down=(w_down, m_w_down, v_w_down), g_pre_mix=(g_pre_mix, m_g_pre_mix, v_g_pre_mix),
                  g_post_mix=(g_post_mix, m_g_post_mix, v_g_post_mix), g_pre_ffn=(g_pre_ffn, m_g_pre_ffn, v_g_pre_ffn),
                  g_post_ffn=(g_post_ffn, m_g_post_ffn, v_g_post_ffn))
    names = list(params)
    updates = {nm: _adamw_nd(*params[nm][:1], grads[nm], *params[nm][1:], name="adamw_" + nm) for nm in grads}

    got = _rs_end(att_scatter[0], updates["ada_w"][0])
    reduced[0].update(w_in=got[0], w_out=got[1])
    for nm in ("w_in", "w_out", "w_gate", "w_up", "w_down"):
        g = jnp.stack([reduced[l][nm] for l in range(n_layers)])
        if nm in ("w_in", "w_gate", "w_up"):
            upd = _adamw_nd(tr(params[nm][0]), g, tr(params[nm][1]), tr(params[nm][2]), name="adamw_" + nm)
            grads[nm], updates[nm] = tr(g), [tr(u) for u in upd]
        else:
            grads[nm], updates[nm] = g, _adamw_nd(params[nm][0], g, *params[nm][1:], name="adamw_" + nm)
    return (loss, grad_x, *[grads[nm] for nm in names], *[updates[nm][0] for nm in names],
            *[updates[nm][1] for nm in names], *[updates[nm][2] for nm in names])
```

```python
import functools

import jax
import jax.numpy as jnp
from jax import lax
from jax.experimental import pallas as pl
from jax.experimental.pallas import tpu as pltpu

F32 = jnp.float32
BF16 = jnp.bfloat16
MESH = pl.DeviceIdType.MESH

D_MODEL = 1024
ATTN_W = 512
KV_W = 128
KVD_W = 256
POOL_W = 512
IN_W = 1280
D_FF = 2816
N_SHARD = 4
FF_SH = D_FF // N_SHARD
IN_SH = IN_W // N_SHARD
OUT_SH = D_MODEL // N_SHARD
ADA_SH = 6 * D_MODEL // N_SHARD
HEAD = 64
N_HEADS = 8
GROUP = 4
BLK = 128
POOL_WINDOWS = (2, 4, 8, 16)
HALO = 16
ROT = 16
ROPE_THETA = 500000.0
EPS = 1e-6
NEG_INF = -1e30
N_DEV = 8

ADAM_LR = 0.001
ADAM_B1 = 0.9
ADAM_B2 = 0.999
ADAM_EPS = 1e-08
ADAM_WD = 0.01
ADAM_STEP = 10

VMEM_LIMIT = 48 * 1024 * 1024
FFN_VMEM_LIMIT = 60 * 1024 * 1024
WGRAD_TOKENS = 2048


def _cp(*sem, vmem=VMEM_LIMIT):
    return pltpu.CompilerParams(dimension_semantics=sem, vmem_limit_bytes=vmem)


def _full(shape):
    nd = len(shape)
    return pl.BlockSpec(shape, lambda *_: (0,) * nd)


def _resident(shape):
    nd = len(shape)
    return pl.BlockSpec(shape, lambda *_: (0,) * nd, pipeline_mode=pl.Buffered(1))


def _rows(tm, ncol):
    return pl.BlockSpec((tm, ncol), lambda i: (i, 0))


def _sds(shape, dtype):
    return jax.ShapeDtypeStruct(shape, dtype)


def _nt(a, b):
    return lax.dot_general(a, b, (((1,), (1,)), ((), ())), preferred_element_type=F32)


def _tn(a, b):
    return lax.dot_general(a, b, (((0,), (0,)), ((), ())), preferred_element_type=F32)


def _mm(a, b):
    return jnp.dot(a, b, preferred_element_type=F32)


def _rstd(x):
    return lax.rsqrt(jnp.mean(x * x, axis=-1, keepdims=True) + EPS)


def _colsum(x):
    return jnp.sum(x, axis=0, keepdims=True)


def _norm_gain_bwd(dy, xhat, rstd, gain):
    p = dy * xhat
    dx = rstd * (dy * gain - xhat * jnp.mean(p * gain, axis=-1, keepdims=True))
    return dx, _colsum(p)


def _rope_tables(pos_b, lane_tab):
    T = pos_b.shape[0]
    tm = min(T, 1024)

    def body(pos_ref, tab_ref, c_ref, s1_ref, s2_ref):
        ang = pos_ref[...].astype(F32) * tab_ref[0:1, :]
        cs = jnp.cos(ang)
        sn = jnp.sin(ang)
        m_rot = tab_ref[1:2, :]
        c_ref[...] = cs * m_rot + (1.0 - m_rot)
        s1_ref[...] = -sn * tab_ref[2:3, :]
        s2_ref[...] = sn * tab_ref[3:4, :]

    out = _sds((T, 128), F32)
    return pl.pallas_call(
        body, name="rope_tables", grid=(T // tm,),
        in_specs=[_rows(tm, 128), _full((8, 128))],
        out_specs=[_rows(tm, 128)] * 3, out_shape=[out] * 3,
        compiler_params=_cp("parallel"),
    )(pos_b, lane_tab)


def _rot_fwd(t, c, s1, s2):
    w = t.shape[-1]
    return t * c + pltpu.roll(t, w - 8, 1) * s1 + pltpu.roll(t, 8, 1) * s2


def _rot_bwd(d, c, s1, s2):
    w = d.shape[-1]
    return d * c + pltpu.roll(d * s1, 8, 1) + pltpu.roll(d * s2, w - 8, 1)


def _store_dup(ref, t):
    low = lax.broadcasted_iota(jnp.int32, t.shape, 1) < HEAD
    sw = pltpu.roll(t, HEAD, 1)
    ref[:, 0:128] = jnp.where(low, t, sw).astype(BF16)
    ref[:, 128:256] = jnp.where(low, sw, t).astype(BF16)


def _fold_dup(d):
    low = lax.broadcasted_iota(jnp.int32, (d.shape[0], 128), 1) < HEAD
    d0 = d[:, 0:128]
    d1 = d[:, 128:256]
    return jnp.where(low, d0 + pltpu.roll(d0, HEAD, 1), d1 + pltpu.roll(d1, HEAD, 1))


def _fwd_in(x, mod8, g8, w_in, b_in, rc, rs1, rs2):
    T = x.shape[0]
    tm = min(T, 1024)

    def body(x_ref, mod_ref, g_ref, w_ref, b_ref, c_ref, s1_ref, s2_ref,
             h_ref, q_ref, k_ref, v_ref, u_ref):
        xf = x_ref[...]
        h = (xf * _rstd(xf) * g_ref[0:1, :]) * (1.0 + mod_ref[1:2, :]) + mod_ref[0:1, :]
        hb = h.astype(BF16)
        h_ref[...] = hb
        c = c_ref[...]
        s1 = s1_ref[...]
        s2 = s2_ref[...]
        proj = _nt(hb, w_ref[...]) + b_ref[...]
        q = _rot_fwd(proj[:, 0:ATTN_W], jnp.tile(c, (1, 4)), jnp.tile(s1, (1, 4)), jnp.tile(s2, (1, 4)))
        q_ref[...] = (q * (HEAD ** -0.5)).astype(BF16)
        _store_dup(k_ref, _rot_fwd(proj[:, ATTN_W:ATTN_W + KV_W], c, s1, s2))
        _store_dup(v_ref, proj[:, ATTN_W + KV_W:ATTN_W + 2 * KV_W])
        u_ref[...] = proj[:, ATTN_W + 2 * KV_W:IN_W]

    return pl.pallas_call(
        body, name="fwd_in", grid=(T // tm,),
        in_specs=[_rows(tm, D_MODEL), _full((8, D_MODEL)), _full((8, D_MODEL)),
                  _resident((IN_W, D_MODEL)), _full((1, IN_W)),
                  _rows(tm, 128), _rows(tm, 128), _rows(tm, 128)],
        out_specs=[_rows(tm, D_MODEL), _rows(tm, ATTN_W), _rows(tm, KVD_W), _rows(tm, KVD_W), _rows(tm, POOL_W)],
        out_shape=[_sds((T, D_MODEL), BF16), _sds((T, ATTN_W), BF16), _sds((T, KVD_W), BF16),
                   _sds((T, KVD_W), BF16), _sds((T, POOL_W), F32)],
        compiler_params=_cp("parallel"),
    )(x, mod8, g8, w_in, b_in, rc, rs1, rs2)


def _band_mask(n):
    kk = lax.broadcasted_iota(jnp.int32, (2 * BLK, BLK), 0)
    qi = lax.broadcasted_iota(jnp.int32, (2 * BLK, BLK), 1)
    first = jnp.where(n > 0, 0, 2 * BLK)
    in_prev = jnp.logical_and(kk < BLK, kk > qi + first)
    in_cur = jnp.logical_and(kk >= BLK, (kk - BLK) <= qi)
    one = jnp.logical_or(in_prev, in_cur)
    return jnp.concatenate([one] * GROUP, axis=1)


def _head_row(ref, j, base=0):
    return jnp.concatenate([ref[base + GROUP * j + r:base + GROUP * j + r + 1, :] for r in range(GROUP)], axis=1)


def _stack_heads(x_ref, j, rows=slice(None)):
    low = lax.broadcasted_iota(jnp.int32, (BLK, 128), 1) < HEAD
    parts = []
    for gp in (2 * j, 2 * j + 1):
        x2 = x_ref[rows, gp * 128:(gp + 1) * 128]
        parts.append(jnp.where(low, x2, jnp.zeros_like(x2)))
        parts.append(jnp.where(low, jnp.zeros_like(x2), x2))
    return jnp.concatenate(parts, axis=0)


def _unstack_heads(o):
    low = lax.broadcasted_iota(jnp.int32, (BLK, 128), 1) < HEAD
    return [jnp.where(low, o[0:BLK], o[BLK:2 * BLK]), jnp.where(low, o[2 * BLK:3 * BLK], o[3 * BLK:4 * BLK])]


def _attn_fwd(q, kd, vd, sink_b):
    T = q.shape[0]
    nb = T // BLK
    assert nb % 2 == 0

    def body(q_ref, kp_ref, kc_ref, vp_ref, vc_ref, sk_ref, o_ref, lse_ref):
        for sub in range(2):
            rows = slice(sub * BLK, (sub + 1) * BLK)
            valid = _band_mask(2 * pl.program_id(0) + sub)
            for j in range(N_HEADS // GROUP):
                lanes = slice(j * 128, (j + 1) * 128)
                k_prev = kp_ref[:, lanes] if sub == 0 else kc_ref[0:BLK, lanes]
                v_prev = vp_ref[:, lanes] if sub == 0 else vc_ref[0:BLK, lanes]
                kcat = jnp.concatenate([k_prev, kc_ref[rows, lanes]], axis=0)
                vcat = jnp.concatenate([v_prev, vc_ref[rows, lanes]], axis=0)
                s = jnp.where(valid, _nt(kcat, _stack_heads(q_ref, j, rows)), NEG_INF)
                sk = _head_row(sk_ref, j)
                m = jnp.maximum(jnp.max(s, axis=0, keepdims=True), sk)
                p = jnp.exp(s - m)
                den = jnp.sum(p, axis=0, keepdims=True) + jnp.exp(sk - m)
                p = p * (1.0 / den)
                o = _tn(p.astype(BF16), vcat)
                o_ref[rows, 2 * j * 128:(2 * j + 2) * 128] = jnp.concatenate(_unstack_heads(o), axis=1).astype(BF16)
                lse = m + jnp.log(den)
                for r in range(GROUP):
                    h = sub * N_HEADS + GROUP * j + r
                    lse_ref[h:h + 1, :] = lse[:, r * 128:(r + 1) * 128]

    prev = lambda i: (jnp.maximum(2 * i - 1, 0), 0)
    cur = lambda i: (i, 0)
    return pl.pallas_call(
        body, name="attn_fwd", grid=(nb // 2,),
        in_specs=[pl.BlockSpec((2 * BLK, ATTN_W), cur),
                  pl.BlockSpec((BLK, KVD_W), prev), pl.BlockSpec((2 * BLK, KVD_W), cur),
                  pl.BlockSpec((BLK, KVD_W), prev), pl.BlockSpec((2 * BLK, KVD_W), cur),
                  _full((8, 128))],
        out_specs=[pl.BlockSpec((2 * BLK, ATTN_W), cur), pl.BlockSpec((2 * N_HEADS, 128), cur)],
        out_shape=[_sds((T, ATTN_W), BF16), _sds((nb * N_HEADS, 128), F32)],
        compiler_params=_cp("parallel"),
    )(q, kd, kd, vd, vd, sink_b)


def _pool_fwd(u, pool_w, pool_scale):
    T = u.shape[0]
    tm = min(T, 1024)

    def body(u_ref, w_ref, sc_ref, out_ref, pooled_ref, halo):
        i = pl.program_id(0)

        @pl.when(i == 0)
        def _():
            halo[...] = jnp.zeros_like(halo)

        ub = u_ref[...]
        ext = jnp.concatenate([halo[...], ub], axis=0)
        halo[...] = ub[tm - HALO:, :]
        tpos = (i * tm + lax.broadcasted_iota(jnp.int32, (tm, 1), 0)).astype(F32)
        for g, w in enumerate(POOL_WINDOWS):
            lanes = slice(g * 128, (g + 1) * 128)
            s = ext[:, lanes]
            sh = 1
            while sh < w:
                s = s + pltpu.roll(s, sh, 0)
                sh *= 2
            cnt = jnp.minimum(tpos + 1.0, float(w))
            pb = (s[HALO:, :] / cnt - ub[:, lanes]).astype(BF16)
            z = _mm(pb, w_ref[g].astype(BF16))
            out_ref[:, lanes] = (z * sc_ref[:, lanes]).astype(BF16)
            pooled_ref[:, lanes] = pb

    return pl.pallas_call(
        body, name="pool_fwd", grid=(T // tm,),
        in_specs=[_rows(tm, POOL_W), _full((4, 128, 128)), _full((1, POOL_W))],
        out_specs=[_rows(tm, POOL_W), _rows(tm, POOL_W)],
        out_shape=[_sds((T, POOL_W), BF16), _sds((T, POOL_W), BF16)],
        scratch_shapes=[pltpu.VMEM((HALO, POOL_W), F32)],
        compiler_params=_cp("arbitrary"),
    )(u, pool_w, pool_scale)


FF_CHUNKS = ((0, 1024), (1024, 2048), (2048, D_FF))


def _out_ffn_fwd(attn, pool, x, w_out, mod8, g8, wg, wu, wd, target=None):
    T = x.shape[0]
    tm = min(T, 256)
    last = target is not None

    def body(*refs):
        a_ref, p_ref, xin_ref, wo_ref, mod_ref, g_ref, wg_ref, wu_ref, wd_ref = refs[:9]
        t_ref = refs[9] if last else None
        mix_ref, x1_ref, h_ref, act_ref, ga_ref, gb_ref, f_ref, x2_ref = refs[9 + last:17 + last]
        mix = _mm(a_ref[...], wo_ref[0:ATTN_W, :]) + _mm(p_ref[...], wo_ref[ATTN_W:, :])
        mix_ref[...] = mix
        xf = xin_ref[...] + mod_ref[2:3, :] * (mix * _rstd(mix) * g_ref[1:2, :])
        x1_ref[...] = xf
        h = (xf * _rstd(xf) * g_ref[2:3, :]) * (1.0 + mod_ref[4:5, :]) + mod_ref[3:4, :]
        hb = h.astype(BF16)
        h_ref[...] = hb
        f = jnp.zeros((tm, D_MODEL), F32)
        for lo, hi in FF_CHUNKS:
            a = _nt(hb, wg_ref[lo:hi, :])
            b = _nt(hb, wu_ref[lo:hi, :])
            sig = jax.nn.sigmoid(a)
            sl = a * sig
            act = (sl * b).astype(BF16)
            act_ref[:, lo:hi] = act
            ga_ref[:, lo:hi] = (b * (sig * (1.0 + a * (1.0 - sig)))).astype(BF16)
            gb_ref[:, lo:hi] = sl.astype(BF16)
            f = f + _mm(act, wd_ref[lo:hi, :])
        f_ref[...] = f
        x2 = xf + mod_ref[5:6, :] * (f * _rstd(f) * g_ref[3:4, :])
        if not last:
            x2_ref[...] = x2
        else:
            loss_ref = refs[18]

            @pl.when(pl.program_id(0) == 0)
            def _():
                loss_ref[...] = jnp.zeros_like(loss_ref)

            e = x2 - t_ref[...]
            x2_ref[...] = e * (1.0 / D_MODEL)
            loss_ref[...] += 0.5 * jnp.sum(jnp.mean(e * e, axis=-1, keepdims=True), axis=0, keepdims=True)

    act_shape = _sds((T, D_FF), BF16)
    wide = _sds((T, D_MODEL), F32)
    weights = [_resident((D_FF, D_MODEL))] * 3
    return pl.pallas_call(
        body, name="out_ffn_fwd_loss" if last else "out_ffn_fwd", grid=(T // tm,),
        in_specs=[_rows(tm, ATTN_W), _rows(tm, POOL_W), _rows(tm, D_MODEL), _resident((D_MODEL, D_MODEL)),
                  _full((8, D_MODEL)), _full((8, D_MODEL)), *weights]
        + ([_rows(tm, D_MODEL)] if last else []),
        out_specs=[_rows(tm, D_MODEL), _rows(tm, D_MODEL), _rows(tm, D_MODEL), _rows(tm, D_FF), _rows(tm, D_FF),
                   _rows(tm, D_FF), _rows(tm, D_MODEL), _rows(tm, D_MODEL)] + ([_full((8, 128))] if last else []),
        out_shape=[wide, wide, _sds((T, D_MODEL), BF16), act_shape, act_shape, act_shape, wide, wide]
        + ([_sds((8, 128), F32)] if last else []),
        compiler_params=_cp("arbitrary" if last else "parallel", vmem=FFN_VMEM_LIMIT),
    )(attn, pool, x, w_out, mod8, g8, wg, wu, wd, *([target] if last else []))


def _ffn_bwd(dx2, f, ga, gb, x1, mod8, g8, wg, wu, wd):
    T = dx2.shape[0]
    tm = min(T, 256)

    def body(dx_ref, f_ref, ga_ref, gb_ref, x_ref, mod_ref, g_ref, wg_ref, wu_ref, wd_ref,
             dx1_ref, df_ref, da_ref, db_ref, red_ref):
        @pl.when(pl.program_id(0) == 0)
        def _():
            red_ref[...] = jnp.zeros_like(red_ref)

        dx = dx_ref[...]
        fv = f_ref[...]
        rstd = _rstd(fv)
        fhat = fv * rstd
        gpost = g_ref[3:4, :]
        gate = mod_ref[5:6, :]
        df, s_post = _norm_gain_bwd(dx, fhat, rstd, gate * gpost)
        red_ref[0:1, :] += gpost * s_post
        red_ref[1:2, :] += gate * s_post
        dfb = df.astype(BF16)
        df_ref[...] = dfb
        dh = jnp.zeros((tm, D_MODEL), F32)
        for lo, hi in FF_CHUNKS:
            dact = _nt(dfb, wd_ref[lo:hi, :])
            da = (dact * ga_ref[:, lo:hi].astype(F32)).astype(BF16)
            db = (dact * gb_ref[:, lo:hi].astype(F32)).astype(BF16)
            da_ref[:, lo:hi] = da
            db_ref[:, lo:hi] = db
            dh = dh + _mm(da, wg_ref[lo:hi, :]) + _mm(db, wu_ref[lo:hi, :])
        xf = x_ref[...]
        rstd1 = _rstd(xf)
        xhat = xf * rstd1
        gpre = g_ref[2:3, :]
        scale1 = 1.0 + mod_ref[4:5, :]
        dxn, s_pre = _norm_gain_bwd(dh, xhat, rstd1, scale1 * gpre)
        red_ref[2:3, :] += _colsum(dh)
        red_ref[3:4, :] += gpre * s_pre
        red_ref[4:5, :] += scale1 * s_pre
        dx1_ref[...] = dx + dxn

    act_shape = _sds((T, D_FF), BF16)
    return pl.pallas_call(
        body, name="ffn_bwd", grid=(T // tm,),
        in_specs=[_rows(tm, D_MODEL), _rows(tm, D_MODEL), _rows(tm, D_FF), _rows(tm, D_FF), _rows(tm, D_MODEL),
                  _full((8, D_MODEL)), _full((8, D_MODEL)),
                  _resident((D_FF, D_MODEL)), _resident((D_FF, D_MODEL)), _resident((D_FF, D_MODEL))],
        out_specs=[_rows(tm, D_MODEL), _rows(tm, D_MODEL), _rows(tm, D_FF), _rows(tm, D_FF), _full((8, D_MODEL))],
        out_shape=[_sds((T, D_MODEL), F32), _sds((T, D_MODEL), BF16), act_shape, act_shape, _sds((8, D_MODEL), F32)],
        compiler_params=_cp("arbitrary"),
    )(dx2, f, ga, gb, x1, mod8, g8, wg, wu, wd)


def _wgrad(a, b, name, after=None):
    T, K = a.shape
    N = b.shape[1]
    tt = min(T, WGRAD_TOKENS)
    tk = next(c for c in (1408, 640, 512, 256, 128) if K % c == 0)

    def body(a_ref, b_ref, *rest):
        o_ref = rest[-1]

        @pl.when(pl.program_id(1) == 0)
        def _():
            o_ref[...] = jnp.zeros_like(o_ref)

        o_ref[...] += _tn(a_ref[...], b_ref[...])

    extra = [] if after is None else [after]
    return pl.pallas_call(
        body, name=name, grid=(K // tk, T // tt),
        in_specs=[pl.BlockSpec((tt, tk), lambda i, t: (t, i)), pl.BlockSpec((tt, N), lambda i, t: (t, 0))]
        + [pl.BlockSpec(memory_space=pl.ANY)] * len(extra),
        out_specs=pl.BlockSpec((tk, N), lambda i, t: (i, 0)),
        out_shape=_sds((K, N), F32),
        compiler_params=_cp("parallel", "arbitrary"),
    )(a, b, *extra)


def _mix_bwd(dx1, mix, mod8, g8, w_out):
    T = dx1.shape[0]
    tm = min(T, 1024)

    def body(dx_ref, mix_ref, mod_ref, g_ref, w_ref, dmix_ref, da_ref, dp_ref, red_ref):
        @pl.when(pl.program_id(0) == 0)
        def _():
            red_ref[...] = jnp.zeros_like(red_ref)

        dx = dx_ref[...]
        mv = mix_ref[...]
        rstd = _rstd(mv)
        mhat = mv * rstd
        gpost = g_ref[1:2, :]
        gate = mod_ref[2:3, :]
        dm, s_post = _norm_gain_bwd(dx, mhat, rstd, gate * gpost)
        red_ref[0:1, :] += gpost * s_post
        red_ref[1:2, :] += gate * s_post
        dmb = dm.astype(BF16)
        dmix_ref[...] = dmb
        dap = _nt(dmb, w_ref[...])
        da_ref[...] = dap[:, 0:ATTN_W].astype(BF16)
        dp_ref[...] = dap[:, ATTN_W:].astype(BF16)

    return pl.pallas_call(
        body, name="mix_bwd", grid=(T // tm,),
        in_specs=[_rows(tm, D_MODEL), _rows(tm, D_MODEL), _full((8, D_MODEL)), _full((8, D_MODEL)),
                  _resident((D_MODEL, D_MODEL))],
        out_specs=[_rows(tm, D_MODEL), _rows(tm, ATTN_W), _rows(tm, POOL_W), _full((8, D_MODEL))],
        out_shape=[_sds((T, D_MODEL), BF16), _sds((T, ATTN_W), BF16), _sds((T, POOL_W), BF16),
                   _sds((8, D_MODEL), F32)],
        compiler_params=_cp("arbitrary"),
    )(dx1, mix, mod8, g8, w_out)


def _attn_bwd(q, kd, vd, lse, dattn, sink_b):
    T = q.shape[0]
    nb = T // BLK
    assert nb % 2 == 0
    npair = nb // 2

    def body(q_ref, do_ref, lse_ref, kp_ref, kc_ref, vp_ref, vc_ref, sk_ref,
             dq_ref, dke_ref, dko_ref, dve_ref, dvo_ref, dsk_ref, carry_k, carry_v):
        i = pl.program_id(0)

        @pl.when(i == 0)
        def _():
            carry_k[...] = jnp.zeros_like(carry_k)
            carry_v[...] = jnp.zeros_like(carry_v)
            dsk_ref[...] = jnp.zeros_like(dsk_ref)

        @pl.when(i < npair)
        def _():
            for j in range(N_HEADS // GROUP):
                lanes = slice(j * 128, (j + 1) * 128)
                parts_k, parts_v = [], []
                for sub in range(2):
                    rows = slice(sub * BLK, (sub + 1) * BLK)
                    valid = _band_mask(2 * i + sub)
                    k_prev = kp_ref[:, lanes] if sub == 0 else kc_ref[0:BLK, lanes]
                    v_prev = vp_ref[:, lanes] if sub == 0 else vc_ref[0:BLK, lanes]
                    kcat = jnp.concatenate([k_prev, kc_ref[rows, lanes]], axis=0)
                    vcat = jnp.concatenate([v_prev, vc_ref[rows, lanes]], axis=0)
                    qs = _stack_heads(q_ref, j, rows)
                    dos = _stack_heads(do_ref, j, rows)
                    lse = _head_row(lse_ref, j, sub * N_HEADS)
                    p = jnp.exp(jnp.where(valid, _nt(kcat, qs), NEG_INF) - lse)
                    dp = _nt(vcat, dos)
                    delta = jnp.sum(p * dp, axis=0, keepdims=True)
                    ds = (p * (dp - delta)).astype(BF16)
                    sink_term = jnp.exp(_head_row(sk_ref, j) - lse) * delta
                    for r in range(GROUP):
                        h = GROUP * j + r
                        dsk_ref[h:h + 1, :] += -jnp.sum(sink_term[:, r * 128:(r + 1) * 128], axis=1, keepdims=True)
                    dq_ref[rows, 2 * j * 128:(2 * j + 2) * 128] = jnp.concatenate(
                        _unstack_heads(_tn(ds, kcat)), axis=1)
                    parts_k.append(_mm(ds, qs))
                    parts_v.append(_mm(p.astype(BF16), dos))
                dko_ref[:, lanes] = carry_k[:, lanes] + parts_k[0][0:BLK]
                dvo_ref[:, lanes] = carry_v[:, lanes] + parts_v[0][0:BLK]
                dke_ref[:, lanes] = parts_k[0][BLK:] + parts_k[1][0:BLK]
                dve_ref[:, lanes] = parts_v[0][BLK:] + parts_v[1][0:BLK]
                carry_k[:, lanes] = parts_k[1][BLK:]
                carry_v[:, lanes] = parts_v[1][BLK:]

        @pl.when(i == npair)
        def _():
            dko_ref[...] = carry_k[...]
            dvo_ref[...] = carry_v[...]

    cur = lambda i: (jnp.minimum(i, npair - 1), 0)
    prev = lambda i: (jnp.minimum(jnp.maximum(2 * i - 1, 0), nb - 1), 0)
    odd = lambda i: (jnp.maximum(i - 1, 0), 0)
    half = _sds((npair * BLK, KVD_W), F32)
    return pl.pallas_call(
        body, name="attn_bwd", grid=(npair + 1,),
        in_specs=[pl.BlockSpec((2 * BLK, ATTN_W), cur), pl.BlockSpec((2 * BLK, ATTN_W), cur),
                  pl.BlockSpec((2 * N_HEADS, 128), cur),
                  pl.BlockSpec((BLK, KVD_W), prev), pl.BlockSpec((2 * BLK, KVD_W), cur),
                  pl.BlockSpec((BLK, KVD_W), prev), pl.BlockSpec((2 * BLK, KVD_W), cur),
                  _full((8, 128))],
        out_specs=[pl.BlockSpec((2 * BLK, ATTN_W), cur), pl.BlockSpec((BLK, KVD_W), cur), pl.BlockSpec((BLK, KVD_W), odd),
                   pl.BlockSpec((BLK, KVD_W), cur), pl.BlockSpec((BLK, KVD_W), odd), _full((8, 128))],
        out_shape=[_sds((T, ATTN_W), F32), half, half, half, half, _sds((8, 128), F32)],
        scratch_shapes=[pltpu.VMEM((BLK, KVD_W), F32), pltpu.VMEM((BLK, KVD_W), F32)],
        compiler_params=_cp("arbitrary"),
    )(q, dattn, lse, kd, kd, vd, vd, sink_b)


def _pool_bwd(dpool, pooled, pool_w, pool_scale):
    T = dpool.shape[0]
    tm = min(T, 1024)
    nbk = T // tm
    ext_rows = tm + HALO

    def body(dp_ref, pl_ref, w_ref, sc_ref, du_ref, dw_ref, dsc_ref, halo):
        i = pl.program_id(0)

        @pl.when(i == 0)
        def _():
            halo[...] = jnp.zeros_like(halo)
            dw_ref[...] = jnp.zeros_like(dw_ref)
            dsc_ref[...] = jnp.zeros_like(dsc_ref)

        blk = nbk - 1 - i
        tpos = (blk * tm + lax.broadcasted_iota(jnp.int32, (tm, 1), 0)).astype(F32)
        for g, w in enumerate(POOL_WINDOWS):
            lanes = slice(g * 128, (g + 1) * 128)
            dp = dp_ref[:, lanes].astype(F32)
            pb = pl_ref[:, lanes]
            wg = w_ref[g].astype(BF16)
            z = _mm(pb, wg)
            dsc_ref[0:1, lanes] += _colsum(dp * z)
            dz = (dp * sc_ref[:, lanes]).astype(BF16)
            dw_ref[g] += _tn(pb, dz)
            dpl = _nt(dz, wg)
            e = dpl / jnp.minimum(tpos + 1.0, float(w))
            s = jnp.concatenate([e, halo[:, lanes]], axis=0)
            halo[:, lanes] = e[0:HALO, :]
            sh = 1
            while sh < w:
                s = s + pltpu.roll(s, ext_rows - sh, 0)
                sh *= 2
            du_ref[:, lanes] = s[0:tm, :] - dpl

    rev = lambda i: (nbk - 1 - i, 0)
    return pl.pallas_call(
        body, name="pool_bwd", grid=(nbk,),
        in_specs=[pl.BlockSpec((tm, POOL_W), rev), pl.BlockSpec((tm, POOL_W), rev),
                  _full((4, 128, 128)), _full((1, POOL_W))],
        out_specs=[pl.BlockSpec((tm, POOL_W), rev), _full((4, 128, 128)), _full((8, POOL_W))],
        out_shape=[_sds((T, POOL_W), F32), _sds((4, 128, 128), F32), _sds((8, POOL_W), F32)],
        scratch_shapes=[pltpu.VMEM((HALO, POOL_W), F32)],
        compiler_params=_cp("arbitrary"),
    )(dpool, pooled, pool_w, pool_scale)


def _interleave_blocks(even, odd):
    parts = []
    for b in range(even.shape[0] // BLK):
        parts += [even[b * BLK:(b + 1) * BLK], odd[b * BLK:(b + 1) * BLK]]
    return jnp.concatenate(parts, axis=0)


def _in_bwd(dq, dk_eo, dv_eo, du, rc, rs1, rs2, x, dx1, mod8, g8, w_in):
    T = x.shape[0]
    tm = min(T, 512)

    def body(dq_ref, dke_ref, dko_ref, dve_ref, dvo_ref, du_ref, c_ref, s1_ref, s2_ref, x_ref, dx1_ref, mod_ref,
             g_ref, w_ref, dx_ref, dproj_ref, red_ref, dbin_ref):
        dk_all = _interleave_blocks(dke_ref[...], dko_ref[...])
        dv_all = _interleave_blocks(dve_ref[...], dvo_ref[...])
        @pl.when(pl.program_id(0) == 0)
        def _():
            red_ref[...] = jnp.zeros_like(red_ref)
            dbin_ref[...] = jnp.zeros_like(dbin_ref)

        c = c_ref[...]
        s1 = s1_ref[...]
        s2 = s2_ref[...]
        dqp = _rot_bwd(dq_ref[...] * (HEAD ** -0.5), jnp.tile(c, (1, 4)), jnp.tile(s1, (1, 4)), jnp.tile(s2, (1, 4)))
        dkp = _rot_bwd(_fold_dup(dk_all), c, s1, s2)
        pieces = ((0, ATTN_W, dqp), (ATTN_W, ATTN_W + KV_W, dkp),
                  (ATTN_W + KV_W, ATTN_W + 2 * KV_W, _fold_dup(dv_all)), (ATTN_W + 2 * KV_W, IN_W, du_ref[...]))
        for lo, hi, val in pieces:
            dbin_ref[0:1, lo:hi] += _colsum(val)
            dproj_ref[:, lo:hi] = val.astype(BF16)
        dh = _mm(dproj_ref[...], w_ref[...])
        xf = x_ref[...]
        rstd = _rstd(xf)
        xhat = xf * rstd
        gpre = g_ref[0:1, :]
        scale1 = 1.0 + mod_ref[1:2, :]
        dxn, s_pre = _norm_gain_bwd(dh, xhat, rstd, scale1 * gpre)
        red_ref[0:1, :] += _colsum(dh)
        red_ref[1:2, :] += gpre * s_pre
        red_ref[2:3, :] += scale1 * s_pre
        dx_ref[...] = dx1_ref[...] + dxn

    return pl.pallas_call(
        body, name="in_bwd", grid=(T // tm,),
        in_specs=[_rows(tm, ATTN_W), *[_rows(tm // 2, KVD_W)] * 4, _rows(tm, POOL_W),
                  _rows(tm, 128), _rows(tm, 128), _rows(tm, 128), _rows(tm, D_MODEL), _rows(tm, D_MODEL),
                  _full((8, D_MODEL)), _full((8, D_MODEL)), _resident((IN_W, D_MODEL))],
        out_specs=[_rows(tm, D_MODEL), _rows(tm, IN_W), _full((8, D_MODEL)), _full((8, IN_W))],
        out_shape=[_sds((T, D_MODEL), F32), _sds((T, IN_W), BF16), _sds((8, D_MODEL), F32), _sds((8, IN_W), F32)],
        compiler_params=_cp("arbitrary"),
    )(dq, *dk_eo, *dv_eo, du, rc, rs1, rs2, x, dx1, mod8, g8, w_in)


def _mod_fwd(c_all, ada_w, ada_b_sh):
    tn = 512

    def body(c_ref, w_ref, b_ref, o_ref):
        cv = c_ref[...]
        ca = (cv * jax.nn.sigmoid(cv)).astype(BF16)
        o_ref[...] = _mm(ca, w_ref[...].astype(BF16)) + b_ref[...]

    return pl.pallas_call(
        body, name="mod_fwd", grid=(2, ADA_SH // tn),
        in_specs=[_full((8, D_MODEL)), pl.BlockSpec((None, D_MODEL, tn), lambda l, j: (l, 0, j)),
                  pl.BlockSpec((None, 1, tn), lambda l, j: (l, 0, j))],
        out_specs=pl.BlockSpec((None, 8, tn), lambda l, j: (l, 0, j)),
        out_shape=_sds((2, 8, ADA_SH), F32),
        compiler_params=_cp("parallel", "parallel"),
    )(c_all, ada_w, ada_b_sh)


def _ada_wgrad(c_all_t, dmod_sh):
    tn = 512

    def body(c_ref, d_ref, o_ref):
        cv = c_ref[...]
        ca = cv * jax.nn.sigmoid(cv)
        o_ref[...] = jnp.dot(ca, d_ref[...], preferred_element_type=F32, precision=lax.Precision.HIGHEST)

    return pl.pallas_call(
        body, name="ada_wgrad", grid=(2, ADA_SH // tn),
        in_specs=[_full((D_MODEL, 8)), pl.BlockSpec((None, 8, tn), lambda l, j: (l, 0, j))],
        out_specs=pl.BlockSpec((None, D_MODEL, tn), lambda l, j: (l, 0, j)),
        out_shape=_sds((2, D_MODEL, ADA_SH), F32),
        compiler_params=_cp("parallel", "parallel"),
    )(c_all_t, dmod_sh)


def _sum_devices(g):
    R = g.shape[1]

    def body(g_ref, o_ref):
        acc = g_ref[0]
        for d in range(1, N_DEV):
            acc = acc + g_ref[d]
        o_ref[...] = acc

    return pl.pallas_call(
        body, name="sum_devices", grid=(1,),
        in_specs=[_full((N_DEV, R, 128))], out_specs=_full((R, 128)), out_shape=_sds((R, 128), F32),
        compiler_params=_cp("arbitrary"),
    )(g)


def _adamw(w, g, m, v, name):
    R, C = w.shape
    tr = R
    for cand in (256, 128, 64, 32, 16, 8):
        if R % cand == 0 and cand * C * 4 <= 2 * 1024 * 1024:
            tr = cand
            break

    def body(w_ref, g_ref, m_ref, v_ref, d_ref, nm_ref, nv_ref):
        gv = g_ref[...]
        mn = ADAM_B1 * m_ref[...] + (1.0 - ADAM_B1) * gv
        vn = ADAM_B2 * v_ref[...] + (1.0 - ADAM_B2) * (gv * gv)
        m_hat = mn / (1.0 - ADAM_B1 ** ADAM_STEP)
        v_hat = vn / (1.0 - ADAM_B2 ** ADAM_STEP)
        d_ref[...] = -ADAM_LR * (m_hat / (jnp.sqrt(v_hat) + ADAM_EPS) + ADAM_WD * w_ref[...])
        nm_ref[...] = mn
        nv_ref[...] = vn

    spec = pl.BlockSpec((tr, C), lambda i: (i, 0))
    out = _sds((R, C), F32)
    return pl.pallas_call(
        body, name=name, grid=(R // tr,),
        in_specs=[spec] * 4, out_specs=[spec] * 3, out_shape=[out] * 3,
        compiler_params=_cp("parallel"),
    )(w, g, m, v)


def _adamw_nd(w, g, m, v, name):
    shape = w.shape
    if w.ndim == 2 and shape[1] < 128:
        view = (1, shape[0] * shape[1])
    else:
        view = (-1, shape[-1])
    outs = _adamw(*[t.reshape(view) for t in (w, g, m, v)], name=name)
    return [o.reshape(shape) for o in outs]


def _coords():
    return lax.axis_index("x"), lax.axis_index("y"), lax.axis_index("c")


def _other_chips(x, y):
    return [(1 - x, y), (x, 1 - y), (1 - x, 1 - y)]


def _allgather8(blk, name):
    m_per, n = blk.shape

    def body(x_ref, out_ref, send_sems, recv_sems, local_sem):
        x, y, c = _coords()
        me, sibling = (x, y, c), (x, y, 1 - c)
        chips = _other_chips(x, y)

        def rows(px, py, pc):
            return out_ref.at[pl.ds((4 * px + 2 * py + pc) * m_per, m_per), :]

        def copy(k, block, to, src=None):
            return pltpu.make_async_remote_copy(
                src_ref=rows(*block) if src is None else src, dst_ref=rows(*block),
                send_sem=send_sems.at[k], recv_sem=recv_sems.at[k], device_id=to, device_id_type=MESH)

        mine = pltpu.make_async_copy(x_ref, rows(*me), local_sem)
        mine.start()
        first = [copy(0, me, sibling, src=x_ref)]
        first += [copy(1 + j, me, (*chip, c), src=x_ref) for j, chip in enumerate(chips)]
        for cp in first:
            cp.start()
        passed = [copy(4 + j, (*chip, c), sibling) for j, chip in enumerate(chips)]
        for j, chip in enumerate(chips):
            copy(1 + j, (*chip, c), me).wait_recv()
            passed[j].start()
        copy(0, sibling, me).wait_recv()
        for j, chip in enumerate(chips):
            copy(4 + j, (*chip, 1 - c), me).wait_recv()
        for cp in first + passed:
            cp.wait_send()
        mine.wait()

    return pl.pallas_call(
        body, name=name,
        out_shape=_sds((N_DEV * m_per, n), blk.dtype),
        in_specs=[pl.BlockSpec(memory_space=pltpu.VMEM)],
        out_specs=pl.BlockSpec(memory_space=pltpu.VMEM),
        scratch_shapes=[pltpu.SemaphoreType.DMA((7,)), pltpu.SemaphoreType.DMA((7,)), pltpu.SemaphoreType.DMA],
        compiler_params=pltpu.CompilerParams(vmem_limit_bytes=VMEM_LIMIT),
    )(blk)


def _row_tile(r, n):
    for cand in range(r, 15, -16):
        if r % cand == 0 and cand % 16 == 0 and cand * n * 4 <= 2 * 1024 * 1024:
            return cand
    return r


def _cast_slot(w, chip, name):
    r, n = w.shape
    tr = _row_tile(r, n)

    def body(chip_ref, w_ref, o_ref):
        o_ref[...] = w_ref[...].astype(BF16)

    grid_spec = pltpu.PrefetchScalarGridSpec(
        num_scalar_prefetch=1, grid=(r // tr,),
        in_specs=[pl.BlockSpec((tr, n), lambda i, ch: (i, 0))],
        out_specs=pl.BlockSpec((None, tr, n), lambda i, ch: (ch[0], i, 0)))
    return pl.pallas_call(
        body, name=name, grid_spec=grid_spec, out_shape=_sds((N_SHARD, r, n), BF16),
        compiler_params=_cp("arbitrary"),
    )(chip, w)


def _allgather_weights(bufs, name):
    nt = len(bufs)
    hom = [pl.BlockSpec(memory_space=pl.ANY)] * nt

    def body(*refs):
        outs = refs[nt:2 * nt]
        send_sems, recv_sems = refs[2 * nt:]
        x, y, c = _coords()
        sibling = (x, y, 1 - c)
        chips = _other_chips(x, y)

        def copy(t, k, block_chip, hc, to):
            r = outs[t].shape[1] // 2
            blk = outs[t].at[2 * block_chip[0] + block_chip[1], pl.ds(hc * r, r)]
            return pltpu.make_async_remote_copy(
                src_ref=blk, dst_ref=blk,
                send_sem=send_sems.at[t, k], recv_sem=recv_sems.at[t, k], device_id=to, device_id_type=MESH)

        started = []
        for t in range(nt):
            for j, chip in enumerate(chips):
                cp = copy(t, j, (x, y), c, (*chip, c))
                cp.start()
                started.append(cp)
        for t in range(nt):
            for j, chip in enumerate(chips):
                copy(t, j, chip, c, sibling).wait_recv()
                fw = copy(t, 3 + j, chip, c, sibling)
                fw.start()
                started.append(fw)
        for t in range(nt):
            for j, chip in enumerate(chips):
                copy(t, 3 + j, chip, 1 - c, sibling).wait_recv()
        for cp in started:
            cp.wait_send()

    return pl.pallas_call(
        body, name=name,
        out_shape=[_sds(b.shape, b.dtype) for b in bufs],
        in_specs=hom, out_specs=hom,
        input_output_aliases={t: t for t in range(nt)},
        scratch_shapes=[pltpu.SemaphoreType.DMA((nt, 6)), pltpu.SemaphoreType.DMA((nt, 6))],
    )(*bufs)


def _join_halves(tots, name):
    nt = len(tots)
    hom = [pl.BlockSpec(memory_space=pl.ANY)] * nt

    def body(*refs):
        outs = refs[nt:2 * nt]
        send_sems, recv_sems = refs[2 * nt:]
        x, y, c = _coords()
        sibling = (x, y, 1 - c)
        cps = []
        for t in range(nt):
            cp = pltpu.make_async_remote_copy(
                src_ref=outs[t].at[c], dst_ref=outs[t].at[c],
                send_sem=send_sems.at[t], recv_sem=recv_sems.at[t], device_id=sibling, device_id_type=MESH)
            cp.start()
            cps.append(cp)
        for t in range(nt):
            pltpu.make_async_remote_copy(
                src_ref=outs[t].at[c], dst_ref=outs[t].at[1 - c],
                send_sem=send_sems.at[t], recv_sem=recv_sems.at[t], device_id=sibling, device_id_type=MESH).wait_recv()
        for cp in cps:
            cp.wait_send()

    return pl.pallas_call(
        body, name=name,
        out_shape=[_sds(t.shape, t.dtype) for t in tots],
        in_specs=hom, out_specs=hom,
        input_output_aliases={t: t for t in range(nt)},
        scratch_shapes=[pltpu.SemaphoreType.DMA((nt,)), pltpu.SemaphoreType.DMA((nt,))],
    )(*tots)


def _pair_sum(g, recv, core, chip, name):
    _, _, r, n = g.shape
    tr = _row_tile(r, n)

    def body(core_ref, chip_ref, g_ref, r_ref, sb_ref, own_ref):
        tot = g_ref[...] + r_ref[...]
        sb_ref[...] = tot.astype(BF16)

        @pl.when(pl.program_id(1) == chip_ref[0])
        def _():
            own_ref[...] = tot

    grid_spec = pltpu.PrefetchScalarGridSpec(
        num_scalar_prefetch=2, grid=(r // tr, N_SHARD),
        in_specs=[pl.BlockSpec((None, None, tr, n), lambda i, s, co, ch: (s, co[0], i, 0)),
                  pl.BlockSpec((None, tr, n), lambda i, s, co, ch: (s, i, 0))],
        out_specs=[pl.BlockSpec((None, tr, n), lambda i, s, co, ch: (s, i, 0)),
                   pl.BlockSpec((tr, n), lambda i, s, co, ch: (i, 0))])
    return pl.pallas_call(
        body, name=name, grid_spec=grid_spec,
        out_shape=[_sds((N_SHARD, r, n), BF16), _sds((r, n), F32)],
        compiler_params=_cp("arbitrary", "arbitrary"),
    )(core, chip, g, recv)


def _chip_sum(own, recv, core, name):
    r, n = own.shape
    tr = _row_tile(r, n)

    def body(core_ref, o_ref, r_ref, t_ref):
        acc = o_ref[...]
        for j in range(3):
            acc = acc + r_ref[j].astype(F32)
        t_ref[...] = acc

    grid_spec = pltpu.PrefetchScalarGridSpec(
        num_scalar_prefetch=1, grid=(r // tr,),
        in_specs=[pl.BlockSpec((tr, n), lambda i, co: (i, 0)), pl.BlockSpec((3, tr, n), lambda i, co: (0, i, 0))],
        out_specs=pl.BlockSpec((None, tr, n), lambda i, co: (co[0], i, 0)))
    return pl.pallas_call(
        body, name=name, grid_spec=grid_spec, out_shape=_sds((2, r, n), F32),
        compiler_params=_cp("arbitrary"),
    )(core, own, recv)


_HBM = pl.BlockSpec(memory_space=pltpu.HBM)
_SEM = pl.BlockSpec(memory_space=pltpu.SEMAPHORE)
_EFFECT = pltpu.SideEffectType.DATAFLOW_SIDE_EFFECTING


def _ici_copies(srcs, dsts, send_sems, recv_sems, send_view, recv_view):
    x, y, c = _coords()
    out = []
    if send_view is None:
        for t in range(len(srcs)):
            r = srcs[t].shape[1] // 2
            out.append(pltpu.make_async_remote_copy(
                src_ref=srcs[t].at[:, pl.ds((1 - c) * r, r)], dst_ref=dsts[t],
                send_sem=send_sems.at[3 * t], recv_sem=recv_sems.at[3 * t],
                device_id=(x, y, 1 - c), device_id_type=MESH))
        return out
    for t in range(len(srcs)):
        for j, chip in enumerate(_other_chips(x, y)):
            out.append(pltpu.make_async_remote_copy(
                src_ref=send_view(srcs[t], chip, j, (x, y), c), dst_ref=recv_view(dsts[t], chip, j, (x, y), c),
                send_sem=send_sems.at[3 * t + j], recv_sem=recv_sems.at[3 * t + j],
                device_id=(*chip, c), device_id_type=MESH))
    return out


def _ici_start(srcs, dsts, after, send_view, recv_view, name):
    nt = len(srcs)
    inplace = dsts is None
    nbuf = nt if inplace else 2 * nt

    def body(*refs):
        send_sems, recv_sems = refs[nbuf + 1], refs[nbuf + 2]
        s_out = refs[nbuf + 3:nbuf + 3 + nt]
        d_out = s_out if inplace else refs[nbuf + 3 + nt:nbuf + 3 + 2 * nt]
        token = refs[-1]
        for cp in _ici_copies(s_out, d_out, send_sems, recv_sems, send_view, recv_view):
            cp.start()
        token[...] = jnp.zeros_like(token)

    bufs = list(srcs) + ([] if inplace else list(dsts))
    res = pl.pallas_call(
        body, name=name,
        out_shape=(pltpu.SemaphoreType.DMA((3 * nt,)), pltpu.SemaphoreType.DMA((3 * nt,)),
                   *[pltpu.HBM(b.shape, b.dtype) for b in bufs], _sds((8, 128), F32)),
        in_specs=[_HBM] * nbuf + [pl.BlockSpec(memory_space=pl.ANY)],
        out_specs=(_SEM, _SEM, *[_HBM] * nbuf, pl.BlockSpec(memory_space=pltpu.VMEM)),
        input_output_aliases={i: 2 + i for i in range(nbuf)},
        compiler_params=pltpu.CompilerParams(has_side_effects=_EFFECT),
    )(*[pltpu.with_memory_space_constraint(b, pltpu.HBM) for b in bufs], after)
    send_sems, recv_sems = res[0], res[1]
    s_thru = list(res[2:2 + nt])
    d_thru = s_thru if inplace else list(res[2 + nt:2 + 2 * nt])
    return send_sems, recv_sems, s_thru, d_thru, res[-1]


def _ici_wait(send_sems, recv_sems, srcs, dsts, after, send_view, recv_view, name):
    nt = len(srcs)
    inplace = dsts is None
    nbuf = nt if inplace else 2 * nt

    def body(*refs):
        send_ref, recv_ref = refs[nbuf], refs[nbuf + 1]
        s_out = refs[nbuf + 3:nbuf + 3 + nt]
        d_out = s_out if inplace else refs[nbuf + 3 + nt:nbuf + 3 + 2 * nt]
        for cp in _ici_copies(s_out, d_out, send_ref, recv_ref, send_view, recv_view):
            cp.wait_send()
            cp.wait_recv()

    bufs = list(srcs) + ([] if inplace else list(dsts))
    res = pl.pallas_call(
        body, name=name,
        out_shape=tuple(pltpu.HBM(b.shape, b.dtype) for b in bufs),
        in_specs=[_HBM] * nbuf + [_SEM, _SEM, pl.BlockSpec(memory_space=pl.ANY)],
        out_specs=tuple([_HBM] * nbuf),
        input_output_aliases={i: i for i in range(nbuf)},
        compiler_params=pltpu.CompilerParams(has_side_effects=_EFFECT),
    )(*bufs, send_sems, recv_sems, after)
    return list(res[:nt]) if inplace else (list(res[:nt]), list(res[nt:]))


def _w_half(buf, chip, c):
    r = buf.shape[1] // 2
    return buf.at[2 * chip[0] + chip[1], pl.ds(c * r, r)]


def _ag_send_view(buf, chip, j, me, c):
    return _w_half(buf, me, c)


def _ag_recv_view(buf, chip, j, me, c):
    return _w_half(buf, me, c)


def _rs_send_view(buf, chip, j, me, c):
    return buf.at[2 * chip[0] + chip[1]]


def _rs_recv_view(buf, chip, j, me, c):
    return buf.at[j]


def _ag_forward(bufs, name):
    nt = len(bufs)
    hom = [pl.BlockSpec(memory_space=pl.ANY)] * nt

    def body(*refs):
        outs = refs[nt:2 * nt]
        send_sems, recv_sems = refs[2 * nt:]
        x, y, c = _coords()
        sibling = (x, y, 1 - c)
        chips = _other_chips(x, y)

        def copy(t, j, hc):
            blk = _w_half(outs[t], chips[j], hc)
            return pltpu.make_async_remote_copy(
                src_ref=blk, dst_ref=blk, send_sem=send_sems.at[t, j], recv_sem=recv_sems.at[t, j],
                device_id=sibling, device_id_type=MESH)

        started = [copy(t, j, c) for t in range(nt) for j in range(3)]
        for cp in started:
            cp.start()
        for t in range(nt):
            for j in range(3):
                copy(t, j, 1 - c).wait_recv()
        for cp in started:
            cp.wait_send()

    return pl.pallas_call(
        body, name=name,
        out_shape=[_sds(b.shape, b.dtype) for b in bufs],
        in_specs=hom, out_specs=hom,
        input_output_aliases={t: t for t in range(nt)},
        scratch_shapes=[pltpu.SemaphoreType.DMA((nt, 3)), pltpu.SemaphoreType.DMA((nt, 3))],
    )(*bufs)


def _rs_swap_begin(grads, after, tag):
    land = [lax.empty((N_SHARD, g.shape[1] // 2, g.shape[2]), g.dtype) for g in grads]
    send_sems, recv_sems, s_thru, d_thru, token = _ici_start(grads, land, after, None, None, name="rs_swapgo_" + tag)
    return dict(sems=(send_sems, recv_sems), grads=s_thru, land=d_thru, tag=tag), token


def _rs_scatter_begin(swap, after):
    tag = swap["tag"]
    x, y, c = _coords()
    core = jnp.reshape(c, (1,)).astype(jnp.int32)
    chip = jnp.reshape(2 * x + y, (1,)).astype(jnp.int32)
    grads, recv = _ici_wait(*swap["sems"], swap["grads"], swap["land"], after, None, None, name="rs_swapend_" + tag)
    sums, owns = [], []
    for t, (g, rv) in enumerate(zip(grads, recv)):
        r = g.shape[1] // 2
        sb, own = _pair_sum(g.reshape(N_SHARD, 2, r, g.shape[2]), rv, core, chip, name=f"rs_pair_{tag}_{t}")
        sums.append(sb)
        owns.append(own)
    land = [lax.empty((3,) + s.shape[1:], s.dtype) for s in sums]
    send_sems, recv_sems, s_thru, d_thru, token = _ici_start(
        sums, land, after, _rs_send_view, _rs_recv_view, name="rs_start_" + tag)
    return dict(sems=(send_sems, recv_sems), sums=s_thru, land=d_thru, owns=owns, core=core, tag=tag), token


def _rs_end(state, after):
    tag = state["tag"]
    _, got = _ici_wait(*state["sems"], state["sums"], state["land"], after, _rs_send_view, _rs_recv_view,
                       name="rs_wait_" + tag)
    tots = [_chip_sum(o, gt, state["core"], name=f"rs_chip_{tag}_{t}")
            for t, (o, gt) in enumerate(zip(state["owns"], got))]
    full = _join_halves(tots, name="rs_join_" + tag)
    return [f.reshape(2 * f.shape[1], f.shape[2]) for f in full]


def _rope_lane_table():
    d = jnp.arange(128) % HEAD
    inv_freq = ROPE_THETA ** (-jnp.arange(0, ROT, 2, dtype=F32) / ROT)
    rot = d < ROT
    rows = [jnp.where(rot, inv_freq[d % (ROT // 2)], 0.0), rot.astype(F32),
            (d < ROT // 2).astype(F32), jnp.logical_and(d >= ROT // 2, rot).astype(F32)]
    return jnp.concatenate([jnp.stack(rows), jnp.zeros((4, 128), F32)], axis=0)


def _pad8(rows):
    return jnp.concatenate([rows, jnp.zeros((8 - rows.shape[0], rows.shape[1]), F32)], axis=0)


def kernel(x, c, positions, ada_w, ada_b, w_in, b_in, sinks, pool_w, pool_scale, w_out, w_gate, w_up, w_down, g_pre_mix, g_post_mix, g_pre_ffn, g_post_ffn, loss_target, m_ada_w, m_ada_b, m_w_in, m_b_in, m_sinks, m_pool_w, m_pool_scale, m_w_out, m_w_gate, m_w_up, m_w_down, m_g_pre_mix, m_g_post_mix, m_g_pre_ffn, m_g_post_ffn, v_ada_w, v_ada_b, v_w_in, v_b_in, v_sinks, v_pool_w, v_pool_scale, v_w_out, v_w_gate, v_w_up, v_w_down, v_g_pre_mix, v_g_post_mix, v_g_pre_ffn, v_g_post_ffn):
    T = x.shape[1]
    n_layers = ada_w.shape[0]
    ax, ay, ac = _coords()
    my_dev = 4 * ax + 2 * ay + ac
    my_chip = 2 * ax + ay
    x0 = x.reshape(T, D_MODEL)
    target = loss_target.reshape(T, D_MODEL)

    c_all = _allgather8(c.reshape(8, 128), name="ag_c").reshape(N_DEV, D_MODEL)
    ada_b_sh = lax.dynamic_slice_in_dim(ada_b, my_chip * ADA_SH, ADA_SH, axis=1).reshape(n_layers, 1, ADA_SH)
    mod_part = _mod_fwd(c_all, ada_w, ada_b_sh)
    mod_all = _allgather8(mod_part.reshape(n_layers * 8, ADA_SH), name="ag_mod")
    mod_all = mod_all.reshape(N_DEV, n_layers, 8, ADA_SH)[0::2]
    mod_mine = lax.dynamic_index_in_dim(mod_all, my_dev, axis=2, keepdims=False)
    mod = jnp.transpose(mod_mine, (1, 0, 2)).reshape(n_layers, 6, D_MODEL)

    pos_b = jnp.broadcast_to(positions.reshape(T, 1), (T, 128))
    rc, rs1, rs2 = _rope_tables(pos_b, _rope_lane_table())

    chip1 = jnp.reshape(my_chip, (1,)).astype(jnp.int32)

    def tr(t):
        return jnp.transpose(t, (0, 2, 1))

    w_in_t, w_gate_t, w_up_t = tr(w_in), tr(w_gate), tr(w_up)

    def cast_layer(l):
        return [_cast_slot(w[l], chip1, name=f"cast_{nm}{l}")
                for nm, w in (("w_in", w_in_t), ("w_out", w_out), ("w_gate", w_gate_t), ("w_up", w_up_t),
                              ("w_down", w_down))]

    def as_operands(bufs):
        gin, gout, gg, gu, gd = bufs
        return (gin.reshape(IN_W, D_MODEL), gout.reshape(D_MODEL, D_MODEL), gg.reshape(D_FF, D_MODEL),
                gu.reshape(D_FF, D_MODEL), gd.reshape(D_FF, D_MODEL))

    bufs0 = cast_layer(0)
    win0 = _allgather_weights(bufs0[:1], name="ag_w0_in")
    rest_send, rest_recv, rest_bufs, _, ag_token = _ici_start(
        bufs0[1:], None, win0[0], _ag_send_view, _ag_recv_view, name="ag_start_0")
    weights = [None] * n_layers

    saved = []
    xl = x0
    for l in range(n_layers):
        mod8 = _pad8(mod[l])
        if l + 1 < n_layers:
            ag_send, ag_recv, ag_bufs, _, ag_token = _ici_start(
                cast_layer(l + 1), None, ag_token, _ag_send_view, _ag_recv_view, name=f"ag_start_{l + 1}")
        if l == 0 or l + 1 < n_layers:
            mod8 = mod8 + ag_token[0, 0]
        g8 = _pad8(jnp.stack([g_pre_mix[l], g_post_mix[l], g_pre_ffn[l], g_post_ffn[l]]))
        sink_b = jnp.broadcast_to(sinks[l][:, None], (N_HEADS, 128))
        psc = pool_scale[l].reshape(1, POOL_W)
        win = win0[0].reshape(IN_W, D_MODEL) if l == 0 else weights[l][0]
        h, q, k, v, u = _fwd_in(xl, mod8, g8, win, b_in[l].reshape(1, IN_W), rc, rs1, rs2)
        attn, lse = _attn_fwd(q, k, v, sink_b)
        pool, pooled = _pool_fwd(u, pool_w[l], psc)
        if l == 0:
            arrived = _ici_wait(rest_send, rest_recv, rest_bufs, None, pool, _ag_send_view, _ag_recv_view,
                                name="ag_wait_0")
            weights[0] = as_operands(win0 + _ag_forward(arrived, name="ag_fwd_0"))
        win, wout, wg, wu, wd = weights[l]
        if l + 1 < n_layers:
            mix, x1, h2, act, ga, gb, f, x2 = _out_ffn_fwd(attn, pool, xl, wout, mod8, g8, wg, wu, wd)
        else:
            mix, x1, h2, act, ga, gb, f, x2, loss_tile = _out_ffn_fwd(attn, pool, xl, wout, mod8, g8, wg, wu, wd,
                                                                      target=target)
        saved.append(dict(x=xl, h=h, q=q, k=k, v=v, lse=lse, attn=attn, pool=pool, pooled=pooled, mix=mix,
                          x1=x1, h2=h2, act=act, ga=ga, gb=gb, f=f, mod8=mod8, g8=g8, sink_b=sink_b, psc=psc))
        xl = x2
        if l + 1 < n_layers:
            arrived = _ici_wait(ag_send, ag_recv, ag_bufs, None, x2, _ag_send_view, _ag_recv_view,
                                name=f"ag_wait_{l + 1}")
            weights[l + 1] = as_operands(_ag_forward(arrived, name=f"ag_fwd_{l + 1}"))

    dy = xl
    loss = lax.psum(loss_tile[0, 0], ("x", "y", "c"))

    small = [None] * n_layers
    dmod_rows = [None] * n_layers
    reduced = [dict() for _ in range(n_layers)]
    att_swap = None
    dx = dy
    for l in reversed(range(n_layers)):
        s = saved[l]
        win, wout, wg, wu, wd = weights[l]
        if att_swap is not None:
            s = dict(s, mod8=s["mod8"] + att_swap[1][0, 0])
        dx1, df, da, db, red_f = _ffn_bwd(dx, s["f"], s["ga"], s["gb"], s["x1"], s["mod8"], s["g8"], wg, wu, wd)
        token = None
        if att_swap is not None:
            att_scatter = _rs_scatter_begin(att_swap[0], dx1)
            token = att_scatter[1]
        ffn_shards = (N_SHARD, FF_SH, D_MODEL)
        g_wd = _wgrad(s["act"], df, name="wgrad_down", after=token).reshape(ffn_shards)
        g_wg = _wgrad(da, s["h2"], name="wgrad_gate").reshape(ffn_shards)
        g_wu = _wgrad(db, s["h2"], name="wgrad_up").reshape(ffn_shards)
        ffn_swap = _rs_swap_begin([g_wg, g_wu, g_wd], dx1, tag=f"{l}f")
        if att_swap is not None:
            got = _rs_end(att_scatter[0], ffn_swap[1])
            reduced[l + 1].update(w_in=got[0], w_out=got[1])
        s = dict(s, mod8=s["mod8"] + ffn_swap[1][0, 0])
        dmix, dattn, dpool, red_c = _mix_bwd(dx1, s["mix"], s["mod8"], s["g8"], wout)
        g_wout = jnp.concatenate([_wgrad(s["attn"], dmix, name="wgrad_out_a"),
                                  _wgrad(s["pool"], dmix, name="wgrad_out_p")], axis=0)
        ffn_scatter = _rs_scatter_begin(ffn_swap[0], dattn)
        dq, dk_e, dk_o, dv_e, dv_o, dsink = _attn_bwd(s["q"], s["k"], s["v"], s["lse"], dattn,
                                                      s["sink_b"] + ffn_scatter[1][0:1, :])
        du, g_poolw, dpsc = _pool_bwd(dpool, s["pooled"], pool_w[l], s["psc"])
        dx, dproj, red_d, dbin = _in_bwd(dq, (dk_e, dk_o), (dv_e, dv_o), du, rc, rs1, rs2, s["x"], dx1, s["mod8"],
                                         s["g8"], win)
        g_win = _wgrad(dproj, s["h"], name="wgrad_in")
        g_win_sh = g_win.reshape(N_SHARD, IN_SH, D_MODEL)
        got = _rs_end(ffn_scatter[0], dproj)
        reduced[l].update(w_gate=got[0], w_up=got[1], w_down=got[2])
        att_swap = _rs_swap_begin([g_win_sh, g_wout.reshape(N_SHARD, OUT_SH, D_MODEL)], dx, tag=f"{l}a")
        dmod_rows[l] = jnp.concatenate([red_d[0], red_d[1], red_c[0], red_f[2], red_f[3], red_f[0]])
        small[l] = jnp.concatenate([red_d[2], red_c[1], red_f[4], red_f[1], dbin[0], dpsc[0], dsink[:, 0],
                                    jnp.zeros((120,), F32), g_poolw.reshape(-1)])
    grad_x = dx.reshape(1, T, D_MODEL)

    per_layer = small[0].shape[0]
    rows_small = n_layers * per_layer // 128
    rows_mod = n_layers * 6 * D_MODEL // 128
    rows_pad = -(rows_small + rows_mod) % 8
    pack = jnp.concatenate(small + dmod_rows + [jnp.zeros((rows_pad * 128,), F32)]).reshape(-1, 128)
    pack = pack + att_swap[1][0, 0]
    gathered = _allgather8(pack, name="ag_small").reshape(N_DEV, pack.shape[0], 128)
    summed = _sum_devices(gathered)
    att_scatter = _rs_scatter_begin(att_swap[0], summed)
    small_sum = summed[:rows_small].reshape(n_layers, per_layer)
    o = 0
    small_g = {}
    for nm, width in (("g_pre_mix", D_MODEL), ("g_post_mix", D_MODEL), ("g_pre_ffn", D_MODEL),
                      ("g_post_ffn", D_MODEL), ("b_in", IN_W), ("pool_scale", POOL_W), ("sinks", 128),
                      ("pool_w", 4 * 128 * 128)):
        small_g[nm] = small_sum[:, o:o + width]
        o += width
    small_g["sinks"] = small_g["sinks"][:, :N_HEADS]
    small_g["pool_w"] = small_g["pool_w"].reshape(n_layers, 4, 128, 128)
    small_g["ada_b"] = summed[rows_small:rows_small + rows_mod].reshape(n_layers, 6 * D_MODEL)
    dmod_all = gathered[:, rows_small:rows_small + rows_mod].reshape(N_DEV, n_layers, N_SHARD, ADA_SH)
    dmod_sh = lax.dynamic_index_in_dim(dmod_all, my_chip, axis=2, keepdims=False)
    g_ada_w = _ada_wgrad(jnp.transpose(c_all), jnp.transpose(dmod_sh, (1, 0, 2)))

    grads = dict(ada_w=g_ada_w, ada_b=small_g["ada_b"], b_in=small_g["b_in"], sinks=small_g["sinks"],
                 pool_w=small_g["pool_w"], pool_scale=small_g["pool_scale"], g_pre_mix=small_g["g_pre_mix"],
                 g_post_mix=small_g["g_post_mix"], g_pre_ffn=small_g["g_pre_ffn"], g_post_ffn=small_g["g_post_ffn"])
    params = dict(ada_w=(ada_w, m_ada_w, v_ada_w), ada_b=(ada_b, m_ada_b, v_ada_b), w_in=(w_in, m_w_in, v_w_in),
                  b_in=(b_in, m_b_in, v_b_in), sinks=(sinks, m_sinks, v_sinks), pool_w=(pool_w, m_pool_w, v_pool_w),
                  pool_scale=(pool_scale, m_pool_scale, v_pool_scale), w_out=(w_out, m_w_out, v_w_out),
                  w_gate=(w_gate, m_w_gate, v_w_gate), w_up=(w_up, m_w_up, v_w_up),
                  w_down=(w_down, m_w_down, v_w_down), g_pre_mix=(g_pre_mix, m_g_pre_mix, v_g_pre_mix),
                  g_post_mix=(g_post_mix, m_g_post_mix, v_g_post_mix), g_pre_ffn=(g_pre_ffn, m_g_pre_ffn, v_g_pre_ffn),
                  g_post_ffn=(g_post_ffn, m_g_post_ffn, v_g_post_ffn))
    names = list(params)
    updates = {nm: _adamw_nd(*params[nm][:1], grads[nm], *params[nm][1:], name="adamw_" + nm) for nm in grads}

    got = _rs_end(att_scatter[0], updates["ada_w"][0])
    reduced[0].update(w_in=got[0], w_out=got[1])
    for nm in ("w_in", "w_out", "w_gate", "w_up", "w_down"):
        g = jnp.stack([reduced[l][nm] for l in range(n_layers)])
        if nm in ("w_in", "w_gate", "w_up"):
            upd = _adamw_nd(tr(params[nm][0]), g, tr(params[nm][1]), tr(params[nm][2]), name="adamw_" + nm)
            grads[nm], updates[nm] = tr(g), [tr(u) for u in upd]
        else:
            grads[nm], updates[nm] = g, _adamw_nd(params[nm][0], g, *params[nm][1:], name="adamw_" + nm)
    return (loss, grad_x, *[grads[nm] for nm in names], *[updates[nm][0] for nm in names],
            *[updates[nm][1] for nm in names], *[updates[nm][2] for nm in names])
```

```python
import functools

import jax
import jax.numpy as jnp
from jax import lax
from jax.experimental import pallas as pl
from jax.experimental.pallas import tpu as pltpu

F32 = jnp.float32
BF16 = jnp.bfloat16
MESH = pl.DeviceIdType.MESH

D_MODEL = 1024
ATTN_W = 512
KV_W = 128
KVD_W = 256
POOL_W = 512
IN_W = 1280
D_FF = 2816
N_SHARD = 4
FF_SH = D_FF // N_SHARD
IN_SH = IN_W // N_SHARD
OUT_SH = D_MODEL // N_SHARD
ADA_SH = 6 * D_MODEL // N_SHARD
HEAD = 64
N_HEADS = 8
GROUP = 4
BLK = 128
POOL_WINDOWS = (2, 4, 8, 16)
HALO = 16
ROT = 16
ROPE_THETA = 500000.0
EPS = 1e-6
NEG_INF = -1e30
N_DEV = 8

ADAM_LR = 0.001
ADAM_B1 = 0.9
ADAM_B2 = 0.999
ADAM_EPS = 1e-08
ADAM_WD = 0.01
ADAM_STEP = 10

VMEM_LIMIT = 48 * 1024 * 1024
FFN_VMEM_LIMIT = 60 * 1024 * 1024
WGRAD_TOKENS = 2048


def _cp(*sem, vmem=VMEM_LIMIT):
    return pltpu.CompilerParams(dimension_semantics=sem, vmem_limit_bytes=vmem)


def _full(shape):
    nd = len(shape)
    return pl.BlockSpec(shape, lambda *_: (0,) * nd)


def _resident(shape):
    nd = len(shape)
    return pl.BlockSpec(shape, lambda *_: (0,) * nd, pipeline_mode=pl.Buffered(1))


def _rows(tm, ncol):
    return pl.BlockSpec((tm, ncol), lambda i: (i, 0))


def _sds(shape, dtype):
    return jax.ShapeDtypeStruct(shape, dtype)


def _nt(a, b):
    return lax.dot_general(a, b, (((1,), (1,)), ((), ())), preferred_element_type=F32)


def _tn(a, b):
    return lax.dot_general(a, b, (((0,), (0,)), ((), ())), preferred_element_type=F32)


def _mm(a, b):
    return jnp.dot(a, b, preferred_element_type=F32)


def _rstd(x):
    return lax.rsqrt(jnp.mean(x * x, axis=-1, keepdims=True) + EPS)


def _colsum(x):
    return jnp.sum(x, axis=0, keepdims=True)


def _norm_gain_bwd(dy, xhat, rstd, gain):
    p = dy * xhat
    dx = rstd * (dy * gain - xhat * jnp.mean(p * gain, axis=-1, keepdims=True))
    return dx, _colsum(p)


def _rope_tables(pos_b, lane_tab):
    T = pos_b.shape[0]
    tm = min(T, 1024)

    def body(pos_ref, tab_ref, c_ref, s1_ref, s2_ref):
        ang = pos_ref[...].astype(F32) * tab_ref[0:1, :]
        cs = jnp.cos(ang)
        sn = jnp.sin(ang)
        m_rot = tab_ref[1:2, :]
        c_ref[...] = cs * m_rot + (1.0 - m_rot)
        s1_ref[...] = -sn * tab_ref[2:3, :]
        s2_ref[...] = sn * tab_ref[3:4, :]

    out = _sds((T, 128), F32)
    return pl.pallas_call(
        body, name="rope_tables", grid=(T // tm,),
        in_specs=[_rows(tm, 128), _full((8, 128))],
        out_specs=[_rows(tm, 128)] * 3, out_shape=[out] * 3,
        compiler_params=_cp("parallel"),
    )(pos_b, lane_tab)


def _rot_fwd(t, c, s1, s2):
    w = t.shape[-1]
    return t * c + pltpu.roll(t, w - 8, 1) * s1 + pltpu.roll(t, 8, 1) * s2


def _rot_bwd(d, c, s1, s2):
    w = d.shape[-1]
    return d * c + pltpu.roll(d * s1, 8, 1) + pltpu.roll(d * s2, w - 8, 1)


def _store_dup(ref, t):
    low = lax.broadcasted_iota(jnp.int32, t.shape, 1) < HEAD
    sw = pltpu.roll(t, HEAD, 1)
    ref[:, 0:128] = jnp.where(low, t, sw).astype(BF16)
    ref[:, 128:256] = jnp.where(low, sw, t).astype(BF16)


def _fold_dup(d):
    low = lax.broadcasted_iota(jnp.int32, (d.shape[0], 128), 1) < HEAD
    d0 = d[:, 0:128]
    d1 = d[:, 128:256]
    return jnp.where(low, d0 + pltpu.roll(d0, HEAD, 1), d1 + pltpu.roll(d1, HEAD, 1))


def _fwd_in(x, mod8, g8, w_in, b_in, rc, rs1, rs2):
    T = x.shape[0]
    tm = min(T, 1024)

    def body(x_ref, mod_ref, g_ref, w_ref, b_ref, c_ref, s1_ref, s2_ref,
             h_ref, q_ref, k_ref, v_ref, u_ref):
        xf = x_ref[...]
        h = (xf * _rstd(xf) * g_ref[0:1, :]) * (1.0 + mod_ref[1:2, :]) + mod_ref[0:1, :]
        hb = h.astype(BF16)
        h_ref[...] = hb
        c = c_ref[...]
        s1 = s1_ref[...]
        s2 = s2_ref[...]
        proj = _nt(hb, w_ref[...]) + b_ref[...]
        q = _rot_fwd(proj[:, 0:ATTN_W], jnp.tile(c, (1, 4)), jnp.tile(s1, (1, 4)), jnp.tile(s2, (1, 4)))
        q_ref[...] = (q * (HEAD ** -0.5)).astype(BF16)
        _store_dup(k_ref, _rot_fwd(proj[:, ATTN_W:ATTN_W + KV_W], c, s1, s2))
        _store_dup(v_ref, proj[:, ATTN_W + KV_W:ATTN_W + 2 * KV_W])
        u_ref[...] = proj[:, ATTN_W + 2 * KV_W:IN_W]

    return pl.pallas_call(
        body, name="fwd_in", grid=(T // tm,),
        in_specs=[_rows(tm, D_MODEL), _full((8, D_MODEL)), _full((8, D_MODEL)),
                  _resident((IN_W, D_MODEL)), _full((1, IN_W)),
                  _rows(tm, 128), _rows(tm, 128), _rows(tm, 128)],
        out_specs=[_rows(tm, D_MODEL), _rows(tm, ATTN_W), _rows(tm, KVD_W), _rows(tm, KVD_W), _rows(tm, POOL_W)],
        out_shape=[_sds((T, D_MODEL), BF16), _sds((T, ATTN_W), BF16), _sds((T, KVD_W), BF16),
                   _sds((T, KVD_W), BF16), _sds((T, POOL_W), F32)],
        compiler_params=_cp("parallel"),
    )(x, mod8, g8, w_in, b_in, rc, rs1, rs2)


def _band_mask(n):
    kk = lax.broadcasted_iota(jnp.int32, (2 * BLK, BLK), 0)
    qi = lax.broadcasted_iota(jnp.int32, (2 * BLK, BLK), 1)
    first = jnp.where(n > 0, 0, 2 * BLK)
    in_prev = jnp.logical_and(kk < BLK, kk > qi + first)
    in_cur = jnp.logical_and(kk >= BLK, (kk - BLK) <= qi)
    one = jnp.logical_or(in_prev, in_cur)
    return jnp.concatenate([one] * GROUP, axis=1)


def _head_row(ref, j, base=0):
    return jnp.concatenate([ref[base + GROUP * j + r:base + GROUP * j + r + 1, :] for r in range(GROUP)], axis=1)


def _stack_heads(x_ref, j, rows=slice(None)):
    low = lax.broadcasted_iota(jnp.int32, (BLK, 128), 1) < HEAD
    parts = []
    for gp in (2 * j, 2 * j + 1):
        x2 = x_ref[rows, gp * 128:(gp + 1) * 128]
        parts.append(jnp.where(low, x2, jnp.zeros_like(x2)))
        parts.append(jnp.where(low, jnp.zeros_like(x2), x2))
    return jnp.concatenate(parts, axis=0)


def _unstack_heads(o):
    low = lax.broadcasted_iota(jnp.int32, (BLK, 128), 1) < HEAD
    return [jnp.where(low, o[0:BLK], o[BLK:2 * BLK]), jnp.where(low, o[2 * BLK:3 * BLK], o[3 * BLK:4 * BLK])]


def _attn_fwd(q, kd, vd, sink_b):
    T = q.shape[0]
    nb = T // BLK
    assert nb % 2 == 0

    def body(q_ref, kp_ref, kc_ref, vp_ref, vc_ref, sk_ref, o_ref, lse_ref):
        for sub in range(2):
            rows = slice(sub * BLK, (sub + 1) * BLK)
            valid = _band_mask(2 * pl.program_id(0) + sub)
            for j in range(N_HEADS // GROUP):
                lanes = slice(j * 128, (j + 1) * 128)
                k_prev = kp_ref[:, lanes] if sub == 0 else kc_ref[0:BLK, lanes]
                v_prev = vp_ref[:, lanes] if sub == 0 else vc_ref[0:BLK, lanes]
                kcat = jnp.concatenate([k_prev, kc_ref[rows, lanes]], axis=0)
                vcat = jnp.concatenate([v_prev, vc_ref[rows, lanes]], axis=0)
                s = jnp.where(valid, _nt(kcat, _stack_heads(q_ref, j, rows)), NEG_INF)
                sk = _head_row(sk_ref, j)
                m = jnp.maximum(jnp.max(s, axis=0, keepdims=True), sk)
                p = jnp.exp(s - m)
                den = jnp.sum(p, axis=0, keepdims=True) + jnp.exp(sk - m)
                p = p * (1.0 / den)
                o = _tn(p.astype(BF16), vcat)
                o_ref[rows, 2 * j * 128:(2 * j + 2) * 128] = jnp.concatenate(_unstack_heads(o), axis=1).astype(BF16)
                lse = m + jnp.log(den)
                for r in range(GROUP):
                    h = sub * N_HEADS + GROUP * j + r
                    lse_ref[h:h + 1, :] = lse[:, r * 128:(r + 1) * 128]

    prev = lambda i: (jnp.maximum(2 * i - 1, 0), 0)
    cur = lambda i: (i, 0)
    return pl.pallas_call(
        body, name="attn_fwd", grid=(nb // 2,),
        in_specs=[pl.BlockSpec((2 * BLK, ATTN_W), cur),
                  pl.BlockSpec((BLK, KVD_W), prev), pl.BlockSpec((2 * BLK, KVD_W), cur),
                  pl.BlockSpec((BLK, KVD_W), prev), pl.BlockSpec((2 * BLK, KVD_W), cur),
                  _full((8, 128))],
        out_specs=[pl.BlockSpec((2 * BLK, ATTN_W), cur), pl.BlockSpec((2 * N_HEADS, 128), cur)],
        out_shape=[_sds((T, ATTN_W), BF16), _sds((nb * N_HEADS, 128), F32)],
        compiler_params=_cp("parallel"),
    )(q, kd, kd, vd, vd, sink_b)


def _pool_fwd(u, pool_w, pool_scale):
    T = u.shape[0]
    tm = min(T, 1024)

    def body(u_ref, w_ref, sc_ref, out_ref, pooled_ref, halo):
        i = pl.program_id(0)

        @pl.when(i == 0)
        def _():
            halo[...] = jnp.zeros_like(halo)

        ub = u_ref[...]
        ext = jnp.concatenate([halo[...], ub], axis=0)
        halo[...] = ub[tm - HALO:, :]
        tpos = (i * tm + lax.broadcasted_iota(jnp.int32, (tm, 1), 0)).astype(F32)
        for g, w in enumerate(POOL_WINDOWS):
            lanes = slice(g * 128, (g + 1) * 128)
            s = ext[:, lanes]
            sh = 1
            while sh < w:
                s = s + pltpu.roll(s, sh, 0)
                sh *= 2
            cnt = jnp.minimum(tpos + 1.0, float(w))
            pb = (s[HALO:, :] / cnt - ub[:, lanes]).astype(BF16)
            z = _mm(pb, w_ref[g].astype(BF16))
            out_ref[:, lanes] = (z * sc_ref[:, lanes]).astype(BF16)
            pooled_ref[:, lanes] = pb

    return pl.pallas_call(
        body, name="pool_fwd", grid=(T // tm,),
        in_specs=[_rows(tm, POOL_W), _full((4, 128, 128)), _full((1, POOL_W))],
        out_specs=[_rows(tm, POOL_W), _rows(tm, POOL_W)],
        out_shape=[_sds((T, POOL_W), BF16), _sds((T, POOL_W), BF16)],
        scratch_shapes=[pltpu.VMEM((HALO, POOL_W), F32)],
        compiler_params=_cp("arbitrary"),
    )(u, pool_w, pool_scale)


FF_CHUNKS = ((0, 1024), (1024, 2048), (2048, D_FF))


def _out_ffn_fwd(attn, pool, x, w_out, mod8, g8, wg, wu, wd, target=None):
    T = x.shape[0]
    tm = min(T, 256)
    last = target is not None

    def body(*refs):
        a_ref, p_ref, xin_ref, wo_ref, mod_ref, g_ref, wg_ref, wu_ref, wd_ref = refs[:9]
        t_ref = refs[9] if last else None
        mix_ref, x1_ref, h_ref, act_ref, ga_ref, gb_ref, f_ref, x2_ref = refs[9 + last:17 + last]
        mix = _mm(a_ref[...], wo_ref[0:ATTN_W, :]) + _mm(p_ref[...], wo_ref[ATTN_W:, :])
        mix_ref[...] = mix
        xf = xin_ref[...] + mod_ref[2:3, :] * (mix * _rstd(mix) * g_ref[1:2, :])
        x1_ref[...] = xf
        h = (xf * _rstd(xf) * g_ref[2:3, :]) * (1.0 + mod_ref[4:5, :]) + mod_ref[3:4, :]
        hb = h.astype(BF16)
        h_ref[...] = hb
        f = jnp.zeros((tm, D_MODEL), F32)
        for lo, hi in FF_CHUNKS:
            a = _nt(hb, wg_ref[lo:hi, :])
            b = _nt(hb, wu_ref[lo:hi, :])
            sig = jax.nn.sigmoid(a)
            sl = a * sig
            act = (sl * b).astype(BF16)
            act_ref[:, lo:hi] = act
            ga_ref[:, lo:hi] = (b * (sig * (1.0 + a * (1.0 - sig)))).astype(BF16)
            gb_ref[:, lo:hi] = sl.astype(BF16)
            f = f + _mm(act, wd_ref[lo:hi, :])
        f_ref[...] = f
        x2 = xf + mod_ref[5:6, :] * (f * _rstd(f) * g_ref[3:4, :])
        if not last:
            x2_ref[...] = x2
        else:
            loss_ref = refs[18]

            @pl.when(pl.program_id(0) == 0)
            def _():
                loss_ref[...] = jnp.zeros_like(loss_ref)

            e = x2 - t_ref[...]
            x2_ref[...] = e * (1.0 / D_MODEL)
            loss_ref[...] += 0.5 * jnp.sum(jnp.mean(e * e, axis=-1, keepdims=True), axis=0, keepdims=True)

    act_shape = _sds((T, D_FF), BF16)
    wide = _sds((T, D_MODEL), F32)
    weights = [_resident((D_FF, D_MODEL))] * 3
    return pl.pallas_call(
        body, name="out_ffn_fwd_loss" if last else "out_ffn_fwd", grid=(T // tm,),
        in_specs=[_rows(tm, ATTN_W), _rows(tm, POOL_W), _rows(tm, D_MODEL), _resident((D_MODEL, D_MODEL)),
                  _full((8, D_MODEL)), _full((8, D_MODEL)), *weights]
        + ([_rows(tm, D_MODEL)] if last else []),
        out_specs=[_rows(tm, D_MODEL), _rows(tm, D_MODEL), _rows(tm, D_MODEL), _rows(tm, D_FF), _rows(tm, D_FF),
                   _rows(tm, D_FF), _rows(tm, D_MODEL), _rows(tm, D_MODEL)] + ([_full((8, 128))] if last else []),
        out_shape=[wide, wide, _sds((T, D_MODEL), BF16), act_shape, act_shape, act_shape, wide, wide]
        + ([_sds((8, 128), F32)] if last else []),
        compiler_params=_cp("arbitrary" if last else "parallel", vmem=FFN_VMEM_LIMIT),
    )(attn, pool, x, w_out, mod8, g8, wg, wu, wd, *([target] if last else []))


def _ffn_bwd(dx2, f, ga, gb, x1, mod8, g8, wg, wu, wd):
    T = dx2.shape[0]
    tm = min(T, 256)

    def body(dx_ref, f_ref, ga_ref, gb_ref, x_ref, mod_ref, g_ref, wg_ref, wu_ref, wd_ref,
             dx1_ref, df_ref, da_ref, db_ref, red_ref):
        @pl.when(pl.program_id(0) == 0)
        def _():
            red_ref[...] = jnp.zeros_like(red_ref)

        dx = dx_ref[...]
        fv = f_ref[...]
        rstd = _rstd(fv)
        fhat = fv * rstd
        gpost = g_ref[3:4, :]
        gate = mod_ref[5:6, :]
        df, s_post = _norm_gain_bwd(dx, fhat, rstd, gate * gpost)
        red_ref[0:1, :] += gpost * s_post
        red_ref[1:2, :] += gate * s_post
        dfb = df.astype(BF16)
        df_ref[...] = dfb
        dh = jnp.zeros((tm, D_MODEL), F32)
        for lo, hi in FF_CHUNKS:
            dact = _nt(dfb, wd_ref[lo:hi, :])
            da = (dact * ga_ref[:, lo:hi].astype(F32)).astype(BF16)
            db = (dact * gb_ref[:, lo:hi].astype(F32)).astype(BF16)
            da_ref[:, lo:hi] = da
            db_ref[:, lo:hi] = db
            dh = dh + _mm(da, wg_ref[lo:hi, :]) + _mm(db, wu_ref[lo:hi, :])
        xf = x_ref[...]
        rstd1 = _rstd(xf)
        xhat = xf * rstd1
        gpre = g_ref[2:3, :]
        scale1 = 1.0 + mod_ref[4:5, :]
        dxn, s_pre = _norm_gain_bwd(dh, xhat, rstd1, scale1 * gpre)
        red_ref[2:3, :] += _colsum(dh)
        red_ref[3:4, :] += gpre * s_pre
        red_ref[4:5, :] += scale1 * s_pre
        dx1_ref[...] = dx + dxn

    act_shape = _sds((T, D_FF), BF16)
    return pl.pallas_call(
        body, name="ffn_bwd", grid=(T // tm,),
        in_specs=[_rows(tm, D_MODEL), _rows(tm, D_MODEL), _rows(tm, D_FF), _rows(tm, D_FF), _rows(tm, D_MODEL),
                  _full((8, D_MODEL)), _full((8, D_MODEL)),
                  _resident((D_FF, D_MODEL)), _resident((D_FF, D_MODEL)), _resident((D_FF, D_MODEL))],
        out_specs=[_rows(tm, D_MODEL), _rows(tm, D_MODEL), _rows(tm, D_FF), _rows(tm, D_FF), _full((8, D_MODEL))],
        out_shape=[_sds((T, D_MODEL), F32), _sds((T, D_MODEL), BF16), act_shape, act_shape, _sds((8, D_MODEL), F32)],
        compiler_params=_cp("arbitrary"),
    )(dx2, f, ga, gb, x1, mod8, g8, wg, wu, wd)


def _wgrad(a, b, name, after=None):
    T, K = a.shape
    N = b.shape[1]
    tt = min(T, WGRAD_TOKENS)
    tk = next(c for c in (1408, 640, 512, 256, 128) if K % c == 0)

    def body(a_ref, b_ref, *rest):
        o_ref = rest[-1]

        @pl.when(pl.program_id(1) == 0)
        def _():
            o_ref[...] = jnp.zeros_like(o_ref)

        o_ref[...] += _tn(a_ref[...], b_ref[...])

    extra = [] if after is None else [after]
    return pl.pallas_call(
        body, name=name, grid=(K // tk, T // tt),
        in_specs=[pl.BlockSpec((tt, tk), lambda i, t: (t, i)), pl.BlockSpec((tt, N), lambda i, t: (t, 0))]
        + [pl.BlockSpec(memory_space=pl.ANY)] * len(extra),
        out_specs=pl.BlockSpec((tk, N), lambda i, t: (i, 0)),
        out_shape=_sds((K, N), F32),
        compiler_params=_cp("parallel", "arbitrary"),
    )(a, b, *extra)


def _mix_bwd(dx1, mix, mod8, g8, w_out):
    T = dx1.shape[0]
    tm = min(T, 1024)

    def body(dx_ref, mix_ref, mod_ref, g_ref, w_ref, dmix_ref, da_ref, dp_ref, red_ref):
        @pl.when(pl.program_id(0) == 0)
        def _():
            red_ref[...] = jnp.zeros_like(red_ref)

        dx = dx_ref[...]
        mv = mix_ref[...]
        rstd = _rstd(mv)
        mhat = mv * rstd
        gpost = g_ref[1:2, :]
        gate = mod_ref[2:3, :]
        dm, s_post = _norm_gain_bwd(dx, mhat, rstd, gate * gpost)
        red_ref[0:1, :] += gpost * s_post
        red_ref[1:2, :] += gate * s_post
        dmb = dm.astype(BF16)
        dmix_ref[...] = dmb
        dap = _nt(dmb, w_ref[...])
        da_ref[...] = dap[:, 0:ATTN_W].astype(BF16)
        dp_ref[...] = dap[:, ATTN_W:].astype(BF16)

    return pl.pallas_call(
        body, name="mix_bwd", grid=(T // tm,),
        in_specs=[_rows(tm, D_MODEL), _rows(tm, D_MODEL), _full((8, D_MODEL)), _full((8, D_MODEL)),
                  _resident((D_MODEL, D_MODEL))],
        out_specs=[_rows(tm, D_MODEL), _rows(tm, ATTN_W), _rows(tm, POOL_W), _full((8, D_MODEL))],
        out_shape=[_sds((T, D_MODEL), BF16), _sds((T, ATTN_W), BF16), _sds((T, POOL_W), BF16),
                   _sds((8, D_MODEL), F32)],
        compiler_params=_cp("arbitrary"),
    )(dx1, mix, mod8, g8, w_out)


def _attn_bwd(q, kd, vd, lse, dattn, sink_b):
    T = q.shape[0]
    nb = T // BLK
    assert nb % 2 == 0
    npair = nb // 2

    def body(q_ref, do_ref, lse_ref, kp_ref, kc_ref, vp_ref, vc_ref, sk_ref,
             dq_ref, dke_ref, dko_ref, dve_ref, dvo_ref, dsk_ref, carry_k, carry_v):
        i = pl.program_id(0)

        @pl.when(i == 0)
        def _():
            carry_k[...] = jnp.zeros_like(carry_k)
            carry_v[...] = jnp.zeros_like(carry_v)
            dsk_ref[...] = jnp.zeros_like(dsk_ref)

        @pl.when(i < npair)
        def _():
            for j in range(N_HEADS // GROUP):
                lanes = slice(j * 128, (j + 1) * 128)
                parts_k, parts_v = [], []
                for sub in range(2):
                    rows = slice(sub * BLK, (sub + 1) * BLK)
                    valid = _band_mask(2 * i + sub)
                    k_prev = kp_ref[:, lanes] if sub == 0 else kc_ref[0:BLK, lanes]
                    v_prev = vp_ref[:, lanes] if sub == 0 else vc_ref[0:BLK, lanes]
                    kcat = jnp.concatenate([k_prev, kc_ref[rows, lanes]], axis=0)
                    vcat = jnp.concatenate([v_prev, vc_ref[rows, lanes]], axis=0)
                    qs = _stack_heads(q_ref, j, rows)
                    dos = _stack_heads(do_ref, j, rows)
                    lse = _head_row(lse_ref, j, sub * N_HEADS)
                    p = jnp.exp(jnp.where(valid, _nt(kcat, qs), NEG_INF) - lse)
                    dp = _nt(vcat, dos)
                    delta = jnp.sum(p * dp, axis=0, keepdims=True)
                    ds = (p * (dp - delta)).astype(BF16)
                    sink_term = jnp.exp(_head_row(sk_ref, j) - lse) * delta
                    for r in range(GROUP):
                        h = GROUP * j + r
                        dsk_ref[h:h + 1, :] += -jnp.sum(sink_term[:, r * 128:(r + 1) * 128], axis=1, keepdims=True)
                    dq_ref[rows, 2 * j * 128:(2 * j + 2) * 128] = jnp.concatenate(
                        _unstack_heads(_tn(ds, kcat)), axis=1)
                    parts_k.append(_mm(ds, qs))
                    parts_v.append(_mm(p.astype(BF16), dos))
                dko_ref[:, lanes] = carry_k[:, lanes] + parts_k[0][0:BLK]
                dvo_ref[:, lanes] = carry_v[:, lanes] + parts_v[0][0:BLK]
                dke_ref[:, lanes] = parts_k[0][BLK:] + parts_k[1][0:BLK]
                dve_ref[:, lanes] = parts_v[0][BLK:] + parts_v[1][0:BLK]
                carry_k[:, lanes] = parts_k[1][BLK:]
                carry_v[:, lanes] = parts_v[1][BLK:]

        @pl.when(i == npair)
        def _():
            dko_ref[...] = carry_k[...]
            dvo_ref[...] = carry_v[...]

    cur = lambda i: (jnp.minimum(i, npair - 1), 0)
    prev = lambda i: (jnp.minimum(jnp.maximum(2 * i - 1, 0), nb - 1), 0)
    odd = lambda i: (jnp.maximum(i - 1, 0), 0)
    half = _sds((npair * BLK, KVD_W), F32)
    return pl.pallas_call(
        body, name="attn_bwd", grid=(npair + 1,),
        in_specs=[pl.BlockSpec((2 * BLK, ATTN_W), cur), pl.BlockSpec((2 * BLK, ATTN_W), cur),
                  pl.BlockSpec((2 * N_HEADS, 128), cur),
                  pl.BlockSpec((BLK, KVD_W), prev), pl.BlockSpec((2 * BLK, KVD_W), cur),
                  pl.BlockSpec((BLK, KVD_W), prev), pl.BlockSpec((2 * BLK, KVD_W), cur),
                  _full((8, 128))],
        out_specs=[pl.BlockSpec((2 * BLK, ATTN_W), cur), pl.BlockSpec((BLK, KVD_W), cur), pl.BlockSpec((BLK, KVD_W), odd),
                   pl.BlockSpec((BLK, KVD_W), cur), pl.BlockSpec((BLK, KVD_W), odd), _full((8, 128))],
        out_shape=[_sds((T, ATTN_W), F32), half, half, half, half, _sds((8, 128), F32)],
        scratch_shapes=[pltpu.VMEM((BLK, KVD_W), F32), pltpu.VMEM((BLK, KVD_W), F32)],
        compiler_params=_cp("arbitrary"),
    )(q, dattn, lse, kd, kd, vd, vd, sink_b)


def _pool_bwd(dpool, pooled, pool_w, pool_scale):
    T = dpool.shape[0]
    tm = min(T, 1024)
    nbk = T // tm
    ext_rows = tm + HALO

    def body(dp_ref, pl_ref, w_ref, sc_ref, du_ref, dw_ref, dsc_ref, halo):
        i = pl.program_id(0)

        @pl.when(i == 0)
        def _():
            halo[...] = jnp.zeros_like(halo)
            dw_ref[...] = jnp.zeros_like(dw_ref)
            dsc_ref[...] = jnp.zeros_like(dsc_ref)

        blk = nbk - 1 - i
        tpos = (blk * tm + lax.broadcasted_iota(jnp.int32, (tm, 1), 0)).astype(F32)
        for g, w in enumerate(POOL_WINDOWS):
            lanes = slice(g * 128, (g + 1) * 128)
            dp = dp_ref[:, lanes].astype(F32)
            pb = pl_ref[:, lanes]
            wg = w_ref[g].astype(BF16)
            z = _mm(pb, wg)
            dsc_ref[0:1, lanes] += _colsum(dp * z)
            dz = (dp * sc_ref[:, lanes]).astype(BF16)
            dw_ref[g] += _tn(pb, dz)
            dpl = _nt(dz, wg)
            e = dpl / jnp.minimum(tpos + 1.0, float(w))
            s = jnp.concatenate([e, halo[:, lanes]], axis=0)
            halo[:, lanes] = e[0:HALO, :]
            sh = 1
            while sh < w:
                s = s + pltpu.roll(s, ext_rows - sh, 0)
                sh *= 2
            du_ref[:, lanes] = s[0:tm, :] - dpl

    rev = lambda i: (nbk - 1 - i, 0)
    return pl.pallas_call(
        body, name="pool_bwd", grid=(nbk,),
        in_specs=[pl.BlockSpec((tm, POOL_W), rev), pl.BlockSpec((tm, POOL_W), rev),
                  _full((4, 128, 128)), _full((1, POOL_W))],
        out_specs=[pl.BlockSpec((tm, POOL_W), rev), _full((4, 128, 128)), _full((8, POOL_W))],
        out_shape=[_sds((T, POOL_W), F32), _sds((4, 128, 128), F32), _sds((8, POOL_W), F32)],
        scratch_shapes=[pltpu.VMEM((HALO, POOL_W), F32)],
        compiler_params=_cp("arbitrary"),
    )(dpool, pooled, pool_w, pool_scale)


def _interleave_blocks(even, odd):
    parts = []
    for b in range(even.shape[0] // BLK):
        parts += [even[b * BLK:(b + 1) * BLK], odd[b * BLK:(b + 1) * BLK]]
    return jnp.concatenate(parts, axis=0)


def _in_bwd(dq, dk_eo, dv_eo, du, rc, rs1, rs2, x, dx1, mod8, g8, w_in):
    T = x.shape[0]
    tm = min(T, 512)

    def body(dq_ref, dke_ref, dko_ref, dve_ref, dvo_ref, du_ref, c_ref, s1_ref, s2_ref, x_ref, dx1_ref, mod_ref,
             g_ref, w_ref, dx_ref, dproj_ref, red_ref, dbin_ref):
        dk_all = _interleave_blocks(dke_ref[...], dko_ref[...])
        dv_all = _interleave_blocks(dve_ref[...], dvo_ref[...])
        @pl.when(pl.program_id(0) == 0)
        def _():
            red_ref[...] = jnp.zeros_like(red_ref)
            dbin_ref[...] = jnp.zeros_like(dbin_ref)

        c = c_ref[...]
        s1 = s1_ref[...]
        s2 = s2_ref[...]
        dqp = _rot_bwd(dq_ref[...] * (HEAD ** -0.5), jnp.tile(c, (1, 4)), jnp.tile(s1, (1, 4)), jnp.tile(s2, (1, 4)))
        dkp = _rot_bwd(_fold_dup(dk_all), c, s1, s2)
        pieces = ((0, ATTN_W, dqp), (ATTN_W, ATTN_W + KV_W, dkp),
                  (ATTN_W + KV_W, ATTN_W + 2 * KV_W, _fold_dup(dv_all)), (ATTN_W + 2 * KV_W, IN_W, du_ref[...]))
        for lo, hi, val in pieces:
            dbin_ref[0:1, lo:hi] += _colsum(val)
            dproj_ref[:, lo:hi] = val.astype(BF16)
        dh = _mm(dproj_ref[...], w_ref[...])
        xf = x_ref[...]
        rstd = _rstd(xf)
        xhat = xf * rstd
        gpre = g_ref[0:1, :]
        scale1 = 1.0 + mod_ref[1:2, :]
        dxn, s_pre = _norm_gain_bwd(dh, xhat, rstd, scale1 * gpre)
        red_ref[0:1, :] += _colsum(dh)
        red_ref[1:2, :] += gpre * s_pre
        red_ref[2:3, :] += scale1 * s_pre
        dx_ref[...] = dx1_ref[...] + dxn

    return pl.pallas_call(
        body, name="in_bwd", grid=(T // tm,),
        in_specs=[_rows(tm, ATTN_W), *[_rows(tm // 2, KVD_W)] * 4, _rows(tm, POOL_W),
                  _rows(tm, 128), _rows(tm, 128), _rows(tm, 128), _rows(tm, D_MODEL), _rows(tm, D_MODEL),
                  _full((8, D_MODEL)), _full((8, D_MODEL)), _resident((IN_W, D_MODEL))],
        out_specs=[_rows(tm, D_MODEL), _rows(tm, IN_W), _full((8, D_MODEL)), _full((8, IN_W))],
        out_shape=[_sds((T, D_MODEL), F32), _sds((T, IN_W), BF16), _sds((8, D_MODEL), F32), _sds((8, IN_W), F32)],
        compiler_params=_cp("arbitrary"),
    )(dq, *dk_eo, *dv_eo, du, rc, rs1, rs2, x, dx1, mod8, g8, w_in)


def _mod_fwd(c_all, ada_w, ada_b_sh):
    tn = 512

    def body(c_ref, w_ref, b_ref, o_ref):
        cv = c_ref[...]
        ca = (cv * jax.nn.sigmoid(cv)).astype(BF16)
        o_ref[...] = _mm(ca, w_ref[...].astype(BF16)) + b_ref[...]

    return pl.pallas_call(
        body, name="mod_fwd", grid=(2, ADA_SH // tn),
        in_specs=[_full((8, D_MODEL)), pl.BlockSpec((None, D_MODEL, tn), lambda l, j: (l, 0, j)),
                  pl.BlockSpec((None, 1, tn), lambda l, j: (l, 0, j))],
        out_specs=pl.BlockSpec((None, 8, tn), lambda l, j: (l, 0, j)),
        out_shape=_sds((2, 8, ADA_SH), F32),
        compiler_params=_cp("parallel", "parallel"),
    )(c_all, ada_w, ada_b_sh)


def _ada_wgrad(c_all_t, dmod_sh):
    tn = 512

    def body(c_ref, d_ref, o_ref):
        cv = c_ref[...]
        ca = cv * jax.nn.sigmoid(cv)
        o_ref[...] = jnp.dot(ca, d_ref[...], preferred_element_type=F32, precision=lax.Precision.HIGHEST)

    return pl.pallas_call(
        body, name="ada_wgrad", grid=(2, ADA_SH // tn),
        in_specs=[_full((D_MODEL, 8)), pl.BlockSpec((None, 8, tn), lambda l, j: (l, 0, j))],
        out_specs=pl.BlockSpec((None, D_MODEL, tn), lambda l, j: (l, 0, j)),
        out_shape=_sds((2, D_MODEL, ADA_SH), F32),
        compiler_params=_cp("parallel", "parallel"),
    )(c_all_t, dmod_sh)


def _sum_devices(g):
    R = g.shape[1]

    def body(g_ref, o_ref):
        acc = g_ref[0]
        for d in range(1, N_DEV):
            acc = acc + g_ref[d]
        o_ref[...] = acc

    return pl.pallas_call(
        body, name="sum_devices", grid=(1,),
        in_specs=[_full((N_DEV, R, 128))], out_specs=_full((R, 128)), out_shape=_sds((R, 128), F32),
        compiler_params=_cp("arbitrary"),
    )(g)


def _adamw(w, g, m, v, name):
    R, C = w.shape
    tr = R
    for cand in (256, 128, 64, 32, 16, 8):
        if R % cand == 0 and cand * C * 4 <= 2 * 1024 * 1024:
            tr = cand
            break

    def body(w_ref, g_ref, m_ref, v_ref, d_ref, nm_ref, nv_ref):
        gv = g_ref[...]
        mn = ADAM_B1 * m_ref[...] + (1.0 - ADAM_B1) * gv
        vn = ADAM_B2 * v_ref[...] + (1.0 - ADAM_B2) * (gv * gv)
        m_hat = mn / (1.0 - ADAM_B1 ** ADAM_STEP)
        v_hat = vn / (1.0 - ADAM_B2 ** ADAM_STEP)
        d_ref[...] = -ADAM_LR * (m_hat / (jnp.sqrt(v_hat) + ADAM_EPS) + ADAM_WD * w_ref[...])
        nm_ref[...] = mn
        nv_ref[...] = vn

    spec = pl.BlockSpec((tr, C), lambda i: (i, 0))
    out = _sds((R, C), F32)
    return pl.pallas_call(
        body, name=name, grid=(R // tr,),
        in_specs=[spec] * 4, out_specs=[spec] * 3, out_shape=[out] * 3,
        compiler_params=_cp("parallel"),
    )(w, g, m, v)


def _adamw_nd(w, g, m, v, name):
    shape = w.shape
    if w.ndim == 2 and shape[1] < 128:
        view = (1, shape[0] * shape[1])
    else:
        view = (-1, shape[-1])
    outs = _adamw(*[t.reshape(view) for t in (w, g, m, v)], name=name)
    return [o.reshape(shape) for o in outs]


def _coords():
    return lax.axis_index("x"), lax.axis_index("y"), lax.axis_index("c")


def _other_chips(x, y):
    return [(1 - x, y), (x, 1 - y), (1 - x, 1 - y)]


def _allgather8(blk, name):
    m_per, n = blk.shape

    def body(x_ref, out_ref, send_sems, recv_sems, local_sem):
        x, y, c = _coords()
        me, sibling = (x, y, c), (x, y, 1 - c)
        chips = _other_chips(x, y)

        def rows(px, py, pc):
            return out_ref.at[pl.ds((4 * px + 2 * py + pc) * m_per, m_per), :]

        def copy(k, block, to, src=None):
            return pltpu.make_async_remote_copy(
                src_ref=rows(*block) if src is None else src, dst_ref=rows(*block),
                send_sem=send_sems.at[k], recv_sem=recv_sems.at[k], device_id=to, device_id_type=MESH)

        mine = pltpu.make_async_copy(x_ref, rows(*me), local_sem)
        mine.start()
        first = [copy(0, me, sibling, src=x_ref)]
        first += [copy(1 + j, me, (*chip, c), src=x_ref) for j, chip in enumerate(chips)]
        for cp in first:
            cp.start()
        passed = [copy(4 + j, (*chip, c), sibling) for j, chip in enumerate(chips)]
        for j, chip in enumerate(chips):
            copy(1 + j, (*chip, c), me).wait_recv()
            passed[j].start()
        copy(0, sibling, me).wait_recv()
        for j, chip in enumerate(chips):
            copy(4 + j, (*chip, 1 - c), me).wait_recv()
        for cp in first + passed:
            cp.wait_send()
        mine.wait()

    return pl.pallas_call(
        body, name=name,
        out_shape=_sds((N_DEV * m_per, n), blk.dtype),
        in_specs=[pl.BlockSpec(memory_space=pltpu.VMEM)],
        out_specs=pl.BlockSpec(memory_space=pltpu.VMEM),
        scratch_shapes=[pltpu.SemaphoreType.DMA((7,)), pltpu.SemaphoreType.DMA((7,)), pltpu.SemaphoreType.DMA],
        compiler_params=pltpu.CompilerParams(vmem_limit_bytes=VMEM_LIMIT),
    )(blk)


def _row_tile(r, n):
    for cand in range(r, 15, -16):
        if r % cand == 0 and cand % 16 == 0 and cand * n * 4 <= 2 * 1024 * 1024:
            return cand
    return r


def _cast_slot(w, chip, name):
    r, n = w.shape
    tr = _row_tile(r, n)

    def body(chip_ref, w_ref, o_ref):
        o_ref[...] = w_ref[...].astype(BF16)

    grid_spec = pltpu.PrefetchScalarGridSpec(
        num_scalar_prefetch=1, grid=(r // tr,),
        in_specs=[pl.BlockSpec((tr, n), lambda i, ch: (i, 0))],
        out_specs=pl.BlockSpec((None, tr, n), lambda i, ch: (ch[0], i, 0)))
    return pl.pallas_call(
        body, name=name, grid_spec=grid_spec, out_shape=_sds((N_SHARD, r, n), BF16),
        compiler_params=_cp("arbitrary"),
    )(chip, w)


def _join_halves(tots, name):
    nt = len(tots)
    hom = [pl.BlockSpec(memory_space=pl.ANY)] * nt

    def body(*refs):
        outs = refs[nt:2 * nt]
        send_sems, recv_sems = refs[2 * nt:]
        x, y, c = _coords()
        sibling = (x, y, 1 - c)
        cps = []
        for t in range(nt):
            cp = pltpu.make_async_remote_copy(
                src_ref=outs[t].at[c], dst_ref=outs[t].at[c],
                send_sem=send_sems.at[t], recv_sem=recv_sems.at[t], device_id=sibling, device_id_type=MESH)
            cp.start()
            cps.append(cp)
        for t in range(nt):
            pltpu.make_async_remote_copy(
                src_ref=outs[t].at[c], dst_ref=outs[t].at[1 - c],
                send_sem=send_sems.at[t], recv_sem=recv_sems.at[t], device_id=sibling, device_id_type=MESH).wait_recv()
        for cp in cps:
            cp.wait_send()

    return pl.pallas_call(
        body, name=name,
        out_shape=[_sds(t.shape, t.dtype) for t in tots],
        in_specs=hom, out_specs=hom,
        input_output_aliases={t: t for t in range(nt)},
        scratch_shapes=[pltpu.SemaphoreType.DMA((nt,)), pltpu.SemaphoreType.DMA((nt,))],
    )(*tots)


def _pair_sum(g, recv, core, chip, name):
    _, _, r, n = g.shape
    tr = _row_tile(r, n)

    def body(core_ref, chip_ref, g_ref, r_ref, sb_ref, own_ref):
        tot = g_ref[...] + r_ref[...]
        sb_ref[...] = tot.astype(BF16)

        @pl.when(pl.program_id(1) == chip_ref[0])
        def _():
            own_ref[...] = tot

    grid_spec = pltpu.PrefetchScalarGridSpec(
        num_scalar_prefetch=2, grid=(r // tr, N_SHARD),
        in_specs=[pl.BlockSpec((None, None, tr, n), lambda i, s, co, ch: (s, co[0], i, 0)),
                  pl.BlockSpec((None, tr, n), lambda i, s, co, ch: (s, i, 0))],
        out_specs=[pl.BlockSpec((None, tr, n), lambda i, s, co, ch: (s, i, 0)),
                   pl.BlockSpec((tr, n), lambda i, s, co, ch: (i, 0))])
    return pl.pallas_call(
        body, name=name, grid_spec=grid_spec,
        out_shape=[_sds((N_SHARD, r, n), BF16), _sds((r, n), F32)],
        compiler_params=_cp("arbitrary", "arbitrary"),
    )(core, chip, g, recv)


def _chip_sum(own, recv, core, name):
    r, n = own.shape
    tr = _row_tile(r, n)

    def body(core_ref, o_ref, r_ref, t_ref):
        acc = o_ref[...]
        for j in range(3):
            acc = acc + r_ref[j].astype(F32)
        t_ref[...] = acc

    grid_spec = pltpu.PrefetchScalarGridSpec(
        num_scalar_prefetch=1, grid=(r // tr,),
        in_specs=[pl.BlockSpec((tr, n), lambda i, co: (i, 0)), pl.BlockSpec((3, tr, n), lambda i, co: (0, i, 0))],
        out_specs=pl.BlockSpec((None, tr, n), lambda i, co: (co[0], i, 0)))
    return pl.pallas_call(
        body, name=name, grid_spec=grid_spec, out_shape=_sds((2, r, n), F32),
        compiler_params=_cp("arbitrary"),
    )(core, own, recv)


_HBM = pl.BlockSpec(memory_space=pltpu.HBM)
_SEM = pl.BlockSpec(memory_space=pltpu.SEMAPHORE)
_EFFECT = pltpu.SideEffectType.DATAFLOW_SIDE_EFFECTING


def _ici_copies(srcs, dsts, send_sems, recv_sems, send_view, recv_view):
    x, y, c = _coords()
    out = []
    if send_view is None:
        for t in range(len(srcs)):
            r = srcs[t].shape[1] // 2
            out.append(pltpu.make_async_remote_copy(
                src_ref=srcs[t].at[:, pl.ds((1 - c) * r, r)], dst_ref=dsts[t],
                send_sem=send_sems.at[3 * t], recv_sem=recv_sems.at[3 * t],
                device_id=(x, y, 1 - c), device_id_type=MESH))
        return out
    for t in range(len(srcs)):
        for j, chip in enumerate(_other_chips(x, y)):
            out.append(pltpu.make_async_remote_copy(
                src_ref=send_view(srcs[t], chip, j, (x, y), c), dst_ref=recv_view(dsts[t], chip, j, (x, y), c),
                send_sem=send_sems.at[3 * t + j], recv_sem=recv_sems.at[3 * t + j],
                device_id=(*chip, c), device_id_type=MESH))
    return out


def _ici_start(srcs, dsts, after, send_view, recv_view, name):
    nt = len(srcs)
    inplace = dsts is None
    nbuf = nt if inplace else 2 * nt

    def body(*refs):
        send_sems, recv_sems = refs[nbuf + 1], refs[nbuf + 2]
        s_out = refs[nbuf + 3:nbuf + 3 + nt]
        d_out = s_out if inplace else refs[nbuf + 3 + nt:nbuf + 3 + 2 * nt]
        token = refs[-1]
        for cp in _ici_copies(s_out, d_out, send_sems, recv_sems, send_view, recv_view):
            cp.start()
        token[...] = jnp.zeros_like(token)

    bufs = list(srcs) + ([] if inplace else list(dsts))
    res = pl.pallas_call(
        body, name=name,
        out_shape=(pltpu.SemaphoreType.DMA((3 * nt,)), pltpu.SemaphoreType.DMA((3 * nt,)),
                   *[pltpu.HBM(b.shape, b.dtype) for b in bufs], _sds((8, 128), F32)),
        in_specs=[_HBM] * nbuf + [pl.BlockSpec(memory_space=pl.ANY)],
        out_specs=(_SEM, _SEM, *[_HBM] * nbuf, pl.BlockSpec(memory_space=pltpu.VMEM)),
        input_output_aliases={i: 2 + i for i in range(nbuf)},
        compiler_params=pltpu.CompilerParams(has_side_effects=_EFFECT),
    )(*[pltpu.with_memory_space_constraint(b, pltpu.HBM) for b in bufs], after)
    send_sems, recv_sems = res[0], res[1]
    s_thru = list(res[2:2 + nt])
    d_thru = s_thru if inplace else list(res[2 + nt:2 + 2 * nt])
    return send_sems, recv_sems, s_thru, d_thru, res[-1]


def _ici_wait(send_sems, recv_sems, srcs, dsts, after, send_view, recv_view, name):
    nt = len(srcs)
    inplace = dsts is None
    nbuf = nt if inplace else 2 * nt

    def body(*refs):
        send_ref, recv_ref = refs[nbuf], refs[nbuf + 1]
        s_out = refs[nbuf + 3:nbuf + 3 + nt]
        d_out = s_out if inplace else refs[nbuf + 3 + nt:nbuf + 3 + 2 * nt]
        for cp in _ici_copies(s_out, d_out, send_ref, recv_ref, send_view, recv_view):
            cp.wait_send()
            cp.wait_recv()

    bufs = list(srcs) + ([] if inplace else list(dsts))
    res = pl.pallas_call(
        body, name=name,
        out_shape=tuple(pltpu.HBM(b.shape, b.dtype) for b in bufs),
        in_specs=[_HBM] * nbuf + [_SEM, _SEM, pl.BlockSpec(memory_space=pl.ANY)],
        out_specs=tuple([_HBM] * nbuf),
        input_output_aliases={i: i for i in range(nbuf)},
        compiler_params=pltpu.CompilerParams(has_side_effects=_EFFECT),
    )(*bufs, send_sems, recv_sems, after)
    return list(res[:nt]) if inplace else (list(res[:nt]), list(res[nt:]))


def _w_half(buf, chip, c):
    r = buf.shape[1] // 2
    return buf.at[2 * chip[0] + chip[1], pl.ds(c * r, r)]


def _ag_send_view(buf, chip, j, me, c):
    return _w_half(buf, me, c)


def _ag_recv_view(buf, chip, j, me, c):
    return _w_half(buf, me, c)


def _rs_send_view(buf, chip, j, me, c):
    return buf.at[2 * chip[0] + chip[1]]


def _rs_recv_view(buf, chip, j, me, c):
    return buf.at[j]


def _ag_forward(bufs, name):
    nt = len(bufs)
    hom = [pl.BlockSpec(memory_space=pl.ANY)] * nt

    def body(*refs):
        outs = refs[nt:2 * nt]
        send_sems, recv_sems = refs[2 * nt:]
        x, y, c = _coords()
        sibling = (x, y, 1 - c)
        chips = _other_chips(x, y)

        def copy(t, j, hc):
            blk = _w_half(outs[t], chips[j], hc)
            return pltpu.make_async_remote_copy(
                src_ref=blk, dst_ref=blk, send_sem=send_sems.at[t, j], recv_sem=recv_sems.at[t, j],
                device_id=sibling, device_id_type=MESH)

        started = [copy(t, j, c) for t in range(nt) for j in range(3)]
        for cp in started:
            cp.start()
        for t in range(nt):
            for j in range(3):
                copy(t, j, 1 - c).wait_recv()
        for cp in started:
            cp.wait_send()

    return pl.pallas_call(
        body, name=name,
        out_shape=[_sds(b.shape, b.dtype) for b in bufs],
        in_specs=hom, out_specs=hom,
        input_output_aliases={t: t for t in range(nt)},
        scratch_shapes=[pltpu.SemaphoreType.DMA((nt, 3)), pltpu.SemaphoreType.DMA((nt, 3))],
    )(*bufs)


def _rs_swap_begin(grads, after, tag):
    land = [lax.empty((N_SHARD, g.shape[1] // 2, g.shape[2]), g.dtype) for g in grads]
    send_sems, recv_sems, s_thru, d_thru, token = _ici_start(grads, land, after, None, None, name="rs_swapgo_" + tag)
    return dict(sems=(send_sems, recv_sems), grads=s_thru, land=d_thru, tag=tag), token


def _rs_scatter_begin(swap, after):
    tag = swap["tag"]
    x, y, c = _coords()
    core = jnp.reshape(c, (1,)).astype(jnp.int32)
    chip = jnp.reshape(2 * x + y, (1,)).astype(jnp.int32)
    grads, recv = _ici_wait(*swap["sems"], swap["grads"], swap["land"], after, None, None, name="rs_swapend_" + tag)
    sums, owns = [], []
    for t, (g, rv) in enumerate(zip(grads, recv)):
        r = g.shape[1] // 2
        sb, own = _pair_sum(g.reshape(N_SHARD, 2, r, g.shape[2]), rv, core, chip, name=f"rs_pair_{tag}_{t}")
        sums.append(sb)
        owns.append(own)
    land = [lax.empty((3,) + s.shape[1:], s.dtype) for s in sums]
    send_sems, recv_sems, s_thru, d_thru, token = _ici_start(
        sums, land, after, _rs_send_view, _rs_recv_view, name="rs_start_" + tag)
    return dict(sems=(send_sems, recv_sems), sums=s_thru, land=d_thru, owns=owns, core=core, tag=tag), token


def _rs_end(state, after):
    tag = state["tag"]
    _, got = _ici_wait(*state["sems"], state["sums"], state["land"], after, _rs_send_view, _rs_recv_view,
                       name="rs_wait_" + tag)
    tots = [_chip_sum(o, gt, state["core"], name=f"rs_chip_{tag}_{t}")
            for t, (o, gt) in enumerate(zip(state["owns"], got))]
    full = _join_halves(tots, name="rs_join_" + tag)
    return [f.reshape(2 * f.shape[1], f.shape[2]) for f in full]


def _rope_lane_table():
    d = jnp.arange(128) % HEAD
    inv_freq = ROPE_THETA ** (-jnp.arange(0, ROT, 2, dtype=F32) / ROT)
    rot = d < ROT
    rows = [jnp.where(rot, inv_freq[d % (ROT // 2)], 0.0), rot.astype(F32),
            (d < ROT // 2).astype(F32), jnp.logical_and(d >= ROT // 2, rot).astype(F32)]
    return jnp.concatenate([jnp.stack(rows), jnp.zeros((4, 128), F32)], axis=0)


def _pad8(rows):
    return jnp.concatenate([rows, jnp.zeros((8 - rows.shape[0], rows.shape[1]), F32)], axis=0)


def kernel(x, c, positions, ada_w, ada_b, w_in, b_in, sinks, pool_w, pool_scale, w_out, w_gate, w_up, w_down, g_pre_mix, g_post_mix, g_pre_ffn, g_post_ffn, loss_target, m_ada_w, m_ada_b, m_w_in, m_b_in, m_sinks, m_pool_w, m_pool_scale, m_w_out, m_w_gate, m_w_up, m_w_down, m_g_pre_mix, m_g_post_mix, m_g_pre_ffn, m_g_post_ffn, v_ada_w, v_ada_b, v_w_in, v_b_in, v_sinks, v_pool_w, v_pool_scale, v_w_out, v_w_gate, v_w_up, v_w_down, v_g_pre_mix, v_g_post_mix, v_g_pre_ffn, v_g_post_ffn):
    T = x.shape[1]
    n_layers = ada_w.shape[0]
    ax, ay, ac = _coords()
    my_dev = 4 * ax + 2 * ay + ac
    my_chip = 2 * ax + ay
    x0 = x.reshape(T, D_MODEL)
    target = loss_target.reshape(T, D_MODEL)

    c_all = _allgather8(c.reshape(8, 128), name="ag_c").reshape(N_DEV, D_MODEL)
    ada_b_sh = lax.dynamic_slice_in_dim(ada_b, my_chip * ADA_SH, ADA_SH, axis=1).reshape(n_layers, 1, ADA_SH)
    mod_part = _mod_fwd(c_all, ada_w, ada_b_sh)
    mod_all = _allgather8(mod_part.reshape(n_layers * 8, ADA_SH), name="ag_mod")
    mod_all = mod_all.reshape(N_DEV, n_layers, 8, ADA_SH)[0::2]
    mod_mine = lax.dynamic_index_in_dim(mod_all, my_dev, axis=2, keepdims=False)
    mod = jnp.transpose(mod_mine, (1, 0, 2)).reshape(n_layers, 6, D_MODEL)

    chip1 = jnp.reshape(my_chip, (1,)).astype(jnp.int32)

    def tr(t):
        return jnp.transpose(t, (0, 2, 1))

    w_in_t, w_gate_t, w_up_t = tr(w_in), tr(w_gate), tr(w_up)

    def cast_layer(l):
        return [_cast_slot(w[l], chip1, name=f"cast_{nm}{l}")
                for nm, w in (("w_in", w_in_t), ("w_out", w_out), ("w_gate", w_gate_t), ("w_up", w_up_t),
                              ("w_down", w_down))]

    def as_operands(bufs):
        gin, gout, gg, gu, gd = bufs
        return (gin.reshape(IN_W, D_MODEL), gout.reshape(D_MODEL, D_MODEL), gg.reshape(D_FF, D_MODEL),
                gu.reshape(D_FF, D_MODEL), gd.reshape(D_FF, D_MODEL))

    bufs0 = cast_layer(0)
    in_send, in_recv, in_bufs, _, in_token = _ici_start(
        bufs0[:1], None, mod, _ag_send_view, _ag_recv_view, name="ag_start_0_in")
    pos_b = jnp.broadcast_to(positions.reshape(T, 1), (T, 128))
    rc, rs1, rs2 = _rope_tables(pos_b, _rope_lane_table() + in_token[0, 0])
    arrived = _ici_wait(in_send, in_recv, in_bufs, None, rc, _ag_send_view, _ag_recv_view, name="ag_wait_0_in")
    win0 = _ag_forward(arrived, name="ag_fwd_0_in")
    rest_send, rest_recv, rest_bufs, _, ag_token = _ici_start(
        bufs0[1:], None, win0[0], _ag_send_view, _ag_recv_view, name="ag_start_0")
    weights = [None] * n_layers

    saved = []
    xl = x0
    for l in range(n_layers):
        mod8 = _pad8(mod[l])
        if l + 1 < n_layers:
            ag_send, ag_recv, ag_bufs, _, ag_token = _ici_start(
                cast_layer(l + 1), None, ag_token, _ag_send_view, _ag_recv_view, name=f"ag_start_{l + 1}")
        if l == 0 or l + 1 < n_layers:
            mod8 = mod8 + ag_token[0, 0]
        g8 = _pad8(jnp.stack([g_pre_mix[l], g_post_mix[l], g_pre_ffn[l], g_post_ffn[l]]))
        sink_b = jnp.broadcast_to(sinks[l][:, None], (N_HEADS, 128))
        psc = pool_scale[l].reshape(1, POOL_W)
        win = win0[0].reshape(IN_W, D_MODEL) if l == 0 else weights[l][0]
        h, q, k, v, u = _fwd_in(xl, mod8, g8, win, b_in[l].reshape(1, IN_W), rc, rs1, rs2)
        attn, lse = _attn_fwd(q, k, v, sink_b)
        pool, pooled = _pool_fwd(u, pool_w[l], psc)
        if l == 0:
            arrived = _ici_wait(rest_send, rest_recv, rest_bufs, None, pool, _ag_send_view, _ag_recv_view,
                                name="ag_wait_0")
            weights[0] = as_operands(win0 + _ag_forward(arrived, name="ag_fwd_0"))
        win, wout, wg, wu, wd = weights[l]
        if l + 1 < n_layers:
            mix, x1, h2, act, ga, gb, f, x2 = _out_ffn_fwd(attn, pool, xl, wout, mod8, g8, wg, wu, wd)
        else:
            mix, x1, h2, act, ga, gb, f, x2, loss_tile = _out_ffn_fwd(attn, pool, xl, wout, mod8, g8, wg, wu, wd,
                                                                      target=target)
        saved.append(dict(x=xl, h=h, q=q, k=k, v=v, lse=lse, attn=attn, pool=pool, pooled=pooled, mix=mix,
                          x1=x1, h2=h2, act=act, ga=ga, gb=gb, f=f, mod8=mod8, g8=g8, sink_b=sink_b, psc=psc))
        xl = x2
        if l + 1 < n_layers:
            arrived = _ici_wait(ag_send, ag_recv, ag_bufs, None, x2, _ag_send_view, _ag_recv_view,
                                name=f"ag_wait_{l + 1}")
            weights[l + 1] = as_operands(_ag_forward(arrived, name=f"ag_fwd_{l + 1}"))

    dy = xl
    loss = lax.psum(loss_tile[0, 0], ("x", "y", "c"))

    small = [None] * n_layers
    dmod_rows = [None] * n_layers
    reduced = [dict() for _ in range(n_layers)]
    att_swap = None
    dx = dy
    for l in reversed(range(n_layers)):
        s = saved[l]
        win, wout, wg, wu, wd = weights[l]
        if att_swap is not None:
            s = dict(s, mod8=s["mod8"] + att_swap[1][0, 0])
        dx1, df, da, db, red_f = _ffn_bwd(dx, s["f"], s["ga"], s["gb"], s["x1"], s["mod8"], s["g8"], wg, wu, wd)
        token = None
        if att_swap is not None:
            att_scatter = _rs_scatter_begin(att_swap[0], dx1)
            token = att_scatter[1]
        ffn_shards = (N_SHARD, FF_SH, D_MODEL)
        g_wd = _wgrad(s["act"], df, name="wgrad_down", after=token).reshape(ffn_shards)
        g_wg = _wgrad(da, s["h2"], name="wgrad_gate").reshape(ffn_shards)
        g_wu = _wgrad(db, s["h2"], name="wgrad_up").reshape(ffn_shards)
        ffn_swap = _rs_swap_begin([g_wg, g_wu, g_wd], dx1, tag=f"{l}f")
        if att_swap is not None:
            got = _rs_end(att_scatter[0], ffn_swap[1])
            reduced[l + 1].update(w_in=got[0], w_out=got[1])
        s = dict(s, mod8=s["mod8"] + ffn_swap[1][0, 0])
        dmix, dattn, dpool, red_c = _mix_bwd(dx1, s["mix"], s["mod8"], s["g8"], wout)
        g_wout = jnp.concatenate([_wgrad(s["attn"], dmix, name="wgrad_out_a"),
                                  _wgrad(s["pool"], dmix, name="wgrad_out_p")], axis=0)
        ffn_scatter = _rs_scatter_begin(ffn_swap[0], dattn)
        dq, dk_e, dk_o, dv_e, dv_o, dsink = _attn_bwd(s["q"], s["k"], s["v"], s["lse"], dattn,
                                                      s["sink_b"] + ffn_scatter[1][0:1, :])
        du, g_poolw, dpsc = _pool_bwd(dpool, s["pooled"], pool_w[l], s["psc"])
        dx, dproj, red_d, dbin = _in_bwd(dq, (dk_e, dk_o), (dv_e, dv_o), du, rc, rs1, rs2, s["x"], dx1, s["mod8"],
                                         s["g8"], win)
        g_win = _wgrad(dproj, s["h"], name="wgrad_in")
        g_win_sh = g_win.reshape(N_SHARD, IN_SH, D_MODEL)
        got = _rs_end(ffn_scatter[0], dproj)
        reduced[l].update(w_gate=got[0], w_up=got[1], w_down=got[2])
        att_swap = _rs_swap_begin([g_win_sh, g_wout.reshape(N_SHARD, OUT_SH, D_MODEL)], dx, tag=f"{l}a")
        dmod_rows[l] = jnp.concatenate([red_d[0], red_d[1], red_c[0], red_f[2], red_f[3], red_f[0]])
        small[l] = jnp.concatenate([red_d[2], red_c[1], red_f[4], red_f[1], dbin[0], dpsc[0], dsink[:, 0],
                                    jnp.zeros((120,), F32), g_poolw.reshape(-1)])
    grad_x = dx.reshape(1, T, D_MODEL)

    per_layer = small[0].shape[0]
    rows_small = n_layers * per_layer // 128
    rows_mod = n_layers * 6 * D_MODEL // 128
    rows_pad = -(rows_small + rows_mod) % 8
    pack = jnp.concatenate(small + dmod_rows + [jnp.zeros((rows_pad * 128,), F32)]).reshape(-1, 128)
    pack = pack + att_swap[1][0, 0]
    gathered = _allgather8(pack, name="ag_small").reshape(N_DEV, pack.shape[0], 128)
    summed = _sum_devices(gathered)
    att_scatter = _rs_scatter_begin(att_swap[0], summed)
    small_sum = summed[:rows_small].reshape(n_layers, per_layer)
    o = 0
    small_g = {}
    for nm, width in (("g_pre_mix", D_MODEL), ("g_post_mix", D_MODEL), ("g_pre_ffn", D_MODEL),
                      ("g_post_ffn", D_MODEL), ("b_in", IN_W), ("pool_scale", POOL_W), ("sinks", 128),
                      ("pool_w", 4 * 128 * 128)):
        small_g[nm] = small_sum[:, o:o + width]
        o += width
    small_g["sinks"] = small_g["sinks"][:, :N_HEADS]
    small_g["pool_w"] = small_g["pool_w"].reshape(n_layers, 4, 128, 128)
    small_g["ada_b"] = summed[rows_small:rows_small + rows_mod].reshape(n_layers, 6 * D_MODEL)
    dmod_all = gathered[:, rows_small:rows_small + rows_mod].reshape(N_DEV, n_layers, N_SHARD, ADA_SH)
    dmod_sh = lax.dynamic_index_in_dim(dmod_all, my_chip, axis=2, keepdims=False)
    g_ada_w = _ada_wgrad(jnp.transpose(c_all), jnp.transpose(dmod_sh, (1, 0, 2)))

    grads = dict(ada_w=g_ada_w, ada_b=small_g["ada_b"], b_in=small_g["b_in"], sinks=small_g["sinks"],
                 pool_w=small_g["pool_w"], pool_scale=small_g["pool_scale"], g_pre_mix=small_g["g_pre_mix"],
                 g_post_mix=small_g["g_post_mix"], g_pre_ffn=small_g["g_pre_ffn"], g_post_ffn=small_g["g_post_ffn"])
    params = dict(ada_w=(ada_w, m_ada_w, v_ada_w), ada_b=(ada_b, m_ada_b, v_ada_b), w_in=(w_in, m_w_in, v_w_in),
                  b_in=(b_in, m_b_in, v_b_in), sinks=(sinks, m_sinks, v_sinks), pool_w=(pool_w, m_pool_w, v_pool_w),
                  pool_scale=(pool_scale, m_pool_scale, v_pool_scale), w_out=(w_out, m_w_out, v_w_out),
                  w_gate=(w_gate, m_w_gate, v_w_gate), w_up=(w_up, m_w_up, v_w_up),
                  w_down=(w_down, m_w_down, v_w_down), g_pre_mix=(g_pre_mix, m_g_pre_mix, v_g_pre_mix),
                  g_post_mix=(g_post_mix, m_g_post_mix, v_g_post_mix), g_pre_ffn=(g_pre_ffn, m_g_pre_ffn, v_g_pre_ffn),
                  g_post_ffn=(g_post_ffn, m_g_post_ffn, v_g_post_ffn))
    names = list(params)
    updates = {nm: _adamw_nd(*params[nm][:1], grads[nm], *params[nm][1:], name="adamw_" + nm) for nm in grads}

    got = _rs_end(att_scatter[0], updates["ada_w"][0])
    reduced[0].update(w_in=got[0], w_out=got[1])
    for nm in ("w_in", "w_out", "w_gate", "w_up", "w_down"):
        g = jnp.stack([reduced[l][nm] for l in range(n_layers)])
        if nm in ("w_in", "w_gate", "w_up"):
            upd = _adamw_nd(tr(params[nm][0]), g, tr(params[nm][1]), tr(params[nm][2]), name="adamw_" + nm)
            grads[nm], updates[nm] = tr(g), [tr(u) for u in upd]
        else:
            grads[nm], updates[nm] = g, _adamw_nd(params[nm][0], g, *params[nm][1:], name="adamw_" + nm)
    return (loss, grad_x, *[grads[nm] for nm in names], *[updates[nm][0] for nm in names],
            *[updates[nm][1] for nm in names], *[updates[nm][2] for nm in names])
```

```python
import functools

import jax
import jax.numpy as jnp
from jax import lax
from jax.experimental import pallas as pl
from jax.experimental.pallas import tpu as pltpu

F32 = jnp.float32
BF16 = jnp.bfloat16
MESH = pl.DeviceIdType.MESH

D_MODEL = 1024
ATTN_W = 512
KV_W = 128
KVD_W = 256
POOL_W = 512
IN_W = 1280
D_FF = 2816
N_SHARD = 4
FF_SH = D_FF // N_SHARD
IN_SH = IN_W // N_SHARD
OUT_SH = D_MODEL // N_SHARD
ADA_SH = 6 * D_MODEL // N_SHARD
HEAD = 64
N_HEADS = 8
GROUP = 4
BLK = 128
POOL_WINDOWS = (2, 4, 8, 16)
HALO = 16
ROT = 16
ROPE_THETA = 500000.0
EPS = 1e-6
NEG_INF = -1e30
N_DEV = 8

ADAM_LR = 0.001
ADAM_B1 = 0.9
ADAM_B2 = 0.999
ADAM_EPS = 1e-08
ADAM_WD = 0.01
ADAM_STEP = 10

VMEM_LIMIT = 48 * 1024 * 1024
FFN_VMEM_LIMIT = 60 * 1024 * 1024
ATTN_FWD_BLOCKS = 4
WGRAD_TOKENS = 2048


def _cp(*sem, vmem=VMEM_LIMIT):
    return pltpu.CompilerParams(dimension_semantics=sem, vmem_limit_bytes=vmem)


def _full(shape):
    nd = len(shape)
    return pl.BlockSpec(shape, lambda *_: (0,) * nd)


def _resident(shape):
    nd = len(shape)
    return pl.BlockSpec(shape, lambda *_: (0,) * nd, pipeline_mode=pl.Buffered(1))


def _rows(tm, ncol):
    return pl.BlockSpec((tm, ncol), lambda i: (i, 0))


def _sds(shape, dtype):
    return jax.ShapeDtypeStruct(shape, dtype)


def _nt(a, b):
    return lax.dot_general(a, b, (((1,), (1,)), ((), ())), preferred_element_type=F32)


def _tn(a, b):
    return lax.dot_general(a, b, (((0,), (0,)), ((), ())), preferred_element_type=F32)


def _mm(a, b):
    return jnp.dot(a, b, preferred_element_type=F32)


def _rstd(x):
    return lax.rsqrt(jnp.mean(x * x, axis=-1, keepdims=True) + EPS)


def _colsum(x):
    return jnp.sum(x, axis=0, keepdims=True)


def _norm_gain_bwd(dy, xhat, rstd, gain):
    p = dy * xhat
    dx = rstd * (dy * gain - xhat * jnp.mean(p * gain, axis=-1, keepdims=True))
    return dx, _colsum(p)


def _rope_tables(pos_b, lane_tab):
    T = pos_b.shape[0]
    tm = min(T, 1024)

    def body(pos_ref, tab_ref, c_ref, s1_ref, s2_ref):
        ang = pos_ref[...].astype(F32) * tab_ref[0:1, :]
        cs = jnp.cos(ang)
        sn = jnp.sin(ang)
        m_rot = tab_ref[1:2, :]
        c_ref[...] = cs * m_rot + (1.0 - m_rot)
        s1_ref[...] = -sn * tab_ref[2:3, :]
        s2_ref[...] = sn * tab_ref[3:4, :]

    out = _sds((T, 128), F32)
    return pl.pallas_call(
        body, name="rope_tables", grid=(T // tm,),
        in_specs=[_rows(tm, 128), _full((8, 128))],
        out_specs=[_rows(tm, 128)] * 3, out_shape=[out] * 3,
        compiler_params=_cp("parallel"),
    )(pos_b, lane_tab)


def _rot_fwd(t, c, s1, s2):
    w = t.shape[-1]
    return t * c + pltpu.roll(t, w - 8, 1) * s1 + pltpu.roll(t, 8, 1) * s2


def _rot_bwd(d, c, s1, s2):
    w = d.shape[-1]
    return d * c + pltpu.roll(d * s1, 8, 1) + pltpu.roll(d * s2, w - 8, 1)


def _store_dup(ref, t):
    low = lax.broadcasted_iota(jnp.int32, t.shape, 1) < HEAD
    sw = pltpu.roll(t, HEAD, 1)
    ref[:, 0:128] = jnp.where(low, t, sw).astype(BF16)
    ref[:, 128:256] = jnp.where(low, sw, t).astype(BF16)


def _fold_dup(d):
    low = lax.broadcasted_iota(jnp.int32, (d.shape[0], 128), 1) < HEAD
    d0 = d[:, 0:128]
    d1 = d[:, 128:256]
    return jnp.where(low, d0 + pltpu.roll(d0, HEAD, 1), d1 + pltpu.roll(d1, HEAD, 1))


def _fwd_in(x, mod8, g8, w_in, b_in, rc, rs1, rs2):
    T = x.shape[0]
    tm = min(T, 1024)

    def body(x_ref, mod_ref, g_ref, w_ref, b_ref, c_ref, s1_ref, s2_ref,
             h_ref, q_ref, k_ref, v_ref, u_ref):
        xf = x_ref[...]
        h = (xf * _rstd(xf) * g_ref[0:1, :]) * (1.0 + mod_ref[1:2, :]) + mod_ref[0:1, :]
        hb = h.astype(BF16)
        h_ref[...] = hb
        c = c_ref[...]
        s1 = s1_ref[...]
        s2 = s2_ref[...]
        proj = _nt(hb, w_ref[...]) + b_ref[...]
        q = _rot_fwd(proj[:, 0:ATTN_W], jnp.tile(c, (1, 4)), jnp.tile(s1, (1, 4)), jnp.tile(s2, (1, 4)))
        q_ref[...] = (q * (HEAD ** -0.5)).astype(BF16)
        _store_dup(k_ref, _rot_fwd(proj[:, ATTN_W:ATTN_W + KV_W], c, s1, s2))
        _store_dup(v_ref, proj[:, ATTN_W + KV_W:ATTN_W + 2 * KV_W])
        u_ref[...] = proj[:, ATTN_W + 2 * KV_W:IN_W]

    return pl.pallas_call(
        body, name="fwd_in", grid=(T // tm,),
        in_specs=[_rows(tm, D_MODEL), _full((8, D_MODEL)), _full((8, D_MODEL)),
                  _resident((IN_W, D_MODEL)), _full((1, IN_W)),
                  _rows(tm, 128), _rows(tm, 128), _rows(tm, 128)],
        out_specs=[_rows(tm, D_MODEL), _rows(tm, ATTN_W), _rows(tm, KVD_W), _rows(tm, KVD_W), _rows(tm, POOL_W)],
        out_shape=[_sds((T, D_MODEL), BF16), _sds((T, ATTN_W), BF16), _sds((T, KVD_W), BF16),
                   _sds((T, KVD_W), BF16), _sds((T, POOL_W), F32)],
        compiler_params=_cp("parallel"),
    )(x, mod8, g8, w_in, b_in, rc, rs1, rs2)


def _band_mask(n):
    kk = lax.broadcasted_iota(jnp.int32, (2 * BLK, BLK), 0)
    qi = lax.broadcasted_iota(jnp.int32, (2 * BLK, BLK), 1)
    first = jnp.where(n > 0, 0, 2 * BLK)
    in_prev = jnp.logical_and(kk < BLK, kk > qi + first)
    in_cur = jnp.logical_and(kk >= BLK, (kk - BLK) <= qi)
    one = jnp.logical_or(in_prev, in_cur)
    return jnp.concatenate([one] * GROUP, axis=1)


def _head_row(ref, j, base=0):
    return jnp.concatenate([ref[base + GROUP * j + r:base + GROUP * j + r + 1, :] for r in range(GROUP)], axis=1)


def _stack_heads(x_ref, j, rows=slice(None)):
    low = lax.broadcasted_iota(jnp.int32, (BLK, 128), 1) < HEAD
    parts = []
    for gp in (2 * j, 2 * j + 1):
        x2 = x_ref[rows, gp * 128:(gp + 1) * 128]
        parts.append(jnp.where(low, x2, jnp.zeros_like(x2)))
        parts.append(jnp.where(low, jnp.zeros_like(x2), x2))
    return jnp.concatenate(parts, axis=0)


def _unstack_heads(o):
    low = lax.broadcasted_iota(jnp.int32, (BLK, 128), 1) < HEAD
    return [jnp.where(low, o[0:BLK], o[BLK:2 * BLK]), jnp.where(low, o[2 * BLK:3 * BLK], o[3 * BLK:4 * BLK])]


def _attn_fwd(q, kd, vd, sink_b):
    T = q.shape[0]
    nb = T // BLK
    nq = ATTN_FWD_BLOCKS if nb % ATTN_FWD_BLOCKS == 0 else 2
    assert nb % nq == 0

    def body(q_ref, kp_ref, kc_ref, vp_ref, vc_ref, sk_ref, o_ref, lse_ref):
        for sub in range(nq):
            rows = slice(sub * BLK, (sub + 1) * BLK)
            before = slice((sub - 1) * BLK, sub * BLK)
            valid = _band_mask(nq * pl.program_id(0) + sub)
            for j in range(N_HEADS // GROUP):
                lanes = slice(j * 128, (j + 1) * 128)
                k_prev = kp_ref[:, lanes] if sub == 0 else kc_ref[before, lanes]
                v_prev = vp_ref[:, lanes] if sub == 0 else vc_ref[before, lanes]
                kcat = jnp.concatenate([k_prev, kc_ref[rows, lanes]], axis=0)
                vcat = jnp.concatenate([v_prev, vc_ref[rows, lanes]], axis=0)
                s = jnp.where(valid, _nt(kcat, _stack_heads(q_ref, j, rows)), NEG_INF)
                sk = _head_row(sk_ref, j)
                m = jnp.maximum(jnp.max(s, axis=0, keepdims=True), sk)
                p = jnp.exp(s - m)
                den = jnp.sum(p, axis=0, keepdims=True) + jnp.exp(sk - m)
                p = p * (1.0 / den)
                o = _tn(p.astype(BF16), vcat)
                o_ref[rows, 2 * j * 128:(2 * j + 2) * 128] = jnp.concatenate(_unstack_heads(o), axis=1).astype(BF16)
                lse = m + jnp.log(den)
                for r in range(GROUP):
                    h = sub * N_HEADS + GROUP * j + r
                    lse_ref[h:h + 1, :] = lse[:, r * 128:(r + 1) * 128]

    prev = lambda i: (jnp.maximum(nq * i - 1, 0), 0)
    cur = lambda i: (i, 0)
    return pl.pallas_call(
        body, name="attn_fwd", grid=(nb // nq,),
        in_specs=[pl.BlockSpec((nq * BLK, ATTN_W), cur),
                  pl.BlockSpec((BLK, KVD_W), prev), pl.BlockSpec((nq * BLK, KVD_W), cur),
                  pl.BlockSpec((BLK, KVD_W), prev), pl.BlockSpec((nq * BLK, KVD_W), cur),
                  _full((8, 128))],
        out_specs=[pl.BlockSpec((nq * BLK, ATTN_W), cur), pl.BlockSpec((nq * N_HEADS, 128), cur)],
        out_shape=[_sds((T, ATTN_W), BF16), _sds((nb * N_HEADS, 128), F32)],
        compiler_params=_cp("parallel"),
    )(q, kd, kd, vd, vd, sink_b)


def _pool_fwd(u, pool_w, pool_scale):
    T = u.shape[0]
    tm = min(T, 1024)

    def body(u_ref, w_ref, sc_ref, out_ref, pooled_ref, halo):
        i = pl.program_id(0)

        @pl.when(i == 0)
        def _():
            halo[...] = jnp.zeros_like(halo)

        ub = u_ref[...]
        ext = jnp.concatenate([halo[...], ub], axis=0)
        halo[...] = ub[tm - HALO:, :]
        tpos = (i * tm + lax.broadcasted_iota(jnp.int32, (tm, 1), 0)).astype(F32)
        for g, w in enumerate(POOL_WINDOWS):
            lanes = slice(g * 128, (g + 1) * 128)
            s = ext[:, lanes]
            sh = 1
            while sh < w:
                s = s + pltpu.roll(s, sh, 0)
                sh *= 2
            cnt = jnp.minimum(tpos + 1.0, float(w))
            pb = (s[HALO:, :] / cnt - ub[:, lanes]).astype(BF16)
            z = _mm(pb, w_ref[g].astype(BF16))
            out_ref[:, lanes] = (z * sc_ref[:, lanes]).astype(BF16)
            pooled_ref[:, lanes] = pb

    return pl.pallas_call(
        body, name="pool_fwd", grid=(T // tm,),
        in_specs=[_rows(tm, POOL_W), _full((4, 128, 128)), _full((1, POOL_W))],
        out_specs=[_rows(tm, POOL_W), _rows(tm, POOL_W)],
        out_shape=[_sds((T, POOL_W), BF16), _sds((T, POOL_W), BF16)],
        scratch_shapes=[pltpu.VMEM((HALO, POOL_W), F32)],
        compiler_params=_cp("arbitrary"),
    )(u, pool_w, pool_scale)


FF_CHUNKS = ((0, 1024), (1024, 2048), (2048, D_FF))


def _out_ffn_fwd(attn, pool, x, w_out, mod8, g8, wg, wu, wd, target=None):
    T = x.shape[0]
    tm = min(T, 256)
    last = target is not None

    def body(*refs):
        a_ref, p_ref, xin_ref, wo_ref, mod_ref, g_ref, wg_ref, wu_ref, wd_ref = refs[:9]
        t_ref = refs[9] if last else None
        mix_ref, x1_ref, h_ref, act_ref, ga_ref, gb_ref, f_ref, x2_ref = refs[9 + last:17 + last]
        mix = _mm(a_ref[...], wo_ref[0:ATTN_W, :]) + _mm(p_ref[...], wo_ref[ATTN_W:, :])
        mix_ref[...] = mix
        xf = xin_ref[...] + mod_ref[2:3, :] * (mix * _rstd(mix) * g_ref[1:2, :])
        x1_ref[...] = xf
        h = (xf * _rstd(xf) * g_ref[2:3, :]) * (1.0 + mod_ref[4:5, :]) + mod_ref[3:4, :]
        hb = h.astype(BF16)
        h_ref[...] = hb
        f = jnp.zeros((tm, D_MODEL), F32)
        for lo, hi in FF_CHUNKS:
            a = _nt(hb, wg_ref[lo:hi, :])
            b = _nt(hb, wu_ref[lo:hi, :])
            sig = jax.nn.sigmoid(a)
            sl = a * sig
            act = (sl * b).astype(BF16)
            act_ref[:, lo:hi] = act
            ga_ref[:, lo:hi] = (b * (sig * (1.0 + a * (1.0 - sig)))).astype(BF16)
            gb_ref[:, lo:hi] = sl.astype(BF16)
            f = f + _mm(act, wd_ref[lo:hi, :])
        f_ref[...] = f
        x2 = xf + mod_ref[5:6, :] * (f * _rstd(f) * g_ref[3:4, :])
        if not last:
            x2_ref[...] = x2
        else:
            loss_ref = refs[18]

            @pl.when(pl.program_id(0) == 0)
            def _():
                loss_ref[...] = jnp.zeros_like(loss_ref)

            e = x2 - t_ref[...]
            x2_ref[...] = e * (1.0 / D_MODEL)
            loss_ref[...] += 0.5 * jnp.sum(jnp.mean(e * e, axis=-1, keepdims=True), axis=0, keepdims=True)

    act_shape = _sds((T, D_FF), BF16)
    wide = _sds((T, D_MODEL), F32)
    weights = [_resident((D_FF, D_MODEL))] * 3
    return pl.pallas_call(
        body, name="out_ffn_fwd_loss" if last else "out_ffn_fwd", grid=(T // tm,),
        in_specs=[_rows(tm, ATTN_W), _rows(tm, POOL_W), _rows(tm, D_MODEL), _resident((D_MODEL, D_MODEL)),
                  _full((8, D_MODEL)), _full((8, D_MODEL)), *weights]
        + ([_rows(tm, D_MODEL)] if last else []),
        out_specs=[_rows(tm, D_MODEL), _rows(tm, D_MODEL), _rows(tm, D_MODEL), _rows(tm, D_FF), _rows(tm, D_FF),
                   _rows(tm, D_FF), _rows(tm, D_MODEL), _rows(tm, D_MODEL)] + ([_full((8, 128))] if last else []),
        out_shape=[wide, wide, _sds((T, D_MODEL), BF16), act_shape, act_shape, act_shape, wide, wide]
        + ([_sds((8, 128), F32)] if last else []),
        compiler_params=_cp("arbitrary" if last else "parallel", vmem=FFN_VMEM_LIMIT),
    )(attn, pool, x, w_out, mod8, g8, wg, wu, wd, *([target] if last else []))


def _ffn_bwd(dx2, f, ga, gb, x1, mod8, g8, wg, wu, wd):
    T = dx2.shape[0]
    tm = min(T, 256)

    def body(dx_ref, f_ref, ga_ref, gb_ref, x_ref, mod_ref, g_ref, wg_ref, wu_ref, wd_ref,
             dx1_ref, df_ref, da_ref, db_ref, red_ref):
        @pl.when(pl.program_id(0) == 0)
        def _():
            red_ref[...] = jnp.zeros_like(red_ref)

        dx = dx_ref[...]
        fv = f_ref[...]
        rstd = _rstd(fv)
        fhat = fv * rstd
        gpost = g_ref[3:4, :]
        gate = mod_ref[5:6, :]
        df, s_post = _norm_gain_bwd(dx, fhat, rstd, gate * gpost)
        red_ref[0:1, :] += gpost * s_post
        red_ref[1:2, :] += gate * s_post
        dfb = df.astype(BF16)
        df_ref[...] = dfb
        dh = jnp.zeros((tm, D_MODEL), F32)
        for lo, hi in FF_CHUNKS:
            dact = _nt(dfb, wd_ref[lo:hi, :])
            da = (dact * ga_ref[:, lo:hi].astype(F32)).astype(BF16)
            db = (dact * gb_ref[:, lo:hi].astype(F32)).astype(BF16)
            da_ref[:, lo:hi] = da
            db_ref[:, lo:hi] = db
            dh = dh + _mm(da, wg_ref[lo:hi, :]) + _mm(db, wu_ref[lo:hi, :])
        xf = x_ref[...]
        rstd1 = _rstd(xf)
        xhat = xf * rstd1
        gpre = g_ref[2:3, :]
        scale1 = 1.0 + mod_ref[4:5, :]
        dxn, s_pre = _norm_gain_bwd(dh, xhat, rstd1, scale1 * gpre)
        red_ref[2:3, :] += _colsum(dh)
        red_ref[3:4, :] += gpre * s_pre
        red_ref[4:5, :] += scale1 * s_pre
        dx1_ref[...] = dx + dxn

    act_shape = _sds((T, D_FF), BF16)
    return pl.pallas_call(
        body, name="ffn_bwd", grid=(T // tm,),
        in_specs=[_rows(tm, D_MODEL), _rows(tm, D_MODEL), _rows(tm, D_FF), _rows(tm, D_FF), _rows(tm, D_MODEL),
                  _full((8, D_MODEL)), _full((8, D_MODEL)),
                  _resident((D_FF, D_MODEL)), _resident((D_FF, D_MODEL)), _resident((D_FF, D_MODEL))],
        out_specs=[_rows(tm, D_MODEL), _rows(tm, D_MODEL), _rows(tm, D_FF), _rows(tm, D_FF), _full((8, D_MODEL))],
        out_shape=[_sds((T, D_MODEL), F32), _sds((T, D_MODEL), BF16), act_shape, act_shape, _sds((8, D_MODEL), F32)],
        compiler_params=_cp("arbitrary"),
    )(dx2, f, ga, gb, x1, mod8, g8, wg, wu, wd)


def _wgrad(a, b, name, after=None):
    T, K = a.shape
    N = b.shape[1]
    tt = min(T, WGRAD_TOKENS)
    tk = next(c for c in (1408, 640, 512, 256, 128) if K % c == 0)

    def body(a_ref, b_ref, *rest):
        o_ref = rest[-1]

        @pl.when(pl.program_id(1) == 0)
        def _():
            o_ref[...] = jnp.zeros_like(o_ref)

        o_ref[...] += _tn(a_ref[...], b_ref[...])

    extra = [] if after is None else [after]
    return pl.pallas_call(
        body, name=name, grid=(K // tk, T // tt),
        in_specs=[pl.BlockSpec((tt, tk), lambda i, t: (t, i)), pl.BlockSpec((tt, N), lambda i, t: (t, 0))]
        + [pl.BlockSpec(memory_space=pl.ANY)] * len(extra),
        out_specs=pl.BlockSpec((tk, N), lambda i, t: (i, 0)),
        out_shape=_sds((K, N), F32),
        compiler_params=_cp("parallel", "arbitrary"),
    )(a, b, *extra)


def _mix_bwd(dx1, mix, mod8, g8, w_out):
    T = dx1.shape[0]
    tm = min(T, 1024)

    def body(dx_ref, mix_ref, mod_ref, g_ref, w_ref, dmix_ref, da_ref, dp_ref, red_ref):
        @pl.when(pl.program_id(0) == 0)
        def _():
            red_ref[...] = jnp.zeros_like(red_ref)

        dx = dx_ref[...]
        mv = mix_ref[...]
        rstd = _rstd(mv)
        mhat = mv * rstd
        gpost = g_ref[1:2, :]
        gate = mod_ref[2:3, :]
        dm, s_post = _norm_gain_bwd(dx, mhat, rstd, gate * gpost)
        red_ref[0:1, :] += gpost * s_post
        red_ref[1:2, :] += gate * s_post
        dmb = dm.astype(BF16)
        dmix_ref[...] = dmb
        dap = _nt(dmb, w_ref[...])
        da_ref[...] = dap[:, 0:ATTN_W].astype(BF16)
        dp_ref[...] = dap[:, ATTN_W:].astype(BF16)

    return pl.pallas_call(
        body, name="mix_bwd", grid=(T // tm,),
        in_specs=[_rows(tm, D_MODEL), _rows(tm, D_MODEL), _full((8, D_MODEL)), _full((8, D_MODEL)),
                  _resident((D_MODEL, D_MODEL))],
        out_specs=[_rows(tm, D_MODEL), _rows(tm, ATTN_W), _rows(tm, POOL_W), _full((8, D_MODEL))],
        out_shape=[_sds((T, D_MODEL), BF16), _sds((T, ATTN_W), BF16), _sds((T, POOL_W), BF16),
                   _sds((8, D_MODEL), F32)],
        compiler_params=_cp("arbitrary"),
    )(dx1, mix, mod8, g8, w_out)


def _attn_bwd(q, kd, vd, lse, dattn, sink_b):
    T = q.shape[0]
    nb = T // BLK
    assert nb % 2 == 0
    npair = nb // 2

    def body(q_ref, do_ref, lse_ref, kp_ref, kc_ref, vp_ref, vc_ref, sk_ref,
             dq_ref, dke_ref, dko_ref, dve_ref, dvo_ref, dsk_ref, carry_k, carry_v):
        i = pl.program_id(0)

        @pl.when(i == 0)
        def _():
            carry_k[...] = jnp.zeros_like(carry_k)
            carry_v[...] = jnp.zeros_like(carry_v)
            dsk_ref[...] = jnp.zeros_like(dsk_ref)

        @pl.when(i < npair)
        def _():
            for j in range(N_HEADS // GROUP):
                lanes = slice(j * 128, (j + 1) * 128)
                parts_k, parts_v = [], []
                for sub in range(2):
                    rows = slice(sub * BLK, (sub + 1) * BLK)
                    valid = _band_mask(2 * i + sub)
                    k_prev = kp_ref[:, lanes] if sub == 0 else kc_ref[0:BLK, lanes]
                    v_prev = vp_ref[:, lanes] if sub == 0 else vc_ref[0:BLK, lanes]
                    kcat = jnp.concatenate([k_prev, kc_ref[rows, lanes]], axis=0)
                    vcat = jnp.concatenate([v_prev, vc_ref[rows, lanes]], axis=0)
                    qs = _stack_heads(q_ref, j, rows)
                    dos = _stack_heads(do_ref, j, rows)
                    lse = _head_row(lse_ref, j, sub * N_HEADS)
                    p = jnp.exp(jnp.where(valid, _nt(kcat, qs), NEG_INF) - lse)
                    dp = _nt(vcat, dos)
                    delta = jnp.sum(p * dp, axis=0, keepdims=True)
                    ds = (p * (dp - delta)).astype(BF16)
                    sink_term = jnp.exp(_head_row(sk_ref, j) - lse) * delta
                    for r in range(GROUP):
                        h = GROUP * j + r
                        dsk_ref[h:h + 1, :] += -jnp.sum(sink_term[:, r * 128:(r + 1) * 128], axis=1, keepdims=True)
                    dq_ref[rows, 2 * j * 128:(2 * j + 2) * 128] = jnp.concatenate(
                        _unstack_heads(_tn(ds, kcat)), axis=1)
                    parts_k.append(_mm(ds, qs))
                    parts_v.append(_mm(p.astype(BF16), dos))
                dko_ref[:, lanes] = carry_k[:, lanes] + parts_k[0][0:BLK]
                dvo_ref[:, lanes] = carry_v[:, lanes] + parts_v[0][0:BLK]
                dke_ref[:, lanes] = parts_k[0][BLK:] + parts_k[1][0:BLK]
                dve_ref[:, lanes] = parts_v[0][BLK:] + parts_v[1][0:BLK]
                carry_k[:, lanes] = parts_k[1][BLK:]
                carry_v[:, lanes] = parts_v[1][BLK:]

        @pl.when(i == npair)
        def _():
            dko_ref[...] = carry_k[...]
            dvo_ref[...] = carry_v[...]

    cur = lambda i: (jnp.minimum(i, npair - 1), 0)
    prev = lambda i: (jnp.minimum(jnp.maximum(2 * i - 1, 0), nb - 1), 0)
    odd = lambda i: (jnp.maximum(i - 1, 0), 0)
    half = _sds((npair * BLK, KVD_W), F32)
    return pl.pallas_call(
        body, name="attn_bwd", grid=(npair + 1,),
        in_specs=[pl.BlockSpec((2 * BLK, ATTN_W), cur), pl.BlockSpec((2 * BLK, ATTN_W), cur),
                  pl.BlockSpec((2 * N_HEADS, 128), cur),
                  pl.BlockSpec((BLK, KVD_W), prev), pl.BlockSpec((2 * BLK, KVD_W), cur),
                  pl.BlockSpec((BLK, KVD_W), prev), pl.BlockSpec((2 * BLK, KVD_W), cur),
                  _full((8, 128))],
        out_specs=[pl.BlockSpec((2 * BLK, ATTN_W), cur), pl.BlockSpec((BLK, KVD_W), cur), pl.BlockSpec((BLK, KVD_W), odd),
                   pl.BlockSpec((BLK, KVD_W), cur), pl.BlockSpec((BLK, KVD_W), odd), _full((8, 128))],
        out_shape=[_sds((T, ATTN_W), F32), half, half, half, half, _sds((8, 128), F32)],
        scratch_shapes=[pltpu.VMEM((BLK, KVD_W), F32), pltpu.VMEM((BLK, KVD_W), F32)],
        compiler_params=_cp("arbitrary"),
    )(q, dattn, lse, kd, kd, vd, vd, sink_b)


def _pool_bwd(dpool, pooled, pool_w, pool_scale):
    T = dpool.shape[0]
    tm = min(T, 1024)
    nbk = T // tm
    ext_rows = tm + HALO

    def body(dp_ref, pl_ref, w_ref, sc_ref, du_ref, dw_ref, dsc_ref, halo):
        i = pl.program_id(0)

        @pl.when(i == 0)
        def _():
            halo[...] = jnp.zeros_like(halo)
            dw_ref[...] = jnp.zeros_like(dw_ref)
            dsc_ref[...] = jnp.zeros_like(dsc_ref)

        blk = nbk - 1 - i
        tpos = (blk * tm + lax.broadcasted_iota(jnp.int32, (tm, 1), 0)).astype(F32)
        for g, w in enumerate(POOL_WINDOWS):
            lanes = slice(g * 128, (g + 1) * 128)
            dp = dp_ref[:, lanes].astype(F32)
            pb = pl_ref[:, lanes]
            wg = w_ref[g].astype(BF16)
            z = _mm(pb, wg)
            dsc_ref[0:1, lanes] += _colsum(dp * z)
            dz = (dp * sc_ref[:, lanes]).astype(BF16)
            dw_ref[g] += _tn(pb, dz)
            dpl = _nt(dz, wg)
            e = dpl / jnp.minimum(tpos + 1.0, float(w))
            s = jnp.concatenate([e, halo[:, lanes]], axis=0)
            halo[:, lanes] = e[0:HALO, :]
            sh = 1
            while sh < w:
                s = s + pltpu.roll(s, ext_rows - sh, 0)
                sh *= 2
            du_ref[:, lanes] = s[0:tm, :] - dpl

    rev = lambda i: (nbk - 1 - i, 0)
    return pl.pallas_call(
        body, name="pool_bwd", grid=(nbk,),
        in_specs=[pl.BlockSpec((tm, POOL_W), rev), pl.BlockSpec((tm, POOL_W), rev),
                  _full((4, 128, 128)), _full((1, POOL_W))],
        out_specs=[pl.BlockSpec((tm, POOL_W), rev), _full((4, 128, 128)), _full((8, POOL_W))],
        out_shape=[_sds((T, POOL_W), F32), _sds((4, 128, 128), F32), _sds((8, POOL_W), F32)],
        scratch_shapes=[pltpu.VMEM((HALO, POOL_W), F32)],
        compiler_params=_cp("arbitrary"),
    )(dpool, pooled, pool_w, pool_scale)


def _interleave_blocks(even, odd):
    parts = []
    for b in range(even.shape[0] // BLK):
        parts += [even[b * BLK:(b + 1) * BLK], odd[b * BLK:(b + 1) * BLK]]
    return jnp.concatenate(parts, axis=0)


def _in_bwd(dq, dk_eo, dv_eo, du, rc, rs1, rs2, x, dx1, mod8, g8, w_in):
    T = x.shape[0]
    tm = min(T, 512)

    def body(dq_ref, dke_ref, dko_ref, dve_ref, dvo_ref, du_ref, c_ref, s1_ref, s2_ref, x_ref, dx1_ref, mod_ref,
             g_ref, w_ref, dx_ref, dproj_ref, red_ref, dbin_ref):
        dk_all = _interleave_blocks(dke_ref[...], dko_ref[...])
        dv_all = _interleave_blocks(dve_ref[...], dvo_ref[...])
        @pl.when(pl.program_id(0) == 0)
        def _():
            red_ref[...] = jnp.zeros_like(red_ref)
            dbin_ref[...] = jnp.zeros_like(dbin_ref)

        c = c_ref[...]
        s1 = s1_ref[...]
        s2 = s2_ref[...]
        dqp = _rot_bwd(dq_ref[...] * (HEAD ** -0.5), jnp.tile(c, (1, 4)), jnp.tile(s1, (1, 4)), jnp.tile(s2, (1, 4)))
        dkp = _rot_bwd(_fold_dup(dk_all), c, s1, s2)
        pieces = ((0, ATTN_W, dqp), (ATTN_W, ATTN_W + KV_W, dkp),
                  (ATTN_W + KV_W, ATTN_W + 2 * KV_W, _fold_dup(dv_all)), (ATTN_W + 2 * KV_W, IN_W, du_ref[...]))
        for lo, hi, val in pieces:
            dbin_ref[0:1, lo:hi] += _colsum(val)
            dproj_ref[:, lo:hi] = val.astype(BF16)
        dh = _mm(dproj_ref[...], w_ref[...])
        xf = x_ref[...]
        rstd = _rstd(xf)
        xhat = xf * rstd
        gpre = g_ref[0:1, :]
        scale1 = 1.0 + mod_ref[1:2, :]
        dxn, s_pre = _norm_gain_bwd(dh, xhat, rstd, scale1 * gpre)
        red_ref[0:1, :] += _colsum(dh)
        red_ref[1:2, :] += gpre * s_pre
        red_ref[2:3, :] += scale1 * s_pre
        dx_ref[...] = dx1_ref[...] + dxn

    return pl.pallas_call(
        body, name="in_bwd", grid=(T // tm,),
        in_specs=[_rows(tm, ATTN_W), *[_rows(tm // 2, KVD_W)] * 4, _rows(tm, POOL_W),
                  _rows(tm, 128), _rows(tm, 128), _rows(tm, 128), _rows(tm, D_MODEL), _rows(tm, D_MODEL),
                  _full((8, D_MODEL)), _full((8, D_MODEL)), _resident((IN_W, D_MODEL))],
        out_specs=[_rows(tm, D_MODEL), _rows(tm, IN_W), _full((8, D_MODEL)), _full((8, IN_W))],
        out_shape=[_sds((T, D_MODEL), F32), _sds((T, IN_W), BF16), _sds((8, D_MODEL), F32), _sds((8, IN_W), F32)],
        compiler_params=_cp("arbitrary"),
    )(dq, *dk_eo, *dv_eo, du, rc, rs1, rs2, x, dx1, mod8, g8, w_in)


def _mod_fwd(c_all, ada_w, ada_b_sh):
    tn = 512

    def body(c_ref, w_ref, b_ref, o_ref):
        cv = c_ref[...]
        ca = (cv * jax.nn.sigmoid(cv)).astype(BF16)
        o_ref[...] = _mm(ca, w_ref[...].astype(BF16)) + b_ref[...]

    return pl.pallas_call(
        body, name="mod_fwd", grid=(2, ADA_SH // tn),
        in_specs=[_full((8, D_MODEL)), pl.BlockSpec((None, D_MODEL, tn), lambda l, j: (l, 0, j)),
                  pl.BlockSpec((None, 1, tn), lambda l, j: (l, 0, j))],
        out_specs=pl.BlockSpec((None, 8, tn), lambda l, j: (l, 0, j)),
        out_shape=_sds((2, 8, ADA_SH), F32),
        compiler_params=_cp("parallel", "parallel"),
    )(c_all, ada_w, ada_b_sh)


def _ada_wgrad(c_all_t, dmod_sh):
    tn = 512

    def body(c_ref, d_ref, o_ref):
        cv = c_ref[...]
        ca = cv * jax.nn.sigmoid(cv)
        o_ref[...] = jnp.dot(ca, d_ref[...], preferred_element_type=F32, precision=lax.Precision.HIGHEST)

    return pl.pallas_call(
        body, name="ada_wgrad", grid=(2, ADA_SH // tn),
        in_specs=[_full((D_MODEL, 8)), pl.BlockSpec((None, 8, tn), lambda l, j: (l, 0, j))],
        out_specs=pl.BlockSpec((None, D_MODEL, tn), lambda l, j: (l, 0, j)),
        out_shape=_sds((2, D_MODEL, ADA_SH), F32),
        compiler_params=_cp("parallel", "parallel"),
    )(c_all_t, dmod_sh)


def _sum_devices(g):
    R = g.shape[1]

    def body(g_ref, o_ref):
        acc = g_ref[0]
        for d in range(1, N_DEV):
            acc = acc + g_ref[d]
        o_ref[...] = acc

    return pl.pallas_call(
        body, name="sum_devices", grid=(1,),
        in_specs=[_full((N_DEV, R, 128))], out_specs=_full((R, 128)), out_shape=_sds((R, 128), F32),
        compiler_params=_cp("arbitrary"),
    )(g)


def _adamw(w, g, m, v, name):
    R, C = w.shape
    tr = R
    for cand in (256, 128, 64, 32, 16, 8):
        if R % cand == 0 and cand * C * 4 <= 2 * 1024 * 1024:
            tr = cand
            break

    def body(w_ref, g_ref, m_ref, v_ref, d_ref, nm_ref, nv_ref):
        gv = g_ref[...]
        mn = ADAM_B1 * m_ref[...] + (1.0 - ADAM_B1) * gv
        vn = ADAM_B2 * v_ref[...] + (1.0 - ADAM_B2) * (gv * gv)
        m_hat = mn / (1.0 - ADAM_B1 ** ADAM_STEP)
        v_hat = vn / (1.0 - ADAM_B2 ** ADAM_STEP)
        d_ref[...] = -ADAM_LR * (m_hat / (jnp.sqrt(v_hat) + ADAM_EPS) + ADAM_WD * w_ref[...])
        nm_ref[...] = mn
        nv_ref[...] = vn

    spec = pl.BlockSpec((tr, C), lambda i: (i, 0))
    out = _sds((R, C), F32)
    return pl.pallas_call(
        body, name=name, grid=(R // tr,),
        in_specs=[spec] * 4, out_specs=[spec] * 3, out_shape=[out] * 3,
        compiler_params=_cp("parallel"),
    )(w, g, m, v)


def _adamw_nd(w, g, m, v, name):
    shape = w.shape
    if w.ndim == 2 and shape[1] < 128:
        view = (1, shape[0] * shape[1])
    else:
        view = (-1, shape[-1])
    outs = _adamw(*[t.reshape(view) for t in (w, g, m, v)], name=name)
    return [o.reshape(shape) for o in outs]


def _coords():
    return lax.axis_index("x"), lax.axis_index("y"), lax.axis_index("c")


def _other_chips(x, y):
    return [(1 - x, y), (x, 1 - y), (1 - x, 1 - y)]


def _allgather8(blk, name):
    m_per, n = blk.shape

    def body(x_ref, out_ref, send_sems, recv_sems, local_sem):
        x, y, c = _coords()
        me, sibling = (x, y, c), (x, y, 1 - c)
        chips = _other_chips(x, y)

        def rows(px, py, pc):
            return out_ref.at[pl.ds((4 * px + 2 * py + pc) * m_per, m_per), :]

        def copy(k, block, to, src=None):
            return pltpu.make_async_remote_copy(
                src_ref=rows(*block) if src is None else src, dst_ref=rows(*block),
                send_sem=send_sems.at[k], recv_sem=recv_sems.at[k], device_id=to, device_id_type=MESH)

        mine = pltpu.make_async_copy(x_ref, rows(*me), local_sem)
        mine.start()
        first = [copy(0, me, sibling, src=x_ref)]
        first += [copy(1 + j, me, (*chip, c), src=x_ref) for j, chip in enumerate(chips)]
        for cp in first:
            cp.start()
        passed = [copy(4 + j, (*chip, c), sibling) for j, chip in enumerate(chips)]
        for j, chip in enumerate(chips):
            copy(1 + j, (*chip, c), me).wait_recv()
            passed[j].start()
        copy(0, sibling, me).wait_recv()
        for j, chip in enumerate(chips):
            copy(4 + j, (*chip, 1 - c), me).wait_recv()
        for cp in first + passed:
            cp.wait_send()
        mine.wait()

    return pl.pallas_call(
        body, name=name,
        out_shape=_sds((N_DEV * m_per, n), blk.dtype),
        in_specs=[pl.BlockSpec(memory_space=pltpu.VMEM)],
        out_specs=pl.BlockSpec(memory_space=pltpu.VMEM),
        scratch_shapes=[pltpu.SemaphoreType.DMA((7,)), pltpu.SemaphoreType.DMA((7,)), pltpu.SemaphoreType.DMA],
        compiler_params=pltpu.CompilerParams(vmem_limit_bytes=VMEM_LIMIT),
    )(blk)


def _row_tile(r, n):
    for cand in range(r, 15, -16):
        if r % cand == 0 and cand % 16 == 0 and cand * n * 4 <= 2 * 1024 * 1024:
            return cand
    return r


def _cast_slot(w, chip, name):
    r, n = w.shape
    tr = _row_tile(r, n)

    def body(chip_ref, w_ref, o_ref):
        o_ref[...] = w_ref[...].astype(BF16)

    grid_spec = pltpu.PrefetchScalarGridSpec(
        num_scalar_prefetch=1, grid=(r // tr,),
        in_specs=[pl.BlockSpec((tr, n), lambda i, ch: (i, 0))],
        out_specs=pl.BlockSpec((None, tr, n), lambda i, ch: (ch[0], i, 0)))
    return pl.pallas_call(
        body, name=name, grid_spec=grid_spec, out_shape=_sds((N_SHARD, r, n), BF16),
        compiler_params=_cp("arbitrary"),
    )(chip, w)


def _join_halves(tots, name):
    nt = len(tots)
    hom = [pl.BlockSpec(memory_space=pl.ANY)] * nt

    def body(*refs):
        outs = refs[nt:2 * nt]
        send_sems, recv_sems = refs[2 * nt:]
        x, y, c = _coords()
        sibling = (x, y, 1 - c)
        cps = []
        for t in range(nt):
            cp = pltpu.make_async_remote_copy(
                src_ref=outs[t].at[c], dst_ref=outs[t].at[c],
                send_sem=send_sems.at[t], recv_sem=recv_sems.at[t], device_id=sibling, device_id_type=MESH)
            cp.start()
            cps.append(cp)
        for t in range(nt):
            pltpu.make_async_remote_copy(
                src_ref=outs[t].at[c], dst_ref=outs[t].at[1 - c],
                send_sem=send_sems.at[t], recv_sem=recv_sems.at[t], device_id=sibling, device_id_type=MESH).wait_recv()
        for cp in cps:
            cp.wait_send()

    return pl.pallas_call(
        body, name=name,
        out_shape=[_sds(t.shape, t.dtype) for t in tots],
        in_specs=hom, out_specs=hom,
        input_output_aliases={t: t for t in range(nt)},
        scratch_shapes=[pltpu.SemaphoreType.DMA((nt,)), pltpu.SemaphoreType.DMA((nt,))],
    )(*tots)


def _pair_sum(g, recv, core, chip, name):
    _, _, r, n = g.shape
    tr = _row_tile(r, n)

    def body(core_ref, chip_ref, g_ref, r_ref, sb_ref, own_ref):
        tot = g_ref[...] + r_ref[...]
        sb_ref[...] = tot.astype(BF16)

        @pl.when(pl.program_id(1) == chip_ref[0])
        def _():
            own_ref[...] = tot

    grid_spec = pltpu.PrefetchScalarGridSpec(
        num_scalar_prefetch=2, grid=(r // tr, N_SHARD),
        in_specs=[pl.BlockSpec((None, None, tr, n), lambda i, s, co, ch: (s, co[0], i, 0)),
                  pl.BlockSpec((None, tr, n), lambda i, s, co, ch: (s, i, 0))],
        out_specs=[pl.BlockSpec((None, tr, n), lambda i, s, co, ch: (s, i, 0)),
                   pl.BlockSpec((tr, n), lambda i, s, co, ch: (i, 0))])
    return pl.pallas_call(
        body, name=name, grid_spec=grid_spec,
        out_shape=[_sds((N_SHARD, r, n), BF16), _sds((r, n), F32)],
        compiler_params=_cp("arbitrary", "arbitrary"),
    )(core, chip, g, recv)


def _chip_sum(own, recv, core, name):
    r, n = own.shape
    tr = _row_tile(r, n)

    def body(core_ref, o_ref, r_ref, t_ref):
        acc = o_ref[...]
        for j in range(3):
            acc = acc + r_ref[j].astype(F32)
        t_ref[...] = acc

    grid_spec = pltpu.PrefetchScalarGridSpec(
        num_scalar_prefetch=1, grid=(r // tr,),
        in_specs=[pl.BlockSpec((tr, n), lambda i, co: (i, 0)), pl.BlockSpec((3, tr, n), lambda i, co: (0, i, 0))],
        out_specs=pl.BlockSpec((None, tr, n), lambda i, co: (co[0], i, 0)))
    return pl.pallas_call(
        body, name=name, grid_spec=grid_spec, out_shape=_sds((2, r, n), F32),
        compiler_params=_cp("arbitrary"),
    )(core, own, recv)


_HBM = pl.BlockSpec(memory_space=pltpu.HBM)
_SEM = pl.BlockSpec(memory_space=pltpu.SEMAPHORE)
_EFFECT = pltpu.SideEffectType.DATAFLOW_SIDE_EFFECTING


def _ici_copies(srcs, dsts, send_sems, recv_sems, send_view, recv_view):
    x, y, c = _coords()
    out = []
    if send_view is None:
        for t in range(len(srcs)):
            r = srcs[t].shape[1] // 2
            out.append(pltpu.make_async_remote_copy(
                src_ref=srcs[t].at[:, pl.ds((1 - c) * r, r)], dst_ref=dsts[t],
                send_sem=send_sems.at[3 * t], recv_sem=recv_sems.at[3 * t],
                device_id=(x, y, 1 - c), device_id_type=MESH))
        return out
    for t in range(len(srcs)):
        for j, chip in enumerate(_other_chips(x, y)):
            out.append(pltpu.make_async_remote_copy(
                src_ref=send_view(srcs[t], chip, j, (x, y), c), dst_ref=recv_view(dsts[t], chip, j, (x, y), c),
                send_sem=send_sems.at[3 * t + j], recv_sem=recv_sems.at[3 * t + j],
                device_id=(*chip, c), device_id_type=MESH))
    return out


def _ici_start(srcs, dsts, after, send_view, recv_view, name):
    nt = len(srcs)
    inplace = dsts is None
    nbuf = nt if inplace else 2 * nt

    def body(*refs):
        send_sems, recv_sems = refs[nbuf + 1], refs[nbuf + 2]
        s_out = refs[nbuf + 3:nbuf + 3 + nt]
        d_out = s_out if inplace else refs[nbuf + 3 + nt:nbuf + 3 + 2 * nt]
        token = refs[-1]
        for cp in _ici_copies(s_out, d_out, send_sems, recv_sems, send_view, recv_view):
            cp.start()
        token[...] = jnp.zeros_like(token)

    bufs = list(srcs) + ([] if inplace else list(dsts))
    res = pl.pallas_call(
        body, name=name,
        out_shape=(pltpu.SemaphoreType.DMA((3 * nt,)), pltpu.SemaphoreType.DMA((3 * nt,)),
                   *[pltpu.HBM(b.shape, b.dtype) for b in bufs], _sds((8, 128), F32)),
        in_specs=[_HBM] * nbuf + [pl.BlockSpec(memory_space=pl.ANY)],
        out_specs=(_SEM, _SEM, *[_HBM] * nbuf, pl.BlockSpec(memory_space=pltpu.VMEM)),
        input_output_aliases={i: 2 + i for i in range(nbuf)},
        compiler_params=pltpu.CompilerParams(has_side_effects=_EFFECT),
    )(*[pltpu.with_memory_space_constraint(b, pltpu.HBM) for b in bufs], after)
    send_sems, recv_sems = res[0], res[1]
    s_thru = list(res[2:2 + nt])
    d_thru = s_thru if inplace else list(res[2 + nt:2 + 2 * nt])
    return send_sems, recv_sems, s_thru, d_thru, res[-1]


def _ici_wait(send_sems, recv_sems, srcs, dsts, after, send_view, recv_view, name):
    nt = len(srcs)
    inplace = dsts is None
    nbuf = nt if inplace else 2 * nt

    def body(*refs):
        send_ref, recv_ref = refs[nbuf], refs[nbuf + 1]
        s_out = refs[nbuf + 3:nbuf + 3 + nt]
        d_out = s_out if inplace else refs[nbuf + 3 + nt:nbuf + 3 + 2 * nt]
        for cp in _ici_copies(s_out, d_out, send_ref, recv_ref, send_view, recv_view):
            cp.wait_send()
            cp.wait_recv()

    bufs = list(srcs) + ([] if inplace else list(dsts))
    res = pl.pallas_call(
        body, name=name,
        out_shape=tuple(pltpu.HBM(b.shape, b.dtype) for b in bufs),
        in_specs=[_HBM] * nbuf + [_SEM, _SEM, pl.BlockSpec(memory_space=pl.ANY)],
        out_specs=tuple([_HBM] * nbuf),
        input_output_aliases={i: i for i in range(nbuf)},
        compiler_params=pltpu.CompilerParams(has_side_effects=_EFFECT),
    )(*bufs, send_sems, recv_sems, after)
    return list(res[:nt]) if inplace else (list(res[:nt]), list(res[nt:]))


def _w_half(buf, chip, c):
    r = buf.shape[1] // 2
    return buf.at[2 * chip[0] + chip[1], pl.ds(c * r, r)]


def _ag_send_view(buf, chip, j, me, c):
    return _w_half(buf, me, c)


def _ag_recv_view(buf, chip, j, me, c):
    return _w_half(buf, me, c)


def _rs_send_view(buf, chip, j, me, c):
    return buf.at[2 * chip[0] + chip[1]]


def _rs_recv_view(buf, chip, j, me, c):
    return buf.at[j]


def _ag_forward(bufs, name):
    nt = len(bufs)
    hom = [pl.BlockSpec(memory_space=pl.ANY)] * nt

    def body(*refs):
        outs = refs[nt:2 * nt]
        send_sems, recv_sems = refs[2 * nt:]
        x, y, c = _coords()
        sibling = (x, y, 1 - c)
        chips = _other_chips(x, y)

        def copy(t, j, hc):
            blk = _w_half(outs[t], chips[j], hc)
            return pltpu.make_async_remote_copy(
                src_ref=blk, dst_ref=blk, send_sem=send_sems.at[t, j], recv_sem=recv_sems.at[t, j],
                device_id=sibling, device_id_type=MESH)

        started = [copy(t, j, c) for t in range(nt) for j in range(3)]
        for cp in started:
            cp.start()
        for t in range(nt):
            for j in range(3):
                copy(t, j, 1 - c).wait_recv()
        for cp in started:
            cp.wait_send()

    return pl.pallas_call(
        body, name=name,
        out_shape=[_sds(b.shape, b.dtype) for b in bufs],
        in_specs=hom, out_specs=hom,
        input_output_aliases={t: t for t in range(nt)},
        scratch_shapes=[pltpu.SemaphoreType.DMA((nt, 3)), pltpu.SemaphoreType.DMA((nt, 3))],
    )(*bufs)


def _rs_swap_begin(grads, after, tag):
    land = [lax.empty((N_SHARD, g.shape[1] // 2, g.shape[2]), g.dtype) for g in grads]
    send_sems, recv_sems, s_thru, d_thru, token = _ici_start(grads, land, after, None, None, name="rs_swapgo_" + tag)
    return dict(sems=(send_sems, recv_sems), grads=s_thru, land=d_thru, tag=tag), token


def _rs_scatter_begin(swap, after):
    tag = swap["tag"]
    x, y, c = _coords()
    core = jnp.reshape(c, (1,)).astype(jnp.int32)
    chip = jnp.reshape(2 * x + y, (1,)).astype(jnp.int32)
    grads, recv = _ici_wait(*swap["sems"], swap["grads"], swap["land"], after, None, None, name="rs_swapend_" + tag)
    sums, owns = [], []
    for t, (g, rv) in enumerate(zip(grads, recv)):
        r = g.shape[1] // 2
        sb, own = _pair_sum(g.reshape(N_SHARD, 2, r, g.shape[2]), rv, core, chip, name=f"rs_pair_{tag}_{t}")
        sums.append(sb)
        owns.append(own)
    land = [lax.empty((3,) + s.shape[1:], s.dtype) for s in sums]
    send_sems, recv_sems, s_thru, d_thru, token = _ici_start(
        sums, land, after, _rs_send_view, _rs_recv_view, name="rs_start_" + tag)
    return dict(sems=(send_sems, recv_sems), sums=s_thru, land=d_thru, owns=owns, core=core, tag=tag), token


def _rs_end(state, after):
    tag = state["tag"]
    _, got = _ici_wait(*state["sems"], state["sums"], state["land"], after, _rs_send_view, _rs_recv_view,
                       name="rs_wait_" + tag)
    tots = [_chip_sum(o, gt, state["core"], name=f"rs_chip_{tag}_{t}")
            for t, (o, gt) in enumerate(zip(state["owns"], got))]
    full = _join_halves(tots, name="rs_join_" + tag)
    return [f.reshape(2 * f.shape[1], f.shape[2]) for f in full]


def _rope_lane_table():
    d = jnp.arange(128) % HEAD
    inv_freq = ROPE_THETA ** (-jnp.arange(0, ROT, 2, dtype=F32) / ROT)
    rot = d < ROT
    rows = [jnp.where(rot, inv_freq[d % (ROT // 2)], 0.0), rot.astype(F32),
            (d < ROT // 2).astype(F32), jnp.logical_and(d >= ROT // 2, rot).astype(F32)]
    return jnp.concatenate([jnp.stack(rows), jnp.zeros((4, 128), F32)], axis=0)


def _pad8(rows):
    return jnp.concatenate([rows, jnp.zeros((8 - rows.shape[0], rows.shape[1]), F32)], axis=0)


def kernel(x, c, positions, ada_w, ada_b, w_in, b_in, sinks, pool_w, pool_scale, w_out, w_gate, w_up, w_down, g_pre_mix, g_post_mix, g_pre_ffn, g_post_ffn, loss_target, m_ada_w, m_ada_b, m_w_in, m_b_in, m_sinks, m_pool_w, m_pool_scale, m_w_out, m_w_gate, m_w_up, m_w_down, m_g_pre_mix, m_g_post_mix, m_g_pre_ffn, m_g_post_ffn, v_ada_w, v_ada_b, v_w_in, v_b_in, v_sinks, v_pool_w, v_pool_scale, v_w_out, v_w_gate, v_w_up, v_w_down, v_g_pre_mix, v_g_post_mix, v_g_pre_ffn, v_g_post_ffn):
    T = x.shape[1]
    n_layers = ada_w.shape[0]
    ax, ay, ac = _coords()
    my_dev = 4 * ax + 2 * ay + ac
    my_chip = 2 * ax + ay
    x0 = x.reshape(T, D_MODEL)
    target = loss_target.reshape(T, D_MODEL)

    c_all = _allgather8(c.reshape(8, 128), name="ag_c").reshape(N_DEV, D_MODEL)
    ada_b_sh = lax.dynamic_slice_in_dim(ada_b, my_chip * ADA_SH, ADA_SH, axis=1).reshape(n_layers, 1, ADA_SH)
    mod_part = _mod_fwd(c_all, ada_w, ada_b_sh)
    mod_all = _allgather8(mod_part.reshape(n_layers * 8, ADA_SH), name="ag_mod")
    mod_all = mod_all.reshape(N_DEV, n_layers, 8, ADA_SH)[0::2]
    mod_mine = lax.dynamic_index_in_dim(mod_all, my_dev, axis=2, keepdims=False)
    mod = jnp.transpose(mod_mine, (1, 0, 2)).reshape(n_layers, 6, D_MODEL)

    chip1 = jnp.reshape(my_chip, (1,)).astype(jnp.int32)

    def tr(t):
        return jnp.transpose(t, (0, 2, 1))

    w_in_t, w_gate_t, w_up_t = tr(w_in), tr(w_gate), tr(w_up)

    def cast_layer(l):
        return [_cast_slot(w[l], chip1, name=f"cast_{nm}{l}")
                for nm, w in (("w_in", w_in_t), ("w_out", w_out), ("w_gate", w_gate_t), ("w_up", w_up_t),
                              ("w_down", w_down))]

    def as_operands(bufs):
        gin, gout, gg, gu, gd = bufs
        return (gin.reshape(IN_W, D_MODEL), gout.reshape(D_MODEL, D_MODEL), gg.reshape(D_FF, D_MODEL),
                gu.reshape(D_FF, D_MODEL), gd.reshape(D_FF, D_MODEL))

    bufs0 = cast_layer(0)
    in_send, in_recv, in_bufs, _, in_token = _ici_start(
        bufs0[:1], None, mod, _ag_send_view, _ag_recv_view, name="ag_start_0_in")
    pos_b = jnp.broadcast_to(positions.reshape(T, 1), (T, 128))
    rc, rs1, rs2 = _rope_tables(pos_b, _rope_lane_table() + in_token[0, 0])
    arrived = _ici_wait(in_send, in_recv, in_bufs, None, rc, _ag_send_view, _ag_recv_view, name="ag_wait_0_in")
    win0 = _ag_forward(arrived, name="ag_fwd_0_in")
    rest_send, rest_recv, rest_bufs, _, ag_token = _ici_start(
        bufs0[1:], None, win0[0], _ag_send_view, _ag_recv_view, name="ag_start_0")
    weights = [None] * n_layers

    saved = []
    xl = x0
    for l in range(n_layers):
        mod8 = _pad8(mod[l])
        if l + 1 < n_layers:
            ag_send, ag_recv, ag_bufs, _, ag_token = _ici_start(
                cast_layer(l + 1), None, ag_token, _ag_send_view, _ag_recv_view, name=f"ag_start_{l + 1}")
        if l == 0 or l + 1 < n_layers:
            mod8 = mod8 + ag_token[0, 0]
        g8 = _pad8(jnp.stack([g_pre_mix[l], g_post_mix[l], g_pre_ffn[l], g_post_ffn[l]]))
        sink_b = jnp.broadcast_to(sinks[l][:, None], (N_HEADS, 128))
        psc = pool_scale[l].reshape(1, POOL_W)
        win = win0[0].reshape(IN_W, D_MODEL) if l == 0 else weights[l][0]
        h, q, k, v, u = _fwd_in(xl, mod8, g8, win, b_in[l].reshape(1, IN_W), rc, rs1, rs2)
        attn, lse = _attn_fwd(q, k, v, sink_b)
        pool, pooled = _pool_fwd(u, pool_w[l], psc)
        if l == 0:
            arrived = _ici_wait(rest_send, rest_recv, rest_bufs, None, pool, _ag_send_view, _ag_recv_view,
                                name="ag_wait_0")
            weights[0] = as_operands(win0 + _ag_forward(arrived, name="ag_fwd_0"))
        win, wout, wg, wu, wd = weights[l]
        if l + 1 < n_layers:
            mix, x1, h2, act, ga, gb, f, x2 = _out_ffn_fwd(attn, pool, xl, wout, mod8, g8, wg, wu, wd)
        else:
            mix, x1, h2, act, ga, gb, f, x2, loss_tile = _out_ffn_fwd(attn, pool, xl, wout, mod8, g8, wg, wu, wd,
                                                                      target=target)
        saved.append(dict(x=xl, h=h, q=q, k=k, v=v, lse=lse, attn=attn, pool=pool, pooled=pooled, mix=mix,
                          x1=x1, h2=h2, act=act, ga=ga, gb=gb, f=f, mod8=mod8, g8=g8, sink_b=sink_b, psc=psc))
        xl = x2
        if l + 1 < n_layers:
            arrived = _ici_wait(ag_send, ag_recv, ag_bufs, None, x2, _ag_send_view, _ag_recv_view,
                                name=f"ag_wait_{l + 1}")
            weights[l + 1] = as_operands(_ag_forward(arrived, name=f"ag_fwd_{l + 1}"))

    dy = xl
    loss = lax.psum(loss_tile[0, 0], ("x", "y", "c"))

    small = [None] * n_layers
    dmod_rows = [None] * n_layers
    reduced = [dict() for _ in range(n_layers)]
    att_swap = None
    dx = dy
    for l in reversed(range(n_layers)):
        s = saved[l]
        win, wout, wg, wu, wd = weights[l]
        if att_swap is not None:
            s = dict(s, mod8=s["mod8"] + att_swap[1][0, 0])
        dx1, df, da, db, red_f = _ffn_bwd(dx, s["f"], s["ga"], s["gb"], s["x1"], s["mod8"], s["g8"], wg, wu, wd)
        token = None
        if att_swap is not None:
            att_scatter = _rs_scatter_begin(att_swap[0], dx1)
            token = att_scatter[1]
        ffn_shards = (N_SHARD, FF_SH, D_MODEL)
        g_wd = _wgrad(s["act"], df, name="wgrad_down", after=token).reshape(ffn_shards)
        g_wg = _wgrad(da, s["h2"], name="wgrad_gate").reshape(ffn_shards)
        g_wu = _wgrad(db, s["h2"], name="wgrad_up").reshape(ffn_shards)
        ffn_swap = _rs_swap_begin([g_wg, g_wu, g_wd], dx1, tag=f"{l}f")
        if att_swap is not None:
            got = _rs_end(att_scatter[0], ffn_swap[1])
            reduced[l + 1].update(w_in=got[0], w_out=got[1])
        s = dict(s, mod8=s["mod8"] + ffn_swap[1][0, 0])
        dmix, dattn, dpool, red_c = _mix_bwd(dx1, s["mix"], s["mod8"], s["g8"], wout)
        g_wout = jnp.concatenate([_wgrad(s["attn"], dmix, name="wgrad_out_a"),
                                  _wgrad(s["pool"], dmix, name="wgrad_out_p")], axis=0)
        ffn_scatter = _rs_scatter_begin(ffn_swap[0], dattn)
        dq, dk_e, dk_o, dv_e, dv_o, dsink = _attn_bwd(s["q"], s["k"], s["v"], s["lse"], dattn,
                                                      s["sink_b"] + ffn_scatter[1][0:1, :])
        du, g_poolw, dpsc = _pool_bwd(dpool, s["pooled"], pool_w[l], s["psc"])
        dx, dproj, red_d, dbin = _in_bwd(dq, (dk_e, dk_o), (dv_e, dv_o), du, rc, rs1, rs2, s["x"], dx1, s["mod8"],
                                         s["g8"], win)
        g_win = _wgrad(dproj, s["h"], name="wgrad_in")
        g_win_sh = g_win.reshape(N_SHARD, IN_SH, D_MODEL)
        got = _rs_end(ffn_scatter[0], dproj)
        reduced[l].update(w_gate=got[0], w_up=got[1], w_down=got[2])
        att_swap = _rs_swap_begin([g_win_sh, g_wout.reshape(N_SHARD, OUT_SH, D_MODEL)], dx, tag=f"{l}a")
        dmod_rows[l] = jnp.concatenate([red_d[0], red_d[1], red_c[0], red_f[2], red_f[3], red_f[0]])
        small[l] = jnp.concatenate([red_d[2], red_c[1], red_f[4], red_f[1], dbin[0], dpsc[0], dsink[:, 0],
                                    jnp.zeros((120,), F32), g_poolw.reshape(-1)])
    grad_x = dx.reshape(1, T, D_MODEL)

    per_layer = small[0].shape[0]
    rows_small = n_layers * per_layer // 128
    rows_mod = n_layers * 6 * D_MODEL // 128
    rows_pad = -(rows_small + rows_mod) % 8
    pack = jnp.concatenate(small + dmod_rows + [jnp.zeros((rows_pad * 128,), F32)]).reshape(-1, 128)
    pack = pack + att_swap[1][0, 0]
    gathered = _allgather8(pack, name="ag_small").reshape(N_DEV, pack.shape[0], 128)
    summed = _sum_devices(gathered)
    att_scatter = _rs_scatter_begin(att_swap[0], summed)
    small_sum = summed[:rows_small].reshape(n_layers, per_layer)
    o = 0
    small_g = {}
    for nm, width in (("g_pre_mix", D_MODEL), ("g_post_mix", D_MODEL), ("g_pre_ffn", D_MODEL),
                      ("g_post_ffn", D_MODEL), ("b_in", IN_W), ("pool_scale", POOL_W), ("sinks", 128),
                      ("pool_w", 4 * 128 * 128)):
        small_g[nm] = small_sum[:, o:o + width]
        o += width
    small_g["sinks"] = small_g["sinks"][:, :N_HEADS]
    small_g["pool_w"] = small_g["pool_w"].reshape(n_layers, 4, 128, 128)
    small_g["ada_b"] = summed[rows_small:rows_small + rows_mod].reshape(n_layers, 6 * D_MODEL)
    dmod_all = gathered[:, rows_small:rows_small + rows_mod].reshape(N_DEV, n_layers, N_SHARD, ADA_SH)
    dmod_sh = lax.dynamic_index_in_dim(dmod_all, my_chip, axis=2, keepdims=False)
    g_ada_w = _ada_wgrad(jnp.transpose(c_all), jnp.transpose(dmod_sh, (1, 0, 2)))

    grads = dict(ada_w=g_ada_w, ada_b=small_g["ada_b"], b_in=small_g["b_in"], sinks=small_g["sinks"],
                 pool_w=small_g["pool_w"], pool_scale=small_g["pool_scale"], g_pre_mix=small_g["g_pre_mix"],
                 g_post_mix=small_g["g_post_mix"], g_pre_ffn=small_g["g_pre_ffn"], g_post_ffn=small_g["g_post_ffn"])
    params = dict(ada_w=(ada_w, m_ada_w, v_ada_w), ada_b=(ada_b, m_ada_b, v_ada_b), w_in=(w_in, m_w_in, v_w_in),
                  b_in=(b_in, m_b_in, v_b_in), sinks=(sinks, m_sinks, v_sinks), pool_w=(pool_w, m_pool_w, v_pool_w),
                  pool_scale=(pool_scale, m_pool_scale, v_pool_scale), w_out=(w_out, m_w_out, v_w_out),
                  w_gate=(w_gate, m_w_gate, v_w_gate), w_up=(w_up, m_w_up, v_w_up),
                  w_down=(w_down, m_w_down, v_w_down), g_pre_mix=(g_pre_mix, m_g_pre_mix, v_g_pre_mix),
                  g_post_mix=(g_post_mix, m_g_post_mix, v_g_post_mix), g_pre_ffn=(g_pre_ffn, m_g_pre_ffn, v_g_pre_ffn),
                  g_post_ffn=(g_post_ffn, m_g_post_ffn, v_g_post_ffn))
    names = list(params)
    updates = {nm: _adamw_nd(*params[nm][:1], grads[nm], *params[nm][1:], name="adamw_" + nm) for nm in grads}

    got = _rs_end(att_scatter[0], updates["ada_w"][0])
    reduced[0].update(w_in=got[0], w_out=got[1])
    for nm in ("w_in", "w_out", "w_gate", "w_up", "w_down"):
        g = jnp.stack([reduced[l][nm] for l in range(n_layers)])
        if nm in ("w_in", "w_gate", "w_up"):
            upd = _adamw_nd(tr(params[nm][0]), g, tr(params[nm][1]), tr(params[nm][2]), name="adamw_" + nm)
            grads[nm], updates[nm] = tr(g), [tr(u) for u in upd]
        else:
            grads[nm], updates[nm] = g, _adamw_nd(params[nm][0], g, *params[nm][1:], name="adamw_" + nm)
    return (loss, grad_x, *[grads[nm] for nm in names], *[updates[nm][0] for nm in names],
            *[updates[nm][1] for nm in names], *[updates[nm][2] for nm in names])
```

```python
import functools

import jax
import jax.numpy as jnp
from jax import lax
from jax.experimental import pallas as pl
from jax.experimental.pallas import tpu as pltpu

F32 = jnp.float32
BF16 = jnp.bfloat16
MESH = pl.DeviceIdType.MESH

D_MODEL = 1024
ATTN_W = 512
KV_W = 128
KVD_W = 256
POOL_W = 512
IN_W = 1280
D_FF = 2816
N_SHARD = 4
FF_SH = D_FF // N_SHARD
IN_SH = IN_W // N_SHARD
OUT_SH = D_MODEL // N_SHARD
ADA_SH = 6 * D_MODEL // N_SHARD
HEAD = 64
N_HEADS = 8
GROUP = 4
BLK = 128
POOL_WINDOWS = (2, 4, 8, 16)
HALO = 16
ROT = 16
ROPE_THETA = 500000.0
EPS = 1e-6
NEG_INF = -1e30
N_DEV = 8

ADAM_LR = 0.001
ADAM_B1 = 0.9
ADAM_B2 = 0.999
ADAM_EPS = 1e-08
ADAM_WD = 0.01
ADAM_STEP = 10

VMEM_LIMIT = 48 * 1024 * 1024
FFN_VMEM_LIMIT = 60 * 1024 * 1024
ATTN_FWD_BLOCKS = 4
ATTN_BWD_BLOCKS = 4
WGRAD_TOKENS = 2048


def _cp(*sem, vmem=VMEM_LIMIT):
    return pltpu.CompilerParams(dimension_semantics=sem, vmem_limit_bytes=vmem)


def _full(shape):
    nd = len(shape)
    return pl.BlockSpec(shape, lambda *_: (0,) * nd)


def _resident(shape):
    nd = len(shape)
    return pl.BlockSpec(shape, lambda *_: (0,) * nd, pipeline_mode=pl.Buffered(1))


def _rows(tm, ncol):
    return pl.BlockSpec((tm, ncol), lambda i: (i, 0))


def _sds(shape, dtype):
    return jax.ShapeDtypeStruct(shape, dtype)


def _nt(a, b):
    return lax.dot_general(a, b, (((1,), (1,)), ((), ())), preferred_element_type=F32)


def _tn(a, b):
    return lax.dot_general(a, b, (((0,), (0,)), ((), ())), preferred_element_type=F32)


def _mm(a, b):
    return jnp.dot(a, b, preferred_element_type=F32)


def _rstd(x):
    return lax.rsqrt(jnp.mean(x * x, axis=-1, keepdims=True) + EPS)


def _colsum(x):
    return jnp.sum(x, axis=0, keepdims=True)


def _norm_gain_bwd(dy, xhat, rstd, gain):
    p = dy * xhat
    dx = rstd * (dy * gain - xhat * jnp.mean(p * gain, axis=-1, keepdims=True))
    return dx, _colsum(p)


def _rope_tables(pos_b, lane_tab):
    T = pos_b.shape[0]
    tm = min(T, 1024)

    def body(pos_ref, tab_ref, c_ref, s1_ref, s2_ref):
        ang = pos_ref[...].astype(F32) * tab_ref[0:1, :]
        cs = jnp.cos(ang)
        sn = jnp.sin(ang)
        m_rot = tab_ref[1:2, :]
        c_ref[...] = cs * m_rot + (1.0 - m_rot)
        s1_ref[...] = -sn * tab_ref[2:3, :]
        s2_ref[...] = sn * tab_ref[3:4, :]

    out = _sds((T, 128), F32)
    return pl.pallas_call(
        body, name="rope_tables", grid=(T // tm,),
        in_specs=[_rows(tm, 128), _full((8, 128))],
        out_specs=[_rows(tm, 128)] * 3, out_shape=[out] * 3,
        compiler_params=_cp("parallel"),
    )(pos_b, lane_tab)


def _rot_fwd(t, c, s1, s2):
    w = t.shape[-1]
    return t * c + pltpu.roll(t, w - 8, 1) * s1 + pltpu.roll(t, 8, 1) * s2


def _rot_bwd(d, c, s1, s2):
    w = d.shape[-1]
    return d * c + pltpu.roll(d * s1, 8, 1) + pltpu.roll(d * s2, w - 8, 1)


def _store_dup(ref, t):
    low = lax.broadcasted_iota(jnp.int32, t.shape, 1) < HEAD
    sw = pltpu.roll(t, HEAD, 1)
    ref[:, 0:128] = jnp.where(low, t, sw).astype(BF16)
    ref[:, 128:256] = jnp.where(low, sw, t).astype(BF16)


def _fold_dup(d):
    low = lax.broadcasted_iota(jnp.int32, (d.shape[0], 128), 1) < HEAD
    d0 = d[:, 0:128]
    d1 = d[:, 128:256]
    return jnp.where(low, d0 + pltpu.roll(d0, HEAD, 1), d1 + pltpu.roll(d1, HEAD, 1))


def _fwd_in(x, mod8, g8, w_in, b_in, rc, rs1, rs2):
    T = x.shape[0]
    tm = min(T, 1024)

    def body(x_ref, mod_ref, g_ref, w_ref, b_ref, c_ref, s1_ref, s2_ref,
             h_ref, q_ref, k_ref, v_ref, u_ref):
        xf = x_ref[...]
        h = (xf * _rstd(xf) * g_ref[0:1, :]) * (1.0 + mod_ref[1:2, :]) + mod_ref[0:1, :]
        hb = h.astype(BF16)
        h_ref[...] = hb
        c = c_ref[...]
        s1 = s1_ref[...]
        s2 = s2_ref[...]
        proj = _nt(hb, w_ref[...]) + b_ref[...]
        q = _rot_fwd(proj[:, 0:ATTN_W], jnp.tile(c, (1, 4)), jnp.tile(s1, (1, 4)), jnp.tile(s2, (1, 4)))
        q_ref[...] = (q * (HEAD ** -0.5)).astype(BF16)
        _store_dup(k_ref, _rot_fwd(proj[:, ATTN_W:ATTN_W + KV_W], c, s1, s2))
        _store_dup(v_ref, proj[:, ATTN_W + KV_W:ATTN_W + 2 * KV_W])
        u_ref[...] = proj[:, ATTN_W + 2 * KV_W:IN_W]

    return pl.pallas_call(
        body, name="fwd_in", grid=(T // tm,),
        in_specs=[_rows(tm, D_MODEL), _full((8, D_MODEL)), _full((8, D_MODEL)),
                  _resident((IN_W, D_MODEL)), _full((1, IN_W)),
                  _rows(tm, 128), _rows(tm, 128), _rows(tm, 128)],
        out_specs=[_rows(tm, D_MODEL), _rows(tm, ATTN_W), _rows(tm, KVD_W), _rows(tm, KVD_W), _rows(tm, POOL_W)],
        out_shape=[_sds((T, D_MODEL), BF16), _sds((T, ATTN_W), BF16), _sds((T, KVD_W), BF16),
                   _sds((T, KVD_W), BF16), _sds((T, POOL_W), F32)],
        compiler_params=_cp("parallel"),
    )(x, mod8, g8, w_in, b_in, rc, rs1, rs2)


def _band_mask(n):
    kk = lax.broadcasted_iota(jnp.int32, (2 * BLK, BLK), 0)
    qi = lax.broadcasted_iota(jnp.int32, (2 * BLK, BLK), 1)
    first = jnp.where(n > 0, 0, 2 * BLK)
    in_prev = jnp.logical_and(kk < BLK, kk > qi + first)
    in_cur = jnp.logical_and(kk >= BLK, (kk - BLK) <= qi)
    one = jnp.logical_or(in_prev, in_cur)
    return jnp.concatenate([one] * GROUP, axis=1)


def _head_row(ref, j, base=0):
    return jnp.concatenate([ref[base + GROUP * j + r:base + GROUP * j + r + 1, :] for r in range(GROUP)], axis=1)


def _stack_heads(x_ref, j, rows=slice(None)):
    low = lax.broadcasted_iota(jnp.int32, (BLK, 128), 1) < HEAD
    parts = []
    for gp in (2 * j, 2 * j + 1):
        x2 = x_ref[rows, gp * 128:(gp + 1) * 128]
        parts.append(jnp.where(low, x2, jnp.zeros_like(x2)))
        parts.append(jnp.where(low, jnp.zeros_like(x2), x2))
    return jnp.concatenate(parts, axis=0)


def _unstack_heads(o):
    low = lax.broadcasted_iota(jnp.int32, (BLK, 128), 1) < HEAD
    return [jnp.where(low, o[0:BLK], o[BLK:2 * BLK]), jnp.where(low, o[2 * BLK:3 * BLK], o[3 * BLK:4 * BLK])]


def _attn_fwd(q, kd, vd, sink_b):
    T = q.shape[0]
    nb = T // BLK
    nq = ATTN_FWD_BLOCKS if nb % ATTN_FWD_BLOCKS == 0 else 2
    assert nb % nq == 0

    def body(q_ref, kp_ref, kc_ref, vp_ref, vc_ref, sk_ref, o_ref, lse_ref):
        for sub in range(nq):
            rows = slice(sub * BLK, (sub + 1) * BLK)
            before = slice((sub - 1) * BLK, sub * BLK)
            valid = _band_mask(nq * pl.program_id(0) + sub)
            for j in range(N_HEADS // GROUP):
                lanes = slice(j * 128, (j + 1) * 128)
                k_prev = kp_ref[:, lanes] if sub == 0 else kc_ref[before, lanes]
                v_prev = vp_ref[:, lanes] if sub == 0 else vc_ref[before, lanes]
                kcat = jnp.concatenate([k_prev, kc_ref[rows, lanes]], axis=0)
                vcat = jnp.concatenate([v_prev, vc_ref[rows, lanes]], axis=0)
                s = jnp.where(valid, _nt(kcat, _stack_heads(q_ref, j, rows)), NEG_INF)
                sk = _head_row(sk_ref, j)
                m = jnp.maximum(jnp.max(s, axis=0, keepdims=True), sk)
                p = jnp.exp(s - m)
                den = jnp.sum(p, axis=0, keepdims=True) + jnp.exp(sk - m)
                p = p * (1.0 / den)
                o = _tn(p.astype(BF16), vcat)
                o_ref[rows, 2 * j * 128:(2 * j + 2) * 128] = jnp.concatenate(_unstack_heads(o), axis=1).astype(BF16)
                lse = m + jnp.log(den)
                for r in range(GROUP):
                    h = sub * N_HEADS + GROUP * j + r
                    lse_ref[h:h + 1, :] = lse[:, r * 128:(r + 1) * 128]

    prev = lambda i: (jnp.maximum(nq * i - 1, 0), 0)
    cur = lambda i: (i, 0)
    return pl.pallas_call(
        body, name="attn_fwd", grid=(nb // nq,),
        in_specs=[pl.BlockSpec((nq * BLK, ATTN_W), cur),
                  pl.BlockSpec((BLK, KVD_W), prev), pl.BlockSpec((nq * BLK, KVD_W), cur),
                  pl.BlockSpec((BLK, KVD_W), prev), pl.BlockSpec((nq * BLK, KVD_W), cur),
                  _full((8, 128))],
        out_specs=[pl.BlockSpec((nq * BLK, ATTN_W), cur), pl.BlockSpec((nq * N_HEADS, 128), cur)],
        out_shape=[_sds((T, ATTN_W), BF16), _sds((nb * N_HEADS, 128), F32)],
        compiler_params=_cp("parallel"),
    )(q, kd, kd, vd, vd, sink_b)


def _pool_fwd(u, pool_w, pool_scale):
    T = u.shape[0]
    tm = min(T, 1024)

    def body(u_ref, w_ref, sc_ref, out_ref, pooled_ref, halo):
        i = pl.program_id(0)

        @pl.when(i == 0)
        def _():
            halo[...] = jnp.zeros_like(halo)

        ub = u_ref[...]
        ext = jnp.concatenate([halo[...], ub], axis=0)
        halo[...] = ub[tm - HALO:, :]
        tpos = (i * tm + lax.broadcasted_iota(jnp.int32, (tm, 1), 0)).astype(F32)
        for g, w in enumerate(POOL_WINDOWS):
            lanes = slice(g * 128, (g + 1) * 128)
            s = ext[:, lanes]
            sh = 1
            while sh < w:
                s = s + pltpu.roll(s, sh, 0)
                sh *= 2
            cnt = jnp.minimum(tpos + 1.0, float(w))
            pb = (s[HALO:, :] / cnt - ub[:, lanes]).astype(BF16)
            z = _mm(pb, w_ref[g].astype(BF16))
            out_ref[:, lanes] = (z * sc_ref[:, lanes]).astype(BF16)
            pooled_ref[:, lanes] = pb

    return pl.pallas_call(
        body, name="pool_fwd", grid=(T // tm,),
        in_specs=[_rows(tm, POOL_W), _full((4, 128, 128)), _full((1, POOL_W))],
        out_specs=[_rows(tm, POOL_W), _rows(tm, POOL_W)],
        out_shape=[_sds((T, POOL_W), BF16), _sds((T, POOL_W), BF16)],
        scratch_shapes=[pltpu.VMEM((HALO, POOL_W), F32)],
        compiler_params=_cp("arbitrary"),
    )(u, pool_w, pool_scale)


FF_CHUNKS = ((0, 1024), (1024, 2048), (2048, D_FF))


def _out_ffn_fwd(attn, pool, x, w_out, mod8, g8, wg, wu, wd, target=None):
    T = x.shape[0]
    tm = min(T, 256)
    last = target is not None

    def body(*refs):
        a_ref, p_ref, xin_ref, wo_ref, mod_ref, g_ref, wg_ref, wu_ref, wd_ref = refs[:9]
        t_ref = refs[9] if last else None
        mix_ref, x1_ref, h_ref, act_ref, ga_ref, gb_ref, f_ref, x2_ref = refs[9 + last:17 + last]
        mix = _mm(a_ref[...], wo_ref[0:ATTN_W, :]) + _mm(p_ref[...], wo_ref[ATTN_W:, :])
        mix_ref[...] = mix
        xf = xin_ref[...] + mod_ref[2:3, :] * (mix * _rstd(mix) * g_ref[1:2, :])
        x1_ref[...] = xf
        h = (xf * _rstd(xf) * g_ref[2:3, :]) * (1.0 + mod_ref[4:5, :]) + mod_ref[3:4, :]
        hb = h.astype(BF16)
        h_ref[...] = hb
        f = jnp.zeros((tm, D_MODEL), F32)
        for lo, hi in FF_CHUNKS:
            a = _nt(hb, wg_ref[lo:hi, :])
            b = _nt(hb, wu_ref[lo:hi, :])
            sig = jax.nn.sigmoid(a)
            sl = a * sig
            act = (sl * b).astype(BF16)
            act_ref[:, lo:hi] = act
            ga_ref[:, lo:hi] = (b * (sig * (1.0 + a * (1.0 - sig)))).astype(BF16)
            gb_ref[:, lo:hi] = sl.astype(BF16)
            f = f + _mm(act, wd_ref[lo:hi, :])
        f_ref[...] = f
        x2 = xf + mod_ref[5:6, :] * (f * _rstd(f) * g_ref[3:4, :])
        if not last:
            x2_ref[...] = x2
        else:
            loss_ref = refs[18]

            @pl.when(pl.program_id(0) == 0)
            def _():
                loss_ref[...] = jnp.zeros_like(loss_ref)

            e = x2 - t_ref[...]
            x2_ref[...] = e * (1.0 / D_MODEL)
            loss_ref[...] += 0.5 * jnp.sum(jnp.mean(e * e, axis=-1, keepdims=True), axis=0, keepdims=True)

    act_shape = _sds((T, D_FF), BF16)
    wide = _sds((T, D_MODEL), F32)
    weights = [_resident((D_FF, D_MODEL))] * 3
    return pl.pallas_call(
        body, name="out_ffn_fwd_loss" if last else "out_ffn_fwd", grid=(T // tm,),
        in_specs=[_rows(tm, ATTN_W), _rows(tm, POOL_W), _rows(tm, D_MODEL), _resident((D_MODEL, D_MODEL)),
                  _full((8, D_MODEL)), _full((8, D_MODEL)), *weights]
        + ([_rows(tm, D_MODEL)] if last else []),
        out_specs=[_rows(tm, D_MODEL), _rows(tm, D_MODEL), _rows(tm, D_MODEL), _rows(tm, D_FF), _rows(tm, D_FF),
                   _rows(tm, D_FF), _rows(tm, D_MODEL), _rows(tm, D_MODEL)] + ([_full((8, 128))] if last else []),
        out_shape=[wide, wide, _sds((T, D_MODEL), BF16), act_shape, act_shape, act_shape, wide, wide]
        + ([_sds((8, 128), F32)] if last else []),
        compiler_params=_cp("arbitrary" if last else "parallel", vmem=FFN_VMEM_LIMIT),
    )(attn, pool, x, w_out, mod8, g8, wg, wu, wd, *([target] if last else []))


def _ffn_bwd(dx2, f, ga, gb, x1, mod8, g8, wg, wu, wd):
    T = dx2.shape[0]
    tm = min(T, 256)

    def body(dx_ref, f_ref, ga_ref, gb_ref, x_ref, mod_ref, g_ref, wg_ref, wu_ref, wd_ref,
             dx1_ref, df_ref, da_ref, db_ref, red_ref):
        @pl.when(pl.program_id(0) == 0)
        def _():
            red_ref[...] = jnp.zeros_like(red_ref)

        dx = dx_ref[...]
        fv = f_ref[...]
        rstd = _rstd(fv)
        fhat = fv * rstd
        gpost = g_ref[3:4, :]
        gate = mod_ref[5:6, :]
        df, s_post = _norm_gain_bwd(dx, fhat, rstd, gate * gpost)
        red_ref[0:1, :] += gpost * s_post
        red_ref[1:2, :] += gate * s_post
        dfb = df.astype(BF16)
        df_ref[...] = dfb
        dh = jnp.zeros((tm, D_MODEL), F32)
        for lo, hi in FF_CHUNKS:
            dact = _nt(dfb, wd_ref[lo:hi, :])
            da = (dact * ga_ref[:, lo:hi].astype(F32)).astype(BF16)
            db = (dact * gb_ref[:, lo:hi].astype(F32)).astype(BF16)
            da_ref[:, lo:hi] = da
            db_ref[:, lo:hi] = db
            dh = dh + _mm(da, wg_ref[lo:hi, :]) + _mm(db, wu_ref[lo:hi, :])
        xf = x_ref[...]
        rstd1 = _rstd(xf)
        xhat = xf * rstd1
        gpre = g_ref[2:3, :]
        scale1 = 1.0 + mod_ref[4:5, :]
        dxn, s_pre = _norm_gain_bwd(dh, xhat, rstd1, scale1 * gpre)
        red_ref[2:3, :] += _colsum(dh)
        red_ref[3:4, :] += gpre * s_pre
        red_ref[4:5, :] += scale1 * s_pre
        dx1_ref[...] = dx + dxn

    act_shape = _sds((T, D_FF), BF16)
    return pl.pallas_call(
        body, name="ffn_bwd", grid=(T // tm,),
        in_specs=[_rows(tm, D_MODEL), _rows(tm, D_MODEL), _rows(tm, D_FF), _rows(tm, D_FF), _rows(tm, D_MODEL),
                  _full((8, D_MODEL)), _full((8, D_MODEL)),
                  _resident((D_FF, D_MODEL)), _resident((D_FF, D_MODEL)), _resident((D_FF, D_MODEL))],
        out_specs=[_rows(tm, D_MODEL), _rows(tm, D_MODEL), _rows(tm, D_FF), _rows(tm, D_FF), _full((8, D_MODEL))],
        out_shape=[_sds((T, D_MODEL), F32), _sds((T, D_MODEL), BF16), act_shape, act_shape, _sds((8, D_MODEL), F32)],
        compiler_params=_cp("arbitrary"),
    )(dx2, f, ga, gb, x1, mod8, g8, wg, wu, wd)


def _wgrad(a, b, name, after=None):
    T, K = a.shape
    N = b.shape[1]
    tt = min(T, WGRAD_TOKENS)
    tk = next(c for c in (1408, 640, 512, 256, 128) if K % c == 0)

    def body(a_ref, b_ref, *rest):
        o_ref = rest[-1]

        @pl.when(pl.program_id(1) == 0)
        def _():
            o_ref[...] = jnp.zeros_like(o_ref)

        o_ref[...] += _tn(a_ref[...], b_ref[...])

    extra = [] if after is None else [after]
    return pl.pallas_call(
        body, name=name, grid=(K // tk, T // tt),
        in_specs=[pl.BlockSpec((tt, tk), lambda i, t: (t, i)), pl.BlockSpec((tt, N), lambda i, t: (t, 0))]
        + [pl.BlockSpec(memory_space=pl.ANY)] * len(extra),
        out_specs=pl.BlockSpec((tk, N), lambda i, t: (i, 0)),
        out_shape=_sds((K, N), F32),
        compiler_params=_cp("parallel", "arbitrary"),
    )(a, b, *extra)


def _mix_bwd(dx1, mix, mod8, g8, w_out):
    T = dx1.shape[0]
    tm = min(T, 1024)

    def body(dx_ref, mix_ref, mod_ref, g_ref, w_ref, dmix_ref, da_ref, dp_ref, red_ref):
        @pl.when(pl.program_id(0) == 0)
        def _():
            red_ref[...] = jnp.zeros_like(red_ref)

        dx = dx_ref[...]
        mv = mix_ref[...]
        rstd = _rstd(mv)
        mhat = mv * rstd
        gpost = g_ref[1:2, :]
        gate = mod_ref[2:3, :]
        dm, s_post = _norm_gain_bwd(dx, mhat, rstd, gate * gpost)
        red_ref[0:1, :] += gpost * s_post
        red_ref[1:2, :] += gate * s_post
        dmb = dm.astype(BF16)
        dmix_ref[...] = dmb
        dap = _nt(dmb, w_ref[...])
        da_ref[...] = dap[:, 0:ATTN_W].astype(BF16)
        dp_ref[...] = dap[:, ATTN_W:].astype(BF16)

    return pl.pallas_call(
        body, name="mix_bwd", grid=(T // tm,),
        in_specs=[_rows(tm, D_MODEL), _rows(tm, D_MODEL), _full((8, D_MODEL)), _full((8, D_MODEL)),
                  _resident((D_MODEL, D_MODEL))],
        out_specs=[_rows(tm, D_MODEL), _rows(tm, ATTN_W), _rows(tm, POOL_W), _full((8, D_MODEL))],
        out_shape=[_sds((T, D_MODEL), BF16), _sds((T, ATTN_W), BF16), _sds((T, POOL_W), BF16),
                   _sds((8, D_MODEL), F32)],
        compiler_params=_cp("arbitrary"),
    )(dx1, mix, mod8, g8, w_out)


def _attn_bwd(q, kd, vd, lse, dattn, sink_b):
    T = q.shape[0]
    nb = T // BLK
    nq = ATTN_BWD_BLOCKS if nb % ATTN_BWD_BLOCKS == 0 else 2
    assert nb % nq == 0
    nstep = nb // nq

    def body(q_ref, do_ref, lse_ref, kp_ref, kc_ref, vp_ref, vc_ref, sk_ref,
             dq_ref, dkm_ref, dkt_ref, dvm_ref, dvt_ref, dsk_ref, carry_k, carry_v):
        i = pl.program_id(0)

        @pl.when(i == 0)
        def _():
            carry_k[...] = jnp.zeros_like(carry_k)
            carry_v[...] = jnp.zeros_like(carry_v)
            dsk_ref[...] = jnp.zeros_like(dsk_ref)

        @pl.when(i < nstep)
        def _():
            for j in range(N_HEADS // GROUP):
                lanes = slice(j * 128, (j + 1) * 128)
                parts_k, parts_v = [], []
                for sub in range(nq):
                    rows = slice(sub * BLK, (sub + 1) * BLK)
                    before = slice((sub - 1) * BLK, sub * BLK)
                    valid = _band_mask(nq * i + sub)
                    k_prev = kp_ref[:, lanes] if sub == 0 else kc_ref[before, lanes]
                    v_prev = vp_ref[:, lanes] if sub == 0 else vc_ref[before, lanes]
                    kcat = jnp.concatenate([k_prev, kc_ref[rows, lanes]], axis=0)
                    vcat = jnp.concatenate([v_prev, vc_ref[rows, lanes]], axis=0)
                    qs = _stack_heads(q_ref, j, rows)
                    dos = _stack_heads(do_ref, j, rows)
                    lse = _head_row(lse_ref, j, sub * N_HEADS)
                    p = jnp.exp(jnp.where(valid, _nt(kcat, qs), NEG_INF) - lse)
                    dp = _nt(vcat, dos)
                    delta = jnp.sum(p * dp, axis=0, keepdims=True)
                    ds = (p * (dp - delta)).astype(BF16)
                    sink_term = jnp.exp(_head_row(sk_ref, j) - lse) * delta
                    for r in range(GROUP):
                        h = GROUP * j + r
                        dsk_ref[h:h + 1, :] += -jnp.sum(sink_term[:, r * 128:(r + 1) * 128], axis=1, keepdims=True)
                    dq_ref[rows, 2 * j * 128:(2 * j + 2) * 128] = jnp.concatenate(
                        _unstack_heads(_tn(ds, kcat)), axis=1)
                    parts_k.append(_mm(ds, qs))
                    parts_v.append(_mm(p.astype(BF16), dos))
                dkt_ref[:, lanes] = carry_k[:, lanes] + parts_k[0][0:BLK]
                dvt_ref[:, lanes] = carry_v[:, lanes] + parts_v[0][0:BLK]
                for s in range(nq - 1):
                    dkm_ref[s * BLK:(s + 1) * BLK, lanes] = parts_k[s][BLK:] + parts_k[s + 1][0:BLK]
                    dvm_ref[s * BLK:(s + 1) * BLK, lanes] = parts_v[s][BLK:] + parts_v[s + 1][0:BLK]
                carry_k[:, lanes] = parts_k[nq - 1][BLK:]
                carry_v[:, lanes] = parts_v[nq - 1][BLK:]

        @pl.when(i == nstep)
        def _():
            dkt_ref[...] = carry_k[...]
            dvt_ref[...] = carry_v[...]

    cur = lambda i: (jnp.minimum(i, nstep - 1), 0)
    prev = lambda i: (jnp.minimum(jnp.maximum(nq * i - 1, 0), nb - 1), 0)
    tail = lambda i: (jnp.maximum(i - 1, 0), 0)
    main_shape = _sds((nstep * (nq - 1) * BLK, KVD_W), F32)
    tail_shape = _sds((nstep * BLK, KVD_W), F32)
    main_spec = pl.BlockSpec(((nq - 1) * BLK, KVD_W), cur)
    tail_spec = pl.BlockSpec((BLK, KVD_W), tail)
    return pl.pallas_call(
        body, name="attn_bwd", grid=(nstep + 1,),
        in_specs=[pl.BlockSpec((nq * BLK, ATTN_W), cur), pl.BlockSpec((nq * BLK, ATTN_W), cur),
                  pl.BlockSpec((nq * N_HEADS, 128), cur),
                  pl.BlockSpec((BLK, KVD_W), prev), pl.BlockSpec((nq * BLK, KVD_W), cur),
                  pl.BlockSpec((BLK, KVD_W), prev), pl.BlockSpec((nq * BLK, KVD_W), cur),
                  _full((8, 128))],
        out_specs=[pl.BlockSpec((nq * BLK, ATTN_W), cur), main_spec, tail_spec, main_spec, tail_spec, _full((8, 128))],
        out_shape=[_sds((T, ATTN_W), F32), main_shape, tail_shape, main_shape, tail_shape, _sds((8, 128), F32)],
        scratch_shapes=[pltpu.VMEM((BLK, KVD_W), F32), pltpu.VMEM((BLK, KVD_W), F32)],
        compiler_params=_cp("arbitrary"),
    )(q, dattn, lse, kd, kd, vd, vd, sink_b)


def _pool_bwd(dpool, pooled, pool_w, pool_scale):
    T = dpool.shape[0]
    tm = min(T, 1024)
    nbk = T // tm
    ext_rows = tm + HALO

    def body(dp_ref, pl_ref, w_ref, sc_ref, du_ref, dw_ref, dsc_ref, halo):
        i = pl.program_id(0)

        @pl.when(i == 0)
        def _():
            halo[...] = jnp.zeros_like(halo)
            dw_ref[...] = jnp.zeros_like(dw_ref)
            dsc_ref[...] = jnp.zeros_like(dsc_ref)

        blk = nbk - 1 - i
        tpos = (blk * tm + lax.broadcasted_iota(jnp.int32, (tm, 1), 0)).astype(F32)
        for g, w in enumerate(POOL_WINDOWS):
            lanes = slice(g * 128, (g + 1) * 128)
            dp = dp_ref[:, lanes].astype(F32)
            pb = pl_ref[:, lanes]
            wg = w_ref[g].astype(BF16)
            z = _mm(pb, wg)
            dsc_ref[0:1, lanes] += _colsum(dp * z)
            dz = (dp * sc_ref[:, lanes]).astype(BF16)
            dw_ref[g] += _tn(pb, dz)
            dpl = _nt(dz, wg)
            e = dpl / jnp.minimum(tpos + 1.0, float(w))
            s = jnp.concatenate([e, halo[:, lanes]], axis=0)
            halo[:, lanes] = e[0:HALO, :]
            sh = 1
            while sh < w:
                s = s + pltpu.roll(s, ext_rows - sh, 0)
                sh *= 2
            du_ref[:, lanes] = s[0:tm, :] - dpl

    rev = lambda i: (nbk - 1 - i, 0)
    return pl.pallas_call(
        body, name="pool_bwd", grid=(nbk,),
        in_specs=[pl.BlockSpec((tm, POOL_W), rev), pl.BlockSpec((tm, POOL_W), rev),
                  _full((4, 128, 128)), _full((1, POOL_W))],
        out_specs=[pl.BlockSpec((tm, POOL_W), rev), _full((4, 128, 128)), _full((8, POOL_W))],
        out_shape=[_sds((T, POOL_W), F32), _sds((4, 128, 128), F32), _sds((8, POOL_W), F32)],
        scratch_shapes=[pltpu.VMEM((HALO, POOL_W), F32)],
        compiler_params=_cp("arbitrary"),
    )(dpool, pooled, pool_w, pool_scale)


def _interleave_groups(main, tail):
    groups = tail.shape[0] // BLK
    m = main.shape[0] // groups
    parts = []
    for b in range(groups):
        parts += [main[b * m:(b + 1) * m], tail[b * BLK:(b + 1) * BLK]]
    return jnp.concatenate(parts, axis=0)


def _in_bwd(dq, dk_mt, dv_mt, du, rc, rs1, rs2, x, dx1, mod8, g8, w_in):
    T = x.shape[0]
    tm = min(T, 512)
    nq = T // dk_mt[1].shape[0]
    t_tail = tm // nq
    t_main = tm - t_tail

    def body(dq_ref, dkm_ref, dkt_ref, dvm_ref, dvt_ref, du_ref, c_ref, s1_ref, s2_ref, x_ref, dx1_ref, mod_ref,
             g_ref, w_ref, dx_ref, dproj_ref, red_ref, dbin_ref):
        dk_all = _interleave_groups(dkm_ref[...], dkt_ref[...])
        dv_all = _interleave_groups(dvm_ref[...], dvt_ref[...])

        @pl.when(pl.program_id(0) == 0)
        def _():
            red_ref[...] = jnp.zeros_like(red_ref)
            dbin_ref[...] = jnp.zeros_like(dbin_ref)

        c = c_ref[...]
        s1 = s1_ref[...]
        s2 = s2_ref[...]
        dqp = _rot_bwd(dq_ref[...] * (HEAD ** -0.5), jnp.tile(c, (1, 4)), jnp.tile(s1, (1, 4)), jnp.tile(s2, (1, 4)))
        dkp = _rot_bwd(_fold_dup(dk_all), c, s1, s2)
        pieces = ((0, ATTN_W, dqp), (ATTN_W, ATTN_W + KV_W, dkp),
                  (ATTN_W + KV_W, ATTN_W + 2 * KV_W, _fold_dup(dv_all)), (ATTN_W + 2 * KV_W, IN_W, du_ref[...]))
        for lo, hi, val in pieces:
            dbin_ref[0:1, lo:hi] += _colsum(val)
            dproj_ref[:, lo:hi] = val.astype(BF16)
        dh = _mm(dproj_ref[...], w_ref[...])
        xf = x_ref[...]
        rstd = _rstd(xf)
        xhat = xf * rstd
        gpre = g_ref[0:1, :]
        scale1 = 1.0 + mod_ref[1:2, :]
        dxn, s_pre = _norm_gain_bwd(dh, xhat, rstd, scale1 * gpre)
        red_ref[0:1, :] += _colsum(dh)
        red_ref[1:2, :] += gpre * s_pre
        red_ref[2:3, :] += scale1 * s_pre
        dx_ref[...] = dx1_ref[...] + dxn

    return pl.pallas_call(
        body, name="in_bwd", grid=(T // tm,),
        in_specs=[_rows(tm, ATTN_W), *[_rows(t_main, KVD_W), _rows(t_tail, KVD_W)] * 2, _rows(tm, POOL_W),
                  _rows(tm, 128), _rows(tm, 128), _rows(tm, 128), _rows(tm, D_MODEL), _rows(tm, D_MODEL),
                  _full((8, D_MODEL)), _full((8, D_MODEL)), _resident((IN_W, D_MODEL))],
        out_specs=[_rows(tm, D_MODEL), _rows(tm, IN_W), _full((8, D_MODEL)), _full((8, IN_W))],
        out_shape=[_sds((T, D_MODEL), F32), _sds((T, IN_W), BF16), _sds((8, D_MODEL), F32), _sds((8, IN_W), F32)],
        compiler_params=_cp("arbitrary"),
    )(dq, *dk_mt, *dv_mt, du, rc, rs1, rs2, x, dx1, mod8, g8, w_in)


def _mod_fwd(c_all, ada_w, ada_b_sh):
    tn = 512

    def body(c_ref, w_ref, b_ref, o_ref):
        cv = c_ref[...]
        ca = (cv * jax.nn.sigmoid(cv)).astype(BF16)
        o_ref[...] = _mm(ca, w_ref[...].astype(BF16)) + b_ref[...]

    return pl.pallas_call(
        body, name="mod_fwd", grid=(2, ADA_SH // tn),
        in_specs=[_full((8, D_MODEL)), pl.BlockSpec((None, D_MODEL, tn), lambda l, j: (l, 0, j)),
                  pl.BlockSpec((None, 1, tn), lambda l, j: (l, 0, j))],
        out_specs=pl.BlockSpec((None, 8, tn), lambda l, j: (l, 0, j)),
        out_shape=_sds((2, 8, ADA_SH), F32),
        compiler_params=_cp("parallel", "parallel"),
    )(c_all, ada_w, ada_b_sh)


def _ada_wgrad(c_all_t, dmod_sh):
    tn = 512

    def body(c_ref, d_ref, o_ref):
        cv = c_ref[...]
        ca = cv * jax.nn.sigmoid(cv)
        o_ref[...] = jnp.dot(ca, d_ref[...], preferred_element_type=F32, precision=lax.Precision.HIGHEST)

    return pl.pallas_call(
        body, name="ada_wgrad", grid=(2, ADA_SH // tn),
        in_specs=[_full((D_MODEL, 8)), pl.BlockSpec((None, 8, tn), lambda l, j: (l, 0, j))],
        out_specs=pl.BlockSpec((None, D_MODEL, tn), lambda l, j: (l, 0, j)),
        out_shape=_sds((2, D_MODEL, ADA_SH), F32),
        compiler_params=_cp("parallel", "parallel"),
    )(c_all_t, dmod_sh)


def _sum_devices(g):
    R = g.shape[1]

    def body(g_ref, o_ref):
        acc = g_ref[0]
        for d in range(1, N_DEV):
            acc = acc + g_ref[d]
        o_ref[...] = acc

    return pl.pallas_call(
        body, name="sum_devices", grid=(1,),
        in_specs=[_full((N_DEV, R, 128))], out_specs=_full((R, 128)), out_shape=_sds((R, 128), F32),
        compiler_params=_cp("arbitrary"),
    )(g)


def _adamw(w, g, m, v, name):
    R, C = w.shape
    tr = R
    for cand in (256, 128, 64, 32, 16, 8):
        if R % cand == 0 and cand * C * 4 <= 2 * 1024 * 1024:
            tr = cand
            break

    def body(w_ref, g_ref, m_ref, v_ref, d_ref, nm_ref, nv_ref):
        gv = g_ref[...]
        mn = ADAM_B1 * m_ref[...] + (1.0 - ADAM_B1) * gv
        vn = ADAM_B2 * v_ref[...] + (1.0 - ADAM_B2) * (gv * gv)
        m_hat = mn / (1.0 - ADAM_B1 ** ADAM_STEP)
        v_hat = vn / (1.0 - ADAM_B2 ** ADAM_STEP)
        d_ref[...] = -ADAM_LR * (m_hat / (jnp.sqrt(v_hat) + ADAM_EPS) + ADAM_WD * w_ref[...])
        nm_ref[...] = mn
        nv_ref[...] = vn

    spec = pl.BlockSpec((tr, C), lambda i: (i, 0))
    out = _sds((R, C), F32)
    return pl.pallas_call(
        body, name=name, grid=(R // tr,),
        in_specs=[spec] * 4, out_specs=[spec] * 3, out_shape=[out] * 3,
        compiler_params=_cp("parallel"),
    )(w, g, m, v)


def _adamw_nd(w, g, m, v, name):
    shape = w.shape
    if w.ndim == 2 and shape[1] < 128:
        view = (1, shape[0] * shape[1])
    else:
        view = (-1, shape[-1])
    outs = _adamw(*[t.reshape(view) for t in (w, g, m, v)], name=name)
    return [o.reshape(shape) for o in outs]


def _coords():
    return lax.axis_index("x"), lax.axis_index("y"), lax.axis_index("c")


def _other_chips(x, y):
    return [(1 - x, y), (x, 1 - y), (1 - x, 1 - y)]


def _allgather8(blk, name):
    m_per, n = blk.shape

    def body(x_ref, out_ref, send_sems, recv_sems, local_sem):
        x, y, c = _coords()
        me, sibling = (x, y, c), (x, y, 1 - c)
        chips = _other_chips(x, y)

        def rows(px, py, pc):
            return out_ref.at[pl.ds((4 * px + 2 * py + pc) * m_per, m_per), :]

        def copy(k, block, to, src=None):
            return pltpu.make_async_remote_copy(
                src_ref=rows(*block) if src is None else src, dst_ref=rows(*block),
                send_sem=send_sems.at[k], recv_sem=recv_sems.at[k], device_id=to, device_id_type=MESH)

        mine = pltpu.make_async_copy(x_ref, rows(*me), local_sem)
        mine.start()
        first = [copy(0, me, sibling, src=x_ref)]
        first += [copy(1 + j, me, (*chip, c), src=x_ref) for j, chip in enumerate(chips)]
        for cp in first:
            cp.start()
        passed = [copy(4 + j, (*chip, c), sibling) for j, chip in enumerate(chips)]
        for j, chip in enumerate(chips):
            copy(1 + j, (*chip, c), me).wait_recv()
            passed[j].start()
        copy(0, sibling, me).wait_recv()
        for j, chip in enumerate(chips):
            copy(4 + j, (*chip, 1 - c), me).wait_recv()
        for cp in first + passed:
            cp.wait_send()
        mine.wait()

    return pl.pallas_call(
        body, name=name,
        out_shape=_sds((N_DEV * m_per, n), blk.dtype),
        in_specs=[pl.BlockSpec(memory_space=pltpu.VMEM)],
        out_specs=pl.BlockSpec(memory_space=pltpu.VMEM),
        scratch_shapes=[pltpu.SemaphoreType.DMA((7,)), pltpu.SemaphoreType.DMA((7,)), pltpu.SemaphoreType.DMA],
        compiler_params=pltpu.CompilerParams(vmem_limit_bytes=VMEM_LIMIT),
    )(blk)


def _row_tile(r, n):
    for cand in range(r, 15, -16):
        if r % cand == 0 and cand % 16 == 0 and cand * n * 4 <= 2 * 1024 * 1024:
            return cand
    return r


def _cast_slot(w, chip, name):
    r, n = w.shape
    tr = _row_tile(r, n)

    def body(chip_ref, w_ref, o_ref):
        o_ref[...] = w_ref[...].astype(BF16)

    grid_spec = pltpu.PrefetchScalarGridSpec(
        num_scalar_prefetch=1, grid=(r // tr,),
        in_specs=[pl.BlockSpec((tr, n), lambda i, ch: (i, 0))],
        out_specs=pl.BlockSpec((None, tr, n), lambda i, ch: (ch[0], i, 0)))
    return pl.pallas_call(
        body, name=name, grid_spec=grid_spec, out_shape=_sds((N_SHARD, r, n), BF16),
        compiler_params=_cp("arbitrary"),
    )(chip, w)


def _join_halves(tots, name):
    nt = len(tots)
    hom = [pl.BlockSpec(memory_space=pl.ANY)] * nt

    def body(*refs):
        outs = refs[nt:2 * nt]
        send_sems, recv_sems = refs[2 * nt:]
        x, y, c = _coords()
        sibling = (x, y, 1 - c)
        cps = []
        for t in range(nt):
            cp = pltpu.make_async_remote_copy(
                src_ref=outs[t].at[c], dst_ref=outs[t].at[c],
                send_sem=send_sems.at[t], recv_sem=recv_sems.at[t], device_id=sibling, device_id_type=MESH)
            cp.start()
            cps.append(cp)
        for t in range(nt):
            pltpu.make_async_remote_copy(
                src_ref=outs[t].at[c], dst_ref=outs[t].at[1 - c],
                send_sem=send_sems.at[t], recv_sem=recv_sems.at[t], device_id=sibling, device_id_type=MESH).wait_recv()
        for cp in cps:
            cp.wait_send()

    return pl.pallas_call(
        body, name=name,
        out_shape=[_sds(t.shape, t.dtype) for t in tots],
        in_specs=hom, out_specs=hom,
        input_output_aliases={t: t for t in range(nt)},
        scratch_shapes=[pltpu.SemaphoreType.DMA((nt,)), pltpu.SemaphoreType.DMA((nt,))],
    )(*tots)


def _pair_sum(g, recv, core, chip, name):
    _, _, r, n = g.shape
    tr = _row_tile(r, n)

    def body(core_ref, chip_ref, g_ref, r_ref, sb_ref, own_ref):
        tot = g_ref[...] + r_ref[...]
        sb_ref[...] = tot.astype(BF16)

        @pl.when(pl.program_id(1) == chip_ref[0])
        def _():
            own_ref[...] = tot

    grid_spec = pltpu.PrefetchScalarGridSpec(
        num_scalar_prefetch=2, grid=(r // tr, N_SHARD),
        in_specs=[pl.BlockSpec((None, None, tr, n), lambda i, s, co, ch: (s, co[0], i, 0)),
                  pl.BlockSpec((None, tr, n), lambda i, s, co, ch: (s, i, 0))],
        out_specs=[pl.BlockSpec((None, tr, n), lambda i, s, co, ch: (s, i, 0)),
                   pl.BlockSpec((tr, n), lambda i, s, co, ch: (i, 0))])
    return pl.pallas_call(
        body, name=name, grid_spec=grid_spec,
        out_shape=[_sds((N_SHARD, r, n), BF16), _sds((r, n), F32)],
        compiler_params=_cp("arbitrary", "arbitrary"),
    )(core, chip, g, recv)


def _chip_sum(own, recv, core, name):
    r, n = own.shape
    tr = _row_tile(r, n)

    def body(core_ref, o_ref, r_ref, t_ref):
        acc = o_ref[...]
        for j in range(3):
            acc = acc + r_ref[j].astype(F32)
        t_ref[...] = acc

    grid_spec = pltpu.PrefetchScalarGridSpec(
        num_scalar_prefetch=1, grid=(r // tr,),
        in_specs=[pl.BlockSpec((tr, n), lambda i, co: (i, 0)), pl.BlockSpec((3, tr, n), lambda i, co: (0, i, 0))],
        out_specs=pl.BlockSpec((None, tr, n), lambda i, co: (co[0], i, 0)))
    return pl.pallas_call(
        body, name=name, grid_spec=grid_spec, out_shape=_sds((2, r, n), F32),
        compiler_params=_cp("arbitrary"),
    )(core, own, recv)


_HBM = pl.BlockSpec(memory_space=pltpu.HBM)
_SEM = pl.BlockSpec(memory_space=pltpu.SEMAPHORE)
_EFFECT = pltpu.SideEffectType.DATAFLOW_SIDE_EFFECTING


def _ici_copies(srcs, dsts, send_sems, recv_sems, send_view, recv_view):
    x, y, c = _coords()
    out = []
    if send_view is None:
        for t in range(len(srcs)):
            r = srcs[t].shape[1] // 2
            out.append(pltpu.make_async_remote_copy(
                src_ref=srcs[t].at[:, pl.ds((1 - c) * r, r)], dst_ref=dsts[t],
                send_sem=send_sems.at[3 * t], recv_sem=recv_sems.at[3 * t],
                device_id=(x, y, 1 - c), device_id_type=MESH))
        return out
    for t in range(len(srcs)):
        for j, chip in enumerate(_other_chips(x, y)):
            out.append(pltpu.make_async_remote_copy(
                src_ref=send_view(srcs[t], chip, j, (x, y), c), dst_ref=recv_view(dsts[t], chip, j, (x, y), c),
                send_sem=send_sems.at[3 * t + j], recv_sem=recv_sems.at[3 * t + j],
                device_id=(*chip, c), device_id_type=MESH))
    return out


def _ici_start(srcs, dsts, after, send_view, recv_view, name):
    nt = len(srcs)
    inplace = dsts is None
    nbuf = nt if inplace else 2 * nt

    def body(*refs):
        send_sems, recv_sems = refs[nbuf + 1], refs[nbuf + 2]
        s_out = refs[nbuf + 3:nbuf + 3 + nt]
        d_out = s_out if inplace else refs[nbuf + 3 + nt:nbuf + 3 + 2 * nt]
        token = refs[-1]
        for cp in _ici_copies(s_out, d_out, send_sems, recv_sems, send_view, recv_view):
            cp.start()
        token[...] = jnp.zeros_like(token)

    bufs = list(srcs) + ([] if inplace else list(dsts))
    res = pl.pallas_call(
        body, name=name,
        out_shape=(pltpu.SemaphoreType.DMA((3 * nt,)), pltpu.SemaphoreType.DMA((3 * nt,)),
                   *[pltpu.HBM(b.shape, b.dtype) for b in bufs], _sds((8, 128), F32)),
        in_specs=[_HBM] * nbuf + [pl.BlockSpec(memory_space=pl.ANY)],
        out_specs=(_SEM, _SEM, *[_HBM] * nbuf, pl.BlockSpec(memory_space=pltpu.VMEM)),
        input_output_aliases={i: 2 + i for i in range(nbuf)},
        compiler_params=pltpu.CompilerParams(has_side_effects=_EFFECT),
    )(*[pltpu.with_memory_space_constraint(b, pltpu.HBM) for b in bufs], after)
    send_sems, recv_sems = res[0], res[1]
    s_thru = list(res[2:2 + nt])
    d_thru = s_thru if inplace else list(res[2 + nt:2 + 2 * nt])
    return send_sems, recv_sems, s_thru, d_thru, res[-1]


def _ici_wait(send_sems, recv_sems, srcs, dsts, after, send_view, recv_view, name):
    nt = len(srcs)
    inplace = dsts is None
    nbuf = nt if inplace else 2 * nt

    def body(*refs):
        send_ref, recv_ref = refs[nbuf], refs[nbuf + 1]
        s_out = refs[nbuf + 3:nbuf + 3 + nt]
        d_out = s_out if inplace else refs[nbuf + 3 + nt:nbuf + 3 + 2 * nt]
        for cp in _ici_copies(s_out, d_out, send_ref, recv_ref, send_view, recv_view):
            cp.wait_send()
            cp.wait_recv()

    bufs = list(srcs) + ([] if inplace else list(dsts))
    res = pl.pallas_call(
        body, name=name,
        out_shape=tuple(pltpu.HBM(b.shape, b.dtype) for b in bufs),
        in_specs=[_HBM] * nbuf + [_SEM, _SEM, pl.BlockSpec(memory_space=pl.ANY)],
        out_specs=tuple([_HBM] * nbuf),
        input_output_aliases={i: i for i in range(nbuf)},
        compiler_params=pltpu.CompilerParams(has_side_effects=_EFFECT),
    )(*bufs, send_sems, recv_sems, after)
    return list(res[:nt]) if inplace else (list(res[:nt]), list(res[nt:]))


def _w_half(buf, chip, c):
    r = buf.shape[1] // 2
    return buf.at[2 * chip[0] + chip[1], pl.ds(c * r, r)]


def _ag_send_view(buf, chip, j, me, c):
    return _w_half(buf, me, c)


def _ag_recv_view(buf, chip, j, me, c):
    return _w_half(buf, me, c)


def _rs_send_view(buf, chip, j, me, c):
    return buf.at[2 * chip[0] + chip[1]]


def _rs_recv_view(buf, chip, j, me, c):
    return buf.at[j]


def _ag_forward(bufs, name):
    nt = len(bufs)
    hom = [pl.BlockSpec(memory_space=pl.ANY)] * nt

    def body(*refs):
        outs = refs[nt:2 * nt]
        send_sems, recv_sems = refs[2 * nt:]
        x, y, c = _coords()
        sibling = (x, y, 1 - c)
        chips = _other_chips(x, y)

        def copy(t, j, hc):
            blk = _w_half(outs[t], chips[j], hc)
            return pltpu.make_async_remote_copy(
                src_ref=blk, dst_ref=blk, send_sem=send_sems.at[t, j], recv_sem=recv_sems.at[t, j],
                device_id=sibling, device_id_type=MESH)

        started = [copy(t, j, c) for t in range(nt) for j in range(3)]
        for cp in started:
            cp.start()
        for t in range(nt):
            for j in range(3):
                copy(t, j, 1 - c).wait_recv()
        for cp in started:
            cp.wait_send()

    return pl.pallas_call(
        body, name=name,
        out_shape=[_sds(b.shape, b.dtype) for b in bufs],
        in_specs=hom, out_specs=hom,
        input_output_aliases={t: t for t in range(nt)},
        scratch_shapes=[pltpu.SemaphoreType.DMA((nt, 3)), pltpu.SemaphoreType.DMA((nt, 3))],
    )(*bufs)


def _rs_swap_begin(grads, after, tag):
    land = [lax.empty((N_SHARD, g.shape[1] // 2, g.shape[2]), g.dtype) for g in grads]
    send_sems, recv_sems, s_thru, d_thru, token = _ici_start(grads, land, after, None, None, name="rs_swapgo_" + tag)
    return dict(sems=(send_sems, recv_sems), grads=s_thru, land=d_thru, tag=tag), token


def _rs_scatter_begin(swap, after):
    tag = swap["tag"]
    x, y, c = _coords()
    core = jnp.reshape(c, (1,)).astype(jnp.int32)
    chip = jnp.reshape(2 * x + y, (1,)).astype(jnp.int32)
    grads, recv = _ici_wait(*swap["sems"], swap["grads"], swap["land"], after, None, None, name="rs_swapend_" + tag)
    sums, owns = [], []
    for t, (g, rv) in enumerate(zip(grads, recv)):
        r = g.shape[1] // 2
        sb, own = _pair_sum(g.reshape(N_SHARD, 2, r, g.shape[2]), rv, core, chip, name=f"rs_pair_{tag}_{t}")
        sums.append(sb)
        owns.append(own)
    land = [lax.empty((3,) + s.shape[1:], s.dtype) for s in sums]
    send_sems, recv_sems, s_thru, d_thru, token = _ici_start(
        sums, land, after, _rs_send_view, _rs_recv_view, name="rs_start_" + tag)
    return dict(sems=(send_sems, recv_sems), sums=s_thru, land=d_thru, owns=owns, core=core, tag=tag), token


def _rs_end(state, after):
    tag = state["tag"]
    _, got = _ici_wait(*state["sems"], state["sums"], state["land"], after, _rs_send_view, _rs_recv_view,
                       name="rs_wait_" + tag)
    tots = [_chip_sum(o, gt, state["core"], name=f"rs_chip_{tag}_{t}")
            for t, (o, gt) in enumerate(zip(state["owns"], got))]
    full = _join_halves(tots, name="rs_join_" + tag)
    return [f.reshape(2 * f.shape[1], f.shape[2]) for f in full]


def _rope_lane_table():
    d = jnp.arange(128) % HEAD
    inv_freq = ROPE_THETA ** (-jnp.arange(0, ROT, 2, dtype=F32) / ROT)
    rot = d < ROT
    rows = [jnp.where(rot, inv_freq[d % (ROT // 2)], 0.0), rot.astype(F32),
            (d < ROT // 2).astype(F32), jnp.logical_and(d >= ROT // 2, rot).astype(F32)]
    return jnp.concatenate([jnp.stack(rows), jnp.zeros((4, 128), F32)], axis=0)


def _pad8(rows):
    return jnp.concatenate([rows, jnp.zeros((8 - rows.shape[0], rows.shape[1]), F32)], axis=0)


def kernel(x, c, positions, ada_w, ada_b, w_in, b_in, sinks, pool_w, pool_scale, w_out, w_gate, w_up, w_down, g_pre_mix, g_post_mix, g_pre_ffn, g_post_ffn, loss_target, m_ada_w, m_ada_b, m_w_in, m_b_in, m_sinks, m_pool_w, m_pool_scale, m_w_out, m_w_gate, m_w_up, m_w_down, m_g_pre_mix, m_g_post_mix, m_g_pre_ffn, m_g_post_ffn, v_ada_w, v_ada_b, v_w_in, v_b_in, v_sinks, v_pool_w, v_pool_scale, v_w_out, v_w_gate, v_w_up, v_w_down, v_g_pre_mix, v_g_post_mix, v_g_pre_ffn, v_g_post_ffn):
    T = x.shape[1]
    n_layers = ada_w.shape[0]
    ax, ay, ac = _coords()
    my_dev = 4 * ax + 2 * ay + ac
    my_chip = 2 * ax + ay
    x0 = x.reshape(T, D_MODEL)
    target = loss_target.reshape(T, D_MODEL)

    c_all = _allgather8(c.reshape(8, 128), name="ag_c").reshape(N_DEV, D_MODEL)
    ada_b_sh = lax.dynamic_slice_in_dim(ada_b, my_chip * ADA_SH, ADA_SH, axis=1).reshape(n_layers, 1, ADA_SH)
    mod_part = _mod_fwd(c_all, ada_w, ada_b_sh)
    mod_all = _allgather8(mod_part.reshape(n_layers * 8, ADA_SH), name="ag_mod")
    mod_all = mod_all.reshape(N_DEV, n_layers, 8, ADA_SH)[0::2]
    mod_mine = lax.dynamic_index_in_dim(mod_all, my_dev, axis=2, keepdims=False)
    mod = jnp.transpose(mod_mine, (1, 0, 2)).reshape(n_layers, 6, D_MODEL)

    chip1 = jnp.reshape(my_chip, (1,)).astype(jnp.int32)

    def tr(t):
        return jnp.transpose(t, (0, 2, 1))

    w_in_t, w_gate_t, w_up_t = tr(w_in), tr(w_gate), tr(w_up)

    def cast_layer(l):
        return [_cast_slot(w[l], chip1, name=f"cast_{nm}{l}")
                for nm, w in (("w_in", w_in_t), ("w_out", w_out), ("w_gate", w_gate_t), ("w_up", w_up_t),
                              ("w_down", w_down))]

    def as_operands(bufs):
        gin, gout, gg, gu, gd = bufs
        return (gin.reshape(IN_W, D_MODEL), gout.reshape(D_MODEL, D_MODEL), gg.reshape(D_FF, D_MODEL),
                gu.reshape(D_FF, D_MODEL), gd.reshape(D_FF, D_MODEL))

    bufs0 = cast_layer(0)
    in_send, in_recv, in_bufs, _, in_token = _ici_start(
        bufs0[:1], None, mod, _ag_send_view, _ag_recv_view, name="ag_start_0_in")
    pos_b = jnp.broadcast_to(positions.reshape(T, 1), (T, 128))
    rc, rs1, rs2 = _rope_tables(pos_b, _rope_lane_table() + in_token[0, 0])
    arrived = _ici_wait(in_send, in_recv, in_bufs, None, rc, _ag_send_view, _ag_recv_view, name="ag_wait_0_in")
    win0 = _ag_forward(arrived, name="ag_fwd_0_in")
    rest_send, rest_recv, rest_bufs, _, ag_token = _ici_start(
        bufs0[1:], None, win0[0], _ag_send_view, _ag_recv_view, name="ag_start_0")
    weights = [None] * n_layers

    saved = []
    xl = x0
    for l in range(n_layers):
        mod8 = _pad8(mod[l])
        if l + 1 < n_layers:
            ag_send, ag_recv, ag_bufs, _, ag_token = _ici_start(
                cast_layer(l + 1), None, ag_token, _ag_send_view, _ag_recv_view, name=f"ag_start_{l + 1}")
        if l == 0 or l + 1 < n_layers:
            mod8 = mod8 + ag_token[0, 0]
        g8 = _pad8(jnp.stack([g_pre_mix[l], g_post_mix[l], g_pre_ffn[l], g_post_ffn[l]]))
        sink_b = jnp.broadcast_to(sinks[l][:, None], (N_HEADS, 128))
        psc = pool_scale[l].reshape(1, POOL_W)
        win = win0[0].reshape(IN_W, D_MODEL) if l == 0 else weights[l][0]
        h, q, k, v, u = _fwd_in(xl, mod8, g8, win, b_in[l].reshape(1, IN_W), rc, rs1, rs2)
        attn, lse = _attn_fwd(q, k, v, sink_b)
        pool, pooled = _pool_fwd(u, pool_w[l], psc)
        if l == 0:
            arrived = _ici_wait(rest_send, rest_recv, rest_bufs, None, pool, _ag_send_view, _ag_recv_view,
                                name="ag_wait_0")
            weights[0] = as_operands(win0 + _ag_forward(arrived, name="ag_fwd_0"))
        win, wout, wg, wu, wd = weights[l]
        if l + 1 < n_layers:
            mix, x1, h2, act, ga, gb, f, x2 = _out_ffn_fwd(attn, pool, xl, wout, mod8, g8, wg, wu, wd)
        else:
            mix, x1, h2, act, ga, gb, f, x2, loss_tile = _out_ffn_fwd(attn, pool, xl, wout, mod8, g8, wg, wu, wd,
                                                                      target=target)
        saved.append(dict(x=xl, h=h, q=q, k=k, v=v, lse=lse, attn=attn, pool=pool, pooled=pooled, mix=mix,
                          x1=x1, h2=h2, act=act, ga=ga, gb=gb, f=f, mod8=mod8, g8=g8, sink_b=sink_b, psc=psc))
        xl = x2
        if l + 1 < n_layers:
            arrived = _ici_wait(ag_send, ag_recv, ag_bufs, None, x2, _ag_send_view, _ag_recv_view,
                                name=f"ag_wait_{l + 1}")
            weights[l + 1] = as_operands(_ag_forward(arrived, name=f"ag_fwd_{l + 1}"))

    dy = xl
    loss = lax.psum(loss_tile[0, 0], ("x", "y", "c"))

    small = [None] * n_layers
    dmod_rows = [None] * n_layers
    reduced = [dict() for _ in range(n_layers)]
    att_swap = None
    dx = dy
    for l in reversed(range(n_layers)):
        s = saved[l]
        win, wout, wg, wu, wd = weights[l]
        if att_swap is not None:
            s = dict(s, mod8=s["mod8"] + att_swap[1][0, 0])
        dx1, df, da, db, red_f = _ffn_bwd(dx, s["f"], s["ga"], s["gb"], s["x1"], s["mod8"], s["g8"], wg, wu, wd)
        token = None
        if att_swap is not None:
            att_scatter = _rs_scatter_begin(att_swap[0], dx1)
            token = att_scatter[1]
        ffn_shards = (N_SHARD, FF_SH, D_MODEL)
        g_wd = _wgrad(s["act"], df, name="wgrad_down", after=token).reshape(ffn_shards)
        g_wg = _wgrad(da, s["h2"], name="wgrad_gate").reshape(ffn_shards)
        g_wu = _wgrad(db, s["h2"], name="wgrad_up").reshape(ffn_shards)
        ffn_swap = _rs_swap_begin([g_wg, g_wu, g_wd], dx1, tag=f"{l}f")
        if att_swap is not None:
            got = _rs_end(att_scatter[0], ffn_swap[1])
            reduced[l + 1].update(w_in=got[0], w_out=got[1])
        s = dict(s, mod8=s["mod8"] + ffn_swap[1][0, 0])
        dmix, dattn, dpool, red_c = _mix_bwd(dx1, s["mix"], s["mod8"], s["g8"], wout)
        g_wout = jnp.concatenate([_wgrad(s["attn"], dmix, name="wgrad_out_a"),
                                  _wgrad(s["pool"], dmix, name="wgrad_out_p")], axis=0)
        ffn_scatter = _rs_scatter_begin(ffn_swap[0], dattn)
        dq, dk_e, dk_o, dv_e, dv_o, dsink = _attn_bwd(s["q"], s["k"], s["v"], s["lse"], dattn,
                                                      s["sink_b"] + ffn_scatter[1][0:1, :])
        du, g_poolw, dpsc = _pool_bwd(dpool, s["pooled"], pool_w[l], s["psc"])
        dx, dproj, red_d, dbin = _in_bwd(dq, (dk_e, dk_o), (dv_e, dv_o), du, rc, rs1, rs2, s["x"], dx1, s["mod8"],
                                         s["g8"], win)
        g_win = _wgrad(dproj, s["h"], name="wgrad_in")
        g_win_sh = g_win.reshape(N_SHARD, IN_SH, D_MODEL)
        got = _rs_end(ffn_scatter[0], dproj)
        reduced[l].update(w_gate=got[0], w_up=got[1], w_down=got[2])
        att_swap = _rs_swap_begin([g_win_sh, g_wout.reshape(N_SHARD, OUT_SH, D_MODEL)], dx, tag=f"{l}a")
        dmod_rows[l] = jnp.concatenate([red_d[0], red_d[1], red_c[0], red_f[2], red_f[3], red_f[0]])
        small[l] = jnp.concatenate([red_d[2], red_c[1], red_f[4], red_f[1], dbin[0], dpsc[0], dsink[:, 0],
                                    jnp.zeros((120,), F32), g_poolw.reshape(-1)])
    grad_x = dx.reshape(1, T, D_MODEL)

    per_layer = small[0].shape[0]
    rows_small = n_layers * per_layer // 128
    rows_mod = n_layers * 6 * D_MODEL // 128
    rows_pad = -(rows_small + rows_mod) % 8
    pack = jnp.concatenate(small + dmod_rows + [jnp.zeros((rows_pad * 128,), F32)]).reshape(-1, 128)
    pack = pack + att_swap[1][0, 0]
    gathered = _allgather8(pack, name="ag_small").reshape(N_DEV, pack.shape[0], 128)
    summed = _sum_devices(gathered)
    att_scatter = _rs_scatter_begin(att_swap[0], summed)
    small_sum = summed[:rows_small].reshape(n_layers, per_layer)
    o = 0
    small_g = {}
    for nm, width in (("g_pre_mix", D_MODEL), ("g_post_mix", D_MODEL), ("g_pre_ffn", D_MODEL),
                      ("g_post_ffn", D_MODEL), ("b_in", IN_W), ("pool_scale", POOL_W), ("sinks", 128),
                      ("pool_w", 4 * 128 * 128)):
        small_g[nm] = small_sum[:, o:o + width]
        o += width
    small_g["sinks"] = small_g["sinks"][:, :N_HEADS]
    small_g["pool_w"] = small_g["pool_w"].reshape(n_layers, 4, 128, 128)
    small_g["ada_b"] = summed[rows_small:rows_small + rows_mod].reshape(n_layers, 6 * D_MODEL)
    dmod_all = gathered[:, rows_small:rows_small + rows_mod].reshape(N_DEV, n_layers, N_SHARD, ADA_SH)
    dmod_sh = lax.dynamic_index_in_dim(dmod_all, my_chip, axis=2, keepdims=False)
    g_ada_w = _ada_wgrad(jnp.transpose(c_all), jnp.transpose(dmod_sh, (1, 0, 2)))

    grads = dict(ada_w=g_ada_w, ada_b=small_g["ada_b"], b_in=small_g["b_in"], sinks=small_g["sinks"],
                 pool_w=small_g["pool_w"], pool_scale=small_g["pool_scale"], g_pre_mix=small_g["g_pre_mix"],
                 g_post_mix=small_g["g_post_mix"], g_pre_ffn=small_g["g_pre_ffn"], g_post_ffn=small_g["g_post_ffn"])
    params = dict(ada_w=(ada_w, m_ada_w, v_ada_w), ada_b=(ada_b, m_ada_b, v_ada_b), w_in=(w_in, m_w_in, v_w_in),
                  b_in=(b_in, m_b_in, v_b_in), sinks=(sinks, m_sinks, v_sinks), pool_w=(pool_w, m_pool_w, v_pool_w),
                  pool_scale=(pool_scale, m_pool_scale, v_pool_scale), w_out=(w_out, m_w_out, v_w_out),
                  w_gate=(w_gate, m_w_gate, v_w_gate), w_up=(w_up, m_w_up, v_w_up),
                  w_down=(w_down, m_w_down, v_w_down), g_pre_mix=(g_pre_mix, m_g_pre_mix, v_g_pre_mix),
                  g_post_mix=(g_post_mix, m_g_post_mix, v_g_post_mix), g_pre_ffn=(g_pre_ffn, m_g_pre_ffn, v_g_pre_ffn),
                  g_post_ffn=(g_post_ffn, m_g_post_ffn, v_g_post_ffn))
    names = list(params)
    updates = {nm: _adamw_nd(*params[nm][:1], grads[nm], *params[nm][1:], name="adamw_" + nm) for nm in grads}

    got = _rs_end(att_scatter[0], updates["ada_w"][0])
    reduced[0].update(w_in=got[0], w_out=got[1])
    for nm in ("w_in", "w_out", "w_gate", "w_up", "w_down"):
        g = jnp.stack([reduced[l][nm] for l in range(n_layers)])
        if nm in ("w_in", "w_gate", "w_up"):
            upd = _adamw_nd(tr(params[nm][0]), g, tr(params[nm][1]), tr(params[nm][2]), name="adamw_" + nm)
            grads[nm], updates[nm] = tr(g), [tr(u) for u in upd]
        else:
            grads[nm], updates[nm] = g, _adamw_nd(params[nm][0], g, *params[nm][1:], name="adamw_" + nm)
    return (loss, grad_x, *[grads[nm] for nm in names], *[updates[nm][0] for nm in names],
            *[updates[nm][1] for nm in names], *[updates[nm][2] for nm in names])
```

```python
import functools

import jax
import jax.numpy as jnp
from jax import lax
from jax.experimental import pallas as pl
from jax.experimental.pallas import tpu as pltpu

F32 = jnp.float32
BF16 = jnp.bfloat16
MESH = pl.DeviceIdType.MESH

D_MODEL = 1024
ATTN_W = 512
KV_W = 128
KVD_W = 256
POOL_W = 512
IN_W = 1280
D_FF = 2816
N_SHARD = 4
FF_SH = D_FF // N_SHARD
IN_SH = IN_W // N_SHARD
OUT_SH = D_MODEL // N_SHARD
ADA_SH = 6 * D_MODEL // N_SHARD
HEAD = 64
N_HEADS = 8
GROUP = 4
BLK = 128
POOL_WINDOWS = (2, 4, 8, 16)
HALO = 16
ROT = 16
ROPE_THETA = 500000.0
EPS = 1e-6
NEG_INF = -1e30
N_DEV = 8

ADAM_LR = 0.001
ADAM_B1 = 0.9
ADAM_B2 = 0.999
ADAM_EPS = 1e-08
ADAM_WD = 0.01
ADAM_STEP = 10

VMEM_LIMIT = 48 * 1024 * 1024
FFN_VMEM_LIMIT = 60 * 1024 * 1024
ATTN_FWD_BLOCKS = 4
ATTN_BWD_BLOCKS = 4
WGRAD_TOKENS = 2048


def _cp(*sem, vmem=VMEM_LIMIT):
    return pltpu.CompilerParams(dimension_semantics=sem, vmem_limit_bytes=vmem)


def _full(shape):
    nd = len(shape)
    return pl.BlockSpec(shape, lambda *_: (0,) * nd)


def _resident(shape):
    nd = len(shape)
    return pl.BlockSpec(shape, lambda *_: (0,) * nd, pipeline_mode=pl.Buffered(1))


def _rows(tm, ncol):
    return pl.BlockSpec((tm, ncol), lambda i: (i, 0))


def _sds(shape, dtype):
    return jax.ShapeDtypeStruct(shape, dtype)


def _nt(a, b):
    return lax.dot_general(a, b, (((1,), (1,)), ((), ())), preferred_element_type=F32)


def _tn(a, b):
    return lax.dot_general(a, b, (((0,), (0,)), ((), ())), preferred_element_type=F32)


def _mm(a, b):
    return jnp.dot(a, b, preferred_element_type=F32)


def _rstd(x):
    return lax.rsqrt(jnp.mean(x * x, axis=-1, keepdims=True) + EPS)


def _colsum(x):
    return jnp.sum(x, axis=0, keepdims=True)


def _norm_gain_bwd(dy, xhat, rstd, gain):
    p = dy * xhat
    dx = rstd * (dy * gain - xhat * jnp.mean(p * gain, axis=-1, keepdims=True))
    return dx, _colsum(p)


def _rope_tables(pos_b, lane_tab):
    T = pos_b.shape[0]
    tm = min(T, 1024)

    def body(pos_ref, tab_ref, c_ref, s1_ref, s2_ref):
        ang = pos_ref[...].astype(F32) * tab_ref[0:1, :]
        cs = jnp.cos(ang)
        sn = jnp.sin(ang)
        m_rot = tab_ref[1:2, :]
        c_ref[...] = cs * m_rot + (1.0 - m_rot)
        s1_ref[...] = -sn * tab_ref[2:3, :]
        s2_ref[...] = sn * tab_ref[3:4, :]

    out = _sds((T, 128), F32)
    return pl.pallas_call(
        body, name="rope_tables", grid=(T // tm,),
        in_specs=[_rows(tm, 128), _full((8, 128))],
        out_specs=[_rows(tm, 128)] * 3, out_shape=[out] * 3,
        compiler_params=_cp("parallel"),
    )(pos_b, lane_tab)


def _rot_fwd(t, c, s1, s2):
    w = t.shape[-1]
    return t * c + pltpu.roll(t, w - 8, 1) * s1 + pltpu.roll(t, 8, 1) * s2


def _rot_bwd(d, c, s1, s2):
    w = d.shape[-1]
    return d * c + pltpu.roll(d * s1, 8, 1) + pltpu.roll(d * s2, w - 8, 1)


def _store_dup(ref, t):
    low = lax.broadcasted_iota(jnp.int32, t.shape, 1) < HEAD
    sw = pltpu.roll(t, HEAD, 1)
    ref[:, 0:128] = jnp.where(low, t, sw).astype(BF16)
    ref[:, 128:256] = jnp.where(low, sw, t).astype(BF16)


def _fold_dup(d):
    low = lax.broadcasted_iota(jnp.int32, (d.shape[0], 128), 1) < HEAD
    d0 = d[:, 0:128]
    d1 = d[:, 128:256]
    return jnp.where(low, d0 + pltpu.roll(d0, HEAD, 1), d1 + pltpu.roll(d1, HEAD, 1))


def _fwd_in(x, mod8, g8, w_in, b_in, rc, rs1, rs2):
    T = x.shape[0]
    tm = min(T, 1024)

    def body(x_ref, mod_ref, g_ref, w_ref, b_ref, c_ref, s1_ref, s2_ref,
             h_ref, q_ref, k_ref, v_ref, u_ref):
        xf = x_ref[...]
        h = (xf * _rstd(xf) * g_ref[0:1, :]) * (1.0 + mod_ref[1:2, :]) + mod_ref[0:1, :]
        hb = h.astype(BF16)
        h_ref[...] = hb
        c = c_ref[...]
        s1 = s1_ref[...]
        s2 = s2_ref[...]
        proj = _nt(hb, w_ref[...]) + b_ref[...]
        q = _rot_fwd(proj[:, 0:ATTN_W], jnp.tile(c, (1, 4)), jnp.tile(s1, (1, 4)), jnp.tile(s2, (1, 4)))
        q_ref[...] = (q * (HEAD ** -0.5)).astype(BF16)
        _store_dup(k_ref, _rot_fwd(proj[:, ATTN_W:ATTN_W + KV_W], c, s1, s2))
        _store_dup(v_ref, proj[:, ATTN_W + KV_W:ATTN_W + 2 * KV_W])
        u_ref[...] = proj[:, ATTN_W + 2 * KV_W:IN_W]

    return pl.pallas_call(
        body, name="fwd_in", grid=(T // tm,),
        in_specs=[_rows(tm, D_MODEL), _full((8, D_MODEL)), _full((8, D_MODEL)),
                  _resident((IN_W, D_MODEL)), _full((1, IN_W)),
                  _rows(tm, 128), _rows(tm, 128), _rows(tm, 128)],
        out_specs=[_rows(tm, D_MODEL), _rows(tm, ATTN_W), _rows(tm, KVD_W), _rows(tm, KVD_W), _rows(tm, POOL_W)],
        out_shape=[_sds((T, D_MODEL), BF16), _sds((T, ATTN_W), BF16), _sds((T, KVD_W), BF16),
                   _sds((T, KVD_W), BF16), _sds((T, POOL_W), F32)],
        compiler_params=_cp("parallel"),
    )(x, mod8, g8, w_in, b_in, rc, rs1, rs2)


def _band_mask(n):
    kk = lax.broadcasted_iota(jnp.int32, (2 * BLK, BLK), 0)
    qi = lax.broadcasted_iota(jnp.int32, (2 * BLK, BLK), 1)
    first = jnp.where(n > 0, 0, 2 * BLK)
    in_prev = jnp.logical_and(kk < BLK, kk > qi + first)
    in_cur = jnp.logical_and(kk >= BLK, (kk - BLK) <= qi)
    one = jnp.logical_or(in_prev, in_cur)
    return jnp.concatenate([one] * GROUP, axis=1)


def _head_row(ref, j, base=0):
    return jnp.concatenate([ref[base + GROUP * j + r:base + GROUP * j + r + 1, :] for r in range(GROUP)], axis=1)


def _stack_heads(x_ref, j, rows=slice(None)):
    low = lax.broadcasted_iota(jnp.int32, (BLK, 128), 1) < HEAD
    parts = []
    for gp in (2 * j, 2 * j + 1):
        x2 = x_ref[rows, gp * 128:(gp + 1) * 128]
        parts.append(jnp.where(low, x2, jnp.zeros_like(x2)))
        parts.append(jnp.where(low, jnp.zeros_like(x2), x2))
    return jnp.concatenate(parts, axis=0)


def _unstack_heads(o):
    low = lax.broadcasted_iota(jnp.int32, (BLK, 128), 1) < HEAD
    return [jnp.where(low, o[0:BLK], o[BLK:2 * BLK]), jnp.where(low, o[2 * BLK:3 * BLK], o[3 * BLK:4 * BLK])]


def _attn_fwd(q, kd, vd, sink_b):
    T = q.shape[0]
    nb = T // BLK
    nq = ATTN_FWD_BLOCKS if nb % ATTN_FWD_BLOCKS == 0 else 2
    assert nb % nq == 0

    def body(q_ref, kp_ref, kc_ref, vp_ref, vc_ref, sk_ref, o_ref, lse_ref):
        for sub in range(nq):
            rows = slice(sub * BLK, (sub + 1) * BLK)
            before = slice((sub - 1) * BLK, sub * BLK)
            valid = _band_mask(nq * pl.program_id(0) + sub)
            for j in range(N_HEADS // GROUP):
                lanes = slice(j * 128, (j + 1) * 128)
                k_prev = kp_ref[:, lanes] if sub == 0 else kc_ref[before, lanes]
                v_prev = vp_ref[:, lanes] if sub == 0 else vc_ref[before, lanes]
                kcat = jnp.concatenate([k_prev, kc_ref[rows, lanes]], axis=0)
                vcat = jnp.concatenate([v_prev, vc_ref[rows, lanes]], axis=0)
                s = jnp.where(valid, _nt(kcat, _stack_heads(q_ref, j, rows)), NEG_INF)
                sk = _head_row(sk_ref, j)
                m = jnp.maximum(jnp.max(s, axis=0, keepdims=True), sk)
                p = jnp.exp(s - m)
                den = jnp.sum(p, axis=0, keepdims=True) + jnp.exp(sk - m)
                p = p * (1.0 / den)
                o = _tn(p.astype(BF16), vcat)
                o_ref[rows, 2 * j * 128:(2 * j + 2) * 128] = jnp.concatenate(_unstack_heads(o), axis=1).astype(BF16)
                lse = m + jnp.log(den)
                for r in range(GROUP):
                    h = sub * N_HEADS + GROUP * j + r
                    lse_ref[h:h + 1, :] = lse[:, r * 128:(r + 1) * 128]

    prev = lambda i: (jnp.maximum(nq * i - 1, 0), 0)
    cur = lambda i: (i, 0)
    return pl.pallas_call(
        body, name="attn_fwd", grid=(nb // nq,),
        in_specs=[pl.BlockSpec((nq * BLK, ATTN_W), cur),
                  pl.BlockSpec((BLK, KVD_W), prev), pl.BlockSpec((nq * BLK, KVD_W), cur),
                  pl.BlockSpec((BLK, KVD_W), prev), pl.BlockSpec((nq * BLK, KVD_W), cur),
                  _full((8, 128))],
        out_specs=[pl.BlockSpec((nq * BLK, ATTN_W), cur), pl.BlockSpec((nq * N_HEADS, 128), cur)],
        out_shape=[_sds((T, ATTN_W), BF16), _sds((nb * N_HEADS, 128), F32)],
        compiler_params=_cp("parallel"),
    )(q, kd, kd, vd, vd, sink_b)


def _pool_fwd(u, pool_w, pool_scale):
    T = u.shape[0]
    tm = min(T, 1024)

    def body(u_ref, w_ref, sc_ref, out_ref, pooled_ref, halo):
        i = pl.program_id(0)

        @pl.when(i == 0)
        def _():
            halo[...] = jnp.zeros_like(halo)

        ub = u_ref[...]
        ext = jnp.concatenate([halo[...], ub], axis=0)
        halo[...] = ub[tm - HALO:, :]
        tpos = (i * tm + lax.broadcasted_iota(jnp.int32, (tm, 1), 0)).astype(F32)
        for g, w in enumerate(POOL_WINDOWS):
            lanes = slice(g * 128, (g + 1) * 128)
            s = ext[:, lanes]
            sh = 1
            while sh < w:
                s = s + pltpu.roll(s, sh, 0)
                sh *= 2
            cnt = jnp.minimum(tpos + 1.0, float(w))
            pb = (s[HALO:, :] / cnt - ub[:, lanes]).astype(BF16)
            z = _mm(pb, w_ref[g].astype(BF16))
            out_ref[:, lanes] = (z * sc_ref[:, lanes]).astype(BF16)
            pooled_ref[:, lanes] = pb

    return pl.pallas_call(
        body, name="pool_fwd", grid=(T // tm,),
        in_specs=[_rows(tm, POOL_W), _full((4, 128, 128)), _full((1, POOL_W))],
        out_specs=[_rows(tm, POOL_W), _rows(tm, POOL_W)],
        out_shape=[_sds((T, POOL_W), BF16), _sds((T, POOL_W), BF16)],
        scratch_shapes=[pltpu.VMEM((HALO, POOL_W), F32)],
        compiler_params=_cp("arbitrary"),
    )(u, pool_w, pool_scale)


FF_CHUNKS = ((0, 1024), (1024, 2048), (2048, D_FF))


def _out_ffn_fwd(attn, pool, x, w_out, mod8, g8, wg, wu, wd, target=None):
    T = x.shape[0]
    tm = min(T, 256)
    last = target is not None

    def body(*refs):
        a_ref, p_ref, xin_ref, wo_ref, mod_ref, g_ref, wg_ref, wu_ref, wd_ref = refs[:9]
        t_ref = refs[9] if last else None
        mix_ref, x1_ref, h_ref, act_ref, ga_ref, gb_ref, f_ref, x2_ref = refs[9 + last:17 + last]
        mix = _mm(a_ref[...], wo_ref[0:ATTN_W, :]) + _mm(p_ref[...], wo_ref[ATTN_W:, :])
        mix_ref[...] = mix
        xf = xin_ref[...] + mod_ref[2:3, :] * (mix * _rstd(mix) * g_ref[1:2, :])
        x1_ref[...] = xf
        h = (xf * _rstd(xf) * g_ref[2:3, :]) * (1.0 + mod_ref[4:5, :]) + mod_ref[3:4, :]
        hb = h.astype(BF16)
        h_ref[...] = hb
        f = jnp.zeros((tm, D_MODEL), F32)
        for lo, hi in FF_CHUNKS:
            a = _nt(hb, wg_ref[lo:hi, :])
            b = _nt(hb, wu_ref[lo:hi, :])
            sig = jax.nn.sigmoid(a)
            sl = a * sig
            act = (sl * b).astype(BF16)
            act_ref[:, lo:hi] = act
            ga_ref[:, lo:hi] = (b * (sig * (1.0 + a * (1.0 - sig)))).astype(BF16)
            gb_ref[:, lo:hi] = sl.astype(BF16)
            f = f + _mm(act, wd_ref[lo:hi, :])
        f_ref[...] = f
        x2 = xf + mod_ref[5:6, :] * (f * _rstd(f) * g_ref[3:4, :])
        if not last:
            x2_ref[...] = x2
        else:
            loss_ref = refs[18]

            @pl.when(pl.program_id(0) == 0)
            def _():
                loss_ref[...] = jnp.zeros_like(loss_ref)

            e = x2 - t_ref[...]
            x2_ref[...] = e * (1.0 / D_MODEL)
            loss_ref[...] += 0.5 * jnp.sum(jnp.mean(e * e, axis=-1, keepdims=True), axis=0, keepdims=True)

    act_shape = _sds((T, D_FF), BF16)
    wide = _sds((T, D_MODEL), F32)
    weights = [_resident((D_FF, D_MODEL))] * 3
    return pl.pallas_call(
        body, name="out_ffn_fwd_loss" if last else "out_ffn_fwd", grid=(T // tm,),
        in_specs=[_rows(tm, ATTN_W), _rows(tm, POOL_W), _rows(tm, D_MODEL), _resident((D_MODEL, D_MODEL)),
                  _full((8, D_MODEL)), _full((8, D_MODEL)), *weights]
        + ([_rows(tm, D_MODEL)] if last else []),
        out_specs=[_rows(tm, D_MODEL), _rows(tm, D_MODEL), _rows(tm, D_MODEL), _rows(tm, D_FF), _rows(tm, D_FF),
                   _rows(tm, D_FF), _rows(tm, D_MODEL), _rows(tm, D_MODEL)] + ([_full((8, 128))] if last else []),
        out_shape=[wide, wide, _sds((T, D_MODEL), BF16), act_shape, act_shape, act_shape, wide, wide]
        + ([_sds((8, 128), F32)] if last else []),
        compiler_params=_cp("arbitrary" if last else "parallel", vmem=FFN_VMEM_LIMIT),
    )(attn, pool, x, w_out, mod8, g8, wg, wu, wd, *([target] if last else []))


def _ffn_bwd(dx2, f, ga, gb, x1, mod8, g8, wg, wu, wd):
    T = dx2.shape[0]
    tm = min(T, 256)

    def body(dx_ref, f_ref, ga_ref, gb_ref, x_ref, mod_ref, g_ref, wg_ref, wu_ref, wd_ref,
             dx1_ref, df_ref, da_ref, db_ref, red_ref):
        @pl.when(pl.program_id(0) == 0)
        def _():
            red_ref[...] = jnp.zeros_like(red_ref)

        dx = dx_ref[...]
        fv = f_ref[...]
        rstd = _rstd(fv)
        fhat = fv * rstd
        gpost = g_ref[3:4, :]
        gate = mod_ref[5:6, :]
        df, s_post = _norm_gain_bwd(dx, fhat, rstd, gate * gpost)
        red_ref[0:1, :] += gpost * s_post
        red_ref[1:2, :] += gate * s_post
        dfb = df.astype(BF16)
        df_ref[...] = dfb
        dh = jnp.zeros((tm, D_MODEL), F32)
        for lo, hi in FF_CHUNKS:
            dact = _nt(dfb, wd_ref[lo:hi, :])
            da = (dact * ga_ref[:, lo:hi].astype(F32)).astype(BF16)
            db = (dact * gb_ref[:, lo:hi].astype(F32)).astype(BF16)
            da_ref[:, lo:hi] = da
            db_ref[:, lo:hi] = db
            dh = dh + _mm(da, wg_ref[lo:hi, :]) + _mm(db, wu_ref[lo:hi, :])
        xf = x_ref[...]
        rstd1 = _rstd(xf)
        xhat = xf * rstd1
        gpre = g_ref[2:3, :]
        scale1 = 1.0 + mod_ref[4:5, :]
        dxn, s_pre = _norm_gain_bwd(dh, xhat, rstd1, scale1 * gpre)
        red_ref[2:3, :] += _colsum(dh)
        red_ref[3:4, :] += gpre * s_pre
        red_ref[4:5, :] += scale1 * s_pre
        dx1_ref[...] = dx + dxn

    act_shape = _sds((T, D_FF), BF16)
    return pl.pallas_call(
        body, name="ffn_bwd", grid=(T // tm,),
        in_specs=[_rows(tm, D_MODEL), _rows(tm, D_MODEL), _rows(tm, D_FF), _rows(tm, D_FF), _rows(tm, D_MODEL),
                  _full((8, D_MODEL)), _full((8, D_MODEL)),
                  _resident((D_FF, D_MODEL)), _resident((D_FF, D_MODEL)), _resident((D_FF, D_MODEL))],
        out_specs=[_rows(tm, D_MODEL), _rows(tm, D_MODEL), _rows(tm, D_FF), _rows(tm, D_FF), _full((8, D_MODEL))],
        out_shape=[_sds((T, D_MODEL), F32), _sds((T, D_MODEL), BF16), act_shape, act_shape, _sds((8, D_MODEL), F32)],
        compiler_params=_cp("arbitrary"),
    )(dx2, f, ga, gb, x1, mod8, g8, wg, wu, wd)


def _wgrad(a, b, name, after=None):
    T, K = a.shape
    N = b.shape[1]
    tt = min(T, WGRAD_TOKENS)
    tk = next(c for c in (1408, 640, 512, 256, 128) if K % c == 0)

    def body(a_ref, b_ref, *rest):
        o_ref = rest[-1]

        @pl.when(pl.program_id(1) == 0)
        def _():
            o_ref[...] = jnp.zeros_like(o_ref)

        o_ref[...] += _tn(a_ref[...], b_ref[...])

    extra = [] if after is None else [after]
    return pl.pallas_call(
        body, name=name, grid=(K // tk, T // tt),
        in_specs=[pl.BlockSpec((tt, tk), lambda i, t: (t, i)), pl.BlockSpec((tt, N), lambda i, t: (t, 0))]
        + [pl.BlockSpec(memory_space=pl.ANY)] * len(extra),
        out_specs=pl.BlockSpec((tk, N), lambda i, t: (i, 0)),
        out_shape=_sds((K, N), F32),
        compiler_params=_cp("parallel", "arbitrary"),
    )(a, b, *extra)


def _mix_bwd(dx1, mix, mod8, g8, w_out):
    T = dx1.shape[0]
    tm = min(T, 1024)

    def body(dx_ref, mix_ref, mod_ref, g_ref, w_ref, dmix_ref, da_ref, dp_ref, red_ref):
        @pl.when(pl.program_id(0) == 0)
        def _():
            red_ref[...] = jnp.zeros_like(red_ref)

        dx = dx_ref[...]
        mv = mix_ref[...]
        rstd = _rstd(mv)
        mhat = mv * rstd
        gpost = g_ref[1:2, :]
        gate = mod_ref[2:3, :]
        dm, s_post = _norm_gain_bwd(dx, mhat, rstd, gate * gpost)
        red_ref[0:1, :] += gpost * s_post
        red_ref[1:2, :] += gate * s_post
        dmb = dm.astype(BF16)
        dmix_ref[...] = dmb
        dap = _nt(dmb, w_ref[...])
        da_ref[...] = dap[:, 0:ATTN_W].astype(BF16)
        dp_ref[...] = dap[:, ATTN_W:].astype(BF16)

    return pl.pallas_call(
        body, name="mix_bwd", grid=(T // tm,),
        in_specs=[_rows(tm, D_MODEL), _rows(tm, D_MODEL), _full((8, D_MODEL)), _full((8, D_MODEL)),
                  _resident((D_MODEL, D_MODEL))],
        out_specs=[_rows(tm, D_MODEL), _rows(tm, ATTN_W), _rows(tm, POOL_W), _full((8, D_MODEL))],
        out_shape=[_sds((T, D_MODEL), BF16), _sds((T, ATTN_W), BF16), _sds((T, POOL_W), BF16),
                   _sds((8, D_MODEL), F32)],
        compiler_params=_cp("arbitrary"),
    )(dx1, mix, mod8, g8, w_out)


def _attn_bwd(q, kd, vd, lse, dattn, sink_b):
    T = q.shape[0]
    nb = T // BLK
    nq = ATTN_BWD_BLOCKS if nb % ATTN_BWD_BLOCKS == 0 else 2
    assert nb % nq == 0
    nstep = nb // nq

    def body(q_ref, do_ref, lse_ref, kp_ref, kc_ref, vp_ref, vc_ref, sk_ref,
             dq_ref, dkm_ref, dkt_ref, dvm_ref, dvt_ref, dsk_ref, carry_k, carry_v):
        i = pl.program_id(0)

        @pl.when(i == 0)
        def _():
            carry_k[...] = jnp.zeros_like(carry_k)
            carry_v[...] = jnp.zeros_like(carry_v)
            dsk_ref[...] = jnp.zeros_like(dsk_ref)

        @pl.when(i < nstep)
        def _():
            for j in range(N_HEADS // GROUP):
                lanes = slice(j * 128, (j + 1) * 128)
                parts_k, parts_v = [], []
                for sub in range(nq):
                    rows = slice(sub * BLK, (sub + 1) * BLK)
                    before = slice((sub - 1) * BLK, sub * BLK)
                    valid = _band_mask(nq * i + sub)
                    k_prev = kp_ref[:, lanes] if sub == 0 else kc_ref[before, lanes]
                    v_prev = vp_ref[:, lanes] if sub == 0 else vc_ref[before, lanes]
                    kcat = jnp.concatenate([k_prev, kc_ref[rows, lanes]], axis=0)
                    vcat = jnp.concatenate([v_prev, vc_ref[rows, lanes]], axis=0)
                    qs = _stack_heads(q_ref, j, rows)
                    dos = _stack_heads(do_ref, j, rows)
                    lse = _head_row(lse_ref, j, sub * N_HEADS)
                    p = jnp.exp(jnp.where(valid, _nt(kcat, qs), NEG_INF) - lse)
                    dp = _nt(vcat, dos)
                    delta = jnp.sum(p * dp, axis=0, keepdims=True)
                    ds = (p * (dp - delta)).astype(BF16)
                    sink_term = jnp.exp(_head_row(sk_ref, j) - lse) * delta
                    for r in range(GROUP):
                        h = GROUP * j + r
                        dsk_ref[h:h + 1, :] += -jnp.sum(sink_term[:, r * 128:(r + 1) * 128], axis=1, keepdims=True)
                    dq_ref[rows, 2 * j * 128:(2 * j + 2) * 128] = jnp.concatenate(
                        _unstack_heads(_tn(ds, kcat)), axis=1)
                    parts_k.append(_mm(ds, qs))
                    parts_v.append(_mm(p.astype(BF16), dos))
                dkt_ref[:, lanes] = carry_k[:, lanes] + parts_k[0][0:BLK]
                dvt_ref[:, lanes] = carry_v[:, lanes] + parts_v[0][0:BLK]
                for s in range(nq - 1):
                    dkm_ref[s * BLK:(s + 1) * BLK, lanes] = parts_k[s][BLK:] + parts_k[s + 1][0:BLK]
                    dvm_ref[s * BLK:(s + 1) * BLK, lanes] = parts_v[s][BLK:] + parts_v[s + 1][0:BLK]
                carry_k[:, lanes] = parts_k[nq - 1][BLK:]
                carry_v[:, lanes] = parts_v[nq - 1][BLK:]

        @pl.when(i == nstep)
        def _():
            dkt_ref[...] = carry_k[...]
            dvt_ref[...] = carry_v[...]

    cur = lambda i: (jnp.minimum(i, nstep - 1), 0)
    prev = lambda i: (jnp.minimum(jnp.maximum(nq * i - 1, 0), nb - 1), 0)
    tail = lambda i: (jnp.maximum(i - 1, 0), 0)
    main_shape = _sds((nstep * (nq - 1) * BLK, KVD_W), F32)
    tail_shape = _sds((nstep * BLK, KVD_W), F32)
    main_spec = pl.BlockSpec(((nq - 1) * BLK, KVD_W), cur)
    tail_spec = pl.BlockSpec((BLK, KVD_W), tail)
    return pl.pallas_call(
        body, name="attn_bwd", grid=(nstep + 1,),
        in_specs=[pl.BlockSpec((nq * BLK, ATTN_W), cur), pl.BlockSpec((nq * BLK, ATTN_W), cur),
                  pl.BlockSpec((nq * N_HEADS, 128), cur),
                  pl.BlockSpec((BLK, KVD_W), prev), pl.BlockSpec((nq * BLK, KVD_W), cur),
                  pl.BlockSpec((BLK, KVD_W), prev), pl.BlockSpec((nq * BLK, KVD_W), cur),
                  _full((8, 128))],
        out_specs=[pl.BlockSpec((nq * BLK, ATTN_W), cur), main_spec, tail_spec, main_spec, tail_spec, _full((8, 128))],
        out_shape=[_sds((T, ATTN_W), F32), main_shape, tail_shape, main_shape, tail_shape, _sds((8, 128), F32)],
        scratch_shapes=[pltpu.VMEM((BLK, KVD_W), F32), pltpu.VMEM((BLK, KVD_W), F32)],
        compiler_params=_cp("arbitrary"),
    )(q, dattn, lse, kd, kd, vd, vd, sink_b)


def _pool_bwd(dpool, pooled, pool_w, pool_scale):
    T = dpool.shape[0]
    tm = min(T, 1024)
    nbk = T // tm
    ext_rows = tm + HALO

    def body(dp_ref, pl_ref, w_ref, sc_ref, du_ref, dw_ref, dsc_ref, halo):
        i = pl.program_id(0)

        @pl.when(i == 0)
        def _():
            halo[...] = jnp.zeros_like(halo)
            dw_ref[...] = jnp.zeros_like(dw_ref)
            dsc_ref[...] = jnp.zeros_like(dsc_ref)

        blk = nbk - 1 - i
        tpos = (blk * tm + lax.broadcasted_iota(jnp.int32, (tm, 1), 0)).astype(F32)
        for g, w in enumerate(POOL_WINDOWS):
            lanes = slice(g * 128, (g + 1) * 128)
            dp = dp_ref[:, lanes].astype(F32)
            pb = pl_ref[:, lanes]
            wg = w_ref[g].astype(BF16)
            z = _mm(pb, wg)
            dsc_ref[0:1, lanes] += _colsum(dp * z)
            dz = (dp * sc_ref[:, lanes]).astype(BF16)
            dw_ref[g] += _tn(pb, dz)
            dpl = _nt(dz, wg)
            e = dpl / jnp.minimum(tpos + 1.0, float(w))
            s = jnp.concatenate([e, halo[:, lanes]], axis=0)
            halo[:, lanes] = e[0:HALO, :]
            sh = 1
            while sh < w:
                s = s + pltpu.roll(s, ext_rows - sh, 0)
                sh *= 2
            du_ref[:, lanes] = s[0:tm, :] - dpl

    rev = lambda i: (nbk - 1 - i, 0)
    return pl.pallas_call(
        body, name="pool_bwd", grid=(nbk,),
        in_specs=[pl.BlockSpec((tm, POOL_W), rev), pl.BlockSpec((tm, POOL_W), rev),
                  _full((4, 128, 128)), _full((1, POOL_W))],
        out_specs=[pl.BlockSpec((tm, POOL_W), rev), _full((4, 128, 128)), _full((8, POOL_W))],
        out_shape=[_sds((T, POOL_W), F32), _sds((4, 128, 128), F32), _sds((8, POOL_W), F32)],
        scratch_shapes=[pltpu.VMEM((HALO, POOL_W), F32)],
        compiler_params=_cp("arbitrary"),
    )(dpool, pooled, pool_w, pool_scale)


def _interleave_groups(main, tail):
    groups = tail.shape[0] // BLK
    m = main.shape[0] // groups
    parts = []
    for b in range(groups):
        parts += [main[b * m:(b + 1) * m], tail[b * BLK:(b + 1) * BLK]]
    return jnp.concatenate(parts, axis=0)


def _in_bwd(dq, dk_mt, dv_mt, du, rc, rs1, rs2, x, dx1, mod8, g8, w_in):
    T = x.shape[0]
    tm = min(T, 1024)
    nq = T // dk_mt[1].shape[0]
    t_tail = tm // nq
    t_main = tm - t_tail

    def body(dq_ref, dkm_ref, dkt_ref, dvm_ref, dvt_ref, du_ref, c_ref, s1_ref, s2_ref, x_ref, dx1_ref, mod_ref,
             g_ref, w_ref, dx_ref, dproj_ref, red_ref, dbin_ref):
        dk_all = _interleave_groups(dkm_ref[...], dkt_ref[...])
        dv_all = _interleave_groups(dvm_ref[...], dvt_ref[...])

        @pl.when(pl.program_id(0) == 0)
        def _():
            red_ref[...] = jnp.zeros_like(red_ref)
            dbin_ref[...] = jnp.zeros_like(dbin_ref)

        c = c_ref[...]
        s1 = s1_ref[...]
        s2 = s2_ref[...]
        dqp = _rot_bwd(dq_ref[...] * (HEAD ** -0.5), jnp.tile(c, (1, 4)), jnp.tile(s1, (1, 4)), jnp.tile(s2, (1, 4)))
        dkp = _rot_bwd(_fold_dup(dk_all), c, s1, s2)
        pieces = ((0, ATTN_W, dqp), (ATTN_W, ATTN_W + KV_W, dkp),
                  (ATTN_W + KV_W, ATTN_W + 2 * KV_W, _fold_dup(dv_all)), (ATTN_W + 2 * KV_W, IN_W, du_ref[...]))
        for lo, hi, val in pieces:
            dbin_ref[0:1, lo:hi] += _colsum(val)
            dproj_ref[:, lo:hi] = val.astype(BF16)
        dh = _mm(dproj_ref[...], w_ref[...])
        xf = x_ref[...]
        rstd = _rstd(xf)
        xhat = xf * rstd
        gpre = g_ref[0:1, :]
        scale1 = 1.0 + mod_ref[1:2, :]
        dxn, s_pre = _norm_gain_bwd(dh, xhat, rstd, scale1 * gpre)
        red_ref[0:1, :] += _colsum(dh)
        red_ref[1:2, :] += gpre * s_pre
        red_ref[2:3, :] += scale1 * s_pre
        dx_ref[...] = dx1_ref[...] + dxn

    return pl.pallas_call(
        body, name="in_bwd", grid=(T // tm,),
        in_specs=[_rows(tm, ATTN_W), *[_rows(t_main, KVD_W), _rows(t_tail, KVD_W)] * 2, _rows(tm, POOL_W),
                  _rows(tm, 128), _rows(tm, 128), _rows(tm, 128), _rows(tm, D_MODEL), _rows(tm, D_MODEL),
                  _full((8, D_MODEL)), _full((8, D_MODEL)), _resident((IN_W, D_MODEL))],
        out_specs=[_rows(tm, D_MODEL), _rows(tm, IN_W), _full((8, D_MODEL)), _full((8, IN_W))],
        out_shape=[_sds((T, D_MODEL), F32), _sds((T, IN_W), BF16), _sds((8, D_MODEL), F32), _sds((8, IN_W), F32)],
        compiler_params=_cp("arbitrary", vmem=FFN_VMEM_LIMIT),
    )(dq, *dk_mt, *dv_mt, du, rc, rs1, rs2, x, dx1, mod8, g8, w_in)


def _mod_fwd(c_all, ada_w, ada_b_sh):
    tn = 512

    def body(c_ref, w_ref, b_ref, o_ref):
        cv = c_ref[...]
        ca = (cv * jax.nn.sigmoid(cv)).astype(BF16)
        o_ref[...] = _mm(ca, w_ref[...].astype(BF16)) + b_ref[...]

    return pl.pallas_call(
        body, name="mod_fwd", grid=(2, ADA_SH // tn),
        in_specs=[_full((8, D_MODEL)), pl.BlockSpec((None, D_MODEL, tn), lambda l, j: (l, 0, j)),
                  pl.BlockSpec((None, 1, tn), lambda l, j: (l, 0, j))],
        out_specs=pl.BlockSpec((None, 8, tn), lambda l, j: (l, 0, j)),
        out_shape=_sds((2, 8, ADA_SH), F32),
        compiler_params=_cp("parallel", "parallel"),
    )(c_all, ada_w, ada_b_sh)


def _ada_wgrad(c_all_t, dmod_sh):
    tn = 512

    def body(c_ref, d_ref, o_ref):
        cv = c_ref[...]
        ca = cv * jax.nn.sigmoid(cv)
        o_ref[...] = jnp.dot(ca, d_ref[...], preferred_element_type=F32, precision=lax.Precision.HIGHEST)

    return pl.pallas_call(
        body, name="ada_wgrad", grid=(2, ADA_SH // tn),
        in_specs=[_full((D_MODEL, 8)), pl.BlockSpec((None, 8, tn), lambda l, j: (l, 0, j))],
        out_specs=pl.BlockSpec((None, D_MODEL, tn), lambda l, j: (l, 0, j)),
        out_shape=_sds((2, D_MODEL, ADA_SH), F32),
        compiler_params=_cp("parallel", "parallel"),
    )(c_all_t, dmod_sh)


def _sum_devices(g):
    R = g.shape[1]

    def body(g_ref, o_ref):
        acc = g_ref[0]
        for d in range(1, N_DEV):
            acc = acc + g_ref[d]
        o_ref[...] = acc

    return pl.pallas_call(
        body, name="sum_devices", grid=(1,),
        in_specs=[_full((N_DEV, R, 128))], out_specs=_full((R, 128)), out_shape=_sds((R, 128), F32),
        compiler_params=_cp("arbitrary"),
    )(g)


def _adamw(w, g, m, v, name):
    R, C = w.shape
    tr = R
    for cand in (256, 128, 64, 32, 16, 8):
        if R % cand == 0 and cand * C * 4 <= 2 * 1024 * 1024:
            tr = cand
            break

    def body(w_ref, g_ref, m_ref, v_ref, d_ref, nm_ref, nv_ref):
        gv = g_ref[...]
        mn = ADAM_B1 * m_ref[...] + (1.0 - ADAM_B1) * gv
        vn = ADAM_B2 * v_ref[...] + (1.0 - ADAM_B2) * (gv * gv)
        m_hat = mn / (1.0 - ADAM_B1 ** ADAM_STEP)
        v_hat = vn / (1.0 - ADAM_B2 ** ADAM_STEP)
        d_ref[...] = -ADAM_LR * (m_hat / (jnp.sqrt(v_hat) + ADAM_EPS) + ADAM_WD * w_ref[...])
        nm_ref[...] = mn
        nv_ref[...] = vn

    spec = pl.BlockSpec((tr, C), lambda i: (i, 0))
    out = _sds((R, C), F32)
    return pl.pallas_call(
        body, name=name, grid=(R // tr,),
        in_specs=[spec] * 4, out_specs=[spec] * 3, out_shape=[out] * 3,
        compiler_params=_cp("parallel"),
    )(w, g, m, v)


def _adamw_nd(w, g, m, v, name):
    shape = w.shape
    if w.ndim == 2 and shape[1] < 128:
        view = (1, shape[0] * shape[1])
    else:
        view = (-1, shape[-1])
    outs = _adamw(*[t.reshape(view) for t in (w, g, m, v)], name=name)
    return [o.reshape(shape) for o in outs]


def _coords():
    return lax.axis_index("x"), lax.axis_index("y"), lax.axis_index("c")


def _other_chips(x, y):
    return [(1 - x, y), (x, 1 - y), (1 - x, 1 - y)]


def _allgather8(blk, name):
    m_per, n = blk.shape

    def body(x_ref, out_ref, send_sems, recv_sems, local_sem):
        x, y, c = _coords()
        me, sibling = (x, y, c), (x, y, 1 - c)
        chips = _other_chips(x, y)

        def rows(px, py, pc):
            return out_ref.at[pl.ds((4 * px + 2 * py + pc) * m_per, m_per), :]

        def copy(k, block, to, src=None):
            return pltpu.make_async_remote_copy(
                src_ref=rows(*block) if src is None else src, dst_ref=rows(*block),
                send_sem=send_sems.at[k], recv_sem=recv_sems.at[k], device_id=to, device_id_type=MESH)

        mine = pltpu.make_async_copy(x_ref, rows(*me), local_sem)
        mine.start()
        first = [copy(0, me, sibling, src=x_ref)]
        first += [copy(1 + j, me, (*chip, c), src=x_ref) for j, chip in enumerate(chips)]
        for cp in first:
            cp.start()
        passed = [copy(4 + j, (*chip, c), sibling) for j, chip in enumerate(chips)]
        for j, chip in enumerate(chips):
            copy(1 + j, (*chip, c), me).wait_recv()
            passed[j].start()
        copy(0, sibling, me).wait_recv()
        for j, chip in enumerate(chips):
            copy(4 + j, (*chip, 1 - c), me).wait_recv()
        for cp in first + passed:
            cp.wait_send()
        mine.wait()

    return pl.pallas_call(
        body, name=name,
        out_shape=_sds((N_DEV * m_per, n), blk.dtype),
        in_specs=[pl.BlockSpec(memory_space=pltpu.VMEM)],
        out_specs=pl.BlockSpec(memory_space=pltpu.VMEM),
        scratch_shapes=[pltpu.SemaphoreType.DMA((7,)), pltpu.SemaphoreType.DMA((7,)), pltpu.SemaphoreType.DMA],
        compiler_params=pltpu.CompilerParams(vmem_limit_bytes=VMEM_LIMIT),
    )(blk)


def _row_tile(r, n):
    for cand in range(r, 15, -16):
        if r % cand == 0 and cand % 16 == 0 and cand * n * 4 <= 2 * 1024 * 1024:
            return cand
    return r


def _cast_slot(w, chip, name):
    r, n = w.shape
    tr = _row_tile(r, n)

    def body(chip_ref, w_ref, o_ref):
        o_ref[...] = w_ref[...].astype(BF16)

    grid_spec = pltpu.PrefetchScalarGridSpec(
        num_scalar_prefetch=1, grid=(r // tr,),
        in_specs=[pl.BlockSpec((tr, n), lambda i, ch: (i, 0))],
        out_specs=pl.BlockSpec((None, tr, n), lambda i, ch: (ch[0], i, 0)))
    return pl.pallas_call(
        body, name=name, grid_spec=grid_spec, out_shape=_sds((N_SHARD, r, n), BF16),
        compiler_params=_cp("arbitrary"),
    )(chip, w)


def _join_halves(tots, name):
    nt = len(tots)
    hom = [pl.BlockSpec(memory_space=pl.ANY)] * nt

    def body(*refs):
        outs = refs[nt:2 * nt]
        send_sems, recv_sems = refs[2 * nt:]
        x, y, c = _coords()
        sibling = (x, y, 1 - c)
        cps = []
        for t in range(nt):
            cp = pltpu.make_async_remote_copy(
                src_ref=outs[t].at[c], dst_ref=outs[t].at[c],
                send_sem=send_sems.at[t], recv_sem=recv_sems.at[t], device_id=sibling, device_id_type=MESH)
            cp.start()
            cps.append(cp)
        for t in range(nt):
            pltpu.make_async_remote_copy(
                src_ref=outs[t].at[c], dst_ref=outs[t].at[1 - c],
                send_sem=send_sems.at[t], recv_sem=recv_sems.at[t], device_id=sibling, device_id_type=MESH).wait_recv()
        for cp in cps:
            cp.wait_send()

    return pl.pallas_call(
        body, name=name,
        out_shape=[_sds(t.shape, t.dtype) for t in tots],
        in_specs=hom, out_specs=hom,
        input_output_aliases={t: t for t in range(nt)},
        scratch_shapes=[pltpu.SemaphoreType.DMA((nt,)), pltpu.SemaphoreType.DMA((nt,))],
    )(*tots)


def _pair_sum(g, recv, core, chip, name):
    _, _, r, n = g.shape
    tr = _row_tile(r, n)

    def body(core_ref, chip_ref, g_ref, r_ref, sb_ref, own_ref):
        tot = g_ref[...] + r_ref[...]
        sb_ref[...] = tot.astype(BF16)

        @pl.when(pl.program_id(1) == chip_ref[0])
        def _():
            own_ref[...] = tot

    grid_spec = pltpu.PrefetchScalarGridSpec(
        num_scalar_prefetch=2, grid=(r // tr, N_SHARD),
        in_specs=[pl.BlockSpec((None, None, tr, n), lambda i, s, co, ch: (s, co[0], i, 0)),
                  pl.BlockSpec((None, tr, n), lambda i, s, co, ch: (s, i, 0))],
        out_specs=[pl.BlockSpec((None, tr, n), lambda i, s, co, ch: (s, i, 0)),
                   pl.BlockSpec((tr, n), lambda i, s, co, ch: (i, 0))])
    return pl.pallas_call(
        body, name=name, grid_spec=grid_spec,
        out_shape=[_sds((N_SHARD, r, n), BF16), _sds((r, n), F32)],
        compiler_params=_cp("arbitrary", "arbitrary"),
    )(core, chip, g, recv)


def _chip_sum(own, recv, core, name):
    r, n = own.shape
    tr = _row_tile(r, n)

    def body(core_ref, o_ref, r_ref, t_ref):
        acc = o_ref[...]
        for j in range(3):
            acc = acc + r_ref[j].astype(F32)
        t_ref[...] = acc

    grid_spec = pltpu.PrefetchScalarGridSpec(
        num_scalar_prefetch=1, grid=(r // tr,),
        in_specs=[pl.BlockSpec((tr, n), lambda i, co: (i, 0)), pl.BlockSpec((3, tr, n), lambda i, co: (0, i, 0))],
        out_specs=pl.BlockSpec((None, tr, n), lambda i, co: (co[0], i, 0)))
    return pl.pallas_call(
        body, name=name, grid_spec=grid_spec, out_shape=_sds((2, r, n), F32),
        compiler_params=_cp("arbitrary"),
    )(core, own, recv)


_HBM = pl.BlockSpec(memory_space=pltpu.HBM)
_SEM = pl.BlockSpec(memory_space=pltpu.SEMAPHORE)
_EFFECT = pltpu.SideEffectType.DATAFLOW_SIDE_EFFECTING


def _ici_copies(srcs, dsts, send_sems, recv_sems, send_view, recv_view):
    x, y, c = _coords()
    out = []
    if send_view is None:
        for t in range(len(srcs)):
            r = srcs[t].shape[1] // 2
            out.append(pltpu.make_async_remote_copy(
                src_ref=srcs[t].at[:, pl.ds((1 - c) * r, r)], dst_ref=dsts[t],
                send_sem=send_sems.at[3 * t], recv_sem=recv_sems.at[3 * t],
                device_id=(x, y, 1 - c), device_id_type=MESH))
        return out
    for t in range(len(srcs)):
        for j, chip in enumerate(_other_chips(x, y)):
            out.append(pltpu.make_async_remote_copy(
                src_ref=send_view(srcs[t], chip, j, (x, y), c), dst_ref=recv_view(dsts[t], chip, j, (x, y), c),
                send_sem=send_sems.at[3 * t + j], recv_sem=recv_sems.at[3 * t + j],
                device_id=(*chip, c), device_id_type=MESH))
    return out


def _ici_start(srcs, dsts, after, send_view, recv_view, name):
    nt = len(srcs)
    inplace = dsts is None
    nbuf = nt if inplace else 2 * nt

    def body(*refs):
        send_sems, recv_sems = refs[nbuf + 1], refs[nbuf + 2]
        s_out = refs[nbuf + 3:nbuf + 3 + nt]
        d_out = s_out if inplace else refs[nbuf + 3 + nt:nbuf + 3 + 2 * nt]
        token = refs[-1]
        for cp in _ici_copies(s_out, d_out, send_sems, recv_sems, send_view, recv_view):
            cp.start()
        token[...] = jnp.zeros_like(token)

    bufs = list(srcs) + ([] if inplace else list(dsts))
    res = pl.pallas_call(
        body, name=name,
        out_shape=(pltpu.SemaphoreType.DMA((3 * nt,)), pltpu.SemaphoreType.DMA((3 * nt,)),
                   *[pltpu.HBM(b.shape, b.dtype) for b in bufs], _sds((8, 128), F32)),
        in_specs=[_HBM] * nbuf + [pl.BlockSpec(memory_space=pl.ANY)],
        out_specs=(_SEM, _SEM, *[_HBM] * nbuf, pl.BlockSpec(memory_space=pltpu.VMEM)),
        input_output_aliases={i: 2 + i for i in range(nbuf)},
        compiler_params=pltpu.CompilerParams(has_side_effects=_EFFECT),
    )(*[pltpu.with_memory_space_constraint(b, pltpu.HBM) for b in bufs], after)
    send_sems, recv_sems = res[0], res[1]
    s_thru = list(res[2:2 + nt])
    d_thru = s_thru if inplace else list(res[2 + nt:2 + 2 * nt])
    return send_sems, recv_sems, s_thru, d_thru, res[-1]


def _ici_wait(send_sems, recv_sems, srcs, dsts, after, send_view, recv_view, name):
    nt = len(srcs)
    inplace = dsts is None
    nbuf = nt if inplace else 2 * nt

    def body(*refs):
        send_ref, recv_ref = refs[nbuf], refs[nbuf + 1]
        s_out = refs[nbuf + 3:nbuf + 3 + nt]
        d_out = s_out if inplace else refs[nbuf + 3 + nt:nbuf + 3 + 2 * nt]
        for cp in _ici_copies(s_out, d_out, send_ref, recv_ref, send_view, recv_view):
            cp.wait_send()
            cp.wait_recv()

    bufs = list(srcs) + ([] if inplace else list(dsts))
    res = pl.pallas_call(
        body, name=name,
        out_shape=tuple(pltpu.HBM(b.shape, b.dtype) for b in bufs),
        in_specs=[_HBM] * nbuf + [_SEM, _SEM, pl.BlockSpec(memory_space=pl.ANY)],
        out_specs=tuple([_HBM] * nbuf),
        input_output_aliases={i: i for i in range(nbuf)},
        compiler_params=pltpu.CompilerParams(has_side_effects=_EFFECT),
    )(*bufs, send_sems, recv_sems, after)
    return list(res[:nt]) if inplace else (list(res[:nt]), list(res[nt:]))


def _w_half(buf, chip, c):
    r = buf.shape[1] // 2
    return buf.at[2 * chip[0] + chip[1], pl.ds(c * r, r)]


def _ag_send_view(buf, chip, j, me, c):
    return _w_half(buf, me, c)


def _ag_recv_view(buf, chip, j, me, c):
    return _w_half(buf, me, c)


def _rs_send_view(buf, chip, j, me, c):
    return buf.at[2 * chip[0] + chip[1]]


def _rs_recv_view(buf, chip, j, me, c):
    return buf.at[j]


def _ag_forward(bufs, name):
    nt = len(bufs)
    hom = [pl.BlockSpec(memory_space=pl.ANY)] * nt

    def body(*refs):
        outs = refs[nt:2 * nt]
        send_sems, recv_sems = refs[2 * nt:]
        x, y, c = _coords()
        sibling = (x, y, 1 - c)
        chips = _other_chips(x, y)

        def copy(t, j, hc):
            blk = _w_half(outs[t], chips[j], hc)
            return pltpu.make_async_remote_copy(
                src_ref=blk, dst_ref=blk, send_sem=send_sems.at[t, j], recv_sem=recv_sems.at[t, j],
                device_id=sibling, device_id_type=MESH)

        started = [copy(t, j, c) for t in range(nt) for j in range(3)]
        for cp in started:
            cp.start()
        for t in range(nt):
            for j in range(3):
                copy(t, j, 1 - c).wait_recv()
        for cp in started:
            cp.wait_send()

    return pl.pallas_call(
        body, name=name,
        out_shape=[_sds(b.shape, b.dtype) for b in bufs],
        in_specs=hom, out_specs=hom,
        input_output_aliases={t: t for t in range(nt)},
        scratch_shapes=[pltpu.SemaphoreType.DMA((nt, 3)), pltpu.SemaphoreType.DMA((nt, 3))],
    )(*bufs)


def _rs_swap_begin(grads, after, tag):
    land = [lax.empty((N_SHARD, g.shape[1] // 2, g.shape[2]), g.dtype) for g in grads]
    send_sems, recv_sems, s_thru, d_thru, token = _ici_start(grads, land, after, None, None, name="rs_swapgo_" + tag)
    return dict(sems=(send_sems, recv_sems), grads=s_thru, land=d_thru, tag=tag), token


def _rs_scatter_begin(swap, after):
    tag = swap["tag"]
    x, y, c = _coords()
    core = jnp.reshape(c, (1,)).astype(jnp.int32)
    chip = jnp.reshape(2 * x + y, (1,)).astype(jnp.int32)
    grads, recv = _ici_wait(*swap["sems"], swap["grads"], swap["land"], after, None, None, name="rs_swapend_" + tag)
    sums, owns = [], []
    for t, (g, rv) in enumerate(zip(grads, recv)):
        r = g.shape[1] // 2
        sb, own = _pair_sum(g.reshape(N_SHARD, 2, r, g.shape[2]), rv, core, chip, name=f"rs_pair_{tag}_{t}")
        sums.append(sb)
        owns.append(own)
    land = [lax.empty((3,) + s.shape[1:], s.dtype) for s in sums]
    send_sems, recv_sems, s_thru, d_thru, token = _ici_start(
        sums, land, after, _rs_send_view, _rs_recv_view, name="rs_start_" + tag)
    return dict(sems=(send_sems, recv_sems), sums=s_thru, land=d_thru, owns=owns, core=core, tag=tag), token


def _rs_end(state, after):
    tag = state["tag"]
    _, got = _ici_wait(*state["sems"], state["sums"], state["land"], after, _rs_send_view, _rs_recv_view,
                       name="rs_wait_" + tag)
    tots = [_chip_sum(o, gt, state["core"], name=f"rs_chip_{tag}_{t}")
            for t, (o, gt) in enumerate(zip(state["owns"], got))]
    full = _join_halves(tots, name="rs_join_" + tag)
    return [f.reshape(2 * f.shape[1], f.shape[2]) for f in full]


def _rope_lane_table():
    d = jnp.arange(128) % HEAD
    inv_freq = ROPE_THETA ** (-jnp.arange(0, ROT, 2, dtype=F32) / ROT)
    rot = d < ROT
    rows = [jnp.where(rot, inv_freq[d % (ROT // 2)], 0.0), rot.astype(F32),
            (d < ROT // 2).astype(F32), jnp.logical_and(d >= ROT // 2, rot).astype(F32)]
    return jnp.concatenate([jnp.stack(rows), jnp.zeros((4, 128), F32)], axis=0)


def _pad8(rows):
    return jnp.concatenate([rows, jnp.zeros((8 - rows.shape[0], rows.shape[1]), F32)], axis=0)


def kernel(x, c, positions, ada_w, ada_b, w_in, b_in, sinks, pool_w, pool_scale, w_out, w_gate, w_up, w_down, g_pre_mix, g_post_mix, g_pre_ffn, g_post_ffn, loss_target, m_ada_w, m_ada_b, m_w_in, m_b_in, m_sinks, m_pool_w, m_pool_scale, m_w_out, m_w_gate, m_w_up, m_w_down, m_g_pre_mix, m_g_post_mix, m_g_pre_ffn, m_g_post_ffn, v_ada_w, v_ada_b, v_w_in, v_b_in, v_sinks, v_pool_w, v_pool_scale, v_w_out, v_w_gate, v_w_up, v_w_down, v_g_pre_mix, v_g_post_mix, v_g_pre_ffn, v_g_post_ffn):
    T = x.shape[1]
    n_layers = ada_w.shape[0]
    ax, ay, ac = _coords()
    my_dev = 4 * ax + 2 * ay + ac
    my_chip = 2 * ax + ay
    x0 = x.reshape(T, D_MODEL)
    target = loss_target.reshape(T, D_MODEL)

    c_all = _allgather8(c.reshape(8, 128), name="ag_c").reshape(N_DEV, D_MODEL)
    ada_b_sh = lax.dynamic_slice_in_dim(ada_b, my_chip * ADA_SH, ADA_SH, axis=1).reshape(n_layers, 1, ADA_SH)
    mod_part = _mod_fwd(c_all, ada_w, ada_b_sh)
    mod_all = _allgather8(mod_part.reshape(n_layers * 8, ADA_SH), name="ag_mod")
    mod_all = mod_all.reshape(N_DEV, n_layers, 8, ADA_SH)[0::2]
    mod_mine = lax.dynamic_index_in_dim(mod_all, my_dev, axis=2, keepdims=False)
    mod = jnp.transpose(mod_mine, (1, 0, 2)).reshape(n_layers, 6, D_MODEL)

    chip1 = jnp.reshape(my_chip, (1,)).astype(jnp.int32)

    def tr(t):
        return jnp.transpose(t, (0, 2, 1))

    w_in_t, w_gate_t, w_up_t = tr(w_in), tr(w_gate), tr(w_up)

    def cast_layer(l):
        return [_cast_slot(w[l], chip1, name=f"cast_{nm}{l}")
                for nm, w in (("w_in", w_in_t), ("w_out", w_out), ("w_gate", w_gate_t), ("w_up", w_up_t),
                              ("w_down", w_down))]

    def as_operands(bufs):
        gin, gout, gg, gu, gd = bufs
        return (gin.reshape(IN_W, D_MODEL), gout.reshape(D_MODEL, D_MODEL), gg.reshape(D_FF, D_MODEL),
                gu.reshape(D_FF, D_MODEL), gd.reshape(D_FF, D_MODEL))

    bufs0 = cast_layer(0)
    in_send, in_recv, in_bufs, _, in_token = _ici_start(
        bufs0[:1], None, mod, _ag_send_view, _ag_recv_view, name="ag_start_0_in")
    pos_b = jnp.broadcast_to(positions.reshape(T, 1), (T, 128))
    rc, rs1, rs2 = _rope_tables(pos_b, _rope_lane_table() + in_token[0, 0])
    arrived = _ici_wait(in_send, in_recv, in_bufs, None, rc, _ag_send_view, _ag_recv_view, name="ag_wait_0_in")
    win0 = _ag_forward(arrived, name="ag_fwd_0_in")
    rest_send, rest_recv, rest_bufs, _, ag_token = _ici_start(
        bufs0[1:], None, win0[0], _ag_send_view, _ag_recv_view, name="ag_start_0")
    weights = [None] * n_layers

    saved = []
    xl = x0
    for l in range(n_layers):
        mod8 = _pad8(mod[l])
        if l + 1 < n_layers:
            ag_send, ag_recv, ag_bufs, _, ag_token = _ici_start(
                cast_layer(l + 1), None, ag_token, _ag_send_view, _ag_recv_view, name=f"ag_start_{l + 1}")
        if l == 0 or l + 1 < n_layers:
            mod8 = mod8 + ag_token[0, 0]
        g8 = _pad8(jnp.stack([g_pre_mix[l], g_post_mix[l], g_pre_ffn[l], g_post_ffn[l]]))
        sink_b = jnp.broadcast_to(sinks[l][:, None], (N_HEADS, 128))
        psc = pool_scale[l].reshape(1, POOL_W)
        win = win0[0].reshape(IN_W, D_MODEL) if l == 0 else weights[l][0]
        h, q, k, v, u = _fwd_in(xl, mod8, g8, win, b_in[l].reshape(1, IN_W), rc, rs1, rs2)
        attn, lse = _attn_fwd(q, k, v, sink_b)
        pool, pooled = _pool_fwd(u, pool_w[l], psc)
        if l == 0:
            arrived = _ici_wait(rest_send, rest_recv, rest_bufs, None, pool, _ag_send_view, _ag_recv_view,
                                name="ag_wait_0")
            weights[0] = as_operands(win0 + _ag_forward(arrived, name="ag_fwd_0"))
        win, wout, wg, wu, wd = weights[l]
        if l + 1 < n_layers:
            mix, x1, h2, act, ga, gb, f, x2 = _out_ffn_fwd(attn, pool, xl, wout, mod8, g8, wg, wu, wd)
        else:
            mix, x1, h2, act, ga, gb, f, x2, loss_tile = _out_ffn_fwd(attn, pool, xl, wout, mod8, g8, wg, wu, wd,
                                                                      target=target)
        saved.append(dict(x=xl, h=h, q=q, k=k, v=v, lse=lse, attn=attn, pool=pool, pooled=pooled, mix=mix,
                          x1=x1, h2=h2, act=act, ga=ga, gb=gb, f=f, mod8=mod8, g8=g8, sink_b=sink_b, psc=psc))
        xl = x2
        if l + 1 < n_layers:
            arrived = _ici_wait(ag_send, ag_recv, ag_bufs, None, x2, _ag_send_view, _ag_recv_view,
                                name=f"ag_wait_{l + 1}")
            weights[l + 1] = as_operands(_ag_forward(arrived, name=f"ag_fwd_{l + 1}"))

    dy = xl
    loss = lax.psum(loss_tile[0, 0], ("x", "y", "c"))

    small = [None] * n_layers
    dmod_rows = [None] * n_layers
    reduced = [dict() for _ in range(n_layers)]
    att_swap = None
    dx = dy
    for l in reversed(range(n_layers)):
        s = saved[l]
        win, wout, wg, wu, wd = weights[l]
        if att_swap is not None:
            s = dict(s, mod8=s["mod8"] + att_swap[1][0, 0])
        dx1, df, da, db, red_f = _ffn_bwd(dx, s["f"], s["ga"], s["gb"], s["x1"], s["mod8"], s["g8"], wg, wu, wd)
        token = None
        if att_swap is not None:
            att_scatter = _rs_scatter_begin(att_swap[0], dx1)
            token = att_scatter[1]
        ffn_shards = (N_SHARD, FF_SH, D_MODEL)
        g_wd = _wgrad(s["act"], df, name="wgrad_down", after=token).reshape(ffn_shards)
        g_wg = _wgrad(da, s["h2"], name="wgrad_gate").reshape(ffn_shards)
        g_wu = _wgrad(db, s["h2"], name="wgrad_up").reshape(ffn_shards)
        ffn_swap = _rs_swap_begin([g_wg, g_wu, g_wd], dx1, tag=f"{l}f")
        if att_swap is not None:
            got = _rs_end(att_scatter[0], ffn_swap[1])
            reduced[l + 1].update(w_in=got[0], w_out=got[1])
        s = dict(s, mod8=s["mod8"] + ffn_swap[1][0, 0])
        dmix, dattn, dpool, red_c = _mix_bwd(dx1, s["mix"], s["mod8"], s["g8"], wout)
        g_wout = jnp.concatenate([_wgrad(s["attn"], dmix, name="wgrad_out_a"),
                                  _wgrad(s["pool"], dmix, name="wgrad_out_p")], axis=0)
        ffn_scatter = _rs_scatter_begin(ffn_swap[0], dattn)
        dq, dk_e, dk_o, dv_e, dv_o, dsink = _attn_bwd(s["q"], s["k"], s["v"], s["lse"], dattn,
                                                      s["sink_b"] + ffn_scatter[1][0:1, :])
        du, g_poolw, dpsc = _pool_bwd(dpool, s["pooled"], pool_w[l], s["psc"])
        dx, dproj, red_d, dbin = _in_bwd(dq, (dk_e, dk_o), (dv_e, dv_o), du, rc, rs1, rs2, s["x"], dx1, s["mod8"],
                                         s["g8"], win)
        g_win = _wgrad(dproj, s["h"], name="wgrad_in")
        g_win_sh = g_win.reshape(N_SHARD, IN_SH, D_MODEL)
        got = _rs_end(ffn_scatter[0], dproj)
        reduced[l].update(w_gate=got[0], w_up=got[1], w_down=got[2])
        att_swap = _rs_swap_begin([g_win_sh, g_wout.reshape(N_SHARD, OUT_SH, D_MODEL)], dx, tag=f"{l}a")
        dmod_rows[l] = jnp.concatenate([red_d[0], red_d[1], red_c[0], red_f[2], red_f[3], red_f[0]])
        small[l] = jnp.concatenate([red_d[2], red_c[1], red_f[4], red_f[1], dbin[0], dpsc[0], dsink[:, 0],
                                    jnp.zeros((120,), F32), g_poolw.reshape(-1)])
    grad_x = dx.reshape(1, T, D_MODEL)

    per_layer = small[0].shape[0]
    rows_small = n_layers * per_layer // 128
    rows_mod = n_layers * 6 * D_MODEL // 128
    rows_pad = -(rows_small + rows_mod) % 8
    pack = jnp.concatenate(small + dmod_rows + [jnp.zeros((rows_pad * 128,), F32)]).reshape(-1, 128)
    pack = pack + att_swap[1][0, 0]
    gathered = _allgather8(pack, name="ag_small").reshape(N_DEV, pack.shape[0], 128)
    summed = _sum_devices(gathered)
    att_scatter = _rs_scatter_begin(att_swap[0], summed)
    small_sum = summed[:rows_small].reshape(n_layers, per_layer)
    o = 0
    small_g = {}
    for nm, width in (("g_pre_mix", D_MODEL), ("g_post_mix", D_MODEL), ("g_pre_ffn", D_MODEL),
                      ("g_post_ffn", D_MODEL), ("b_in", IN_W), ("pool_scale", POOL_W), ("sinks", 128),
                      ("pool_w", 4 * 128 * 128)):
        small_g[nm] = small_sum[:, o:o + width]
        o += width
    small_g["sinks"] = small_g["sinks"][:, :N_HEADS]
    small_g["pool_w"] = small_g["pool_w"].reshape(n_layers, 4, 128, 128)
    small_g["ada_b"] = summed[rows_small:rows_small + rows_mod].reshape(n_layers, 6 * D_MODEL)
    dmod_all = gathered[:, rows_small:rows_small + rows_mod].reshape(N_DEV, n_layers, N_SHARD, ADA_SH)
    dmod_sh = lax.dynamic_index_in_dim(dmod_all, my_chip, axis=2, keepdims=False)
    g_ada_w = _ada_wgrad(jnp.transpose(c_all), jnp.transpose(dmod_sh, (1, 0, 2)))

    grads = dict(ada_w=g_ada_w, ada_b=small_g["ada_b"], b_in=small_g["b_in"], sinks=small_g["sinks"],
                 pool_w=small_g["pool_w"], pool_scale=small_g["pool_scale"], g_pre_mix=small_g["g_pre_mix"],
                 g_post_mix=small_g["g_post_mix"], g_pre_ffn=small_g["g_pre_ffn"], g_post_ffn=small_g["g_post_ffn"])
    params = dict(ada_w=(ada_w, m_ada_w, v_ada_w), ada_b=(ada_b, m_ada_b, v_ada_b), w_in=(w_in, m_w_in, v_w_in),
                  b_in=(b_in, m_b_in, v_b_in), sinks=(sinks, m_sinks, v_sinks), pool_w=(pool_w, m_pool_w, v_pool_w),
                  pool_scale=(pool_scale, m_pool_scale, v_pool_scale), w_out=(w_out, m_w_out, v_w_out),
                  w_gate=(w_gate, m_w_gate, v_w_gate), w_up=(w_up, m_w_up, v_w_up),
                  w_down=(w_down, m_w_down, v_w_down), g_pre_mix=(g_pre_mix, m_g_pre_mix, v_g_pre_mix),
                  g_post_mix=(g_post_mix, m_g_post_mix, v_g_post_mix), g_pre_ffn=(g_pre_ffn, m_g_pre_ffn, v_g_pre_ffn),
                  g_post_ffn=(g_post_ffn, m_g_post_ffn, v_g_post_ffn))
    names = list(params)
    updates = {nm: _adamw_nd(*params[nm][:1], grads[nm], *params[nm][1:], name="adamw_" + nm) for nm in grads}

    got = _rs_end(att_scatter[0], updates["ada_w"][0])
    reduced[0].update(w_in=got[0], w_out=got[1])
    for nm in ("w_in", "w_out", "w_gate", "w_up", "w_down"):
        g = jnp.stack([reduced[l][nm] for l in range(n_layers)])
        if nm in ("w_in", "w_gate", "w_up"):
            upd = _adamw_nd(tr(params[nm][0]), g, tr(params[nm][1]), tr(params[nm][2]), name="adamw_" + nm)
            grads[nm], updates[nm] = tr(g), [tr(u) for u in upd]
        else:
            grads[nm], updates[nm] = g, _adamw_nd(params[nm][0], g, *params[nm][1:], name="adamw_" + nm)
    return (loss, grad_x, *[grads[nm] for nm in names], *[updates[nm][0] for nm in names],
            *[updates[nm][1] for nm in names], *[updates[nm][2] for nm in names])
```

```python
import jax
import jax.numpy as jnp
from jax import lax
from jax.experimental import pallas as pl
from jax.experimental.pallas import tpu as pltpu

F32 = jnp.float32
BF16 = jnp.bfloat16
MESH = pl.DeviceIdType.MESH

D_MODEL = 1024
ATTN_W = 512
KV_W = 128
KVD_W = 256
POOL_W = 512
IN_W = 1280
D_FF = 2816
N_SHARD = 4
FF_SH = D_FF // N_SHARD
IN_SH = IN_W // N_SHARD
OUT_SH = D_MODEL // N_SHARD
ADA_SH = 6 * D_MODEL // N_SHARD
HEAD = 64
N_HEADS = 8
GROUP = 4
BLK = 128
POOL_WINDOWS = (2, 4, 8, 16)
HALO = 16
ROT = 16
ROPE_THETA = 500000.0
EPS = 1e-6
NEG_INF = -1e30
N_DEV = 8

ADAM_LR = 0.001
ADAM_B1 = 0.9
ADAM_B2 = 0.999
ADAM_EPS = 1e-08
ADAM_WD = 0.01
ADAM_STEP = 10

VMEM_LIMIT = 48 * 1024 * 1024
FFN_VMEM_LIMIT = 60 * 1024 * 1024
ATTN_FWD_BLOCKS = 8
ATTN_BWD_BLOCKS = 8
WGRAD_TOKENS = 2048


def _cp(*sem, vmem=VMEM_LIMIT):
    return pltpu.CompilerParams(dimension_semantics=sem, vmem_limit_bytes=vmem)


def _full(shape):
    nd = len(shape)
    return pl.BlockSpec(shape, lambda *_: (0,) * nd)


def _resident(shape):
    nd = len(shape)
    return pl.BlockSpec(shape, lambda *_: (0,) * nd, pipeline_mode=pl.Buffered(1))


def _rows(tm, ncol):
    return pl.BlockSpec((tm, ncol), lambda i: (i, 0))


def _sds(shape, dtype):
    return jax.ShapeDtypeStruct(shape, dtype)


def _nt(a, b):
    return lax.dot_general(a, b, (((1,), (1,)), ((), ())), preferred_element_type=F32)


def _tn(a, b):
    return lax.dot_general(a, b, (((0,), (0,)), ((), ())), preferred_element_type=F32)


def _mm(a, b):
    return jnp.dot(a, b, preferred_element_type=F32)


def _rstd(x):
    return lax.rsqrt(jnp.mean(x * x, axis=-1, keepdims=True) + EPS)


def _colsum(x):
    return jnp.sum(x, axis=0, keepdims=True)


def _norm_gain_bwd(dy, xhat, rstd, gain):
    p = dy * xhat
    dx = rstd * (dy * gain - xhat * jnp.mean(p * gain, axis=-1, keepdims=True))
    return dx, _colsum(p)


def _rope_tables(pos_b, lane_tab):
    T = pos_b.shape[0]
    tm = min(T, 1024)

    def body(pos_ref, tab_ref, c_ref, s1_ref, s2_ref):
        ang = pos_ref[...].astype(F32) * tab_ref[0:1, :]
        cs = jnp.cos(ang)
        sn = jnp.sin(ang)
        m_rot = tab_ref[1:2, :]
        c_ref[...] = cs * m_rot + (1.0 - m_rot)
        s1_ref[...] = -sn * tab_ref[2:3, :]
        s2_ref[...] = sn * tab_ref[3:4, :]

    out = _sds((T, 128), F32)
    return pl.pallas_call(
        body, name="rope_tables", grid=(T // tm,),
        in_specs=[_rows(tm, 128), _full((8, 128))],
        out_specs=[_rows(tm, 128)] * 3, out_shape=[out] * 3,
        compiler_params=_cp("parallel"),
    )(pos_b, lane_tab)


def _rot_fwd(t, c, s1, s2):
    w = t.shape[-1]
    return t * c + pltpu.roll(t, w - 8, 1) * s1 + pltpu.roll(t, 8, 1) * s2


def _rot_bwd(d, c, s1, s2):
    w = d.shape[-1]
    return d * c + pltpu.roll(d * s1, 8, 1) + pltpu.roll(d * s2, w - 8, 1)


def _store_dup(ref, t):
    low = lax.broadcasted_iota(jnp.int32, t.shape, 1) < HEAD
    sw = pltpu.roll(t, HEAD, 1)
    ref[:, 0:128] = jnp.where(low, t, sw).astype(BF16)
    ref[:, 128:256] = jnp.where(low, sw, t).astype(BF16)


def _fold_dup(d):
    low = lax.broadcasted_iota(jnp.int32, (d.shape[0], 128), 1) < HEAD
    d0 = d[:, 0:128]
    d1 = d[:, 128:256]
    return jnp.where(low, d0 + pltpu.roll(d0, HEAD, 1), d1 + pltpu.roll(d1, HEAD, 1))


def _fwd_in(x, mod8, g8, w_in, b_in, rc, rs1, rs2):
    T = x.shape[0]
    tm = min(T, 1024)

    def body(x_ref, mod_ref, g_ref, w_ref, b_ref, c_ref, s1_ref, s2_ref,
             h_ref, q_ref, k_ref, v_ref, u_ref):
        xf = x_ref[...]
        h = (xf * _rstd(xf) * g_ref[0:1, :]) * (1.0 + mod_ref[1:2, :]) + mod_ref[0:1, :]
        hb = h.astype(BF16)
        h_ref[...] = hb
        c = c_ref[...]
        s1 = s1_ref[...]
        s2 = s2_ref[...]
        proj = _nt(hb, w_ref[...]) + b_ref[...]
        q = _rot_fwd(proj[:, 0:ATTN_W], jnp.tile(c, (1, 4)), jnp.tile(s1, (1, 4)), jnp.tile(s2, (1, 4)))
        q_ref[...] = (q * (HEAD ** -0.5)).astype(BF16)
        _store_dup(k_ref, _rot_fwd(proj[:, ATTN_W:ATTN_W + KV_W], c, s1, s2))
        _store_dup(v_ref, proj[:, ATTN_W + KV_W:ATTN_W + 2 * KV_W])
        u_ref[...] = proj[:, ATTN_W + 2 * KV_W:IN_W]

    return pl.pallas_call(
        body, name="fwd_in", grid=(T // tm,),
        in_specs=[_rows(tm, D_MODEL), _full((8, D_MODEL)), _full((8, D_MODEL)),
                  _resident((IN_W, D_MODEL)), _full((1, IN_W)),
                  _rows(tm, 128), _rows(tm, 128), _rows(tm, 128)],
        out_specs=[_rows(tm, D_MODEL), _rows(tm, ATTN_W), _rows(tm, KVD_W), _rows(tm, KVD_W), _rows(tm, POOL_W)],
        out_shape=[_sds((T, D_MODEL), BF16), _sds((T, ATTN_W), BF16), _sds((T, KVD_W), BF16),
                   _sds((T, KVD_W), BF16), _sds((T, POOL_W), F32)],
        compiler_params=_cp("parallel"),
    )(x, mod8, g8, w_in, b_in, rc, rs1, rs2)


def _band_mask(n):
    kk = lax.broadcasted_iota(jnp.int32, (2 * BLK, BLK), 0)
    qi = lax.broadcasted_iota(jnp.int32, (2 * BLK, BLK), 1)
    first = jnp.where(n > 0, 0, 2 * BLK)
    in_prev = jnp.logical_and(kk < BLK, kk > qi + first)
    in_cur = jnp.logical_and(kk >= BLK, (kk - BLK) <= qi)
    one = jnp.logical_or(in_prev, in_cur)
    return jnp.concatenate([one] * GROUP, axis=1)


def _head_row(ref, j, base=0):
    return jnp.concatenate([ref[base + GROUP * j + r:base + GROUP * j + r + 1, :] for r in range(GROUP)], axis=1)


def _stack_heads(x_ref, j, rows=slice(None)):
    low = lax.broadcasted_iota(jnp.int32, (BLK, 128), 1) < HEAD
    parts = []
    for gp in (2 * j, 2 * j + 1):
        x2 = x_ref[rows, gp * 128:(gp + 1) * 128]
        parts.append(jnp.where(low, x2, jnp.zeros_like(x2)))
        parts.append(jnp.where(low, jnp.zeros_like(x2), x2))
    return jnp.concatenate(parts, axis=0)


def _unstack_heads(o):
    low = lax.broadcasted_iota(jnp.int32, (BLK, 128), 1) < HEAD
    return [jnp.where(low, o[0:BLK], o[BLK:2 * BLK]), jnp.where(low, o[2 * BLK:3 * BLK], o[3 * BLK:4 * BLK])]


def _attn_fwd(q, kd, vd, sink_b):
    T = q.shape[0]
    nb = T // BLK
    nq = ATTN_FWD_BLOCKS if nb % ATTN_FWD_BLOCKS == 0 else 2
    assert nb % nq == 0

    def body(q_ref, kp_ref, kc_ref, vp_ref, vc_ref, sk_ref, o_ref, lse_ref):
        for sub in range(nq):
            rows = slice(sub * BLK, (sub + 1) * BLK)
            before = slice((sub - 1) * BLK, sub * BLK)
            valid = _band_mask(nq * pl.program_id(0) + sub)
            for j in range(N_HEADS // GROUP):
                lanes = slice(j * 128, (j + 1) * 128)
                k_prev = kp_ref[:, lanes] if sub == 0 else kc_ref[before, lanes]
                v_prev = vp_ref[:, lanes] if sub == 0 else vc_ref[before, lanes]
                kcat = jnp.concatenate([k_prev, kc_ref[rows, lanes]], axis=0)
                vcat = jnp.concatenate([v_prev, vc_ref[rows, lanes]], axis=0)
                s = jnp.where(valid, _nt(kcat, _stack_heads(q_ref, j, rows)), NEG_INF)
                sk = _head_row(sk_ref, j)
                m = jnp.maximum(jnp.max(s, axis=0, keepdims=True), sk)
                p = jnp.exp(s - m)
                den = jnp.sum(p, axis=0, keepdims=True) + jnp.exp(sk - m)
                p = p * (1.0 / den)
                o = _tn(p.astype(BF16), vcat)
                o_ref[rows, 2 * j * 128:(2 * j + 2) * 128] = jnp.concatenate(_unstack_heads(o), axis=1).astype(BF16)
                lse = m + jnp.log(den)
                for r in range(GROUP):
                    h = sub * N_HEADS + GROUP * j + r
                    lse_ref[h:h + 1, :] = lse[:, r * 128:(r + 1) * 128]

    prev = lambda i: (jnp.maximum(nq * i - 1, 0), 0)
    cur = lambda i: (i, 0)
    return pl.pallas_call(
        body, name="attn_fwd", grid=(nb // nq,),
        in_specs=[pl.BlockSpec((nq * BLK, ATTN_W), cur),
                  pl.BlockSpec((BLK, KVD_W), prev), pl.BlockSpec((nq * BLK, KVD_W), cur),
                  pl.BlockSpec((BLK, KVD_W), prev), pl.BlockSpec((nq * BLK, KVD_W), cur),
                  _full((8, 128))],
        out_specs=[pl.BlockSpec((nq * BLK, ATTN_W), cur), pl.BlockSpec((nq * N_HEADS, 128), cur)],
        out_shape=[_sds((T, ATTN_W), BF16), _sds((nb * N_HEADS, 128), F32)],
        compiler_params=_cp("parallel"),
    )(q, kd, kd, vd, vd, sink_b)


def _pool_fwd(u, pool_w, pool_scale):
    T = u.shape[0]
    tm = min(T, 1024)

    def body(u_ref, w_ref, sc_ref, out_ref, pooled_ref, halo):
        i = pl.program_id(0)

        @pl.when(i == 0)
        def _():
            halo[...] = jnp.zeros_like(halo)

        ub = u_ref[...]
        ext = jnp.concatenate([halo[...], ub], axis=0)
        halo[...] = ub[tm - HALO:, :]
        tpos = (i * tm + lax.broadcasted_iota(jnp.int32, (tm, 1), 0)).astype(F32)
        for g, w in enumerate(POOL_WINDOWS):
            lanes = slice(g * 128, (g + 1) * 128)
            s = ext[:, lanes]
            sh = 1
            while sh < w:
                s = s + pltpu.roll(s, sh, 0)
                sh *= 2
            cnt = jnp.minimum(tpos + 1.0, float(w))
            pb = (s[HALO:, :] / cnt - ub[:, lanes]).astype(BF16)
            z = _mm(pb, w_ref[g].astype(BF16))
            out_ref[:, lanes] = (z * sc_ref[:, lanes]).astype(BF16)
            pooled_ref[:, lanes] = pb

    return pl.pallas_call(
        body, name="pool_fwd", grid=(T // tm,),
        in_specs=[_rows(tm, POOL_W), _full((4, 128, 128)), _full((1, POOL_W))],
        out_specs=[_rows(tm, POOL_W), _rows(tm, POOL_W)],
        out_shape=[_sds((T, POOL_W), BF16), _sds((T, POOL_W), BF16)],
        scratch_shapes=[pltpu.VMEM((HALO, POOL_W), F32)],
        compiler_params=_cp("arbitrary"),
    )(u, pool_w, pool_scale)


FF_CHUNKS = ((0, 1024), (1024, 2048), (2048, D_FF))


def _out_ffn_fwd(attn, pool, x, w_out, mod8, g8, wg, wu, wd, target=None):
    T = x.shape[0]
    tm = min(T, 256)
    last = target is not None

    def body(*refs):
        a_ref, p_ref, xin_ref, wo_ref, mod_ref, g_ref, wg_ref, wu_ref, wd_ref = refs[:9]
        t_ref = refs[9] if last else None
        mix_ref, x1_ref, h_ref, act_ref, ga_ref, gb_ref, f_ref, x2_ref = refs[9 + last:17 + last]
        mix = _mm(a_ref[...], wo_ref[0:ATTN_W, :]) + _mm(p_ref[...], wo_ref[ATTN_W:, :])
        mix_ref[...] = mix
        xf = xin_ref[...] + mod_ref[2:3, :] * (mix * _rstd(mix) * g_ref[1:2, :])
        x1_ref[...] = xf
        h = (xf * _rstd(xf) * g_ref[2:3, :]) * (1.0 + mod_ref[4:5, :]) + mod_ref[3:4, :]
        hb = h.astype(BF16)
        h_ref[...] = hb
        f = jnp.zeros((tm, D_MODEL), F32)
        for lo, hi in FF_CHUNKS:
            a = _nt(hb, wg_ref[lo:hi, :])
            b = _nt(hb, wu_ref[lo:hi, :])
            sig = jax.nn.sigmoid(a)
            sl = a * sig
            act = (sl * b).astype(BF16)
            act_ref[:, lo:hi] = act
            ga_ref[:, lo:hi] = (b * (sig * (1.0 + a * (1.0 - sig)))).astype(BF16)
            gb_ref[:, lo:hi] = sl.astype(BF16)
            f = f + _mm(act, wd_ref[lo:hi, :])
        f_ref[...] = f
        x2 = xf + mod_ref[5:6, :] * (f * _rstd(f) * g_ref[3:4, :])
        if not last:
            x2_ref[...] = x2
        else:
            loss_ref = refs[18]

            @pl.when(pl.program_id(0) == 0)
            def _():
                loss_ref[...] = jnp.zeros_like(loss_ref)

            e = x2 - t_ref[...]
            x2_ref[...] = e * (1.0 / D_MODEL)
            loss_ref[...] += 0.5 * jnp.sum(jnp.mean(e * e, axis=-1, keepdims=True), axis=0, keepdims=True)

    act_shape = _sds((T, D_FF), BF16)
    wide = _sds((T, D_MODEL), F32)
    weights = [_resident((D_FF, D_MODEL))] * 3
    return pl.pallas_call(
        body, name="out_ffn_fwd_loss" if last else "out_ffn_fwd", grid=(T // tm,),
        in_specs=[_rows(tm, ATTN_W), _rows(tm, POOL_W), _rows(tm, D_MODEL), _resident((D_MODEL, D_MODEL)),
                  _full((8, D_MODEL)), _full((8, D_MODEL)), *weights]
        + ([_rows(tm, D_MODEL)] if last else []),
        out_specs=[_rows(tm, D_MODEL), _rows(tm, D_MODEL), _rows(tm, D_MODEL), _rows(tm, D_FF), _rows(tm, D_FF),
                   _rows(tm, D_FF), _rows(tm, D_MODEL), _rows(tm, D_MODEL)] + ([_full((8, 128))] if last else []),
        out_shape=[wide, wide, _sds((T, D_MODEL), BF16), act_shape, act_shape, act_shape, wide, wide]
        + ([_sds((8, 128), F32)] if last else []),
        compiler_params=_cp("arbitrary" if last else "parallel", vmem=FFN_VMEM_LIMIT),
    )(attn, pool, x, w_out, mod8, g8, wg, wu, wd, *([target] if last else []))


def _ffn_bwd(dx2, f, ga, gb, x1, mod8, g8, wg, wu, wd):
    T = dx2.shape[0]
    tm = min(T, 256)

    def body(dx_ref, f_ref, ga_ref, gb_ref, x_ref, mod_ref, g_ref, wg_ref, wu_ref, wd_ref,
             dx1_ref, df_ref, da_ref, db_ref, red_ref):
        @pl.when(pl.program_id(0) == 0)
        def _():
            red_ref[...] = jnp.zeros_like(red_ref)

        dx = dx_ref[...]
        fv = f_ref[...]
        rstd = _rstd(fv)
        fhat = fv * rstd
        gpost = g_ref[3:4, :]
        gate = mod_ref[5:6, :]
        df, s_post = _norm_gain_bwd(dx, fhat, rstd, gate * gpost)
        red_ref[0:1, :] += gpost * s_post
        red_ref[1:2, :] += gate * s_post
        dfb = df.astype(BF16)
        df_ref[...] = dfb
        dh = jnp.zeros((tm, D_MODEL), F32)
        for lo, hi in FF_CHUNKS:
            dact = _nt(dfb, wd_ref[lo:hi, :])
            da = (dact * ga_ref[:, lo:hi].astype(F32)).astype(BF16)
            db = (dact * gb_ref[:, lo:hi].astype(F32)).astype(BF16)
            da_ref[:, lo:hi] = da
            db_ref[:, lo:hi] = db
            dh = dh + _mm(da, wg_ref[lo:hi, :]) + _mm(db, wu_ref[lo:hi, :])
        xf = x_ref[...]
        rstd1 = _rstd(xf)
        xhat = xf * rstd1
        gpre = g_ref[2:3, :]
        scale1 = 1.0 + mod_ref[4:5, :]
        dxn, s_pre = _norm_gain_bwd(dh, xhat, rstd1, scale1 * gpre)
        red_ref[2:3, :] += _colsum(dh)
        red_ref[3:4, :] += gpre * s_pre
        red_ref[4:5, :] += scale1 * s_pre
        dx1_ref[...] = dx + dxn

    act_shape = _sds((T, D_FF), BF16)
    return pl.pallas_call(
        body, name="ffn_bwd", grid=(T // tm,),
        in_specs=[_rows(tm, D_MODEL), _rows(tm, D_MODEL), _rows(tm, D_FF), _rows(tm, D_FF), _rows(tm, D_MODEL),
                  _full((8, D_MODEL)), _full((8, D_MODEL)),
                  _resident((D_FF, D_MODEL)), _resident((D_FF, D_MODEL)), _resident((D_FF, D_MODEL))],
        out_specs=[_rows(tm, D_MODEL), _rows(tm, D_MODEL), _rows(tm, D_FF), _rows(tm, D_FF), _full((8, D_MODEL))],
        out_shape=[_sds((T, D_MODEL), F32), _sds((T, D_MODEL), BF16), act_shape, act_shape, _sds((8, D_MODEL), F32)],
        compiler_params=_cp("arbitrary"),
    )(dx2, f, ga, gb, x1, mod8, g8, wg, wu, wd)


def _wgrad(a, b, name, after=None):
    T, K = a.shape
    N = b.shape[1]
    tt = min(T, WGRAD_TOKENS)
    tk = next(c for c in (1408, 640, 512, 256, 128) if K % c == 0)

    def body(a_ref, b_ref, *rest):
        o_ref = rest[-1]

        @pl.when(pl.program_id(1) == 0)
        def _():
            o_ref[...] = jnp.zeros_like(o_ref)

        o_ref[...] += _tn(a_ref[...], b_ref[...])

    extra = [] if after is None else [after]
    return pl.pallas_call(
        body, name=name, grid=(K // tk, T // tt),
        in_specs=[pl.BlockSpec((tt, tk), lambda i, t: (t, i)), pl.BlockSpec((tt, N), lambda i, t: (t, 0))]
        + [pl.BlockSpec(memory_space=pl.ANY)] * len(extra),
        out_specs=pl.BlockSpec((tk, N), lambda i, t: (i, 0)),
        out_shape=_sds((K, N), F32),
        compiler_params=_cp("parallel", "arbitrary"),
    )(a, b, *extra)


def _mix_bwd(dx1, mix, mod8, g8, w_out):
    T = dx1.shape[0]
    tm = min(T, 1024)

    def body(dx_ref, mix_ref, mod_ref, g_ref, w_ref, dmix_ref, da_ref, dp_ref, red_ref):
        @pl.when(pl.program_id(0) == 0)
        def _():
            red_ref[...] = jnp.zeros_like(red_ref)

        dx = dx_ref[...]
        mv = mix_ref[...]
        rstd = _rstd(mv)
        mhat = mv * rstd
        gpost = g_ref[1:2, :]
        gate = mod_ref[2:3, :]
        dm, s_post = _norm_gain_bwd(dx, mhat, rstd, gate * gpost)
        red_ref[0:1, :] += gpost * s_post
        red_ref[1:2, :] += gate * s_post
        dmb = dm.astype(BF16)
        dmix_ref[...] = dmb
        dap = _nt(dmb, w_ref[...])
        da_ref[...] = dap[:, 0:ATTN_W].astype(BF16)
        dp_ref[...] = dap[:, ATTN_W:].astype(BF16)

    return pl.pallas_call(
        body, name="mix_bwd", grid=(T // tm,),
        in_specs=[_rows(tm, D_MODEL), _rows(tm, D_MODEL), _full((8, D_MODEL)), _full((8, D_MODEL)),
                  _resident((D_MODEL, D_MODEL))],
        out_specs=[_rows(tm, D_MODEL), _rows(tm, ATTN_W), _rows(tm, POOL_W), _full((8, D_MODEL))],
        out_shape=[_sds((T, D_MODEL), BF16), _sds((T, ATTN_W), BF16), _sds((T, POOL_W), BF16),
                   _sds((8, D_MODEL), F32)],
        compiler_params=_cp("arbitrary"),
    )(dx1, mix, mod8, g8, w_out)


def _attn_bwd(q, kd, vd, lse, dattn, sink_b):
    T = q.shape[0]
    nb = T // BLK
    nq = ATTN_BWD_BLOCKS if nb % ATTN_BWD_BLOCKS == 0 else 2
    assert nb % nq == 0
    nstep = nb // nq

    def body(q_ref, do_ref, lse_ref, kp_ref, kc_ref, vp_ref, vc_ref, sk_ref,
             dq_ref, dkm_ref, dkt_ref, dvm_ref, dvt_ref, dsk_ref, carry_k, carry_v):
        i = pl.program_id(0)

        @pl.when(i == 0)
        def _():
            carry_k[...] = jnp.zeros_like(carry_k)
            carry_v[...] = jnp.zeros_like(carry_v)
            dsk_ref[...] = jnp.zeros_like(dsk_ref)

        @pl.when(i < nstep)
        def _():
            for j in range(N_HEADS // GROUP):
                lanes = slice(j * 128, (j + 1) * 128)
                parts_k, parts_v = [], []
                for sub in range(nq):
                    rows = slice(sub * BLK, (sub + 1) * BLK)
                    before = slice((sub - 1) * BLK, sub * BLK)
                    valid = _band_mask(nq * i + sub)
                    k_prev = kp_ref[:, lanes] if sub == 0 else kc_ref[before, lanes]
                    v_prev = vp_ref[:, lanes] if sub == 0 else vc_ref[before, lanes]
                    kcat = jnp.concatenate([k_prev, kc_ref[rows, lanes]], axis=0)
                    vcat = jnp.concatenate([v_prev, vc_ref[rows, lanes]], axis=0)
                    qs = _stack_heads(q_ref, j, rows)
                    dos = _stack_heads(do_ref, j, rows)
                    lse = _head_row(lse_ref, j, sub * N_HEADS)
                    p = jnp.exp(jnp.where(valid, _nt(kcat, qs), NEG_INF) - lse)
                    dp = _nt(vcat, dos)
                    delta = jnp.sum(p * dp, axis=0, keepdims=True)
                    ds = (p * (dp - delta)).astype(BF16)
                    sink_term = jnp.exp(_head_row(sk_ref, j) - lse) * delta
                    for r in range(GROUP):
                        h = GROUP * j + r
                        dsk_ref[h:h + 1, :] += -jnp.sum(sink_term[:, r * 128:(r + 1) * 128], axis=1, keepdims=True)
                    dq_ref[rows, 2 * j * 128:(2 * j + 2) * 128] = jnp.concatenate(
                        _unstack_heads(_tn(ds, kcat)), axis=1)
                    parts_k.append(_mm(ds, qs))
                    parts_v.append(_mm(p.astype(BF16), dos))
                dkt_ref[:, lanes] = carry_k[:, lanes] + parts_k[0][0:BLK]
                dvt_ref[:, lanes] = carry_v[:, lanes] + parts_v[0][0:BLK]
                for s in range(nq - 1):
                    dkm_ref[s * BLK:(s + 1) * BLK, lanes] = parts_k[s][BLK:] + parts_k[s + 1][0:BLK]
                    dvm_ref[s * BLK:(s + 1) * BLK, lanes] = parts_v[s][BLK:] + parts_v[s + 1][0:BLK]
                carry_k[:, lanes] = parts_k[nq - 1][BLK:]
                carry_v[:, lanes] = parts_v[nq - 1][BLK:]

        @pl.when(i == nstep)
        def _():
            dkt_ref[...] = carry_k[...]
            dvt_ref[...] = carry_v[...]

    cur = lambda i: (jnp.minimum(i, nstep - 1), 0)
    prev = lambda i: (jnp.minimum(jnp.maximum(nq * i - 1, 0), nb - 1), 0)
    tail = lambda i: (jnp.maximum(i - 1, 0), 0)
    main_shape = _sds((nstep * (nq - 1) * BLK, KVD_W), F32)
    tail_shape = _sds((nstep * BLK, KVD_W), F32)
    main_spec = pl.BlockSpec(((nq - 1) * BLK, KVD_W), cur)
    tail_spec = pl.BlockSpec((BLK, KVD_W), tail)
    return pl.pallas_call(
        body, name="attn_bwd", grid=(nstep + 1,),
        in_specs=[pl.BlockSpec((nq * BLK, ATTN_W), cur), pl.BlockSpec((nq * BLK, ATTN_W), cur),
                  pl.BlockSpec((nq * N_HEADS, 128), cur),
                  pl.BlockSpec((BLK, KVD_W), prev), pl.BlockSpec((nq * BLK, KVD_W), cur),
                  pl.BlockSpec((BLK, KVD_W), prev), pl.BlockSpec((nq * BLK, KVD_W), cur),
                  _full((8, 128))],
        out_specs=[pl.BlockSpec((nq * BLK, ATTN_W), cur), main_spec, tail_spec, main_spec, tail_spec, _full((8, 128))],
        out_shape=[_sds((T, ATTN_W), F32), main_shape, tail_shape, main_shape, tail_shape, _sds((8, 128), F32)],
        scratch_shapes=[pltpu.VMEM((BLK, KVD_W), F32), pltpu.VMEM((BLK, KVD_W), F32)],
        compiler_params=_cp("arbitrary"),
    )(q, dattn, lse, kd, kd, vd, vd, sink_b)


def _pool_bwd(dpool, pooled, pool_w, pool_scale):
    T = dpool.shape[0]
    tm = min(T, 1024)
    nbk = T // tm
    ext_rows = tm + HALO

    def body(dp_ref, pl_ref, w_ref, sc_ref, du_ref, dw_ref, dsc_ref, halo):
        i = pl.program_id(0)

        @pl.when(i == 0)
        def _():
            halo[...] = jnp.zeros_like(halo)
            dw_ref[...] = jnp.zeros_like(dw_ref)
            dsc_ref[...] = jnp.zeros_like(dsc_ref)

        blk = nbk - 1 - i
        tpos = (blk * tm + lax.broadcasted_iota(jnp.int32, (tm, 1), 0)).astype(F32)
        for g, w in enumerate(POOL_WINDOWS):
            lanes = slice(g * 128, (g + 1) * 128)
            dp = dp_ref[:, lanes].astype(F32)
            pb = pl_ref[:, lanes]
            wg = w_ref[g].astype(BF16)
            z = _mm(pb, wg)
            dsc_ref[0:1, lanes] += _colsum(dp * z)
            dz = (dp * sc_ref[:, lanes]).astype(BF16)
            dw_ref[g] += _tn(pb, dz)
            dpl = _nt(dz, wg)
            e = dpl / jnp.minimum(tpos + 1.0, float(w))
            s = jnp.concatenate([e, halo[:, lanes]], axis=0)
            halo[:, lanes] = e[0:HALO, :]
            sh = 1
            while sh < w:
                s = s + pltpu.roll(s, ext_rows - sh, 0)
                sh *= 2
            du_ref[:, lanes] = s[0:tm, :] - dpl

    rev = lambda i: (nbk - 1 - i, 0)
    return pl.pallas_call(
        body, name="pool_bwd", grid=(nbk,),
        in_specs=[pl.BlockSpec((tm, POOL_W), rev), pl.BlockSpec((tm, POOL_W), rev),
                  _full((4, 128, 128)), _full((1, POOL_W))],
        out_specs=[pl.BlockSpec((tm, POOL_W), rev), _full((4, 128, 128)), _full((8, POOL_W))],
        out_shape=[_sds((T, POOL_W), F32), _sds((4, 128, 128), F32), _sds((8, POOL_W), F32)],
        scratch_shapes=[pltpu.VMEM((HALO, POOL_W), F32)],
        compiler_params=_cp("arbitrary"),
    )(dpool, pooled, pool_w, pool_scale)


def _interleave_groups(main, tail):
    groups = tail.shape[0] // BLK
    m = main.shape[0] // groups
    parts = []
    for b in range(groups):
        parts += [main[b * m:(b + 1) * m], tail[b * BLK:(b + 1) * BLK]]
    return jnp.concatenate(parts, axis=0)


def _in_bwd(dq, dk_mt, dv_mt, du, rc, rs1, rs2, x, dx1, mod8, g8, w_in):
    T = x.shape[0]
    tm = min(T, 1024)
    nq = T // dk_mt[1].shape[0]
    t_tail = tm // nq
    t_main = tm - t_tail

    def body(dq_ref, dkm_ref, dkt_ref, dvm_ref, dvt_ref, du_ref, c_ref, s1_ref, s2_ref, x_ref, dx1_ref, mod_ref,
             g_ref, w_ref, dx_ref, dproj_ref, red_ref, dbin_ref):
        dk_all = _interleave_groups(dkm_ref[...], dkt_ref[...])
        dv_all = _interleave_groups(dvm_ref[...], dvt_ref[...])

        @pl.when(pl.program_id(0) == 0)
        def _():
            red_ref[...] = jnp.zeros_like(red_ref)
            dbin_ref[...] = jnp.zeros_like(dbin_ref)

        c = c_ref[...]
        s1 = s1_ref[...]
        s2 = s2_ref[...]
        dqp = _rot_bwd(dq_ref[...] * (HEAD ** -0.5), jnp.tile(c, (1, 4)), jnp.tile(s1, (1, 4)), jnp.tile(s2, (1, 4)))
        dkp = _rot_bwd(_fold_dup(dk_all), c, s1, s2)
        pieces = ((0, ATTN_W, dqp), (ATTN_W, ATTN_W + KV_W, dkp),
                  (ATTN_W + KV_W, ATTN_W + 2 * KV_W, _fold_dup(dv_all)), (ATTN_W + 2 * KV_W, IN_W, du_ref[...]))
        for lo, hi, val in pieces:
            dbin_ref[0:1, lo:hi] += _colsum(val)
            dproj_ref[:, lo:hi] = val.astype(BF16)
        dh = _mm(dproj_ref[...], w_ref[...])
        xf = x_ref[...]
        rstd = _rstd(xf)
        xhat = xf * rstd
        gpre = g_ref[0:1, :]
        scale1 = 1.0 + mod_ref[1:2, :]
        dxn, s_pre = _norm_gain_bwd(dh, xhat, rstd, scale1 * gpre)
        red_ref[0:1, :] += _colsum(dh)
        red_ref[1:2, :] += gpre * s_pre
        red_ref[2:3, :] += scale1 * s_pre
        dx_ref[...] = dx1_ref[...] + dxn

    return pl.pallas_call(
        body, name="in_bwd", grid=(T // tm,),
        in_specs=[_rows(tm, ATTN_W), *[_rows(t_main, KVD_W), _rows(t_tail, KVD_W)] * 2, _rows(tm, POOL_W),
                  _rows(tm, 128), _rows(tm, 128), _rows(tm, 128), _rows(tm, D_MODEL), _rows(tm, D_MODEL),
                  _full((8, D_MODEL)), _full((8, D_MODEL)), _resident((IN_W, D_MODEL))],
        out_specs=[_rows(tm, D_MODEL), _rows(tm, IN_W), _full((8, D_MODEL)), _full((8, IN_W))],
        out_shape=[_sds((T, D_MODEL), F32), _sds((T, IN_W), BF16), _sds((8, D_MODEL), F32), _sds((8, IN_W), F32)],
        compiler_params=_cp("arbitrary", vmem=FFN_VMEM_LIMIT),
    )(dq, *dk_mt, *dv_mt, du, rc, rs1, rs2, x, dx1, mod8, g8, w_in)


def _mod_fwd(c_all, ada_w, ada_b_sh):
    tn = 512

    def body(c_ref, w_ref, b_ref, o_ref):
        cv = c_ref[...]
        ca = (cv * jax.nn.sigmoid(cv)).astype(BF16)
        o_ref[...] = _mm(ca, w_ref[...].astype(BF16)) + b_ref[...]

    return pl.pallas_call(
        body, name="mod_fwd", grid=(2, ADA_SH // tn),
        in_specs=[_full((8, D_MODEL)), pl.BlockSpec((None, D_MODEL, tn), lambda l, j: (l, 0, j)),
                  pl.BlockSpec((None, 1, tn), lambda l, j: (l, 0, j))],
        out_specs=pl.BlockSpec((None, 8, tn), lambda l, j: (l, 0, j)),
        out_shape=_sds((2, 8, ADA_SH), F32),
        compiler_params=_cp("parallel", "parallel"),
    )(c_all, ada_w, ada_b_sh)


def _ada_wgrad(c_all_t, dmod_sh):
    tn = 512

    def body(c_ref, d_ref, o_ref):
        cv = c_ref[...]
        ca = cv * jax.nn.sigmoid(cv)
        o_ref[...] = jnp.dot(ca, d_ref[...], preferred_element_type=F32, precision=lax.Precision.HIGHEST)

    return pl.pallas_call(
        body, name="ada_wgrad", grid=(2, ADA_SH // tn),
        in_specs=[_full((D_MODEL, 8)), pl.BlockSpec((None, 8, tn), lambda l, j: (l, 0, j))],
        out_specs=pl.BlockSpec((None, D_MODEL, tn), lambda l, j: (l, 0, j)),
        out_shape=_sds((2, D_MODEL, ADA_SH), F32),
        compiler_params=_cp("parallel", "parallel"),
    )(c_all_t, dmod_sh)


def _sum_devices(g):
    R = g.shape[1]

    def body(g_ref, o_ref):
        acc = g_ref[0]
        for d in range(1, N_DEV):
            acc = acc + g_ref[d]
        o_ref[...] = acc

    return pl.pallas_call(
        body, name="sum_devices", grid=(1,),
        in_specs=[_full((N_DEV, R, 128))], out_specs=_full((R, 128)), out_shape=_sds((R, 128), F32),
        compiler_params=_cp("arbitrary"),
    )(g)


def _adamw(w, g, m, v, name):
    R, C = w.shape
    tr = R
    for cand in (256, 128, 64, 32, 16, 8):
        if R % cand == 0 and cand * C * 4 <= 2 * 1024 * 1024:
            tr = cand
            break

    def body(w_ref, g_ref, m_ref, v_ref, d_ref, nm_ref, nv_ref):
        gv = g_ref[...]
        mn = ADAM_B1 * m_ref[...] + (1.0 - ADAM_B1) * gv
        vn = ADAM_B2 * v_ref[...] + (1.0 - ADAM_B2) * (gv * gv)
        m_hat = mn / (1.0 - ADAM_B1 ** ADAM_STEP)
        v_hat = vn / (1.0 - ADAM_B2 ** ADAM_STEP)
        d_ref[...] = -ADAM_LR * (m_hat / (jnp.sqrt(v_hat) + ADAM_EPS) + ADAM_WD * w_ref[...])
        nm_ref[...] = mn
        nv_ref[...] = vn

    spec = pl.BlockSpec((tr, C), lambda i: (i, 0))
    out = _sds((R, C), F32)
    return pl.pallas_call(
        body, name=name, grid=(R // tr,),
        in_specs=[spec] * 4, out_specs=[spec] * 3, out_shape=[out] * 3,
        compiler_params=_cp("parallel"),
    )(w, g, m, v)


def _adamw_nd(w, g, m, v, name):
    shape = w.shape
    if w.ndim == 2 and shape[1] < 128:
        view = (1, shape[0] * shape[1])
    else:
        view = (-1, shape[-1])
    outs = _adamw(*[t.reshape(view) for t in (w, g, m, v)], name=name)
    return [o.reshape(shape) for o in outs]


def _coords():
    return lax.axis_index("x"), lax.axis_index("y"), lax.axis_index("c")


def _other_chips(x, y):
    return [(1 - x, y), (x, 1 - y), (1 - x, 1 - y)]


def _allgather8(blk, name):
    m_per, n = blk.shape

    def body(x_ref, out_ref, send_sems, recv_sems, local_sem):
        x, y, c = _coords()
        me, sibling = (x, y, c), (x, y, 1 - c)
        chips = _other_chips(x, y)

        def rows(px, py, pc):
            return out_ref.at[pl.ds((4 * px + 2 * py + pc) * m_per, m_per), :]

        def copy(k, block, to, src=None):
            return pltpu.make_async_remote_copy(
                src_ref=rows(*block) if src is None else src, dst_ref=rows(*block),
                send_sem=send_sems.at[k], recv_sem=recv_sems.at[k], device_id=to, device_id_type=MESH)

        mine = pltpu.make_async_copy(x_ref, rows(*me), local_sem)
        mine.start()
        first = [copy(0, me, sibling, src=x_ref)]
        first += [copy(1 + j, me, (*chip, c), src=x_ref) for j, chip in enumerate(chips)]
        for cp in first:
            cp.start()
        passed = [copy(4 + j, (*chip, c), sibling) for j, chip in enumerate(chips)]
        for j, chip in enumerate(chips):
            copy(1 + j, (*chip, c), me).wait_recv()
            passed[j].start()
        copy(0, sibling, me).wait_recv()
        for j, chip in enumerate(chips):
            copy(4 + j, (*chip, 1 - c), me).wait_recv()
        for cp in first + passed:
            cp.wait_send()
        mine.wait()

    return pl.pallas_call(
        body, name=name,
        out_shape=_sds((N_DEV * m_per, n), blk.dtype),
        in_specs=[pl.BlockSpec(memory_space=pltpu.VMEM)],
        out_specs=pl.BlockSpec(memory_space=pltpu.VMEM),
        scratch_shapes=[pltpu.SemaphoreType.DMA((7,)), pltpu.SemaphoreType.DMA((7,)), pltpu.SemaphoreType.DMA],
        compiler_params=pltpu.CompilerParams(vmem_limit_bytes=VMEM_LIMIT),
    )(blk)


def _row_tile(r, n):
    for cand in range(r, 15, -16):
        if r % cand == 0 and cand % 16 == 0 and cand * n * 4 <= 2 * 1024 * 1024:
            return cand
    return r


def _cast_slot(w, chip, name):
    r, n = w.shape
    tr = _row_tile(r, n)

    def body(chip_ref, w_ref, o_ref):
        o_ref[...] = w_ref[...].astype(BF16)

    grid_spec = pltpu.PrefetchScalarGridSpec(
        num_scalar_prefetch=1, grid=(r // tr,),
        in_specs=[pl.BlockSpec((tr, n), lambda i, ch: (i, 0))],
        out_specs=pl.BlockSpec((None, tr, n), lambda i, ch: (ch[0], i, 0)))
    return pl.pallas_call(
        body, name=name, grid_spec=grid_spec, out_shape=_sds((N_SHARD, r, n), BF16),
        compiler_params=_cp("arbitrary"),
    )(chip, w)


def _join_halves(tots, name):
    nt = len(tots)
    hom = [pl.BlockSpec(memory_space=pl.ANY)] * nt

    def body(*refs):
        outs = refs[nt:2 * nt]
        send_sems, recv_sems = refs[2 * nt:]
        x, y, c = _coords()
        sibling = (x, y, 1 - c)
        cps = []
        for t in range(nt):
            cp = pltpu.make_async_remote_copy(
                src_ref=outs[t].at[c], dst_ref=outs[t].at[c],
                send_sem=send_sems.at[t], recv_sem=recv_sems.at[t], device_id=sibling, device_id_type=MESH)
            cp.start()
            cps.append(cp)
        for t in range(nt):
            pltpu.make_async_remote_copy(
                src_ref=outs[t].at[c], dst_ref=outs[t].at[1 - c],
                send_sem=send_sems.at[t], recv_sem=recv_sems.at[t], device_id=sibling, device_id_type=MESH).wait_recv()
        for cp in cps:
            cp.wait_send()

    return pl.pallas_call(
        body, name=name,
        out_shape=[_sds(t.shape, t.dtype) for t in tots],
        in_specs=hom, out_specs=hom,
        input_output_aliases={t: t for t in range(nt)},
        scratch_shapes=[pltpu.SemaphoreType.DMA((nt,)), pltpu.SemaphoreType.DMA((nt,))],
    )(*tots)


def _pair_sum(g, recv, core, chip, name):
    _, _, r, n = g.shape
    tr = _row_tile(r, n)

    def body(core_ref, chip_ref, g_ref, r_ref, sb_ref, own_ref):
        tot = g_ref[...] + r_ref[...]
        sb_ref[...] = tot.astype(BF16)

        @pl.when(pl.program_id(1) == chip_ref[0])
        def _():
            own_ref[...] = tot

    grid_spec = pltpu.PrefetchScalarGridSpec(
        num_scalar_prefetch=2, grid=(r // tr, N_SHARD),
        in_specs=[pl.BlockSpec((None, None, tr, n), lambda i, s, co, ch: (s, co[0], i, 0)),
                  pl.BlockSpec((None, tr, n), lambda i, s, co, ch: (s, i, 0))],
        out_specs=[pl.BlockSpec((None, tr, n), lambda i, s, co, ch: (s, i, 0)),
                   pl.BlockSpec((tr, n), lambda i, s, co, ch: (i, 0))])
    return pl.pallas_call(
        body, name=name, grid_spec=grid_spec,
        out_shape=[_sds((N_SHARD, r, n), BF16), _sds((r, n), F32)],
        compiler_params=_cp("arbitrary", "arbitrary"),
    )(core, chip, g, recv)


def _chip_sum(own, recv, core, name):
    r, n = own.shape
    tr = _row_tile(r, n)

    def body(core_ref, o_ref, r_ref, t_ref):
        acc = o_ref[...]
        for j in range(3):
            acc = acc + r_ref[j].astype(F32)
        t_ref[...] = acc

    grid_spec = pltpu.PrefetchScalarGridSpec(
        num_scalar_prefetch=1, grid=(r // tr,),
        in_specs=[pl.BlockSpec((tr, n), lambda i, co: (i, 0)), pl.BlockSpec((3, tr, n), lambda i, co: (0, i, 0))],
        out_specs=pl.BlockSpec((None, tr, n), lambda i, co: (co[0], i, 0)))
    return pl.pallas_call(
        body, name=name, grid_spec=grid_spec, out_shape=_sds((2, r, n), F32),
        compiler_params=_cp("arbitrary"),
    )(core, own, recv)


_HBM = pl.BlockSpec(memory_space=pltpu.HBM)
_SEM = pl.BlockSpec(memory_space=pltpu.SEMAPHORE)
_EFFECT = pltpu.SideEffectType.DATAFLOW_SIDE_EFFECTING


def _ici_copies(srcs, dsts, send_sems, recv_sems, send_view, recv_view):
    x, y, c = _coords()
    out = []
    if send_view is None:
        for t in range(len(srcs)):
            r = srcs[t].shape[1] // 2
            out.append(pltpu.make_async_remote_copy(
                src_ref=srcs[t].at[:, pl.ds((1 - c) * r, r)], dst_ref=dsts[t],
                send_sem=send_sems.at[3 * t], recv_sem=recv_sems.at[3 * t],
                device_id=(x, y, 1 - c), device_id_type=MESH))
        return out
    for t in range(len(srcs)):
        for j, chip in enumerate(_other_chips(x, y)):
            out.append(pltpu.make_async_remote_copy(
                src_ref=send_view(srcs[t], chip, j, (x, y), c), dst_ref=recv_view(dsts[t], chip, j, (x, y), c),
                send_sem=send_sems.at[3 * t + j], recv_sem=recv_sems.at[3 * t + j],
                device_id=(*chip, c), device_id_type=MESH))
    return out


def _ici_start(srcs, dsts, after, send_view, recv_view, name):
    nt = len(srcs)
    inplace = dsts is None
    nbuf = nt if inplace else 2 * nt

    def body(*refs):
        send_sems, recv_sems = refs[nbuf + 1], refs[nbuf + 2]
        s_out = refs[nbuf + 3:nbuf + 3 + nt]
        d_out = s_out if inplace else refs[nbuf + 3 + nt:nbuf + 3 + 2 * nt]
        token = refs[-1]
        for cp in _ici_copies(s_out, d_out, send_sems, recv_sems, send_view, recv_view):
            cp.start()
        token[...] = jnp.zeros_like(token)

    bufs = list(srcs) + ([] if inplace else list(dsts))
    res = pl.pallas_call(
        body, name=name,
        out_shape=(pltpu.SemaphoreType.DMA((3 * nt,)), pltpu.SemaphoreType.DMA((3 * nt,)),
                   *[pltpu.HBM(b.shape, b.dtype) for b in bufs], _sds((8, 128), F32)),
        in_specs=[_HBM] * nbuf + [pl.BlockSpec(memory_space=pl.ANY)],
        out_specs=(_SEM, _SEM, *[_HBM] * nbuf, pl.BlockSpec(memory_space=pltpu.VMEM)),
        input_output_aliases={i: 2 + i for i in range(nbuf)},
        compiler_params=pltpu.CompilerParams(has_side_effects=_EFFECT),
    )(*[pltpu.with_memory_space_constraint(b, pltpu.HBM) for b in bufs], after)
    send_sems, recv_sems = res[0], res[1]
    s_thru = list(res[2:2 + nt])
    d_thru = s_thru if inplace else list(res[2 + nt:2 + 2 * nt])
    return send_sems, recv_sems, s_thru, d_thru, res[-1]


def _ici_wait(send_sems, recv_sems, srcs, dsts, after, send_view, recv_view, name):
    nt = len(srcs)
    inplace = dsts is None
    nbuf = nt if inplace else 2 * nt

    def body(*refs):
        send_ref, recv_ref = refs[nbuf], refs[nbuf + 1]
        s_out = refs[nbuf + 3:nbuf + 3 + nt]
        d_out = s_out if inplace else refs[nbuf + 3 + nt:nbuf + 3 + 2 * nt]
        for cp in _ici_copies(s_out, d_out, send_ref, recv_ref, send_view, recv_view):
            cp.wait_send()
            cp.wait_recv()

    bufs = list(srcs) + ([] if inplace else list(dsts))
    res = pl.pallas_call(
        body, name=name,
        out_shape=tuple(pltpu.HBM(b.shape, b.dtype) for b in bufs),
        in_specs=[_HBM] * nbuf + [_SEM, _SEM, pl.BlockSpec(memory_space=pl.ANY)],
        out_specs=tuple([_HBM] * nbuf),
        input_output_aliases={i: i for i in range(nbuf)},
        compiler_params=pltpu.CompilerParams(has_side_effects=_EFFECT),
    )(*bufs, send_sems, recv_sems, after)
    return list(res[:nt]) if inplace else (list(res[:nt]), list(res[nt:]))


def _w_half(buf, chip, c):
    r = buf.shape[1] // 2
    return buf.at[2 * chip[0] + chip[1], pl.ds(c * r, r)]


def _ag_send_view(buf, chip, j, me, c):
    return _w_half(buf, me, c)


def _ag_recv_view(buf, chip, j, me, c):
    return _w_half(buf, me, c)


def _rs_send_view(buf, chip, j, me, c):
    return buf.at[2 * chip[0] + chip[1]]


def _rs_recv_view(buf, chip, j, me, c):
    return buf.at[j]


def _ag_forward(bufs, name):
    nt = len(bufs)
    hom = [pl.BlockSpec(memory_space=pl.ANY)] * nt

    def body(*refs):
        outs = refs[nt:2 * nt]
        send_sems, recv_sems = refs[2 * nt:]
        x, y, c = _coords()
        sibling = (x, y, 1 - c)
        chips = _other_chips(x, y)

        def copy(t, j, hc):
            blk = _w_half(outs[t], chips[j], hc)
            return pltpu.make_async_remote_copy(
                src_ref=blk, dst_ref=blk, send_sem=send_sems.at[t, j], recv_sem=recv_sems.at[t, j],
                device_id=sibling, device_id_type=MESH)

        started = [copy(t, j, c) for t in range(nt) for j in range(3)]
        for cp in started:
            cp.start()
        for t in range(nt):
            for j in range(3):
                copy(t, j, 1 - c).wait_recv()
        for cp in started:
            cp.wait_send()

    return pl.pallas_call(
        body, name=name,
        out_shape=[_sds(b.shape, b.dtype) for b in bufs],
        in_specs=hom, out_specs=hom,
        input_output_aliases={t: t for t in range(nt)},
        scratch_shapes=[pltpu.SemaphoreType.DMA((nt, 3)), pltpu.SemaphoreType.DMA((nt, 3))],
    )(*bufs)


def _rs_swap_begin(grads, after, tag):
    land = [lax.empty((N_SHARD, g.shape[1] // 2, g.shape[2]), g.dtype) for g in grads]
    send_sems, recv_sems, s_thru, d_thru, token = _ici_start(grads, land, after, None, None, name="rs_swapgo_" + tag)
    return dict(sems=(send_sems, recv_sems), grads=s_thru, land=d_thru, tag=tag), token


def _rs_scatter_begin(swap, after):
    tag = swap["tag"]
    x, y, c = _coords()
    core = jnp.reshape(c, (1,)).astype(jnp.int32)
    chip = jnp.reshape(2 * x + y, (1,)).astype(jnp.int32)
    grads, recv = _ici_wait(*swap["sems"], swap["grads"], swap["land"], after, None, None, name="rs_swapend_" + tag)
    sums, owns = [], []
    for t, (g, rv) in enumerate(zip(grads, recv)):
        r = g.shape[1] // 2
        sb, own = _pair_sum(g.reshape(N_SHARD, 2, r, g.shape[2]), rv, core, chip, name=f"rs_pair_{tag}_{t}")
        sums.append(sb)
        owns.append(own)
    land = [lax.empty((3,) + s.shape[1:], s.dtype) for s in sums]
    send_sems, recv_sems, s_thru, d_thru, token = _ici_start(
        sums, land, after, _rs_send_view, _rs_recv_view, name="rs_start_" + tag)
    return dict(sems=(send_sems, recv_sems), sums=s_thru, land=d_thru, owns=owns, core=core, tag=tag), token


def _rs_end(state, after):
    tag = state["tag"]
    _, got = _ici_wait(*state["sems"], state["sums"], state["land"], after, _rs_send_view, _rs_recv_view,
                       name="rs_wait_" + tag)
    tots = [_chip_sum(o, gt, state["core"], name=f"rs_chip_{tag}_{t}")
            for t, (o, gt) in enumerate(zip(state["owns"], got))]
    full = _join_halves(tots, name="rs_join_" + tag)
    return [f.reshape(2 * f.shape[1], f.shape[2]) for f in full]


def _rope_lane_table():
    d = jnp.arange(128) % HEAD
    inv_freq = ROPE_THETA ** (-jnp.arange(0, ROT, 2, dtype=F32) / ROT)
    rot = d < ROT
    rows = [jnp.where(rot, inv_freq[d % (ROT // 2)], 0.0), rot.astype(F32),
            (d < ROT // 2).astype(F32), jnp.logical_and(d >= ROT // 2, rot).astype(F32)]
    return jnp.concatenate([jnp.stack(rows), jnp.zeros((4, 128), F32)], axis=0)


def _pad8(rows):
    return jnp.concatenate([rows, jnp.zeros((8 - rows.shape[0], rows.shape[1]), F32)], axis=0)


def kernel(x, c, positions, ada_w, ada_b, w_in, b_in, sinks, pool_w, pool_scale, w_out, w_gate, w_up, w_down, g_pre_mix, g_post_mix, g_pre_ffn, g_post_ffn, loss_target, m_ada_w, m_ada_b, m_w_in, m_b_in, m_sinks, m_pool_w, m_pool_scale, m_w_out, m_w_gate, m_w_up, m_w_down, m_g_pre_mix, m_g_post_mix, m_g_pre_ffn, m_g_post_ffn, v_ada_w, v_ada_b, v_w_in, v_b_in, v_sinks, v_pool_w, v_pool_scale, v_w_out, v_w_gate, v_w_up, v_w_down, v_g_pre_mix, v_g_post_mix, v_g_pre_ffn, v_g_post_ffn):
    T = x.shape[1]
    n_layers = ada_w.shape[0]
    ax, ay, ac = _coords()
    my_dev = 4 * ax + 2 * ay + ac
    my_chip = 2 * ax + ay
    x0 = x.reshape(T, D_MODEL)
    target = loss_target.reshape(T, D_MODEL)

    c_all = _allgather8(c.reshape(8, 128), name="ag_c").reshape(N_DEV, D_MODEL)
    ada_b_sh = lax.dynamic_slice_in_dim(ada_b, my_chip * ADA_SH, ADA_SH, axis=1).reshape(n_layers, 1, ADA_SH)
    mod_part = _mod_fwd(c_all, ada_w, ada_b_sh)
    mod_all = _allgather8(mod_part.reshape(n_layers * 8, ADA_SH), name="ag_mod")
    mod_all = mod_all.reshape(N_DEV, n_layers, 8, ADA_SH)[0::2]
    mod_mine = lax.dynamic_index_in_dim(mod_all, my_dev, axis=2, keepdims=False)
    mod = jnp.transpose(mod_mine, (1, 0, 2)).reshape(n_layers, 6, D_MODEL)

    chip1 = jnp.reshape(my_chip, (1,)).astype(jnp.int32)

    def tr(t):
        return jnp.transpose(t, (0, 2, 1))

    w_in_t, w_gate_t, w_up_t = tr(w_in), tr(w_gate), tr(w_up)

    def cast_layer(l):
        return [_cast_slot(w[l], chip1, name=f"cast_{nm}{l}")
                for nm, w in (("w_in", w_in_t), ("w_out", w_out), ("w_gate", w_gate_t), ("w_up", w_up_t),
                              ("w_down", w_down))]

    def as_operands(bufs):
        gin, gout, gg, gu, gd = bufs
        return (gin.reshape(IN_W, D_MODEL), gout.reshape(D_MODEL, D_MODEL), gg.reshape(D_FF, D_MODEL),
                gu.reshape(D_FF, D_MODEL), gd.reshape(D_FF, D_MODEL))

    bufs0 = cast_layer(0)
    in_send, in_recv, in_bufs, _, in_token = _ici_start(
        bufs0[:1], None, mod, _ag_send_view, _ag_recv_view, name="ag_start_0_in")
    pos_b = jnp.broadcast_to(positions.reshape(T, 1), (T, 128))
    rc, rs1, rs2 = _rope_tables(pos_b, _rope_lane_table() + in_token[0, 0])
    arrived = _ici_wait(in_send, in_recv, in_bufs, None, rc, _ag_send_view, _ag_recv_view, name="ag_wait_0_in")
    win0 = _ag_forward(arrived, name="ag_fwd_0_in")
    rest_send, rest_recv, rest_bufs, _, ag_token = _ici_start(
        bufs0[1:], None, win0[0], _ag_send_view, _ag_recv_view, name="ag_start_0")
    weights = [None] * n_layers

    saved = []
    xl = x0
    for l in range(n_layers):
        mod8 = _pad8(mod[l])
        if l + 1 < n_layers:
            ag_send, ag_recv, ag_bufs, _, ag_token = _ici_start(
                cast_layer(l + 1), None, ag_token, _ag_send_view, _ag_recv_view, name=f"ag_start_{l + 1}")
        if l == 0 or l + 1 < n_layers:
            mod8 = mod8 + ag_token[0, 0]
        g8 = _pad8(jnp.stack([g_pre_mix[l], g_post_mix[l], g_pre_ffn[l], g_post_ffn[l]]))
        sink_b = jnp.broadcast_to(sinks[l][:, None], (N_HEADS, 128))
        psc = pool_scale[l].reshape(1, POOL_W)
        win = win0[0].reshape(IN_W, D_MODEL) if l == 0 else weights[l][0]
        h, q, k, v, u = _fwd_in(xl, mod8, g8, win, b_in[l].reshape(1, IN_W), rc, rs1, rs2)
        attn, lse = _attn_fwd(q, k, v, sink_b)
        pool, pooled = _pool_fwd(u, pool_w[l], psc)
        if l == 0:
            arrived = _ici_wait(rest_send, rest_recv, rest_bufs, None, pool, _ag_send_view, _ag_recv_view,
                                name="ag_wait_0")
            weights[0] = as_operands(win0 + _ag_forward(arrived, name="ag_fwd_0"))
        win, wout, wg, wu, wd = weights[l]
        if l + 1 < n_layers:
            mix, x1, h2, act, ga, gb, f, x2 = _out_ffn_fwd(attn, pool, xl, wout, mod8, g8, wg, wu, wd)
        else:
            mix, x1, h2, act, ga, gb, f, x2, loss_tile = _out_ffn_fwd(attn, pool, xl, wout, mod8, g8, wg, wu, wd,
                                                                      target=target)
        saved.append(dict(x=xl, h=h, q=q, k=k, v=v, lse=lse, attn=attn, pool=pool, pooled=pooled, mix=mix,
                          x1=x1, h2=h2, act=act, ga=ga, gb=gb, f=f, mod8=mod8, g8=g8, sink_b=sink_b, psc=psc))
        xl = x2
        if l + 1 < n_layers:
            arrived = _ici_wait(ag_send, ag_recv, ag_bufs, None, x2, _ag_send_view, _ag_recv_view,
                                name=f"ag_wait_{l + 1}")
            weights[l + 1] = as_operands(_ag_forward(arrived, name=f"ag_fwd_{l + 1}"))

    dy = xl
    loss = lax.psum(loss_tile[0, 0], ("x", "y", "c"))

    small = [None] * n_layers
    dmod_rows = [None] * n_layers
    reduced = [dict() for _ in range(n_layers)]
    att_swap = None
    dx = dy
    for l in reversed(range(n_layers)):
        s = saved[l]
        win, wout, wg, wu, wd = weights[l]
        if att_swap is not None:
            s = dict(s, mod8=s["mod8"] + att_swap[1][0, 0])
        dx1, df, da, db, red_f = _ffn_bwd(dx, s["f"], s["ga"], s["gb"], s["x1"], s["mod8"], s["g8"], wg, wu, wd)
        token = None
        if att_swap is not None:
            att_scatter = _rs_scatter_begin(att_swap[0], dx1)
            token = att_scatter[1]
        ffn_shards = (N_SHARD, FF_SH, D_MODEL)
        g_wd = _wgrad(s["act"], df, name="wgrad_down", after=token).reshape(ffn_shards)
        g_wg = _wgrad(da, s["h2"], name="wgrad_gate").reshape(ffn_shards)
        g_wu = _wgrad(db, s["h2"], name="wgrad_up").reshape(ffn_shards)
        ffn_swap = _rs_swap_begin([g_wg, g_wu, g_wd], dx1, tag=f"{l}f")
        if att_swap is not None:
            got = _rs_end(att_scatter[0], ffn_swap[1])
            reduced[l + 1].update(w_in=got[0], w_out=got[1])
        s = dict(s, mod8=s["mod8"] + ffn_swap[1][0, 0])
        dmix, dattn, dpool, red_c = _mix_bwd(dx1, s["mix"], s["mod8"], s["g8"], wout)
        g_wout = jnp.concatenate([_wgrad(s["attn"], dmix, name="wgrad_out_a"),
                                  _wgrad(s["pool"], dmix, name="wgrad_out_p")], axis=0)
        ffn_scatter = _rs_scatter_begin(ffn_swap[0], dattn)
        dq, dk_e, dk_o, dv_e, dv_o, dsink = _attn_bwd(s["q"], s["k"], s["v"], s["lse"], dattn,
                                                      s["sink_b"] + ffn_scatter[1][0:1, :])
        du, g_poolw, dpsc = _pool_bwd(dpool, s["pooled"], pool_w[l], s["psc"])
        dx, dproj, red_d, dbin = _in_bwd(dq, (dk_e, dk_o), (dv_e, dv_o), du, rc, rs1, rs2, s["x"], dx1, s["mod8"],
                                         s["g8"], win)
        g_win = _wgrad(dproj, s["h"], name="wgrad_in")
        g_win_sh = g_win.reshape(N_SHARD, IN_SH, D_MODEL)
        got = _rs_end(ffn_scatter[0], dproj)
        reduced[l].update(w_gate=got[0], w_up=got[1], w_down=got[2])
        att_swap = _rs_swap_begin([g_win_sh, g_wout.reshape(N_SHARD, OUT_SH, D_MODEL)], dx, tag=f"{l}a")
        dmod_rows[l] = jnp.concatenate([red_d[0], red_d[1], red_c[0], red_f[2], red_f[3], red_f[0]])
        small[l] = jnp.concatenate([red_d[2], red_c[1], red_f[4], red_f[1], dbin[0], dpsc[0], dsink[:, 0],
                                    jnp.zeros((120,), F32), g_poolw.reshape(-1)])
    grad_x = dx.reshape(1, T, D_MODEL)

    per_layer = small[0].shape[0]
    rows_small = n_layers * per_layer // 128
    rows_mod = n_layers * 6 * D_MODEL // 128
    rows_pad = -(rows_small + rows_mod) % 8
    pack = jnp.concatenate(small + dmod_rows + [jnp.zeros((rows_pad * 128,), F32)]).reshape(-1, 128)
    pack = pack + att_swap[1][0, 0]
    gathered = _allgather8(pack, name="ag_small").reshape(N_DEV, pack.shape[0], 128)
    summed = _sum_devices(gathered)
    att_scatter = _rs_scatter_begin(att_swap[0], summed)
    small_sum = summed[:rows_small].reshape(n_layers, per_layer)
    o = 0
    small_g = {}
    for nm, width in (("g_pre_mix", D_MODEL), ("g_post_mix", D_MODEL), ("g_pre_ffn", D_MODEL),
                      ("g_post_ffn", D_MODEL), ("b_in", IN_W), ("pool_scale", POOL_W), ("sinks", 128),
                      ("pool_w", 4 * 128 * 128)):
        small_g[nm] = small_sum[:, o:o + width]
        o += width
    small_g["sinks"] = small_g["sinks"][:, :N_HEADS]
    small_g["pool_w"] = small_g["pool_w"].reshape(n_layers, 4, 128, 128)
    small_g["ada_b"] = summed[rows_small:rows_small + rows_mod].reshape(n_layers, 6 * D_MODEL)
    dmod_all = gathered[:, rows_small:rows_small + rows_mod].reshape(N_DEV, n_layers, N_SHARD, ADA_SH)
    dmod_sh = lax.dynamic_index_in_dim(dmod_all, my_chip, axis=2, keepdims=False)
    g_ada_w = _ada_wgrad(jnp.transpose(c_all), jnp.transpose(dmod_sh, (1, 0, 2)))

    grads = dict(ada_w=g_ada_w, ada_b=small_g["ada_b"], b_in=small_g["b_in"], sinks=small_g["sinks"],
                 pool_w=small_g["pool_w"], pool_scale=small_g["pool_scale"], g_pre_mix=small_g["g_pre_mix"],
                 g_post_mix=small_g["g_post_mix"], g_pre_ffn=small_g["g_pre_ffn"], g_post_ffn=small_g["g_post_ffn"])
    params = dict(ada_w=(ada_w, m_ada_w, v_ada_w), ada_b=(ada_b, m_ada_b, v_ada_b), w_in=(w_in, m_w_in, v_w_in),
                  b_in=(b_in, m_b_in, v_b_in), sinks=(sinks, m_sinks, v_sinks), pool_w=(pool_w, m_pool_w, v_pool_w),
                  pool_scale=(pool_scale, m_pool_scale, v_pool_scale), w_out=(w_out, m_w_out, v_w_out),
                  w_gate=(w_gate, m_w_gate, v_w_gate), w_up=(w_up, m_w_up, v_w_up),
                  w_down=(w_down, m_w_down, v_w_down), g_pre_mix=(g_pre_mix, m_g_pre_mix, v_g_pre_mix),
                  g_post_mix=(g_post_mix, m_g_post_mix, v_g_post_mix), g_pre_ffn=(g_pre_ffn, m_g_pre_ffn, v_g_pre_ffn),
                  g_post_ffn=(g_post_ffn, m_g_post_ffn, v_g_post_ffn))
    names = list(params)
    updates = {nm: _adamw_nd(*params[nm][:1], grads[nm], *params[nm][1:], name="adamw_" + nm) for nm in grads}

    got = _rs_end(att_scatter[0], updates["ada_w"][0])
    reduced[0].update(w_in=got[0], w_out=got[1])
    for nm in ("w_in", "w_out", "w_gate", "w_up", "w_down"):
        g = jnp.stack([reduced[l][nm] for l in range(n_layers)])
        if nm in ("w_in", "w_gate", "w_up"):
            upd = _adamw_nd(tr(params[nm][0]), g, tr(params[nm][1]), tr(params[nm][2]), name="adamw_" + nm)
            grads[nm], updates[nm] = tr(g), [tr(u) for u in upd]
        else:
            grads[nm], updates[nm] = g, _adamw_nd(params[nm][0], g, *params[nm][1:], name="adamw_" + nm)
    return (loss, grad_x, *[grads[nm] for nm in names], *[updates[nm][0] for nm in names],
            *[updates[nm][1] for nm in names], *[updates[nm][2] for nm in names])
```

```python
import jax
import jax.numpy as jnp
from jax import lax
from jax.experimental import pallas as pl
from jax.experimental.pallas import tpu as pltpu

F32 = jnp.float32
BF16 = jnp.bfloat16
MESH = pl.DeviceIdType.MESH

D_MODEL = 1024
ATTN_W = 512
KV_W = 128
KVD_W = 256
POOL_W = 512
IN_W = 1280
D_FF = 2816
N_SHARD = 4
FF_SH = D_FF // N_SHARD
IN_SH = IN_W // N_SHARD
OUT_SH = D_MODEL // N_SHARD
ADA_SH = 6 * D_MODEL // N_SHARD
HEAD = 64
N_HEADS = 8
GROUP = 4
BLK = 128
POOL_WINDOWS = (2, 4, 8, 16)
HALO = 16
ROT = 16
ROPE_THETA = 500000.0
EPS = 1e-6
NEG_INF = -1e30
N_DEV = 8

ADAM_LR = 0.001
ADAM_B1 = 0.9
ADAM_B2 = 0.999
ADAM_EPS = 1e-08
ADAM_WD = 0.01
ADAM_STEP = 10

VMEM_LIMIT = 48 * 1024 * 1024
FFN_VMEM_LIMIT = 60 * 1024 * 1024
ATTN_FWD_BLOCKS = 8
ATTN_BWD_BLOCKS = 8
WGRAD_TOKENS = 2048


def _cp(*sem, vmem=VMEM_LIMIT):
    return pltpu.CompilerParams(dimension_semantics=sem, vmem_limit_bytes=vmem)


def _full(shape):
    nd = len(shape)
    return pl.BlockSpec(shape, lambda *_: (0,) * nd)


def _resident(shape):
    nd = len(shape)
    return pl.BlockSpec(shape, lambda *_: (0,) * nd, pipeline_mode=pl.Buffered(1))


def _rows(tm, ncol):
    return pl.BlockSpec((tm, ncol), lambda i: (i, 0))


def _sds(shape, dtype):
    return jax.ShapeDtypeStruct(shape, dtype)


def _nt(a, b):
    return lax.dot_general(a, b, (((1,), (1,)), ((), ())), preferred_element_type=F32)


def _tn(a, b):
    return lax.dot_general(a, b, (((0,), (0,)), ((), ())), preferred_element_type=F32)


def _mm(a, b):
    return jnp.dot(a, b, preferred_element_type=F32)


def _rstd(x):
    return lax.rsqrt(jnp.mean(x * x, axis=-1, keepdims=True) + EPS)


def _colsum(x):
    return jnp.sum(x, axis=0, keepdims=True)


def _norm_gain_bwd(dy, xhat, rstd, gain):
    p = dy * xhat
    dx = rstd * (dy * gain - xhat * jnp.mean(p * gain, axis=-1, keepdims=True))
    return dx, _colsum(p)


def _rope_tables(pos_b, lane_tab):
    T = pos_b.shape[0]
    tm = min(T, 1024)

    def body(pos_ref, tab_ref, c_ref, s1_ref, s2_ref):
        ang = pos_ref[...].astype(F32) * tab_ref[0:1, :]
        cs = jnp.cos(ang)
        sn = jnp.sin(ang)
        m_rot = tab_ref[1:2, :]
        c_ref[...] = cs * m_rot + (1.0 - m_rot)
        s1_ref[...] = -sn * tab_ref[2:3, :]
        s2_ref[...] = sn * tab_ref[3:4, :]

    out = _sds((T, 128), F32)
    return pl.pallas_call(
        body, name="rope_tables", grid=(T // tm,),
        in_specs=[_rows(tm, 128), _full((8, 128))],
        out_specs=[_rows(tm, 128)] * 3, out_shape=[out] * 3,
        compiler_params=_cp("parallel"),
    )(pos_b, lane_tab)


def _rot_fwd(t, c, s1, s2):
    w = t.shape[-1]
    return t * c + pltpu.roll(t, w - 8, 1) * s1 + pltpu.roll(t, 8, 1) * s2


def _rot_bwd(d, c, s1, s2):
    w = d.shape[-1]
    return d * c + pltpu.roll(d * s1, 8, 1) + pltpu.roll(d * s2, w - 8, 1)


def _store_dup(ref, t):
    low = lax.broadcasted_iota(jnp.int32, t.shape, 1) < HEAD
    sw = pltpu.roll(t, HEAD, 1)
    ref[:, 0:128] = jnp.where(low, t, sw).astype(BF16)
    ref[:, 128:256] = jnp.where(low, sw, t).astype(BF16)


def _fold_dup(d):
    low = lax.broadcasted_iota(jnp.int32, (d.shape[0], 128), 1) < HEAD
    d0 = d[:, 0:128]
    d1 = d[:, 128:256]
    return jnp.where(low, d0 + pltpu.roll(d0, HEAD, 1), d1 + pltpu.roll(d1, HEAD, 1))


def _fwd_in(x, mod8, g8, w_in, b_in, rc, rs1, rs2):
    T = x.shape[0]
    tm = min(T, 1024)

    def body(x_ref, mod_ref, g_ref, w_ref, b_ref, c_ref, s1_ref, s2_ref,
             h_ref, q_ref, k_ref, v_ref, u_ref):
        xf = x_ref[...]
        h = (xf * _rstd(xf) * g_ref[0:1, :]) * (1.0 + mod_ref[1:2, :]) + mod_ref[0:1, :]
        hb = h.astype(BF16)
        h_ref[...] = hb
        c = c_ref[...]
        s1 = s1_ref[...]
        s2 = s2_ref[...]
        proj = _nt(hb, w_ref[...]) + b_ref[...]
        q = _rot_fwd(proj[:, 0:ATTN_W], jnp.tile(c, (1, 4)), jnp.tile(s1, (1, 4)), jnp.tile(s2, (1, 4)))
        q_ref[...] = (q * (HEAD ** -0.5)).astype(BF16)
        _store_dup(k_ref, _rot_fwd(proj[:, ATTN_W:ATTN_W + KV_W], c, s1, s2))
        _store_dup(v_ref, proj[:, ATTN_W + KV_W:ATTN_W + 2 * KV_W])
        u_ref[...] = proj[:, ATTN_W + 2 * KV_W:IN_W]

    return pl.pallas_call(
        body, name="fwd_in", grid=(T // tm,),
        in_specs=[_rows(tm, D_MODEL), _full((8, D_MODEL)), _full((8, D_MODEL)),
                  _resident((IN_W, D_MODEL)), _full((1, IN_W)),
                  _rows(tm, 128), _rows(tm, 128), _rows(tm, 128)],
        out_specs=[_rows(tm, D_MODEL), _rows(tm, ATTN_W), _rows(tm, KVD_W), _rows(tm, KVD_W), _rows(tm, POOL_W)],
        out_shape=[_sds((T, D_MODEL), BF16), _sds((T, ATTN_W), BF16), _sds((T, KVD_W), BF16),
                   _sds((T, KVD_W), BF16), _sds((T, POOL_W), F32)],
        compiler_params=_cp("parallel"),
    )(x, mod8, g8, w_in, b_in, rc, rs1, rs2)


def _band_mask(n):
    kk = lax.broadcasted_iota(jnp.int32, (2 * BLK, BLK), 0)
    qi = lax.broadcasted_iota(jnp.int32, (2 * BLK, BLK), 1)
    first = jnp.where(n > 0, 0, 2 * BLK)
    in_prev = jnp.logical_and(kk < BLK, kk > qi + first)
    in_cur = jnp.logical_and(kk >= BLK, (kk - BLK) <= qi)
    one = jnp.logical_or(in_prev, in_cur)
    return jnp.concatenate([one] * GROUP, axis=1)


def _head_row(ref, j, base=0):
    return jnp.concatenate([ref[base + GROUP * j + r:base + GROUP * j + r + 1, :] for r in range(GROUP)], axis=1)


def _stack_heads(x_ref, j, rows=slice(None)):
    low = lax.broadcasted_iota(jnp.int32, (BLK, 128), 1) < HEAD
    parts = []
    for gp in (2 * j, 2 * j + 1):
        x2 = x_ref[rows, gp * 128:(gp + 1) * 128]
        parts.append(jnp.where(low, x2, jnp.zeros_like(x2)))
        parts.append(jnp.where(low, jnp.zeros_like(x2), x2))
    return jnp.concatenate(parts, axis=0)


def _unstack_heads(o):
    low = lax.broadcasted_iota(jnp.int32, (BLK, 128), 1) < HEAD
    return [jnp.where(low, o[0:BLK], o[BLK:2 * BLK]), jnp.where(low, o[2 * BLK:3 * BLK], o[3 * BLK:4 * BLK])]


def _attn_fwd(q, kd, vd, sink_b):
    T = q.shape[0]
    nb = T // BLK
    nq = ATTN_FWD_BLOCKS if nb % ATTN_FWD_BLOCKS == 0 else 2
    assert nb % nq == 0

    def body(q_ref, kp_ref, kc_ref, vp_ref, vc_ref, sk_ref, o_ref, lse_ref):
        for sub in range(nq):
            rows = slice(sub * BLK, (sub + 1) * BLK)
            before = slice((sub - 1) * BLK, sub * BLK)
            valid = _band_mask(nq * pl.program_id(0) + sub)
            for j in range(N_HEADS // GROUP):
                lanes = slice(j * 128, (j + 1) * 128)
                k_prev = kp_ref[:, lanes] if sub == 0 else kc_ref[before, lanes]
                v_prev = vp_ref[:, lanes] if sub == 0 else vc_ref[before, lanes]
                kcat = jnp.concatenate([k_prev, kc_ref[rows, lanes]], axis=0)
                vcat = jnp.concatenate([v_prev, vc_ref[rows, lanes]], axis=0)
                s = jnp.where(valid, _nt(kcat, _stack_heads(q_ref, j, rows)), NEG_INF)
                sk = _head_row(sk_ref, j)
                m = jnp.maximum(jnp.max(s, axis=0, keepdims=True), sk)
                p = jnp.exp(s - m)
                den = jnp.sum(p, axis=0, keepdims=True) + jnp.exp(sk - m)
                p = p * (1.0 / den)
                o = _tn(p.astype(BF16), vcat)
                o_ref[rows, 2 * j * 128:(2 * j + 2) * 128] = jnp.concatenate(_unstack_heads(o), axis=1).astype(BF16)
                lse = m + jnp.log(den)
                for r in range(GROUP):
                    h = sub * N_HEADS + GROUP * j + r
                    lse_ref[h:h + 1, :] = lse[:, r * 128:(r + 1) * 128]

    prev = lambda i: (jnp.maximum(nq * i - 1, 0), 0)
    cur = lambda i: (i, 0)
    return pl.pallas_call(
        body, name="attn_fwd", grid=(nb // nq,),
        in_specs=[pl.BlockSpec((nq * BLK, ATTN_W), cur),
                  pl.BlockSpec((BLK, KVD_W), prev), pl.BlockSpec((nq * BLK, KVD_W), cur),
                  pl.BlockSpec((BLK, KVD_W), prev), pl.BlockSpec((nq * BLK, KVD_W), cur),
                  _full((8, 128))],
        out_specs=[pl.BlockSpec((nq * BLK, ATTN_W), cur), pl.BlockSpec((nq * N_HEADS, 128), cur)],
        out_shape=[_sds((T, ATTN_W), BF16), _sds((nb * N_HEADS, 128), F32)],
        compiler_params=_cp("parallel"),
    )(q, kd, kd, vd, vd, sink_b)


def _pool_fwd(u, pool_w, pool_scale):
    T = u.shape[0]
    tm = min(T, 1024)

    def body(u_ref, w_ref, sc_ref, out_ref, pooled_ref, halo):
        i = pl.program_id(0)

        @pl.when(i == 0)
        def _():
            halo[...] = jnp.zeros_like(halo)

        ub = u_ref[...]
        ext = jnp.concatenate([halo[...], ub], axis=0)
        halo[...] = ub[tm - HALO:, :]
        tpos = (i * tm + lax.broadcasted_iota(jnp.int32, (tm, 1), 0)).astype(F32)
        for g, w in enumerate(POOL_WINDOWS):
            lanes = slice(g * 128, (g + 1) * 128)
            s = ext[:, lanes]
            sh = 1
            while sh < w:
                s = s + pltpu.roll(s, sh, 0)
                sh *= 2
            cnt = jnp.minimum(tpos + 1.0, float(w))
            pb = (s[HALO:, :] / cnt - ub[:, lanes]).astype(BF16)
            z = _mm(pb, w_ref[g].astype(BF16))
            out_ref[:, lanes] = (z * sc_ref[:, lanes]).astype(BF16)
            pooled_ref[:, lanes] = pb

    return pl.pallas_call(
        body, name="pool_fwd", grid=(T // tm,),
        in_specs=[_rows(tm, POOL_W), _full((4, 128, 128)), _full((1, POOL_W))],
        out_specs=[_rows(tm, POOL_W), _rows(tm, POOL_W)],
        out_shape=[_sds((T, POOL_W), BF16), _sds((T, POOL_W), BF16)],
        scratch_shapes=[pltpu.VMEM((HALO, POOL_W), F32)],
        compiler_params=_cp("arbitrary"),
    )(u, pool_w, pool_scale)


FF_CHUNKS = ((0, 1024), (1024, 2048), (2048, D_FF))


def _out_ffn_fwd(attn, pool, x, w_out, mod8, g8, wg, wu, wd, target=None):
    T = x.shape[0]
    tm = min(T, 256)
    last = target is not None

    def body(*refs):
        a_ref, p_ref, xin_ref, wo_ref, mod_ref, g_ref, wg_ref, wu_ref, wd_ref = refs[:9]
        t_ref = refs[9] if last else None
        mix_ref, x1_ref, h_ref, act_ref, ga_ref, gb_ref, f_ref, x2_ref = refs[9 + last:17 + last]
        mix = _mm(a_ref[...], wo_ref[0:ATTN_W, :]) + _mm(p_ref[...], wo_ref[ATTN_W:, :])
        mix_ref[...] = mix
        xf = xin_ref[...] + mod_ref[2:3, :] * (mix * _rstd(mix) * g_ref[1:2, :])
        x1_ref[...] = xf
        h = (xf * _rstd(xf) * g_ref[2:3, :]) * (1.0 + mod_ref[4:5, :]) + mod_ref[3:4, :]
        hb = h.astype(BF16)
        h_ref[...] = hb
        f = jnp.zeros((tm, D_MODEL), F32)
        for lo, hi in FF_CHUNKS:
            a = _nt(hb, wg_ref[lo:hi, :])
            b = _nt(hb, wu_ref[lo:hi, :])
            sig = jax.nn.sigmoid(a)
            sl = a * sig
            act = (sl * b).astype(BF16)
            act_ref[:, lo:hi] = act
            ga_ref[:, lo:hi] = (b * (sig * (1.0 + a * (1.0 - sig)))).astype(BF16)
            gb_ref[:, lo:hi] = sl.astype(BF16)
            f = f + _mm(act, wd_ref[lo:hi, :])
        f_ref[...] = f
        x2 = xf + mod_ref[5:6, :] * (f * _rstd(f) * g_ref[3:4, :])
        if not last:
            x2_ref[...] = x2
        else:
            loss_ref = refs[18]

            @pl.when(pl.program_id(0) == 0)
            def _():
                loss_ref[...] = jnp.zeros_like(loss_ref)

            e = x2 - t_ref[...]
            x2_ref[...] = e * (1.0 / D_MODEL)
            loss_ref[...] += 0.5 * jnp.sum(jnp.mean(e * e, axis=-1, keepdims=True), axis=0, keepdims=True)

    act_shape = _sds((T, D_FF), BF16)
    wide = _sds((T, D_MODEL), F32)
    weights = [_resident((D_FF, D_MODEL))] * 3
    return pl.pallas_call(
        body, name="out_ffn_fwd_loss" if last else "out_ffn_fwd", grid=(T // tm,),
        in_specs=[_rows(tm, ATTN_W), _rows(tm, POOL_W), _rows(tm, D_MODEL), _resident((D_MODEL, D_MODEL)),
                  _full((8, D_MODEL)), _full((8, D_MODEL)), *weights]
        + ([_rows(tm, D_MODEL)] if last else []),
        out_specs=[_rows(tm, D_MODEL), _rows(tm, D_MODEL), _rows(tm, D_MODEL), _rows(tm, D_FF), _rows(tm, D_FF),
                   _rows(tm, D_FF), _rows(tm, D_MODEL), _rows(tm, D_MODEL)] + ([_full((8, 128))] if last else []),
        out_shape=[wide, wide, _sds((T, D_MODEL), BF16), act_shape, act_shape, act_shape, wide, wide]
        + ([_sds((8, 128), F32)] if last else []),
        compiler_params=_cp("arbitrary" if last else "parallel", vmem=FFN_VMEM_LIMIT),
    )(attn, pool, x, w_out, mod8, g8, wg, wu, wd, *([target] if last else []))


def _ffn_bwd(dx2, f, ga, gb, x1, mod8, g8, wg, wu, wd):
    T = dx2.shape[0]
    tm = min(T, 256)

    def body(dx_ref, f_ref, ga_ref, gb_ref, x_ref, mod_ref, g_ref, wg_ref, wu_ref, wd_ref,
             dx1_ref, df_ref, da_ref, db_ref, red_ref):
        @pl.when(pl.program_id(0) == 0)
        def _():
            red_ref[...] = jnp.zeros_like(red_ref)

        dx = dx_ref[...]
        fv = f_ref[...]
        rstd = _rstd(fv)
        fhat = fv * rstd
        gpost = g_ref[3:4, :]
        gate = mod_ref[5:6, :]
        df, s_post = _norm_gain_bwd(dx, fhat, rstd, gate * gpost)
        red_ref[0:1, :] += gpost * s_post
        red_ref[1:2, :] += gate * s_post
        dfb = df.astype(BF16)
        df_ref[...] = dfb
        dh = jnp.zeros((tm, D_MODEL), F32)
        for lo, hi in FF_CHUNKS:
            dact = _nt(dfb, wd_ref[lo:hi, :])
            da = (dact * ga_ref[:, lo:hi].astype(F32)).astype(BF16)
            db = (dact * gb_ref[:, lo:hi].astype(F32)).astype(BF16)
            da_ref[:, lo:hi] = da
            db_ref[:, lo:hi] = db
            dh = dh + _mm(da, wg_ref[lo:hi, :]) + _mm(db, wu_ref[lo:hi, :])
        xf = x_ref[...]
        rstd1 = _rstd(xf)
        xhat = xf * rstd1
        gpre = g_ref[2:3, :]
        scale1 = 1.0 + mod_ref[4:5, :]
        dxn, s_pre = _norm_gain_bwd(dh, xhat, rstd1, scale1 * gpre)
        red_ref[2:3, :] += _colsum(dh)
        red_ref[3:4, :] += gpre * s_pre
        red_ref[4:5, :] += scale1 * s_pre
        dx1_ref[...] = dx + dxn

    act_shape = _sds((T, D_FF), BF16)
    return pl.pallas_call(
        body, name="ffn_bwd", grid=(T // tm,),
        in_specs=[_rows(tm, D_MODEL), _rows(tm, D_MODEL), _rows(tm, D_FF), _rows(tm, D_FF), _rows(tm, D_MODEL),
                  _full((8, D_MODEL)), _full((8, D_MODEL)),
                  _resident((D_FF, D_MODEL)), _resident((D_FF, D_MODEL)), _resident((D_FF, D_MODEL))],
        out_specs=[_rows(tm, D_MODEL), _rows(tm, D_MODEL), _rows(tm, D_FF), _rows(tm, D_FF), _full((8, D_MODEL))],
        out_shape=[_sds((T, D_MODEL), F32), _sds((T, D_MODEL), BF16), act_shape, act_shape, _sds((8, D_MODEL), F32)],
        compiler_params=_cp("arbitrary"),
    )(dx2, f, ga, gb, x1, mod8, g8, wg, wu, wd)


def _wgrad(a, b, name, after=None):
    T, K = a.shape
    N = b.shape[1]
    tt = min(T, WGRAD_TOKENS)
    tk = next(c for c in (1408, 640, 512, 256, 128) if K % c == 0)
    nt = T // tt
    steps = (K // tk) * nt
    ring = 3

    def body(a_hbm, b_hbm, *rest):
        o_ref, a_buf, b_buf, sems = rest[-4:]
        t = pl.program_id(1)
        s = pl.program_id(0) * nt + t

        def fetch(step):
            slot = step % ring
            rows = pl.ds((step % nt) * tt, tt)
            return (pltpu.make_async_copy(a_hbm.at[rows, pl.ds((step // nt) * tk, tk)], a_buf.at[slot], sems.at[0, slot]),
                    pltpu.make_async_copy(b_hbm.at[rows, :], b_buf.at[slot], sems.at[1, slot]))

        @pl.when(s == 0)
        def _():
            for first in range(min(ring - 1, steps)):
                for cp in fetch(first):
                    cp.start()

        @pl.when(s + ring - 1 < steps)
        def _():
            for cp in fetch(s + ring - 1):
                cp.start()

        for cp in fetch(s):
            cp.wait()

        @pl.when(t == 0)
        def _():
            o_ref[...] = jnp.zeros_like(o_ref)

        slot = s % ring
        o_ref[...] += _tn(a_buf[slot], b_buf[slot])

    extra = [] if after is None else [after]
    return pl.pallas_call(
        body, name=name, grid=(K // tk, nt),
        in_specs=[pl.BlockSpec(memory_space=pl.ANY)] * (2 + len(extra)),
        out_specs=pl.BlockSpec((tk, N), lambda i, t: (i, 0)),
        out_shape=_sds((K, N), F32),
        scratch_shapes=[pltpu.VMEM((ring, tt, tk), a.dtype), pltpu.VMEM((ring, tt, N), b.dtype),
                        pltpu.SemaphoreType.DMA((2, ring))],
        compiler_params=_cp("arbitrary", "arbitrary"),
    )(a, b, *extra)


def _mix_bwd(dx1, mix, mod8, g8, w_out):
    T = dx1.shape[0]
    tm = min(T, 1024)

    def body(dx_ref, mix_ref, mod_ref, g_ref, w_ref, dmix_ref, da_ref, dp_ref, red_ref):
        @pl.when(pl.program_id(0) == 0)
        def _():
            red_ref[...] = jnp.zeros_like(red_ref)

        dx = dx_ref[...]
        mv = mix_ref[...]
        rstd = _rstd(mv)
        mhat = mv * rstd
        gpost = g_ref[1:2, :]
        gate = mod_ref[2:3, :]
        dm, s_post = _norm_gain_bwd(dx, mhat, rstd, gate * gpost)
        red_ref[0:1, :] += gpost * s_post
        red_ref[1:2, :] += gate * s_post
        dmb = dm.astype(BF16)
        dmix_ref[...] = dmb
        dap = _nt(dmb, w_ref[...])
        da_ref[...] = dap[:, 0:ATTN_W].astype(BF16)
        dp_ref[...] = dap[:, ATTN_W:].astype(BF16)

    return pl.pallas_call(
        body, name="mix_bwd", grid=(T // tm,),
        in_specs=[_rows(tm, D_MODEL), _rows(tm, D_MODEL), _full((8, D_MODEL)), _full((8, D_MODEL)),
                  _resident((D_MODEL, D_MODEL))],
        out_specs=[_rows(tm, D_MODEL), _rows(tm, ATTN_W), _rows(tm, POOL_W), _full((8, D_MODEL))],
        out_shape=[_sds((T, D_MODEL), BF16), _sds((T, ATTN_W), BF16), _sds((T, POOL_W), BF16),
                   _sds((8, D_MODEL), F32)],
        compiler_params=_cp("arbitrary"),
    )(dx1, mix, mod8, g8, w_out)


def _attn_bwd(q, kd, vd, lse, dattn, sink_b):
    T = q.shape[0]
    nb = T // BLK
    nq = ATTN_BWD_BLOCKS if nb % ATTN_BWD_BLOCKS == 0 else 2
    assert nb % nq == 0
    nstep = nb // nq

    def body(q_ref, do_ref, lse_ref, kp_ref, kc_ref, vp_ref, vc_ref, sk_ref,
             dq_ref, dkm_ref, dkt_ref, dvm_ref, dvt_ref, dsk_ref, carry_k, carry_v):
        i = pl.program_id(0)

        @pl.when(i == 0)
        def _():
            carry_k[...] = jnp.zeros_like(carry_k)
            carry_v[...] = jnp.zeros_like(carry_v)
            dsk_ref[...] = jnp.zeros_like(dsk_ref)

        @pl.when(i < nstep)
        def _():
            for j in range(N_HEADS // GROUP):
                lanes = slice(j * 128, (j + 1) * 128)
                parts_k, parts_v = [], []
                for sub in range(nq):
                    rows = slice(sub * BLK, (sub + 1) * BLK)
                    before = slice((sub - 1) * BLK, sub * BLK)
                    valid = _band_mask(nq * i + sub)
                    k_prev = kp_ref[:, lanes] if sub == 0 else kc_ref[before, lanes]
                    v_prev = vp_ref[:, lanes] if sub == 0 else vc_ref[before, lanes]
                    kcat = jnp.concatenate([k_prev, kc_ref[rows, lanes]], axis=0)
                    vcat = jnp.concatenate([v_prev, vc_ref[rows, lanes]], axis=0)
                    qs = _stack_heads(q_ref, j, rows)
                    dos = _stack_heads(do_ref, j, rows)
                    lse = _head_row(lse_ref, j, sub * N_HEADS)
                    p = jnp.exp(jnp.where(valid, _nt(kcat, qs), NEG_INF) - lse)
                    dp = _nt(vcat, dos)
                    delta = jnp.sum(p * dp, axis=0, keepdims=True)
                    ds = (p * (dp - delta)).astype(BF16)
                    sink_term = jnp.exp(_head_row(sk_ref, j) - lse) * delta
                    for r in range(GROUP):
                        h = GROUP * j + r
                        dsk_ref[h:h + 1, :] += -jnp.sum(sink_term[:, r * 128:(r + 1) * 128], axis=1, keepdims=True)
                    dq_ref[rows, 2 * j * 128:(2 * j + 2) * 128] = jnp.concatenate(
                        _unstack_heads(_tn(ds, kcat)), axis=1)
                    parts_k.append(_mm(ds, qs))
                    parts_v.append(_mm(p.astype(BF16), dos))
                dkt_ref[:, lanes] = carry_k[:, lanes] + parts_k[0][0:BLK]
                dvt_ref[:, lanes] = carry_v[:, lanes] + parts_v[0][0:BLK]
                for s in range(nq - 1):
                    dkm_ref[s * BLK:(s + 1) * BLK, lanes] = parts_k[s][BLK:] + parts_k[s + 1][0:BLK]
                    dvm_ref[s * BLK:(s + 1) * BLK, lanes] = parts_v[s][BLK:] + parts_v[s + 1][0:BLK]
                carry_k[:, lanes] = parts_k[nq - 1][BLK:]
                carry_v[:, lanes] = parts_v[nq - 1][BLK:]

        @pl.when(i == nstep)
        def _():
            dkt_ref[...] = carry_k[...]
            dvt_ref[...] = carry_v[...]

    cur = lambda i: (jnp.minimum(i, nstep - 1), 0)
    prev = lambda i: (jnp.minimum(jnp.maximum(nq * i - 1, 0), nb - 1), 0)
    tail = lambda i: (jnp.maximum(i - 1, 0), 0)
    main_shape = _sds((nstep * (nq - 1) * BLK, KVD_W), F32)
    tail_shape = _sds((nstep * BLK, KVD_W), F32)
    main_spec = pl.BlockSpec(((nq - 1) * BLK, KVD_W), cur)
    tail_spec = pl.BlockSpec((BLK, KVD_W), tail)
    return pl.pallas_call(
        body, name="attn_bwd", grid=(nstep + 1,),
        in_specs=[pl.BlockSpec((nq * BLK, ATTN_W), cur), pl.BlockSpec((nq * BLK, ATTN_W), cur),
                  pl.BlockSpec((nq * N_HEADS, 128), cur),
                  pl.BlockSpec((BLK, KVD_W), prev), pl.BlockSpec((nq * BLK, KVD_W), cur),
                  pl.BlockSpec((BLK, KVD_W), prev), pl.BlockSpec((nq * BLK, KVD_W), cur),
                  _full((8, 128))],
        out_specs=[pl.BlockSpec((nq * BLK, ATTN_W), cur), main_spec, tail_spec, main_spec, tail_spec, _full((8, 128))],
        out_shape=[_sds((T, ATTN_W), F32), main_shape, tail_shape, main_shape, tail_shape, _sds((8, 128), F32)],
        scratch_shapes=[pltpu.VMEM((BLK, KVD_W), F32), pltpu.VMEM((BLK, KVD_W), F32)],
        compiler_params=_cp("arbitrary"),
    )(q, dattn, lse, kd, kd, vd, vd, sink_b)


def _pool_bwd(dpool, pooled, pool_w, pool_scale):
    T = dpool.shape[0]
    tm = min(T, 1024)
    nbk = T // tm
    ext_rows = tm + HALO

    def body(dp_ref, pl_ref, w_ref, sc_ref, du_ref, dw_ref, dsc_ref, halo):
        i = pl.program_id(0)

        @pl.when(i == 0)
        def _():
            halo[...] = jnp.zeros_like(halo)
            dw_ref[...] = jnp.zeros_like(dw_ref)
            dsc_ref[...] = jnp.zeros_like(dsc_ref)

        blk = nbk - 1 - i
        tpos = (blk * tm + lax.broadcasted_iota(jnp.int32, (tm, 1), 0)).astype(F32)
        for g, w in enumerate(POOL_WINDOWS):
            lanes = slice(g * 128, (g + 1) * 128)
            dp = dp_ref[:, lanes].astype(F32)
            pb = pl_ref[:, lanes]
            wg = w_ref[g].astype(BF16)
            z = _mm(pb, wg)
            dsc_ref[0:1, lanes] += _colsum(dp * z)
            dz = (dp * sc_ref[:, lanes]).astype(BF16)
            dw_ref[g] += _tn(pb, dz)
            dpl = _nt(dz, wg)
            e = dpl / jnp.minimum(tpos + 1.0, float(w))
            s = jnp.concatenate([e, halo[:, lanes]], axis=0)
            halo[:, lanes] = e[0:HALO, :]
            sh = 1
            while sh < w:
                s = s + pltpu.roll(s, ext_rows - sh, 0)
                sh *= 2
            du_ref[:, lanes] = s[0:tm, :] - dpl

    rev = lambda i: (nbk - 1 - i, 0)
    return pl.pallas_call(
        body, name="pool_bwd", grid=(nbk,),
        in_specs=[pl.BlockSpec((tm, POOL_W), rev), pl.BlockSpec((tm, POOL_W), rev),
                  _full((4, 128, 128)), _full((1, POOL_W))],
        out_specs=[pl.BlockSpec((tm, POOL_W), rev), _full((4, 128, 128)), _full((8, POOL_W))],
        out_shape=[_sds((T, POOL_W), F32), _sds((4, 128, 128), F32), _sds((8, POOL_W), F32)],
        scratch_shapes=[pltpu.VMEM((HALO, POOL_W), F32)],
        compiler_params=_cp("arbitrary"),
    )(dpool, pooled, pool_w, pool_scale)


def _interleave_groups(main, tail):
    groups = tail.shape[0] // BLK
    m = main.shape[0] // groups
    parts = []
    for b in range(groups):
        parts += [main[b * m:(b + 1) * m], tail[b * BLK:(b + 1) * BLK]]
    return jnp.concatenate(parts, axis=0)


def _in_bwd(dq, dk_mt, dv_mt, du, rc, rs1, rs2, x, dx1, mod8, g8, w_in):
    T = x.shape[0]
    tm = min(T, 1024)
    nq = T // dk_mt[1].shape[0]
    t_tail = tm // nq
    t_main = tm - t_tail

    def body(dq_ref, dkm_ref, dkt_ref, dvm_ref, dvt_ref, du_ref, c_ref, s1_ref, s2_ref, x_ref, dx1_ref, mod_ref,
             g_ref, w_ref, dx_ref, dproj_ref, red_ref, dbin_ref):
        dk_all = _interleave_groups(dkm_ref[...], dkt_ref[...])
        dv_all = _interleave_groups(dvm_ref[...], dvt_ref[...])

        @pl.when(pl.program_id(0) == 0)
        def _():
            red_ref[...] = jnp.zeros_like(red_ref)
            dbin_ref[...] = jnp.zeros_like(dbin_ref)

        c = c_ref[...]
        s1 = s1_ref[...]
        s2 = s2_ref[...]
        dqp = _rot_bwd(dq_ref[...] * (HEAD ** -0.5), jnp.tile(c, (1, 4)), jnp.tile(s1, (1, 4)), jnp.tile(s2, (1, 4)))
        dkp = _rot_bwd(_fold_dup(dk_all), c, s1, s2)
        pieces = ((0, ATTN_W, dqp), (ATTN_W, ATTN_W + KV_W, dkp),
                  (ATTN_W + KV_W, ATTN_W + 2 * KV_W, _fold_dup(dv_all)), (ATTN_W + 2 * KV_W, IN_W, du_ref[...]))
        for lo, hi, val in pieces:
            dbin_ref[0:1, lo:hi] += _colsum(val)
            dproj_ref[:, lo:hi] = val.astype(BF16)
        dh = _mm(dproj_ref[...], w_ref[...])
        xf = x_ref[...]
        rstd = _rstd(xf)
        xhat = xf * rstd
        gpre = g_ref[0:1, :]
        scale1 = 1.0 + mod_ref[1:2, :]
        dxn, s_pre = _norm_gain_bwd(dh, xhat, rstd, scale1 * gpre)
        red_ref[0:1, :] += _colsum(dh)
        red_ref[1:2, :] += gpre * s_pre
        red_ref[2:3, :] += scale1 * s_pre
        dx_ref[...] = dx1_ref[...] + dxn

    return pl.pallas_call(
        body, name="in_bwd", grid=(T // tm,),
        in_specs=[_rows(tm, ATTN_W), *[_rows(t_main, KVD_W), _rows(t_tail, KVD_W)] * 2, _rows(tm, POOL_W),
                  _rows(tm, 128), _rows(tm, 128), _rows(tm, 128), _rows(tm, D_MODEL), _rows(tm, D_MODEL),
                  _full((8, D_MODEL)), _full((8, D_MODEL)), _resident((IN_W, D_MODEL))],
        out_specs=[_rows(tm, D_MODEL), _rows(tm, IN_W), _full((8, D_MODEL)), _full((8, IN_W))],
        out_shape=[_sds((T, D_MODEL), F32), _sds((T, IN_W), BF16), _sds((8, D_MODEL), F32), _sds((8, IN_W), F32)],
        compiler_params=_cp("arbitrary", vmem=FFN_VMEM_LIMIT),
    )(dq, *dk_mt, *dv_mt, du, rc, rs1, rs2, x, dx1, mod8, g8, w_in)


def _mod_fwd(c_all, ada_w, ada_b_sh):
    tn = 512

    def body(c_ref, w_ref, b_ref, o_ref):
        cv = c_ref[...]
        ca = (cv * jax.nn.sigmoid(cv)).astype(BF16)
        o_ref[...] = _mm(ca, w_ref[...].astype(BF16)) + b_ref[...]

    return pl.pallas_call(
        body, name="mod_fwd", grid=(2, ADA_SH // tn),
        in_specs=[_full((8, D_MODEL)), pl.BlockSpec((None, D_MODEL, tn), lambda l, j: (l, 0, j)),
                  pl.BlockSpec((None, 1, tn), lambda l, j: (l, 0, j))],
        out_specs=pl.BlockSpec((None, 8, tn), lambda l, j: (l, 0, j)),
        out_shape=_sds((2, 8, ADA_SH), F32),
        compiler_params=_cp("parallel", "parallel"),
    )(c_all, ada_w, ada_b_sh)


def _ada_wgrad(c_all_t, dmod_sh):
    tn = 512

    def body(c_ref, d_ref, o_ref):
        cv = c_ref[...]
        ca = cv * jax.nn.sigmoid(cv)
        o_ref[...] = jnp.dot(ca, d_ref[...], preferred_element_type=F32, precision=lax.Precision.HIGHEST)

    return pl.pallas_call(
        body, name="ada_wgrad", grid=(2, ADA_SH // tn),
        in_specs=[_full((D_MODEL, 8)), pl.BlockSpec((None, 8, tn), lambda l, j: (l, 0, j))],
        out_specs=pl.BlockSpec((None, D_MODEL, tn), lambda l, j: (l, 0, j)),
        out_shape=_sds((2, D_MODEL, ADA_SH), F32),
        compiler_params=_cp("parallel", "parallel"),
    )(c_all_t, dmod_sh)


def _sum_devices(g):
    R = g.shape[1]

    def body(g_ref, o_ref):
        acc = g_ref[0]
        for d in range(1, N_DEV):
            acc = acc + g_ref[d]
        o_ref[...] = acc

    return pl.pallas_call(
        body, name="sum_devices", grid=(1,),
        in_specs=[_full((N_DEV, R, 128))], out_specs=_full((R, 128)), out_shape=_sds((R, 128), F32),
        compiler_params=_cp("arbitrary"),
    )(g)


def _adamw(w, g, m, v, name):
    R, C = w.shape
    tr = R
    for cand in (256, 128, 64, 32, 16, 8):
        if R % cand == 0 and cand * C * 4 <= 2 * 1024 * 1024:
            tr = cand
            break

    def body(w_ref, g_ref, m_ref, v_ref, d_ref, nm_ref, nv_ref):
        gv = g_ref[...]
        mn = ADAM_B1 * m_ref[...] + (1.0 - ADAM_B1) * gv
        vn = ADAM_B2 * v_ref[...] + (1.0 - ADAM_B2) * (gv * gv)
        m_hat = mn / (1.0 - ADAM_B1 ** ADAM_STEP)
        v_hat = vn / (1.0 - ADAM_B2 ** ADAM_STEP)
        d_ref[...] = -ADAM_LR * (m_hat / (jnp.sqrt(v_hat) + ADAM_EPS) + ADAM_WD * w_ref[...])
        nm_ref[...] = mn
        nv_ref[...] = vn

    spec = pl.BlockSpec((tr, C), lambda i: (i, 0))
    out = _sds((R, C), F32)
    return pl.pallas_call(
        body, name=name, grid=(R // tr,),
        in_specs=[spec] * 4, out_specs=[spec] * 3, out_shape=[out] * 3,
        compiler_params=_cp("parallel"),
    )(w, g, m, v)


def _adamw_nd(w, g, m, v, name):
    shape = w.shape
    if w.ndim == 2 and shape[1] < 128:
        view = (1, shape[0] * shape[1])
    else:
        view = (-1, shape[-1])
    outs = _adamw(*[t.reshape(view) for t in (w, g, m, v)], name=name)
    return [o.reshape(shape) for o in outs]


def _coords():
    return lax.axis_index("x"), lax.axis_index("y"), lax.axis_index("c")


def _other_chips(x, y):
    return [(1 - x, y), (x, 1 - y), (1 - x, 1 - y)]


def _allgather8(blk, name):
    m_per, n = blk.shape

    def body(x_ref, out_ref, send_sems, recv_sems, local_sem):
        x, y, c = _coords()
        me, sibling = (x, y, c), (x, y, 1 - c)
        chips = _other_chips(x, y)

        def rows(px, py, pc):
            return out_ref.at[pl.ds((4 * px + 2 * py + pc) * m_per, m_per), :]

        def copy(k, block, to, src=None):
            return pltpu.make_async_remote_copy(
                src_ref=rows(*block) if src is None else src, dst_ref=rows(*block),
                send_sem=send_sems.at[k], recv_sem=recv_sems.at[k], device_id=to, device_id_type=MESH)

        mine = pltpu.make_async_copy(x_ref, rows(*me), local_sem)
        mine.start()
        first = [copy(0, me, sibling, src=x_ref)]
        first += [copy(1 + j, me, (*chip, c), src=x_ref) for j, chip in enumerate(chips)]
        for cp in first:
            cp.start()
        passed = [copy(4 + j, (*chip, c), sibling) for j, chip in enumerate(chips)]
        for j, chip in enumerate(chips):
            copy(1 + j, (*chip, c), me).wait_recv()
            passed[j].start()
        copy(0, sibling, me).wait_recv()
        for j, chip in enumerate(chips):
            copy(4 + j, (*chip, 1 - c), me).wait_recv()
        for cp in first + passed:
            cp.wait_send()
        mine.wait()

    return pl.pallas_call(
        body, name=name,
        out_shape=_sds((N_DEV * m_per, n), blk.dtype),
        in_specs=[pl.BlockSpec(memory_space=pltpu.VMEM)],
        out_specs=pl.BlockSpec(memory_space=pltpu.VMEM),
        scratch_shapes=[pltpu.SemaphoreType.DMA((7,)), pltpu.SemaphoreType.DMA((7,)), pltpu.SemaphoreType.DMA],
        compiler_params=pltpu.CompilerParams(vmem_limit_bytes=VMEM_LIMIT),
    )(blk)


def _row_tile(r, n):
    for cand in range(r, 15, -16):
        if r % cand == 0 and cand % 16 == 0 and cand * n * 4 <= 2 * 1024 * 1024:
            return cand
    return r


def _cast_slot(w, chip, name):
    r, n = w.shape
    tr = _row_tile(r, n)

    def body(chip_ref, w_ref, o_ref):
        o_ref[...] = w_ref[...].astype(BF16)

    grid_spec = pltpu.PrefetchScalarGridSpec(
        num_scalar_prefetch=1, grid=(r // tr,),
        in_specs=[pl.BlockSpec((tr, n), lambda i, ch: (i, 0))],
        out_specs=pl.BlockSpec((None, tr, n), lambda i, ch: (ch[0], i, 0)))
    return pl.pallas_call(
        body, name=name, grid_spec=grid_spec, out_shape=_sds((N_SHARD, r, n), BF16),
        compiler_params=_cp("arbitrary"),
    )(chip, w)


def _join_halves(tots, name):
    nt = len(tots)
    hom = [pl.BlockSpec(memory_space=pl.ANY)] * nt

    def body(*refs):
        outs = refs[nt:2 * nt]
        send_sems, recv_sems = refs[2 * nt:]
        x, y, c = _coords()
        sibling = (x, y, 1 - c)
        cps = []
        for t in range(nt):
            cp = pltpu.make_async_remote_copy(
                src_ref=outs[t].at[c], dst_ref=outs[t].at[c],
                send_sem=send_sems.at[t], recv_sem=recv_sems.at[t], device_id=sibling, device_id_type=MESH)
            cp.start()
            cps.append(cp)
        for t in range(nt):
            pltpu.make_async_remote_copy(
                src_ref=outs[t].at[c], dst_ref=outs[t].at[1 - c],
                send_sem=send_sems.at[t], recv_sem=recv_sems.at[t], device_id=sibling, device_id_type=MESH).wait_recv()
        for cp in cps:
            cp.wait_send()

    return pl.pallas_call(
        body, name=name,
        out_shape=[_sds(t.shape, t.dtype) for t in tots],
        in_specs=hom, out_specs=hom,
        input_output_aliases={t: t for t in range(nt)},
        scratch_shapes=[pltpu.SemaphoreType.DMA((nt,)), pltpu.SemaphoreType.DMA((nt,))],
    )(*tots)


def _pair_sum(g, recv, core, chip, name):
    _, _, r, n = g.shape
    tr = _row_tile(r, n)

    def body(core_ref, chip_ref, g_ref, r_ref, sb_ref, own_ref):
        tot = g_ref[...] + r_ref[...]
        sb_ref[...] = tot.astype(BF16)

        @pl.when(pl.program_id(1) == chip_ref[0])
        def _():
            own_ref[...] = tot

    grid_spec = pltpu.PrefetchScalarGridSpec(
        num_scalar_prefetch=2, grid=(r // tr, N_SHARD),
        in_specs=[pl.BlockSpec((None, None, tr, n), lambda i, s, co, ch: (s, co[0], i, 0)),
                  pl.BlockSpec((None, tr, n), lambda i, s, co, ch: (s, i, 0))],
        out_specs=[pl.BlockSpec((None, tr, n), lambda i, s, co, ch: (s, i, 0)),
                   pl.BlockSpec((tr, n), lambda i, s, co, ch: (i, 0))])
    return pl.pallas_call(
        body, name=name, grid_spec=grid_spec,
        out_shape=[_sds((N_SHARD, r, n), BF16), _sds((r, n), F32)],
        compiler_params=_cp("arbitrary", "arbitrary"),
    )(core, chip, g, recv)


def _chip_sum(own, recv, core, name):
    r, n = own.shape
    tr = _row_tile(r, n)

    def body(core_ref, o_ref, r_ref, t_ref):
        acc = o_ref[...]
        for j in range(3):
            acc = acc + r_ref[j].astype(F32)
        t_ref[...] = acc

    grid_spec = pltpu.PrefetchScalarGridSpec(
        num_scalar_prefetch=1, grid=(r // tr,),
        in_specs=[pl.BlockSpec((tr, n), lambda i, co: (i, 0)), pl.BlockSpec((3, tr, n), lambda i, co: (0, i, 0))],
        out_specs=pl.BlockSpec((None, tr, n), lambda i, co: (co[0], i, 0)))
    return pl.pallas_call(
        body, name=name, grid_spec=grid_spec, out_shape=_sds((2, r, n), F32),
        compiler_params=_cp("arbitrary"),
    )(core, own, recv)


_HBM = pl.BlockSpec(memory_space=pltpu.HBM)
_SEM = pl.BlockSpec(memory_space=pltpu.SEMAPHORE)
_EFFECT = pltpu.SideEffectType.DATAFLOW_SIDE_EFFECTING


def _ici_copies(srcs, dsts, send_sems, recv_sems, send_view, recv_view):
    x, y, c = _coords()
    out = []
    if send_view is None:
        for t in range(len(srcs)):
            r = srcs[t].shape[1] // 2
            out.append(pltpu.make_async_remote_copy(
                src_ref=srcs[t].at[:, pl.ds((1 - c) * r, r)], dst_ref=dsts[t],
                send_sem=send_sems.at[3 * t], recv_sem=recv_sems.at[3 * t],
                device_id=(x, y, 1 - c), device_id_type=MESH))
        return out
    for t in range(len(srcs)):
        for j, chip in enumerate(_other_chips(x, y)):
            out.append(pltpu.make_async_remote_copy(
                src_ref=send_view(srcs[t], chip, j, (x, y), c), dst_ref=recv_view(dsts[t], chip, j, (x, y), c),
                send_sem=send_sems.at[3 * t + j], recv_sem=recv_sems.at[3 * t + j],
                device_id=(*chip, c), device_id_type=MESH))
    return out


def _ici_start(srcs, dsts, after, send_view, recv_view, name):
    nt = len(srcs)
    inplace = dsts is None
    nbuf = nt if inplace else 2 * nt

    def body(*refs):
        send_sems, recv_sems = refs[nbuf + 1], refs[nbuf + 2]
        s_out = refs[nbuf + 3:nbuf + 3 + nt]
        d_out = s_out if inplace else refs[nbuf + 3 + nt:nbuf + 3 + 2 * nt]
        token = refs[-1]
        for cp in _ici_copies(s_out, d_out, send_sems, recv_sems, send_view, recv_view):
            cp.start()
        token[...] = jnp.zeros_like(token)

    bufs = list(srcs) + ([] if inplace else list(dsts))
    res = pl.pallas_call(
        body, name=name,
        out_shape=(pltpu.SemaphoreType.DMA((3 * nt,)), pltpu.SemaphoreType.DMA((3 * nt,)),
                   *[pltpu.HBM(b.shape, b.dtype) for b in bufs], _sds((8, 128), F32)),
        in_specs=[_HBM] * nbuf + [pl.BlockSpec(memory_space=pl.ANY)],
        out_specs=(_SEM, _SEM, *[_HBM] * nbuf, pl.BlockSpec(memory_space=pltpu.VMEM)),
        input_output_aliases={i: 2 + i for i in range(nbuf)},
        compiler_params=pltpu.CompilerParams(has_side_effects=_EFFECT),
    )(*[pltpu.with_memory_space_constraint(b, pltpu.HBM) for b in bufs], after)
    send_sems, recv_sems = res[0], res[1]
    s_thru = list(res[2:2 + nt])
    d_thru = s_thru if inplace else list(res[2 + nt:2 + 2 * nt])
    return send_sems, recv_sems, s_thru, d_thru, res[-1]


def _ici_wait(send_sems, recv_sems, srcs, dsts, after, send_view, recv_view, name):
    nt = len(srcs)
    inplace = dsts is None
    nbuf = nt if inplace else 2 * nt

    def body(*refs):
        send_ref, recv_ref = refs[nbuf], refs[nbuf + 1]
        s_out = refs[nbuf + 3:nbuf + 3 + nt]
        d_out = s_out if inplace else refs[nbuf + 3 + nt:nbuf + 3 + 2 * nt]
        for cp in _ici_copies(s_out, d_out, send_ref, recv_ref, send_view, recv_view):
            cp.wait_send()
            cp.wait_recv()

    bufs = list(srcs) + ([] if inplace else list(dsts))
    res = pl.pallas_call(
        body, name=name,
        out_shape=tuple(pltpu.HBM(b.shape, b.dtype) for b in bufs),
        in_specs=[_HBM] * nbuf + [_SEM, _SEM, pl.BlockSpec(memory_space=pl.ANY)],
        out_specs=tuple([_HBM] * nbuf),
        input_output_aliases={i: i for i in range(nbuf)},
        compiler_params=pltpu.CompilerParams(has_side_effects=_EFFECT),
    )(*bufs, send_sems, recv_sems, after)
    return list(res[:nt]) if inplace else (list(res[:nt]), list(res[nt:]))


def _w_half(buf, chip, c):
    r = buf.shape[1] // 2
    return buf.at[2 * chip[0] + chip[1], pl.ds(c * r, r)]


def _ag_send_view(buf, chip, j, me, c):
    return _w_half(buf, me, c)


def _ag_recv_view(buf, chip, j, me, c):
    return _w_half(buf, me, c)


def _rs_send_view(buf, chip, j, me, c):
    return buf.at[2 * chip[0] + chip[1]]


def _rs_recv_view(buf, chip, j, me, c):
    return buf.at[j]


def _ag_forward(bufs, name):
    nt = len(bufs)
    hom = [pl.BlockSpec(memory_space=pl.ANY)] * nt

    def body(*refs):
        outs = refs[nt:2 * nt]
        send_sems, recv_sems = refs[2 * nt:]
        x, y, c = _coords()
        sibling = (x, y, 1 - c)
        chips = _other_chips(x, y)

        def copy(t, j, hc):
            blk = _w_half(outs[t], chips[j], hc)
            return pltpu.make_async_remote_copy(
                src_ref=blk, dst_ref=blk, send_sem=send_sems.at[t, j], recv_sem=recv_sems.at[t, j],
                device_id=sibling, device_id_type=MESH)

        started = [copy(t, j, c) for t in range(nt) for j in range(3)]
        for cp in started:
            cp.start()
        for t in range(nt):
            for j in range(3):
                copy(t, j, 1 - c).wait_recv()
        for cp in started:
            cp.wait_send()

    return pl.pallas_call(
        body, name=name,
        out_shape=[_sds(b.shape, b.dtype) for b in bufs],
        in_specs=hom, out_specs=hom,
        input_output_aliases={t: t for t in range(nt)},
        scratch_shapes=[pltpu.SemaphoreType.DMA((nt, 3)), pltpu.SemaphoreType.DMA((nt, 3))],
    )(*bufs)


def _rs_swap_begin(grads, after, tag):
    land = [lax.empty((N_SHARD, g.shape[1] // 2, g.shape[2]), g.dtype) for g in grads]
    send_sems, recv_sems, s_thru, d_thru, token = _ici_start(grads, land, after, None, None, name="rs_swapgo_" + tag)
    return dict(sems=(send_sems, recv_sems), grads=s_thru, land=d_thru, tag=tag), token


def _rs_scatter_begin(swap, after):
    tag = swap["tag"]
    x, y, c = _coords()
    core = jnp.reshape(c, (1,)).astype(jnp.int32)
    chip = jnp.reshape(2 * x + y, (1,)).astype(jnp.int32)
    grads, recv = _ici_wait(*swap["sems"], swap["grads"], swap["land"], after, None, None, name="rs_swapend_" + tag)
    sums, owns = [], []
    for t, (g, rv) in enumerate(zip(grads, recv)):
        r = g.shape[1] // 2
        sb, own = _pair_sum(g.reshape(N_SHARD, 2, r, g.shape[2]), rv, core, chip, name=f"rs_pair_{tag}_{t}")
        sums.append(sb)
        owns.append(own)
    land = [lax.empty((3,) + s.shape[1:], s.dtype) for s in sums]
    send_sems, recv_sems, s_thru, d_thru, token = _ici_start(
        sums, land, after, _rs_send_view, _rs_recv_view, name="rs_start_" + tag)
    return dict(sems=(send_sems, recv_sems), sums=s_thru, land=d_thru, owns=owns, core=core, tag=tag), token


def _rs_end(state, after):
    tag = state["tag"]
    _, got = _ici_wait(*state["sems"], state["sums"], state["land"], after, _rs_send_view, _rs_recv_view,
                       name="rs_wait_" + tag)
    tots = [_chip_sum(o, gt, state["core"], name=f"rs_chip_{tag}_{t}")
            for t, (o, gt) in enumerate(zip(state["owns"], got))]
    full = _join_halves(tots, name="rs_join_" + tag)
    return [f.reshape(2 * f.shape[1], f.shape[2]) for f in full]


def _rope_lane_table():
    d = jnp.arange(128) % HEAD
    inv_freq = ROPE_THETA ** (-jnp.arange(0, ROT, 2, dtype=F32) / ROT)
    rot = d < ROT
    rows = [jnp.where(rot, inv_freq[d % (ROT // 2)], 0.0), rot.astype(F32),
            (d < ROT // 2).astype(F32), jnp.logical_and(d >= ROT // 2, rot).astype(F32)]
    return jnp.concatenate([jnp.stack(rows), jnp.zeros((4, 128), F32)], axis=0)


def _pad8(rows):
    return jnp.concatenate([rows, jnp.zeros((8 - rows.shape[0], rows.shape[1]), F32)], axis=0)


def kernel(x, c, positions, ada_w, ada_b, w_in, b_in, sinks, pool_w, pool_scale, w_out, w_gate, w_up, w_down, g_pre_mix, g_post_mix, g_pre_ffn, g_post_ffn, loss_target, m_ada_w, m_ada_b, m_w_in, m_b_in, m_sinks, m_pool_w, m_pool_scale, m_w_out, m_w_gate, m_w_up, m_w_down, m_g_pre_mix, m_g_post_mix, m_g_pre_ffn, m_g_post_ffn, v_ada_w, v_ada_b, v_w_in, v_b_in, v_sinks, v_pool_w, v_pool_scale, v_w_out, v_w_gate, v_w_up, v_w_down, v_g_pre_mix, v_g_post_mix, v_g_pre_ffn, v_g_post_ffn):
    T = x.shape[1]
    n_layers = ada_w.shape[0]
    ax, ay, ac = _coords()
    my_dev = 4 * ax + 2 * ay + ac
    my_chip = 2 * ax + ay
    x0 = x.reshape(T, D_MODEL)
    target = loss_target.reshape(T, D_MODEL)

    c_all = _allgather8(c.reshape(8, 128), name="ag_c").reshape(N_DEV, D_MODEL)
    ada_b_sh = lax.dynamic_slice_in_dim(ada_b, my_chip * ADA_SH, ADA_SH, axis=1).reshape(n_layers, 1, ADA_SH)
    mod_part = _mod_fwd(c_all, ada_w, ada_b_sh)
    mod_all = _allgather8(mod_part.reshape(n_layers * 8, ADA_SH), name="ag_mod")
    mod_all = mod_all.reshape(N_DEV, n_layers, 8, ADA_SH)[0::2]
    mod_mine = lax.dynamic_index_in_dim(mod_all, my_dev, axis=2, keepdims=False)
    mod = jnp.transpose(mod_mine, (1, 0, 2)).reshape(n_layers, 6, D_MODEL)

    chip1 = jnp.reshape(my_chip, (1,)).astype(jnp.int32)

    def tr(t):
        return jnp.transpose(t, (0, 2, 1))

    w_in_t, w_gate_t, w_up_t = tr(w_in), tr(w_gate), tr(w_up)

    def cast_layer(l):
        return [_cast_slot(w[l], chip1, name=f"cast_{nm}{l}")
                for nm, w in (("w_in", w_in_t), ("w_out", w_out), ("w_gate", w_gate_t), ("w_up", w_up_t),
                              ("w_down", w_down))]

    def as_operands(bufs):
        gin, gout, gg, gu, gd = bufs
        return (gin.reshape(IN_W, D_MODEL), gout.reshape(D_MODEL, D_MODEL), gg.reshape(D_FF, D_MODEL),
                gu.reshape(D_FF, D_MODEL), gd.reshape(D_FF, D_MODEL))

    bufs0 = cast_layer(0)
    in_send, in_recv, in_bufs, _, in_token = _ici_start(
        bufs0[:1], None, mod, _ag_send_view, _ag_recv_view, name="ag_start_0_in")
    pos_b = jnp.broadcast_to(positions.reshape(T, 1), (T, 128))
    rc, rs1, rs2 = _rope_tables(pos_b, _rope_lane_table() + in_token[0, 0])
    arrived = _ici_wait(in_send, in_recv, in_bufs, None, rc, _ag_send_view, _ag_recv_view, name="ag_wait_0_in")
    win0 = _ag_forward(arrived, name="ag_fwd_0_in")
    rest_send, rest_recv, rest_bufs, _, ag_token = _ici_start(
        bufs0[1:], None, win0[0], _ag_send_view, _ag_recv_view, name="ag_start_0")
    weights = [None] * n_layers

    saved = []
    xl = x0
    for l in range(n_layers):
        mod8 = _pad8(mod[l])
        if l + 1 < n_layers:
            ag_send, ag_recv, ag_bufs, _, ag_token = _ici_start(
                cast_layer(l + 1), None, ag_token, _ag_send_view, _ag_recv_view, name=f"ag_start_{l + 1}")
        if l == 0 or l + 1 < n_layers:
            mod8 = mod8 + ag_token[0, 0]
        g8 = _pad8(jnp.stack([g_pre_mix[l], g_post_mix[l], g_pre_ffn[l], g_post_ffn[l]]))
        sink_b = jnp.broadcast_to(sinks[l][:, None], (N_HEADS, 128))
        psc = pool_scale[l].reshape(1, POOL_W)
        win = win0[0].reshape(IN_W, D_MODEL) if l == 0 else weights[l][0]
        h, q, k, v, u = _fwd_in(xl, mod8, g8, win, b_in[l].reshape(1, IN_W), rc, rs1, rs2)
        attn, lse = _attn_fwd(q, k, v, sink_b)
        pool, pooled = _pool_fwd(u, pool_w[l], psc)
        if l == 0:
            arrived = _ici_wait(rest_send, rest_recv, rest_bufs, None, pool, _ag_send_view, _ag_recv_view,
                                name="ag_wait_0")
            weights[0] = as_operands(win0 + _ag_forward(arrived, name="ag_fwd_0"))
        win, wout, wg, wu, wd = weights[l]
        if l + 1 < n_layers:
            mix, x1, h2, act, ga, gb, f, x2 = _out_ffn_fwd(attn, pool, xl, wout, mod8, g8, wg, wu, wd)
        else:
            mix, x1, h2, act, ga, gb, f, x2, loss_tile = _out_ffn_fwd(attn, pool, xl, wout, mod8, g8, wg, wu, wd,
                                                                      target=target)
        saved.append(dict(x=xl, h=h, q=q, k=k, v=v, lse=lse, attn=attn, pool=pool, pooled=pooled, mix=mix,
                          x1=x1, h2=h2, act=act, ga=ga, gb=gb, f=f, mod8=mod8, g8=g8, sink_b=sink_b, psc=psc))
        xl = x2
        if l + 1 < n_layers:
            arrived = _ici_wait(ag_send, ag_recv, ag_bufs, None, x2, _ag_send_view, _ag_recv_view,
                                name=f"ag_wait_{l + 1}")
            weights[l + 1] = as_operands(_ag_forward(arrived, name=f"ag_fwd_{l + 1}"))

    dy = xl
    loss = lax.psum(loss_tile[0, 0], ("x", "y", "c"))

    small = [None] * n_layers
    dmod_rows = [None] * n_layers
    reduced = [dict() for _ in range(n_layers)]
    att_swap = None
    dx = dy
    for l in reversed(range(n_layers)):
        s = saved[l]
        win, wout, wg, wu, wd = weights[l]
        if att_swap is not None:
            s = dict(s, mod8=s["mod8"] + att_swap[1][0, 0])
        dx1, df, da, db, red_f = _ffn_bwd(dx, s["f"], s["ga"], s["gb"], s["x1"], s["mod8"], s["g8"], wg, wu, wd)
        token = None
        if att_swap is not None:
            att_scatter = _rs_scatter_begin(att_swap[0], dx1)
            token = att_scatter[1]
        ffn_shards = (N_SHARD, FF_SH, D_MODEL)
        g_wd = _wgrad(s["act"], df, name="wgrad_down", after=token).reshape(ffn_shards)
        g_wg = _wgrad(da, s["h2"], name="wgrad_gate").reshape(ffn_shards)
        g_wu = _wgrad(db, s["h2"], name="wgrad_up").reshape(ffn_shards)
        ffn_swap = _rs_swap_begin([g_wg, g_wu, g_wd], dx1, tag=f"{l}f")
        if att_swap is not None:
            got = _rs_end(att_scatter[0], ffn_swap[1])
            reduced[l + 1].update(w_in=got[0], w_out=got[1])
        s = dict(s, mod8=s["mod8"] + ffn_swap[1][0, 0])
        dmix, dattn, dpool, red_c = _mix_bwd(dx1, s["mix"], s["mod8"], s["g8"], wout)
        g_wout = jnp.concatenate([_wgrad(s["attn"], dmix, name="wgrad_out_a"),
                                  _wgrad(s["pool"], dmix, name="wgrad_out_p")], axis=0)
        ffn_scatter = _rs_scatter_begin(ffn_swap[0], dattn)
        dq, dk_e, dk_o, dv_e, dv_o, dsink = _attn_bwd(s["q"], s["k"], s["v"], s["lse"], dattn,
                                                      s["sink_b"] + ffn_scatter[1][0:1, :])
        du, g_poolw, dpsc = _pool_bwd(dpool, s["pooled"], pool_w[l], s["psc"])
        dx, dproj, red_d, dbin = _in_bwd(dq, (dk_e, dk_o), (dv_e, dv_o), du, rc, rs1, rs2, s["x"], dx1, s["mod8"],
                                         s["g8"], win)
        g_win = _wgrad(dproj, s["h"], name="wgrad_in")
        g_win_sh = g_win.reshape(N_SHARD, IN_SH, D_MODEL)
        got = _rs_end(ffn_scatter[0], dproj)
        reduced[l].update(w_gate=got[0], w_up=got[1], w_down=got[2])
        att_swap = _rs_swap_begin([g_win_sh, g_wout.reshape(N_SHARD, OUT_SH, D_MODEL)], dx, tag=f"{l}a")
        dmod_rows[l] = jnp.concatenate([red_d[0], red_d[1], red_c[0], red_f[2], red_f[3], red_f[0]])
        small[l] = jnp.concatenate([red_d[2], red_c[1], red_f[4], red_f[1], dbin[0], dpsc[0], dsink[:, 0],
                                    jnp.zeros((120,), F32), g_poolw.reshape(-1)])
    grad_x = dx.reshape(1, T, D_MODEL)

    per_layer = small[0].shape[0]
    rows_small = n_layers * per_layer // 128
    rows_mod = n_layers * 6 * D_MODEL // 128
    rows_pad = -(rows_small + rows_mod) % 8
    pack = jnp.concatenate(small + dmod_rows + [jnp.zeros((rows_pad * 128,), F32)]).reshape(-1, 128)
    pack = pack + att_swap[1][0, 0]
    gathered = _allgather8(pack, name="ag_small").reshape(N_DEV, pack.shape[0], 128)
    summed = _sum_devices(gathered)
    att_scatter = _rs_scatter_begin(att_swap[0], summed)
    small_sum = summed[:rows_small].reshape(n_layers, per_layer)
    o = 0
    small_g = {}
    for nm, width in (("g_pre_mix", D_MODEL), ("g_post_mix", D_MODEL), ("g_pre_ffn", D_MODEL),
                      ("g_post_ffn", D_MODEL), ("b_in", IN_W), ("pool_scale", POOL_W), ("sinks", 128),
                      ("pool_w", 4 * 128 * 128)):
        small_g[nm] = small_sum[:, o:o + width]
        o += width
    small_g["sinks"] = small_g["sinks"][:, :N_HEADS]
    small_g["pool_w"] = small_g["pool_w"].reshape(n_layers, 4, 128, 128)
    small_g["ada_b"] = summed[rows_small:rows_small + rows_mod].reshape(n_layers, 6 * D_MODEL)
    dmod_all = gathered[:, rows_small:rows_small + rows_mod].reshape(N_DEV, n_layers, N_SHARD, ADA_SH)
    dmod_sh = lax.dynamic_index_in_dim(dmod_all, my_chip, axis=2, keepdims=False)
    g_ada_w = _ada_wgrad(jnp.transpose(c_all), jnp.transpose(dmod_sh, (1, 0, 2)))

    grads = dict(ada_w=g_ada_w, ada_b=small_g["ada_b"], b_in=small_g["b_in"], sinks=small_g["sinks"],
                 pool_w=small_g["pool_w"], pool_scale=small_g["pool_scale"], g_pre_mix=small_g["g_pre_mix"],
                 g_post_mix=small_g["g_post_mix"], g_pre_ffn=small_g["g_pre_ffn"], g_post_ffn=small_g["g_post_ffn"])
    params = dict(ada_w=(ada_w, m_ada_w, v_ada_w), ada_b=(ada_b, m_ada_b, v_ada_b), w_in=(w_in, m_w_in, v_w_in),
                  b_in=(b_in, m_b_in, v_b_in), sinks=(sinks, m_sinks, v_sinks), pool_w=(pool_w, m_pool_w, v_pool_w),
                  pool_scale=(pool_scale, m_pool_scale, v_pool_scale), w_out=(w_out, m_w_out, v_w_out),
                  w_gate=(w_gate, m_w_gate, v_w_gate), w_up=(w_up, m_w_up, v_w_up),
                  w_down=(w_down, m_w_down, v_w_down), g_pre_mix=(g_pre_mix, m_g_pre_mix, v_g_pre_mix),
                  g_post_mix=(g_post_mix, m_g_post_mix, v_g_post_mix), g_pre_ffn=(g_pre_ffn, m_g_pre_ffn, v_g_pre_ffn),
                  g_post_ffn=(g_post_ffn, m_g_post_ffn, v_g_post_ffn))
    names = list(params)
    updates = {nm: _adamw_nd(*params[nm][:1], grads[nm], *params[nm][1:], name="adamw_" + nm) for nm in grads}

    got = _rs_end(att_scatter[0], updates["ada_w"][0])
    reduced[0].update(w_in=got[0], w_out=got[1])
    for nm in ("w_in", "w_out", "w_gate", "w_up", "w_down"):
        g = jnp.stack([reduced[l][nm] for l in range(n_layers)])
        if nm in ("w_in", "w_gate", "w_up"):
            upd = _adamw_nd(tr(params[nm][0]), g, tr(params[nm][1]), tr(params[nm][2]), name="adamw_" + nm)
            grads[nm], updates[nm] = tr(g), [tr(u) for u in upd]
        else:
            grads[nm], updates[nm] = g, _adamw_nd(params[nm][0], g, *params[nm][1:], name="adamw_" + nm)
    return (loss, grad_x, *[grads[nm] for nm in names], *[updates[nm][0] for nm in names],
            *[updates[nm][1] for nm in names], *[updates[nm][2] for nm in names])
```
